```python
import jax, jax.numpy as jnp
from jax import lax
import numpy as np

D_MODEL = 2048
BATCH = 8
SEQ = 2048
DEPTH = 4

CHUNK = 64
N_MIXERS = 4
D_FF = 4 * D_MODEL
NORM_EPS = 1e-6

RET_HEADS = 8
RET_DK = D_MODEL // RET_HEADS
RET_DV = 2 * D_MODEL // RET_HEADS
RET_IN = 2 * RET_HEADS * RET_DK + 2 * RET_HEADS * RET_DV
ROPE_BASE = 10000.0

GDN_HEADS = 16
GDN_DK = D_MODEL // GDN_HEADS
GDN_DV = D_MODEL // GDN_HEADS
GDN_QKV = GDN_HEADS * (2 * GDN_DK + GDN_DV)
GDN_IN = GDN_QKV + GDN_HEADS * GDN_DV + 2 * GDN_HEADS
CONV_WIDTH = 4

GLA_HEADS = 4
GLA_DK = D_MODEL // 2 // GLA_HEADS
GLA_DV = D_MODEL // GLA_HEADS
GLA_GATE_RANK = 16
GLA_TAU = 16.0
GLA_IN = 2 * GLA_HEADS * GLA_DK + 2 * GLA_HEADS * GLA_DV + GLA_GATE_RANK

LRU_WIDTH = D_MODEL
LRU_BLOCKS = 16
LRU_BLOCK = LRU_WIDTH // LRU_BLOCKS
LRU_C = 8.0

kernel_name = "interleaved_hybrid_chunk_causal_encoder"

F32 = jnp.float32


def _layers_of(m):
    return (DEPTH - m + N_MIXERS - 1) // N_MIXERS


def rmsnorm(x, g):
    xf = x.astype(F32)
    y = xf * lax.rsqrt(jnp.mean(xf * xf, axis=-1, keepdims=True) + NORM_EPS)
    return (y * g.astype(F32)).astype(x.dtype)


def head_norm(o, gain, center):
    of = o.astype(F32)
    if center:
        of = of - jnp.mean(of, axis=-1, keepdims=True)
    of = of * lax.rsqrt(jnp.mean(of * of, axis=-1, keepdims=True) + NORM_EPS)
    of = of * gain.astype(F32)
    return of.reshape(o.shape[0], o.shape[1], -1)


def l2norm(x):
    return x * lax.rsqrt(jnp.sum(x * x, axis=-1, keepdims=True) + NORM_EPS)


def causal_depthwise_conv(x, w):
    width, s = w.shape[0], x.shape[1]
    xp = jnp.pad(x, ((0, 0), (width - 1, 0), (0, 0)))
    out = xp[:, 0:s] * w[0]
    for tap in range(1, width):
        out = out + xp[:, tap:tap + s] * w[tap]
    return out


def to_chunks(x):
    b, s, h, d = x.shape
    return x.reshape(b, s // CHUNK, CHUNK, h, d).transpose(0, 3, 1, 2, 4)


def from_chunks(x):
    b, h, n, c, d = x.shape
    return x.transpose(0, 2, 3, 1, 4).reshape(b, n * c, h, d)


def scalar_chunks(x):
    b, s, h = x.shape
    return x.reshape(b, s // CHUNK, CHUNK, h).transpose(0, 3, 1, 2)


def rotary(x):
    s, d = x.shape[1], x.shape[-1]
    inv = ROPE_BASE ** (-jnp.arange(0, d, 2, dtype=F32) / d)
    ang = jnp.arange(s, dtype=F32)[:, None] * inv[None, :]
    cos, sin = jnp.cos(ang)[:, None, :], jnp.sin(ang)[:, None, :]
    x1, x2 = x[..., : d // 2], x[..., d // 2:]
    return jnp.concatenate([x1 * cos - x2 * sin, x1 * sin + x2 * cos], axis=-1)


def retention_mixer(h, w_in, gn_gain, w_out):
    b, s, _ = h.shape
    H, DK, DV = RET_HEADS, RET_DK, RET_DV
    q, k, v, g = jnp.split(h @ w_in, [H * DK, 2 * H * DK, 2 * H * DK + H * DV], axis=-1)
    q = rotary(q.astype(F32).reshape(b, s, H, DK))
    k = rotary(k.astype(F32).reshape(b, s, H, DK)) * DK ** -0.5
    v = v.astype(F32).reshape(b, s, H, DV)
    qc, kc, vc = to_chunks(q), to_chunks(k), to_chunks(v)
    log_gamma = jnp.log1p(-jnp.exp2(-5.0 - jnp.arange(H, dtype=F32)))
    pos = jnp.arange(CHUNK, dtype=F32)
    dist = jnp.abs(pos[:, None] - pos[None, :])
    decay_intra = jnp.exp(log_gamma[:, None, None] * dist)
    scores = jnp.einsum('bhncd,bhnmd->bhncm', qc, kc) * decay_intra[None, :, None]
    o_intra = jnp.einsum('bhncm,bhnmv->bhncv', scores, vc)
    q_decay = jnp.exp(log_gamma[:, None] * (pos + 1.0))[None, :, :, None]
    k_decay = jnp.exp(log_gamma[:, None] * (CHUNK - 1.0 - pos))[None, :, :, None]
    chunk_decay = jnp.exp(log_gamma * CHUNK)[None, :, None, None]

    def step(state, xs):
        q_n, k_n, v_n = xs
        o = jnp.einsum('bhcd,bhdv->bhcv', q_n * q_decay, state)
        state = state * chunk_decay + jnp.einsum('bhcd,bhcv->bhdv', k_n * k_decay, v_n)
        return state, o

    xs = (jnp.moveaxis(qc, 2, 0), jnp.moveaxis(kc, 2, 0), jnp.moveaxis(vc, 2, 0))
    _, o_inter = lax.scan(step, jnp.zeros((b, H, DK, DV), F32), xs)
    o = from_chunks(o_intra + jnp.moveaxis(o_inter, 0, 2))
    o = head_norm(o, gn_gain, center=True) * jax.nn.silu(g.astype(F32))
    return o.astype(h.dtype) @ w_out


def gated_deltanet_mixer(h, w_in, conv_w, a_log, dt_bias, norm_gain, w_out):
    b, s, _ = h.shape
    H, DK, DV = GDN_HEADS, GDN_DK, GDN_DV
    qkv, z, beta_logit, a_logit = jnp.split(
        h @ w_in, [GDN_QKV, GDN_QKV + H * DV, GDN_QKV + H * DV + H], axis=-1)
    qkv = jax.nn.silu(causal_depthwise_conv(qkv, conv_w)).astype(F32)
    q, k, v = jnp.split(qkv, [H * DK, 2 * H * DK], axis=-1)
    q = l2norm(q.reshape(b, s, H, DK)) * DK ** -0.5
    k = l2norm(k.reshape(b, s, H, DK))
    v = v.reshape(b, s, H, DV)
    beta = jax.nn.sigmoid(beta_logit.astype(F32))
    log_alpha = -jnp.exp(a_log.astype(F32)) * jax.nn.softplus(a_logit.astype(F32) + dt_bias.astype(F32))
    qc, kc, vc = to_chunks(q), to_chunks(k), to_chunks(v)
    beta_c = scalar_chunks(beta)
    cum = jnp.cumsum(scalar_chunks(log_alpha), axis=-1)
    idx = jnp.arange(CHUNK)
    strict = idx[:, None] > idx[None, :]
    rel = jnp.where(strict, jnp.exp(jnp.where(strict, cum[..., :, None] - cum[..., None, :], 0.0)), 0.0)
    a_mat = beta_c[..., :, None] * rel * jnp.einsum('bhncd,bhnmd->bhncm', kc, kc)
    l_mat = a_mat + jnp.eye(CHUNK, dtype=F32)
    rhs = jnp.concatenate([beta_c[..., None] * vc, (beta_c * jnp.exp(cum))[..., None] * kc], axis=-1)
    sol = lax.linalg.triangular_solve(l_mat, rhs, left_side=True, lower=True, unit_diagonal=True)
    u, w = sol[..., :DV], sol[..., DV:]
    k_end = kc * jnp.exp(cum[..., -1:] - cum)[..., None]
    trans = (jnp.exp(cum[..., -1])[..., None, None] * jnp.eye(DK, dtype=F32)
             - jnp.einsum('bhnck,bhncj->bhnkj', k_end, w))
    inject = jnp.einsum('bhnck,bhncv->bhnkv', k_end, u)

    def step(state, xs):
        t, g, q_n = xs
        state = jnp.einsum('bhkj,bhjv->bhkv', t, state) + g
        return state, jnp.einsum('bhck,bhkv->bhcv', q_n, state)

    xs = (jnp.moveaxis(trans, 2, 0), jnp.moveaxis(inject, 2, 0), jnp.moveaxis(qc, 2, 0))
    _, o = lax.scan(step, jnp.zeros((b, H, DK, DV), F32), xs)
    o = from_chunks(jnp.moveaxis(o, 0, 2))
    o = head_norm(o, norm_gain, center=False) * jax.nn.silu(z.astype(F32))
    return o.astype(h.dtype) @ w_out


def gla_mixer(h, w_in, w_gate_up, gate_bias, norm_gain, w_out):
    b, s, _ = h.shape
    H, DK, DV = GLA_HEADS, GLA_DK, GLA_DV
    QK, V = H * DK, H * DV
    q, k, v, r, gate_low = jnp.split(h @ w_in, [QK, 2 * QK, 2 * QK + V, 2 * QK + 2 * V], axis=-1)
    gate_logit = (gate_low @ w_gate_up + gate_bias).astype(F32)
    log_alpha = jax.nn.log_sigmoid(gate_logit) / GLA_TAU
    qc = to_chunks(q.astype(F32).reshape(b, s, H, DK)) * DK ** -0.5
    kc = to_chunks(k.astype(F32).reshape(b, s, H, DK))
    vc = to_chunks(v.astype(F32).reshape(b, s, H, DV))
    cum = jnp.cumsum(to_chunks(log_alpha.reshape(b, s, H, DK)), axis=-2)
    ref = cum[..., CHUNK // 2 - 1:CHUNK // 2, :]
    fwd, bwd = jnp.exp(cum - ref), jnp.exp(ref - cum)
    s_lo = jnp.einsum('bhnck,bhnmk->bhncm', qc * fwd, kc * bwd)
    s_up = jnp.einsum('bhnck,bhnmk->bhncm', qc * bwd, kc * fwd)
    idx = jnp.arange(CHUNK)
    scores = jnp.where(idx[:, None] >= idx[None, :], s_lo, s_up)
    o_intra = jnp.einsum('bhncm,bhnmv->bhncv', scores, vc)
    q_in = qc * jnp.exp(cum)
    k_end = kc * jnp.exp(cum[..., -1:, :] - cum)
    chunk_dec = jnp.exp(cum[..., -1, :])

    def step(state, xs):
        q_n, k_n, v_n, dec = xs
        o = jnp.einsum('bhck,bhkv->bhcv', q_n, state)
        state = state * dec[..., None] + jnp.einsum('bhck,bhcv->bhkv', k_n, v_n)
        return state, o

    xs = tuple(jnp.moveaxis(t, 2, 0) for t in (q_in, k_end, vc, chunk_dec))
    _, o_inter = lax.scan(step, jnp.zeros((b, H, DK, DV), F32), xs)
    o = from_chunks(o_intra + jnp.moveaxis(o_inter, 0, 2))
    o = head_norm(o, norm_gain, center=False) * jax.nn.silu(r.astype(F32))
    return o.astype(h.dtype) @ w_out


def rglru_mixer(h, w_in, conv_w, conv_b, w_rgate, b_rgate, w_igate, b_igate, lam, w_out):
    b, s, _ = h.shape
    xb, yb = jnp.split(h @ w_in, [LRU_WIDTH], axis=-1)
    yb = jax.nn.gelu(yb.astype(F32))
    xb = (causal_depthwise_conv(xb, conv_w) + conv_b).astype(F32)
    xblk = xb.reshape(b, s, LRU_BLOCKS, LRU_BLOCK)
    r = jax.nn.sigmoid(jnp.einsum('bsnd,nde->bsne', xblk, w_rgate.astype(F32)) + b_rgate.astype(F32))
    i = jax.nn.sigmoid(jnp.einsum('bsnd,nde->bsne', xblk, w_igate.astype(F32)) + b_igate.astype(F32))
    r, i = r.reshape(b, s, LRU_WIDTH), i.reshape(b, s, LRU_WIDTH)
    log_a = -LRU_C * jax.nn.softplus(-lam.astype(F32)) * r
    a = jnp.exp(log_a)
    u = jnp.sqrt(-jnp.expm1(2.0 * log_a)) * (i * xb)

    def combine(left, right):
        a_l, b_l = left
        a_r, b_r = right
        return a_l * a_r, a_r * b_l + b_r

    _, hs = lax.associative_scan(combine, (a, u), axis=1)
    return (hs * yb).astype(h.dtype) @ w_out


def squared_relu_mlp(h, w_up, w_down):
    return jnp.square(jax.nn.relu(h @ w_up)) @ w_down


def _fwd_setup_inputs(seed: int = 0) -> dict:
    key = jax.random.key(seed)
    ks = iter(jax.random.split(key, 40))

    def nrm(shape, scale):
        return scale * jax.random.normal(next(ks), shape, F32)

    def uni(shape, lo, hi):
        return jax.random.uniform(next(ks), shape, F32, lo, hi)

    n_ret, n_gdn, n_gla, n_lru = (_layers_of(m) for m in range(N_MIXERS))
    d_in = D_MODEL ** -0.5
    dt = jnp.exp(uni((n_gdn, GDN_HEADS), float(np.log(1e-3)), float(np.log(1e-1))))
    a0 = uni((n_lru, LRU_WIDTH), 0.9, 0.999)
    s0 = a0 ** (1.0 / LRU_C)
    return {
        "x": nrm((BATCH, SEQ, D_MODEL), 1.0),
        "norm1": 1.0 + nrm((DEPTH, D_MODEL), 0.02),
        "norm2": 1.0 + nrm((DEPTH, D_MODEL), 0.02),
        "final_norm": 1.0 + nrm((D_MODEL,), 0.02),
        "ret_w_in": nrm((n_ret, D_MODEL, RET_IN), d_in),
        "ret_gn_gain": 1.0 + nrm((n_ret, RET_HEADS, RET_DV), 0.02),
        "ret_w_out": nrm((n_ret, RET_HEADS * RET_DV, D_MODEL), (RET_HEADS * RET_DV) ** -0.5),
        "gdn_w_in": nrm((n_gdn, D_MODEL, GDN_IN), d_in),
        "gdn_conv_w": nrm((n_gdn, CONV_WIDTH, GDN_QKV), CONV_WIDTH ** -0.5),
        "gdn_a_log": jnp.log(uni((n_gdn, GDN_HEADS), 1.0, 16.0)),
        "gdn_dt_bias": dt + jnp.log(-jnp.expm1(-dt)),
        "gdn_norm_gain": 1.0 + nrm((n_gdn, GDN_DV), 0.02),
        "gdn_w_out": nrm((n_gdn, GDN_HEADS * GDN_DV, D_MODEL), (GDN_HEADS * GDN_DV) ** -0.5),
        "gla_w_in": nrm((n_gla, D_MODEL, GLA_IN), d_in),
        "gla_w_gate_up": nrm((n_gla, GLA_GATE_RANK, GLA_HEADS * GLA_DK), GLA_GATE_RANK ** -0.5),
        "gla_gate_bias": nrm((n_gla, GLA_HEADS * GLA_DK), 0.1),
        "gla_norm_gain": 1.0 + nrm((n_gla, GLA_HEADS, GLA_DV), 0.02),
        "gla_w_out": nrm((n_gla, GLA_HEADS * GLA_DV, D_MODEL), (GLA_HEADS * GLA_DV) ** -0.5),
        "lru_w_in": nrm((n_lru, D_MODEL, 2 * LRU_WIDTH), d_in),
        "lru_conv_w": nrm((n_lru, CONV_WIDTH, LRU_WIDTH), CONV_WIDTH ** -0.5),
        "lru_conv_b": nrm((n_lru, LRU_WIDTH), 0.02),
        "lru_w_rgate": nrm((n_lru, LRU_BLOCKS, LRU_BLOCK, LRU_BLOCK), LRU_BLOCK ** -0.5),
        "lru_b_rgate": nrm((n_lru, LRU_BLOCKS, LRU_BLOCK), 0.02),
        "lru_w_igate": nrm((n_lru, LRU_BLOCKS, LRU_BLOCK, LRU_BLOCK), LRU_BLOCK ** -0.5),
        "lru_b_igate": nrm((n_lru, LRU_BLOCKS, LRU_BLOCK), 0.02),
        "lru_lambda": jnp.log(s0) - jnp.log1p(-s0),
        "lru_w_out": nrm((n_lru, LRU_WIDTH, D_MODEL), LRU_WIDTH ** -0.5),
        "mlp_w_up": nrm((DEPTH, D_MODEL, D_FF), d_in),
        "mlp_w_down": nrm((DEPTH, D_FF, D_MODEL), D_FF ** -0.5),
    }


def _fwd_reference(x, norm1, norm2, final_norm,
              ret_w_in, ret_gn_gain, ret_w_out,
              gdn_w_in, gdn_conv_w, gdn_a_log, gdn_dt_bias, gdn_norm_gain, gdn_w_out,
              gla_w_in, gla_w_gate_up, gla_gate_bias, gla_norm_gain, gla_w_out,
              lru_w_in, lru_conv_w, lru_conv_b, lru_w_rgate, lru_b_rgate, lru_w_igate,
              lru_b_igate, lru_lambda, lru_w_out,
              mlp_w_up, mlp_w_down):
    for layer in range(DEPTH):
        m, j = layer % N_MIXERS, layer // N_MIXERS
        hn = rmsnorm(x, norm1[layer])
        if m == 0:
            y = retention_mixer(hn, ret_w_in[j], ret_gn_gain[j], ret_w_out[j])
        elif m == 1:
            y = gated_deltanet_mixer(hn, gdn_w_in[j], gdn_conv_w[j], gdn_a_log[j], gdn_dt_bias[j],
                                     gdn_norm_gain[j], gdn_w_out[j])
        elif m == 2:
            y = gla_mixer(hn, gla_w_in[j], gla_w_gate_up[j], gla_gate_bias[j], gla_norm_gain[j], gla_w_out[j])
        else:
            y = rglru_mixer(hn, lru_w_in[j], lru_conv_w[j], lru_conv_b[j], lru_w_rgate[j], lru_b_rgate[j],
                            lru_w_igate[j], lru_b_igate[j], lru_lambda[j], lru_w_out[j])
        x = x + y
        x = x + squared_relu_mlp(rmsnorm(x, norm2[layer]), mlp_w_up[layer], mlp_w_down[layer])
    return rmsnorm(x, final_norm)


import jax as _jax
import jax.numpy as _jnp

TWIN_FORMAT = 'train_step'
FWD_PARAMS = ['x', 'norm1', 'norm2', 'final_norm', 'ret_w_in', 'ret_gn_gain', 'ret_w_out', 'gdn_w_in', 'gdn_conv_w', 'gdn_a_log', 'gdn_dt_bias', 'gdn_norm_gain', 'gdn_w_out', 'gla_w_in', 'gla_w_gate_up', 'gla_gate_bias', 'gla_norm_gain', 'gla_w_out', 'lru_w_in', 'lru_conv_w', 'lru_conv_b', 'lru_w_rgate', 'lru_b_rgate', 'lru_w_igate', 'lru_b_igate', 'lru_lambda', 'lru_w_out', 'mlp_w_up', 'mlp_w_down']
TWIN_WEIGHTS = ['norm1', 'norm2', 'final_norm', 'ret_w_in', 'ret_gn_gain', 'ret_w_out', 'gdn_w_in', 'gdn_conv_w', 'gdn_a_log', 'gdn_dt_bias', 'gdn_norm_gain', 'gdn_w_out', 'gla_w_in', 'gla_w_gate_up', 'gla_gate_bias', 'gla_norm_gain', 'gla_w_out', 'lru_w_in', 'lru_conv_w', 'lru_conv_b', 'lru_w_rgate', 'lru_b_rgate', 'lru_w_igate', 'lru_b_igate', 'lru_lambda', 'lru_w_out', 'mlp_w_up', 'mlp_w_down']
TWIN_DIFF_INPUT = 'x'
TWIN_INPUTS = ['x', 'norm1', 'norm2', 'final_norm', 'ret_w_in', 'ret_gn_gain', 'ret_w_out', 'gdn_w_in', 'gdn_conv_w', 'gdn_a_log', 'gdn_dt_bias', 'gdn_norm_gain', 'gdn_w_out', 'gla_w_in', 'gla_w_gate_up', 'gla_gate_bias', 'gla_norm_gain', 'gla_w_out', 'lru_w_in', 'lru_conv_w', 'lru_conv_b', 'lru_w_rgate', 'lru_b_rgate', 'lru_w_igate', 'lru_b_igate', 'lru_lambda', 'lru_w_out', 'mlp_w_up', 'mlp_w_down', 'loss_target', 'm_norm1', 'm_norm2', 'm_final_norm', 'm_ret_w_in', 'm_ret_gn_gain', 'm_ret_w_out', 'm_gdn_w_in', 'm_gdn_conv_w', 'm_gdn_a_log', 'm_gdn_dt_bias', 'm_gdn_norm_gain', 'm_gdn_w_out', 'm_gla_w_in', 'm_gla_w_gate_up', 'm_gla_gate_bias', 'm_gla_norm_gain', 'm_gla_w_out', 'm_lru_w_in', 'm_lru_conv_w', 'm_lru_conv_b', 'm_lru_w_rgate', 'm_lru_b_rgate', 'm_lru_w_igate', 'm_lru_b_igate', 'm_lru_lambda', 'm_lru_w_out', 'm_mlp_w_up', 'm_mlp_w_down', 'v_norm1', 'v_norm2', 'v_final_norm', 'v_ret_w_in', 'v_ret_gn_gain', 'v_ret_w_out', 'v_gdn_w_in', 'v_gdn_conv_w', 'v_gdn_a_log', 'v_gdn_dt_bias', 'v_gdn_norm_gain', 'v_gdn_w_out', 'v_gla_w_in', 'v_gla_w_gate_up', 'v_gla_gate_bias', 'v_gla_norm_gain', 'v_gla_w_out', 'v_lru_w_in', 'v_lru_conv_w', 'v_lru_conv_b', 'v_lru_w_rgate', 'v_lru_b_rgate', 'v_lru_w_igate', 'v_lru_b_igate', 'v_lru_lambda', 'v_lru_w_out', 'v_mlp_w_up', 'v_mlp_w_down']
TWIN_OUTPUTS = ['loss', 'grad_x', 'grad_norm1', 'grad_norm2', 'grad_final_norm', 'grad_ret_w_in', 'grad_ret_gn_gain', 'grad_ret_w_out', 'grad_gdn_w_in', 'grad_gdn_conv_w', 'grad_gdn_a_log', 'grad_gdn_dt_bias', 'grad_gdn_norm_gain', 'grad_gdn_w_out', 'grad_gla_w_in', 'grad_gla_w_gate_up', 'grad_gla_gate_bias', 'grad_gla_norm_gain', 'grad_gla_w_out', 'grad_lru_w_in', 'grad_lru_conv_w', 'grad_lru_conv_b', 'grad_lru_w_rgate', 'grad_lru_b_rgate', 'grad_lru_w_igate', 'grad_lru_b_igate', 'grad_lru_lambda', 'grad_lru_w_out', 'grad_mlp_w_up', 'grad_mlp_w_down', 'delta_norm1', 'delta_norm2', 'delta_final_norm', 'delta_ret_w_in', 'delta_ret_gn_gain', 'delta_ret_w_out', 'delta_gdn_w_in', 'delta_gdn_conv_w', 'delta_gdn_a_log', 'delta_gdn_dt_bias', 'delta_gdn_norm_gain', 'delta_gdn_w_out', 'delta_gla_w_in', 'delta_gla_w_gate_up', 'delta_gla_gate_bias', 'delta_gla_norm_gain', 'delta_gla_w_out', 'delta_lru_w_in', 'delta_lru_conv_w', 'delta_lru_conv_b', 'delta_lru_w_rgate', 'delta_lru_b_rgate', 'delta_lru_w_igate', 'delta_lru_b_igate', 'delta_lru_lambda', 'delta_lru_w_out', 'delta_mlp_w_up', 'delta_mlp_w_down', 'new_m_norm1', 'new_m_norm2', 'new_m_final_norm', 'new_m_ret_w_in', 'new_m_ret_gn_gain', 'new_m_ret_w_out', 'new_m_gdn_w_in', 'new_m_gdn_conv_w', 'new_m_gdn_a_log', 'new_m_gdn_dt_bias', 'new_m_gdn_norm_gain', 'new_m_gdn_w_out', 'new_m_gla_w_in', 'new_m_gla_w_gate_up', 'new_m_gla_gate_bias', 'new_m_gla_norm_gain', 'new_m_gla_w_out', 'new_m_lru_w_in', 'new_m_lru_conv_w', 'new_m_lru_conv_b', 'new_m_lru_w_rgate', 'new_m_lru_b_rgate', 'new_m_lru_w_igate', 'new_m_lru_b_igate', 'new_m_lru_lambda', 'new_m_lru_w_out', 'new_m_mlp_w_up', 'new_m_mlp_w_down', 'new_v_norm1', 'new_v_norm2', 'new_v_final_norm', 'new_v_ret_w_in', 'new_v_ret_gn_gain', 'new_v_ret_w_out', 'new_v_gdn_w_in', 'new_v_gdn_conv_w', 'new_v_gdn_a_log', 'new_v_gdn_dt_bias', 'new_v_gdn_norm_gain', 'new_v_gdn_w_out', 'new_v_gla_w_in', 'new_v_gla_w_gate_up', 'new_v_gla_gate_bias', 'new_v_gla_norm_gain', 'new_v_gla_w_out', 'new_v_lru_w_in', 'new_v_lru_conv_w', 'new_v_lru_conv_b', 'new_v_lru_w_rgate', 'new_v_lru_b_rgate', 'new_v_lru_w_igate', 'new_v_lru_b_igate', 'new_v_lru_lambda', 'new_v_lru_w_out', 'new_v_mlp_w_up', 'new_v_mlp_w_down']
TWIN_LEAF_KINDS = {'loss': 'loss', 'grad_x': 'grad_x', 'grad_norm1': 'grad_w', 'grad_norm2': 'grad_w', 'grad_final_norm': 'grad_w', 'grad_ret_w_in': 'grad_w', 'grad_ret_gn_gain': 'grad_w', 'grad_ret_w_out': 'grad_w', 'grad_gdn_w_in': 'grad_w', 'grad_gdn_conv_w': 'grad_w', 'grad_gdn_a_log': 'grad_w', 'grad_gdn_dt_bias': 'grad_w', 'grad_gdn_norm_gain': 'grad_w', 'grad_gdn_w_out': 'grad_w', 'grad_gla_w_in': 'grad_w', 'grad_gla_w_gate_up': 'grad_w', 'grad_gla_gate_bias': 'grad_w', 'grad_gla_norm_gain': 'grad_w', 'grad_gla_w_out': 'grad_w', 'grad_lru_w_in': 'grad_w', 'grad_lru_conv_w': 'grad_w', 'grad_lru_conv_b': 'grad_w', 'grad_lru_w_rgate': 'grad_w', 'grad_lru_b_rgate': 'grad_w', 'grad_lru_w_igate': 'grad_w', 'grad_lru_b_igate': 'grad_w', 'grad_lru_lambda': 'grad_w', 'grad_lru_w_out': 'grad_w', 'grad_mlp_w_up': 'grad_w', 'grad_mlp_w_down': 'grad_w', 'delta_norm1': 'delta_w', 'delta_norm2': 'delta_w', 'delta_final_norm': 'delta_w', 'delta_ret_w_in': 'delta_w', 'delta_ret_gn_gain': 'delta_w', 'delta_ret_w_out': 'delta_w', 'delta_gdn_w_in': 'delta_w', 'delta_gdn_conv_w': 'delta_w', 'delta_gdn_a_log': 'delta_w', 'delta_gdn_dt_bias': 'delta_w', 'delta_gdn_norm_gain': 'delta_w', 'delta_gdn_w_out': 'delta_w', 'delta_gla_w_in': 'delta_w', 'delta_gla_w_gate_up': 'delta_w', 'delta_gla_gate_bias': 'delta_w', 'delta_gla_norm_gain': 'delta_w', 'delta_gla_w_out': 'delta_w', 'delta_lru_w_in': 'delta_w', 'delta_lru_conv_w': 'delta_w', 'delta_lru_conv_b': 'delta_w', 'delta_lru_w_rgate': 'delta_w', 'delta_lru_b_rgate': 'delta_w', 'delta_lru_w_igate': 'delta_w', 'delta_lru_b_igate': 'delta_w', 'delta_lru_lambda': 'delta_w', 'delta_lru_w_out': 'delta_w', 'delta_mlp_w_up': 'delta_w', 'delta_mlp_w_down': 'delta_w', 'new_m_norm1': 'new_m', 'new_m_norm2': 'new_m', 'new_m_final_norm': 'new_m', 'new_m_ret_w_in': 'new_m', 'new_m_ret_gn_gain': 'new_m', 'new_m_ret_w_out': 'new_m', 'new_m_gdn_w_in': 'new_m', 'new_m_gdn_conv_w': 'new_m', 'new_m_gdn_a_log': 'new_m', 'new_m_gdn_dt_bias': 'new_m', 'new_m_gdn_norm_gain': 'new_m', 'new_m_gdn_w_out': 'new_m', 'new_m_gla_w_in': 'new_m', 'new_m_gla_w_gate_up': 'new_m', 'new_m_gla_gate_bias': 'new_m', 'new_m_gla_norm_gain': 'new_m', 'new_m_gla_w_out': 'new_m', 'new_m_lru_w_in': 'new_m', 'new_m_lru_conv_w': 'new_m', 'new_m_lru_conv_b': 'new_m', 'new_m_lru_w_rgate': 'new_m', 'new_m_lru_b_rgate': 'new_m', 'new_m_lru_w_igate': 'new_m', 'new_m_lru_b_igate': 'new_m', 'new_m_lru_lambda': 'new_m', 'new_m_lru_w_out': 'new_m', 'new_m_mlp_w_up': 'new_m', 'new_m_mlp_w_down': 'new_m', 'new_v_norm1': 'new_v', 'new_v_norm2': 'new_v', 'new_v_final_norm': 'new_v', 'new_v_ret_w_in': 'new_v', 'new_v_ret_gn_gain': 'new_v', 'new_v_ret_w_out': 'new_v', 'new_v_gdn_w_in': 'new_v', 'new_v_gdn_conv_w': 'new_v', 'new_v_gdn_a_log': 'new_v', 'new_v_gdn_dt_bias': 'new_v', 'new_v_gdn_norm_gain': 'new_v', 'new_v_gdn_w_out': 'new_v', 'new_v_gla_w_in': 'new_v', 'new_v_gla_w_gate_up': 'new_v', 'new_v_gla_gate_bias': 'new_v', 'new_v_gla_norm_gain': 'new_v', 'new_v_gla_w_out': 'new_v', 'new_v_lru_w_in': 'new_v', 'new_v_lru_conv_w': 'new_v', 'new_v_lru_conv_b': 'new_v', 'new_v_lru_w_rgate': 'new_v', 'new_v_lru_b_rgate': 'new_v', 'new_v_lru_w_igate': 'new_v', 'new_v_lru_b_igate': 'new_v', 'new_v_lru_lambda': 'new_v', 'new_v_lru_w_out': 'new_v', 'new_v_mlp_w_up': 'new_v', 'new_v_mlp_w_down': 'new_v'}


def _forward(args):
    return _fwd_reference(*[args[k] for k in FWD_PARAMS])


def _output_shape():
    out = _jax.eval_shape(lambda: _forward(_fwd_setup_inputs(0)))
    return out.shape, out.dtype

N_MICROBATCH = 1
ADAM_LR = 0.001
ADAM_B1 = 0.9
ADAM_B2 = 0.999
ADAM_EPS = 1e-08
ADAM_WD = 0.01
ADAM_STEP = 10
PER_EXAMPLE_BATCH_AXIS = {'x': 0, 'loss_target': 0}
SHARED_INPUTS = []
_WEIGHT_DTYPES = {'norm1': _jnp.float32, 'norm2': _jnp.float32, 'final_norm': _jnp.float32, 'ret_w_in': _jnp.float32, 'ret_gn_gain': _jnp.float32, 'ret_w_out': _jnp.float32, 'gdn_w_in': _jnp.float32, 'gdn_conv_w': _jnp.float32, 'gdn_a_log': _jnp.float32, 'gdn_dt_bias': _jnp.float32, 'gdn_norm_gain': _jnp.float32, 'gdn_w_out': _jnp.float32, 'gla_w_in': _jnp.float32, 'gla_w_gate_up': _jnp.float32, 'gla_gate_bias': _jnp.float32, 'gla_norm_gain': _jnp.float32, 'gla_w_out': _jnp.float32, 'lru_w_in': _jnp.float32, 'lru_conv_w': _jnp.float32, 'lru_conv_b': _jnp.float32, 'lru_w_rgate': _jnp.float32, 'lru_b_rgate': _jnp.float32, 'lru_w_igate': _jnp.float32, 'lru_b_igate': _jnp.float32, 'lru_lambda': _jnp.float32, 'lru_w_out': _jnp.float32, 'mlp_w_up': _jnp.float32, 'mlp_w_down': _jnp.float32}
MOMENT_SCALE = {'norm1': 6.300811e-02, 'norm2': 5.038751e-02, 'final_norm': 8.226446e+00, 'ret_w_in': 3.926705e-02, 'ret_gn_gain': 3.349567e-02, 'ret_w_out': 4.753202e-02, 'gdn_w_in': 2.248957e-02, 'gdn_conv_w': 2.066790e-02, 'gdn_a_log': 1.269703e-01, 'gdn_dt_bias': 1.235678e-01, 'gdn_norm_gain': 1.089820e-01, 'gdn_w_out': 2.814332e-02, 'gla_w_in': 2.230658e-02, 'gla_w_gate_up': 6.297609e-03, 'gla_gate_bias': 1.398590e-02, 'gla_norm_gain': 1.821263e-02, 'gla_w_out': 1.792969e-02, 'lru_w_in': 4.131400e-02, 'lru_conv_w': 4.268858e-02, 'lru_conv_b': 1.185131e-01, 'lru_w_rgate': 3.709571e-03, 'lru_b_rgate': 7.508073e-03, 'lru_w_igate': 8.168484e-03, 'lru_b_igate': 1.575915e-02, 'lru_lambda': 1.908958e-02, 'lru_w_out': 4.537655e-02, 'mlp_w_up': 2.513397e-02, 'mlp_w_down': 5.168631e-02}


def _to_microbatches(a, axis):
    t = _jnp.moveaxis(a, axis, 0)
    t = t.reshape((N_MICROBATCH, t.shape[0] // N_MICROBATCH) + t.shape[1:])
    return _jnp.moveaxis(t, 1, axis + 1)


def setup_inputs(seed: int = 0) -> dict:
    inp = _fwd_setup_inputs(seed)
    key = _jax.random.fold_in(_jax.random.key(seed), 7919)
    shape, _ = _output_shape()
    out = dict(inp)
    out["loss_target"] = _jax.random.normal(_jax.random.fold_in(key, 0), shape, _jnp.float32)
    for i, name in enumerate(TWIN_WEIGHTS):
        w = inp[name].astype(_jnp.float32)
        if MOMENT_SCALE is None:
            s = _jnp.sqrt(_jnp.mean(_jnp.square(w)) + 1e-30)
        else:
            s = MOMENT_SCALE[name]
        km, kv = _jax.random.split(_jax.random.fold_in(key, i + 1))
        out[name] = w
        out["m_" + name] = s * _jax.random.normal(km, w.shape, _jnp.float32)
        out["v_" + name] = (s * s) * _jax.random.uniform(kv, w.shape, _jnp.float32, 0.5, 1.5)
    if N_MICROBATCH > 1:
        for name, axis in PER_EXAMPLE_BATCH_AXIS.items():
            out[name] = _to_microbatches(out[name], axis)
    return {'x': out['x'], 'norm1': out['norm1'], 'norm2': out['norm2'], 'final_norm': out['final_norm'], 'ret_w_in': out['ret_w_in'], 'ret_gn_gain': out['ret_gn_gain'], 'ret_w_out': out['ret_w_out'], 'gdn_w_in': out['gdn_w_in'], 'gdn_conv_w': out['gdn_conv_w'], 'gdn_a_log': out['gdn_a_log'], 'gdn_dt_bias': out['gdn_dt_bias'], 'gdn_norm_gain': out['gdn_norm_gain'], 'gdn_w_out': out['gdn_w_out'], 'gla_w_in': out['gla_w_in'], 'gla_w_gate_up': out['gla_w_gate_up'], 'gla_gate_bias': out['gla_gate_bias'], 'gla_norm_gain': out['gla_norm_gain'], 'gla_w_out': out['gla_w_out'], 'lru_w_in': out['lru_w_in'], 'lru_conv_w': out['lru_conv_w'], 'lru_conv_b': out['lru_conv_b'], 'lru_w_rgate': out['lru_w_rgate'], 'lru_b_rgate': out['lru_b_rgate'], 'lru_w_igate': out['lru_w_igate'], 'lru_b_igate': out['lru_b_igate'], 'lru_lambda': out['lru_lambda'], 'lru_w_out': out['lru_w_out'], 'mlp_w_up': out['mlp_w_up'], 'mlp_w_down': out['mlp_w_down'], 'loss_target': out['loss_target'], 'm_norm1': out['m_norm1'], 'm_norm2': out['m_norm2'], 'm_final_norm': out['m_final_norm'], 'm_ret_w_in': out['m_ret_w_in'], 'm_ret_gn_gain': out['m_ret_gn_gain'], 'm_ret_w_out': out['m_ret_w_out'], 'm_gdn_w_in': out['m_gdn_w_in'], 'm_gdn_conv_w': out['m_gdn_conv_w'], 'm_gdn_a_log': out['m_gdn_a_log'], 'm_gdn_dt_bias': out['m_gdn_dt_bias'], 'm_gdn_norm_gain': out['m_gdn_norm_gain'], 'm_gdn_w_out': out['m_gdn_w_out'], 'm_gla_w_in': out['m_gla_w_in'], 'm_gla_w_gate_up': out['m_gla_w_gate_up'], 'm_gla_gate_bias': out['m_gla_gate_bias'], 'm_gla_norm_gain': out['m_gla_norm_gain'], 'm_gla_w_out': out['m_gla_w_out'], 'm_lru_w_in': out['m_lru_w_in'], 'm_lru_conv_w': out['m_lru_conv_w'], 'm_lru_conv_b': out['m_lru_conv_b'], 'm_lru_w_rgate': out['m_lru_w_rgate'], 'm_lru_b_rgate': out['m_lru_b_rgate'], 'm_lru_w_igate': out['m_lru_w_igate'], 'm_lru_b_igate': out['m_lru_b_igate'], 'm_lru_lambda': out['m_lru_lambda'], 'm_lru_w_out': out['m_lru_w_out'], 'm_mlp_w_up': out['m_mlp_w_up'], 'm_mlp_w_down': out['m_mlp_w_down'], 'v_norm1': out['v_norm1'], 'v_norm2': out['v_norm2'], 'v_final_norm': out['v_final_norm'], 'v_ret_w_in': out['v_ret_w_in'], 'v_ret_gn_gain': out['v_ret_gn_gain'], 'v_ret_w_out': out['v_ret_w_out'], 'v_gdn_w_in': out['v_gdn_w_in'], 'v_gdn_conv_w': out['v_gdn_conv_w'], 'v_gdn_a_log': out['v_gdn_a_log'], 'v_gdn_dt_bias': out['v_gdn_dt_bias'], 'v_gdn_norm_gain': out['v_gdn_norm_gain'], 'v_gdn_w_out': out['v_gdn_w_out'], 'v_gla_w_in': out['v_gla_w_in'], 'v_gla_w_gate_up': out['v_gla_w_gate_up'], 'v_gla_gate_bias': out['v_gla_gate_bias'], 'v_gla_norm_gain': out['v_gla_norm_gain'], 'v_gla_w_out': out['v_gla_w_out'], 'v_lru_w_in': out['v_lru_w_in'], 'v_lru_conv_w': out['v_lru_conv_w'], 'v_lru_conv_b': out['v_lru_conv_b'], 'v_lru_w_rgate': out['v_lru_w_rgate'], 'v_lru_b_rgate': out['v_lru_b_rgate'], 'v_lru_w_igate': out['v_lru_w_igate'], 'v_lru_b_igate': out['v_lru_b_igate'], 'v_lru_lambda': out['v_lru_lambda'], 'v_lru_w_out': out['v_lru_w_out'], 'v_mlp_w_up': out['v_mlp_w_up'], 'v_mlp_w_down': out['v_mlp_w_down']}


def _loss(weights, diff, rest, loss_target):
    with _jax.named_scope("forward"):
        args = {**rest, TWIN_DIFF_INPUT: diff, **{k: w.astype(_WEIGHT_DTYPES[k]) for k, w in weights.items()}}
        y = _forward(args)
    with _jax.named_scope("loss_head"):
        err = _jnp.square(y.astype(_jnp.float32) - loss_target)
        return 0.5 * _jnp.sum(_jnp.mean(err, axis=-1)) if err.ndim else 0.5 * err


def _adamw(w, g, m, v):
    m = ADAM_B1 * m + (1.0 - ADAM_B1) * g
    v = ADAM_B2 * v + (1.0 - ADAM_B2) * _jnp.square(g)
    m_hat = m / (1.0 - ADAM_B1 ** ADAM_STEP)
    v_hat = v / (1.0 - ADAM_B2 ** ADAM_STEP)
    delta = -ADAM_LR * (m_hat / (_jnp.sqrt(v_hat) + ADAM_EPS) + ADAM_WD * w)
    return delta, m, v


def reference(x, norm1, norm2, final_norm, ret_w_in, ret_gn_gain, ret_w_out, gdn_w_in, gdn_conv_w, gdn_a_log, gdn_dt_bias, gdn_norm_gain, gdn_w_out, gla_w_in, gla_w_gate_up, gla_gate_bias, gla_norm_gain, gla_w_out, lru_w_in, lru_conv_w, lru_conv_b, lru_w_rgate, lru_b_rgate, lru_w_igate, lru_b_igate, lru_lambda, lru_w_out, mlp_w_up, mlp_w_down, loss_target, m_norm1, m_norm2, m_final_norm, m_ret_w_in, m_ret_gn_gain, m_ret_w_out, m_gdn_w_in, m_gdn_conv_w, m_gdn_a_log, m_gdn_dt_bias, m_gdn_norm_gain, m_gdn_w_out, m_gla_w_in, m_gla_w_gate_up, m_gla_gate_bias, m_gla_norm_gain, m_gla_w_out, m_lru_w_in, m_lru_conv_w, m_lru_conv_b, m_lru_w_rgate, m_lru_b_rgate, m_lru_w_igate, m_lru_b_igate, m_lru_lambda, m_lru_w_out, m_mlp_w_up, m_mlp_w_down, v_norm1, v_norm2, v_final_norm, v_ret_w_in, v_ret_gn_gain, v_ret_w_out, v_gdn_w_in, v_gdn_conv_w, v_gdn_a_log, v_gdn_dt_bias, v_gdn_norm_gain, v_gdn_w_out, v_gla_w_in, v_gla_w_gate_up, v_gla_gate_bias, v_gla_norm_gain, v_gla_w_out, v_lru_w_in, v_lru_conv_w, v_lru_conv_b, v_lru_w_rgate, v_lru_b_rgate, v_lru_w_igate, v_lru_b_igate, v_lru_lambda, v_lru_w_out, v_mlp_w_up, v_mlp_w_down):
    given = dict(x=x, norm1=norm1, norm2=norm2, final_norm=final_norm, ret_w_in=ret_w_in, ret_gn_gain=ret_gn_gain, ret_w_out=ret_w_out, gdn_w_in=gdn_w_in, gdn_conv_w=gdn_conv_w, gdn_a_log=gdn_a_log, gdn_dt_bias=gdn_dt_bias, gdn_norm_gain=gdn_norm_gain, gdn_w_out=gdn_w_out, gla_w_in=gla_w_in, gla_w_gate_up=gla_w_gate_up, gla_gate_bias=gla_gate_bias, gla_norm_gain=gla_norm_gain, gla_w_out=gla_w_out, lru_w_in=lru_w_in, lru_conv_w=lru_conv_w, lru_conv_b=lru_conv_b, lru_w_rgate=lru_w_rgate, lru_b_rgate=lru_b_rgate, lru_w_igate=lru_w_igate, lru_b_igate=lru_b_igate, lru_lambda=lru_lambda, lru_w_out=lru_w_out, mlp_w_up=mlp_w_up, mlp_w_down=mlp_w_down, loss_target=loss_target, m_norm1=m_norm1, m_norm2=m_norm2, m_final_norm=m_final_norm, m_ret_w_in=m_ret_w_in, m_ret_gn_gain=m_ret_gn_gain, m_ret_w_out=m_ret_w_out, m_gdn_w_in=m_gdn_w_in, m_gdn_conv_w=m_gdn_conv_w, m_gdn_a_log=m_gdn_a_log, m_gdn_dt_bias=m_gdn_dt_bias, m_gdn_norm_gain=m_gdn_norm_gain, m_gdn_w_out=m_gdn_w_out, m_gla_w_in=m_gla_w_in, m_gla_w_gate_up=m_gla_w_gate_up, m_gla_gate_bias=m_gla_gate_bias, m_gla_norm_gain=m_gla_norm_gain, m_gla_w_out=m_gla_w_out, m_lru_w_in=m_lru_w_in, m_lru_conv_w=m_lru_conv_w, m_lru_conv_b=m_lru_conv_b, m_lru_w_rgate=m_lru_w_rgate, m_lru_b_rgate=m_lru_b_rgate, m_lru_w_igate=m_lru_w_igate, m_lru_b_igate=m_lru_b_igate, m_lru_lambda=m_lru_lambda, m_lru_w_out=m_lru_w_out, m_mlp_w_up=m_mlp_w_up, m_mlp_w_down=m_mlp_w_down, v_norm1=v_norm1, v_norm2=v_norm2, v_final_norm=v_final_norm, v_ret_w_in=v_ret_w_in, v_ret_gn_gain=v_ret_gn_gain, v_ret_w_out=v_ret_w_out, v_gdn_w_in=v_gdn_w_in, v_gdn_conv_w=v_gdn_conv_w, v_gdn_a_log=v_gdn_a_log, v_gdn_dt_bias=v_gdn_dt_bias, v_gdn_norm_gain=v_gdn_norm_gain, v_gdn_w_out=v_gdn_w_out, v_gla_w_in=v_gla_w_in, v_gla_w_gate_up=v_gla_w_gate_up, v_gla_gate_bias=v_gla_gate_bias, v_gla_norm_gain=v_gla_norm_gain, v_gla_w_out=v_gla_w_out, v_lru_w_in=v_lru_w_in, v_lru_conv_w=v_lru_conv_w, v_lru_conv_b=v_lru_conv_b, v_lru_w_rgate=v_lru_w_rgate, v_lru_b_rgate=v_lru_b_rgate, v_lru_w_igate=v_lru_w_igate, v_lru_b_igate=v_lru_b_igate, v_lru_lambda=v_lru_lambda, v_lru_w_out=v_lru_w_out, v_mlp_w_up=v_mlp_w_up, v_mlp_w_down=v_mlp_w_down)
    weights = {n: given[n] for n in TWIN_WEIGHTS}
    shared = {n: given[n] for n in SHARED_INPUTS}
    per_example = {n: given[n] for n in ['x']}
    grad_fn = _jax.value_and_grad(_loss, argnums=(0, 1))

    def one_microbatch(ex, loss_target):
        ex = dict(ex)
        diff = ex.pop(TWIN_DIFF_INPUT)
        return grad_fn(weights, diff, {**shared, **ex}, loss_target)

    if N_MICROBATCH == 1:
        loss, (grad_w, grad_x) = one_microbatch(per_example, given["loss_target"])
    else:
        def body(carry, xs):
            loss_sum, grad_sum = carry
            l_k, (gw_k, gx_k) = one_microbatch(xs[0], xs[1])
            with _jax.named_scope("update"):
                return (loss_sum + l_k, _jax.tree.map(_jnp.add, grad_sum, gw_k)), gx_k

        init = (_jnp.zeros((), _jnp.float32), _jax.tree.map(_jnp.zeros_like, weights))
        (loss, grad_w), grad_x = _jax.lax.scan(body, init, (per_example, given["loss_target"]))
    with _jax.named_scope("update"):
        delta_w, new_m, new_v = {}, {}, {}
        for n in TWIN_WEIGHTS:
            delta_w[n], new_m[n], new_v[n] = _adamw(weights[n], grad_w[n], given["m_" + n], given["v_" + n])
    return (loss, grad_x, *[grad_w[n] for n in TWIN_WEIGHTS], *[delta_w[n] for n in TWIN_WEIGHTS],
            *[new_m[n] for n in TWIN_WEIGHTS], *[new_v[n] for n in TWIN_WEIGHTS])
```

```python
import functools
import math

import jax
import jax.numpy as jnp
from jax import lax
from jax.experimental import pallas as pl
from jax.experimental.pallas import tpu as pltpu

F32 = jnp.float32
BF = jnp.bfloat16
MESH = pl.DeviceIdType.MESH

NORM_EPS = 1e-6
CHUNK = 64
RET_HEADS = 8
GDN_HEADS = 16
GLA_HEADS = 4
GLA_RANK = 16
GLA_TAU = 16.0
LRU_BLOCKS = 16
LRU_C = 8.0
CONV_WIDTH = 4
ROPE_BASE = 10000.0
N_CHIPS = 4
N_DEV = 8

ADAM_LR = 0.001
ADAM_B1 = 0.9
ADAM_B2 = 0.999
ADAM_EPS = 1e-08
ADAM_WD = 0.01
ADAM_STEP = 10

VMEM_LIMIT_BYTES = 56 * 1024 * 1024
TOKEN_BLOCK = 256
ROW_BLOCK = 256


def _cparams(**kw):
    return pltpu.CompilerParams(vmem_limit_bytes=VMEM_LIMIT_BYTES, **kw)


def _raw_mm(a, b, ta, tb, hi):
    if hi:
        a, b, prec = a.astype(F32), b.astype(F32), lax.Precision.HIGHEST
    else:
        a, b, prec = a.astype(BF), b.astype(BF), None
    dims = (((0 if ta else 1,), (1 if tb else 0,)), ((), ()))
    return lax.dot_general(a, b, dims, precision=prec, preferred_element_type=F32)


@functools.partial(jax.custom_vjp, nondiff_argnums=(2, 3, 4))
def _mm_vjp(a, b, ta, tb, hi):
    return _raw_mm(a, b, ta, tb, hi)


def _mm_vjp_fwd(a, b, ta, tb, hi):
    return _raw_mm(a, b, ta, tb, hi), (a, b)


def _mm_vjp_bwd(ta, tb, hi, res, g):
    a, b = res
    da = _raw_mm(b, g, tb, True, hi) if ta else _raw_mm(g, b, False, not tb, hi)
    db = _raw_mm(g, a, True, ta, hi) if tb else _raw_mm(a, g, not ta, False, hi)
    return da, db


_mm_vjp.defvjp(_mm_vjp_fwd, _mm_vjp_bwd)


def _mm_diff(a, b, ta=False, tb=False, hi=False):
    return _mm_vjp(a, b, ta, tb, hi)


def _mm_plain(a, b, ta=False, tb=False, hi=False):
    return _raw_mm(a, b, ta, tb, hi)


def _shift_rows(x, k, up):
    if k == 0:
        return x
    n = x.shape[0]
    rows = lax.broadcasted_iota(jnp.int32, x.shape, 0)
    if up:
        return jnp.where(rows < n - k, pltpu.roll(x, n - k, 0), 0.0)
    return jnp.where(rows >= k, pltpu.roll(x, k, 0), 0.0)


@functools.partial(jax.custom_vjp, nondiff_argnums=(1, 2))
def _shift_vjp(x, k, up):
    return _shift_rows(x, k, up)


def _shift_vjp_fwd(x, k, up):
    return _shift_rows(x, k, up), None


def _shift_vjp_bwd(k, up, _, g):
    return (_shift_rows(g, k, not up),)


_shift_vjp.defvjp(_shift_vjp_fwd, _shift_vjp_bwd)


def _sigmoid(x):
    return 1.0 / (1.0 + jnp.exp(-x))


def _silu(x):
    return x * _sigmoid(x)


def _softplus(x):
    return jnp.maximum(x, 0.0) + jnp.log(1.0 + jnp.exp(-jnp.abs(x)))


def _gelu_tanh(x):
    return 0.5 * x * (1.0 + jnp.tanh(math.sqrt(2.0 / math.pi) * (x + 0.044715 * (x * x * x))))


def _expm1(x):
    series = x * (1.0 + x * (0.5 + x * (1.0 / 6.0 + x * (1.0 / 24.0))))
    return jnp.where(jnp.abs(x) < 0.03, series, jnp.exp(x) - 1.0)


def _rmsnorm(x, g):
    return x * lax.rsqrt(jnp.mean(x * x, axis=-1, keepdims=True) + NORM_EPS) * g


def _head_norm(o, gain, center):
    if center:
        o = o - jnp.mean(o, axis=-1, keepdims=True)
    return o * lax.rsqrt(jnp.mean(o * o, axis=-1, keepdims=True) + NORM_EPS) * gain


def _l2norm(x):
    return x * lax.rsqrt(jnp.sum(x * x, axis=-1, keepdims=True) + NORM_EPS)


def _iota2(shape, dim):
    return lax.broadcasted_iota(jnp.int32, shape, dim)


def _tri_ones(n, upper=False):
    i, j = _iota2((n, n), 0), _iota2((n, n), 1)
    return jnp.where((i <= j) if upper else (j <= i), 1.0, 0.0).astype(F32)


def _linscan(a, u, rev):
    n = a.shape[0]
    rows = _iota2(a.shape, 0)
    d = 1
    while d < n:
        if rev:
            valid = rows < n - d
            a_s, u_s = pltpu.roll(a, n - d, 0), pltpu.roll(u, n - d, 0)
        else:
            valid = rows >= d
            a_s, u_s = pltpu.roll(a, d, 0), pltpu.roll(u, d, 0)
        u = a * jnp.where(valid, u_s, 0.0) + u
        a = a * jnp.where(valid, a_s, 1.0)
        d *= 2
    return u


def _ret_chunk(mm, q, k, v, g, gain, state, cos, sin, dintra, qdec, kdec, cdec):
    dk = q.shape[-1]
    half = dk // 2

    def rot(t):
        t1, t2 = t[:, :half], t[:, half:]
        return jnp.concatenate([t1 * cos - t2 * sin, t1 * sin + t2 * cos], axis=-1)

    qr = rot(q)
    kr = rot(k) * (dk ** -0.5)
    scores = mm(qr, kr, tb=True) * dintra
    o = mm(scores, v) + mm(qr * qdec, state)
    new_state = state * cdec + mm(kr * kdec, v, ta=True)
    y = _head_norm(o, gain, True) * _silu(g)
    return y, new_state


def _gla_chunk(mm, q, k, v, r, glow, wgu, bias, gain, state_t):
    dk = q.shape[-1]
    c = q.shape[0]
    logit = mm(glow, wgu) + bias
    la = -_softplus(-logit) * (1.0 / GLA_TAU)
    cum = mm(_tri_ones(c), la, hi=True)
    rows = _iota2(la.shape, 0)
    ref = jnp.sum(jnp.where(rows < c // 2, la, 0.0), axis=0, keepdims=True)
    tot = jnp.sum(la, axis=0, keepdims=True)
    fwd, bwd = jnp.exp(cum - ref), jnp.exp(ref - cum)
    qs = q * (dk ** -0.5)
    s_lo = mm(qs * fwd, k * bwd, tb=True)
    s_up = mm(qs * bwd, k * fwd, tb=True)
    i, j = _iota2((c, c), 0), _iota2((c, c), 1)
    scores = jnp.where(i >= j, s_lo, s_up)
    o = mm(scores, v) + mm(qs * jnp.exp(cum), state_t, tb=True)
    k_end = k * jnp.exp(tot - cum)
    new_state_t = state_t * jnp.exp(tot) + mm(v, k_end, ta=True)
    y = _head_norm(o, gain, False) * _silu(r)
    return y, new_state_t


def _gdn_chunk(mm, q, k, v, z, ba, a_log, dt_bias, gain, state, head):
    c, dk = q.shape
    lanes = _iota2((1, ba.shape[-1]), 1)
    oh_b = jnp.where(lanes == head, 1.0, 0.0).astype(F32)
    oh_a = jnp.where(lanes == head + GDN_HEADS, 1.0, 0.0).astype(F32)
    beta = _sigmoid(jnp.sum(ba * oh_b, axis=-1, keepdims=True))
    a_logit = jnp.sum(ba * oh_a, axis=-1, keepdims=True)
    a_h = jnp.sum(a_log * oh_b, axis=-1, keepdims=True)
    dt_h = jnp.sum(dt_bias * oh_b, axis=-1, keepdims=True)
    la = -jnp.exp(a_h) * _softplus(a_logit + dt_h)
    q = _l2norm(q) * (dk ** -0.5)
    k = _l2norm(k)
    cum_k = mm(_tri_ones(c), la * jnp.ones((c, dk), F32), hi=True)
    la_sq = la * jnp.ones((c, c), F32)
    cum_i = mm(_tri_ones(c), la_sq, hi=True)
    cum_j = mm(la_sq, _tri_ones(c, upper=True), ta=True, hi=True)
    i, j = _iota2((c, c), 0), _iota2((c, c), 1)
    strict = i > j
    rel = jnp.where(strict, jnp.exp(jnp.where(strict, cum_i - cum_j, 0.0)), 0.0)
    a_mat = beta * rel * mm(k, k, tb=True)
    eye_c = jnp.where(i == j, 1.0, 0.0).astype(F32)
    inv = eye_c - a_mat
    power = a_mat
    for _ in range(int(math.log2(c)) - 1):
        power = mm(power, power, hi=True)
        inv = inv + mm(inv, power, hi=True)
    tot = jnp.sum(la, axis=0, keepdims=True)
    u = mm(inv, beta * v, hi=True)
    w = mm(inv, (beta * jnp.exp(cum_k)) * k, hi=True)
    k_end = k * jnp.exp(tot - cum_k)
    di, dj = _iota2((dk, dk), 0), _iota2((dk, dk), 1)
    trans = jnp.exp(tot) * jnp.where(di == dj, 1.0, 0.0).astype(F32) - mm(k_end, w, ta=True)
    new_state = mm(trans, state) + mm(k_end, u, ta=True)
    o = mm(q, new_state)
    y = _head_norm(o, gain, False) * _silu(z)
    return y, new_state


def _conv_taps(shift, x, taps):
    out = None
    for tap, w in enumerate(taps):
        term = shift(x, CONV_WIDTH - 1 - tap, False) * w
        out = term if out is None else out + term
    return out


def _lru_pre(mm, shift, xb, yb, taps, cb, lam, wr, br, wi, bi):
    xb = _conv_taps(shift, xb, taps) + cb
    r = _sigmoid(mm(xb, wr) + br)
    i = _sigmoid(mm(xb, wi) + bi)
    log_a = (-LRU_C) * _softplus(-lam) * r
    a = jnp.exp(log_a)
    u = jnp.sqrt(-_expm1(2.0 * log_a)) * (i * xb)
    return a, u, _gelu_tanh(yb)


def _pick(n, pref):
    for t in (pref, 2048, 1024, 512, 256, 128):
        if t <= n and n % t == 0:
            return t
    return n


def _matmul(a, b, *, name, ta=False, tb=False, out_dtype=F32, b_sharded=False, o_sharded=False,
            epilogue=None, extra=None, bm=512, bn=512, bk=2048):
    m, kdim = (a.shape[1], a.shape[0]) if ta else a.shape
    if b_sharded:
        nq = b.shape[2]
        n = b.shape[1] if tb else N_CHIPS * nq
        assert kdim == (N_CHIPS * nq if tb else b.shape[1])
    else:
        n = b.shape[0] if tb else b.shape[1]
        assert kdim == (b.shape[1] if tb else b.shape[0])
    bm, bk = _pick(m, bm), _pick(kdim, bk)
    if b_sharded and tb:
        bk = _pick(b.shape[2], bk)
    bn = _pick(b.shape[2] if (b_sharded and not tb) else (n // N_CHIPS if o_sharded else n), bn)
    nk = kdim // bk
    grid = (m // bm, n // bn, nk)

    a_spec = pl.BlockSpec((bk, bm), lambda i, j, k: (k, i)) if ta else pl.BlockSpec((bm, bk), lambda i, j, k: (i, k))
    if b_sharded and not tb:
        per = b.shape[2] // bn
        b_spec = pl.BlockSpec((None, bk, bn), lambda i, j, k: (j // per, k, j % per))
    elif b_sharded:
        per = b.shape[2] // bk
        b_spec = pl.BlockSpec((None, bn, bk), lambda i, j, k: (k // per, j, k % per))
    elif tb:
        b_spec = pl.BlockSpec((bn, bk), lambda i, j, k: (j, k))
    else:
        b_spec = pl.BlockSpec((bk, bn), lambda i, j, k: (k, j))
    if o_sharded:
        per_o = (n // N_CHIPS) // bn
        o_spec = pl.BlockSpec((None, bm, bn), lambda i, j, k: (j // per_o, i, j % per_o))
        out_shape = jax.ShapeDtypeStruct((N_CHIPS, m, n // N_CHIPS), out_dtype)
    else:
        o_spec = pl.BlockSpec((bm, bn), lambda i, j, k: (i, j))
        out_shape = jax.ShapeDtypeStruct((m, n), out_dtype)
    in_specs = [a_spec, b_spec]
    operands = [a, b]
    if extra is not None:
        in_specs.append(pl.BlockSpec((bm, bn), lambda i, j, k: (i, j)))
        operands.append(extra)

    def body(*refs):
        a_ref, b_ref = refs[0], refs[1]
        e_ref = refs[2] if extra is not None else None
        o_ref, acc_ref = refs[-2], refs[-1]
        k = pl.program_id(2)

        @pl.when(k == 0)
        def _():
            acc_ref[...] = jnp.zeros_like(acc_ref)

        acc_ref[...] += _raw_mm(a_ref[...], b_ref[...], ta, tb, False)

        @pl.when(k == nk - 1)
        def _():
            acc = acc_ref[...]
            if epilogue == 'add':
                acc = acc + e_ref[...].astype(F32)
            elif epilogue == 'relu2':
                acc = jnp.square(jnp.maximum(acc, 0.0))
            elif epilogue == 'dact':
                acc = acc * (2.0 * jnp.sqrt(e_ref[...].astype(F32)))
            o_ref[...] = acc.astype(o_ref.dtype)

    return pl.pallas_call(
        body, name=name, grid=grid, in_specs=in_specs, out_specs=o_spec, out_shape=out_shape,
        scratch_shapes=[pltpu.VMEM((bm, bn), F32)], compiler_params=_cparams(),
    )(*operands)


def _rmsnorm_fwd(x, g, name):
    s, d = x.shape

    def body(x_ref, g_ref, o_ref):
        o_ref[...] = _rmsnorm(x_ref[...], g_ref[...]).astype(BF)

    return pl.pallas_call(
        body, name=name, grid=(s // ROW_BLOCK,),
        in_specs=[pl.BlockSpec((ROW_BLOCK, d), lambda i: (i, 0)), pl.BlockSpec((1, d), lambda i: (0, 0))],
        out_specs=pl.BlockSpec((ROW_BLOCK, d), lambda i: (i, 0)),
        out_shape=jax.ShapeDtypeStruct((s, d), BF), compiler_params=_cparams(),
    )(x, g.reshape(1, d))


def _rmsnorm_bwd(x, g, dh, dres, name):
    s, d = x.shape

    def body(x_ref, g_ref, dh_ref, dres_ref, dx_ref, dg_ref):
        _, vjp = jax.vjp(_rmsnorm, x_ref[...], g_ref[...])
        dx, dg = vjp(dh_ref[...].astype(F32))
        dx_ref[...] = dres_ref[...] + dx

        @pl.when(pl.program_id(0) == 0)
        def _():
            dg_ref[...] = jnp.zeros_like(dg_ref)

        dg_ref[...] += dg

    row = pl.BlockSpec((ROW_BLOCK, d), lambda i: (i, 0))
    vec = pl.BlockSpec((1, d), lambda i: (0, 0))
    dx, dg = pl.pallas_call(
        body, name=name, grid=(s // ROW_BLOCK,), in_specs=[row, vec, row, row], out_specs=[row, vec],
        out_shape=[jax.ShapeDtypeStruct((s, d), F32), jax.ShapeDtypeStruct((1, d), F32)],
        compiler_params=_cparams(),
    )(x, g.reshape(1, d), dh, dres)
    return dx, dg.reshape(d)


def _loss_head(x, g, target):
    s, d = x.shape

    def loss_fn(xv, gv, tv):
        err = _rmsnorm(xv, gv) - tv
        return 0.5 * jnp.sum(jnp.mean(err * err, axis=-1, keepdims=True), axis=0, keepdims=True)

    def body(x_ref, g_ref, t_ref, dx_ref, dg_ref, loss_ref):
        tv = t_ref[...]
        loss, vjp = jax.vjp(lambda xv, gv: loss_fn(xv, gv, tv), x_ref[...], g_ref[...])
        dx, dg = vjp(jnp.ones((1, 1), F32))
        dx_ref[...] = dx

        @pl.when(pl.program_id(0) == 0)
        def _():
            dg_ref[...] = jnp.zeros_like(dg_ref)
            loss_ref[...] = jnp.zeros_like(loss_ref)

        dg_ref[...] += dg
        loss_ref[...] += loss * jnp.ones_like(loss_ref)

    row = pl.BlockSpec((ROW_BLOCK, d), lambda i: (i, 0))
    vec = pl.BlockSpec((1, d), lambda i: (0, 0))
    dx, dg, loss = pl.pallas_call(
        body, name="loss_head", grid=(s // ROW_BLOCK,), in_specs=[row, vec, row],
        out_specs=[row, vec, pl.BlockSpec((1, 128), lambda i: (0, 0))],
        out_shape=[jax.ShapeDtypeStruct((s, d), F32), jax.ShapeDtypeStruct((1, d), F32),
                   jax.ShapeDtypeStruct((1, 128), F32)],
        compiler_params=_cparams(),
    )(x, g.reshape(1, d), target)
    return loss[0, 0], dx, dg.reshape(d)


def _ret_tables(s, dk):
    h = jnp.arange(RET_HEADS, dtype=F32)
    log_gamma = jnp.log1p(-jnp.exp2(-5.0 - h))
    pos = jnp.arange(CHUNK, dtype=F32)
    dist = jnp.abs(pos[:, None] - pos[None, :])
    dintra = jnp.exp(log_gamma[:, None, None] * dist)
    qdec = jnp.exp(log_gamma[:, None] * (pos + 1.0))[:, :, None]
    kdec = jnp.exp(log_gamma[:, None] * (CHUNK - 1.0 - pos))[:, :, None]
    cdec = jnp.exp(log_gamma * CHUNK)[:, None, None]
    inv = ROPE_BASE ** (-jnp.arange(0, dk, 2, dtype=F32) / dk)
    ang = jnp.arange(s, dtype=F32)[:, None] * inv[None, :]
    return jnp.cos(ang), jnp.sin(ang), dintra, qdec, kdec, cdec


def _ret_specs(s, dk, dv, tb, rev):
    nh = RET_HEADS
    nb = s // tb
    bi = (lambda b: nb - 1 - b) if rev else (lambda b: b)
    voff = 2 * nh * dk // dv
    cpb = tb // CHUNK
    return dict(
        q=pl.BlockSpec((tb, dk), lambda h, b: (bi(b), h)),
        k=pl.BlockSpec((tb, dk), lambda h, b: (bi(b), nh + h)),
        v=pl.BlockSpec((tb, dv), lambda h, b: (bi(b), voff + h)),
        g=pl.BlockSpec((tb, dv), lambda h, b: (bi(b), voff + nh + h)),
        gain=pl.BlockSpec((None, 1, dv), lambda h, b: (h, 0, 0)),
        cs=pl.BlockSpec((tb, dk // 2), lambda h, b: (bi(b), 0)),
        dintra=pl.BlockSpec((None, CHUNK, CHUNK), lambda h, b: (h, 0, 0)),
        dec=pl.BlockSpec((None, CHUNK, 1), lambda h, b: (h, 0, 0)),
        cdec=pl.BlockSpec((None, 1, 1), lambda h, b: (h, 0, 0)),
        hv=pl.BlockSpec((tb, dv), lambda h, b: (bi(b), h)),
        hk=pl.BlockSpec((tb, dk), lambda h, b: (bi(b), h)),
        st=pl.BlockSpec((None, cpb, dk, dv), lambda h, b: (h, bi(b), 0, 0)),
    )


def _ret_fwd(proj, gain, tables):
    s = proj.shape[0]
    d = proj.shape[1] // 6
    dk, dv = d // RET_HEADS, 2 * d // RET_HEADS
    tb = min(TOKEN_BLOCK, s)
    cpb = tb // CHUNK
    sp = _ret_specs(s, dk, dv, tb, False)
    cos, sin, dintra, qdec, kdec, cdec = tables

    def body(q_ref, k_ref, v_ref, g_ref, gain_ref, cos_ref, sin_ref, di_ref, qd_ref, kd_ref, cd_ref,
             y_ref, st_ref, state):
        @pl.when(pl.program_id(1) == 0)
        def _():
            state[...] = jnp.zeros_like(state)

        for c in range(cpb):
            sl = pl.ds(c * CHUNK, CHUNK)
            st_ref[c] = state[...]
            y, new_state = _ret_chunk(_mm_plain, q_ref[sl, :], k_ref[sl, :], v_ref[sl, :], g_ref[sl, :],
                                      gain_ref[...], state[...], cos_ref[sl, :], sin_ref[sl, :],
                                      di_ref[...], qd_ref[...], kd_ref[...], cd_ref[...])
            y_ref[sl, :] = y.astype(BF)
            state[...] = new_state

    return pl.pallas_call(
        body, name="ret_fwd", grid=(RET_HEADS, s // tb),
        in_specs=[sp['q'], sp['k'], sp['v'], sp['g'], sp['gain'], sp['cs'], sp['cs'], sp['dintra'], sp['dec'],
                  sp['dec'], sp['cdec']],
        out_specs=[sp['hv'], sp['st']],
        out_shape=[jax.ShapeDtypeStruct((s, RET_HEADS * dv), BF),
                   jax.ShapeDtypeStruct((RET_HEADS, s // CHUNK, dk, dv), F32)],
        scratch_shapes=[pltpu.VMEM((dk, dv), F32)], compiler_params=_cparams(),
    )(proj, proj, proj, proj, gain.reshape(RET_HEADS, 1, dv), cos, sin, dintra, qdec, kdec, cdec)


def _ret_bwd(proj, gain, tables, states, dy):
    s = proj.shape[0]
    d = proj.shape[1] // 6
    dk, dv = d // RET_HEADS, 2 * d // RET_HEADS
    tb = min(TOKEN_BLOCK, s)
    cpb = tb // CHUNK
    sp = _ret_specs(s, dk, dv, tb, True)
    cos, sin, dintra, qdec, kdec, cdec = tables

    def body(q_ref, k_ref, v_ref, g_ref, gain_ref, cos_ref, sin_ref, di_ref, qd_ref, kd_ref, cd_ref, st_ref,
             dy_ref, dq_ref, dk_ref, dv_ref, dg_ref, dgain_ref, dstate):
        @pl.when(pl.program_id(1) == 0)
        def _():
            dstate[...] = jnp.zeros_like(dstate)
            dgain_ref[...] = jnp.zeros_like(dgain_ref)

        for c in reversed(range(cpb)):
            sl = pl.ds(c * CHUNK, CHUNK)
            cos_c, sin_c = cos_ref[sl, :], sin_ref[sl, :]
            di, qd, kd, cd = di_ref[...], qd_ref[...], kd_ref[...], cd_ref[...]

            def fn(q, k, v, g, gn, st):
                return _ret_chunk(_mm_diff, q, k, v, g, gn, st, cos_c, sin_c, di, qd, kd, cd)

            _, vjp = jax.vjp(fn, q_ref[sl, :], k_ref[sl, :], v_ref[sl, :], g_ref[sl, :], gain_ref[...],
                             st_ref[c])
            dq, dkk, dvv, dg, dgn, dst = vjp((dy_ref[sl, :].astype(F32), dstate[...]))
            dq_ref[sl, :] = dq.astype(BF)
            dk_ref[sl, :] = dkk.astype(BF)
            dv_ref[sl, :] = dvv.astype(BF)
            dg_ref[sl, :] = dg.astype(BF)
            dgain_ref[...] += dgn
            dstate[...] = dst

    dq, dkk, dvv, dg, dgain = pl.pallas_call(
        body, name="ret_bwd", grid=(RET_HEADS, s // tb),
        in_specs=[sp['q'], sp['k'], sp['v'], sp['g'], sp['gain'], sp['cs'], sp['cs'], sp['dintra'], sp['dec'],
                  sp['dec'], sp['cdec'], sp['st'], sp['hv']],
        out_specs=[sp['hk'], sp['hk'], sp['hv'], sp['hv'], sp['gain']],
        out_shape=[jax.ShapeDtypeStruct((s, RET_HEADS * dk), BF), jax.ShapeDtypeStruct((s, RET_HEADS * dk), BF),
                   jax.ShapeDtypeStruct((s, RET_HEADS * dv), BF), jax.ShapeDtypeStruct((s, RET_HEADS * dv), BF),
                   jax.ShapeDtypeStruct((RET_HEADS, 1, dv), F32)],
        scratch_shapes=[pltpu.VMEM((dk, dv), F32)], compiler_params=_cparams(),
    )(proj, proj, proj, proj, gain.reshape(RET_HEADS, 1, dv), cos, sin, dintra, qdec, kdec, cdec, states, dy)
    return jnp.concatenate([dq, dkk, dvv, dg], axis=1), dgain.reshape(RET_HEADS, dv)


def _gla_specs(s, dk, dv, tb, rev):
    nh = GLA_HEADS
    nb = s // tb
    bi = (lambda b: nb - 1 - b) if rev else (lambda b: b)
    voff = 2 * nh * dk // dv
    cpb = tb // CHUNK
    return dict(
        q=pl.BlockSpec((tb, dk), lambda h, b: (bi(b), h)),
        k=pl.BlockSpec((tb, dk), lambda h, b: (bi(b), nh + h)),
        v=pl.BlockSpec((tb, dv), lambda h, b: (bi(b), voff + h)),
        r=pl.BlockSpec((tb, dv), lambda h, b: (bi(b), voff + nh + h)),
        glow=pl.BlockSpec((tb, 128), lambda h, b: (bi(b), 0)),
        wgu=pl.BlockSpec((128, dk), lambda h, b: (0, h)),
        bias=pl.BlockSpec((1, dk), lambda h, b: (0, h)),
        gain=pl.BlockSpec((None, 1, dv), lambda h, b: (h, 0, 0)),
        hv=pl.BlockSpec((tb, dv), lambda h, b: (bi(b), h)),
        hk=pl.BlockSpec((tb, dk), lambda h, b: (bi(b), h)),
        hg=pl.BlockSpec((tb, 128), lambda h, b: (bi(b), h)),
        st=pl.BlockSpec((None, cpb, dv, dk), lambda h, b: (h, bi(b), 0, 0)),
    )


def _gla_fwd(proj, glow, wgu, bias, gain):
    s = proj.shape[0]
    d = proj.shape[1] // 3
    dk, dv = d // 2 // GLA_HEADS, d // GLA_HEADS
    tb = min(TOKEN_BLOCK, s)
    cpb = tb // CHUNK
    sp = _gla_specs(s, dk, dv, tb, False)

    def body(q_ref, k_ref, v_ref, r_ref, gl_ref, wgu_ref, b_ref, gain_ref, y_ref, st_ref, state):
        @pl.when(pl.program_id(1) == 0)
        def _():
            state[...] = jnp.zeros_like(state)

        for c in range(cpb):
            sl = pl.ds(c * CHUNK, CHUNK)
            st_ref[c] = state[...]
            y, new_state = _gla_chunk(_mm_plain, q_ref[sl, :], k_ref[sl, :], v_ref[sl, :], r_ref[sl, :],
                                      gl_ref[sl, :], wgu_ref[...], b_ref[...], gain_ref[...], state[...])
            y_ref[sl, :] = y.astype(BF)
            state[...] = new_state

    return pl.pallas_call(
        body, name="gla_fwd", grid=(GLA_HEADS, s // tb),
        in_specs=[sp['q'], sp['k'], sp['v'], sp['r'], sp['glow'], sp['wgu'], sp['bias'], sp['gain']],
        out_specs=[sp['hv'], sp['st']],
        out_shape=[jax.ShapeDtypeStruct((s, GLA_HEADS * dv), BF),
                   jax.ShapeDtypeStruct((GLA_HEADS, s // CHUNK, dv, dk), F32)],
        scratch_shapes=[pltpu.VMEM((dv, dk), F32)], compiler_params=_cparams(),
    )(proj, proj, proj, proj, glow, wgu, bias, gain.reshape(GLA_HEADS, 1, dv))


def _gla_bwd(proj, glow, wgu, bias, gain, states, dy):
    s = proj.shape[0]
    d = proj.shape[1] // 3
    dk, dv = d // 2 // GLA_HEADS, d // GLA_HEADS
    tb = min(TOKEN_BLOCK, s)
    cpb = tb // CHUNK
    sp = _gla_specs(s, dk, dv, tb, True)

    def body(q_ref, k_ref, v_ref, r_ref, gl_ref, wgu_ref, b_ref, gain_ref, st_ref, dy_ref,
             dq_ref, dk_ref, dv_ref, dr_ref, dgl_ref, dwgu_ref, db_ref, dgain_ref, dstate):
        @pl.when(pl.program_id(1) == 0)
        def _():
            dstate[...] = jnp.zeros_like(dstate)
            dwgu_ref[...] = jnp.zeros_like(dwgu_ref)
            db_ref[...] = jnp.zeros_like(db_ref)
            dgain_ref[...] = jnp.zeros_like(dgain_ref)

        for c in reversed(range(cpb)):
            sl = pl.ds(c * CHUNK, CHUNK)

            def fn(q, k, v, r, gl, w, b, gn, st):
                return _gla_chunk(_mm_diff, q, k, v, r, gl, w, b, gn, st)

            _, vjp = jax.vjp(fn, q_ref[sl, :], k_ref[sl, :], v_ref[sl, :], r_ref[sl, :], gl_ref[sl, :],
                             wgu_ref[...], b_ref[...], gain_ref[...], st_ref[c])
            dq, dkk, dvv, dr, dgl, dw, db, dgn, dst = vjp((dy_ref[sl, :].astype(F32), dstate[...]))
            dq_ref[sl, :] = dq.astype(BF)
            dk_ref[sl, :] = dkk.astype(BF)
            dv_ref[sl, :] = dvv.astype(BF)
            dr_ref[sl, :] = dr.astype(BF)
            dgl_ref[sl, :] = dgl.astype(BF)
            dwgu_ref[...] += dw
            db_ref[...] += db
            dgain_ref[...] += dgn
            dstate[...] = dst

    nh = GLA_HEADS
    dq, dkk, dvv, dr, dgl, dwgu, db, dgain = pl.pallas_call(
        body, name="gla_bwd", grid=(nh, s // tb),
        in_specs=[sp['q'], sp['k'], sp['v'], sp['r'], sp['glow'], sp['wgu'], sp['bias'], sp['gain'], sp['st'],
                  sp['hv']],
        out_specs=[sp['hk'], sp['hk'], sp['hv'], sp['hv'], sp['hg'], sp['wgu'], sp['bias'], sp['gain']],
        out_shape=[jax.ShapeDtypeStruct((s, nh * dk), BF), jax.ShapeDtypeStruct((s, nh * dk), BF),
                   jax.ShapeDtypeStruct((s, nh * dv), BF), jax.ShapeDtypeStruct((s, nh * dv), BF),
                   jax.ShapeDtypeStruct((s, nh * 128), BF), jax.ShapeDtypeStruct((128, nh * dk), F32),
                   jax.ShapeDtypeStruct((1, nh * dk), F32), jax.ShapeDtypeStruct((nh, 1, dv), F32)],
        scratch_shapes=[pltpu.VMEM((dv, dk), F32)], compiler_params=_cparams(),
    )(proj, proj, proj, proj, glow, wgu, bias, gain.reshape(nh, 1, dv), states, dy)
    return jnp.concatenate([dq, dkk, dvv, dr], axis=1), dgl, dwgu, db, dgain.reshape(nh, dv)


def _gdn_specs(s, dk):
    nh = GDN_HEADS
    col = lambda off: pl.BlockSpec((s, dk), lambda h: (0, off + h))
    tap = lambda off: pl.BlockSpec((CONV_WIDTH, dk), lambda h: (0, off + h))
    vec = pl.BlockSpec((1, 128), lambda h: (0, 0))
    return dict(q=col(0), k=col(nh), v=col(2 * nh), z=col(3 * nh), ba=pl.BlockSpec((s, 128), lambda h: (0, 0)),
                cq=tap(0), ck=tap(nh), cv=tap(2 * nh), vec=vec, gain=pl.BlockSpec((1, dk), lambda h: (0, 0)),
                head=col(0))


def _conv_silu(shift, x, taps):
    return _silu(_conv_taps(shift, x, taps))


def _load_taps(ref):
    return [ref[t:t + 1, :] for t in range(CONV_WIDTH)]


def _gdn_fwd(proj, ba, conv_w, a_log, dt_bias, gain):
    s = proj.shape[0]
    dk = proj.shape[1] // (4 * GDN_HEADS)
    sp = _gdn_specs(s, dk)
    nchunks = s // CHUNK

    def body(q_ref, k_ref, v_ref, z_ref, ba_ref, cq_ref, ck_ref, cv_ref, al_ref, dt_ref, gain_ref, y_ref,
             qc, kc, vc, state):
        head = pl.program_id(0)
        qc[...] = _conv_silu(_shift_rows, q_ref[...], _load_taps(cq_ref))
        kc[...] = _conv_silu(_shift_rows, k_ref[...], _load_taps(ck_ref))
        vc[...] = _conv_silu(_shift_rows, v_ref[...], _load_taps(cv_ref))
        state[...] = jnp.zeros_like(state)

        def step(c, carry):
            sl = pl.ds(pl.multiple_of(c * CHUNK, CHUNK), CHUNK)
            y, new_state = _gdn_chunk(_mm_plain, qc[sl, :], kc[sl, :], vc[sl, :], z_ref[sl, :], ba_ref[sl, :],
                                      al_ref[...], dt_ref[...], gain_ref[...], state[...], head)
            y_ref[sl, :] = y.astype(BF)
            state[...] = new_state
            return carry

        lax.fori_loop(0, nchunks, step, 0)

    return pl.pallas_call(
        body, name="gdn_fwd", grid=(GDN_HEADS,),
        in_specs=[sp['q'], sp['k'], sp['v'], sp['z'], sp['ba'], sp['cq'], sp['ck'], sp['cv'], sp['vec'], sp['vec'],
                  sp['gain']],
        out_specs=sp['head'], out_shape=jax.ShapeDtypeStruct((s, GDN_HEADS * dk), BF),
        scratch_shapes=[pltpu.VMEM((s, dk), F32)] * 3 + [pltpu.VMEM((dk, dk), F32)],
        compiler_params=_cparams(),
    )(proj, proj, proj, proj, ba, conv_w, conv_w, conv_w, a_log, dt_bias, gain)


def _gdn_bwd(proj, ba, conv_w, a_log, dt_bias, gain, dy):
    s = proj.shape[0]
    dk = proj.shape[1] // (4 * GDN_HEADS)
    sp = _gdn_specs(s, dk)
    nchunks = s // CHUNK

    def body(q_ref, k_ref, v_ref, z_ref, ba_ref, cq_ref, ck_ref, cv_ref, al_ref, dt_ref, gain_ref, dy_ref,
             dq_ref, dk_ref, dv_ref, dz_ref, dba_ref, dcq_ref, dck_ref, dcv_ref, dal_ref, ddt_ref, dgain_ref,
             qc, kc, vc, dqc, dkc, dvc, states, state, dstate):
        head = pl.program_id(0)

        @pl.when(head == 0)
        def _():
            dba_ref[...] = jnp.zeros_like(dba_ref)
            dal_ref[...] = jnp.zeros_like(dal_ref)
            ddt_ref[...] = jnp.zeros_like(ddt_ref)
            dgain_ref[...] = jnp.zeros_like(dgain_ref)

        qc[...] = _conv_silu(_shift_rows, q_ref[...], _load_taps(cq_ref))
        kc[...] = _conv_silu(_shift_rows, k_ref[...], _load_taps(ck_ref))
        vc[...] = _conv_silu(_shift_rows, v_ref[...], _load_taps(cv_ref))
        state[...] = jnp.zeros_like(state)

        def fstep(c, carry):
            sl = pl.ds(pl.multiple_of(c * CHUNK, CHUNK), CHUNK)
            states[c] = state[...]
            _, new_state = _gdn_chunk(_mm_plain, qc[sl, :], kc[sl, :], vc[sl, :], z_ref[sl, :], ba_ref[sl, :],
                                      al_ref[...], dt_ref[...], gain_ref[...], state[...], head)
            state[...] = new_state
            return carry

        lax.fori_loop(0, nchunks, fstep, 0)
        dstate[...] = jnp.zeros_like(dstate)

        def bstep(i, carry):
            c = nchunks - 1 - i
            sl = pl.ds(pl.multiple_of(c * CHUNK, CHUNK), CHUNK)

            def fn(q, k, v, z, b, al, dt, gn, st):
                return _gdn_chunk(_mm_diff, q, k, v, z, b, al, dt, gn, st, head)

            _, vjp = jax.vjp(fn, qc[sl, :], kc[sl, :], vc[sl, :], z_ref[sl, :], ba_ref[sl, :], al_ref[...],
                             dt_ref[...], gain_ref[...], states[c])
            dq, dkk, dvv, dz, db, dal, ddt, dgn, dst = vjp((dy_ref[sl, :].astype(F32), dstate[...]))
            dqc[sl, :] = dq
            dkc[sl, :] = dkk
            dvc[sl, :] = dvv
            dz_ref[sl, :] = dz.astype(BF)
            dba_ref[sl, :] += db
            dal_ref[...] += dal
            ddt_ref[...] += ddt
            dgain_ref[...] += dgn
            dstate[...] = dst
            return carry

        lax.fori_loop(0, nchunks, bstep, 0)

        for x_ref, c_ref, dpost, dx_ref, dc_ref in ((q_ref, cq_ref, dqc, dq_ref, dcq_ref),
                                                    (k_ref, ck_ref, dkc, dk_ref, dck_ref),
                                                    (v_ref, cv_ref, dvc, dv_ref, dcv_ref)):
            _, vjp = jax.vjp(lambda x, *taps: _conv_silu(_shift_vjp, x, taps), x_ref[...], *_load_taps(c_ref))
            grads = vjp(dpost[...])
            dx_ref[...] = grads[0].astype(BF)
            for t in range(CONV_WIDTH):
                dc_ref[t:t + 1, :] = grads[1 + t]

    nh = GDN_HEADS
    col_bf = jax.ShapeDtypeStruct((s, nh * dk), BF)
    tap_out = jax.ShapeDtypeStruct((CONV_WIDTH, nh * dk), F32)
    tap_spec = pl.BlockSpec((CONV_WIDTH, dk), lambda h: (0, h))
    dq, dkk, dvv, dz, dba, dcq, dck, dcv, dal, ddt, dgain = pl.pallas_call(
        body, name="gdn_bwd", grid=(nh,),
        in_specs=[sp['q'], sp['k'], sp['v'], sp['z'], sp['ba'], sp['cq'], sp['ck'], sp['cv'], sp['vec'], sp['vec'],
                  sp['gain'], sp['head']],
        out_specs=[sp['head']] * 4 + [sp['ba'], tap_spec, tap_spec, tap_spec, sp['vec'], sp['vec'], sp['gain']],
        out_shape=[col_bf] * 4 + [jax.ShapeDtypeStruct((s, 128), F32), tap_out, tap_out, tap_out,
                                  jax.ShapeDtypeStruct((1, 128), F32), jax.ShapeDtypeStruct((1, 128), F32),
                                  jax.ShapeDtypeStruct((1, dk), F32)],
        scratch_shapes=[pltpu.VMEM((s, dk), F32)] * 6 + [pltpu.VMEM((nchunks, dk, dk), F32),
                                                        pltpu.VMEM((dk, dk), F32), pltpu.VMEM((dk, dk), F32)],
        compiler_params=_cparams(),
    )(proj, proj, proj, proj, ba, conv_w, conv_w, conv_w, a_log, dt_bias, gain, dy)
    return (jnp.concatenate([dq, dkk, dvv, dz], axis=1), dba, jnp.concatenate([dcq, dck, dcv], axis=1), dal, ddt,
            dgain)


def _lru_specs(s, bw):
    nb = LRU_BLOCKS
    return dict(
        xb=pl.BlockSpec((s, bw), lambda n: (0, n)), yb=pl.BlockSpec((s, bw), lambda n: (0, nb + n)),
        taps=pl.BlockSpec((CONV_WIDTH, bw), lambda n: (0, n)), vec=pl.BlockSpec((1, bw), lambda n: (0, n)),
        w=pl.BlockSpec((None, bw, bw), lambda n: (n, 0, 0)), b=pl.BlockSpec((None, 1, bw), lambda n: (n, 0, 0)),
        col=pl.BlockSpec((s, bw), lambda n: (0, n)))


def _lru_fwd(proj, conv_w, conv_b, lam, wr, br, wi, bi):
    s = proj.shape[0]
    bw = proj.shape[1] // (2 * LRU_BLOCKS)
    sp = _lru_specs(s, bw)

    def body(xb_ref, yb_ref, cw_ref, cb_ref, lam_ref, wr_ref, br_ref, wi_ref, bi_ref, y_ref):
        a, u, gy = _lru_pre(_mm_plain, _shift_rows, xb_ref[...], yb_ref[...], _load_taps(cw_ref), cb_ref[...],
                            lam_ref[...], wr_ref[...], br_ref[...], wi_ref[...], bi_ref[...])
        y_ref[...] = (_linscan(a, u, False) * gy).astype(BF)

    return pl.pallas_call(
        body, name="lru_fwd", grid=(LRU_BLOCKS,),
        in_specs=[sp['xb'], sp['yb'], sp['taps'], sp['vec'], sp['vec'], sp['w'], sp['b'], sp['w'], sp['b']],
        out_specs=sp['col'], out_shape=jax.ShapeDtypeStruct((s, LRU_BLOCKS * bw), BF),
        compiler_params=_cparams(),
    )(proj, proj, conv_w, conv_b, lam, wr, br, wi, bi)


def _lru_bwd(proj, conv_w, conv_b, lam, wr, br, wi, bi, dy):
    s = proj.shape[0]
    nb = LRU_BLOCKS
    bw = proj.shape[1] // (2 * nb)
    sp = _lru_specs(s, bw)

    def body(xb_ref, yb_ref, cw_ref, cb_ref, lam_ref, wr_ref, br_ref, wi_ref, bi_ref, dy_ref,
             dxb_ref, dyb_ref, dcw_ref, dcb_ref, dlam_ref, dwr_ref, dbr_ref, dwi_ref, dbi_ref):
        def pre(xb, yb, t0, t1, t2, t3, cb, lm, w_r, b_r, w_i, b_i):
            return _lru_pre(_mm_diff, _shift_vjp, xb, yb, (t0, t1, t2, t3), cb, lm, w_r, b_r, w_i, b_i)

        (a, u, gy), vjp = jax.vjp(pre, xb_ref[...], yb_ref[...], *_load_taps(cw_ref), cb_ref[...], lam_ref[...],
                                  wr_ref[...], br_ref[...], wi_ref[...], bi_ref[...])
        h = _linscan(a, u, False)
        dout = dy_ref[...].astype(F32)
        g = _linscan(_shift_rows(a, 1, True), dout * gy, True)
        grads = vjp((g * _shift_rows(h, 1, False), g, dout * h))
        dxb_ref[...] = grads[0].astype(BF)
        dyb_ref[...] = grads[1].astype(BF)
        for t in range(CONV_WIDTH):
            dcw_ref[t:t + 1, :] = grads[2 + t]
        dcb_ref[...] = grads[6]
        dlam_ref[...] = grads[7]
        dwr_ref[...] = grads[8]
        dbr_ref[...] = grads[9]
        dwi_ref[...] = grads[10]
        dbi_ref[...] = grads[11]

    outs = pl.pallas_call(
        body, name="lru_bwd", grid=(nb,),
        in_specs=[sp['xb'], sp['yb'], sp['taps'], sp['vec'], sp['vec'], sp['w'], sp['b'], sp['w'], sp['b'], sp['col']],
        out_specs=[sp['col'], sp['col'], sp['taps'], sp['vec'], sp['vec'], sp['w'], sp['b'], sp['w'], sp['b']],
        out_shape=[jax.ShapeDtypeStruct((s, nb * bw), BF), jax.ShapeDtypeStruct((s, nb * bw), BF),
                   jax.ShapeDtypeStruct((CONV_WIDTH, nb * bw), F32), jax.ShapeDtypeStruct((1, nb * bw), F32),
                   jax.ShapeDtypeStruct((1, nb * bw), F32), jax.ShapeDtypeStruct((nb, bw, bw), F32),
                   jax.ShapeDtypeStruct((nb, 1, bw), F32), jax.ShapeDtypeStruct((nb, bw, bw), F32),
                   jax.ShapeDtypeStruct((nb, 1, bw), F32)],
        compiler_params=_cparams(),
    )(proj, proj, conv_w, conv_b, lam, wr, br, wi, bi, dy)
    return (jnp.concatenate([outs[0], outs[1]], axis=1),) + tuple(outs[2:])


def _row_tile(rows, cols, bytes_per_row_set):
    t = 8
    while t * 2 <= rows and rows % (t * 2) == 0 and (t * 2) * cols * bytes_per_row_set <= 4 * 1024 * 1024:
        t *= 2
    return t


def _sum_slabs(a, out_dtype, name):
    n, rows, cols = a.shape
    tr = _row_tile(rows, cols, 4 * (n + 1))

    def body(*refs):
        acc = refs[0][...].astype(F32)
        for r in refs[1:n]:
            acc = acc + r[...].astype(F32)
        refs[n][...] = acc.astype(out_dtype)

    specs = [pl.BlockSpec((None, tr, cols), functools.partial(lambda i, k: (k, i, 0), k=k)) for k in range(n)]
    return pl.pallas_call(
        body, name=name, grid=(rows // tr,), in_specs=specs, out_specs=pl.BlockSpec((tr, cols), lambda i: (i, 0)),
        out_shape=jax.ShapeDtypeStruct((rows, cols), out_dtype), compiler_params=_cparams(),
    )(*([a] * n))


def _adamw(w, g, m, v, name):
    rows, cols = w.shape
    tr = _row_tile(rows, cols, 4 * 7)
    c1 = 1.0 / (1.0 - ADAM_B1 ** ADAM_STEP)
    c2 = 1.0 / (1.0 - ADAM_B2 ** ADAM_STEP)

    def body(w_ref, g_ref, m_ref, v_ref, d_ref, nm_ref, nv_ref):
        gv = g_ref[...]
        nm = ADAM_B1 * m_ref[...] + (1.0 - ADAM_B1) * gv
        nv = ADAM_B2 * v_ref[...] + (1.0 - ADAM_B2) * (gv * gv)
        d_ref[...] = -ADAM_LR * ((nm * c1) / (jnp.sqrt(nv * c2) + ADAM_EPS) + ADAM_WD * w_ref[...])
        nm_ref[...] = nm
        nv_ref[...] = nv

    spec = pl.BlockSpec((tr, cols), lambda i: (i, 0))
    shape = jax.ShapeDtypeStruct((rows, cols), F32)
    return pl.pallas_call(
        body, name=name, grid=(rows // tr,), in_specs=[spec] * 4, out_specs=[spec] * 3, out_shape=[shape] * 3,
        compiler_params=_cparams(),
    )(w, g, m, v)


_ANY = pl.BlockSpec(memory_space=pl.ANY)


def _mesh_pos():
    return lax.axis_index("x"), lax.axis_index("y"), lax.axis_index("c")


def _remote(src, dst, send_sems, recv_sems, k, dev):
    return pltpu.make_async_remote_copy(src_ref=src, dst_ref=dst, send_sem=send_sems.at[k], recv_sem=recv_sems.at[k],
                                        device_id=dev, device_id_type=MESH)


def _all_gather_chips(shards, split, name):
    n = len(shards)

    def body(*refs):
        ins, outs = refs[:n], refs[n:2 * n]
        send_sems, recv_sems, local_sems = refs[2 * n:]
        x, y, c = _mesh_pos()
        me = 2 * x + y
        sibling = (x, y, 1 - c)
        chips = [(1 - x, y), (x, 1 - y), (1 - x, 1 - y)]

        def half(i, which):
            hr = ins[i].shape[0] // 2
            return pl.ds(which * hr, hr)

        local = [pltpu.make_async_copy(ins[i], outs[i].at[me], local_sems.at[i]) for i in range(n)]
        for cp in local:
            cp.start()
        first = []
        for i in range(n):
            src = ins[i].at[half(i, c)] if split[i] else ins[i]
            dst = outs[i].at[me, half(i, c)] if split[i] else outs[i].at[me]
            for j, chip in enumerate(chips):
                first.append(_remote(src, dst, send_sems, recv_sems, 6 * i + j, (*chip, c)))
        for cp in first:
            cp.start()
        passed = []
        for i in range(n):
            for j, chip in enumerate(chips):
                pj = 2 * chip[0] + chip[1]
                blk = outs[i].at[pj, half(i, c)] if split[i] else outs[i].at[pj]
                _remote(blk, blk, send_sems, recv_sems, 6 * i + j, (*chip, c)).wait_recv()
                if split[i]:
                    fwd = _remote(blk, blk, send_sems, recv_sems, 6 * i + 3 + j, sibling)
                    fwd.start()
                    passed.append(fwd)
        for i in range(n):
            if split[i]:
                for j, chip in enumerate(chips):
                    blk = outs[i].at[2 * chip[0] + chip[1], half(i, 1 - c)]
                    _remote(blk, blk, send_sems, recv_sems, 6 * i + 3 + j, sibling).wait_recv()
        for cp in first + passed:
            cp.wait_send()
        for cp in local:
            cp.wait()

    return pl.pallas_call(
        body, name=name, in_specs=[_ANY] * n, out_specs=[_ANY] * n,
        out_shape=[jax.ShapeDtypeStruct((N_CHIPS,) + s.shape, s.dtype) for s in shards],
        scratch_shapes=[pltpu.SemaphoreType.DMA((6 * n,)), pltpu.SemaphoreType.DMA((6 * n,)),
                        pltpu.SemaphoreType.DMA((n,))],
    )(*shards)


def _all_gather_devices(part, name):
    def body(in_ref, out_ref, send_sems, recv_sems, local_sem):
        x, y, c = _mesh_pos()
        me = 4 * x + 2 * y + c
        local = pltpu.make_async_copy(in_ref, out_ref.at[me], local_sem)
        local.start()
        peers = [(x ^ (k >> 2), y ^ ((k >> 1) & 1), c ^ (k & 1)) for k in range(1, N_DEV)]
        sends = [_remote(in_ref, out_ref.at[me], send_sems, recv_sems, k, p) for k, p in enumerate(peers)]
        for cp in sends:
            cp.start()
        for k, p in enumerate(peers):
            blk = out_ref.at[4 * p[0] + 2 * p[1] + p[2]]
            _remote(blk, blk, send_sems, recv_sems, k, p).wait_recv()
        for cp in sends:
            cp.wait_send()
        local.wait()

    return pl.pallas_call(
        body, name=name, in_specs=[_ANY], out_specs=_ANY,
        out_shape=jax.ShapeDtypeStruct((N_DEV,) + part.shape, part.dtype),
        scratch_shapes=[pltpu.SemaphoreType.DMA((N_DEV - 1,)), pltpu.SemaphoreType.DMA((N_DEV - 1,)),
                        pltpu.SemaphoreType.DMA],
    )(part)


def _sibling_swap_halves(grads, name):
    n = len(grads)

    def body(*refs):
        ins, outs = refs[:n], refs[n:2 * n]
        send_sems, recv_sems, local_sems = refs[2 * n:]
        x, y, c = _mesh_pos()
        copies = []
        for i in range(n):
            hr = ins[i].shape[1] // 2
            keep, give = pl.ds(c * hr, hr), pl.ds((1 - c) * hr, hr)
            copies.append(pltpu.make_async_copy(ins[i].at[:, keep], outs[i].at[0], local_sems.at[i]))
            copies.append(_remote(ins[i].at[:, give], outs[i].at[1], send_sems, recv_sems, i, (x, y, 1 - c)))
        for cp in copies:
            cp.start()
        for cp in copies:
            cp.wait()

    return pl.pallas_call(
        body, name=name, in_specs=[_ANY] * n, out_specs=[_ANY] * n,
        out_shape=[jax.ShapeDtypeStruct((2, g.shape[0], g.shape[1] // 2, g.shape[2]), g.dtype) for g in grads],
        scratch_shapes=[pltpu.SemaphoreType.DMA((n,)), pltpu.SemaphoreType.DMA((n,)), pltpu.SemaphoreType.DMA((n,))],
    )(*grads)


def _chip_scatter(parts, name):
    n = len(parts)

    def body(*refs):
        ins, outs = refs[:n], refs[n:2 * n]
        send_sems, recv_sems, local_sems = refs[2 * n:]
        x, y, c = _mesh_pos()
        chips = [(1 - x, y), (x, 1 - y), (1 - x, 1 - y)]
        copies = []
        for i in range(n):
            copies.append(pltpu.make_async_copy(ins[i].at[2 * x + y], outs[i].at[3], local_sems.at[i]))
            for j, chip in enumerate(chips):
                copies.append(_remote(ins[i].at[2 * chip[0] + chip[1]], outs[i].at[j], send_sems, recv_sems, 3 * i + j,
                                      (*chip, c)))
        for cp in copies:
            cp.start()
        for cp in copies:
            cp.wait()

    return pl.pallas_call(
        body, name=name, in_specs=[_ANY] * n, out_specs=[_ANY] * n,
        out_shape=[jax.ShapeDtypeStruct(p.shape, p.dtype) for p in parts],
        scratch_shapes=[pltpu.SemaphoreType.DMA((3 * n,)), pltpu.SemaphoreType.DMA((3 * n,)),
                        pltpu.SemaphoreType.DMA((n,))],
    )(*parts)


def _sibling_join_halves(halves, name):
    n = len(halves)

    def body(*refs):
        ins, outs = refs[:n], refs[n:2 * n]
        send_sems, recv_sems, local_sems = refs[2 * n:]
        x, y, c = _mesh_pos()
        copies = []
        for i in range(n):
            hr = ins[i].shape[0]
            mine = outs[i].at[pl.ds(c * hr, hr)]
            copies.append(pltpu.make_async_copy(ins[i], mine, local_sems.at[i]))
            copies.append(_remote(ins[i], mine, send_sems, recv_sems, i, (x, y, 1 - c)))
        for cp in copies:
            cp.start()
        for i in range(n):
            hr = ins[i].shape[0]
            copies[2 * i].wait()
            copies[2 * i + 1].wait_send()
            theirs = outs[i].at[pl.ds((1 - c) * hr, hr)]
            _remote(theirs, theirs, send_sems, recv_sems, i, (x, y, 1 - c)).wait_recv()

    return pl.pallas_call(
        body, name=name, in_specs=[_ANY] * n, out_specs=[_ANY] * n,
        out_shape=[jax.ShapeDtypeStruct((2 * h.shape[0], h.shape[1]), h.dtype) for h in halves],
        scratch_shapes=[pltpu.SemaphoreType.DMA((n,)), pltpu.SemaphoreType.DMA((n,)), pltpu.SemaphoreType.DMA((n,))],
    )(*halves)


def _reduce_scatter(grads, tag):
    swapped = _sibling_swap_halves(grads, f"rs_swap_{tag}")
    parts = []
    for i, a in enumerate(swapped):
        _, four, hr, cols = a.shape
        parts.append(_sum_slabs(a.reshape(2, four * hr, cols), BF, f"rs_add2_{tag}_{i}").reshape(four, hr, cols))
    landed = _chip_scatter(parts, f"rs_scatter_{tag}")
    halves = [_sum_slabs(a, F32, f"rs_add4_{tag}_{i}") for i, a in enumerate(landed)]
    return _sibling_join_halves(halves, f"rs_join_{tag}")


def _pad_cols(a, width=128):
    return jnp.pad(a, ((0, 0), (0, width - a.shape[1])))


def _mixer_forward(kind, hn, w, tables):
    if kind == 0:
        proj = _matmul(hn, w['ret_w_in'], name="ret_proj", b_sharded=True)
        og, states = _ret_fwd(proj, w['ret_gn_gain'], tables)
        return og, (proj, states)
    if kind == 1:
        proj = _matmul(hn, w['gdn_w_main'], name="gdn_proj")
        ba = _matmul(hn, w['gdn_w_small'], name="gdn_proj_ba")
        og = _gdn_fwd(proj, ba, w['gdn_conv_w'], w['gdn_a_log'], w['gdn_dt_bias'], w['gdn_norm_gain'])
        return og, (proj, ba)
    if kind == 2:
        proj = _matmul(hn, w['gla_w_main'], name="gla_proj")
        glow = _matmul(hn, w['gla_w_small'], name="gla_proj_gate")
        og, states = _gla_fwd(proj, glow, w['gla_w_gate_up'], w['gla_gate_bias'], w['gla_norm_gain'])
        return og, (proj, glow, states)
    proj = _matmul(hn, w['lru_w_in'], name="lru_proj", b_sharded=True)
    og = _lru_fwd(proj, w['lru_conv_w'], w['lru_conv_b'], w['lru_lambda'], w['lru_w_rgate'], w['lru_b_rgate'],
                  w['lru_w_igate'], w['lru_b_igate'])
    return og, (proj,)


def _mixer_backward(kind, hn, w, tables, saved, d_og, grads):
    d = hn.shape[1]
    if kind == 0:
        proj, states = saved
        d_proj, grads['ret_gn_gain'] = _ret_bwd(proj, w['ret_gn_gain'], tables, states, d_og)
        grads['ret_w_in'] = _matmul(hn, d_proj, name="ret_dw_in", ta=True, out_dtype=BF, o_sharded=True)
        return _matmul(d_proj, w['ret_w_in'], name="ret_dhn", tb=True, b_sharded=True)
    if kind == 1:
        proj, ba = saved
        d_proj, d_ba, grads['gdn_conv_w'], grads['gdn_a_log'], grads['gdn_dt_bias'], grads['gdn_norm_gain'] = _gdn_bwd(
            proj, ba, w['gdn_conv_w'], w['gdn_a_log'], w['gdn_dt_bias'], w['gdn_norm_gain'], d_og)
        d_ba = d_ba.astype(BF)
        dw_main = _matmul(hn, d_proj, name="gdn_dw_main", ta=True, out_dtype=BF)
        dw_small = _matmul(hn, d_ba, name="gdn_dw_small", ta=True, out_dtype=BF)
        dw = jnp.concatenate([dw_main, dw_small[:, :2 * GDN_HEADS]], axis=1)
        grads['gdn_w_in'] = dw.reshape(d, N_CHIPS, dw.shape[1] // N_CHIPS).transpose(1, 0, 2)
        d_hn = _matmul(d_proj, w['gdn_w_main'], name="gdn_dhn_main", tb=True)
        return _matmul(d_ba, w['gdn_w_small'], name="gdn_dhn_small", tb=True, epilogue='add', extra=d_hn)
    if kind == 2:
        proj, glow, states = saved
        d_proj, d_glow4, d_wgu, grads['gla_gate_bias'], grads['gla_norm_gain'] = _gla_bwd(
            proj, glow, w['gla_w_gate_up'], w['gla_gate_bias'], w['gla_norm_gain'], states, d_og)
        grads['gla_w_gate_up'] = d_wgu[:GLA_RANK]
        dw_main = _matmul(hn, d_proj, name="gla_dw_main", ta=True, out_dtype=BF)
        dw_small4 = _matmul(hn, d_glow4, name="gla_dw_small", ta=True, out_dtype=F32)
        dw_small = dw_small4.reshape(d, GLA_HEADS, 128)[:, :, :GLA_RANK].sum(axis=1).astype(BF)
        dw = jnp.concatenate([dw_main, dw_small], axis=1)
        grads['gla_w_in'] = dw.reshape(d, N_CHIPS, dw.shape[1] // N_CHIPS).transpose(1, 0, 2)
        d_hn = _matmul(d_proj, w['gla_w_main'], name="gla_dhn_main", tb=True)
        w_small4 = jnp.tile(w['gla_w_small'], (1, GLA_HEADS))
        return _matmul(d_glow4, w_small4, name="gla_dhn_small", tb=True, epilogue='add', extra=d_hn)
    (proj,) = saved
    (d_proj, grads['lru_conv_w'], grads['lru_conv_b'], grads['lru_lambda'], grads['lru_w_rgate'], grads['lru_b_rgate'],
     grads['lru_w_igate'], grads['lru_b_igate']) = _lru_bwd(
        proj, w['lru_conv_w'], w['lru_conv_b'], w['lru_lambda'], w['lru_w_rgate'], w['lru_b_rgate'], w['lru_w_igate'],
        w['lru_b_igate'], d_og)
    grads['lru_w_in'] = _matmul(hn, d_proj, name="lru_dw_in", ta=True, out_dtype=BF, o_sharded=True)
    return _matmul(d_proj, w['lru_w_in'], name="lru_dhn", tb=True, b_sharded=True)


_W_OUT = ('ret_w_out', 'gdn_w_out', 'gla_w_out', 'lru_w_out')
_W_IN = ('ret_w_in', 'gdn_w_in', 'gla_w_in', 'lru_w_in')


def _layer_forward(layer, x, w, tables):
    hn = _rmsnorm_fwd(x, w['norm1'][layer], f"norm1_fwd_{layer}")
    og, mixer_saved = _mixer_forward(layer, hn, w, tables)
    x1 = _matmul(og, w[_W_OUT[layer]], name=f"mixer_out_{layer}", epilogue='add', extra=x)
    hn2 = _rmsnorm_fwd(x1, w['norm2'][layer], f"norm2_fwd_{layer}")
    act = _matmul(hn2, w['mlp_w_up'][layer], name="mlp_up", b_sharded=True, epilogue='relu2', out_dtype=BF)
    x2 = _matmul(act, w['mlp_w_down'][layer], name="mlp_down", epilogue='add', extra=x1)
    return x2, (x, hn, mixer_saved, og, x1, hn2, act)


def _layer_backward(layer, dx2, w, tables, saved):
    x, hn, mixer_saved, og, x1, hn2, act = saved
    d = x.shape[1]
    grads = {}
    d_up = _matmul(dx2, w['mlp_w_down'][layer], name="mlp_d_up", tb=True, epilogue='dact', extra=act, out_dtype=BF)
    dw_down = _matmul(act, dx2, name="mlp_dw_down", ta=True, out_dtype=BF)
    grads['mlp_w_down'] = dw_down.reshape(N_CHIPS, dw_down.shape[0] // N_CHIPS, d)
    grads['mlp_w_up'] = _matmul(hn2, d_up, name="mlp_dw_up", ta=True, out_dtype=BF, o_sharded=True)
    d_hn2 = _matmul(d_up, w['mlp_w_up'][layer], name="mlp_d_hn", tb=True, b_sharded=True)
    dx1, grads['norm2'] = _rmsnorm_bwd(x1, w['norm2'][layer], d_hn2, dx2, f"norm2_bwd_{layer}")
    w_out = w[_W_OUT[layer]]
    d_og = _matmul(dx1, w_out, name=f"mixer_d_og_{layer}", tb=True, out_dtype=BF)
    dw_out = _matmul(og, dx1, name=f"mixer_dw_out_{layer}", ta=True, out_dtype=BF)
    grads[_W_OUT[layer]] = dw_out.reshape(N_CHIPS, dw_out.shape[0] // N_CHIPS, d)
    d_hn = _mixer_backward(layer, hn, w, tables, mixer_saved, d_og, grads)
    dx, grads['norm1'] = _rmsnorm_bwd(x, w['norm1'][layer], d_hn, dx1, f"norm1_bwd_{layer}")
    return dx, grads


def _pack(arrays):
    flat = []
    for a in arrays:
        v = a.astype(F32).reshape(-1)
        v = jnp.pad(v, (0, (-v.shape[0]) % 128))
        flat.append(v.reshape(-1, 128))
    buf = jnp.concatenate(flat, axis=0)
    return jnp.pad(buf, ((0, (-buf.shape[0]) % 8), (0, 0)))


def _unpack(buf, shapes):
    lead = buf.shape[:-2]
    out, off = [], 0
    for shp in shapes:
        n = math.prod(shp)
        rows = -(-n // 128)
        piece = buf[..., off:off + rows, :].reshape(lead + (rows * 128,))[..., :n]
        out.append(piece.reshape(lead + tuple(shp)))
        off += rows
    return out


_WEIGHTS = ('norm1', 'norm2', 'final_norm', 'ret_w_in', 'ret_gn_gain', 'ret_w_out', 'gdn_w_in', 'gdn_conv_w',
            'gdn_a_log', 'gdn_dt_bias', 'gdn_norm_gain', 'gdn_w_out', 'gla_w_in', 'gla_w_gate_up', 'gla_gate_bias',
            'gla_norm_gain', 'gla_w_out', 'lru_w_in', 'lru_conv_w', 'lru_conv_b', 'lru_w_rgate', 'lru_b_rgate',
            'lru_w_igate', 'lru_b_igate', 'lru_lambda', 'lru_w_out', 'mlp_w_up', 'mlp_w_down')
_FWD_PARAMS = ('x',) + _WEIGHTS
_BIG = ('ret_w_in', 'ret_w_out', 'gdn_w_in', 'gdn_w_out', 'gla_w_in', 'gla_w_out', 'lru_w_in', 'lru_w_out',
        'mlp_w_up', 'mlp_w_down')
_SMALL = tuple(n for n in _WEIGHTS if n not in _BIG)
_SMALL_SHARDED = ('ret_gn_gain', 'gdn_conv_w', 'gla_w_gate_up', 'gla_gate_bias', 'gla_norm_gain', 'lru_conv_w',
                  'lru_conv_b', 'lru_lambda')


def kernel(*args):
    names = _FWD_PARAMS + ('loss_target',) + tuple('m_' + n for n in _WEIGHTS) + tuple('v_' + n for n in _WEIGHTS)
    assert len(args) == len(names)
    a = dict(zip(names, args))
    x = a['x'][0]
    target = a['loss_target'][0]
    s, d = x.shape
    chip = 2 * lax.axis_index("x") + lax.axis_index("y")

    small_local = [a[n][0] if a[n].ndim == 3 else a[n] for n in _SMALL_SHARDED]
    small_pack = _pack(small_local)
    gathered = {}
    for layer in range(4):
        w_in, w_out = _W_IN[layer], _W_OUT[layer]
        ops = [a[w_in][0].astype(BF), a[w_out][0].astype(BF), a['mlp_w_up'][layer].astype(BF),
               a['mlp_w_down'][layer].astype(BF)]
        split = [True] * 4
        if layer == 0:
            ops.append(small_pack)
            split.append(False)
        res = _all_gather_chips(ops, split, f"gather_weights_{layer}")
        gathered[w_in], gathered[w_out] = res[0], res[1]
        gathered[('mlp_w_up', layer)], gathered[('mlp_w_down', layer)] = res[2], res[3]
        if layer == 0:
            small_all = res[4]

    def whole_cols(g):
        return g.transpose(1, 0, 2).reshape(g.shape[1], N_CHIPS * g.shape[2])

    def whole_rows(g):
        return g.reshape(N_CHIPS * g.shape[1], g.shape[2])

    w = {}
    for n, piece in zip(_SMALL_SHARDED, _unpack(small_all, [p.shape for p in small_local])):
        w[n] = whole_cols(piece)
    for n in _SMALL:
        if n not in _SMALL_SHARDED:
            w[n] = a[n][0] if n.startswith('lru_') else a[n]
    w['gdn_a_log'], w['gdn_dt_bias'] = _pad_cols(w['gdn_a_log']), _pad_cols(w['gdn_dt_bias'])
    w['gla_w_gate_up'] = jnp.pad(w['gla_w_gate_up'], ((0, 128 - GLA_RANK), (0, 0)))
    nb = LRU_BLOCKS
    w['lru_b_rgate'] = w['lru_b_rgate'].reshape(nb, 1, -1)
    w['lru_b_igate'] = w['lru_b_igate'].reshape(nb, 1, -1)
    w['ret_w_in'] = gathered['ret_w_in']
    w['lru_w_in'] = gathered['lru_w_in']
    gdn_full = whole_cols(gathered['gdn_w_in'])
    n_gdn = gdn_full.shape[1] - 2 * GDN_HEADS
    w['gdn_w_main'], w['gdn_w_small'] = gdn_full[:, :n_gdn], _pad_cols(gdn_full[:, n_gdn:])
    gla_full = whole_cols(gathered['gla_w_in'])
    n_gla = gla_full.shape[1] - GLA_RANK
    w['gla_w_main'], w['gla_w_small'] = gla_full[:, :n_gla], _pad_cols(gla_full[:, n_gla:])
    for n in _W_OUT:
        w[n] = whole_rows(gathered[n])
    w['mlp_w_up'] = [gathered[('mlp_w_up', layer)] for layer in range(4)]
    w['mlp_w_down'] = [whole_rows(gathered[('mlp_w_down', layer)]) for layer in range(4)]

    tables = _ret_tables(s, d // RET_HEADS)
    saved = []
    h = x
    for layer in range(4):
        h, sv = _layer_forward(layer, h, w, tables)
        saved.append(sv)
    loss_part, dh, g_final = _loss_head(h, w['final_norm'], target)
    loss = lax.psum(loss_part, ("x", "y", "c"))

    grad, delta, new_m, new_v = {}, {}, {}, {}
    small_grads = {'final_norm': g_final}
    norm_grads = {'norm1': [None] * 4, 'norm2': [None] * 4}
    mlp_grads = {'mlp_w_up': [None] * 4, 'mlp_w_down': [None] * 4}
    for layer in reversed(range(4)):
        dh, g = _layer_backward(layer, dh, w, tables, saved[layer])
        saved[layer] = None
        w_in, w_out = _W_IN[layer], _W_OUT[layer]
        red = _reduce_scatter([g[w_in], g[w_out], g['mlp_w_up'], g['mlp_w_down']], str(layer))
        for n, gr in ((w_in, red[0]), (w_out, red[1])):
            grad[n] = gr
            delta[n], new_m[n], new_v[n] = _adamw(a[n][0], gr, a['m_' + n][0], a['v_' + n][0], f"adamw_{n}")
        mlp_grads['mlp_w_up'][layer], mlp_grads['mlp_w_down'][layer] = red[2], red[3]
        norm_grads['norm1'][layer], norm_grads['norm2'][layer] = g['norm1'], g['norm2']
        for n in _SMALL:
            if n in g:
                small_grads[n] = g[n]
    for n in ('mlp_w_up', 'mlp_w_down'):
        gr = jnp.concatenate(mlp_grads[n], axis=0)
        flat = lambda t: t.reshape(gr.shape)
        grad[n] = gr
        delta[n], new_m[n], new_v[n] = _adamw(flat(a[n]), gr, flat(a['m_' + n]), flat(a['v_' + n]), f"adamw_{n}")
    small_grads['norm1'] = jnp.stack(norm_grads['norm1'])
    small_grads['norm2'] = jnp.stack(norm_grads['norm2'])

    full_shapes = [small_grads[n].shape for n in _SMALL]
    total = _sum_slabs(_all_gather_devices(_pack([small_grads[n] for n in _SMALL]), "gather_small_grads"), F32,
                       "sum_small_grads")
    local_g = {}
    for n, full in zip(_SMALL, _unpack(total, full_shapes)):
        shp = a[n].shape
        if n in _SMALL_SHARDED:
            full = full.reshape(full.shape[0], -1)
            cq = shp[-1]
            full = lax.dynamic_slice_in_dim(full, chip * cq, cq, axis=1)
        elif n in ('gdn_a_log', 'gdn_dt_bias'):
            full = full[:, :shp[-1]]
        local_g[n] = full.reshape(shp)
    shapes = [a[n].shape for n in _SMALL]
    packed = [_pack([src[n] for n in _SMALL]) for src in
              (a, local_g, {n: a['m_' + n] for n in _SMALL}, {n: a['v_' + n] for n in _SMALL})]
    upd = _adamw(*packed, "adamw_small")
    for n, gr, dl, nm, nv in zip(_SMALL, _unpack(packed[1], shapes), *[_unpack(u, shapes) for u in upd]):
        grad[n], delta[n], new_m[n], new_v[n] = gr, dl, nm, nv

    out = [loss, dh.reshape(a['x'].shape)]
    for group in (grad, delta, new_m, new_v):
        out += [group[n].reshape(a[n].shape) for n in _WEIGHTS]
    return tuple(out)
```

```python
import functools
import math

import jax
import jax.numpy as jnp
from jax import lax
from jax.experimental import pallas as pl
from jax.experimental.pallas import tpu as pltpu

F32 = jnp.float32
BF = jnp.bfloat16
MESH = pl.DeviceIdType.MESH

NORM_EPS = 1e-6
CHUNK = 64
RET_HEADS = 8
GDN_HEADS = 16
GLA_HEADS = 4
GLA_RANK = 16
GLA_TAU = 16.0
LRU_BLOCKS = 16
LRU_C = 8.0
CONV_WIDTH = 4
ROPE_BASE = 10000.0
N_CHIPS = 4
N_DEV = 8

ADAM_LR = 0.001
ADAM_B1 = 0.9
ADAM_B2 = 0.999
ADAM_EPS = 1e-08
ADAM_WD = 0.01
ADAM_STEP = 10

VMEM_LIMIT_BYTES = 56 * 1024 * 1024
TOKEN_BLOCK = 256
ROW_BLOCK = 256
GDN_UNROLL = 2


def _cparams(**kw):
    return pltpu.CompilerParams(vmem_limit_bytes=VMEM_LIMIT_BYTES, **kw)


def _raw_mm(a, b, ta, tb, hi):
    if hi:
        a, b, prec = a.astype(F32), b.astype(F32), lax.Precision.HIGHEST
    else:
        a, b, prec = a.astype(BF), b.astype(BF), None
    dims = (((0 if ta else 1,), (1 if tb else 0,)), ((), ()))
    return lax.dot_general(a, b, dims, precision=prec, preferred_element_type=F32)


@functools.partial(jax.custom_vjp, nondiff_argnums=(2, 3, 4))
def _mm_vjp(a, b, ta, tb, hi):
    return _raw_mm(a, b, ta, tb, hi)


def _mm_vjp_fwd(a, b, ta, tb, hi):
    return _raw_mm(a, b, ta, tb, hi), (a, b)


def _mm_vjp_bwd(ta, tb, hi, res, g):
    a, b = res
    da = _raw_mm(b, g, tb, True, hi) if ta else _raw_mm(g, b, False, not tb, hi)
    db = _raw_mm(g, a, True, ta, hi) if tb else _raw_mm(a, g, not ta, False, hi)
    return da, db


_mm_vjp.defvjp(_mm_vjp_fwd, _mm_vjp_bwd)


def _mm_diff(a, b, ta=False, tb=False, hi=False):
    return _mm_vjp(a, b, ta, tb, hi)


def _mm_plain(a, b, ta=False, tb=False, hi=False):
    return _raw_mm(a, b, ta, tb, hi)


def _shift_rows(x, k, up):
    if k == 0:
        return x
    n = x.shape[0]
    rows = lax.broadcasted_iota(jnp.int32, x.shape, 0)
    if up:
        return jnp.where(rows < n - k, pltpu.roll(x, n - k, 0), 0.0)
    return jnp.where(rows >= k, pltpu.roll(x, k, 0), 0.0)


@functools.partial(jax.custom_vjp, nondiff_argnums=(1, 2))
def _shift_vjp(x, k, up):
    return _shift_rows(x, k, up)


def _shift_vjp_fwd(x, k, up):
    return _shift_rows(x, k, up), None


def _shift_vjp_bwd(k, up, _, g):
    return (_shift_rows(g, k, not up),)


_shift_vjp.defvjp(_shift_vjp_fwd, _shift_vjp_bwd)


def _sigmoid(x):
    return 1.0 / (1.0 + jnp.exp(-x))


def _silu(x):
    return x * _sigmoid(x)


def _softplus(x):
    return jnp.maximum(x, 0.0) + jnp.log(1.0 + jnp.exp(-jnp.abs(x)))


def _gelu_tanh(x):
    return 0.5 * x * (1.0 + jnp.tanh(math.sqrt(2.0 / math.pi) * (x + 0.044715 * (x * x * x))))


def _expm1(x):
    series = x * (1.0 + x * (0.5 + x * (1.0 / 6.0 + x * (1.0 / 24.0))))
    return jnp.where(jnp.abs(x) < 0.03, series, jnp.exp(x) - 1.0)


def _rmsnorm(x, g):
    return x * lax.rsqrt(jnp.mean(x * x, axis=-1, keepdims=True) + NORM_EPS) * g


def _head_norm(o, gain, center):
    if center:
        o = o - jnp.mean(o, axis=-1, keepdims=True)
    return o * lax.rsqrt(jnp.mean(o * o, axis=-1, keepdims=True) + NORM_EPS) * gain


def _l2norm(x):
    return x * lax.rsqrt(jnp.sum(x * x, axis=-1, keepdims=True) + NORM_EPS)


def _iota2(shape, dim):
    return lax.broadcasted_iota(jnp.int32, shape, dim)


def _tri_ones(n, upper=False):
    i, j = _iota2((n, n), 0), _iota2((n, n), 1)
    return jnp.where((i <= j) if upper else (j <= i), 1.0, 0.0).astype(F32)


def _linscan(a, u, rev):
    n = a.shape[0]
    rows = _iota2(a.shape, 0)
    d = 1
    while d < n:
        if rev:
            valid = rows < n - d
            a_s, u_s = pltpu.roll(a, n - d, 0), pltpu.roll(u, n - d, 0)
        else:
            valid = rows >= d
            a_s, u_s = pltpu.roll(a, d, 0), pltpu.roll(u, d, 0)
        u = a * jnp.where(valid, u_s, 0.0) + u
        a = a * jnp.where(valid, a_s, 1.0)
        d *= 2
    return u


def _ret_chunk(mm, q, k, v, g, gain, state, cos, sin, dintra, qdec, kdec, cdec):
    dk = q.shape[-1]
    half = dk // 2

    def rot(t):
        t1, t2 = t[:, :half], t[:, half:]
        return jnp.concatenate([t1 * cos - t2 * sin, t1 * sin + t2 * cos], axis=-1)

    qr = rot(q)
    kr = rot(k) * (dk ** -0.5)
    scores = mm(qr, kr, tb=True) * dintra
    o = mm(scores, v) + mm(qr * qdec, state)
    new_state = state * cdec + mm(kr * kdec, v, ta=True)
    y = _head_norm(o, gain, True) * _silu(g)
    return y, new_state


def _gla_chunk(mm, q, k, v, r, glow, wgu, bias, gain, state_t):
    dk = q.shape[-1]
    c = q.shape[0]
    logit = mm(glow, wgu) + bias
    la = -_softplus(-logit) * (1.0 / GLA_TAU)
    cum = mm(_tri_ones(c), la, hi=True)
    rows = _iota2(la.shape, 0)
    ref = jnp.sum(jnp.where(rows < c // 2, la, 0.0), axis=0, keepdims=True)
    tot = jnp.sum(la, axis=0, keepdims=True)
    fwd, bwd = jnp.exp(cum - ref), jnp.exp(ref - cum)
    qs = q * (dk ** -0.5)
    s_lo = mm(qs * fwd, k * bwd, tb=True)
    s_up = mm(qs * bwd, k * fwd, tb=True)
    i, j = _iota2((c, c), 0), _iota2((c, c), 1)
    scores = jnp.where(i >= j, s_lo, s_up)
    o = mm(scores, v) + mm(qs * jnp.exp(cum), state_t, tb=True)
    k_end = k * jnp.exp(tot - cum)
    new_state_t = state_t * jnp.exp(tot) + mm(v, k_end, ta=True)
    y = _head_norm(o, gain, False) * _silu(r)
    return y, new_state_t


def _gdn_chunk(mm, q, k, v, z, ba, a_log, dt_bias, gain, state, head):
    c, dk = q.shape
    lanes = _iota2((1, ba.shape[-1]), 1)
    oh_b = jnp.where(lanes == head, 1.0, 0.0).astype(F32)
    oh_a = jnp.where(lanes == head + GDN_HEADS, 1.0, 0.0).astype(F32)
    beta = _sigmoid(jnp.sum(ba * oh_b, axis=-1, keepdims=True))
    a_logit = jnp.sum(ba * oh_a, axis=-1, keepdims=True)
    a_h = jnp.sum(a_log * oh_b, axis=-1, keepdims=True)
    dt_h = jnp.sum(dt_bias * oh_b, axis=-1, keepdims=True)
    la = -jnp.exp(a_h) * _softplus(a_logit + dt_h)
    q = _l2norm(q) * (dk ** -0.5)
    k = _l2norm(k)
    wide = max(c, dk)
    cum_w = mm(_tri_ones(c), la * jnp.ones((c, wide), F32), hi=True)
    cum_k, cum_i = cum_w[:, :dk], cum_w[:, :c]
    cum_j = cum_i.T
    i, j = _iota2((c, c), 0), _iota2((c, c), 1)
    strict = i > j
    rel = jnp.where(strict, jnp.exp(jnp.where(strict, cum_i - cum_j, 0.0)), 0.0)
    a_mat = beta * rel * mm(k, k, tb=True)
    eye_c = jnp.where(i == j, 1.0, 0.0).astype(F32)
    inv = eye_c - a_mat
    power = a_mat
    for _ in range(int(math.log2(c)) - 1):
        power = mm(power, power, hi=True)
        inv = inv + mm(inv, power, hi=True)
    tot = jnp.sum(la, axis=0, keepdims=True)
    u = mm(inv, beta * v, hi=True)
    w = mm(inv, (beta * jnp.exp(cum_k)) * k, hi=True)
    k_end = k * jnp.exp(tot - cum_k)
    di, dj = _iota2((dk, dk), 0), _iota2((dk, dk), 1)
    trans = jnp.exp(tot) * jnp.where(di == dj, 1.0, 0.0).astype(F32) - mm(k_end, w, ta=True)
    new_state = mm(trans, state) + mm(k_end, u, ta=True)
    o = mm(q, new_state)
    y = _head_norm(o, gain, False) * _silu(z)
    return y, new_state


def _conv_taps(shift, x, taps):
    out = None
    for tap, w in enumerate(taps):
        term = shift(x, CONV_WIDTH - 1 - tap, False) * w
        out = term if out is None else out + term
    return out


def _lru_pre(mm, shift, xb, yb, taps, cb, lam, wr, br, wi, bi):
    xb = _conv_taps(shift, xb, taps) + cb
    r = _sigmoid(mm(xb, wr) + br)
    i = _sigmoid(mm(xb, wi) + bi)
    log_a = (-LRU_C) * _softplus(-lam) * r
    a = jnp.exp(log_a)
    u = jnp.sqrt(-_expm1(2.0 * log_a)) * (i * xb)
    return a, u, _gelu_tanh(yb)


def _pick(n, pref):
    for t in (pref, 2048, 1024, 512, 256, 128):
        if t <= n and n % t == 0:
            return t
    return n


def _matmul(a, b, *, name, ta=False, tb=False, out_dtype=F32, b_sharded=False, o_sharded=False,
            epilogue=None, extra=None, bm=512, bn=512, bk=2048):
    m, kdim = (a.shape[1], a.shape[0]) if ta else a.shape
    if b_sharded:
        nq = b.shape[2]
        n = b.shape[1] if tb else N_CHIPS * nq
        assert kdim == (N_CHIPS * nq if tb else b.shape[1])
    else:
        n = b.shape[0] if tb else b.shape[1]
        assert kdim == (b.shape[1] if tb else b.shape[0])
    bm, bk = _pick(m, bm), _pick(kdim, bk)
    if b_sharded and tb:
        bk = _pick(b.shape[2], bk)
    bn = _pick(b.shape[2] if (b_sharded and not tb) else (n // N_CHIPS if o_sharded else n), bn)
    nk = kdim // bk
    grid = (m // bm, n // bn, nk)

    a_spec = pl.BlockSpec((bk, bm), lambda i, j, k: (k, i)) if ta else pl.BlockSpec((bm, bk), lambda i, j, k: (i, k))
    if b_sharded and not tb:
        per = b.shape[2] // bn
        b_spec = pl.BlockSpec((None, bk, bn), lambda i, j, k: (j // per, k, j % per))
    elif b_sharded:
        per = b.shape[2] // bk
        b_spec = pl.BlockSpec((None, bn, bk), lambda i, j, k: (k // per, j, k % per))
    elif tb:
        b_spec = pl.BlockSpec((bn, bk), lambda i, j, k: (j, k))
    else:
        b_spec = pl.BlockSpec((bk, bn), lambda i, j, k: (k, j))
    if o_sharded:
        per_o = (n // N_CHIPS) // bn
        o_spec = pl.BlockSpec((None, bm, bn), lambda i, j, k: (j // per_o, i, j % per_o))
        out_shape = jax.ShapeDtypeStruct((N_CHIPS, m, n // N_CHIPS), out_dtype)
    else:
        o_spec = pl.BlockSpec((bm, bn), lambda i, j, k: (i, j))
        out_shape = jax.ShapeDtypeStruct((m, n), out_dtype)
    in_specs = [a_spec, b_spec]
    operands = [a, b]
    if extra is not None:
        in_specs.append(pl.BlockSpec((bm, bn), lambda i, j, k: (i, j)))
        operands.append(extra)

    def body(*refs):
        a_ref, b_ref = refs[0], refs[1]
        e_ref = refs[2] if extra is not None else None
        o_ref, acc_ref = refs[-2], refs[-1]
        k = pl.program_id(2)

        @pl.when(k == 0)
        def _():
            acc_ref[...] = jnp.zeros_like(acc_ref)

        acc_ref[...] += _raw_mm(a_ref[...], b_ref[...], ta, tb, False)

        @pl.when(k == nk - 1)
        def _():
            acc = acc_ref[...]
            if epilogue == 'add':
                acc = acc + e_ref[...].astype(F32)
            elif epilogue == 'relu2':
                acc = jnp.square(jnp.maximum(acc, 0.0))
            elif epilogue == 'dact':
                acc = acc * (2.0 * jnp.sqrt(e_ref[...].astype(F32)))
            o_ref[...] = acc.astype(o_ref.dtype)

    return pl.pallas_call(
        body, name=name, grid=grid, in_specs=in_specs, out_specs=o_spec, out_shape=out_shape,
        scratch_shapes=[pltpu.VMEM((bm, bn), F32)], compiler_params=_cparams(),
    )(*operands)


def _rmsnorm_fwd(x, g, name):
    s, d = x.shape

    def body(x_ref, g_ref, o_ref):
        o_ref[...] = _rmsnorm(x_ref[...], g_ref[...]).astype(BF)

    return pl.pallas_call(
        body, name=name, grid=(s // ROW_BLOCK,),
        in_specs=[pl.BlockSpec((ROW_BLOCK, d), lambda i: (i, 0)), pl.BlockSpec((1, d), lambda i: (0, 0))],
        out_specs=pl.BlockSpec((ROW_BLOCK, d), lambda i: (i, 0)),
        out_shape=jax.ShapeDtypeStruct((s, d), BF), compiler_params=_cparams(),
    )(x, g.reshape(1, d))


def _rmsnorm_bwd(x, g, dh, dres, name):
    s, d = x.shape

    def body(x_ref, g_ref, dh_ref, dres_ref, dx_ref, dg_ref):
        _, vjp = jax.vjp(_rmsnorm, x_ref[...], g_ref[...])
        dx, dg = vjp(dh_ref[...].astype(F32))
        dx_ref[...] = dres_ref[...] + dx

        @pl.when(pl.program_id(0) == 0)
        def _():
            dg_ref[...] = jnp.zeros_like(dg_ref)

        dg_ref[...] += dg

    row = pl.BlockSpec((ROW_BLOCK, d), lambda i: (i, 0))
    vec = pl.BlockSpec((1, d), lambda i: (0, 0))
    dx, dg = pl.pallas_call(
        body, name=name, grid=(s // ROW_BLOCK,), in_specs=[row, vec, row, row], out_specs=[row, vec],
        out_shape=[jax.ShapeDtypeStruct((s, d), F32), jax.ShapeDtypeStruct((1, d), F32)],
        compiler_params=_cparams(),
    )(x, g.reshape(1, d), dh, dres)
    return dx, dg.reshape(d)


def _loss_head(x, g, target):
    s, d = x.shape

    def loss_fn(xv, gv, tv):
        err = _rmsnorm(xv, gv) - tv
        return 0.5 * jnp.sum(jnp.mean(err * err, axis=-1, keepdims=True), axis=0, keepdims=True)

    def body(x_ref, g_ref, t_ref, dx_ref, dg_ref, loss_ref):
        tv = t_ref[...]
        loss, vjp = jax.vjp(lambda xv, gv: loss_fn(xv, gv, tv), x_ref[...], g_ref[...])
        dx, dg = vjp(jnp.ones((1, 1), F32))
        dx_ref[...] = dx

        @pl.when(pl.program_id(0) == 0)
        def _():
            dg_ref[...] = jnp.zeros_like(dg_ref)
            loss_ref[...] = jnp.zeros_like(loss_ref)

        dg_ref[...] += dg
        loss_ref[...] += loss * jnp.ones_like(loss_ref)

    row = pl.BlockSpec((ROW_BLOCK, d), lambda i: (i, 0))
    vec = pl.BlockSpec((1, d), lambda i: (0, 0))
    dx, dg, loss = pl.pallas_call(
        body, name="loss_head", grid=(s // ROW_BLOCK,), in_specs=[row, vec, row],
        out_specs=[row, vec, pl.BlockSpec((1, 128), lambda i: (0, 0))],
        out_shape=[jax.ShapeDtypeStruct((s, d), F32), jax.ShapeDtypeStruct((1, d), F32),
                   jax.ShapeDtypeStruct((1, 128), F32)],
        compiler_params=_cparams(),
    )(x, g.reshape(1, d), target)
    return loss[0, 0], dx, dg.reshape(d)


def _ret_tables(s, dk):
    h = jnp.arange(RET_HEADS, dtype=F32)
    log_gamma = jnp.log1p(-jnp.exp2(-5.0 - h))
    pos = jnp.arange(CHUNK, dtype=F32)
    dist = jnp.abs(pos[:, None] - pos[None, :])
    dintra = jnp.exp(log_gamma[:, None, None] * dist)
    qdec = jnp.exp(log_gamma[:, None] * (pos + 1.0))[:, :, None]
    kdec = jnp.exp(log_gamma[:, None] * (CHUNK - 1.0 - pos))[:, :, None]
    cdec = jnp.exp(log_gamma * CHUNK)[:, None, None]
    inv = ROPE_BASE ** (-jnp.arange(0, dk, 2, dtype=F32) / dk)
    ang = jnp.arange(s, dtype=F32)[:, None] * inv[None, :]
    return jnp.cos(ang), jnp.sin(ang), dintra, qdec, kdec, cdec


def _ret_specs(s, dk, dv, tb, rev):
    nh = RET_HEADS
    nb = s // tb
    bi = (lambda b: nb - 1 - b) if rev else (lambda b: b)
    voff = 2 * nh * dk // dv
    cpb = tb // CHUNK
    return dict(
        q=pl.BlockSpec((tb, dk), lambda h, b: (bi(b), h)),
        k=pl.BlockSpec((tb, dk), lambda h, b: (bi(b), nh + h)),
        v=pl.BlockSpec((tb, dv), lambda h, b: (bi(b), voff + h)),
        g=pl.BlockSpec((tb, dv), lambda h, b: (bi(b), voff + nh + h)),
        gain=pl.BlockSpec((None, 1, dv), lambda h, b: (h, 0, 0)),
        cs=pl.BlockSpec((tb, dk // 2), lambda h, b: (bi(b), 0)),
        dintra=pl.BlockSpec((None, CHUNK, CHUNK), lambda h, b: (h, 0, 0)),
        dec=pl.BlockSpec((None, CHUNK, 1), lambda h, b: (h, 0, 0)),
        cdec=pl.BlockSpec((None, 1, 1), lambda h, b: (h, 0, 0)),
        hv=pl.BlockSpec((tb, dv), lambda h, b: (bi(b), h)),
        hk=pl.BlockSpec((tb, dk), lambda h, b: (bi(b), h)),
        st=pl.BlockSpec((None, cpb, dk, dv), lambda h, b: (h, bi(b), 0, 0)),
    )


def _ret_fwd(proj, gain, tables):
    s = proj.shape[0]
    d = proj.shape[1] // 6
    dk, dv = d // RET_HEADS, 2 * d // RET_HEADS
    tb = min(TOKEN_BLOCK, s)
    cpb = tb // CHUNK
    sp = _ret_specs(s, dk, dv, tb, False)
    cos, sin, dintra, qdec, kdec, cdec = tables

    def body(q_ref, k_ref, v_ref, g_ref, gain_ref, cos_ref, sin_ref, di_ref, qd_ref, kd_ref, cd_ref,
             y_ref, st_ref, state):
        @pl.when(pl.program_id(1) == 0)
        def _():
            state[...] = jnp.zeros_like(state)

        for c in range(cpb):
            sl = pl.ds(c * CHUNK, CHUNK)
            st_ref[c] = state[...]
            y, new_state = _ret_chunk(_mm_plain, q_ref[sl, :], k_ref[sl, :], v_ref[sl, :], g_ref[sl, :],
                                      gain_ref[...], state[...], cos_ref[sl, :], sin_ref[sl, :],
                                      di_ref[...], qd_ref[...], kd_ref[...], cd_ref[...])
            y_ref[sl, :] = y.astype(BF)
            state[...] = new_state

    return pl.pallas_call(
        body, name="ret_fwd", grid=(RET_HEADS, s // tb),
        in_specs=[sp['q'], sp['k'], sp['v'], sp['g'], sp['gain'], sp['cs'], sp['cs'], sp['dintra'], sp['dec'],
                  sp['dec'], sp['cdec']],
        out_specs=[sp['hv'], sp['st']],
        out_shape=[jax.ShapeDtypeStruct((s, RET_HEADS * dv), BF),
                   jax.ShapeDtypeStruct((RET_HEADS, s // CHUNK, dk, dv), F32)],
        scratch_shapes=[pltpu.VMEM((dk, dv), F32)], compiler_params=_cparams(),
    )(proj, proj, proj, proj, gain.reshape(RET_HEADS, 1, dv), cos, sin, dintra, qdec, kdec, cdec)


def _ret_bwd(proj, gain, tables, states, dy):
    s = proj.shape[0]
    d = proj.shape[1] // 6
    dk, dv = d // RET_HEADS, 2 * d // RET_HEADS
    tb = min(TOKEN_BLOCK, s)
    cpb = tb // CHUNK
    sp = _ret_specs(s, dk, dv, tb, True)
    cos, sin, dintra, qdec, kdec, cdec = tables

    def body(q_ref, k_ref, v_ref, g_ref, gain_ref, cos_ref, sin_ref, di_ref, qd_ref, kd_ref, cd_ref, st_ref,
             dy_ref, dq_ref, dk_ref, dv_ref, dg_ref, dgain_ref, dstate):
        @pl.when(pl.program_id(1) == 0)
        def _():
            dstate[...] = jnp.zeros_like(dstate)
            dgain_ref[...] = jnp.zeros_like(dgain_ref)

        for c in reversed(range(cpb)):
            sl = pl.ds(c * CHUNK, CHUNK)
            cos_c, sin_c = cos_ref[sl, :], sin_ref[sl, :]
            di, qd, kd, cd = di_ref[...], qd_ref[...], kd_ref[...], cd_ref[...]

            def fn(q, k, v, g, gn, st):
                return _ret_chunk(_mm_diff, q, k, v, g, gn, st, cos_c, sin_c, di, qd, kd, cd)

            _, vjp = jax.vjp(fn, q_ref[sl, :], k_ref[sl, :], v_ref[sl, :], g_ref[sl, :], gain_ref[...],
                             st_ref[c])
            dq, dkk, dvv, dg, dgn, dst = vjp((dy_ref[sl, :].astype(F32), dstate[...]))
            dq_ref[sl, :] = dq.astype(BF)
            dk_ref[sl, :] = dkk.astype(BF)
            dv_ref[sl, :] = dvv.astype(BF)
            dg_ref[sl, :] = dg.astype(BF)
            dgain_ref[...] += dgn
            dstate[...] = dst

    dq, dkk, dvv, dg, dgain = pl.pallas_call(
        body, name="ret_bwd", grid=(RET_HEADS, s // tb),
        in_specs=[sp['q'], sp['k'], sp['v'], sp['g'], sp['gain'], sp['cs'], sp['cs'], sp['dintra'], sp['dec'],
                  sp['dec'], sp['cdec'], sp['st'], sp['hv']],
        out_specs=[sp['hk'], sp['hk'], sp['hv'], sp['hv'], sp['gain']],
        out_shape=[jax.ShapeDtypeStruct((s, RET_HEADS * dk), BF), jax.ShapeDtypeStruct((s, RET_HEADS * dk), BF),
                   jax.ShapeDtypeStruct((s, RET_HEADS * dv), BF), jax.ShapeDtypeStruct((s, RET_HEADS * dv), BF),
                   jax.ShapeDtypeStruct((RET_HEADS, 1, dv), F32)],
        scratch_shapes=[pltpu.VMEM((dk, dv), F32)], compiler_params=_cparams(),
    )(proj, proj, proj, proj, gain.reshape(RET_HEADS, 1, dv), cos, sin, dintra, qdec, kdec, cdec, states, dy)
    return jnp.concatenate([dq, dkk, dvv, dg], axis=1), dgain.reshape(RET_HEADS, dv)


def _gla_specs(s, dk, dv, tb, rev):
    nh = GLA_HEADS
    nb = s // tb
    bi = (lambda b: nb - 1 - b) if rev else (lambda b: b)
    voff = 2 * nh * dk // dv
    cpb = tb // CHUNK
    return dict(
        q=pl.BlockSpec((tb, dk), lambda h, b: (bi(b), h)),
        k=pl.BlockSpec((tb, dk), lambda h, b: (bi(b), nh + h)),
        v=pl.BlockSpec((tb, dv), lambda h, b: (bi(b), voff + h)),
        r=pl.BlockSpec((tb, dv), lambda h, b: (bi(b), voff + nh + h)),
        glow=pl.BlockSpec((tb, 128), lambda h, b: (bi(b), 0)),
        wgu=pl.BlockSpec((128, dk), lambda h, b: (0, h)),
        bias=pl.BlockSpec((1, dk), lambda h, b: (0, h)),
        gain=pl.BlockSpec((None, 1, dv), lambda h, b: (h, 0, 0)),
        hv=pl.BlockSpec((tb, dv), lambda h, b: (bi(b), h)),
        hk=pl.BlockSpec((tb, dk), lambda h, b: (bi(b), h)),
        hg=pl.BlockSpec((tb, 128), lambda h, b: (bi(b), h)),
        st=pl.BlockSpec((None, cpb, dv, dk), lambda h, b: (h, bi(b), 0, 0)),
    )


def _gla_fwd(proj, glow, wgu, bias, gain):
    s = proj.shape[0]
    d = proj.shape[1] // 3
    dk, dv = d // 2 // GLA_HEADS, d // GLA_HEADS
    tb = min(TOKEN_BLOCK, s)
    cpb = tb // CHUNK
    sp = _gla_specs(s, dk, dv, tb, False)

    def body(q_ref, k_ref, v_ref, r_ref, gl_ref, wgu_ref, b_ref, gain_ref, y_ref, st_ref, state):
        @pl.when(pl.program_id(1) == 0)
        def _():
            state[...] = jnp.zeros_like(state)

        for c in range(cpb):
            sl = pl.ds(c * CHUNK, CHUNK)
            st_ref[c] = state[...]
            y, new_state = _gla_chunk(_mm_plain, q_ref[sl, :], k_ref[sl, :], v_ref[sl, :], r_ref[sl, :],
                                      gl_ref[sl, :], wgu_ref[...], b_ref[...], gain_ref[...], state[...])
            y_ref[sl, :] = y.astype(BF)
            state[...] = new_state

    return pl.pallas_call(
        body, name="gla_fwd", grid=(GLA_HEADS, s // tb),
        in_specs=[sp['q'], sp['k'], sp['v'], sp['r'], sp['glow'], sp['wgu'], sp['bias'], sp['gain']],
        out_specs=[sp['hv'], sp['st']],
        out_shape=[jax.ShapeDtypeStruct((s, GLA_HEADS * dv), BF),
                   jax.ShapeDtypeStruct((GLA_HEADS, s // CHUNK, dv, dk), F32)],
        scratch_shapes=[pltpu.VMEM((dv, dk), F32)], compiler_params=_cparams(),
    )(proj, proj, proj, proj, glow, wgu, bias, gain.reshape(GLA_HEADS, 1, dv))


def _gla_bwd(proj, glow, wgu, bias, gain, states, dy):
    s = proj.shape[0]
    d = proj.shape[1] // 3
    dk, dv = d // 2 // GLA_HEADS, d // GLA_HEADS
    tb = min(TOKEN_BLOCK, s)
    cpb = tb // CHUNK
    sp = _gla_specs(s, dk, dv, tb, True)

    def body(q_ref, k_ref, v_ref, r_ref, gl_ref, wgu_ref, b_ref, gain_ref, st_ref, dy_ref,
             dq_ref, dk_ref, dv_ref, dr_ref, dgl_ref, dwgu_ref, db_ref, dgain_ref, dstate):
        @pl.when(pl.program_id(1) == 0)
        def _():
            dstate[...] = jnp.zeros_like(dstate)
            dwgu_ref[...] = jnp.zeros_like(dwgu_ref)
            db_ref[...] = jnp.zeros_like(db_ref)
            dgain_ref[...] = jnp.zeros_like(dgain_ref)

        for c in reversed(range(cpb)):
            sl = pl.ds(c * CHUNK, CHUNK)

            def fn(q, k, v, r, gl, w, b, gn, st):
                return _gla_chunk(_mm_diff, q, k, v, r, gl, w, b, gn, st)

            _, vjp = jax.vjp(fn, q_ref[sl, :], k_ref[sl, :], v_ref[sl, :], r_ref[sl, :], gl_ref[sl, :],
                             wgu_ref[...], b_ref[...], gain_ref[...], st_ref[c])
            dq, dkk, dvv, dr, dgl, dw, db, dgn, dst = vjp((dy_ref[sl, :].astype(F32), dstate[...]))
            dq_ref[sl, :] = dq.astype(BF)
            dk_ref[sl, :] = dkk.astype(BF)
            dv_ref[sl, :] = dvv.astype(BF)
            dr_ref[sl, :] = dr.astype(BF)
            dgl_ref[sl, :] = dgl.astype(BF)
            dwgu_ref[...] += dw
            db_ref[...] += db
            dgain_ref[...] += dgn
            dstate[...] = dst

    nh = GLA_HEADS
    dq, dkk, dvv, dr, dgl, dwgu, db, dgain = pl.pallas_call(
        body, name="gla_bwd", grid=(nh, s // tb),
        in_specs=[sp['q'], sp['k'], sp['v'], sp['r'], sp['glow'], sp['wgu'], sp['bias'], sp['gain'], sp['st'],
                  sp['hv']],
        out_specs=[sp['hk'], sp['hk'], sp['hv'], sp['hv'], sp['hg'], sp['wgu'], sp['bias'], sp['gain']],
        out_shape=[jax.ShapeDtypeStruct((s, nh * dk), BF), jax.ShapeDtypeStruct((s, nh * dk), BF),
                   jax.ShapeDtypeStruct((s, nh * dv), BF), jax.ShapeDtypeStruct((s, nh * dv), BF),
                   jax.ShapeDtypeStruct((s, nh * 128), BF), jax.ShapeDtypeStruct((128, nh * dk), F32),
                   jax.ShapeDtypeStruct((1, nh * dk), F32), jax.ShapeDtypeStruct((nh, 1, dv), F32)],
        scratch_shapes=[pltpu.VMEM((dv, dk), F32)], compiler_params=_cparams(),
    )(proj, proj, proj, proj, glow, wgu, bias, gain.reshape(nh, 1, dv), states, dy)
    return jnp.concatenate([dq, dkk, dvv, dr], axis=1), dgl, dwgu, db, dgain.reshape(nh, dv)


def _gdn_specs(s, dk):
    nh = GDN_HEADS
    col = lambda off: pl.BlockSpec((s, dk), lambda h: (0, off + h))
    tap = lambda off: pl.BlockSpec((CONV_WIDTH, dk), lambda h: (0, off + h))
    vec = pl.BlockSpec((1, 128), lambda h: (0, 0))
    return dict(q=col(0), k=col(nh), v=col(2 * nh), z=col(3 * nh), ba=pl.BlockSpec((s, 128), lambda h: (0, 0)),
                cq=tap(0), ck=tap(nh), cv=tap(2 * nh), vec=vec, gain=pl.BlockSpec((1, dk), lambda h: (0, 0)),
                head=col(0))


def _conv_silu(shift, x, taps):
    return _silu(_conv_taps(shift, x, taps))


def _load_taps(ref):
    return [ref[t:t + 1, :] for t in range(CONV_WIDTH)]


def _gdn_fwd(proj, ba, conv_w, a_log, dt_bias, gain):
    s = proj.shape[0]
    dk = proj.shape[1] // (4 * GDN_HEADS)
    sp = _gdn_specs(s, dk)
    nchunks = s // CHUNK

    def body(q_ref, k_ref, v_ref, z_ref, ba_ref, cq_ref, ck_ref, cv_ref, al_ref, dt_ref, gain_ref, y_ref,
             qc, kc, vc, state):
        head = pl.program_id(0)
        qc[...] = _conv_silu(_shift_rows, q_ref[...], _load_taps(cq_ref))
        kc[...] = _conv_silu(_shift_rows, k_ref[...], _load_taps(ck_ref))
        vc[...] = _conv_silu(_shift_rows, v_ref[...], _load_taps(cv_ref))
        state[...] = jnp.zeros_like(state)

        def step(c, carry):
            sl = pl.ds(pl.multiple_of(c * CHUNK, CHUNK), CHUNK)
            y, new_state = _gdn_chunk(_mm_plain, qc[sl, :], kc[sl, :], vc[sl, :], z_ref[sl, :], ba_ref[sl, :],
                                      al_ref[...], dt_ref[...], gain_ref[...], state[...], head)
            y_ref[sl, :] = y.astype(BF)
            state[...] = new_state
            return carry

        lax.fori_loop(0, nchunks, step, 0, unroll=GDN_UNROLL)

    return pl.pallas_call(
        body, name="gdn_fwd", grid=(GDN_HEADS,),
        in_specs=[sp['q'], sp['k'], sp['v'], sp['z'], sp['ba'], sp['cq'], sp['ck'], sp['cv'], sp['vec'], sp['vec'],
                  sp['gain']],
        out_specs=sp['head'], out_shape=jax.ShapeDtypeStruct((s, GDN_HEADS * dk), BF),
        scratch_shapes=[pltpu.VMEM((s, dk), F32)] * 3 + [pltpu.VMEM((dk, dk), F32)],
        compiler_params=_cparams(),
    )(proj, proj, proj, proj, ba, conv_w, conv_w, conv_w, a_log, dt_bias, gain)


def _gdn_bwd(proj, ba, conv_w, a_log, dt_bias, gain, dy):
    s = proj.shape[0]
    dk = proj.shape[1] // (4 * GDN_HEADS)
    sp = _gdn_specs(s, dk)
    nchunks = s // CHUNK

    def body(q_ref, k_ref, v_ref, z_ref, ba_ref, cq_ref, ck_ref, cv_ref, al_ref, dt_ref, gain_ref, dy_ref,
             dq_ref, dk_ref, dv_ref, dz_ref, dba_ref, dcq_ref, dck_ref, dcv_ref, dal_ref, ddt_ref, dgain_ref,
             qc, kc, vc, dqc, dkc, dvc, states, state, dstate):
        head = pl.program_id(0)

        @pl.when(head == 0)
        def _():
            dba_ref[...] = jnp.zeros_like(dba_ref)
            dal_ref[...] = jnp.zeros_like(dal_ref)
            ddt_ref[...] = jnp.zeros_like(ddt_ref)
            dgain_ref[...] = jnp.zeros_like(dgain_ref)

        qc[...] = _conv_silu(_shift_rows, q_ref[...], _load_taps(cq_ref))
        kc[...] = _conv_silu(_shift_rows, k_ref[...], _load_taps(ck_ref))
        vc[...] = _conv_silu(_shift_rows, v_ref[...], _load_taps(cv_ref))
        state[...] = jnp.zeros_like(state)

        def fstep(c, carry):
            sl = pl.ds(pl.multiple_of(c * CHUNK, CHUNK), CHUNK)
            states[c] = state[...]
            _, new_state = _gdn_chunk(_mm_plain, qc[sl, :], kc[sl, :], vc[sl, :], z_ref[sl, :], ba_ref[sl, :],
                                      al_ref[...], dt_ref[...], gain_ref[...], state[...], head)
            state[...] = new_state
            return carry

        lax.fori_loop(0, nchunks, fstep, 0, unroll=GDN_UNROLL)
        dstate[...] = jnp.zeros_like(dstate)

        def bstep(i, carry):
            c = nchunks - 1 - i
            sl = pl.ds(pl.multiple_of(c * CHUNK, CHUNK), CHUNK)

            def fn(q, k, v, z, b, al, dt, gn, st):
                return _gdn_chunk(_mm_diff, q, k, v, z, b, al, dt, gn, st, head)

            _, vjp = jax.vjp(fn, qc[sl, :], kc[sl, :], vc[sl, :], z_ref[sl, :], ba_ref[sl, :], al_ref[...],
                             dt_ref[...], gain_ref[...], states[c])
            dq, dkk, dvv, dz, db, dal, ddt, dgn, dst = vjp((dy_ref[sl, :].astype(F32), dstate[...]))
            dqc[sl, :] = dq
            dkc[sl, :] = dkk
            dvc[sl, :] = dvv
            dz_ref[sl, :] = dz.astype(BF)
            dba_ref[sl, :] += db
            dal_ref[...] += dal
            ddt_ref[...] += ddt
            dgain_ref[...] += dgn
            dstate[...] = dst
            return carry

        lax.fori_loop(0, nchunks, bstep, 0, unroll=GDN_UNROLL)

        for x_ref, c_ref, dpost, dx_ref, dc_ref in ((q_ref, cq_ref, dqc, dq_ref, dcq_ref),
                                                    (k_ref, ck_ref, dkc, dk_ref, dck_ref),
                                                    (v_ref, cv_ref, dvc, dv_ref, dcv_ref)):
            _, vjp = jax.vjp(lambda x, *taps: _conv_silu(_shift_vjp, x, taps), x_ref[...], *_load_taps(c_ref))
            grads = vjp(dpost[...])
            dx_ref[...] = grads[0].astype(BF)
            for t in range(CONV_WIDTH):
                dc_ref[t:t + 1, :] = grads[1 + t]

    nh = GDN_HEADS
    col_bf = jax.ShapeDtypeStruct((s, nh * dk), BF)
    tap_out = jax.ShapeDtypeStruct((CONV_WIDTH, nh * dk), F32)
    tap_spec = pl.BlockSpec((CONV_WIDTH, dk), lambda h: (0, h))
    dq, dkk, dvv, dz, dba, dcq, dck, dcv, dal, ddt, dgain = pl.pallas_call(
        body, name="gdn_bwd", grid=(nh,),
        in_specs=[sp['q'], sp['k'], sp['v'], sp['z'], sp['ba'], sp['cq'], sp['ck'], sp['cv'], sp['vec'], sp['vec'],
                  sp['gain'], sp['head']],
        out_specs=[sp['head']] * 4 + [sp['ba'], tap_spec, tap_spec, tap_spec, sp['vec'], sp['vec'], sp['gain']],
        out_shape=[col_bf] * 4 + [jax.ShapeDtypeStruct((s, 128), F32), tap_out, tap_out, tap_out,
                                  jax.ShapeDtypeStruct((1, 128), F32), jax.ShapeDtypeStruct((1, 128), F32),
                                  jax.ShapeDtypeStruct((1, dk), F32)],
        scratch_shapes=[pltpu.VMEM((s, dk), F32)] * 6 + [pltpu.VMEM((nchunks, dk, dk), F32),
                                                        pltpu.VMEM((dk, dk), F32), pltpu.VMEM((dk, dk), F32)],
        compiler_params=_cparams(),
    )(proj, proj, proj, proj, ba, conv_w, conv_w, conv_w, a_log, dt_bias, gain, dy)
    return (jnp.concatenate([dq, dkk, dvv, dz], axis=1), dba, jnp.concatenate([dcq, dck, dcv], axis=1), dal, ddt,
            dgain)


def _lru_specs(s, bw):
    nb = LRU_BLOCKS
    return dict(
        xb=pl.BlockSpec((s, bw), lambda n: (0, n)), yb=pl.BlockSpec((s, bw), lambda n: (0, nb + n)),
        taps=pl.BlockSpec((CONV_WIDTH, bw), lambda n: (0, n)), vec=pl.BlockSpec((1, bw), lambda n: (0, n)),
        w=pl.BlockSpec((None, bw, bw), lambda n: (n, 0, 0)), b=pl.BlockSpec((None, 1, bw), lambda n: (n, 0, 0)),
        col=pl.BlockSpec((s, bw), lambda n: (0, n)))


def _lru_fwd(proj, conv_w, conv_b, lam, wr, br, wi, bi):
    s = proj.shape[0]
    bw = proj.shape[1] // (2 * LRU_BLOCKS)
    sp = _lru_specs(s, bw)

    def body(xb_ref, yb_ref, cw_ref, cb_ref, lam_ref, wr_ref, br_ref, wi_ref, bi_ref, y_ref):
        a, u, gy = _lru_pre(_mm_plain, _shift_rows, xb_ref[...], yb_ref[...], _load_taps(cw_ref), cb_ref[...],
                            lam_ref[...], wr_ref[...], br_ref[...], wi_ref[...], bi_ref[...])
        y_ref[...] = (_linscan(a, u, False) * gy).astype(BF)

    return pl.pallas_call(
        body, name="lru_fwd", grid=(LRU_BLOCKS,),
        in_specs=[sp['xb'], sp['yb'], sp['taps'], sp['vec'], sp['vec'], sp['w'], sp['b'], sp['w'], sp['b']],
        out_specs=sp['col'], out_shape=jax.ShapeDtypeStruct((s, LRU_BLOCKS * bw), BF),
        compiler_params=_cparams(),
    )(proj, proj, conv_w, conv_b, lam, wr, br, wi, bi)


def _lru_bwd(proj, conv_w, conv_b, lam, wr, br, wi, bi, dy):
    s = proj.shape[0]
    nb = LRU_BLOCKS
    bw = proj.shape[1] // (2 * nb)
    sp = _lru_specs(s, bw)

    def body(xb_ref, yb_ref, cw_ref, cb_ref, lam_ref, wr_ref, br_ref, wi_ref, bi_ref, dy_ref,
             dxb_ref, dyb_ref, dcw_ref, dcb_ref, dlam_ref, dwr_ref, dbr_ref, dwi_ref, dbi_ref):
        def pre(xb, yb, t0, t1, t2, t3, cb, lm, w_r, b_r, w_i, b_i):
            return _lru_pre(_mm_diff, _shift_vjp, xb, yb, (t0, t1, t2, t3), cb, lm, w_r, b_r, w_i, b_i)

        (a, u, gy), vjp = jax.vjp(pre, xb_ref[...], yb_ref[...], *_load_taps(cw_ref), cb_ref[...], lam_ref[...],
                                  wr_ref[...], br_ref[...], wi_ref[...], bi_ref[...])
        h = _linscan(a, u, False)
        dout = dy_ref[...].astype(F32)
        g = _linscan(_shift_rows(a, 1, True), dout * gy, True)
        grads = vjp((g * _shift_rows(h, 1, False), g, dout * h))
        dxb_ref[...] = grads[0].astype(BF)
        dyb_ref[...] = grads[1].astype(BF)
        for t in range(CONV_WIDTH):
            dcw_ref[t:t + 1, :] = grads[2 + t]
        dcb_ref[...] = grads[6]
        dlam_ref[...] = grads[7]
        dwr_ref[...] = grads[8]
        dbr_ref[...] = grads[9]
        dwi_ref[...] = grads[10]
        dbi_ref[...] = grads[11]

    outs = pl.pallas_call(
        body, name="lru_bwd", grid=(nb,),
        in_specs=[sp['xb'], sp['yb'], sp['taps'], sp['vec'], sp['vec'], sp['w'], sp['b'], sp['w'], sp['b'], sp['col']],
        out_specs=[sp['col'], sp['col'], sp['taps'], sp['vec'], sp['vec'], sp['w'], sp['b'], sp['w'], sp['b']],
        out_shape=[jax.ShapeDtypeStruct((s, nb * bw), BF), jax.ShapeDtypeStruct((s, nb * bw), BF),
                   jax.ShapeDtypeStruct((CONV_WIDTH, nb * bw), F32), jax.ShapeDtypeStruct((1, nb * bw), F32),
                   jax.ShapeDtypeStruct((1, nb * bw), F32), jax.ShapeDtypeStruct((nb, bw, bw), F32),
                   jax.ShapeDtypeStruct((nb, 1, bw), F32), jax.ShapeDtypeStruct((nb, bw, bw), F32),
                   jax.ShapeDtypeStruct((nb, 1, bw), F32)],
        compiler_params=_cparams(),
    )(proj, proj, conv_w, conv_b, lam, wr, br, wi, bi, dy)
    return (jnp.concatenate([outs[0], outs[1]], axis=1),) + tuple(outs[2:])


def _row_tile(rows, cols, bytes_per_row_set):
    t = 8
    while t * 2 <= rows and rows % (t * 2) == 0 and (t * 2) * cols * bytes_per_row_set <= 4 * 1024 * 1024:
        t *= 2
    return t


def _sum_slabs(a, out_dtype, name):
    n, rows, cols = a.shape
    tr = _row_tile(rows, cols, 4 * (n + 1))

    def body(*refs):
        acc = refs[0][...].astype(F32)
        for r in refs[1:n]:
            acc = acc + r[...].astype(F32)
        refs[n][...] = acc.astype(out_dtype)

    specs = [pl.BlockSpec((None, tr, cols), functools.partial(lambda i, k: (k, i, 0), k=k)) for k in range(n)]
    return pl.pallas_call(
        body, name=name, grid=(rows // tr,), in_specs=specs, out_specs=pl.BlockSpec((tr, cols), lambda i: (i, 0)),
        out_shape=jax.ShapeDtypeStruct((rows, cols), out_dtype), compiler_params=_cparams(),
    )(*([a] * n))


def _sum_keep_and_landed(g, landed, core, name):
    four, hr, cols = landed.shape
    tr = _row_tile(hr, cols, 2 * 3)
    tph = hr // tr

    def body(core_ref, g_ref, l_ref, o_ref):
        o_ref[...] = (g_ref[...].astype(F32) + l_ref[...].astype(F32)).astype(o_ref.dtype)

    grid_spec = pltpu.PrefetchScalarGridSpec(
        num_scalar_prefetch=1, grid=(four, tph),
        in_specs=[pl.BlockSpec((None, tr, cols), lambda j, r, core_ref: (j, core_ref[0] * tph + r, 0)),
                  pl.BlockSpec((None, tr, cols), lambda j, r, core_ref: (j, r, 0))],
        out_specs=pl.BlockSpec((None, tr, cols), lambda j, r, core_ref: (j, r, 0)))
    return pl.pallas_call(body, name=name, grid_spec=grid_spec, out_shape=jax.ShapeDtypeStruct(landed.shape, g.dtype),
                          compiler_params=_cparams())(core, g, landed)


def _sum_chip_parts(parts, landed, chip, name):
    _, rows, cols = parts.shape
    tr = _row_tile(rows, cols, 2 * 4 + 4)

    def body(chip_ref, p_ref, l0_ref, l1_ref, l2_ref, o_ref):
        acc = l0_ref[...].astype(F32) + l1_ref[...].astype(F32)
        o_ref[...] = (acc + l2_ref[...].astype(F32)) + p_ref[...].astype(F32)

    slab = lambda k: pl.BlockSpec((None, tr, cols), lambda i, chip_ref: (k, i, 0))
    grid_spec = pltpu.PrefetchScalarGridSpec(
        num_scalar_prefetch=1, grid=(rows // tr,),
        in_specs=[pl.BlockSpec((None, tr, cols), lambda i, chip_ref: (chip_ref[0], i, 0)), slab(0), slab(1), slab(2)],
        out_specs=pl.BlockSpec((tr, cols), lambda i, chip_ref: (i, 0)))
    return pl.pallas_call(body, name=name, grid_spec=grid_spec, out_shape=jax.ShapeDtypeStruct((rows, cols), F32),
                          compiler_params=_cparams())(chip, parts, landed, landed, landed)


_ADAM_C1 = 1.0 / (1.0 - ADAM_B1 ** ADAM_STEP)
_ADAM_C2 = 1.0 / (1.0 - ADAM_B2 ** ADAM_STEP)


def _adamw_math(w, g, m, v):
    nm = ADAM_B1 * m + (1.0 - ADAM_B1) * g
    nv = ADAM_B2 * v + (1.0 - ADAM_B2) * (g * g)
    delta = -ADAM_LR * ((nm * _ADAM_C1) / (jnp.sqrt(nv * _ADAM_C2) + ADAM_EPS) + ADAM_WD * w)
    return delta, nm, nv


def _adamw(w, g, m, v, name):
    rows, cols = w.shape
    tr = _row_tile(rows, cols, 4 * 7)

    def body(w_ref, g_ref, m_ref, v_ref, d_ref, nm_ref, nv_ref):
        d_ref[...], nm_ref[...], nv_ref[...] = _adamw_math(w_ref[...], g_ref[...], m_ref[...], v_ref[...])

    spec = pl.BlockSpec((tr, cols), lambda i: (i, 0))
    shape = jax.ShapeDtypeStruct((rows, cols), F32)
    return pl.pallas_call(
        body, name=name, grid=(rows // tr,), in_specs=[spec] * 4, out_specs=[spec] * 3, out_shape=[shape] * 3,
        compiler_params=_cparams(),
    )(w, g, m, v)


def _adamw_halves(w, m, v, layer, mine, theirs, core, name, carried=None):
    n_layers, rows, cols = w.shape
    hr = mine.shape[0]
    tr = _row_tile(hr, cols, 4 * 9)
    tph = hr // tr

    def body(core_ref, w_ref, q_ref, t_ref, m_ref, v_ref, *rest):
        g_ref, d_ref, nm_ref, nv_ref = rest[-4:]
        is_mine = (pl.program_id(0) // tph) == core_ref[0]
        g = jnp.where(is_mine, q_ref[...], t_ref[...])
        g_ref[...] = g
        d_ref[...], nm_ref[...], nv_ref[...] = _adamw_math(w_ref[...], g, m_ref[...], v_ref[...])

    slab = pl.BlockSpec((None, tr, cols), lambda i, core_ref: (layer, i, 0))
    half = pl.BlockSpec((tr, cols), lambda i, core_ref: (i % tph, 0))
    in_specs = [slab, half, half, slab, slab]
    operands = [core, w, mine, theirs, m, v]
    aliases = {}
    if carried is not None:
        in_specs += [_ANY] * 4
        aliases = {len(operands) + k: k for k in range(4)}
        operands += list(carried)
    grid_spec = pltpu.PrefetchScalarGridSpec(num_scalar_prefetch=1, grid=(rows // tr,), in_specs=in_specs,
                                             out_specs=[slab] * 4)
    return pl.pallas_call(
        body, name=name, grid_spec=grid_spec, out_shape=[jax.ShapeDtypeStruct(w.shape, F32)] * 4,
        input_output_aliases=aliases, compiler_params=_cparams(),
    )(*operands)


_ANY = pl.BlockSpec(memory_space=pl.ANY)


def _mesh_pos():
    return lax.axis_index("x"), lax.axis_index("y"), lax.axis_index("c")


def _remote(src, dst, send_sems, recv_sems, k, dev):
    return pltpu.make_async_remote_copy(src_ref=src, dst_ref=dst, send_sem=send_sems.at[k], recv_sem=recv_sems.at[k],
                                        device_id=dev, device_id_type=MESH)


STREAM_CHUNK_BYTES = 1024 * 1024
STREAM_SLOTS = 3


def _chunk_rows(rows, cols, itemsize):
    t = 16
    while t * 2 <= rows and rows % (t * 2) == 0 and (t * 2) * cols * itemsize <= STREAM_CHUNK_BYTES:
        t *= 2
    return t


def _stream_chunks(n_chunks, src_at, dst_at, buf, load_sems, send_sems, recv_sem, sibling):
    def load(k, slot):
        return pltpu.make_async_copy(src_at(k), buf.at[slot], load_sems.at[slot])

    def send(k, slot):
        return pltpu.make_async_remote_copy(src_ref=buf.at[slot], dst_ref=dst_at(k), send_sem=send_sems.at[slot],
                                            recv_sem=recv_sem, device_id=sibling, device_id_type=MESH)

    def step(k, carry):
        slot = k % STREAM_SLOTS

        @pl.when(k >= STREAM_SLOTS)
        def _():
            send(k - STREAM_SLOTS, slot).wait_send()

        load(k, slot).start()

        @pl.when(k >= 1)
        def _():
            prev = (k - 1) % STREAM_SLOTS
            load(k - 1, prev).wait()
            send(k - 1, prev).start()

        return carry

    lax.fori_loop(0, n_chunks, step, 0)
    last = (n_chunks - 1) % STREAM_SLOTS
    load(n_chunks - 1, last).wait()
    send(n_chunks - 1, last).start()
    for k in range(max(0, n_chunks - STREAM_SLOTS), n_chunks):
        send(k, k % STREAM_SLOTS).wait_send()


def _all_gather_chips(shards, split, name):
    n = len(shards)
    tr = [_chunk_rows(s.shape[0] // 2, s.shape[1], s.dtype.itemsize) if split[i] else 0 for i, s in enumerate(shards)]

    def body(*refs):
        ins, outs = refs[:n], refs[n:2 * n]
        send_sems, recv_sems, load_sems, stream_sems = refs[2 * n:2 * n + 4]
        bufs = dict(zip([i for i in range(n) if split[i]], refs[2 * n + 4:]))
        x, y, c = _mesh_pos()
        me = 2 * x + y
        sibling = (x, y, 1 - c)
        chips = [(1 - x, y), (x, 1 - y), (1 - x, 1 - y)]

        def half(i, which):
            hr = ins[i].shape[0] // 2
            return pl.ds(which * hr, hr)

        first = []
        for i in range(n):
            src = ins[i].at[half(i, c)] if split[i] else ins[i]
            dst = outs[i].at[me, half(i, c)] if split[i] else outs[i].at[me]
            for j, chip in enumerate(chips):
                first.append(_remote(src, dst, send_sems, recv_sems, 6 * i + j, (*chip, c)))
        for cp in first:
            cp.start()
        for i in range(n):
            for j, chip in enumerate(chips):
                pj = 2 * chip[0] + chip[1]
                blk = outs[i].at[pj, half(i, c)] if split[i] else outs[i].at[pj]
                _remote(blk, blk, send_sems, recv_sems, 6 * i + j, (*chip, c)).wait_recv()
                if split[i]:
                    hr = ins[i].shape[0] // 2

                    def rows_at(k, i=i, pj=pj, hr=hr):
                        return outs[i].at[pj, pl.ds(c * hr + k * tr[i], tr[i])]

                    _stream_chunks(hr // tr[i], rows_at, rows_at, bufs[i], load_sems, stream_sems,
                                   recv_sems.at[6 * i + 3 + j], sibling)
        for i in range(n):
            if split[i]:
                for j, chip in enumerate(chips):
                    blk = outs[i].at[2 * chip[0] + chip[1], half(i, 1 - c)]
                    _remote(blk, blk, send_sems, recv_sems, 6 * i + 3 + j, sibling).wait_recv()
        for cp in first:
            cp.wait_send()

    return pl.pallas_call(
        body, name=name, in_specs=[_ANY] * n, out_specs=[_ANY] * n,
        out_shape=[jax.ShapeDtypeStruct((N_CHIPS,) + s.shape, s.dtype) for s in shards],
        scratch_shapes=[pltpu.SemaphoreType.DMA((6 * n,)), pltpu.SemaphoreType.DMA((6 * n,)),
                        pltpu.SemaphoreType.DMA((STREAM_SLOTS,)), pltpu.SemaphoreType.DMA((STREAM_SLOTS,))]
        + [pltpu.VMEM((STREAM_SLOTS, tr[i], s.shape[1]), s.dtype) for i, s in enumerate(shards) if split[i]],
        compiler_params=_cparams(),
    )(*shards)


def _all_gather_devices(part, name):
    def body(in_ref, out_ref, send_sems, recv_sems, local_sem):
        x, y, c = _mesh_pos()
        me = 4 * x + 2 * y + c
        local = pltpu.make_async_copy(in_ref, out_ref.at[me], local_sem)
        local.start()
        peers = [(x ^ (k >> 2), y ^ ((k >> 1) & 1), c ^ (k & 1)) for k in range(1, N_DEV)]
        sends = [_remote(in_ref, out_ref.at[me], send_sems, recv_sems, k, p) for k, p in enumerate(peers)]
        for cp in sends:
            cp.start()
        for k, p in enumerate(peers):
            blk = out_ref.at[4 * p[0] + 2 * p[1] + p[2]]
            _remote(blk, blk, send_sems, recv_sems, k, p).wait_recv()
        for cp in sends:
            cp.wait_send()
        local.wait()

    return pl.pallas_call(
        body, name=name, in_specs=[_ANY], out_specs=_ANY,
        out_shape=jax.ShapeDtypeStruct((N_DEV,) + part.shape, part.dtype),
        scratch_shapes=[pltpu.SemaphoreType.DMA((N_DEV - 1,)), pltpu.SemaphoreType.DMA((N_DEV - 1,)),
                        pltpu.SemaphoreType.DMA],
    )(part)


def _stream_to_sibling(srcs, halved, name):
    n = len(srcs)
    geo = []
    for s in srcs:
        rows = s.shape[1] // 2 if halved else s.shape[1]
        geo.append((s.shape[0], rows, s.shape[2], _chunk_rows(rows, s.shape[2], s.dtype.itemsize)))

    def body(*refs):
        ins, outs = refs[:n], refs[n:2 * n]
        recv_sems, load_sems, send_sems = refs[2 * n:2 * n + 3]
        bufs = refs[2 * n + 3:]
        x, y, c = _mesh_pos()
        sibling = (x, y, 1 - c)
        for i in range(n):
            slabs, rows, _, tr = geo[i]
            per_slab = rows // tr
            off = (1 - c) * rows if halved else 0

            def src_at(k, i=i, per_slab=per_slab, tr=tr, off=off):
                return ins[i].at[k // per_slab, pl.ds(off + (k % per_slab) * tr, tr)]

            def dst_at(k, i=i, per_slab=per_slab, tr=tr):
                return outs[i].at[k // per_slab, pl.ds((k % per_slab) * tr, tr)]

            _stream_chunks(slabs * per_slab, src_at, dst_at, bufs[i], load_sems, send_sems, recv_sems.at[i], sibling)
        for i in range(n):
            pltpu.make_async_remote_copy(src_ref=outs[i], dst_ref=outs[i], send_sem=send_sems.at[0],
                                         recv_sem=recv_sems.at[i], device_id=sibling, device_id_type=MESH).wait_recv()

    return pl.pallas_call(
        body, name=name, in_specs=[_ANY] * n, out_specs=[_ANY] * n,
        out_shape=[jax.ShapeDtypeStruct((g[0], g[1], g[2]), s.dtype) for g, s in zip(geo, srcs)],
        scratch_shapes=[pltpu.SemaphoreType.DMA((n,)), pltpu.SemaphoreType.DMA((STREAM_SLOTS,)),
                        pltpu.SemaphoreType.DMA((STREAM_SLOTS,))]
        + [pltpu.VMEM((STREAM_SLOTS, g[3], g[2]), s.dtype) for g, s in zip(geo, srcs)],
        compiler_params=_cparams(),
    )(*srcs)


def _chip_scatter(parts, name):
    n = len(parts)

    def body(*refs):
        ins, outs = refs[:n], refs[n:2 * n]
        send_sems, recv_sems = refs[2 * n:]
        x, y, c = _mesh_pos()
        chips = [(1 - x, y), (x, 1 - y), (1 - x, 1 - y)]
        copies = []
        for i in range(n):
            for j, chip in enumerate(chips):
                copies.append(_remote(ins[i].at[2 * chip[0] + chip[1]], outs[i].at[j], send_sems, recv_sems, 3 * i + j,
                                      (*chip, c)))
        for cp in copies:
            cp.start()
        for cp in copies:
            cp.wait()

    return pl.pallas_call(
        body, name=name, in_specs=[_ANY] * n, out_specs=[_ANY] * n,
        out_shape=[jax.ShapeDtypeStruct((3,) + p.shape[1:], p.dtype) for p in parts],
        scratch_shapes=[pltpu.SemaphoreType.DMA((3 * n,)), pltpu.SemaphoreType.DMA((3 * n,))],
    )(*parts)


def _reduce_scatter(grads, tag, core, chip):
    landed = _stream_to_sibling(grads, True, f"rs_swap_{tag}")
    parts = [_sum_keep_and_landed(g, l, core, f"rs_add2_{tag}_{i}") for i, (g, l) in enumerate(zip(grads, landed))]
    landed = _chip_scatter(parts, f"rs_scatter_{tag}")
    mine = [_sum_chip_parts(p, l, chip, f"rs_add4_{tag}_{i}") for i, (p, l) in enumerate(zip(parts, landed))]
    theirs = _stream_to_sibling([m[None] for m in mine], False, f"rs_join_{tag}")
    return [(m, t[0]) for m, t in zip(mine, theirs)]


def _pad_cols(a, width=128):
    return jnp.pad(a, ((0, 0), (0, width - a.shape[1])))


def _mixer_forward(kind, hn, w, tables):
    if kind == 0:
        proj = _matmul(hn, w['ret_w_in'], name="ret_proj", b_sharded=True)
        og, states = _ret_fwd(proj, w['ret_gn_gain'], tables)
        return og, (proj, states)
    if kind == 1:
        proj = _matmul(hn, w['gdn_w_main'], name="gdn_proj")
        ba = _matmul(hn, w['gdn_w_small'], name="gdn_proj_ba")
        og = _gdn_fwd(proj, ba, w['gdn_conv_w'], w['gdn_a_log'], w['gdn_dt_bias'], w['gdn_norm_gain'])
        return og, (proj, ba)
    if kind == 2:
        proj = _matmul(hn, w['gla_w_main'], name="gla_proj")
        glow = _matmul(hn, w['gla_w_small'], name="gla_proj_gate")
        og, states = _gla_fwd(proj, glow, w['gla_w_gate_up'], w['gla_gate_bias'], w['gla_norm_gain'])
        return og, (proj, glow, states)
    proj = _matmul(hn, w['lru_w_in'], name="lru_proj", b_sharded=True)
    og = _lru_fwd(proj, w['lru_conv_w'], w['lru_conv_b'], w['lru_lambda'], w['lru_w_rgate'], w['lru_b_rgate'],
                  w['lru_w_igate'], w['lru_b_igate'])
    return og, (proj,)


def _mixer_backward(kind, hn, w, tables, saved, d_og, grads):
    d = hn.shape[1]
    if kind == 0:
        proj, states = saved
        d_proj, grads['ret_gn_gain'] = _ret_bwd(proj, w['ret_gn_gain'], tables, states, d_og)
        grads['ret_w_in'] = _matmul(hn, d_proj, name="ret_dw_in", ta=True, out_dtype=BF, o_sharded=True)
        return _matmul(d_proj, w['ret_w_in'], name="ret_dhn", tb=True, b_sharded=True)
    if kind == 1:
        proj, ba = saved
        d_proj, d_ba, grads['gdn_conv_w'], grads['gdn_a_log'], grads['gdn_dt_bias'], grads['gdn_norm_gain'] = _gdn_bwd(
            proj, ba, w['gdn_conv_w'], w['gdn_a_log'], w['gdn_dt_bias'], w['gdn_norm_gain'], d_og)
        d_ba = d_ba.astype(BF)
        dw_main = _matmul(hn, d_proj, name="gdn_dw_main", ta=True, out_dtype=BF)
        dw_small = _matmul(hn, d_ba, name="gdn_dw_small", ta=True, out_dtype=BF)
        dw = jnp.concatenate([dw_main, dw_small[:, :2 * GDN_HEADS]], axis=1)
        grads['gdn_w_in'] = dw.reshape(d, N_CHIPS, dw.shape[1] // N_CHIPS).transpose(1, 0, 2)
        d_hn = _matmul(d_proj, w['gdn_w_main'], name="gdn_dhn_main", tb=True)
        return _matmul(d_ba, w['gdn_w_small'], name="gdn_dhn_small", tb=True, epilogue='add', extra=d_hn)
    if kind == 2:
        proj, glow, states = saved
        d_proj, d_glow4, d_wgu, grads['gla_gate_bias'], grads['gla_norm_gain'] = _gla_bwd(
            proj, glow, w['gla_w_gate_up'], w['gla_gate_bias'], w['gla_norm_gain'], states, d_og)
        grads['gla_w_gate_up'] = d_wgu[:GLA_RANK]
        dw_main = _matmul(hn, d_proj, name="gla_dw_main", ta=True, out_dtype=BF)
        dw_small4 = _matmul(hn, d_glow4, name="gla_dw_small", ta=True, out_dtype=F32)
        dw_small = dw_small4.reshape(d, GLA_HEADS, 128)[:, :, :GLA_RANK].sum(axis=1).astype(BF)
        dw = jnp.concatenate([dw_main, dw_small], axis=1)
        grads['gla_w_in'] = dw.reshape(d, N_CHIPS, dw.shape[1] // N_CHIPS).transpose(1, 0, 2)
        d_hn = _matmul(d_proj, w['gla_w_main'], name="gla_dhn_main", tb=True)
        w_small4 = jnp.tile(w['gla_w_small'], (1, GLA_HEADS))
        return _matmul(d_glow4, w_small4, name="gla_dhn_small", tb=True, epilogue='add', extra=d_hn)
    (proj,) = saved
    (d_proj, grads['lru_conv_w'], grads['lru_conv_b'], grads['lru_lambda'], grads['lru_w_rgate'], grads['lru_b_rgate'],
     grads['lru_w_igate'], grads['lru_b_igate']) = _lru_bwd(
        proj, w['lru_conv_w'], w['lru_conv_b'], w['lru_lambda'], w['lru_w_rgate'], w['lru_b_rgate'], w['lru_w_igate'],
        w['lru_b_igate'], d_og)
    grads['lru_w_in'] = _matmul(hn, d_proj, name="lru_dw_in", ta=True, out_dtype=BF, o_sharded=True)
    return _matmul(d_proj, w['lru_w_in'], name="lru_dhn", tb=True, b_sharded=True)


_W_OUT = ('ret_w_out', 'gdn_w_out', 'gla_w_out', 'lru_w_out')
_W_IN = ('ret_w_in', 'gdn_w_in', 'gla_w_in', 'lru_w_in')


def _layer_forward(layer, x, w, tables):
    hn = _rmsnorm_fwd(x, w['norm1'][layer], f"norm1_fwd_{layer}")
    og, mixer_saved = _mixer_forward(layer, hn, w, tables)
    x1 = _matmul(og, w[_W_OUT[layer]], name=f"mixer_out_{layer}", epilogue='add', extra=x)
    hn2 = _rmsnorm_fwd(x1, w['norm2'][layer], f"norm2_fwd_{layer}")
    act = _matmul(hn2, w['mlp_w_up'][layer], name="mlp_up", b_sharded=True, epilogue='relu2', out_dtype=BF)
    x2 = _matmul(act, w['mlp_w_down'][layer], name="mlp_down", epilogue='add', extra=x1)
    return x2, (x, hn, mixer_saved, og, x1, hn2, act)


def _layer_backward(layer, dx2, w, tables, saved):
    x, hn, mixer_saved, og, x1, hn2, act = saved
    d = x.shape[1]
    grads = {}
    d_up = _matmul(dx2, w['mlp_w_down'][layer], name="mlp_d_up", tb=True, epilogue='dact', extra=act, out_dtype=BF)
    dw_down = _matmul(act, dx2, name="mlp_dw_down", ta=True, out_dtype=BF)
    grads['mlp_w_down'] = dw_down.reshape(N_CHIPS, dw_down.shape[0] // N_CHIPS, d)
    grads['mlp_w_up'] = _matmul(hn2, d_up, name="mlp_dw_up", ta=True, out_dtype=BF, o_sharded=True)
    d_hn2 = _matmul(d_up, w['mlp_w_up'][layer], name="mlp_d_hn", tb=True, b_sharded=True)
    dx1, grads['norm2'] = _rmsnorm_bwd(x1, w['norm2'][layer], d_hn2, dx2, f"norm2_bwd_{layer}")
    w_out = w[_W_OUT[layer]]
    d_og = _matmul(dx1, w_out, name=f"mixer_d_og_{layer}", tb=True, out_dtype=BF)
    dw_out = _matmul(og, dx1, name=f"mixer_dw_out_{layer}", ta=True, out_dtype=BF)
    grads[_W_OUT[layer]] = dw_out.reshape(N_CHIPS, dw_out.shape[0] // N_CHIPS, d)
    d_hn = _mixer_backward(layer, hn, w, tables, mixer_saved, d_og, grads)
    dx, grads['norm1'] = _rmsnorm_bwd(x, w['norm1'][layer], d_hn, dx1, f"norm1_bwd_{layer}")
    return dx, grads


def _pack(arrays):
    flat = []
    for a in arrays:
        v = a.astype(F32).reshape(-1)
        v = jnp.pad(v, (0, (-v.shape[0]) % 128))
        flat.append(v.reshape(-1, 128))
    buf = jnp.concatenate(flat, axis=0)
    return jnp.pad(buf, ((0, (-buf.shape[0]) % 8), (0, 0)))


def _unpack(buf, shapes):
    lead = buf.shape[:-2]
    out, off = [], 0
    for shp in shapes:
        n = math.prod(shp)
        rows = -(-n // 128)
        piece = buf[..., off:off + rows, :].reshape(lead + (rows * 128,))[..., :n]
        out.append(piece.reshape(lead + tuple(shp)))
        off += rows
    return out


_WEIGHTS = ('norm1', 'norm2', 'final_norm', 'ret_w_in', 'ret_gn_gain', 'ret_w_out', 'gdn_w_in', 'gdn_conv_w',
            'gdn_a_log', 'gdn_dt_bias', 'gdn_norm_gain', 'gdn_w_out', 'gla_w_in', 'gla_w_gate_up', 'gla_gate_bias',
            'gla_norm_gain', 'gla_w_out', 'lru_w_in', 'lru_conv_w', 'lru_conv_b', 'lru_w_rgate', 'lru_b_rgate',
            'lru_w_igate', 'lru_b_igate', 'lru_lambda', 'lru_w_out', 'mlp_w_up', 'mlp_w_down')
_FWD_PARAMS = ('x',) + _WEIGHTS
_BIG = ('ret_w_in', 'ret_w_out', 'gdn_w_in', 'gdn_w_out', 'gla_w_in', 'gla_w_out', 'lru_w_in', 'lru_w_out',
        'mlp_w_up', 'mlp_w_down')
_SMALL = tuple(n for n in _WEIGHTS if n not in _BIG)
_SMALL_SHARDED = ('ret_gn_gain', 'gdn_conv_w', 'gla_w_gate_up', 'gla_gate_bias', 'gla_norm_gain', 'lru_conv_w',
                  'lru_conv_b', 'lru_lambda')


def kernel(*args):
    names = _FWD_PARAMS + ('loss_target',) + tuple('m_' + n for n in _WEIGHTS) + tuple('v_' + n for n in _WEIGHTS)
    assert len(args) == len(names)
    a = dict(zip(names, args))
    x = a['x'][0]
    target = a['loss_target'][0]
    s, d = x.shape
    chip = 2 * lax.axis_index("x") + lax.axis_index("y")

    small_local = [a[n][0] if a[n].ndim == 3 else a[n] for n in _SMALL_SHARDED]
    small_pack = _pack(small_local)
    gathered = {}
    for layer in range(4):
        w_in, w_out = _W_IN[layer], _W_OUT[layer]
        ops = [a[w_in][0].astype(BF), a[w_out][0].astype(BF), a['mlp_w_up'][layer].astype(BF),
               a['mlp_w_down'][layer].astype(BF)]
        split = [True] * 4
        if layer == 0:
            ops.append(small_pack)
            split.append(False)
        res = _all_gather_chips(ops, split, f"gather_weights_{layer}")
        res = [lax.dynamic_update_slice_in_dim(r, o[None], chip, axis=0) for r, o in zip(res, ops)]
        gathered[w_in], gathered[w_out] = res[0], res[1]
        gathered[('mlp_w_up', layer)], gathered[('mlp_w_down', layer)] = res[2], res[3]
        if layer == 0:
            small_all = res[4]

    def whole_cols(g):
        return g.transpose(1, 0, 2).reshape(g.shape[1], N_CHIPS * g.shape[2])

    def whole_rows(g):
        return g.reshape(N_CHIPS * g.shape[1], g.shape[2])

    w = {}
    for n, piece in zip(_SMALL_SHARDED, _unpack(small_all, [p.shape for p in small_local])):
        w[n] = whole_cols(piece)
    for n in _SMALL:
        if n not in _SMALL_SHARDED:
            w[n] = a[n][0] if n.startswith('lru_') else a[n]
    w['gdn_a_log'], w['gdn_dt_bias'] = _pad_cols(w['gdn_a_log']), _pad_cols(w['gdn_dt_bias'])
    w['gla_w_gate_up'] = jnp.pad(w['gla_w_gate_up'], ((0, 128 - GLA_RANK), (0, 0)))
    nb = LRU_BLOCKS
    w['lru_b_rgate'] = w['lru_b_rgate'].reshape(nb, 1, -1)
    w['lru_b_igate'] = w['lru_b_igate'].reshape(nb, 1, -1)
    w['ret_w_in'] = gathered['ret_w_in']
    w['lru_w_in'] = gathered['lru_w_in']
    gdn_full = whole_cols(gathered['gdn_w_in'])
    n_gdn = gdn_full.shape[1] - 2 * GDN_HEADS
    w['gdn_w_main'], w['gdn_w_small'] = gdn_full[:, :n_gdn], _pad_cols(gdn_full[:, n_gdn:])
    gla_full = whole_cols(gathered['gla_w_in'])
    n_gla = gla_full.shape[1] - GLA_RANK
    w['gla_w_main'], w['gla_w_small'] = gla_full[:, :n_gla], _pad_cols(gla_full[:, n_gla:])
    for n in _W_OUT:
        w[n] = whole_rows(gathered[n])
    w['mlp_w_up'] = [gathered[('mlp_w_up', layer)] for layer in range(4)]
    w['mlp_w_down'] = [whole_rows(gathered[('mlp_w_down', layer)]) for layer in range(4)]

    tables = _ret_tables(s, d // RET_HEADS)
    saved = []
    h = x
    for layer in range(4):
        h, sv = _layer_forward(layer, h, w, tables)
        saved.append(sv)
    loss_part, dh, g_final = _loss_head(h, w['final_norm'], target)
    loss = lax.psum(loss_part, ("x", "y", "c"))

    grad, delta, new_m, new_v = {}, {}, {}, {}
    small_grads = {'final_norm': g_final}
    norm_grads = {'norm1': [None] * 4, 'norm2': [None] * 4}
    core_arr = lax.axis_index("c").astype(jnp.int32).reshape(1)
    chip_arr = chip.astype(jnp.int32).reshape(1)
    mlp_upd = {'mlp_w_up': None, 'mlp_w_down': None}
    for layer in reversed(range(4)):
        dh, g = _layer_backward(layer, dh, w, tables, saved[layer])
        saved[layer] = None
        w_in, w_out = _W_IN[layer], _W_OUT[layer]
        red = _reduce_scatter([g[w_in], g[w_out], g['mlp_w_up'], g['mlp_w_down']], str(layer), core_arr, chip_arr)
        for n, (mine, theirs) in ((w_in, red[0]), (w_out, red[1])):
            grad[n], delta[n], new_m[n], new_v[n] = _adamw_halves(a[n], a['m_' + n], a['v_' + n], 0, mine, theirs,
                                                                  core_arr, f"adamw_{n}")
        for n, (mine, theirs) in (('mlp_w_up', red[2]), ('mlp_w_down', red[3])):
            mlp_upd[n] = _adamw_halves(a[n], a['m_' + n], a['v_' + n], layer, mine, theirs, core_arr,
                                       f"adamw_{n}_{layer}", carried=mlp_upd[n])
        norm_grads['norm1'][layer], norm_grads['norm2'][layer] = g['norm1'], g['norm2']
        for n in _SMALL:
            if n in g:
                small_grads[n] = g[n]
    for n in ('mlp_w_up', 'mlp_w_down'):
        grad[n], delta[n], new_m[n], new_v[n] = mlp_upd[n]
    small_grads['norm1'] = jnp.stack(norm_grads['norm1'])
    small_grads['norm2'] = jnp.stack(norm_grads['norm2'])

    full_shapes = [small_grads[n].shape for n in _SMALL]
    total = _sum_slabs(_all_gather_devices(_pack([small_grads[n] for n in _SMALL]), "gather_small_grads"), F32,
                       "sum_small_grads")
    local_g = {}
    for n, full in zip(_SMALL, _unpack(total, full_shapes)):
        shp = a[n].shape
        if n in _SMALL_SHARDED:
            full = full.reshape(full.shape[0], -1)
            cq = shp[-1]
            full = lax.dynamic_slice_in_dim(full, chip * cq, cq, axis=1)
        elif n in ('gdn_a_log', 'gdn_dt_bias'):
            full = full[:, :shp[-1]]
        local_g[n] = full.reshape(shp)
    shapes = [a[n].shape for n in _SMALL]
    packed = [_pack([src[n] for n in _SMALL]) for src in
              (a, local_g, {n: a['m_' + n] for n in _SMALL}, {n: a['v_' + n] for n in _SMALL})]
    upd = _adamw(*packed, "adamw_small")
    for n, gr, dl, nm, nv in zip(_SMALL, _unpack(packed[1], shapes), *[_unpack(u, shapes) for u in upd]):
        grad[n], delta[n], new_m[n], new_v[n] = gr, dl, nm, nv

    out = [loss, dh.reshape(a['x'].shape)]
    for group in (grad, delta, new_m, new_v):
        out += [group[n].reshape(a[n].shape) for n in _WEIGHTS]
    return tuple(out)
```

```python
import functools
import math

import jax
import jax.numpy as jnp
from jax import lax
from jax.experimental import pallas as pl
from jax.experimental.pallas import tpu as pltpu

F32 = jnp.float32
BF = jnp.bfloat16
MESH = pl.DeviceIdType.MESH

NORM_EPS = 1e-6
CHUNK = 64
RET_HEADS = 8
GDN_HEADS = 16
GLA_HEADS = 4
GLA_RANK = 16
GLA_TAU = 16.0
LRU_BLOCKS = 16
LRU_C = 8.0
CONV_WIDTH = 4
ROPE_BASE = 10000.0
N_CHIPS = 4
N_DEV = 8

ADAM_LR = 0.001
ADAM_B1 = 0.9
ADAM_B2 = 0.999
ADAM_EPS = 1e-08
ADAM_WD = 0.01
ADAM_STEP = 10

VMEM_LIMIT_BYTES = 56 * 1024 * 1024
TOKEN_BLOCK = 256
ROW_BLOCK = 256
GDN_GROUP = 8


def _cparams(**kw):
    return pltpu.CompilerParams(vmem_limit_bytes=VMEM_LIMIT_BYTES, **kw)


def _bf16_parts(x, n):
    parts = []
    for _ in range(n - 1):
        p = x.astype(BF)
        parts.append(p)
        x = x - p.astype(F32)
    return parts + [x.astype(BF)]


def _raw_mm(a, b, ta, tb, hi):
    nb = a.ndim - 2
    batch = tuple(range(nb))
    dims = (((nb + (0 if ta else 1),), (nb + (1 if tb else 0),)), (batch, batch))

    def dot(p, q):
        return lax.dot_general(p, q, dims, preferred_element_type=F32)

    if not hi:
        return dot(a.astype(BF), b.astype(BF))
    if hi == 'l':
        ae = a.astype(BF)
        b0, b1, b2 = _bf16_parts(b.astype(F32), 3)
        return dot(ae, b0) + (dot(ae, b1) + dot(ae, b2))
    if hi == 'r':
        be = b.astype(BF)
        a0, a1, a2 = _bf16_parts(a.astype(F32), 3)
        return dot(a0, be) + (dot(a1, be) + dot(a2, be))
    a0, a1 = _bf16_parts(a.astype(F32), 2)
    b0, b1 = _bf16_parts(b.astype(F32), 2)
    return dot(a0, b0) + (dot(a0, b1) + dot(a1, b0))


@functools.partial(jax.custom_vjp, nondiff_argnums=(2, 3, 4))
def _mm_vjp(a, b, ta, tb, hi):
    return _raw_mm(a, b, ta, tb, hi)


def _mm_vjp_fwd(a, b, ta, tb, hi):
    return _raw_mm(a, b, ta, tb, hi), (a, b)


def _mm_vjp_bwd(ta, tb, hi, res, g):
    a, b = res
    if ta:
        da = _raw_mm(b, g, tb, True, 'l' if hi == 'r' else bool(hi))
    else:
        da = _raw_mm(g, b, False, not tb, 'r' if hi == 'r' else bool(hi))
    if tb:
        db = _raw_mm(g, a, True, ta, 'r' if hi == 'l' else bool(hi))
    else:
        db = _raw_mm(a, g, not ta, False, 'l' if hi == 'l' else bool(hi))
    return da, db


_mm_vjp.defvjp(_mm_vjp_fwd, _mm_vjp_bwd)


def _mm_diff(a, b, ta=False, tb=False, hi=False):
    return _mm_vjp(a, b, ta, tb, hi)


def _mm_plain(a, b, ta=False, tb=False, hi=False):
    return _raw_mm(a, b, ta, tb, hi)


def _shift_rows(x, k, up):
    if k == 0:
        return x
    n = x.shape[0]
    rows = lax.broadcasted_iota(jnp.int32, x.shape, 0)
    if up:
        return jnp.where(rows < n - k, pltpu.roll(x, n - k, 0), 0.0)
    return jnp.where(rows >= k, pltpu.roll(x, k, 0), 0.0)


@functools.partial(jax.custom_vjp, nondiff_argnums=(1, 2))
def _shift_vjp(x, k, up):
    return _shift_rows(x, k, up)


def _shift_vjp_fwd(x, k, up):
    return _shift_rows(x, k, up), None


def _shift_vjp_bwd(k, up, _, g):
    return (_shift_rows(g, k, not up),)


_shift_vjp.defvjp(_shift_vjp_fwd, _shift_vjp_bwd)


def _sigmoid(x):
    return 1.0 / (1.0 + jnp.exp(-x))


def _silu(x):
    return x * _sigmoid(x)


def _softplus(x):
    return jnp.maximum(x, 0.0) + jnp.log(1.0 + jnp.exp(-jnp.abs(x)))


def _gelu_tanh(x):
    return 0.5 * x * (1.0 + jnp.tanh(math.sqrt(2.0 / math.pi) * (x + 0.044715 * (x * x * x))))


def _expm1(x):
    series = x * (1.0 + x * (0.5 + x * (1.0 / 6.0 + x * (1.0 / 24.0))))
    return jnp.where(jnp.abs(x) < 0.03, series, jnp.exp(x) - 1.0)


def _rmsnorm(x, g):
    return x * lax.rsqrt(jnp.mean(x * x, axis=-1, keepdims=True) + NORM_EPS) * g


def _head_norm(o, gain, center):
    if center:
        o = o - jnp.mean(o, axis=-1, keepdims=True)
    return o * lax.rsqrt(jnp.mean(o * o, axis=-1, keepdims=True) + NORM_EPS) * gain


def _l2norm(x):
    return x * lax.rsqrt(jnp.sum(x * x, axis=-1, keepdims=True) + NORM_EPS)


def _iota2(shape, dim):
    return lax.broadcasted_iota(jnp.int32, shape, dim)


def _tri_ones(n, upper=False):
    i, j = _iota2((n, n), 0), _iota2((n, n), 1)
    return jnp.where((i <= j) if upper else (j <= i), 1.0, 0.0).astype(F32)


def _linscan(a, u, rev):
    n = a.shape[0]
    rows = _iota2(a.shape, 0)
    d = 1
    while d < n:
        if rev:
            valid = rows < n - d
            a_s, u_s = pltpu.roll(a, n - d, 0), pltpu.roll(u, n - d, 0)
        else:
            valid = rows >= d
            a_s, u_s = pltpu.roll(a, d, 0), pltpu.roll(u, d, 0)
        u = a * jnp.where(valid, u_s, 0.0) + u
        a = a * jnp.where(valid, a_s, 1.0)
        d *= 2
    return u


def _ret_chunk(mm, q, k, v, g, gain, state, cos, sin, dintra, qdec, kdec, cdec):
    dk = q.shape[-1]
    half = dk // 2

    def rot(t):
        t1, t2 = t[:, :half], t[:, half:]
        return jnp.concatenate([t1 * cos - t2 * sin, t1 * sin + t2 * cos], axis=-1)

    qr = rot(q)
    kr = rot(k) * (dk ** -0.5)
    scores = mm(qr, kr, tb=True) * dintra
    o = mm(scores, v) + mm(qr * qdec, state)
    new_state = state * cdec + mm(kr * kdec, v, ta=True)
    y = _head_norm(o, gain, True) * _silu(g)
    return y, new_state


def _gla_chunk(mm, q, k, v, r, glow, wgu, bias, gain, state_t):
    dk = q.shape[-1]
    c = q.shape[0]
    logit = mm(glow, wgu) + bias
    la = -_softplus(-logit) * (1.0 / GLA_TAU)
    cum = mm(_tri_ones(c), la, hi='l')
    rows = _iota2(la.shape, 0)
    ref = jnp.sum(jnp.where(rows < c // 2, la, 0.0), axis=0, keepdims=True)
    tot = jnp.sum(la, axis=0, keepdims=True)
    fwd, bwd = jnp.exp(cum - ref), jnp.exp(ref - cum)
    qs = q * (dk ** -0.5)
    s_lo = mm(qs * fwd, k * bwd, tb=True)
    s_up = mm(qs * bwd, k * fwd, tb=True)
    i, j = _iota2((c, c), 0), _iota2((c, c), 1)
    scores = jnp.where(i >= j, s_lo, s_up)
    o = mm(scores, v) + mm(qs * jnp.exp(cum), state_t, tb=True)
    k_end = k * jnp.exp(tot - cum)
    new_state_t = state_t * jnp.exp(tot) + mm(v, k_end, ta=True)
    y = _head_norm(o, gain, False) * _silu(r)
    return y, new_state_t


def _gdn_group(mm, q, k, v, z, ba, a_log, dt_bias, gain, state, head):
    g, c, dk = q.shape
    lanes = lax.broadcasted_iota(jnp.int32, (1, 1, ba.shape[-1]), 2)
    oh_b = jnp.where(lanes == head, 1.0, 0.0).astype(F32)
    oh_a = jnp.where(lanes == head + GDN_HEADS, 1.0, 0.0).astype(F32)
    beta = _sigmoid(jnp.sum(ba * oh_b, axis=-1, keepdims=True))
    a_logit = jnp.sum(ba * oh_a, axis=-1, keepdims=True)
    a_h = jnp.sum(a_log * oh_b[0], axis=-1, keepdims=True)
    dt_h = jnp.sum(dt_bias * oh_b[0], axis=-1, keepdims=True)
    la = -jnp.exp(a_h) * _softplus(a_logit + dt_h)
    q = _l2norm(q) * (dk ** -0.5)
    k = _l2norm(k)
    tri = jnp.broadcast_to(_tri_ones(c), (g, c, c))
    tri_up = jnp.broadcast_to(_tri_ones(c, upper=True), (g, c, c))
    cum_k = mm(tri, la * jnp.ones((g, c, dk), F32), hi='l')
    la_sq = la * jnp.ones((g, c, c), F32)
    cum_i = mm(tri, la_sq, hi='l')
    cum_j = mm(la_sq, tri_up, ta=True, hi='r')
    i, j = _iota2((c, c), 0), _iota2((c, c), 1)
    strict = i > j
    rel = jnp.where(strict, jnp.exp(jnp.where(strict, cum_i - cum_j, 0.0)), 0.0)
    a_mat = beta * rel * mm(k, k, tb=True)
    inv = jnp.where(i == j, 1.0, 0.0).astype(F32) - a_mat
    power = a_mat
    for _ in range(int(math.log2(c)) - 1):
        power = mm(power, power, hi=True)
        inv = inv + mm(inv, power, hi=True)
    tot = jnp.sum(la, axis=1, keepdims=True)
    u = mm(inv, beta * v, hi=True)
    w = mm(inv, (beta * jnp.exp(cum_k)) * k, hi=True)
    k_end = k * jnp.exp(tot - cum_k)
    di, dj = _iota2((dk, dk), 0), _iota2((dk, dk), 1)
    trans = jnp.exp(tot) * jnp.where(di == dj, 1.0, 0.0).astype(F32) - mm(k_end, w, ta=True)
    inject = mm(k_end, u, ta=True)
    ys = []
    for n in range(g):
        state = mm(trans[n], state) + inject[n]
        ys.append(_head_norm(mm(q[n], state), gain, False) * _silu(z[n]))
    return ys, state


def _conv_taps(shift, x, taps):
    out = None
    for tap, w in enumerate(taps):
        term = shift(x, CONV_WIDTH - 1 - tap, False) * w
        out = term if out is None else out + term
    return out


def _lru_pre(mm, shift, xb, yb, taps, cb, lam, wr, br, wi, bi):
    xb = _conv_taps(shift, xb, taps) + cb
    r = _sigmoid(mm(xb, wr) + br)
    i = _sigmoid(mm(xb, wi) + bi)
    log_a = (-LRU_C) * _softplus(-lam) * r
    a = jnp.exp(log_a)
    u = jnp.sqrt(-_expm1(2.0 * log_a)) * (i * xb)
    return a, u, _gelu_tanh(yb)


def _pick(n, pref):
    for t in (pref, 2048, 1024, 512, 256, 128):
        if t <= n and n % t == 0:
            return t
    return n


def _matmul(a, b, *, name, ta=False, tb=False, out_dtype=F32, b_sharded=False, o_sharded=False,
            epilogue=None, extra=None, bm=512, bn=512, bk=2048):
    m, kdim = (a.shape[1], a.shape[0]) if ta else a.shape
    if b_sharded:
        nq = b.shape[2]
        n = b.shape[1] if tb else N_CHIPS * nq
        assert kdim == (N_CHIPS * nq if tb else b.shape[1])
    else:
        n = b.shape[0] if tb else b.shape[1]
        assert kdim == (b.shape[1] if tb else b.shape[0])
    bm, bk = _pick(m, bm), _pick(kdim, bk)
    if b_sharded and tb:
        bk = _pick(b.shape[2], bk)
    bn = _pick(b.shape[2] if (b_sharded and not tb) else (n // N_CHIPS if o_sharded else n), bn)
    nk = kdim // bk
    grid = (m // bm, n // bn, nk)

    a_spec = pl.BlockSpec((bk, bm), lambda i, j, k: (k, i)) if ta else pl.BlockSpec((bm, bk), lambda i, j, k: (i, k))
    if b_sharded and not tb:
        per = b.shape[2] // bn
        b_spec = pl.BlockSpec((None, bk, bn), lambda i, j, k: (j // per, k, j % per))
    elif b_sharded:
        per = b.shape[2] // bk
        b_spec = pl.BlockSpec((None, bn, bk), lambda i, j, k: (k // per, j, k % per))
    elif tb:
        b_spec = pl.BlockSpec((bn, bk), lambda i, j, k: (j, k))
    else:
        b_spec = pl.BlockSpec((bk, bn), lambda i, j, k: (k, j))
    if o_sharded:
        per_o = (n // N_CHIPS) // bn
        o_spec = pl.BlockSpec((None, bm, bn), lambda i, j, k: (j // per_o, i, j % per_o))
        out_shape = jax.ShapeDtypeStruct((N_CHIPS, m, n // N_CHIPS), out_dtype)
    else:
        o_spec = pl.BlockSpec((bm, bn), lambda i, j, k: (i, j))
        out_shape = jax.ShapeDtypeStruct((m, n), out_dtype)
    in_specs = [a_spec, b_spec]
    operands = [a, b]
    if extra is not None:
        in_specs.append(pl.BlockSpec((bm, bn), lambda i, j, k: (i, j)))
        operands.append(extra)

    def body(*refs):
        a_ref, b_ref = refs[0], refs[1]
        e_ref = refs[2] if extra is not None else None
        o_ref, acc_ref = refs[-2], refs[-1]
        k = pl.program_id(2)

        @pl.when(k == 0)
        def _():
            acc_ref[...] = jnp.zeros_like(acc_ref)

        acc_ref[...] += _raw_mm(a_ref[...], b_ref[...], ta, tb, False)

        @pl.when(k == nk - 1)
        def _():
            acc = acc_ref[...]
            if epilogue == 'add':
                acc = acc + e_ref[...].astype(F32)
            elif epilogue == 'relu2':
                acc = jnp.square(jnp.maximum(acc, 0.0))
            elif epilogue == 'dact':
                acc = acc * (2.0 * jnp.sqrt(e_ref[...].astype(F32)))
            o_ref[...] = acc.astype(o_ref.dtype)

    return pl.pallas_call(
        body, name=name, grid=grid, in_specs=in_specs, out_specs=o_spec, out_shape=out_shape,
        scratch_shapes=[pltpu.VMEM((bm, bn), F32)], compiler_params=_cparams(),
    )(*operands)


def _rmsnorm_fwd(x, g, name):
    s, d = x.shape

    def body(x_ref, g_ref, o_ref):
        o_ref[...] = _rmsnorm(x_ref[...], g_ref[...]).astype(BF)

    return pl.pallas_call(
        body, name=name, grid=(s // ROW_BLOCK,),
        in_specs=[pl.BlockSpec((ROW_BLOCK, d), lambda i: (i, 0)), pl.BlockSpec((1, d), lambda i: (0, 0))],
        out_specs=pl.BlockSpec((ROW_BLOCK, d), lambda i: (i, 0)),
        out_shape=jax.ShapeDtypeStruct((s, d), BF), compiler_params=_cparams(),
    )(x, g.reshape(1, d))


def _rmsnorm_bwd(x, g, dh, dres, name):
    s, d = x.shape

    def body(x_ref, g_ref, dh_ref, dres_ref, dx_ref, dg_ref):
        _, vjp = jax.vjp(_rmsnorm, x_ref[...], g_ref[...])
        dx, dg = vjp(dh_ref[...].astype(F32))
        dx_ref[...] = dres_ref[...] + dx

        @pl.when(pl.program_id(0) == 0)
        def _():
            dg_ref[...] = jnp.zeros_like(dg_ref)

        dg_ref[...] += dg

    row = pl.BlockSpec((ROW_BLOCK, d), lambda i: (i, 0))
    vec = pl.BlockSpec((1, d), lambda i: (0, 0))
    dx, dg = pl.pallas_call(
        body, name=name, grid=(s // ROW_BLOCK,), in_specs=[row, vec, row, row], out_specs=[row, vec],
        out_shape=[jax.ShapeDtypeStruct((s, d), F32), jax.ShapeDtypeStruct((1, d), F32)],
        compiler_params=_cparams(),
    )(x, g.reshape(1, d), dh, dres)
    return dx, dg.reshape(d)


def _loss_head(x, g, target):
    s, d = x.shape

    def loss_fn(xv, gv, tv):
        err = _rmsnorm(xv, gv) - tv
        return 0.5 * jnp.sum(jnp.mean(err * err, axis=-1, keepdims=True), axis=0, keepdims=True)

    def body(x_ref, g_ref, t_ref, dx_ref, dg_ref, loss_ref):
        tv = t_ref[...]
        loss, vjp = jax.vjp(lambda xv, gv: loss_fn(xv, gv, tv), x_ref[...], g_ref[...])
        dx, dg = vjp(jnp.ones((1, 1), F32))
        dx_ref[...] = dx

        @pl.when(pl.program_id(0) == 0)
        def _():
            dg_ref[...] = jnp.zeros_like(dg_ref)
            loss_ref[...] = jnp.zeros_like(loss_ref)

        dg_ref[...] += dg
        loss_ref[...] += loss * jnp.ones_like(loss_ref)

    row = pl.BlockSpec((ROW_BLOCK, d), lambda i: (i, 0))
    vec = pl.BlockSpec((1, d), lambda i: (0, 0))
    dx, dg, loss = pl.pallas_call(
        body, name="loss_head", grid=(s // ROW_BLOCK,), in_specs=[row, vec, row],
        out_specs=[row, vec, pl.BlockSpec((1, 128), lambda i: (0, 0))],
        out_shape=[jax.ShapeDtypeStruct((s, d), F32), jax.ShapeDtypeStruct((1, d), F32),
                   jax.ShapeDtypeStruct((1, 128), F32)],
        compiler_params=_cparams(),
    )(x, g.reshape(1, d), target)
    return loss[0, 0], dx, dg.reshape(d)


def _ret_tables(s, dk):
    h = jnp.arange(RET_HEADS, dtype=F32)
    log_gamma = jnp.log1p(-jnp.exp2(-5.0 - h))
    pos = jnp.arange(CHUNK, dtype=F32)
    dist = jnp.abs(pos[:, None] - pos[None, :])
    dintra = jnp.exp(log_gamma[:, None, None] * dist)
    qdec = jnp.exp(log_gamma[:, None] * (pos + 1.0))[:, :, None]
    kdec = jnp.exp(log_gamma[:, None] * (CHUNK - 1.0 - pos))[:, :, None]
    cdec = jnp.exp(log_gamma * CHUNK)[:, None, None]
    inv = ROPE_BASE ** (-jnp.arange(0, dk, 2, dtype=F32) / dk)
    ang = jnp.arange(s, dtype=F32)[:, None] * inv[None, :]
    return jnp.cos(ang), jnp.sin(ang), dintra, qdec, kdec, cdec


def _ret_specs(s, dk, dv, tb, rev):
    nh = RET_HEADS
    nb = s // tb
    bi = (lambda b: nb - 1 - b) if rev else (lambda b: b)
    voff = 2 * nh * dk // dv
    cpb = tb // CHUNK
    return dict(
        q=pl.BlockSpec((tb, dk), lambda h, b: (bi(b), h)),
        k=pl.BlockSpec((tb, dk), lambda h, b: (bi(b), nh + h)),
        v=pl.BlockSpec((tb, dv), lambda h, b: (bi(b), voff + h)),
        g=pl.BlockSpec((tb, dv), lambda h, b: (bi(b), voff + nh + h)),
        gain=pl.BlockSpec((None, 1, dv), lambda h, b: (h, 0, 0)),
        cs=pl.BlockSpec((tb, dk // 2), lambda h, b: (bi(b), 0)),
        dintra=pl.BlockSpec((None, CHUNK, CHUNK), lambda h, b: (h, 0, 0)),
        dec=pl.BlockSpec((None, CHUNK, 1), lambda h, b: (h, 0, 0)),
        cdec=pl.BlockSpec((None, 1, 1), lambda h, b: (h, 0, 0)),
        hv=pl.BlockSpec((tb, dv), lambda h, b: (bi(b), h)),
        hk=pl.BlockSpec((tb, dk), lambda h, b: (bi(b), h)),
        st=pl.BlockSpec((None, cpb, dk, dv), lambda h, b: (h, bi(b), 0, 0)),
    )


def _ret_fwd(proj, gain, tables):
    s = proj.shape[0]
    d = proj.shape[1] // 6
    dk, dv = d // RET_HEADS, 2 * d // RET_HEADS
    tb = min(TOKEN_BLOCK, s)
    cpb = tb // CHUNK
    sp = _ret_specs(s, dk, dv, tb, False)
    cos, sin, dintra, qdec, kdec, cdec = tables

    def body(q_ref, k_ref, v_ref, g_ref, gain_ref, cos_ref, sin_ref, di_ref, qd_ref, kd_ref, cd_ref,
             y_ref, st_ref, state):
        @pl.when(pl.program_id(1) == 0)
        def _():
            state[...] = jnp.zeros_like(state)

        for c in range(cpb):
            sl = pl.ds(c * CHUNK, CHUNK)
            st_ref[c] = state[...]
            y, new_state = _ret_chunk(_mm_plain, q_ref[sl, :], k_ref[sl, :], v_ref[sl, :], g_ref[sl, :],
                                      gain_ref[...], state[...], cos_ref[sl, :], sin_ref[sl, :],
                                      di_ref[...], qd_ref[...], kd_ref[...], cd_ref[...])
            y_ref[sl, :] = y.astype(BF)
            state[...] = new_state

    return pl.pallas_call(
        body, name="ret_fwd", grid=(RET_HEADS, s // tb),
        in_specs=[sp['q'], sp['k'], sp['v'], sp['g'], sp['gain'], sp['cs'], sp['cs'], sp['dintra'], sp['dec'],
                  sp['dec'], sp['cdec']],
        out_specs=[sp['hv'], sp['st']],
        out_shape=[jax.ShapeDtypeStruct((s, RET_HEADS * dv), BF),
                   jax.ShapeDtypeStruct((RET_HEADS, s // CHUNK, dk, dv), F32)],
        scratch_shapes=[pltpu.VMEM((dk, dv), F32)], compiler_params=_cparams(),
    )(proj, proj, proj, proj, gain.reshape(RET_HEADS, 1, dv), cos, sin, dintra, qdec, kdec, cdec)


def _ret_bwd(proj, gain, tables, states, dy):
    s = proj.shape[0]
    d = proj.shape[1] // 6
    dk, dv = d // RET_HEADS, 2 * d // RET_HEADS
    tb = min(TOKEN_BLOCK, s)
    cpb = tb // CHUNK
    sp = _ret_specs(s, dk, dv, tb, True)
    cos, sin, dintra, qdec, kdec, cdec = tables

    def body(q_ref, k_ref, v_ref, g_ref, gain_ref, cos_ref, sin_ref, di_ref, qd_ref, kd_ref, cd_ref, st_ref,
             dy_ref, dq_ref, dk_ref, dv_ref, dg_ref, dgain_ref, dstate):
        @pl.when(pl.program_id(1) == 0)
        def _():
            dstate[...] = jnp.zeros_like(dstate)
            dgain_ref[...] = jnp.zeros_like(dgain_ref)

        for c in reversed(range(cpb)):
            sl = pl.ds(c * CHUNK, CHUNK)
            cos_c, sin_c = cos_ref[sl, :], sin_ref[sl, :]
            di, qd, kd, cd = di_ref[...], qd_ref[...], kd_ref[...], cd_ref[...]

            def fn(q, k, v, g, gn, st):
                return _ret_chunk(_mm_diff, q, k, v, g, gn, st, cos_c, sin_c, di, qd, kd, cd)

            _, vjp = jax.vjp(fn, q_ref[sl, :], k_ref[sl, :], v_ref[sl, :], g_ref[sl, :], gain_ref[...],
                             st_ref[c])
            dq, dkk, dvv, dg, dgn, dst = vjp((dy_ref[sl, :].astype(F32), dstate[...]))
            dq_ref[sl, :] = dq.astype(BF)
            dk_ref[sl, :] = dkk.astype(BF)
            dv_ref[sl, :] = dvv.astype(BF)
            dg_ref[sl, :] = dg.astype(BF)
            dgain_ref[...] += dgn
            dstate[...] = dst

    dq, dkk, dvv, dg, dgain = pl.pallas_call(
        body, name="ret_bwd", grid=(RET_HEADS, s // tb),
        in_specs=[sp['q'], sp['k'], sp['v'], sp['g'], sp['gain'], sp['cs'], sp['cs'], sp['dintra'], sp['dec'],
                  sp['dec'], sp['cdec'], sp['st'], sp['hv']],
        out_specs=[sp['hk'], sp['hk'], sp['hv'], sp['hv'], sp['gain']],
        out_shape=[jax.ShapeDtypeStruct((s, RET_HEADS * dk), BF), jax.ShapeDtypeStruct((s, RET_HEADS * dk), BF),
                   jax.ShapeDtypeStruct((s, RET_HEADS * dv), BF), jax.ShapeDtypeStruct((s, RET_HEADS * dv), BF),
                   jax.ShapeDtypeStruct((RET_HEADS, 1, dv), F32)],
        scratch_shapes=[pltpu.VMEM((dk, dv), F32)], compiler_params=_cparams(),
    )(proj, proj, proj, proj, gain.reshape(RET_HEADS, 1, dv), cos, sin, dintra, qdec, kdec, cdec, states, dy)
    return jnp.concatenate([dq, dkk, dvv, dg], axis=1), dgain.reshape(RET_HEADS, dv)


def _gla_specs(s, dk, dv, tb, rev):
    nh = GLA_HEADS
    nb = s // tb
    bi = (lambda b: nb - 1 - b) if rev else (lambda b: b)
    voff = 2 * nh * dk // dv
    cpb = tb // CHUNK
    return dict(
        q=pl.BlockSpec((tb, dk), lambda h, b: (bi(b), h)),
        k=pl.BlockSpec((tb, dk), lambda h, b: (bi(b), nh + h)),
        v=pl.BlockSpec((tb, dv), lambda h, b: (bi(b), voff + h)),
        r=pl.BlockSpec((tb, dv), lambda h, b: (bi(b), voff + nh + h)),
        glow=pl.BlockSpec((tb, 128), lambda h, b: (bi(b), 0)),
        wgu=pl.BlockSpec((128, dk), lambda h, b: (0, h)),
        bias=pl.BlockSpec((1, dk), lambda h, b: (0, h)),
        gain=pl.BlockSpec((None, 1, dv), lambda h, b: (h, 0, 0)),
        hv=pl.BlockSpec((tb, dv), lambda h, b: (bi(b), h)),
        hk=pl.BlockSpec((tb, dk), lambda h, b: (bi(b), h)),
        hg=pl.BlockSpec((tb, 128), lambda h, b: (bi(b), h)),
        st=pl.BlockSpec((None, cpb, dv, dk), lambda h, b: (h, bi(b), 0, 0)),
    )


def _gla_fwd(proj, glow, wgu, bias, gain):
    s = proj.shape[0]
    d = proj.shape[1] // 3
    dk, dv = d // 2 // GLA_HEADS, d // GLA_HEADS
    tb = min(TOKEN_BLOCK, s)
    cpb = tb // CHUNK
    sp = _gla_specs(s, dk, dv, tb, False)

    def body(q_ref, k_ref, v_ref, r_ref, gl_ref, wgu_ref, b_ref, gain_ref, y_ref, st_ref, state):
        @pl.when(pl.program_id(1) == 0)
        def _():
            state[...] = jnp.zeros_like(state)

        for c in range(cpb):
            sl = pl.ds(c * CHUNK, CHUNK)
            st_ref[c] = state[...]
            y, new_state = _gla_chunk(_mm_plain, q_ref[sl, :], k_ref[sl, :], v_ref[sl, :], r_ref[sl, :],
                                      gl_ref[sl, :], wgu_ref[...], b_ref[...], gain_ref[...], state[...])
            y_ref[sl, :] = y.astype(BF)
            state[...] = new_state

    return pl.pallas_call(
        body, name="gla_fwd", grid=(GLA_HEADS, s // tb),
        in_specs=[sp['q'], sp['k'], sp['v'], sp['r'], sp['glow'], sp['wgu'], sp['bias'], sp['gain']],
        out_specs=[sp['hv'], sp['st']],
        out_shape=[jax.ShapeDtypeStruct((s, GLA_HEADS * dv), BF),
                   jax.ShapeDtypeStruct((GLA_HEADS, s // CHUNK, dv, dk), F32)],
        scratch_shapes=[pltpu.VMEM((dv, dk), F32)], compiler_params=_cparams(),
    )(proj, proj, proj, proj, glow, wgu, bias, gain.reshape(GLA_HEADS, 1, dv))


def _gla_bwd(proj, glow, wgu, bias, gain, states, dy):
    s = proj.shape[0]
    d = proj.shape[1] // 3
    dk, dv = d // 2 // GLA_HEADS, d // GLA_HEADS
    tb = min(TOKEN_BLOCK, s)
    cpb = tb // CHUNK
    sp = _gla_specs(s, dk, dv, tb, True)

    def body(q_ref, k_ref, v_ref, r_ref, gl_ref, wgu_ref, b_ref, gain_ref, st_ref, dy_ref,
             dq_ref, dk_ref, dv_ref, dr_ref, dgl_ref, dwgu_ref, db_ref, dgain_ref, dstate):
        @pl.when(pl.program_id(1) == 0)
        def _():
            dstate[...] = jnp.zeros_like(dstate)
            dwgu_ref[...] = jnp.zeros_like(dwgu_ref)
            db_ref[...] = jnp.zeros_like(db_ref)
            dgain_ref[...] = jnp.zeros_like(dgain_ref)

        for c in reversed(range(cpb)):
            sl = pl.ds(c * CHUNK, CHUNK)

            def fn(q, k, v, r, gl, w, b, gn, st):
                return _gla_chunk(_mm_diff, q, k, v, r, gl, w, b, gn, st)

            _, vjp = jax.vjp(fn, q_ref[sl, :], k_ref[sl, :], v_ref[sl, :], r_ref[sl, :], gl_ref[sl, :],
                             wgu_ref[...], b_ref[...], gain_ref[...], st_ref[c])
            dq, dkk, dvv, dr, dgl, dw, db, dgn, dst = vjp((dy_ref[sl, :].astype(F32), dstate[...]))
            dq_ref[sl, :] = dq.astype(BF)
            dk_ref[sl, :] = dkk.astype(BF)
            dv_ref[sl, :] = dvv.astype(BF)
            dr_ref[sl, :] = dr.astype(BF)
            dgl_ref[sl, :] = dgl.astype(BF)
            dwgu_ref[...] += dw
            db_ref[...] += db
            dgain_ref[...] += dgn
            dstate[...] = dst

    nh = GLA_HEADS
    dq, dkk, dvv, dr, dgl, dwgu, db, dgain = pl.pallas_call(
        body, name="gla_bwd", grid=(nh, s // tb),
        in_specs=[sp['q'], sp['k'], sp['v'], sp['r'], sp['glow'], sp['wgu'], sp['bias'], sp['gain'], sp['st'],
                  sp['hv']],
        out_specs=[sp['hk'], sp['hk'], sp['hv'], sp['hv'], sp['hg'], sp['wgu'], sp['bias'], sp['gain']],
        out_shape=[jax.ShapeDtypeStruct((s, nh * dk), BF), jax.ShapeDtypeStruct((s, nh * dk), BF),
                   jax.ShapeDtypeStruct((s, nh * dv), BF), jax.ShapeDtypeStruct((s, nh * dv), BF),
                   jax.ShapeDtypeStruct((s, nh * 128), BF), jax.ShapeDtypeStruct((128, nh * dk), F32),
                   jax.ShapeDtypeStruct((1, nh * dk), F32), jax.ShapeDtypeStruct((nh, 1, dv), F32)],
        scratch_shapes=[pltpu.VMEM((dv, dk), F32)], compiler_params=_cparams(),
    )(proj, proj, proj, proj, glow, wgu, bias, gain.reshape(nh, 1, dv), states, dy)
    return jnp.concatenate([dq, dkk, dvv, dr], axis=1), dgl, dwgu, db, dgain.reshape(nh, dv)


def _gdn_specs(s, dk):
    nh = GDN_HEADS
    col = lambda off: pl.BlockSpec((s, dk), lambda h: (0, off + h))
    tap = lambda off: pl.BlockSpec((CONV_WIDTH, dk), lambda h: (0, off + h))
    vec = pl.BlockSpec((1, 128), lambda h: (0, 0))
    return dict(q=col(0), k=col(nh), v=col(2 * nh), z=col(3 * nh), ba=pl.BlockSpec((s, 128), lambda h: (0, 0)),
                cq=tap(0), ck=tap(nh), cv=tap(2 * nh), vec=vec, gain=pl.BlockSpec((1, dk), lambda h: (0, 0)),
                head=col(0))


def _conv_silu(shift, x, taps):
    return _silu(_conv_taps(shift, x, taps))


def _load_taps(ref):
    return [ref[t:t + 1, :] for t in range(CONV_WIDTH)]


def _gdn_fwd(proj, ba, conv_w, a_log, dt_bias, gain):
    s = proj.shape[0]
    dk = proj.shape[1] // (4 * GDN_HEADS)
    sp = _gdn_specs(s, dk)
    rows = GDN_GROUP * CHUNK

    def grp(t):
        return t.reshape(GDN_GROUP, CHUNK, t.shape[-1])

    def body(q_ref, k_ref, v_ref, z_ref, ba_ref, cq_ref, ck_ref, cv_ref, al_ref, dt_ref, gain_ref, y_ref,
             qc, kc, vc, state):
        head = pl.program_id(0)
        qc[...] = _conv_silu(_shift_rows, q_ref[...], _load_taps(cq_ref))
        kc[...] = _conv_silu(_shift_rows, k_ref[...], _load_taps(ck_ref))
        vc[...] = _conv_silu(_shift_rows, v_ref[...], _load_taps(cv_ref))
        state[...] = jnp.zeros_like(state)

        def step(n, carry):
            base = pl.multiple_of(n * rows, rows)
            sl = pl.ds(base, rows)
            ys, new_state = _gdn_group(_mm_plain, grp(qc[sl, :]), grp(kc[sl, :]), grp(vc[sl, :]), grp(z_ref[sl, :]),
                                       grp(ba_ref[sl, :]), al_ref[...], dt_ref[...], gain_ref[...], state[...], head)
            for i, y in enumerate(ys):
                y_ref[pl.ds(base + i * CHUNK, CHUNK), :] = y.astype(BF)
            state[...] = new_state
            return carry

        lax.fori_loop(0, s // rows, step, 0)

    return pl.pallas_call(
        body, name="gdn_fwd", grid=(GDN_HEADS,),
        in_specs=[sp['q'], sp['k'], sp['v'], sp['z'], sp['ba'], sp['cq'], sp['ck'], sp['cv'], sp['vec'], sp['vec'],
                  sp['gain']],
        out_specs=sp['head'], out_shape=jax.ShapeDtypeStruct((s, GDN_HEADS * dk), BF),
        scratch_shapes=[pltpu.VMEM((s, dk), F32)] * 3 + [pltpu.VMEM((dk, dk), F32)],
        compiler_params=_cparams(),
    )(proj, proj, proj, proj, ba, conv_w, conv_w, conv_w, a_log, dt_bias, gain)


def _gdn_bwd(proj, ba, conv_w, a_log, dt_bias, gain, dy):
    s = proj.shape[0]
    dk = proj.shape[1] // (4 * GDN_HEADS)
    sp = _gdn_specs(s, dk)
    rows = GDN_GROUP * CHUNK
    ngroups = s // rows

    def grp(t):
        return t.reshape(GDN_GROUP, CHUNK, t.shape[-1])

    def body(q_ref, k_ref, v_ref, z_ref, ba_ref, cq_ref, ck_ref, cv_ref, al_ref, dt_ref, gain_ref, dy_ref,
             dq_ref, dk_ref, dv_ref, dz_ref, dba_ref, dcq_ref, dck_ref, dcv_ref, dal_ref, ddt_ref, dgain_ref,
             qc, kc, vc, dqc, dkc, dvc, states, state, dstate):
        head = pl.program_id(0)

        @pl.when(head == 0)
        def _():
            dba_ref[...] = jnp.zeros_like(dba_ref)
            dal_ref[...] = jnp.zeros_like(dal_ref)
            ddt_ref[...] = jnp.zeros_like(ddt_ref)
            dgain_ref[...] = jnp.zeros_like(dgain_ref)

        qc[...] = _conv_silu(_shift_rows, q_ref[...], _load_taps(cq_ref))
        kc[...] = _conv_silu(_shift_rows, k_ref[...], _load_taps(ck_ref))
        vc[...] = _conv_silu(_shift_rows, v_ref[...], _load_taps(cv_ref))
        state[...] = jnp.zeros_like(state)

        def fstep(n, carry):
            sl = pl.ds(pl.multiple_of(n * rows, rows), rows)
            states[n] = state[...]
            _, new_state = _gdn_group(_mm_plain, grp(qc[sl, :]), grp(kc[sl, :]), grp(vc[sl, :]), grp(z_ref[sl, :]),
                                      grp(ba_ref[sl, :]), al_ref[...], dt_ref[...], gain_ref[...], state[...], head)
            state[...] = new_state
            return carry

        lax.fori_loop(0, ngroups, fstep, 0)
        dstate[...] = jnp.zeros_like(dstate)

        def bstep(i, carry):
            n = ngroups - 1 - i
            base = pl.multiple_of(n * rows, rows)
            sl = pl.ds(base, rows)

            def fn(q, k, v, z, b, al, dt, gn, st):
                return _gdn_group(_mm_diff, q, k, v, z, b, al, dt, gn, st, head)

            _, vjp = jax.vjp(fn, grp(qc[sl, :]), grp(kc[sl, :]), grp(vc[sl, :]), grp(z_ref[sl, :]),
                             grp(ba_ref[sl, :]), al_ref[...], dt_ref[...], gain_ref[...], states[n])
            dys = [dy_ref[pl.ds(base + j * CHUNK, CHUNK), :].astype(F32) for j in range(GDN_GROUP)]
            dq, dkk, dvv, dz, db, dal, ddt, dgn, dst = vjp((dys, dstate[...]))
            dqc[sl, :] = dq.reshape(rows, dk)
            dkc[sl, :] = dkk.reshape(rows, dk)
            dvc[sl, :] = dvv.reshape(rows, dk)
            dz_ref[sl, :] = dz.reshape(rows, dk).astype(BF)
            dba_ref[sl, :] += db.reshape(rows, db.shape[-1])
            dal_ref[...] += dal
            ddt_ref[...] += ddt
            dgain_ref[...] += dgn
            dstate[...] = dst
            return carry

        lax.fori_loop(0, ngroups, bstep, 0)

        for x_ref, c_ref, dpost, dx_ref, dc_ref in ((q_ref, cq_ref, dqc, dq_ref, dcq_ref),
                                                    (k_ref, ck_ref, dkc, dk_ref, dck_ref),
                                                    (v_ref, cv_ref, dvc, dv_ref, dcv_ref)):
            _, vjp = jax.vjp(lambda x, *taps: _conv_silu(_shift_vjp, x, taps), x_ref[...], *_load_taps(c_ref))
            grads = vjp(dpost[...])
            dx_ref[...] = grads[0].astype(BF)
            for t in range(CONV_WIDTH):
                dc_ref[t:t + 1, :] = grads[1 + t]

    nh = GDN_HEADS
    col_bf = jax.ShapeDtypeStruct((s, nh * dk), BF)
    tap_out = jax.ShapeDtypeStruct((CONV_WIDTH, nh * dk), F32)
    tap_spec = pl.BlockSpec((CONV_WIDTH, dk), lambda h: (0, h))
    dq, dkk, dvv, dz, dba, dcq, dck, dcv, dal, ddt, dgain = pl.pallas_call(
        body, name="gdn_bwd", grid=(nh,),
        in_specs=[sp['q'], sp['k'], sp['v'], sp['z'], sp['ba'], sp['cq'], sp['ck'], sp['cv'], sp['vec'], sp['vec'],
                  sp['gain'], sp['head']],
        out_specs=[sp['head']] * 4 + [sp['ba'], tap_spec, tap_spec, tap_spec, sp['vec'], sp['vec'], sp['gain']],
        out_shape=[col_bf] * 4 + [jax.ShapeDtypeStruct((s, 128), F32), tap_out, tap_out, tap_out,
                                  jax.ShapeDtypeStruct((1, 128), F32), jax.ShapeDtypeStruct((1, 128), F32),
                                  jax.ShapeDtypeStruct((1, dk), F32)],
        scratch_shapes=[pltpu.VMEM((s, dk), F32)] * 6 + [pltpu.VMEM((ngroups, dk, dk), F32),
                                                        pltpu.VMEM((dk, dk), F32), pltpu.VMEM((dk, dk), F32)],
        compiler_params=_cparams(),
    )(proj, proj, proj, proj, ba, conv_w, conv_w, conv_w, a_log, dt_bias, gain, dy)
    return (jnp.concatenate([dq, dkk, dvv, dz], axis=1), dba, jnp.concatenate([dcq, dck, dcv], axis=1), dal, ddt,
            dgain)


def _lru_specs(s, bw):
    nb = LRU_BLOCKS
    return dict(
        xb=pl.BlockSpec((s, bw), lambda n: (0, n)), yb=pl.BlockSpec((s, bw), lambda n: (0, nb + n)),
        taps=pl.BlockSpec((CONV_WIDTH, bw), lambda n: (0, n)), vec=pl.BlockSpec((1, bw), lambda n: (0, n)),
        w=pl.BlockSpec((None, bw, bw), lambda n: (n, 0, 0)), b=pl.BlockSpec((None, 1, bw), lambda n: (n, 0, 0)),
        col=pl.BlockSpec((s, bw), lambda n: (0, n)))


def _lru_fwd(proj, conv_w, conv_b, lam, wr, br, wi, bi):
    s = proj.shape[0]
    bw = proj.shape[1] // (2 * LRU_BLOCKS)
    sp = _lru_specs(s, bw)

    def body(xb_ref, yb_ref, cw_ref, cb_ref, lam_ref, wr_ref, br_ref, wi_ref, bi_ref, y_ref):
        a, u, gy = _lru_pre(_mm_plain, _shift_rows, xb_ref[...], yb_ref[...], _load_taps(cw_ref), cb_ref[...],
                            lam_ref[...], wr_ref[...], br_ref[...], wi_ref[...], bi_ref[...])
        y_ref[...] = (_linscan(a, u, False) * gy).astype(BF)

    return pl.pallas_call(
        body, name="lru_fwd", grid=(LRU_BLOCKS,),
        in_specs=[sp['xb'], sp['yb'], sp['taps'], sp['vec'], sp['vec'], sp['w'], sp['b'], sp['w'], sp['b']],
        out_specs=sp['col'], out_shape=jax.ShapeDtypeStruct((s, LRU_BLOCKS * bw), BF),
        compiler_params=_cparams(),
    )(proj, proj, conv_w, conv_b, lam, wr, br, wi, bi)


def _lru_bwd(proj, conv_w, conv_b, lam, wr, br, wi, bi, dy):
    s = proj.shape[0]
    nb = LRU_BLOCKS
    bw = proj.shape[1] // (2 * nb)
    sp = _lru_specs(s, bw)

    def body(xb_ref, yb_ref, cw_ref, cb_ref, lam_ref, wr_ref, br_ref, wi_ref, bi_ref, dy_ref,
             dxb_ref, dyb_ref, dcw_ref, dcb_ref, dlam_ref, dwr_ref, dbr_ref, dwi_ref, dbi_ref):
        def pre(xb, yb, t0, t1, t2, t3, cb, lm, w_r, b_r, w_i, b_i):
            return _lru_pre(_mm_diff, _shift_vjp, xb, yb, (t0, t1, t2, t3), cb, lm, w_r, b_r, w_i, b_i)

        (a, u, gy), vjp = jax.vjp(pre, xb_ref[...], yb_ref[...], *_load_taps(cw_ref), cb_ref[...], lam_ref[...],
                                  wr_ref[...], br_ref[...], wi_ref[...], bi_ref[...])
        h = _linscan(a, u, False)
        dout = dy_ref[...].astype(F32)
        g = _linscan(_shift_rows(a, 1, True), dout * gy, True)
        grads = vjp((g * _shift_rows(h, 1, False), g, dout * h))
        dxb_ref[...] = grads[0].astype(BF)
        dyb_ref[...] = grads[1].astype(BF)
        for t in range(CONV_WIDTH):
            dcw_ref[t:t + 1, :] = grads[2 + t]
        dcb_ref[...] = grads[6]
        dlam_ref[...] = grads[7]
        dwr_ref[...] = grads[8]
        dbr_ref[...] = grads[9]
        dwi_ref[...] = grads[10]
        dbi_ref[...] = grads[11]

    outs = pl.pallas_call(
        body, name="lru_bwd", grid=(nb,),
        in_specs=[sp['xb'], sp['yb'], sp['taps'], sp['vec'], sp['vec'], sp['w'], sp['b'], sp['w'], sp['b'], sp['col']],
        out_specs=[sp['col'], sp['col'], sp['taps'], sp['vec'], sp['vec'], sp['w'], sp['b'], sp['w'], sp['b']],
        out_shape=[jax.ShapeDtypeStruct((s, nb * bw), BF), jax.ShapeDtypeStruct((s, nb * bw), BF),
                   jax.ShapeDtypeStruct((CONV_WIDTH, nb * bw), F32), jax.ShapeDtypeStruct((1, nb * bw), F32),
                   jax.ShapeDtypeStruct((1, nb * bw), F32), jax.ShapeDtypeStruct((nb, bw, bw), F32),
                   jax.ShapeDtypeStruct((nb, 1, bw), F32), jax.ShapeDtypeStruct((nb, bw, bw), F32),
                   jax.ShapeDtypeStruct((nb, 1, bw), F32)],
        compiler_params=_cparams(),
    )(proj, proj, conv_w, conv_b, lam, wr, br, wi, bi, dy)
    return (jnp.concatenate([outs[0], outs[1]], axis=1),) + tuple(outs[2:])


def _row_tile(rows, cols, bytes_per_row_set):
    t = 8
    while t * 2 <= rows and rows % (t * 2) == 0 and (t * 2) * cols * bytes_per_row_set <= 4 * 1024 * 1024:
        t *= 2
    return t


def _sum_slabs(a, out_dtype, name):
    n, rows, cols = a.shape
    tr = _row_tile(rows, cols, 4 * (n + 1))

    def body(*refs):
        acc = refs[0][...].astype(F32)
        for r in refs[1:n]:
            acc = acc + r[...].astype(F32)
        refs[n][...] = acc.astype(out_dtype)

    specs = [pl.BlockSpec((None, tr, cols), functools.partial(lambda i, k: (k, i, 0), k=k)) for k in range(n)]
    return pl.pallas_call(
        body, name=name, grid=(rows // tr,), in_specs=specs, out_specs=pl.BlockSpec((tr, cols), lambda i: (i, 0)),
        out_shape=jax.ShapeDtypeStruct((rows, cols), out_dtype), compiler_params=_cparams(),
    )(*([a] * n))


def _sum_keep_and_landed(g, landed, core, name):
    four, hr, cols = landed.shape
    tr = _row_tile(hr, cols, 2 * 3)
    tph = hr // tr

    def body(core_ref, g_ref, l_ref, o_ref):
        o_ref[...] = (g_ref[...].astype(F32) + l_ref[...].astype(F32)).astype(o_ref.dtype)

    grid_spec = pltpu.PrefetchScalarGridSpec(
        num_scalar_prefetch=1, grid=(four, tph),
        in_specs=[pl.BlockSpec((None, tr, cols), lambda j, r, core_ref: (j, core_ref[0] * tph + r, 0)),
                  pl.BlockSpec((None, tr, cols), lambda j, r, core_ref: (j, r, 0))],
        out_specs=pl.BlockSpec((None, tr, cols), lambda j, r, core_ref: (j, r, 0)))
    return pl.pallas_call(body, name=name, grid_spec=grid_spec, out_shape=jax.ShapeDtypeStruct(landed.shape, g.dtype),
                          compiler_params=_cparams())(core, g, landed)


def _sum_chip_parts(parts, landed, chip, name):
    _, rows, cols = parts.shape
    tr = _row_tile(rows, cols, 2 * 4 + 4)

    def body(chip_ref, p_ref, l0_ref, l1_ref, l2_ref, o_ref):
        acc = l0_ref[...].astype(F32) + l1_ref[...].astype(F32)
        o_ref[...] = (acc + l2_ref[...].astype(F32)) + p_ref[...].astype(F32)

    slab = lambda k: pl.BlockSpec((None, tr, cols), lambda i, chip_ref: (k, i, 0))
    grid_spec = pltpu.PrefetchScalarGridSpec(
        num_scalar_prefetch=1, grid=(rows // tr,),
        in_specs=[pl.BlockSpec((None, tr, cols), lambda i, chip_ref: (chip_ref[0], i, 0)), slab(0), slab(1), slab(2)],
        out_specs=pl.BlockSpec((tr, cols), lambda i, chip_ref: (i, 0)))
    return pl.pallas_call(body, name=name, grid_spec=grid_spec, out_shape=jax.ShapeDtypeStruct((rows, cols), F32),
                          compiler_params=_cparams())(chip, parts, landed, landed, landed)


_ADAM_C1 = 1.0 / (1.0 - ADAM_B1 ** ADAM_STEP)
_ADAM_C2 = 1.0 / (1.0 - ADAM_B2 ** ADAM_STEP)


def _adamw_math(w, g, m, v):
    nm = ADAM_B1 * m + (1.0 - ADAM_B1) * g
    nv = ADAM_B2 * v + (1.0 - ADAM_B2) * (g * g)
    delta = -ADAM_LR * ((nm * _ADAM_C1) / (jnp.sqrt(nv * _ADAM_C2) + ADAM_EPS) + ADAM_WD * w)
    return delta, nm, nv


def _adamw(w, g, m, v, name):
    rows, cols = w.shape
    tr = _row_tile(rows, cols, 4 * 7)

    def body(w_ref, g_ref, m_ref, v_ref, d_ref, nm_ref, nv_ref):
        d_ref[...], nm_ref[...], nv_ref[...] = _adamw_math(w_ref[...], g_ref[...], m_ref[...], v_ref[...])

    spec = pl.BlockSpec((tr, cols), lambda i: (i, 0))
    shape = jax.ShapeDtypeStruct((rows, cols), F32)
    return pl.pallas_call(
        body, name=name, grid=(rows // tr,), in_specs=[spec] * 4, out_specs=[spec] * 3, out_shape=[shape] * 3,
        compiler_params=_cparams(),
    )(w, g, m, v)


def _adamw_halves(w, m, v, layer, mine, theirs, core, name, carried=None):
    n_layers, rows, cols = w.shape
    hr = mine.shape[0]
    tr = _row_tile(hr, cols, 4 * 9)
    tph = hr // tr

    def body(core_ref, w_ref, q_ref, t_ref, m_ref, v_ref, *rest):
        g_ref, d_ref, nm_ref, nv_ref = rest[-4:]
        is_mine = (pl.program_id(0) // tph) == core_ref[0]
        g = jnp.where(is_mine, q_ref[...], t_ref[...])
        g_ref[...] = g
        d_ref[...], nm_ref[...], nv_ref[...] = _adamw_math(w_ref[...], g, m_ref[...], v_ref[...])

    slab = pl.BlockSpec((None, tr, cols), lambda i, core_ref: (layer, i, 0))
    half = pl.BlockSpec((tr, cols), lambda i, core_ref: (i % tph, 0))
    in_specs = [slab, half, half, slab, slab]
    operands = [core, w, mine, theirs, m, v]
    aliases = {}
    if carried is not None:
        in_specs += [_ANY] * 4
        aliases = {len(operands) + k: k for k in range(4)}
        operands += list(carried)
    grid_spec = pltpu.PrefetchScalarGridSpec(num_scalar_prefetch=1, grid=(rows // tr,), in_specs=in_specs,
                                             out_specs=[slab] * 4)
    return pl.pallas_call(
        body, name=name, grid_spec=grid_spec, out_shape=[jax.ShapeDtypeStruct(w.shape, F32)] * 4,
        input_output_aliases=aliases, compiler_params=_cparams(),
    )(*operands)


_ANY = pl.BlockSpec(memory_space=pl.ANY)


def _mesh_pos():
    return lax.axis_index("x"), lax.axis_index("y"), lax.axis_index("c")


def _remote(src, dst, send_sems, recv_sems, k, dev):
    return pltpu.make_async_remote_copy(src_ref=src, dst_ref=dst, send_sem=send_sems.at[k], recv_sem=recv_sems.at[k],
                                        device_id=dev, device_id_type=MESH)


STREAM_CHUNK_BYTES = 1024 * 1024
STREAM_SLOTS = 3


def _chunk_rows(rows, cols, itemsize):
    t = 16
    while t * 2 <= rows and rows % (t * 2) == 0 and (t * 2) * cols * itemsize <= STREAM_CHUNK_BYTES:
        t *= 2
    return t


def _stream_chunks(n_chunks, src_at, dst_at, buf, load_sems, send_sems, recv_sem, sibling):
    def load(k, slot):
        return pltpu.make_async_copy(src_at(k), buf.at[slot], load_sems.at[slot])

    def send(k, slot):
        return pltpu.make_async_remote_copy(src_ref=buf.at[slot], dst_ref=dst_at(k), send_sem=send_sems.at[slot],
                                            recv_sem=recv_sem, device_id=sibling, device_id_type=MESH)

    def step(k, carry):
        slot = k % STREAM_SLOTS

        @pl.when(k >= STREAM_SLOTS)
        def _():
            send(k - STREAM_SLOTS, slot).wait_send()

        load(k, slot).start()

        @pl.when(k >= 1)
        def _():
            prev = (k - 1) % STREAM_SLOTS
            load(k - 1, prev).wait()
            send(k - 1, prev).start()

        return carry

    lax.fori_loop(0, n_chunks, step, 0)
    last = (n_chunks - 1) % STREAM_SLOTS
    load(n_chunks - 1, last).wait()
    send(n_chunks - 1, last).start()
    for k in range(max(0, n_chunks - STREAM_SLOTS), n_chunks):
        send(k, k % STREAM_SLOTS).wait_send()


def _all_gather_chips(shards, split, name):
    n = len(shards)
    tr = [_chunk_rows(s.shape[0] // 2, s.shape[1], s.dtype.itemsize) if split[i] else 0 for i, s in enumerate(shards)]

    def body(*refs):
        ins, outs = refs[:n], refs[n:2 * n]
        send_sems, recv_sems, load_sems, stream_sems = refs[2 * n:2 * n + 4]
        bufs = dict(zip([i for i in range(n) if split[i]], refs[2 * n + 4:]))
        x, y, c = _mesh_pos()
        me = 2 * x + y
        sibling = (x, y, 1 - c)
        chips = [(1 - x, y), (x, 1 - y), (1 - x, 1 - y)]

        def half(i, which):
            hr = ins[i].shape[0] // 2
            return pl.ds(which * hr, hr)

        first = []
        for i in range(n):
            src = ins[i].at[half(i, c)] if split[i] else ins[i]
            dst = outs[i].at[me, half(i, c)] if split[i] else outs[i].at[me]
            for j, chip in enumerate(chips):
                first.append(_remote(src, dst, send_sems, recv_sems, 6 * i + j, (*chip, c)))
        for cp in first:
            cp.start()
        for i in range(n):
            for j, chip in enumerate(chips):
                pj = 2 * chip[0] + chip[1]
                blk = outs[i].at[pj, half(i, c)] if split[i] else outs[i].at[pj]
                _remote(blk, blk, send_sems, recv_sems, 6 * i + j, (*chip, c)).wait_recv()
                if split[i]:
                    hr = ins[i].shape[0] // 2

                    def rows_at(k, i=i, pj=pj, hr=hr):
                        return outs[i].at[pj, pl.ds(c * hr + k * tr[i], tr[i])]

                    _stream_chunks(hr // tr[i], rows_at, rows_at, bufs[i], load_sems, stream_sems,
                                   recv_sems.at[6 * i + 3 + j], sibling)
        for i in range(n):
            if split[i]:
                for j, chip in enumerate(chips):
                    blk = outs[i].at[2 * chip[0] + chip[1], half(i, 1 - c)]
                    _remote(blk, blk, send_sems, recv_sems, 6 * i + 3 + j, sibling).wait_recv()
        for cp in first:
            cp.wait_send()

    return pl.pallas_call(
        body, name=name, in_specs=[_ANY] * n, out_specs=[_ANY] * n,
        out_shape=[jax.ShapeDtypeStruct((N_CHIPS,) + s.shape, s.dtype) for s in shards],
        scratch_shapes=[pltpu.SemaphoreType.DMA((6 * n,)), pltpu.SemaphoreType.DMA((6 * n,)),
                        pltpu.SemaphoreType.DMA((STREAM_SLOTS,)), pltpu.SemaphoreType.DMA((STREAM_SLOTS,))]
        + [pltpu.VMEM((STREAM_SLOTS, tr[i], s.shape[1]), s.dtype) for i, s in enumerate(shards) if split[i]],
        compiler_params=_cparams(),
    )(*shards)


def _all_gather_devices(part, name):
    def body(in_ref, out_ref, send_sems, recv_sems, local_sem):
        x, y, c = _mesh_pos()
        me = 4 * x + 2 * y + c
        local = pltpu.make_async_copy(in_ref, out_ref.at[me], local_sem)
        local.start()
        peers = [(x ^ (k >> 2), y ^ ((k >> 1) & 1), c ^ (k & 1)) for k in range(1, N_DEV)]
        sends = [_remote(in_ref, out_ref.at[me], send_sems, recv_sems, k, p) for k, p in enumerate(peers)]
        for cp in sends:
            cp.start()
        for k, p in enumerate(peers):
            blk = out_ref.at[4 * p[0] + 2 * p[1] + p[2]]
            _remote(blk, blk, send_sems, recv_sems, k, p).wait_recv()
        for cp in sends:
            cp.wait_send()
        local.wait()

    return pl.pallas_call(
        body, name=name, in_specs=[_ANY], out_specs=_ANY,
        out_shape=jax.ShapeDtypeStruct((N_DEV,) + part.shape, part.dtype),
        scratch_shapes=[pltpu.SemaphoreType.DMA((N_DEV - 1,)), pltpu.SemaphoreType.DMA((N_DEV - 1,)),
                        pltpu.SemaphoreType.DMA],
    )(part)


def _stream_to_sibling(srcs, halved, name):
    n = len(srcs)
    geo = []
    for s in srcs:
        rows = s.shape[1] // 2 if halved else s.shape[1]
        geo.append((s.shape[0], rows, s.shape[2], _chunk_rows(rows, s.shape[2], s.dtype.itemsize)))

    def body(*refs):
        ins, outs = refs[:n], refs[n:2 * n]
        recv_sems, load_sems, send_sems = refs[2 * n:2 * n + 3]
        bufs = refs[2 * n + 3:]
        x, y, c = _mesh_pos()
        sibling = (x, y, 1 - c)
        for i in range(n):
            slabs, rows, _, tr = geo[i]
            per_slab = rows // tr
            off = (1 - c) * rows if halved else 0

            def src_at(k, i=i, per_slab=per_slab, tr=tr, off=off):
                return ins[i].at[k // per_slab, pl.ds(off + (k % per_slab) * tr, tr)]

            def dst_at(k, i=i, per_slab=per_slab, tr=tr):
                return outs[i].at[k // per_slab, pl.ds((k % per_slab) * tr, tr)]

            _stream_chunks(slabs * per_slab, src_at, dst_at, bufs[i], load_sems, send_sems, recv_sems.at[i], sibling)
        for i in range(n):
            pltpu.make_async_remote_copy(src_ref=outs[i], dst_ref=outs[i], send_sem=send_sems.at[0],
                                         recv_sem=recv_sems.at[i], device_id=sibling, device_id_type=MESH).wait_recv()

    return pl.pallas_call(
        body, name=name, in_specs=[_ANY] * n, out_specs=[_ANY] * n,
        out_shape=[jax.ShapeDtypeStruct((g[0], g[1], g[2]), s.dtype) for g, s in zip(geo, srcs)],
        scratch_shapes=[pltpu.SemaphoreType.DMA((n,)), pltpu.SemaphoreType.DMA((STREAM_SLOTS,)),
                        pltpu.SemaphoreType.DMA((STREAM_SLOTS,))]
        + [pltpu.VMEM((STREAM_SLOTS, g[3], g[2]), s.dtype) for g, s in zip(geo, srcs)],
        compiler_params=_cparams(),
    )(*srcs)


def _chip_scatter(parts, name):
    n = len(parts)

    def body(*refs):
        ins, outs = refs[:n], refs[n:2 * n]
        send_sems, recv_sems = refs[2 * n:]
        x, y, c = _mesh_pos()
        chips = [(1 - x, y), (x, 1 - y), (1 - x, 1 - y)]
        copies = []
        for i in range(n):
            for j, chip in enumerate(chips):
                copies.append(_remote(ins[i].at[2 * chip[0] + chip[1]], outs[i].at[j], send_sems, recv_sems, 3 * i + j,
                                      (*chip, c)))
        for cp in copies:
            cp.start()
        for cp in copies:
            cp.wait()

    return pl.pallas_call(
        body, name=name, in_specs=[_ANY] * n, out_specs=[_ANY] * n,
        out_shape=[jax.ShapeDtypeStruct((3,) + p.shape[1:], p.dtype) for p in parts],
        scratch_shapes=[pltpu.SemaphoreType.DMA((3 * n,)), pltpu.SemaphoreType.DMA((3 * n,))],
    )(*parts)


def _reduce_scatter(grads, tag, core, chip):
    landed = _stream_to_sibling(grads, True, f"rs_swap_{tag}")
    parts = [_sum_keep_and_landed(g, l, core, f"rs_add2_{tag}_{i}") for i, (g, l) in enumerate(zip(grads, landed))]
    landed = _chip_scatter(parts, f"rs_scatter_{tag}")
    mine = [_sum_chip_parts(p, l, chip, f"rs_add4_{tag}_{i}") for i, (p, l) in enumerate(zip(parts, landed))]
    theirs = _stream_to_sibling([m[None] for m in mine], False, f"rs_join_{tag}")
    return [(m, t[0]) for m, t in zip(mine, theirs)]


def _pad_cols(a, width=128):
    return jnp.pad(a, ((0, 0), (0, width - a.shape[1])))


def _mixer_forward(kind, hn, w, tables):
    if kind == 0:
        proj = _matmul(hn, w['ret_w_in'], name="ret_proj", b_sharded=True)
        og, states = _ret_fwd(proj, w['ret_gn_gain'], tables)
        return og, (proj, states)
    if kind == 1:
        proj = _matmul(hn, w['gdn_w_main'], name="gdn_proj")
        ba = _matmul(hn, w['gdn_w_small'], name="gdn_proj_ba")
        og = _gdn_fwd(proj, ba, w['gdn_conv_w'], w['gdn_a_log'], w['gdn_dt_bias'], w['gdn_norm_gain'])
        return og, (proj, ba)
    if kind == 2:
        proj = _matmul(hn, w['gla_w_main'], name="gla_proj")
        glow = _matmul(hn, w['gla_w_small'], name="gla_proj_gate")
        og, states = _gla_fwd(proj, glow, w['gla_w_gate_up'], w['gla_gate_bias'], w['gla_norm_gain'])
        return og, (proj, glow, states)
    proj = _matmul(hn, w['lru_w_in'], name="lru_proj", b_sharded=True)
    og = _lru_fwd(proj, w['lru_conv_w'], w['lru_conv_b'], w['lru_lambda'], w['lru_w_rgate'], w['lru_b_rgate'],
                  w['lru_w_igate'], w['lru_b_igate'])
    return og, (proj,)


def _mixer_backward(kind, hn, w, tables, saved, d_og, grads):
    d = hn.shape[1]
    if kind == 0:
        proj, states = saved
        d_proj, grads['ret_gn_gain'] = _ret_bwd(proj, w['ret_gn_gain'], tables, states, d_og)
        grads['ret_w_in'] = _matmul(hn, d_proj, name="ret_dw_in", ta=True, out_dtype=BF, o_sharded=True)
        return _matmul(d_proj, w['ret_w_in'], name="ret_dhn", tb=True, b_sharded=True)
    if kind == 1:
        proj, ba = saved
        d_proj, d_ba, grads['gdn_conv_w'], grads['gdn_a_log'], grads['gdn_dt_bias'], grads['gdn_norm_gain'] = _gdn_bwd(
            proj, ba, w['gdn_conv_w'], w['gdn_a_log'], w['gdn_dt_bias'], w['gdn_norm_gain'], d_og)
        d_ba = d_ba.astype(BF)
        dw_main = _matmul(hn, d_proj, name="gdn_dw_main", ta=True, out_dtype=BF)
        dw_small = _matmul(hn, d_ba, name="gdn_dw_small", ta=True, out_dtype=BF)
        dw = jnp.concatenate([dw_main, dw_small[:, :2 * GDN_HEADS]], axis=1)
        grads['gdn_w_in'] = dw.reshape(d, N_CHIPS, dw.shape[1] // N_CHIPS).transpose(1, 0, 2)
        d_hn = _matmul(d_proj, w['gdn_w_main'], name="gdn_dhn_main", tb=True)
        return _matmul(d_ba, w['gdn_w_small'], name="gdn_dhn_small", tb=True, epilogue='add', extra=d_hn)
    if kind == 2:
        proj, glow, states = saved
        d_proj, d_glow4, d_wgu, grads['gla_gate_bias'], grads['gla_norm_gain'] = _gla_bwd(
            proj, glow, w['gla_w_gate_up'], w['gla_gate_bias'], w['gla_norm_gain'], states, d_og)
        grads['gla_w_gate_up'] = d_wgu[:GLA_RANK]
        dw_main = _matmul(hn, d_proj, name="gla_dw_main", ta=True, out_dtype=BF)
        dw_small4 = _matmul(hn, d_glow4, name="gla_dw_small", ta=True, out_dtype=F32)
        dw_small = dw_small4.reshape(d, GLA_HEADS, 128)[:, :, :GLA_RANK].sum(axis=1).astype(BF)
        dw = jnp.concatenate([dw_main, dw_small], axis=1)
        grads['gla_w_in'] = dw.reshape(d, N_CHIPS, dw.shape[1] // N_CHIPS).transpose(1, 0, 2)
        d_hn = _matmul(d_proj, w['gla_w_main'], name="gla_dhn_main", tb=True)
        w_small4 = jnp.tile(w['gla_w_small'], (1, GLA_HEADS))
        return _matmul(d_glow4, w_small4, name="gla_dhn_small", tb=True, epilogue='add', extra=d_hn)
    (proj,) = saved
    (d_proj, grads['lru_conv_w'], grads['lru_conv_b'], grads['lru_lambda'], grads['lru_w_rgate'], grads['lru_b_rgate'],
     grads['lru_w_igate'], grads['lru_b_igate']) = _lru_bwd(
        proj, w['lru_conv_w'], w['lru_conv_b'], w['lru_lambda'], w['lru_w_rgate'], w['lru_b_rgate'], w['lru_w_igate'],
        w['lru_b_igate'], d_og)
    grads['lru_w_in'] = _matmul(hn, d_proj, name="lru_dw_in", ta=True, out_dtype=BF, o_sharded=True)
    return _matmul(d_proj, w['lru_w_in'], name="lru_dhn", tb=True, b_sharded=True)


_W_OUT = ('ret_w_out', 'gdn_w_out', 'gla_w_out', 'lru_w_out')
_W_IN = ('ret_w_in', 'gdn_w_in', 'gla_w_in', 'lru_w_in')


def _layer_forward(layer, x, w, tables):
    hn = _rmsnorm_fwd(x, w['norm1'][layer], f"norm1_fwd_{layer}")
    og, mixer_saved = _mixer_forward(layer, hn, w, tables)
    x1 = _matmul(og, w[_W_OUT[layer]], name=f"mixer_out_{layer}", epilogue='add', extra=x)
    hn2 = _rmsnorm_fwd(x1, w['norm2'][layer], f"norm2_fwd_{layer}")
    act = _matmul(hn2, w['mlp_w_up'][layer], name="mlp_up", b_sharded=True, epilogue='relu2', out_dtype=BF)
    x2 = _matmul(act, w['mlp_w_down'][layer], name="mlp_down", epilogue='add', extra=x1)
    return x2, (x, hn, mixer_saved, og, x1, hn2, act)


def _layer_backward(layer, dx2, w, tables, saved):
    x, hn, mixer_saved, og, x1, hn2, act = saved
    d = x.shape[1]
    grads = {}
    d_up = _matmul(dx2, w['mlp_w_down'][layer], name="mlp_d_up", tb=True, epilogue='dact', extra=act, out_dtype=BF)
    dw_down = _matmul(act, dx2, name="mlp_dw_down", ta=True, out_dtype=BF)
    grads['mlp_w_down'] = dw_down.reshape(N_CHIPS, dw_down.shape[0] // N_CHIPS, d)
    grads['mlp_w_up'] = _matmul(hn2, d_up, name="mlp_dw_up", ta=True, out_dtype=BF, o_sharded=True)
    d_hn2 = _matmul(d_up, w['mlp_w_up'][layer], name="mlp_d_hn", tb=True, b_sharded=True)
    dx1, grads['norm2'] = _rmsnorm_bwd(x1, w['norm2'][layer], d_hn2, dx2, f"norm2_bwd_{layer}")
    w_out = w[_W_OUT[layer]]
    d_og = _matmul(dx1, w_out, name=f"mixer_d_og_{layer}", tb=True, out_dtype=BF)
    dw_out = _matmul(og, dx1, name=f"mixer_dw_out_{layer}", ta=True, out_dtype=BF)
    grads[_W_OUT[layer]] = dw_out.reshape(N_CHIPS, dw_out.shape[0] // N_CHIPS, d)
    d_hn = _mixer_backward(layer, hn, w, tables, mixer_saved, d_og, grads)
    dx, grads['norm1'] = _rmsnorm_bwd(x, w['norm1'][layer], d_hn, dx1, f"norm1_bwd_{layer}")
    return dx, grads


def _pack(arrays):
    flat = []
    for a in arrays:
        v = a.astype(F32).reshape(-1)
        v = jnp.pad(v, (0, (-v.shape[0]) % 128))
        flat.append(v.reshape(-1, 128))
    buf = jnp.concatenate(flat, axis=0)
    return jnp.pad(buf, ((0, (-buf.shape[0]) % 8), (0, 0)))


def _unpack(buf, shapes):
    lead = buf.shape[:-2]
    out, off = [], 0
    for shp in shapes:
        n = math.prod(shp)
        rows = -(-n // 128)
        piece = buf[..., off:off + rows, :].reshape(lead + (rows * 128,))[..., :n]
        out.append(piece.reshape(lead + tuple(shp)))
        off += rows
    return out


_WEIGHTS = ('norm1', 'norm2', 'final_norm', 'ret_w_in', 'ret_gn_gain', 'ret_w_out', 'gdn_w_in', 'gdn_conv_w',
            'gdn_a_log', 'gdn_dt_bias', 'gdn_norm_gain', 'gdn_w_out', 'gla_w_in', 'gla_w_gate_up', 'gla_gate_bias',
            'gla_norm_gain', 'gla_w_out', 'lru_w_in', 'lru_conv_w', 'lru_conv_b', 'lru_w_rgate', 'lru_b_rgate',
            'lru_w_igate', 'lru_b_igate', 'lru_lambda', 'lru_w_out', 'mlp_w_up', 'mlp_w_down')
_FWD_PARAMS = ('x',) + _WEIGHTS
_BIG = ('ret_w_in', 'ret_w_out', 'gdn_w_in', 'gdn_w_out', 'gla_w_in', 'gla_w_out', 'lru_w_in', 'lru_w_out',
        'mlp_w_up', 'mlp_w_down')
_SMALL = tuple(n for n in _WEIGHTS if n not in _BIG)
_SMALL_SHARDED = ('ret_gn_gain', 'gdn_conv_w', 'gla_w_gate_up', 'gla_gate_bias', 'gla_norm_gain', 'lru_conv_w',
                  'lru_conv_b', 'lru_lambda')


def kernel(*args):
    names = _FWD_PARAMS + ('loss_target',) + tuple('m_' + n for n in _WEIGHTS) + tuple('v_' + n for n in _WEIGHTS)
    assert len(args) == len(names)
    a = dict(zip(names, args))
    x = a['x'][0]
    target = a['loss_target'][0]
    s, d = x.shape
    chip = 2 * lax.axis_index("x") + lax.axis_index("y")

    small_local = [a[n][0] if a[n].ndim == 3 else a[n] for n in _SMALL_SHARDED]
    small_pack = _pack(small_local)
    gathered = {}
    for layer in range(4):
        w_in, w_out = _W_IN[layer], _W_OUT[layer]
        ops = [a[w_in][0].astype(BF), a[w_out][0].astype(BF), a['mlp_w_up'][layer].astype(BF),
               a['mlp_w_down'][layer].astype(BF)]
        split = [True] * 4
        if layer == 0:
            ops.append(small_pack)
            split.append(False)
        res = _all_gather_chips(ops, split, f"gather_weights_{layer}")
        res = [lax.dynamic_update_slice_in_dim(r, o[None], chip, axis=0) for r, o in zip(res, ops)]
        gathered[w_in], gathered[w_out] = res[0], res[1]
        gathered[('mlp_w_up', layer)], gathered[('mlp_w_down', layer)] = res[2], res[3]
        if layer == 0:
            small_all = res[4]

    def whole_cols(g):
        return g.transpose(1, 0, 2).reshape(g.shape[1], N_CHIPS * g.shape[2])

    def whole_rows(g):
        return g.reshape(N_CHIPS * g.shape[1], g.shape[2])

    w = {}
    for n, piece in zip(_SMALL_SHARDED, _unpack(small_all, [p.shape for p in small_local])):
        w[n] = whole_cols(piece)
    for n in _SMALL:
        if n not in _SMALL_SHARDED:
            w[n] = a[n][0] if n.startswith('lru_') else a[n]
    w['gdn_a_log'], w['gdn_dt_bias'] = _pad_cols(w['gdn_a_log']), _pad_cols(w['gdn_dt_bias'])
    w['gla_w_gate_up'] = jnp.pad(w['gla_w_gate_up'], ((0, 128 - GLA_RANK), (0, 0)))
    nb = LRU_BLOCKS
    w['lru_b_rgate'] = w['lru_b_rgate'].reshape(nb, 1, -1)
    w['lru_b_igate'] = w['lru_b_igate'].reshape(nb, 1, -1)
    w['ret_w_in'] = gathered['ret_w_in']
    w['lru_w_in'] = gathered['lru_w_in']
    gdn_full = whole_cols(gathered['gdn_w_in'])
    n_gdn = gdn_full.shape[1] - 2 * GDN_HEADS
    w['gdn_w_main'], w['gdn_w_small'] = gdn_full[:, :n_gdn], _pad_cols(gdn_full[:, n_gdn:])
    gla_full = whole_cols(gathered['gla_w_in'])
    n_gla = gla_full.shape[1] - GLA_RANK
    w['gla_w_main'], w['gla_w_small'] = gla_full[:, :n_gla], _pad_cols(gla_full[:, n_gla:])
    for n in _W_OUT:
        w[n] = whole_rows(gathered[n])
    w['mlp_w_up'] = [gathered[('mlp_w_up', layer)] for layer in range(4)]
    w['mlp_w_down'] = [whole_rows(gathered[('mlp_w_down', layer)]) for layer in range(4)]

    tables = _ret_tables(s, d // RET_HEADS)
    saved = []
    h = x
    for layer in range(4):
        h, sv = _layer_forward(layer, h, w, tables)
        saved.append(sv)
    loss_part, dh, g_final = _loss_head(h, w['final_norm'], target)
    loss = lax.psum(loss_part, ("x", "y", "c"))

    grad, delta, new_m, new_v = {}, {}, {}, {}
    small_grads = {'final_norm': g_final}
    norm_grads = {'norm1': [None] * 4, 'norm2': [None] * 4}
    core_arr = lax.axis_index("c").astype(jnp.int32).reshape(1)
    chip_arr = chip.astype(jnp.int32).reshape(1)
    mlp_upd = {'mlp_w_up': None, 'mlp_w_down': None}
    for layer in reversed(range(4)):
        dh, g = _layer_backward(layer, dh, w, tables, saved[layer])
        saved[layer] = None
        w_in, w_out = _W_IN[layer], _W_OUT[layer]
        red = _reduce_scatter([g[w_in], g[w_out], g['mlp_w_up'], g['mlp_w_down']], str(layer), core_arr, chip_arr)
        for n, (mine, theirs) in ((w_in, red[0]), (w_out, red[1])):
            grad[n], delta[n], new_m[n], new_v[n] = _adamw_halves(a[n], a['m_' + n], a['v_' + n], 0, mine, theirs,
                                                                  core_arr, f"adamw_{n}")
        for n, (mine, theirs) in (('mlp_w_up', red[2]), ('mlp_w_down', red[3])):
            mlp_upd[n] = _adamw_halves(a[n], a['m_' + n], a['v_' + n], layer, mine, theirs, core_arr,
                                       f"adamw_{n}_{layer}", carried=mlp_upd[n])
        norm_grads['norm1'][layer], norm_grads['norm2'][layer] = g['norm1'], g['norm2']
        for n in _SMALL:
            if n in g:
                small_grads[n] = g[n]
    for n in ('mlp_w_up', 'mlp_w_down'):
        grad[n], delta[n], new_m[n], new_v[n] = mlp_upd[n]
    small_grads['norm1'] = jnp.stack(norm_grads['norm1'])
    small_grads['norm2'] = jnp.stack(norm_grads['norm2'])

    full_shapes = [small_grads[n].shape for n in _SMALL]
    total = _sum_slabs(_all_gather_devices(_pack([small_grads[n] for n in _SMALL]), "gather_small_grads"), F32,
                       "sum_small_grads")
    local_g = {}
    for n, full in zip(_SMALL, _unpack(total, full_shapes)):
        shp = a[n].shape
        if n in _SMALL_SHARDED:
            full = full.reshape(full.shape[0], -1)
            cq = shp[-1]
            full = lax.dynamic_slice_in_dim(full, chip * cq, cq, axis=1)
        elif n in ('gdn_a_log', 'gdn_dt_bias'):
            full = full[:, :shp[-1]]
        local_g[n] = full.reshape(shp)
    shapes = [a[n].shape for n in _SMALL]
    packed = [_pack([src[n] for n in _SMALL]) for src in
              (a, local_g, {n: a['m_' + n] for n in _SMALL}, {n: a['v_' + n] for n in _SMALL})]
    upd = _adamw(*packed, "adamw_small")
    for n, gr, dl, nm, nv in zip(_SMALL, _unpack(packed[1], shapes), *[_unpack(u, shapes) for u in upd]):
        grad[n], delta[n], new_m[n], new_v[n] = gr, dl, nm, nv

    out = [loss, dh.reshape(a['x'].shape)]
    for group in (grad, delta, new_m, new_v):
        out += [group[n].reshape(a[n].shape) for n in _WEIGHTS]
    return tuple(out)
```

```python
import functools
import math

import jax
import jax.numpy as jnp
from jax import lax
from jax.experimental import pallas as pl
from jax.experimental.pallas import tpu as pltpu

F32 = jnp.float32
BF = jnp.bfloat16
MESH = pl.DeviceIdType.MESH

NORM_EPS = 1e-6
CHUNK = 64
RET_HEADS = 8
GDN_HEADS = 16
GLA_HEADS = 4
GLA_RANK = 16
GLA_TAU = 16.0
LRU_BLOCKS = 16
LRU_C = 8.0
CONV_WIDTH = 4
ROPE_BASE = 10000.0
N_CHIPS = 4
N_DEV = 8

ADAM_LR = 0.001
ADAM_B1 = 0.9
ADAM_B2 = 0.999
ADAM_EPS = 1e-08
ADAM_WD = 0.01
ADAM_STEP = 10

VMEM_LIMIT_BYTES = 56 * 1024 * 1024
TOKEN_BLOCK = 256
ROW_BLOCK = 256
GDN_GROUP = 8


def _cparams(**kw):
    return pltpu.CompilerParams(vmem_limit_bytes=VMEM_LIMIT_BYTES, **kw)


def _bf16_parts(x, n):
    parts = []
    for _ in range(n - 1):
        p = x.astype(BF)
        parts.append(p)
        x = x - p.astype(F32)
    return parts + [x.astype(BF)]


def _raw_mm(a, b, ta, tb, hi):
    nb = a.ndim - 2
    batch = tuple(range(nb))
    dims = (((nb + (0 if ta else 1),), (nb + (1 if tb else 0),)), (batch, batch))

    def dot(p, q):
        return lax.dot_general(p, q, dims, preferred_element_type=F32)

    if not hi:
        return dot(a.astype(BF), b.astype(BF))
    if hi == 'l':
        ae = a.astype(BF)
        b0, b1, b2 = _bf16_parts(b.astype(F32), 3)
        return dot(ae, b0) + (dot(ae, b1) + dot(ae, b2))
    if hi == 'r':
        be = b.astype(BF)
        a0, a1, a2 = _bf16_parts(a.astype(F32), 3)
        return dot(a0, be) + (dot(a1, be) + dot(a2, be))
    a0, a1 = _bf16_parts(a.astype(F32), 2)
    b0, b1 = _bf16_parts(b.astype(F32), 2)
    return dot(a0, b0) + (dot(a0, b1) + dot(a1, b0))


@functools.partial(jax.custom_vjp, nondiff_argnums=(2, 3, 4))
def _mm_vjp(a, b, ta, tb, hi):
    return _raw_mm(a, b, ta, tb, hi)


def _mm_vjp_fwd(a, b, ta, tb, hi):
    return _raw_mm(a, b, ta, tb, hi), (a, b)


def _mm_vjp_bwd(ta, tb, hi, res, g):
    a, b = res
    if ta:
        da = _raw_mm(b, g, tb, True, 'l' if hi == 'r' else bool(hi))
    else:
        da = _raw_mm(g, b, False, not tb, 'r' if hi == 'r' else bool(hi))
    if tb:
        db = _raw_mm(g, a, True, ta, 'r' if hi == 'l' else bool(hi))
    else:
        db = _raw_mm(a, g, not ta, False, 'l' if hi == 'l' else bool(hi))
    return da, db


_mm_vjp.defvjp(_mm_vjp_fwd, _mm_vjp_bwd)


def _mm_diff(a, b, ta=False, tb=False, hi=False):
    return _mm_vjp(a, b, ta, tb, hi)


def _mm_plain(a, b, ta=False, tb=False, hi=False):
    return _raw_mm(a, b, ta, tb, hi)


def _shift_rows(x, k, up):
    if k == 0:
        return x
    n = x.shape[0]
    rows = lax.broadcasted_iota(jnp.int32, x.shape, 0)
    if up:
        return jnp.where(rows < n - k, pltpu.roll(x, n - k, 0), 0.0)
    return jnp.where(rows >= k, pltpu.roll(x, k, 0), 0.0)


@functools.partial(jax.custom_vjp, nondiff_argnums=(1, 2))
def _shift_vjp(x, k, up):
    return _shift_rows(x, k, up)


def _shift_vjp_fwd(x, k, up):
    return _shift_rows(x, k, up), None


def _shift_vjp_bwd(k, up, _, g):
    return (_shift_rows(g, k, not up),)


_shift_vjp.defvjp(_shift_vjp_fwd, _shift_vjp_bwd)


def _sigmoid(x):
    return 1.0 / (1.0 + jnp.exp(-x))


def _silu(x):
    return x * _sigmoid(x)


def _softplus(x):
    return jnp.maximum(x, 0.0) + jnp.log(1.0 + jnp.exp(-jnp.abs(x)))


def _gelu_tanh(x):
    return 0.5 * x * (1.0 + jnp.tanh(math.sqrt(2.0 / math.pi) * (x + 0.044715 * (x * x * x))))


def _expm1(x):
    series = x * (1.0 + x * (0.5 + x * (1.0 / 6.0 + x * (1.0 / 24.0))))
    return jnp.where(jnp.abs(x) < 0.03, series, jnp.exp(x) - 1.0)


def _rmsnorm(x, g):
    return x * lax.rsqrt(jnp.mean(x * x, axis=-1, keepdims=True) + NORM_EPS) * g


def _head_norm(o, gain, center):
    if center:
        o = o - jnp.mean(o, axis=-1, keepdims=True)
    return o * lax.rsqrt(jnp.mean(o * o, axis=-1, keepdims=True) + NORM_EPS) * gain


def _l2norm(x):
    return x * lax.rsqrt(jnp.sum(x * x, axis=-1, keepdims=True) + NORM_EPS)


def _iota2(shape, dim):
    return lax.broadcasted_iota(jnp.int32, shape, dim)


def _tri_ones(n, upper=False):
    i, j = _iota2((n, n), 0), _iota2((n, n), 1)
    return jnp.where((i <= j) if upper else (j <= i), 1.0, 0.0).astype(F32)


def _linscan(a, u, rev):
    n = a.shape[0]
    rows = _iota2(a.shape, 0)
    d = 1
    while d < n:
        if rev:
            valid = rows < n - d
            a_s, u_s = pltpu.roll(a, n - d, 0), pltpu.roll(u, n - d, 0)
        else:
            valid = rows >= d
            a_s, u_s = pltpu.roll(a, d, 0), pltpu.roll(u, d, 0)
        u = a * jnp.where(valid, u_s, 0.0) + u
        a = a * jnp.where(valid, a_s, 1.0)
        d *= 2
    return u


def _ret_chunk(mm, q, k, v, g, gain, state, cos, sin, dintra, qdec, kdec, cdec):
    dk = q.shape[-1]
    half = dk // 2

    def rot(t):
        t1, t2 = t[:, :half], t[:, half:]
        return jnp.concatenate([t1 * cos - t2 * sin, t1 * sin + t2 * cos], axis=-1)

    qr = rot(q)
    kr = rot(k) * (dk ** -0.5)
    scores = mm(qr, kr, tb=True) * dintra
    o = mm(scores, v) + mm(qr * qdec, state)
    new_state = state * cdec + mm(kr * kdec, v, ta=True)
    y = _head_norm(o, gain, True) * _silu(g)
    return y, new_state


def _gla_chunk(mm, q, k, v, r, glow, wgu, bias, gain, state_t):
    dk = q.shape[-1]
    c = q.shape[0]
    logit = mm(glow, wgu) + bias
    la = -_softplus(-logit) * (1.0 / GLA_TAU)
    cum = mm(_tri_ones(c), la, hi='l')
    rows = _iota2(la.shape, 0)
    ref = jnp.sum(jnp.where(rows < c // 2, la, 0.0), axis=0, keepdims=True)
    tot = jnp.sum(la, axis=0, keepdims=True)
    fwd, bwd = jnp.exp(cum - ref), jnp.exp(ref - cum)
    qs = q * (dk ** -0.5)
    s_lo = mm(qs * fwd, k * bwd, tb=True)
    s_up = mm(qs * bwd, k * fwd, tb=True)
    i, j = _iota2((c, c), 0), _iota2((c, c), 1)
    scores = jnp.where(i >= j, s_lo, s_up)
    o = mm(scores, v) + mm(qs * jnp.exp(cum), state_t, tb=True)
    k_end = k * jnp.exp(tot - cum)
    new_state_t = state_t * jnp.exp(tot) + mm(v, k_end, ta=True)
    y = _head_norm(o, gain, False) * _silu(r)
    return y, new_state_t


def _gdn_group(mm, q, k, v, z, ba, a_log, dt_bias, gain, state, head):
    g, c, dk = q.shape
    lanes = lax.broadcasted_iota(jnp.int32, (1, 1, ba.shape[-1]), 2)
    oh_b = jnp.where(lanes == head, 1.0, 0.0).astype(F32)
    oh_a = jnp.where(lanes == head + GDN_HEADS, 1.0, 0.0).astype(F32)
    beta = _sigmoid(jnp.sum(ba * oh_b, axis=-1, keepdims=True))
    a_logit = jnp.sum(ba * oh_a, axis=-1, keepdims=True)
    a_h = jnp.sum(a_log * oh_b[0], axis=-1, keepdims=True)
    dt_h = jnp.sum(dt_bias * oh_b[0], axis=-1, keepdims=True)
    la = -jnp.exp(a_h) * _softplus(a_logit + dt_h)
    q = _l2norm(q) * (dk ** -0.5)
    k = _l2norm(k)
    tri = jnp.broadcast_to(_tri_ones(c), (g, c, c))
    tri_up = jnp.broadcast_to(_tri_ones(c, upper=True), (g, c, c))
    cum_k = mm(tri, la * jnp.ones((g, c, dk), F32), hi='l')
    la_sq = la * jnp.ones((g, c, c), F32)
    cum_i = mm(tri, la_sq, hi='l')
    cum_j = mm(la_sq, tri_up, ta=True, hi='r')
    i, j = _iota2((c, c), 0), _iota2((c, c), 1)
    strict = i > j
    rel = jnp.where(strict, jnp.exp(jnp.where(strict, cum_i - cum_j, 0.0)), 0.0)
    a_mat = beta * rel * mm(k, k, tb=True)
    inv = jnp.where(i == j, 1.0, 0.0).astype(F32) - a_mat
    power = a_mat
    for _ in range(int(math.log2(c)) - 1):
        power = mm(power, power, hi=True)
        inv = inv + mm(inv, power, hi=True)
    tot = jnp.sum(la, axis=1, keepdims=True)
    u = mm(inv, beta * v, hi=True)
    w = mm(inv, (beta * jnp.exp(cum_k)) * k, hi=True)
    k_end = k * jnp.exp(tot - cum_k)
    di, dj = _iota2((dk, dk), 0), _iota2((dk, dk), 1)
    trans = jnp.exp(tot) * jnp.where(di == dj, 1.0, 0.0).astype(F32) - mm(k_end, w, ta=True)
    inject = mm(k_end, u, ta=True)
    ys = []
    for n in range(g):
        state = mm(trans[n], state) + inject[n]
        ys.append(_head_norm(mm(q[n], state), gain, False) * _silu(z[n]))
    return ys, state


def _conv_taps(shift, x, taps):
    out = None
    for tap, w in enumerate(taps):
        term = shift(x, CONV_WIDTH - 1 - tap, False) * w
        out = term if out is None else out + term
    return out


def _lru_pre(mm, shift, xb, yb, taps, cb, lam, wr, br, wi, bi):
    xb = _conv_taps(shift, xb, taps) + cb
    r = _sigmoid(mm(xb, wr) + br)
    i = _sigmoid(mm(xb, wi) + bi)
    log_a = (-LRU_C) * _softplus(-lam) * r
    a = jnp.exp(log_a)
    u = jnp.sqrt(-_expm1(2.0 * log_a)) * (i * xb)
    return a, u, _gelu_tanh(yb)


_TOKEN_SPEC = pl.BlockSpec((8, 128), lambda *_: (0, 0))


def _pick(n, pref):
    for t in (pref, 2048, 1024, 512, 256, 128):
        if t <= n and n % t == 0:
            return t
    return n


def _matmul(a, b, *, name, ta=False, tb=False, out_dtype=F32, b_sharded=False, o_sharded=False,
            epilogue=None, extra=None, token=None, bm=512, bn=512, bk=2048):
    m, kdim = (a.shape[1], a.shape[0]) if ta else a.shape
    if b_sharded:
        nq = b.shape[2]
        n = b.shape[1] if tb else N_CHIPS * nq
        assert kdim == (N_CHIPS * nq if tb else b.shape[1])
    else:
        n = b.shape[0] if tb else b.shape[1]
        assert kdim == (b.shape[1] if tb else b.shape[0])
    bm, bk = _pick(m, bm), _pick(kdim, bk)
    if b_sharded and tb:
        bk = _pick(b.shape[2], bk)
    bn = _pick(b.shape[2] if (b_sharded and not tb) else (n // N_CHIPS if o_sharded else n), bn)
    nk = kdim // bk
    grid = (m // bm, n // bn, nk)

    a_spec = pl.BlockSpec((bk, bm), lambda i, j, k: (k, i)) if ta else pl.BlockSpec((bm, bk), lambda i, j, k: (i, k))
    if b_sharded and not tb:
        per = b.shape[2] // bn
        b_spec = pl.BlockSpec((None, bk, bn), lambda i, j, k: (j // per, k, j % per))
    elif b_sharded:
        per = b.shape[2] // bk
        b_spec = pl.BlockSpec((None, bn, bk), lambda i, j, k: (k // per, j, k % per))
    elif tb:
        b_spec = pl.BlockSpec((bn, bk), lambda i, j, k: (j, k))
    else:
        b_spec = pl.BlockSpec((bk, bn), lambda i, j, k: (k, j))
    if o_sharded:
        per_o = (n // N_CHIPS) // bn
        o_spec = pl.BlockSpec((None, bm, bn), lambda i, j, k: (j // per_o, i, j % per_o))
        out_shape = jax.ShapeDtypeStruct((N_CHIPS, m, n // N_CHIPS), out_dtype)
    else:
        o_spec = pl.BlockSpec((bm, bn), lambda i, j, k: (i, j))
        out_shape = jax.ShapeDtypeStruct((m, n), out_dtype)
    in_specs = [a_spec, b_spec]
    operands = [a, b]
    if extra is not None:
        in_specs.append(pl.BlockSpec((bm, bn), lambda i, j, k: (i, j)))
        operands.append(extra)
    if token is not None:
        in_specs.append(_TOKEN_SPEC)
        operands.append(token)

    def body(*refs):
        a_ref, b_ref = refs[0], refs[1]
        e_ref = refs[2] if extra is not None else None
        o_ref, acc_ref = refs[-2], refs[-1]
        k = pl.program_id(2)

        @pl.when(k == 0)
        def _():
            acc_ref[...] = jnp.zeros_like(acc_ref)

        acc_ref[...] += _raw_mm(a_ref[...], b_ref[...], ta, tb, False)

        @pl.when(k == nk - 1)
        def _():
            acc = acc_ref[...]
            if epilogue == 'add':
                acc = acc + e_ref[...].astype(F32)
            elif epilogue == 'relu2':
                acc = jnp.square(jnp.maximum(acc, 0.0))
            elif epilogue == 'dact':
                acc = acc * (2.0 * jnp.sqrt(e_ref[...].astype(F32)))
            o_ref[...] = acc.astype(o_ref.dtype)

    return pl.pallas_call(
        body, name=name, grid=grid, in_specs=in_specs, out_specs=o_spec, out_shape=out_shape,
        scratch_shapes=[pltpu.VMEM((bm, bn), F32)], compiler_params=_cparams(),
    )(*operands)


def _rmsnorm_fwd(x, g, name, token=None):
    s, d = x.shape

    def body(x_ref, g_ref, *rest):
        rest[-1][...] = _rmsnorm(x_ref[...], g_ref[...]).astype(BF)

    return pl.pallas_call(
        body, name=name, grid=(s // ROW_BLOCK,),
        in_specs=[pl.BlockSpec((ROW_BLOCK, d), lambda i: (i, 0)), pl.BlockSpec((1, d), lambda i: (0, 0))]
        + ([] if token is None else [_TOKEN_SPEC]),
        out_specs=pl.BlockSpec((ROW_BLOCK, d), lambda i: (i, 0)),
        out_shape=jax.ShapeDtypeStruct((s, d), BF), compiler_params=_cparams(),
    )(x, g.reshape(1, d), *([] if token is None else [token]))


def _rmsnorm_bwd(x, g, dh, dres, name):
    s, d = x.shape

    def body(x_ref, g_ref, dh_ref, dres_ref, dx_ref, dg_ref):
        _, vjp = jax.vjp(_rmsnorm, x_ref[...], g_ref[...])
        dx, dg = vjp(dh_ref[...].astype(F32))
        dx_ref[...] = dres_ref[...] + dx

        @pl.when(pl.program_id(0) == 0)
        def _():
            dg_ref[...] = jnp.zeros_like(dg_ref)

        dg_ref[...] += dg

    row = pl.BlockSpec((ROW_BLOCK, d), lambda i: (i, 0))
    vec = pl.BlockSpec((1, d), lambda i: (0, 0))
    dx, dg = pl.pallas_call(
        body, name=name, grid=(s // ROW_BLOCK,), in_specs=[row, vec, row, row], out_specs=[row, vec],
        out_shape=[jax.ShapeDtypeStruct((s, d), F32), jax.ShapeDtypeStruct((1, d), F32)],
        compiler_params=_cparams(),
    )(x, g.reshape(1, d), dh, dres)
    return dx, dg.reshape(d)


def _loss_head(x, g, target):
    s, d = x.shape

    def loss_fn(xv, gv, tv):
        err = _rmsnorm(xv, gv) - tv
        return 0.5 * jnp.sum(jnp.mean(err * err, axis=-1, keepdims=True), axis=0, keepdims=True)

    def body(x_ref, g_ref, t_ref, dx_ref, dg_ref, loss_ref):
        tv = t_ref[...]
        loss, vjp = jax.vjp(lambda xv, gv: loss_fn(xv, gv, tv), x_ref[...], g_ref[...])
        dx, dg = vjp(jnp.ones((1, 1), F32))
        dx_ref[...] = dx

        @pl.when(pl.program_id(0) == 0)
        def _():
            dg_ref[...] = jnp.zeros_like(dg_ref)
            loss_ref[...] = jnp.zeros_like(loss_ref)

        dg_ref[...] += dg
        loss_ref[...] += loss * jnp.ones_like(loss_ref)

    row = pl.BlockSpec((ROW_BLOCK, d), lambda i: (i, 0))
    vec = pl.BlockSpec((1, d), lambda i: (0, 0))
    dx, dg, loss = pl.pallas_call(
        body, name="loss_head", grid=(s // ROW_BLOCK,), in_specs=[row, vec, row],
        out_specs=[row, vec, pl.BlockSpec((1, 128), lambda i: (0, 0))],
        out_shape=[jax.ShapeDtypeStruct((s, d), F32), jax.ShapeDtypeStruct((1, d), F32),
                   jax.ShapeDtypeStruct((1, 128), F32)],
        compiler_params=_cparams(),
    )(x, g.reshape(1, d), target)
    return loss[0, 0], dx, dg.reshape(d)


def _ret_tables(s, dk):
    h = jnp.arange(RET_HEADS, dtype=F32)
    log_gamma = jnp.log1p(-jnp.exp2(-5.0 - h))
    pos = jnp.arange(CHUNK, dtype=F32)
    dist = jnp.abs(pos[:, None] - pos[None, :])
    dintra = jnp.exp(log_gamma[:, None, None] * dist)
    qdec = jnp.exp(log_gamma[:, None] * (pos + 1.0))[:, :, None]
    kdec = jnp.exp(log_gamma[:, None] * (CHUNK - 1.0 - pos))[:, :, None]
    cdec = jnp.exp(log_gamma * CHUNK)[:, None, None]
    inv = ROPE_BASE ** (-jnp.arange(0, dk, 2, dtype=F32) / dk)
    ang = jnp.arange(s, dtype=F32)[:, None] * inv[None, :]
    return jnp.cos(ang), jnp.sin(ang), dintra, qdec, kdec, cdec


def _ret_specs(s, dk, dv, tb, rev):
    nh = RET_HEADS
    nb = s // tb
    bi = (lambda b: nb - 1 - b) if rev else (lambda b: b)
    voff = 2 * nh * dk // dv
    cpb = tb // CHUNK
    return dict(
        q=pl.BlockSpec((tb, dk), lambda h, b: (bi(b), h)),
        k=pl.BlockSpec((tb, dk), lambda h, b: (bi(b), nh + h)),
        v=pl.BlockSpec((tb, dv), lambda h, b: (bi(b), voff + h)),
        g=pl.BlockSpec((tb, dv), lambda h, b: (bi(b), voff + nh + h)),
        gain=pl.BlockSpec((None, 1, dv), lambda h, b: (h, 0, 0)),
        cs=pl.BlockSpec((tb, dk // 2), lambda h, b: (bi(b), 0)),
        dintra=pl.BlockSpec((None, CHUNK, CHUNK), lambda h, b: (h, 0, 0)),
        dec=pl.BlockSpec((None, CHUNK, 1), lambda h, b: (h, 0, 0)),
        cdec=pl.BlockSpec((None, 1, 1), lambda h, b: (h, 0, 0)),
        hv=pl.BlockSpec((tb, dv), lambda h, b: (bi(b), h)),
        hk=pl.BlockSpec((tb, dk), lambda h, b: (bi(b), h)),
        st=pl.BlockSpec((None, cpb, dk, dv), lambda h, b: (h, bi(b), 0, 0)),
    )


def _ret_fwd(proj, gain, tables):
    s = proj.shape[0]
    d = proj.shape[1] // 6
    dk, dv = d // RET_HEADS, 2 * d // RET_HEADS
    tb = min(TOKEN_BLOCK, s)
    cpb = tb // CHUNK
    sp = _ret_specs(s, dk, dv, tb, False)
    cos, sin, dintra, qdec, kdec, cdec = tables

    def body(q_ref, k_ref, v_ref, g_ref, gain_ref, cos_ref, sin_ref, di_ref, qd_ref, kd_ref, cd_ref,
             y_ref, st_ref, state):
        @pl.when(pl.program_id(1) == 0)
        def _():
            state[...] = jnp.zeros_like(state)

        for c in range(cpb):
            sl = pl.ds(c * CHUNK, CHUNK)
            st_ref[c] = state[...]
            y, new_state = _ret_chunk(_mm_plain, q_ref[sl, :], k_ref[sl, :], v_ref[sl, :], g_ref[sl, :],
                                      gain_ref[...], state[...], cos_ref[sl, :], sin_ref[sl, :],
                                      di_ref[...], qd_ref[...], kd_ref[...], cd_ref[...])
            y_ref[sl, :] = y.astype(BF)
            state[...] = new_state

    return pl.pallas_call(
        body, name="ret_fwd", grid=(RET_HEADS, s // tb),
        in_specs=[sp['q'], sp['k'], sp['v'], sp['g'], sp['gain'], sp['cs'], sp['cs'], sp['dintra'], sp['dec'],
                  sp['dec'], sp['cdec']],
        out_specs=[sp['hv'], sp['st']],
        out_shape=[jax.ShapeDtypeStruct((s, RET_HEADS * dv), BF),
                   jax.ShapeDtypeStruct((RET_HEADS, s // CHUNK, dk, dv), F32)],
        scratch_shapes=[pltpu.VMEM((dk, dv), F32)], compiler_params=_cparams(),
    )(proj, proj, proj, proj, gain.reshape(RET_HEADS, 1, dv), cos, sin, dintra, qdec, kdec, cdec)


def _ret_bwd(proj, gain, tables, states, dy):
    s = proj.shape[0]
    d = proj.shape[1] // 6
    dk, dv = d // RET_HEADS, 2 * d // RET_HEADS
    tb = min(TOKEN_BLOCK, s)
    cpb = tb // CHUNK
    sp = _ret_specs(s, dk, dv, tb, True)
    cos, sin, dintra, qdec, kdec, cdec = tables

    def body(q_ref, k_ref, v_ref, g_ref, gain_ref, cos_ref, sin_ref, di_ref, qd_ref, kd_ref, cd_ref, st_ref,
             dy_ref, dq_ref, dk_ref, dv_ref, dg_ref, dgain_ref, dstate):
        @pl.when(pl.program_id(1) == 0)
        def _():
            dstate[...] = jnp.zeros_like(dstate)
            dgain_ref[...] = jnp.zeros_like(dgain_ref)

        for c in reversed(range(cpb)):
            sl = pl.ds(c * CHUNK, CHUNK)
            cos_c, sin_c = cos_ref[sl, :], sin_ref[sl, :]
            di, qd, kd, cd = di_ref[...], qd_ref[...], kd_ref[...], cd_ref[...]

            def fn(q, k, v, g, gn, st):
                return _ret_chunk(_mm_diff, q, k, v, g, gn, st, cos_c, sin_c, di, qd, kd, cd)

            _, vjp = jax.vjp(fn, q_ref[sl, :], k_ref[sl, :], v_ref[sl, :], g_ref[sl, :], gain_ref[...],
                             st_ref[c])
            dq, dkk, dvv, dg, dgn, dst = vjp((dy_ref[sl, :].astype(F32), dstate[...]))
            dq_ref[sl, :] = dq.astype(BF)
            dk_ref[sl, :] = dkk.astype(BF)
            dv_ref[sl, :] = dvv.astype(BF)
            dg_ref[sl, :] = dg.astype(BF)
            dgain_ref[...] += dgn
            dstate[...] = dst

    dq, dkk, dvv, dg, dgain = pl.pallas_call(
        body, name="ret_bwd", grid=(RET_HEADS, s // tb),
        in_specs=[sp['q'], sp['k'], sp['v'], sp['g'], sp['gain'], sp['cs'], sp['cs'], sp['dintra'], sp['dec'],
                  sp['dec'], sp['cdec'], sp['st'], sp['hv']],
        out_specs=[sp['hk'], sp['hk'], sp['hv'], sp['hv'], sp['gain']],
        out_shape=[jax.ShapeDtypeStruct((s, RET_HEADS * dk), BF), jax.ShapeDtypeStruct((s, RET_HEADS * dk), BF),
                   jax.ShapeDtypeStruct((s, RET_HEADS * dv), BF), jax.ShapeDtypeStruct((s, RET_HEADS * dv), BF),
                   jax.ShapeDtypeStruct((RET_HEADS, 1, dv), F32)],
        scratch_shapes=[pltpu.VMEM((dk, dv), F32)], compiler_params=_cparams(),
    )(proj, proj, proj, proj, gain.reshape(RET_HEADS, 1, dv), cos, sin, dintra, qdec, kdec, cdec, states, dy)
    return jnp.concatenate([dq, dkk, dvv, dg], axis=1), dgain.reshape(RET_HEADS, dv)


def _gla_specs(s, dk, dv, tb, rev):
    nh = GLA_HEADS
    nb = s // tb
    bi = (lambda b: nb - 1 - b) if rev else (lambda b: b)
    voff = 2 * nh * dk // dv
    cpb = tb // CHUNK
    return dict(
        q=pl.BlockSpec((tb, dk), lambda h, b: (bi(b), h)),
        k=pl.BlockSpec((tb, dk), lambda h, b: (bi(b), nh + h)),
        v=pl.BlockSpec((tb, dv), lambda h, b: (bi(b), voff + h)),
        r=pl.BlockSpec((tb, dv), lambda h, b: (bi(b), voff + nh + h)),
        glow=pl.BlockSpec((tb, 128), lambda h, b: (bi(b), 0)),
        wgu=pl.BlockSpec((128, dk), lambda h, b: (0, h)),
        bias=pl.BlockSpec((1, dk), lambda h, b: (0, h)),
        gain=pl.BlockSpec((None, 1, dv), lambda h, b: (h, 0, 0)),
        hv=pl.BlockSpec((tb, dv), lambda h, b: (bi(b), h)),
        hk=pl.BlockSpec((tb, dk), lambda h, b: (bi(b), h)),
        hg=pl.BlockSpec((tb, 128), lambda h, b: (bi(b), h)),
        st=pl.BlockSpec((None, cpb, dv, dk), lambda h, b: (h, bi(b), 0, 0)),
    )


def _gla_fwd(proj, glow, wgu, bias, gain):
    s = proj.shape[0]
    d = proj.shape[1] // 3
    dk, dv = d // 2 // GLA_HEADS, d // GLA_HEADS
    tb = min(TOKEN_BLOCK, s)
    cpb = tb // CHUNK
    sp = _gla_specs(s, dk, dv, tb, False)

    def body(q_ref, k_ref, v_ref, r_ref, gl_ref, wgu_ref, b_ref, gain_ref, y_ref, st_ref, state):
        @pl.when(pl.program_id(1) == 0)
        def _():
            state[...] = jnp.zeros_like(state)

        for c in range(cpb):
            sl = pl.ds(c * CHUNK, CHUNK)
            st_ref[c] = state[...]
            y, new_state = _gla_chunk(_mm_plain, q_ref[sl, :], k_ref[sl, :], v_ref[sl, :], r_ref[sl, :],
                                      gl_ref[sl, :], wgu_ref[...], b_ref[...], gain_ref[...], state[...])
            y_ref[sl, :] = y.astype(BF)
            state[...] = new_state

    return pl.pallas_call(
        body, name="gla_fwd", grid=(GLA_HEADS, s // tb),
        in_specs=[sp['q'], sp['k'], sp['v'], sp['r'], sp['glow'], sp['wgu'], sp['bias'], sp['gain']],
        out_specs=[sp['hv'], sp['st']],
        out_shape=[jax.ShapeDtypeStruct((s, GLA_HEADS * dv), BF),
                   jax.ShapeDtypeStruct((GLA_HEADS, s // CHUNK, dv, dk), F32)],
        scratch_shapes=[pltpu.VMEM((dv, dk), F32)], compiler_params=_cparams(),
    )(proj, proj, proj, proj, glow, wgu, bias, gain.reshape(GLA_HEADS, 1, dv))


def _gla_bwd(proj, glow, wgu, bias, gain, states, dy):
    s = proj.shape[0]
    d = proj.shape[1] // 3
    dk, dv = d // 2 // GLA_HEADS, d // GLA_HEADS
    tb = min(TOKEN_BLOCK, s)
    cpb = tb // CHUNK
    sp = _gla_specs(s, dk, dv, tb, True)

    def body(q_ref, k_ref, v_ref, r_ref, gl_ref, wgu_ref, b_ref, gain_ref, st_ref, dy_ref,
             dq_ref, dk_ref, dv_ref, dr_ref, dgl_ref, dwgu_ref, db_ref, dgain_ref, dstate):
        @pl.when(pl.program_id(1) == 0)
        def _():
            dstate[...] = jnp.zeros_like(dstate)
            dwgu_ref[...] = jnp.zeros_like(dwgu_ref)
            db_ref[...] = jnp.zeros_like(db_ref)
            dgain_ref[...] = jnp.zeros_like(dgain_ref)

        for c in reversed(range(cpb)):
            sl = pl.ds(c * CHUNK, CHUNK)

            def fn(q, k, v, r, gl, w, b, gn, st):
                return _gla_chunk(_mm_diff, q, k, v, r, gl, w, b, gn, st)

            _, vjp = jax.vjp(fn, q_ref[sl, :], k_ref[sl, :], v_ref[sl, :], r_ref[sl, :], gl_ref[sl, :],
                             wgu_ref[...], b_ref[...], gain_ref[...], st_ref[c])
            dq, dkk, dvv, dr, dgl, dw, db, dgn, dst = vjp((dy_ref[sl, :].astype(F32), dstate[...]))
            dq_ref[sl, :] = dq.astype(BF)
            dk_ref[sl, :] = dkk.astype(BF)
            dv_ref[sl, :] = dvv.astype(BF)
            dr_ref[sl, :] = dr.astype(BF)
            dgl_ref[sl, :] = dgl.astype(BF)
            dwgu_ref[...] += dw
            db_ref[...] += db
            dgain_ref[...] += dgn
            dstate[...] = dst

    nh = GLA_HEADS
    dq, dkk, dvv, dr, dgl, dwgu, db, dgain = pl.pallas_call(
        body, name="gla_bwd", grid=(nh, s // tb),
        in_specs=[sp['q'], sp['k'], sp['v'], sp['r'], sp['glow'], sp['wgu'], sp['bias'], sp['gain'], sp['st'],
                  sp['hv']],
        out_specs=[sp['hk'], sp['hk'], sp['hv'], sp['hv'], sp['hg'], sp['wgu'], sp['bias'], sp['gain']],
        out_shape=[jax.ShapeDtypeStruct((s, nh * dk), BF), jax.ShapeDtypeStruct((s, nh * dk), BF),
                   jax.ShapeDtypeStruct((s, nh * dv), BF), jax.ShapeDtypeStruct((s, nh * dv), BF),
                   jax.ShapeDtypeStruct((s, nh * 128), BF), jax.ShapeDtypeStruct((128, nh * dk), F32),
                   jax.ShapeDtypeStruct((1, nh * dk), F32), jax.ShapeDtypeStruct((nh, 1, dv), F32)],
        scratch_shapes=[pltpu.VMEM((dv, dk), F32)], compiler_params=_cparams(),
    )(proj, proj, proj, proj, glow, wgu, bias, gain.reshape(nh, 1, dv), states, dy)
    return jnp.concatenate([dq, dkk, dvv, dr], axis=1), dgl, dwgu, db, dgain.reshape(nh, dv)


def _gdn_specs(s, dk):
    nh = GDN_HEADS
    col = lambda off: pl.BlockSpec((s, dk), lambda h: (0, off + h))
    tap = lambda off: pl.BlockSpec((CONV_WIDTH, dk), lambda h: (0, off + h))
    vec = pl.BlockSpec((1, 128), lambda h: (0, 0))
    return dict(q=col(0), k=col(nh), v=col(2 * nh), z=col(3 * nh), ba=pl.BlockSpec((s, 128), lambda h: (0, 0)),
                cq=tap(0), ck=tap(nh), cv=tap(2 * nh), vec=vec, gain=pl.BlockSpec((1, dk), lambda h: (0, 0)),
                head=col(0))


def _conv_silu(shift, x, taps):
    return _silu(_conv_taps(shift, x, taps))


def _load_taps(ref):
    return [ref[t:t + 1, :] for t in range(CONV_WIDTH)]


def _gdn_fwd(proj, ba, conv_w, a_log, dt_bias, gain):
    s = proj.shape[0]
    dk = proj.shape[1] // (4 * GDN_HEADS)
    sp = _gdn_specs(s, dk)
    rows = GDN_GROUP * CHUNK

    def grp(t):
        return t.reshape(GDN_GROUP, CHUNK, t.shape[-1])

    def body(q_ref, k_ref, v_ref, z_ref, ba_ref, cq_ref, ck_ref, cv_ref, al_ref, dt_ref, gain_ref, y_ref,
             qc, kc, vc, state):
        head = pl.program_id(0)
        qc[...] = _conv_silu(_shift_rows, q_ref[...], _load_taps(cq_ref))
        kc[...] = _conv_silu(_shift_rows, k_ref[...], _load_taps(ck_ref))
        vc[...] = _conv_silu(_shift_rows, v_ref[...], _load_taps(cv_ref))
        state[...] = jnp.zeros_like(state)

        def step(n, carry):
            base = pl.multiple_of(n * rows, rows)
            sl = pl.ds(base, rows)
            ys, new_state = _gdn_group(_mm_plain, grp(qc[sl, :]), grp(kc[sl, :]), grp(vc[sl, :]), grp(z_ref[sl, :]),
                                       grp(ba_ref[sl, :]), al_ref[...], dt_ref[...], gain_ref[...], state[...], head)
            for i, y in enumerate(ys):
                y_ref[pl.ds(base + i * CHUNK, CHUNK), :] = y.astype(BF)
            state[...] = new_state
            return carry

        lax.fori_loop(0, s // rows, step, 0)

    return pl.pallas_call(
        body, name="gdn_fwd", grid=(GDN_HEADS,),
        in_specs=[sp['q'], sp['k'], sp['v'], sp['z'], sp['ba'], sp['cq'], sp['ck'], sp['cv'], sp['vec'], sp['vec'],
                  sp['gain']],
        out_specs=sp['head'], out_shape=jax.ShapeDtypeStruct((s, GDN_HEADS * dk), BF),
        scratch_shapes=[pltpu.VMEM((s, dk), F32)] * 3 + [pltpu.VMEM((dk, dk), F32)],
        compiler_params=_cparams(),
    )(proj, proj, proj, proj, ba, conv_w, conv_w, conv_w, a_log, dt_bias, gain)


def _gdn_bwd(proj, ba, conv_w, a_log, dt_bias, gain, dy):
    s = proj.shape[0]
    dk = proj.shape[1] // (4 * GDN_HEADS)
    sp = _gdn_specs(s, dk)
    rows = GDN_GROUP * CHUNK
    ngroups = s // rows

    def grp(t):
        return t.reshape(GDN_GROUP, CHUNK, t.shape[-1])

    def body(q_ref, k_ref, v_ref, z_ref, ba_ref, cq_ref, ck_ref, cv_ref, al_ref, dt_ref, gain_ref, dy_ref,
             dq_ref, dk_ref, dv_ref, dz_ref, dba_ref, dcq_ref, dck_ref, dcv_ref, dal_ref, ddt_ref, dgain_ref,
             qc, kc, vc, dqc, dkc, dvc, states, state, dstate):
        head = pl.program_id(0)

        @pl.when(head == 0)
        def _():
            dba_ref[...] = jnp.zeros_like(dba_ref)
            dal_ref[...] = jnp.zeros_like(dal_ref)
            ddt_ref[...] = jnp.zeros_like(ddt_ref)
            dgain_ref[...] = jnp.zeros_like(dgain_ref)

        qc[...] = _conv_silu(_shift_rows, q_ref[...], _load_taps(cq_ref))
        kc[...] = _conv_silu(_shift_rows, k_ref[...], _load_taps(ck_ref))
        vc[...] = _conv_silu(_shift_rows, v_ref[...], _load_taps(cv_ref))
        state[...] = jnp.zeros_like(state)

        def fstep(n, carry):
            sl = pl.ds(pl.multiple_of(n * rows, rows), rows)
            states[n] = state[...]
            _, new_state = _gdn_group(_mm_plain, grp(qc[sl, :]), grp(kc[sl, :]), grp(vc[sl, :]), grp(z_ref[sl, :]),
                                      grp(ba_ref[sl, :]), al_ref[...], dt_ref[...], gain_ref[...], state[...], head)
            state[...] = new_state
            return carry

        lax.fori_loop(0, ngroups, fstep, 0)
        dstate[...] = jnp.zeros_like(dstate)

        def bstep(i, carry):
            n = ngroups - 1 - i
            base = pl.multiple_of(n * rows, rows)
            sl = pl.ds(base, rows)

            def fn(q, k, v, z, b, al, dt, gn, st):
                return _gdn_group(_mm_diff, q, k, v, z, b, al, dt, gn, st, head)

            _, vjp = jax.vjp(fn, grp(qc[sl, :]), grp(kc[sl, :]), grp(vc[sl, :]), grp(z_ref[sl, :]),
                             grp(ba_ref[sl, :]), al_ref[...], dt_ref[...], gain_ref[...], states[n])
            dys = [dy_ref[pl.ds(base + j * CHUNK, CHUNK), :].astype(F32) for j in range(GDN_GROUP)]
            dq, dkk, dvv, dz, db, dal, ddt, dgn, dst = vjp((dys, dstate[...]))
            dqc[sl, :] = dq.reshape(rows, dk)
            dkc[sl, :] = dkk.reshape(rows, dk)
            dvc[sl, :] = dvv.reshape(rows, dk)
            dz_ref[sl, :] = dz.reshape(rows, dk).astype(BF)
            dba_ref[sl, :] += db.reshape(rows, db.shape[-1])
            dal_ref[...] += dal
            ddt_ref[...] += ddt
            dgain_ref[...] += dgn
            dstate[...] = dst
            return carry

        lax.fori_loop(0, ngroups, bstep, 0)

        for x_ref, c_ref, dpost, dx_ref, dc_ref in ((q_ref, cq_ref, dqc, dq_ref, dcq_ref),
                                                    (k_ref, ck_ref, dkc, dk_ref, dck_ref),
                                                    (v_ref, cv_ref, dvc, dv_ref, dcv_ref)):
            _, vjp = jax.vjp(lambda x, *taps: _conv_silu(_shift_vjp, x, taps), x_ref[...], *_load_taps(c_ref))
            grads = vjp(dpost[...])
            dx_ref[...] = grads[0].astype(BF)
            for t in range(CONV_WIDTH):
                dc_ref[t:t + 1, :] = grads[1 + t]

    nh = GDN_HEADS
    col_bf = jax.ShapeDtypeStruct((s, nh * dk), BF)
    tap_out = jax.ShapeDtypeStruct((CONV_WIDTH, nh * dk), F32)
    tap_spec = pl.BlockSpec((CONV_WIDTH, dk), lambda h: (0, h))
    dq, dkk, dvv, dz, dba, dcq, dck, dcv, dal, ddt, dgain = pl.pallas_call(
        body, name="gdn_bwd", grid=(nh,),
        in_specs=[sp['q'], sp['k'], sp['v'], sp['z'], sp['ba'], sp['cq'], sp['ck'], sp['cv'], sp['vec'], sp['vec'],
                  sp['gain'], sp['head']],
        out_specs=[sp['head']] * 4 + [sp['ba'], tap_spec, tap_spec, tap_spec, sp['vec'], sp['vec'], sp['gain']],
        out_shape=[col_bf] * 4 + [jax.ShapeDtypeStruct((s, 128), F32), tap_out, tap_out, tap_out,
                                  jax.ShapeDtypeStruct((1, 128), F32), jax.ShapeDtypeStruct((1, 128), F32),
                                  jax.ShapeDtypeStruct((1, dk), F32)],
        scratch_shapes=[pltpu.VMEM((s, dk), F32)] * 6 + [pltpu.VMEM((ngroups, dk, dk), F32),
                                                        pltpu.VMEM((dk, dk), F32), pltpu.VMEM((dk, dk), F32)],
        compiler_params=_cparams(),
    )(proj, proj, proj, proj, ba, conv_w, conv_w, conv_w, a_log, dt_bias, gain, dy)
    return (jnp.concatenate([dq, dkk, dvv, dz], axis=1), dba, jnp.concatenate([dcq, dck, dcv], axis=1), dal, ddt,
            dgain)


def _lru_specs(s, bw):
    nb = LRU_BLOCKS
    return dict(
        xb=pl.BlockSpec((s, bw), lambda n: (0, n)), yb=pl.BlockSpec((s, bw), lambda n: (0, nb + n)),
        taps=pl.BlockSpec((CONV_WIDTH, bw), lambda n: (0, n)), vec=pl.BlockSpec((1, bw), lambda n: (0, n)),
        w=pl.BlockSpec((None, bw, bw), lambda n: (n, 0, 0)), b=pl.BlockSpec((None, 1, bw), lambda n: (n, 0, 0)),
        col=pl.BlockSpec((s, bw), lambda n: (0, n)))


def _lru_fwd(proj, conv_w, conv_b, lam, wr, br, wi, bi):
    s = proj.shape[0]
    bw = proj.shape[1] // (2 * LRU_BLOCKS)
    sp = _lru_specs(s, bw)

    def body(xb_ref, yb_ref, cw_ref, cb_ref, lam_ref, wr_ref, br_ref, wi_ref, bi_ref, y_ref):
        a, u, gy = _lru_pre(_mm_plain, _shift_rows, xb_ref[...], yb_ref[...], _load_taps(cw_ref), cb_ref[...],
                            lam_ref[...], wr_ref[...], br_ref[...], wi_ref[...], bi_ref[...])
        y_ref[...] = (_linscan(a, u, False) * gy).astype(BF)

    return pl.pallas_call(
        body, name="lru_fwd", grid=(LRU_BLOCKS,),
        in_specs=[sp['xb'], sp['yb'], sp['taps'], sp['vec'], sp['vec'], sp['w'], sp['b'], sp['w'], sp['b']],
        out_specs=sp['col'], out_shape=jax.ShapeDtypeStruct((s, LRU_BLOCKS * bw), BF),
        compiler_params=_cparams(),
    )(proj, proj, conv_w, conv_b, lam, wr, br, wi, bi)


def _lru_bwd(proj, conv_w, conv_b, lam, wr, br, wi, bi, dy):
    s = proj.shape[0]
    nb = LRU_BLOCKS
    bw = proj.shape[1] // (2 * nb)
    sp = _lru_specs(s, bw)

    def body(xb_ref, yb_ref, cw_ref, cb_ref, lam_ref, wr_ref, br_ref, wi_ref, bi_ref, dy_ref,
             dxb_ref, dyb_ref, dcw_ref, dcb_ref, dlam_ref, dwr_ref, dbr_ref, dwi_ref, dbi_ref):
        def pre(xb, yb, t0, t1, t2, t3, cb, lm, w_r, b_r, w_i, b_i):
            return _lru_pre(_mm_diff, _shift_vjp, xb, yb, (t0, t1, t2, t3), cb, lm, w_r, b_r, w_i, b_i)

        (a, u, gy), vjp = jax.vjp(pre, xb_ref[...], yb_ref[...], *_load_taps(cw_ref), cb_ref[...], lam_ref[...],
                                  wr_ref[...], br_ref[...], wi_ref[...], bi_ref[...])
        h = _linscan(a, u, False)
        dout = dy_ref[...].astype(F32)
        g = _linscan(_shift_rows(a, 1, True), dout * gy, True)
        grads = vjp((g * _shift_rows(h, 1, False), g, dout * h))
        dxb_ref[...] = grads[0].astype(BF)
        dyb_ref[...] = grads[1].astype(BF)
        for t in range(CONV_WIDTH):
            dcw_ref[t:t + 1, :] = grads[2 + t]
        dcb_ref[...] = grads[6]
        dlam_ref[...] = grads[7]
        dwr_ref[...] = grads[8]
        dbr_ref[...] = grads[9]
        dwi_ref[...] = grads[10]
        dbi_ref[...] = grads[11]

    outs = pl.pallas_call(
        body, name="lru_bwd", grid=(nb,),
        in_specs=[sp['xb'], sp['yb'], sp['taps'], sp['vec'], sp['vec'], sp['w'], sp['b'], sp['w'], sp['b'], sp['col']],
        out_specs=[sp['col'], sp['col'], sp['taps'], sp['vec'], sp['vec'], sp['w'], sp['b'], sp['w'], sp['b']],
        out_shape=[jax.ShapeDtypeStruct((s, nb * bw), BF), jax.ShapeDtypeStruct((s, nb * bw), BF),
                   jax.ShapeDtypeStruct((CONV_WIDTH, nb * bw), F32), jax.ShapeDtypeStruct((1, nb * bw), F32),
                   jax.ShapeDtypeStruct((1, nb * bw), F32), jax.ShapeDtypeStruct((nb, bw, bw), F32),
                   jax.ShapeDtypeStruct((nb, 1, bw), F32), jax.ShapeDtypeStruct((nb, bw, bw), F32),
                   jax.ShapeDtypeStruct((nb, 1, bw), F32)],
        compiler_params=_cparams(),
    )(proj, proj, conv_w, conv_b, lam, wr, br, wi, bi, dy)
    return (jnp.concatenate([outs[0], outs[1]], axis=1),) + tuple(outs[2:])


def _row_tile(rows, cols, bytes_per_row_set):
    t = 8
    while t * 2 <= rows and rows % (t * 2) == 0 and (t * 2) * cols * bytes_per_row_set <= 4 * 1024 * 1024:
        t *= 2
    return t


def _sum_slabs(a, out_dtype, name):
    n, rows, cols = a.shape
    tr = _row_tile(rows, cols, 4 * (n + 1))

    def body(*refs):
        acc = refs[0][...].astype(F32)
        for r in refs[1:n]:
            acc = acc + r[...].astype(F32)
        refs[n][...] = acc.astype(out_dtype)

    specs = [pl.BlockSpec((None, tr, cols), functools.partial(lambda i, k: (k, i, 0), k=k)) for k in range(n)]
    return pl.pallas_call(
        body, name=name, grid=(rows // tr,), in_specs=specs, out_specs=pl.BlockSpec((tr, cols), lambda i: (i, 0)),
        out_shape=jax.ShapeDtypeStruct((rows, cols), out_dtype), compiler_params=_cparams(),
    )(*([a] * n))


def _sum_keep_and_landed(g, landed, core, name):
    four, hr, cols = landed.shape
    tr = _row_tile(hr, cols, 2 * 3)
    tph = hr // tr

    def body(core_ref, g_ref, l_ref, o_ref):
        o_ref[...] = (g_ref[...].astype(F32) + l_ref[...].astype(F32)).astype(o_ref.dtype)

    grid_spec = pltpu.PrefetchScalarGridSpec(
        num_scalar_prefetch=1, grid=(four, tph),
        in_specs=[pl.BlockSpec((None, tr, cols), lambda j, r, core_ref: (j, core_ref[0] * tph + r, 0)),
                  pl.BlockSpec((None, tr, cols), lambda j, r, core_ref: (j, r, 0))],
        out_specs=pl.BlockSpec((None, tr, cols), lambda j, r, core_ref: (j, r, 0)))
    return pl.pallas_call(body, name=name, grid_spec=grid_spec, out_shape=jax.ShapeDtypeStruct(landed.shape, g.dtype),
                          compiler_params=_cparams())(core, g, landed)


def _sum_chip_parts(parts, landed, chip, name):
    _, rows, cols = parts.shape
    tr = _row_tile(rows, cols, 2 * 4 + 4)

    def body(chip_ref, p_ref, l0_ref, l1_ref, l2_ref, o_ref):
        acc = l0_ref[...].astype(F32) + l1_ref[...].astype(F32)
        o_ref[...] = (acc + l2_ref[...].astype(F32)) + p_ref[...].astype(F32)

    slab = lambda k: pl.BlockSpec((None, tr, cols), lambda i, chip_ref: (k, i, 0))
    grid_spec = pltpu.PrefetchScalarGridSpec(
        num_scalar_prefetch=1, grid=(rows // tr,),
        in_specs=[pl.BlockSpec((None, tr, cols), lambda i, chip_ref: (chip_ref[0], i, 0)), slab(0), slab(1), slab(2)],
        out_specs=pl.BlockSpec((tr, cols), lambda i, chip_ref: (i, 0)))
    return pl.pallas_call(body, name=name, grid_spec=grid_spec, out_shape=jax.ShapeDtypeStruct((rows, cols), F32),
                          compiler_params=_cparams())(chip, parts, landed, landed, landed)


_ADAM_C1 = 1.0 / (1.0 - ADAM_B1 ** ADAM_STEP)
_ADAM_C2 = 1.0 / (1.0 - ADAM_B2 ** ADAM_STEP)


def _adamw_math(w, g, m, v):
    nm = ADAM_B1 * m + (1.0 - ADAM_B1) * g
    nv = ADAM_B2 * v + (1.0 - ADAM_B2) * (g * g)
    delta = -ADAM_LR * ((nm * _ADAM_C1) / (jnp.sqrt(nv * _ADAM_C2) + ADAM_EPS) + ADAM_WD * w)
    return delta, nm, nv


def _adamw(w, g, m, v, name):
    rows, cols = w.shape
    tr = _row_tile(rows, cols, 4 * 7)

    def body(w_ref, g_ref, m_ref, v_ref, d_ref, nm_ref, nv_ref):
        d_ref[...], nm_ref[...], nv_ref[...] = _adamw_math(w_ref[...], g_ref[...], m_ref[...], v_ref[...])

    spec = pl.BlockSpec((tr, cols), lambda i: (i, 0))
    shape = jax.ShapeDtypeStruct((rows, cols), F32)
    return pl.pallas_call(
        body, name=name, grid=(rows // tr,), in_specs=[spec] * 4, out_specs=[spec] * 3, out_shape=[shape] * 3,
        compiler_params=_cparams(),
    )(w, g, m, v)


def _adamw_halves(w, m, v, layer, mine, theirs, core, name, carried=None):
    n_layers, rows, cols = w.shape
    hr = mine.shape[0]
    tr = _row_tile(hr, cols, 4 * 9)
    tph = hr // tr

    def body(core_ref, w_ref, q_ref, t_ref, m_ref, v_ref, *rest):
        g_ref, d_ref, nm_ref, nv_ref = rest[-4:]
        is_mine = (pl.program_id(0) // tph) == core_ref[0]
        g = jnp.where(is_mine, q_ref[...], t_ref[...])
        g_ref[...] = g
        d_ref[...], nm_ref[...], nv_ref[...] = _adamw_math(w_ref[...], g, m_ref[...], v_ref[...])

    slab = pl.BlockSpec((None, tr, cols), lambda i, core_ref: (layer, i, 0))
    half = pl.BlockSpec((tr, cols), lambda i, core_ref: (i % tph, 0))
    in_specs = [slab, half, half, slab, slab]
    operands = [core, w, mine, theirs, m, v]
    aliases = {}
    if carried is not None:
        in_specs += [_ANY] * 4
        aliases = {len(operands) + k: k for k in range(4)}
        operands += list(carried)
    grid_spec = pltpu.PrefetchScalarGridSpec(num_scalar_prefetch=1, grid=(rows // tr,), in_specs=in_specs,
                                             out_specs=[slab] * 4)
    return pl.pallas_call(
        body, name=name, grid_spec=grid_spec, out_shape=[jax.ShapeDtypeStruct(w.shape, F32)] * 4,
        input_output_aliases=aliases, compiler_params=_cparams(),
    )(*operands)


_ANY = pl.BlockSpec(memory_space=pl.ANY)


def _mesh_pos():
    return lax.axis_index("x"), lax.axis_index("y"), lax.axis_index("c")


def _remote(src, dst, send_sems, recv_sems, k, dev):
    return pltpu.make_async_remote_copy(src_ref=src, dst_ref=dst, send_sem=send_sems.at[k], recv_sem=recv_sems.at[k],
                                        device_id=dev, device_id_type=MESH)


STREAM_CHUNK_BYTES = 1024 * 1024
STREAM_SLOTS = 3


def _chunk_rows(rows, cols, itemsize):
    t = 16
    while t * 2 <= rows and rows % (t * 2) == 0 and (t * 2) * cols * itemsize <= STREAM_CHUNK_BYTES:
        t *= 2
    return t


def _stream_chunks(n_chunks, src_at, dst_at, buf, load_sems, send_sems, recv_sem, sibling):
    def load(k, slot):
        return pltpu.make_async_copy(src_at(k), buf.at[slot], load_sems.at[slot])

    def send(k, slot):
        return pltpu.make_async_remote_copy(src_ref=buf.at[slot], dst_ref=dst_at(k), send_sem=send_sems.at[slot],
                                            recv_sem=recv_sem, device_id=sibling, device_id_type=MESH)

    def step(k, carry):
        slot = k % STREAM_SLOTS

        @pl.when(k >= STREAM_SLOTS)
        def _():
            send(k - STREAM_SLOTS, slot).wait_send()

        load(k, slot).start()

        @pl.when(k >= 1)
        def _():
            prev = (k - 1) % STREAM_SLOTS
            load(k - 1, prev).wait()
            send(k - 1, prev).start()

        return carry

    lax.fori_loop(0, n_chunks, step, 0)
    last = (n_chunks - 1) % STREAM_SLOTS
    load(n_chunks - 1, last).wait()
    send(n_chunks - 1, last).start()
    for k in range(max(0, n_chunks - STREAM_SLOTS), n_chunks):
        send(k, k % STREAM_SLOTS).wait_send()


_HBM = pl.BlockSpec(memory_space=pltpu.HBM)
_SEM = pl.BlockSpec(memory_space=pltpu.SEMAPHORE)
_DATAFLOW = pltpu.SideEffectType.DATAFLOW_SIDE_EFFECTING


def _chip_copies(kind, ins, lands, split, send_sems, recv_sems):
    x, y, c = _mesh_pos()
    me = 2 * x + y
    chips = [(1 - x, y), (x, 1 - y), (1 - x, 1 - y)]
    pairs = []
    for i in range(len(ins)):
        for j, chip in enumerate(chips):
            pj = 2 * chip[0] + chip[1]
            if kind == 'scatter':
                src, dst, got = ins[i].at[pj], lands[i].at[j], lands[i].at[j]
            elif split[i]:
                hr = ins[i].shape[0] // 2
                rows = pl.ds(c * hr, hr)
                src, dst, got = ins[i].at[rows], lands[i].at[me, rows], lands[i].at[pj, rows]
            else:
                src, dst, got = ins[i], lands[i].at[me], lands[i].at[pj]
            k = 3 * i + j
            pairs.append((_remote(src, dst, send_sems, recv_sems, k, (*chip, c)),
                          _remote(got, got, send_sems, recv_sems, k, (*chip, c))))
    return pairs


def _chip_exchange_start(kind, srcs, land_shapes, split, name, after):
    n = len(srcs)

    def body(*refs):
        ins, lands = refs[:n], refs[n:2 * n]
        send_sems, recv_sems = refs[2 * n + 1], refs[2 * n + 2]
        token = refs[-1]
        for send, _ in _chip_copies(kind, ins, lands, split, send_sems, recv_sems):
            send.start()
        token[...] = jnp.zeros_like(token)

    hbm = lambda t: pltpu.with_memory_space_constraint(t, pltpu.HBM)
    operands = [hbm(s) for s in srcs] + [hbm(lax.empty(shp, s.dtype)) for shp, s in zip(land_shapes, srcs)] + [after]
    out = pl.pallas_call(
        body, name=name, in_specs=[_HBM] * (2 * n) + [_ANY],
        out_specs=[_SEM, _SEM] + [_HBM] * (2 * n) + [pl.BlockSpec(memory_space=pltpu.VMEM)],
        out_shape=[pltpu.SemaphoreType.DMA((3 * n,)), pltpu.SemaphoreType.DMA((3 * n,))]
        + [pltpu.HBM(s.shape, s.dtype) for s in srcs] + [pltpu.HBM(shp, s.dtype) for shp, s in zip(land_shapes, srcs)]
        + [jax.ShapeDtypeStruct((8, 128), F32)],
        input_output_aliases={i: 2 + i for i in range(2 * n)},
        compiler_params=pltpu.CompilerParams(has_side_effects=_DATAFLOW),
    )(*operands)
    return out[0], out[1], out[2:2 + n], out[2 + n:2 + 2 * n], out[-1]


def _chip_exchange_wait(kind, started, split, name, after):
    send_sems, recv_sems, srcs, lands, _ = started
    n = len(srcs)

    def body(*refs):
        ins, land_refs = refs[:n], refs[n:2 * n]
        for send, arrived in _chip_copies(kind, ins, land_refs, split, refs[2 * n], refs[2 * n + 1]):
            send.wait_send()
            arrived.wait_recv()

    out = pl.pallas_call(
        body, name=name, in_specs=[_HBM] * (2 * n) + [_SEM, _SEM, _ANY], out_specs=[_HBM] * (2 * n),
        out_shape=[pltpu.HBM(s.shape, s.dtype) for s in srcs] + [pltpu.HBM(l.shape, l.dtype) for l in lands],
        input_output_aliases={i: i for i in range(2 * n)},
        compiler_params=pltpu.CompilerParams(has_side_effects=_DATAFLOW),
    )(*srcs, *lands, send_sems, recv_sems, after)
    return out[:n], out[n:]


def _pass_to_sibling(gathered, name):
    n = len(gathered)
    tr = [_chunk_rows(g.shape[1] // 2, g.shape[2], g.dtype.itemsize) for g in gathered]

    def body(*refs):
        outs = refs[n:2 * n]
        recv_sems, load_sems, send_sems = refs[2 * n:2 * n + 3]
        bufs = refs[2 * n + 3:]
        x, y, c = _mesh_pos()
        sibling = (x, y, 1 - c)
        chips = [(1 - x, y), (x, 1 - y), (1 - x, 1 - y)]
        for i in range(n):
            hr = outs[i].shape[1] // 2
            for j, chip in enumerate(chips):
                def rows_at(k, i=i, pj=2 * chip[0] + chip[1], hr=hr):
                    return outs[i].at[pj, pl.ds(c * hr + k * tr[i], tr[i])]

                _stream_chunks(hr // tr[i], rows_at, rows_at, bufs[i], load_sems, send_sems, recv_sems.at[3 * i + j],
                               sibling)
        for i in range(n):
            hr = outs[i].shape[1] // 2
            for j, chip in enumerate(chips):
                blk = outs[i].at[2 * chip[0] + chip[1], pl.ds((1 - c) * hr, hr)]
                pltpu.make_async_remote_copy(src_ref=blk, dst_ref=blk, send_sem=send_sems.at[0],
                                             recv_sem=recv_sems.at[3 * i + j], device_id=sibling,
                                             device_id_type=MESH).wait_recv()

    return pl.pallas_call(
        body, name=name, in_specs=[_ANY] * n, out_specs=[_ANY] * n,
        out_shape=[jax.ShapeDtypeStruct(g.shape, g.dtype) for g in gathered],
        input_output_aliases={i: i for i in range(n)},
        scratch_shapes=[pltpu.SemaphoreType.DMA((3 * n,)), pltpu.SemaphoreType.DMA((STREAM_SLOTS,)),
                        pltpu.SemaphoreType.DMA((STREAM_SLOTS,))]
        + [pltpu.VMEM((STREAM_SLOTS, tr[i], g.shape[2]), g.dtype) for i, g in enumerate(gathered)],
        compiler_params=_cparams(),
    )(*gathered)


def _all_gather_devices(part, name):
    def body(in_ref, out_ref, send_sems, recv_sems, local_sem):
        x, y, c = _mesh_pos()
        me = 4 * x + 2 * y + c
        local = pltpu.make_async_copy(in_ref, out_ref.at[me], local_sem)
        local.start()
        peers = [(x ^ (k >> 2), y ^ ((k >> 1) & 1), c ^ (k & 1)) for k in range(1, N_DEV)]
        sends = [_remote(in_ref, out_ref.at[me], send_sems, recv_sems, k, p) for k, p in enumerate(peers)]
        for cp in sends:
            cp.start()
        for k, p in enumerate(peers):
            blk = out_ref.at[4 * p[0] + 2 * p[1] + p[2]]
            _remote(blk, blk, send_sems, recv_sems, k, p).wait_recv()
        for cp in sends:
            cp.wait_send()
        local.wait()

    return pl.pallas_call(
        body, name=name, in_specs=[_ANY], out_specs=_ANY,
        out_shape=jax.ShapeDtypeStruct((N_DEV,) + part.shape, part.dtype),
        scratch_shapes=[pltpu.SemaphoreType.DMA((N_DEV - 1,)), pltpu.SemaphoreType.DMA((N_DEV - 1,)),
                        pltpu.SemaphoreType.DMA],
    )(part)


def _stream_to_sibling(srcs, halved, name):
    n = len(srcs)
    geo = []
    for s in srcs:
        rows = s.shape[1] // 2 if halved else s.shape[1]
        geo.append((s.shape[0], rows, s.shape[2], _chunk_rows(rows, s.shape[2], s.dtype.itemsize)))

    def body(*refs):
        ins, outs = refs[:n], refs[n:2 * n]
        recv_sems, load_sems, send_sems = refs[2 * n:2 * n + 3]
        bufs = refs[2 * n + 3:]
        x, y, c = _mesh_pos()
        sibling = (x, y, 1 - c)
        for i in range(n):
            slabs, rows, _, tr = geo[i]
            per_slab = rows // tr
            off = (1 - c) * rows if halved else 0

            def src_at(k, i=i, per_slab=per_slab, tr=tr, off=off):
                return ins[i].at[k // per_slab, pl.ds(off + (k % per_slab) * tr, tr)]

            def dst_at(k, i=i, per_slab=per_slab, tr=tr):
                return outs[i].at[k // per_slab, pl.ds((k % per_slab) * tr, tr)]

            _stream_chunks(slabs * per_slab, src_at, dst_at, bufs[i], load_sems, send_sems, recv_sems.at[i], sibling)
        for i in range(n):
            pltpu.make_async_remote_copy(src_ref=outs[i], dst_ref=outs[i], send_sem=send_sems.at[0],
                                         recv_sem=recv_sems.at[i], device_id=sibling, device_id_type=MESH).wait_recv()

    return pl.pallas_call(
        body, name=name, in_specs=[_ANY] * n, out_specs=[_ANY] * n,
        out_shape=[jax.ShapeDtypeStruct((g[0], g[1], g[2]), s.dtype) for g, s in zip(geo, srcs)],
        scratch_shapes=[pltpu.SemaphoreType.DMA((n,)), pltpu.SemaphoreType.DMA((STREAM_SLOTS,)),
                        pltpu.SemaphoreType.DMA((STREAM_SLOTS,))]
        + [pltpu.VMEM((STREAM_SLOTS, g[3], g[2]), s.dtype) for g, s in zip(geo, srcs)],
        compiler_params=_cparams(),
    )(*srcs)


def _reduce_scatter_begin(grads, tag, core):
    landed = _stream_to_sibling(grads, True, f"rs_swap_{tag}")
    parts = [_sum_keep_and_landed(g, l, core, f"rs_add2_{tag}_{i}") for i, (g, l) in enumerate(zip(grads, landed))]
    return _chip_exchange_start('scatter', parts, [(3,) + p.shape[1:] for p in parts], None, f"rs_scatter_start_{tag}",
                                core)


def _reduce_scatter_finish(started, tag, chip, after):
    parts, landed = _chip_exchange_wait('scatter', started, None, f"rs_scatter_wait_{tag}", after)
    mine = [_sum_chip_parts(p, l, chip, f"rs_add4_{tag}_{i}") for i, (p, l) in enumerate(zip(parts, landed))]
    theirs = _stream_to_sibling([m[None] for m in mine], False, f"rs_join_{tag}")
    return [(m, t[0]) for m, t in zip(mine, theirs)]


def _pad_cols(a, width=128):
    return jnp.pad(a, ((0, 0), (0, width - a.shape[1])))


def _mixer_forward(kind, hn, w, tables):
    if kind == 0:
        proj = _matmul(hn, w['ret_w_in'], name="ret_proj", b_sharded=True)
        og, states = _ret_fwd(proj, w['ret_gn_gain'], tables)
        return og, (proj, states)
    if kind == 1:
        proj = _matmul(hn, w['gdn_w_main'], name="gdn_proj")
        ba = _matmul(hn, w['gdn_w_small'], name="gdn_proj_ba")
        og = _gdn_fwd(proj, ba, w['gdn_conv_w'], w['gdn_a_log'], w['gdn_dt_bias'], w['gdn_norm_gain'])
        return og, (proj, ba)
    if kind == 2:
        proj = _matmul(hn, w['gla_w_main'], name="gla_proj")
        glow = _matmul(hn, w['gla_w_small'], name="gla_proj_gate")
        og, states = _gla_fwd(proj, glow, w['gla_w_gate_up'], w['gla_gate_bias'], w['gla_norm_gain'])
        return og, (proj, glow, states)
    proj = _matmul(hn, w['lru_w_in'], name="lru_proj", b_sharded=True)
    og = _lru_fwd(proj, w['lru_conv_w'], w['lru_conv_b'], w['lru_lambda'], w['lru_w_rgate'], w['lru_b_rgate'],
                  w['lru_w_igate'], w['lru_b_igate'])
    return og, (proj,)


def _mixer_backward(kind, hn, w, tables, saved, d_og, grads):
    d = hn.shape[1]
    if kind == 0:
        proj, states = saved
        d_proj, grads['ret_gn_gain'] = _ret_bwd(proj, w['ret_gn_gain'], tables, states, d_og)
        grads['ret_w_in'] = _matmul(hn, d_proj, name="ret_dw_in", ta=True, out_dtype=BF, o_sharded=True)
        return _matmul(d_proj, w['ret_w_in'], name="ret_dhn", tb=True, b_sharded=True)
    if kind == 1:
        proj, ba = saved
        d_proj, d_ba, grads['gdn_conv_w'], grads['gdn_a_log'], grads['gdn_dt_bias'], grads['gdn_norm_gain'] = _gdn_bwd(
            proj, ba, w['gdn_conv_w'], w['gdn_a_log'], w['gdn_dt_bias'], w['gdn_norm_gain'], d_og)
        d_ba = d_ba.astype(BF)
        dw_main = _matmul(hn, d_proj, name="gdn_dw_main", ta=True, out_dtype=BF)
        dw_small = _matmul(hn, d_ba, name="gdn_dw_small", ta=True, out_dtype=BF)
        dw = jnp.concatenate([dw_main, dw_small[:, :2 * GDN_HEADS]], axis=1)
        grads['gdn_w_in'] = dw.reshape(d, N_CHIPS, dw.shape[1] // N_CHIPS).transpose(1, 0, 2)
        d_hn = _matmul(d_proj, w['gdn_w_main'], name="gdn_dhn_main", tb=True)
        return _matmul(d_ba, w['gdn_w_small'], name="gdn_dhn_small", tb=True, epilogue='add', extra=d_hn)
    if kind == 2:
        proj, glow, states = saved
        d_proj, d_glow4, d_wgu, grads['gla_gate_bias'], grads['gla_norm_gain'] = _gla_bwd(
            proj, glow, w['gla_w_gate_up'], w['gla_gate_bias'], w['gla_norm_gain'], states, d_og)
        grads['gla_w_gate_up'] = d_wgu[:GLA_RANK]
        dw_main = _matmul(hn, d_proj, name="gla_dw_main", ta=True, out_dtype=BF)
        dw_small4 = _matmul(hn, d_glow4, name="gla_dw_small", ta=True, out_dtype=F32)
        dw_small = dw_small4.reshape(d, GLA_HEADS, 128)[:, :, :GLA_RANK].sum(axis=1).astype(BF)
        dw = jnp.concatenate([dw_main, dw_small], axis=1)
        grads['gla_w_in'] = dw.reshape(d, N_CHIPS, dw.shape[1] // N_CHIPS).transpose(1, 0, 2)
        d_hn = _matmul(d_proj, w['gla_w_main'], name="gla_dhn_main", tb=True)
        w_small4 = jnp.tile(w['gla_w_small'], (1, GLA_HEADS))
        return _matmul(d_glow4, w_small4, name="gla_dhn_small", tb=True, epilogue='add', extra=d_hn)
    (proj,) = saved
    (d_proj, grads['lru_conv_w'], grads['lru_conv_b'], grads['lru_lambda'], grads['lru_w_rgate'], grads['lru_b_rgate'],
     grads['lru_w_igate'], grads['lru_b_igate']) = _lru_bwd(
        proj, w['lru_conv_w'], w['lru_conv_b'], w['lru_lambda'], w['lru_w_rgate'], w['lru_b_rgate'], w['lru_w_igate'],
        w['lru_b_igate'], d_og)
    grads['lru_w_in'] = _matmul(hn, d_proj, name="lru_dw_in", ta=True, out_dtype=BF, o_sharded=True)
    return _matmul(d_proj, w['lru_w_in'], name="lru_dhn", tb=True, b_sharded=True)


_W_OUT = ('ret_w_out', 'gdn_w_out', 'gla_w_out', 'lru_w_out')
_W_IN = ('ret_w_in', 'gdn_w_in', 'gla_w_in', 'lru_w_in')


def _layer_forward(layer, x, w, tables, token=None):
    hn = _rmsnorm_fwd(x, w['norm1'][layer], f"norm1_fwd_{layer}", token)
    og, mixer_saved = _mixer_forward(layer, hn, w, tables)
    x1 = _matmul(og, w[_W_OUT[layer]], name=f"mixer_out_{layer}", epilogue='add', extra=x)
    hn2 = _rmsnorm_fwd(x1, w['norm2'][layer], f"norm2_fwd_{layer}")
    act = _matmul(hn2, w['mlp_w_up'][layer], name="mlp_up", b_sharded=True, epilogue='relu2', out_dtype=BF)
    x2 = _matmul(act, w['mlp_w_down'][layer], name="mlp_down", epilogue='add', extra=x1)
    return x2, (x, hn, mixer_saved, og, x1, hn2, act)


def _layer_backward(layer, dx2, w, tables, saved, token=None):
    x, hn, mixer_saved, og, x1, hn2, act = saved
    d = x.shape[1]
    grads = {}
    d_up = _matmul(dx2, w['mlp_w_down'][layer], name="mlp_d_up", tb=True, epilogue='dact', extra=act, out_dtype=BF,
                   token=token)
    dw_down = _matmul(act, dx2, name="mlp_dw_down", ta=True, out_dtype=BF)
    grads['mlp_w_down'] = dw_down.reshape(N_CHIPS, dw_down.shape[0] // N_CHIPS, d)
    grads['mlp_w_up'] = _matmul(hn2, d_up, name="mlp_dw_up", ta=True, out_dtype=BF, o_sharded=True)
    d_hn2 = _matmul(d_up, w['mlp_w_up'][layer], name="mlp_d_hn", tb=True, b_sharded=True)
    dx1, grads['norm2'] = _rmsnorm_bwd(x1, w['norm2'][layer], d_hn2, dx2, f"norm2_bwd_{layer}")
    w_out = w[_W_OUT[layer]]
    d_og = _matmul(dx1, w_out, name=f"mixer_d_og_{layer}", tb=True, out_dtype=BF)
    dw_out = _matmul(og, dx1, name=f"mixer_dw_out_{layer}", ta=True, out_dtype=BF)
    grads[_W_OUT[layer]] = dw_out.reshape(N_CHIPS, dw_out.shape[0] // N_CHIPS, d)
    d_hn = _mixer_backward(layer, hn, w, tables, mixer_saved, d_og, grads)
    dx, grads['norm1'] = _rmsnorm_bwd(x, w['norm1'][layer], d_hn, dx1, f"norm1_bwd_{layer}")
    return dx, grads


def _pack(arrays):
    flat = []
    for a in arrays:
        v = a.astype(F32).reshape(-1)
        v = jnp.pad(v, (0, (-v.shape[0]) % 128))
        flat.append(v.reshape(-1, 128))
    buf = jnp.concatenate(flat, axis=0)
    return jnp.pad(buf, ((0, (-buf.shape[0]) % 8), (0, 0)))


def _unpack(buf, shapes):
    lead = buf.shape[:-2]
    out, off = [], 0
    for shp in shapes:
        n = math.prod(shp)
        rows = -(-n // 128)
        piece = buf[..., off:off + rows, :].reshape(lead + (rows * 128,))[..., :n]
        out.append(piece.reshape(lead + tuple(shp)))
        off += rows
    return out


_WEIGHTS = ('norm1', 'norm2', 'final_norm', 'ret_w_in', 'ret_gn_gain', 'ret_w_out', 'gdn_w_in', 'gdn_conv_w',
            'gdn_a_log', 'gdn_dt_bias', 'gdn_norm_gain', 'gdn_w_out', 'gla_w_in', 'gla_w_gate_up', 'gla_gate_bias',
            'gla_norm_gain', 'gla_w_out', 'lru_w_in', 'lru_conv_w', 'lru_conv_b', 'lru_w_rgate', 'lru_b_rgate',
            'lru_w_igate', 'lru_b_igate', 'lru_lambda', 'lru_w_out', 'mlp_w_up', 'mlp_w_down')
_FWD_PARAMS = ('x',) + _WEIGHTS
_BIG = ('ret_w_in', 'ret_w_out', 'gdn_w_in', 'gdn_w_out', 'gla_w_in', 'gla_w_out', 'lru_w_in', 'lru_w_out',
        'mlp_w_up', 'mlp_w_down')
_SMALL = tuple(n for n in _WEIGHTS if n not in _BIG)
_SMALL_SHARDED = ('ret_gn_gain', 'gdn_conv_w', 'gla_w_gate_up', 'gla_gate_bias', 'gla_norm_gain', 'lru_conv_w',
                  'lru_conv_b', 'lru_lambda')


def kernel(*args):
    names = _FWD_PARAMS + ('loss_target',) + tuple('m_' + n for n in _WEIGHTS) + tuple('v_' + n for n in _WEIGHTS)
    assert len(args) == len(names)
    a = dict(zip(names, args))
    x = a['x'][0]
    target = a['loss_target'][0]
    s, d = x.shape
    chip = 2 * lax.axis_index("x") + lax.axis_index("y")

    small_local = [a[n][0] if a[n].ndim == 3 else a[n] for n in _SMALL_SHARDED]
    small_pack = _pack(small_local)
    core_arr = lax.axis_index("c").astype(jnp.int32).reshape(1)
    chip_arr = chip.astype(jnp.int32).reshape(1)

    def whole_cols(g):
        return g.transpose(1, 0, 2).reshape(g.shape[1], N_CHIPS * g.shape[2])

    def whole_rows(g):
        return g.reshape(N_CHIPS * g.shape[1], g.shape[2])

    w = {'mlp_w_up': [None] * 4, 'mlp_w_down': [None] * 4}
    for n in _SMALL:
        if n not in _SMALL_SHARDED:
            w[n] = a[n][0] if n.startswith('lru_') else a[n]
    w['gdn_a_log'], w['gdn_dt_bias'] = _pad_cols(w['gdn_a_log']), _pad_cols(w['gdn_dt_bias'])
    w['lru_b_rgate'] = w['lru_b_rgate'].reshape(LRU_BLOCKS, 1, -1)
    w['lru_b_igate'] = w['lru_b_igate'].reshape(LRU_BLOCKS, 1, -1)

    def gather_start(layer, after):
        ops = [a[_W_IN[layer]][0].astype(BF), a[_W_OUT[layer]][0].astype(BF), a['mlp_w_up'][layer].astype(BF),
               a['mlp_w_down'][layer].astype(BF)] + ([small_pack] if layer == 0 else [])
        split = [True] * 4 + ([False] if layer == 0 else [])
        return _chip_exchange_start('gather', ops, [(N_CHIPS,) + o.shape for o in ops], split,
                                    f"gather_start_{layer}", after), split

    def gather_finish(layer, started, after):
        started, split = started
        own, lands = _chip_exchange_wait('gather', started, split, f"gather_wait_{layer}", after)
        lands = list(_pass_to_sibling(lands[:4], f"gather_pass_{layer}")) + list(lands[4:])
        g_in, g_out, g_up, g_down = [lax.dynamic_update_slice_in_dim(l, o[None], chip, axis=0)
                                     for l, o in zip(lands[:4], own[:4])]
        w['mlp_w_up'][layer], w['mlp_w_down'][layer], w[_W_OUT[layer]] = g_up, whole_rows(g_down), whole_rows(g_out)
        if layer == 0:
            small_all = lax.dynamic_update_slice_in_dim(lands[4], own[4][None], chip, axis=0)
            for n, piece in zip(_SMALL_SHARDED, _unpack(small_all, [p.shape for p in small_local])):
                w[n] = whole_cols(piece)
            w['gla_w_gate_up'] = jnp.pad(w['gla_w_gate_up'], ((0, 128 - GLA_RANK), (0, 0)))
        if layer in (0, 3):
            w[_W_IN[layer]] = g_in
        else:
            name, tail = ('gdn', 2 * GDN_HEADS) if layer == 1 else ('gla', GLA_RANK)
            full = whole_cols(g_in)
            w[name + '_w_main'] = full[:, :full.shape[1] - tail]
            w[name + '_w_small'] = _pad_cols(full[:, full.shape[1] - tail:])

    tables = _ret_tables(s, d // RET_HEADS)
    saved = []
    h = x
    gather_finish(0, gather_start(0, x), x)
    for layer in range(4):
        nxt = gather_start(layer + 1, w[_W_OUT[layer]]) if layer < 3 else None
        h, sv = _layer_forward(layer, h, w, tables, None if nxt is None else nxt[0][4])
        saved.append(sv)
        if nxt is not None:
            gather_finish(layer + 1, nxt, h)
    loss_part, dh, g_final = _loss_head(h, w['final_norm'], target)
    loss = lax.psum(loss_part, ("x", "y", "c"))

    grad, delta, new_m, new_v = {}, {}, {}, {}
    small_grads = {'final_norm': g_final}
    norm_grads = {'norm1': [None] * 4, 'norm2': [None] * 4}
    mlp_upd = {'mlp_w_up': None, 'mlp_w_down': None}

    def update_layer(layer, started, after):
        red = _reduce_scatter_finish(started, str(layer), chip_arr, after)
        for n, (mine, theirs) in ((_W_IN[layer], red[0]), (_W_OUT[layer], red[1])):
            grad[n], delta[n], new_m[n], new_v[n] = _adamw_halves(a[n], a['m_' + n], a['v_' + n], 0, mine, theirs,
                                                                  core_arr, f"adamw_{n}")
        for n, (mine, theirs) in (('mlp_w_up', red[2]), ('mlp_w_down', red[3])):
            mlp_upd[n] = _adamw_halves(a[n], a['m_' + n], a['v_' + n], layer, mine, theirs, core_arr,
                                       f"adamw_{n}_{layer}", carried=mlp_upd[n])

    pending = None
    for layer in reversed(range(4)):
        dh, g = _layer_backward(layer, dh, w, tables, saved[layer], None if pending is None else pending[1][4])
        saved[layer] = None
        started = _reduce_scatter_begin([g[_W_IN[layer]], g[_W_OUT[layer]], g['mlp_w_up'], g['mlp_w_down']],
                                        str(layer), core_arr)
        if pending is not None:
            update_layer(*pending, dh)
        pending = (layer, started)
        norm_grads['norm1'][layer], norm_grads['norm2'][layer] = g['norm1'], g['norm2']
        for n in _SMALL:
            if n in g:
                small_grads[n] = g[n]
    small_grads['norm1'] = jnp.stack(norm_grads['norm1'])
    small_grads['norm2'] = jnp.stack(norm_grads['norm2'])

    full_shapes = [small_grads[n].shape for n in _SMALL]
    total = _sum_slabs(_all_gather_devices(_pack([small_grads[n] for n in _SMALL]), "gather_small_grads"), F32,
                       "sum_small_grads")
    local_g = {}
    for n, full in zip(_SMALL, _unpack(total, full_shapes)):
        shp = a[n].shape
        if n in _SMALL_SHARDED:
            full = full.reshape(full.shape[0], -1)
            cq = shp[-1]
            full = lax.dynamic_slice_in_dim(full, chip * cq, cq, axis=1)
        elif n in ('gdn_a_log', 'gdn_dt_bias'):
            full = full[:, :shp[-1]]
        local_g[n] = full.reshape(shp)
    shapes = [a[n].shape for n in _SMALL]
    packed = [_pack([src[n] for n in _SMALL]) for src in
              (a, local_g, {n: a['m_' + n] for n in _SMALL}, {n: a['v_' + n] for n in _SMALL})]
    upd = _adamw(*packed, "adamw_small")
    for n, gr, dl, nm, nv in zip(_SMALL, _unpack(packed[1], shapes), *[_unpack(u, shapes) for u in upd]):
        grad[n], delta[n], new_m[n], new_v[n] = gr, dl, nm, nv
    update_layer(*pending, upd[0])
    for n in ('mlp_w_up', 'mlp_w_down'):
        grad[n], delta[n], new_m[n], new_v[n] = mlp_upd[n]

    out = [loss, dh.reshape(a['x'].shape)]
    for group in (grad, delta, new_m, new_v):
        out += [group[n].reshape(a[n].shape) for n in _WEIGHTS]
    return tuple(out)
```

```python
import functools
import math

import jax
import jax.numpy as jnp
from jax import lax
from jax.experimental import pallas as pl
from jax.experimental.pallas import tpu as pltpu

F32 = jnp.float32
BF = jnp.bfloat16
MESH = pl.DeviceIdType.MESH

NORM_EPS = 1e-6
CHUNK = 64
RET_HEADS = 8
GDN_HEADS = 16
GLA_HEADS = 4
GLA_RANK = 16
GLA_TAU = 16.0
LRU_BLOCKS = 16
LRU_C = 8.0
CONV_WIDTH = 4
ROPE_BASE = 10000.0
N_CHIPS = 4
N_DEV = 8

ADAM_LR = 0.001
ADAM_B1 = 0.9
ADAM_B2 = 0.999
ADAM_EPS = 1e-08
ADAM_WD = 0.01
ADAM_STEP = 10

VMEM_LIMIT_BYTES = 56 * 1024 * 1024
TOKEN_BLOCK = 256
ROW_BLOCK = 256
GDN_GROUP = 8


def _cparams(**kw):
    return pltpu.CompilerParams(vmem_limit_bytes=VMEM_LIMIT_BYTES, **kw)


def _bf16_parts(x, n):
    parts = []
    for _ in range(n - 1):
        p = x.astype(BF)
        parts.append(p)
        x = x - p.astype(F32)
    return parts + [x.astype(BF)]


def _raw_mm(a, b, ta, tb, hi):
    nb = a.ndim - 2
    batch = tuple(range(nb))
    dims = (((nb + (0 if ta else 1),), (nb + (1 if tb else 0),)), (batch, batch))

    def dot(p, q):
        return lax.dot_general(p, q, dims, preferred_element_type=F32)

    if not hi:
        return dot(a.astype(BF), b.astype(BF))
    if hi == 'l':
        ae = a.astype(BF)
        b0, b1, b2 = _bf16_parts(b.astype(F32), 3)
        return dot(ae, b0) + (dot(ae, b1) + dot(ae, b2))
    if hi == 'r':
        be = b.astype(BF)
        a0, a1, a2 = _bf16_parts(a.astype(F32), 3)
        return dot(a0, be) + (dot(a1, be) + dot(a2, be))
    a0, a1 = _bf16_parts(a.astype(F32), 2)
    b0, b1 = _bf16_parts(b.astype(F32), 2)
    return dot(a0, b0) + (dot(a0, b1) + dot(a1, b0))


@functools.partial(jax.custom_vjp, nondiff_argnums=(2, 3, 4))
def _mm_vjp(a, b, ta, tb, hi):
    return _raw_mm(a, b, ta, tb, hi)


def _mm_vjp_fwd(a, b, ta, tb, hi):
    return _raw_mm(a, b, ta, tb, hi), (a, b)


def _mm_vjp_bwd(ta, tb, hi, res, g):
    a, b = res
    if ta:
        da = _raw_mm(b, g, tb, True, 'l' if hi == 'r' else bool(hi))
    else:
        da = _raw_mm(g, b, False, not tb, 'r' if hi == 'r' else bool(hi))
    if tb:
        db = _raw_mm(g, a, True, ta, 'r' if hi == 'l' else bool(hi))
    else:
        db = _raw_mm(a, g, not ta, False, 'l' if hi == 'l' else bool(hi))
    return da, db


_mm_vjp.defvjp(_mm_vjp_fwd, _mm_vjp_bwd)


def _mm_diff(a, b, ta=False, tb=False, hi=False):
    return _mm_vjp(a, b, ta, tb, hi)


def _mm_plain(a, b, ta=False, tb=False, hi=False):
    return _raw_mm(a, b, ta, tb, hi)


def _shift_rows(x, k, up):
    if k == 0:
        return x
    n = x.shape[0]
    rows = lax.broadcasted_iota(jnp.int32, x.shape, 0)
    if up:
        return jnp.where(rows < n - k, pltpu.roll(x, n - k, 0), 0.0)
    return jnp.where(rows >= k, pltpu.roll(x, k, 0), 0.0)


@functools.partial(jax.custom_vjp, nondiff_argnums=(1, 2))
def _shift_vjp(x, k, up):
    return _shift_rows(x, k, up)


def _shift_vjp_fwd(x, k, up):
    return _shift_rows(x, k, up), None


def _shift_vjp_bwd(k, up, _, g):
    return (_shift_rows(g, k, not up),)


_shift_vjp.defvjp(_shift_vjp_fwd, _shift_vjp_bwd)


def _sigmoid(x):
    return 1.0 / (1.0 + jnp.exp(-x))


def _silu(x):
    return x * _sigmoid(x)


def _softplus(x):
    return jnp.maximum(x, 0.0) + jnp.log(1.0 + jnp.exp(-jnp.abs(x)))


def _gelu_tanh(x):
    return 0.5 * x * (1.0 + jnp.tanh(math.sqrt(2.0 / math.pi) * (x + 0.044715 * (x * x * x))))


def _expm1(x):
    series = x * (1.0 + x * (0.5 + x * (1.0 / 6.0 + x * (1.0 / 24.0))))
    return jnp.where(jnp.abs(x) < 0.03, series, jnp.exp(x) - 1.0)


def _rmsnorm(x, g):
    return x * lax.rsqrt(jnp.mean(x * x, axis=-1, keepdims=True) + NORM_EPS) * g


def _head_norm(o, gain, center):
    if center:
        o = o - jnp.mean(o, axis=-1, keepdims=True)
    return o * lax.rsqrt(jnp.mean(o * o, axis=-1, keepdims=True) + NORM_EPS) * gain


def _l2norm(x):
    return x * lax.rsqrt(jnp.sum(x * x, axis=-1, keepdims=True) + NORM_EPS)


def _iota2(shape, dim):
    return lax.broadcasted_iota(jnp.int32, shape, dim)


def _tri_ones(n, upper=False):
    i, j = _iota2((n, n), 0), _iota2((n, n), 1)
    return jnp.where((i <= j) if upper else (j <= i), 1.0, 0.0).astype(F32)


def _linscan(a, u, rev):
    n = a.shape[0]
    rows = _iota2(a.shape, 0)
    d = 1
    while d < n:
        if rev:
            valid = rows < n - d
            a_s, u_s = pltpu.roll(a, n - d, 0), pltpu.roll(u, n - d, 0)
        else:
            valid = rows >= d
            a_s, u_s = pltpu.roll(a, d, 0), pltpu.roll(u, d, 0)
        u = a * jnp.where(valid, u_s, 0.0) + u
        a = a * jnp.where(valid, a_s, 1.0)
        d *= 2
    return u


def _ret_chunk(mm, q, k, v, g, gain, state, cos, sin, dintra, qdec, kdec, cdec):
    dk = q.shape[-1]
    half = dk // 2

    def rot(t):
        t1, t2 = t[:, :half], t[:, half:]
        return jnp.concatenate([t1 * cos - t2 * sin, t1 * sin + t2 * cos], axis=-1)

    qr = rot(q)
    kr = rot(k) * (dk ** -0.5)
    scores = mm(qr, kr, tb=True) * dintra
    o = mm(scores, v) + mm(qr * qdec, state)
    new_state = state * cdec + mm(kr * kdec, v, ta=True)
    y = _head_norm(o, gain, True) * _silu(g)
    return y, new_state


def _gla_chunk(mm, q, k, v, r, glow, wgu, bias, gain, state_t):
    dk = q.shape[-1]
    c = q.shape[0]
    logit = mm(glow, wgu) + bias
    la = -_softplus(-logit) * (1.0 / GLA_TAU)
    cum = mm(_tri_ones(c), la, hi='l')
    rows = _iota2(la.shape, 0)
    ref = jnp.sum(jnp.where(rows < c // 2, la, 0.0), axis=0, keepdims=True)
    tot = jnp.sum(la, axis=0, keepdims=True)
    fwd, bwd = jnp.exp(cum - ref), jnp.exp(ref - cum)
    qs = q * (dk ** -0.5)
    s_lo = mm(qs * fwd, k * bwd, tb=True)
    s_up = mm(qs * bwd, k * fwd, tb=True)
    i, j = _iota2((c, c), 0), _iota2((c, c), 1)
    scores = jnp.where(i >= j, s_lo, s_up)
    o = mm(scores, v) + mm(qs * jnp.exp(cum), state_t, tb=True)
    k_end = k * jnp.exp(tot - cum)
    new_state_t = state_t * jnp.exp(tot) + mm(v, k_end, ta=True)
    y = _head_norm(o, gain, False) * _silu(r)
    return y, new_state_t


def _gdn_group(mm, q, k, v, z, ba, a_log, dt_bias, gain, state, head):
    g, c, dk = q.shape
    lanes = lax.broadcasted_iota(jnp.int32, (1, 1, ba.shape[-1]), 2)
    oh_b = jnp.where(lanes == head, 1.0, 0.0).astype(F32)
    oh_a = jnp.where(lanes == head + GDN_HEADS, 1.0, 0.0).astype(F32)
    beta = _sigmoid(jnp.sum(ba * oh_b, axis=-1, keepdims=True))
    a_logit = jnp.sum(ba * oh_a, axis=-1, keepdims=True)
    a_h = jnp.sum(a_log * oh_b[0], axis=-1, keepdims=True)
    dt_h = jnp.sum(dt_bias * oh_b[0], axis=-1, keepdims=True)
    la = -jnp.exp(a_h) * _softplus(a_logit + dt_h)
    q = _l2norm(q) * (dk ** -0.5)
    k = _l2norm(k)
    tri = jnp.broadcast_to(_tri_ones(c), (g, c, c))
    tri_up = jnp.broadcast_to(_tri_ones(c, upper=True), (g, c, c))
    cum_k = mm(tri, la * jnp.ones((g, c, dk), F32), hi='l')
    la_sq = la * jnp.ones((g, c, c), F32)
    cum_i = mm(tri, la_sq, hi='l')
    cum_j = mm(la_sq, tri_up, ta=True, hi='r')
    i, j = _iota2((c, c), 0), _iota2((c, c), 1)
    strict = i > j
    rel = jnp.where(strict, jnp.exp(jnp.where(strict, cum_i - cum_j, 0.0)), 0.0)
    a_mat = beta * rel * mm(k, k, tb=True)
    inv = jnp.where(i == j, 1.0, 0.0).astype(F32) - a_mat
    power = a_mat
    for _ in range(int(math.log2(c)) - 1):
        power = mm(power, power, hi=True)
        inv = inv + mm(inv, power, hi=True)
    tot = jnp.sum(la, axis=1, keepdims=True)
    u = mm(inv, beta * v, hi=True)
    w = mm(inv, (beta * jnp.exp(cum_k)) * k, hi=True)
    k_end = k * jnp.exp(tot - cum_k)
    di, dj = _iota2((dk, dk), 0), _iota2((dk, dk), 1)
    trans = jnp.exp(tot) * jnp.where(di == dj, 1.0, 0.0).astype(F32) - mm(k_end, w, ta=True)
    inject = mm(k_end, u, ta=True)
    ys = []
    for n in range(g):
        state = mm(trans[n], state) + inject[n]
        ys.append(_head_norm(mm(q[n], state), gain, False) * _silu(z[n]))
    return ys, state


def _conv_taps(shift, x, taps):
    out = None
    for tap, w in enumerate(taps):
        term = shift(x, CONV_WIDTH - 1 - tap, False) * w
        out = term if out is None else out + term
    return out


def _lru_pre(mm, shift, xb, yb, taps, cb, lam, wr, br, wi, bi):
    xb = _conv_taps(shift, xb, taps) + cb
    r = _sigmoid(mm(xb, wr) + br)
    i = _sigmoid(mm(xb, wi) + bi)
    log_a = (-LRU_C) * _softplus(-lam) * r
    a = jnp.exp(log_a)
    u = jnp.sqrt(-_expm1(2.0 * log_a)) * (i * xb)
    return a, u, _gelu_tanh(yb)


_TOKEN_SPEC = pl.BlockSpec((8, 128), lambda *_: (0, 0))


MATMUL_VMEM_BUDGET = 40 * 1024 * 1024
_TILE_SIZES = (3072, 2048, 1536, 1024, 768, 512, 384, 256, 128)


def _tile_choices(extent):
    return [t for t in _TILE_SIZES if t <= extent and extent % t == 0] or [extent]


def _matmul_tiles(m, n, kdim, n_unit, k_unit, a_item, b_item, o_item, e_item):
    best = None
    for bm in _tile_choices(m):
        for bn in _tile_choices(n_unit):
            for bk in _tile_choices(k_unit):
                nk = kdim // bk
                vmem = 2 * (bm * bk * a_item + bk * bn * b_item + bm * bn * (o_item + e_item)) + bm * bn * 4 * (2 if nk > 1 else 1)
                if vmem > MATMUL_VMEM_BUDGET:
                    continue
                steps = (m // bm) * (n // bn) * nk
                traffic = m * kdim * a_item * (1 if nk == 1 else n // bn) + kdim * n * b_item * (m // bm)
                if best is None or (steps, traffic) < best[0]:
                    best = ((steps, traffic), (bm, bn, bk))
    return best[1]


def _matmul(a, b, *, name, ta=False, tb=False, out_dtype=F32, b_sharded=False, o_sharded=False,
            epilogue=None, extra=None, token=None):
    m, kdim = (a.shape[1], a.shape[0]) if ta else a.shape
    if b_sharded:
        nq = b.shape[2]
        n = b.shape[1] if tb else N_CHIPS * nq
        assert kdim == (N_CHIPS * nq if tb else b.shape[1])
    else:
        n = b.shape[0] if tb else b.shape[1]
        assert kdim == (b.shape[1] if tb else b.shape[0])
    n_unit = b.shape[2] if (b_sharded and not tb) else (n // N_CHIPS if o_sharded else n)
    k_unit = b.shape[2] if (b_sharded and tb) else kdim
    bm, bn, bk = _matmul_tiles(m, n, kdim, n_unit, k_unit, a.dtype.itemsize, b.dtype.itemsize,
                               jnp.dtype(out_dtype).itemsize, 0 if extra is None else extra.dtype.itemsize)
    nk = kdim // bk
    grid = (m // bm, n // bn, nk)

    a_spec = pl.BlockSpec((bk, bm), lambda i, j, k: (k, i)) if ta else pl.BlockSpec((bm, bk), lambda i, j, k: (i, k))
    if b_sharded and not tb:
        per = b.shape[2] // bn
        b_spec = pl.BlockSpec((None, bk, bn), lambda i, j, k: (j // per, k, j % per))
    elif b_sharded:
        per = b.shape[2] // bk
        b_spec = pl.BlockSpec((None, bn, bk), lambda i, j, k: (k // per, j, k % per))
    elif tb:
        b_spec = pl.BlockSpec((bn, bk), lambda i, j, k: (j, k))
    else:
        b_spec = pl.BlockSpec((bk, bn), lambda i, j, k: (k, j))
    if o_sharded:
        per_o = (n // N_CHIPS) // bn
        o_spec = pl.BlockSpec((None, bm, bn), lambda i, j, k: (j // per_o, i, j % per_o))
        out_shape = jax.ShapeDtypeStruct((N_CHIPS, m, n // N_CHIPS), out_dtype)
    else:
        o_spec = pl.BlockSpec((bm, bn), lambda i, j, k: (i, j))
        out_shape = jax.ShapeDtypeStruct((m, n), out_dtype)
    in_specs = [a_spec, b_spec]
    operands = [a, b]
    if extra is not None:
        in_specs.append(pl.BlockSpec((bm, bn), lambda i, j, k: (i, j)))
        operands.append(extra)
    if token is not None:
        in_specs.append(_TOKEN_SPEC)
        operands.append(token)

    def finish(acc, e_ref, o_ref):
        if epilogue == 'add':
            acc = acc + e_ref[...].astype(F32)
        elif epilogue == 'relu2':
            acc = jnp.square(jnp.maximum(acc, 0.0))
        elif epilogue == 'dact':
            acc = acc * (2.0 * jnp.sqrt(e_ref[...].astype(F32)))
        o_ref[...] = acc.astype(o_ref.dtype)

    def body_one_step(*refs):
        finish(_raw_mm(refs[0][...], refs[1][...], ta, tb, False), refs[2] if extra is not None else None, refs[-1])

    def body(*refs):
        a_ref, b_ref = refs[0], refs[1]
        e_ref = refs[2] if extra is not None else None
        o_ref, acc_ref = refs[-2], refs[-1]
        k = pl.program_id(2)

        @pl.when(k == 0)
        def _():
            acc_ref[...] = jnp.zeros_like(acc_ref)

        acc_ref[...] += _raw_mm(a_ref[...], b_ref[...], ta, tb, False)

        @pl.when(k == nk - 1)
        def _():
            finish(acc_ref[...], e_ref, o_ref)

    return pl.pallas_call(
        body_one_step if nk == 1 else body, name=name, grid=grid, in_specs=in_specs, out_specs=o_spec,
        out_shape=out_shape, scratch_shapes=[] if nk == 1 else [pltpu.VMEM((bm, bn), F32)],
        compiler_params=_cparams(),
    )(*operands)


def _rmsnorm_fwd(x, g, name, token=None):
    s, d = x.shape

    def body(x_ref, g_ref, *rest):
        rest[-1][...] = _rmsnorm(x_ref[...], g_ref[...]).astype(BF)

    return pl.pallas_call(
        body, name=name, grid=(s // ROW_BLOCK,),
        in_specs=[pl.BlockSpec((ROW_BLOCK, d), lambda i: (i, 0)), pl.BlockSpec((1, d), lambda i: (0, 0))]
        + ([] if token is None else [_TOKEN_SPEC]),
        out_specs=pl.BlockSpec((ROW_BLOCK, d), lambda i: (i, 0)),
        out_shape=jax.ShapeDtypeStruct((s, d), BF), compiler_params=_cparams(),
    )(x, g.reshape(1, d), *([] if token is None else [token]))


def _rmsnorm_bwd(x, g, dh, dres, name):
    s, d = x.shape

    def body(x_ref, g_ref, dh_ref, dres_ref, dx_ref, dg_ref):
        _, vjp = jax.vjp(_rmsnorm, x_ref[...], g_ref[...])
        dx, dg = vjp(dh_ref[...].astype(F32))
        dx_ref[...] = dres_ref[...] + dx

        @pl.when(pl.program_id(0) == 0)
        def _():
            dg_ref[...] = jnp.zeros_like(dg_ref)

        dg_ref[...] += dg

    row = pl.BlockSpec((ROW_BLOCK, d), lambda i: (i, 0))
    vec = pl.BlockSpec((1, d), lambda i: (0, 0))
    dx, dg = pl.pallas_call(
        body, name=name, grid=(s // ROW_BLOCK,), in_specs=[row, vec, row, row], out_specs=[row, vec],
        out_shape=[jax.ShapeDtypeStruct((s, d), F32), jax.ShapeDtypeStruct((1, d), F32)],
        compiler_params=_cparams(),
    )(x, g.reshape(1, d), dh, dres)
    return dx, dg.reshape(d)


def _loss_head(x, g, target):
    s, d = x.shape

    def loss_fn(xv, gv, tv):
        err = _rmsnorm(xv, gv) - tv
        return 0.5 * jnp.sum(jnp.mean(err * err, axis=-1, keepdims=True), axis=0, keepdims=True)

    def body(x_ref, g_ref, t_ref, dx_ref, dg_ref, loss_ref):
        tv = t_ref[...]
        loss, vjp = jax.vjp(lambda xv, gv: loss_fn(xv, gv, tv), x_ref[...], g_ref[...])
        dx, dg = vjp(jnp.ones((1, 1), F32))
        dx_ref[...] = dx

        @pl.when(pl.program_id(0) == 0)
        def _():
            dg_ref[...] = jnp.zeros_like(dg_ref)
            loss_ref[...] = jnp.zeros_like(loss_ref)

        dg_ref[...] += dg
        loss_ref[...] += loss * jnp.ones_like(loss_ref)

    row = pl.BlockSpec((ROW_BLOCK, d), lambda i: (i, 0))
    vec = pl.BlockSpec((1, d), lambda i: (0, 0))
    dx, dg, loss = pl.pallas_call(
        body, name="loss_head", grid=(s // ROW_BLOCK,), in_specs=[row, vec, row],
        out_specs=[row, vec, pl.BlockSpec((1, 128), lambda i: (0, 0))],
        out_shape=[jax.ShapeDtypeStruct((s, d), F32), jax.ShapeDtypeStruct((1, d), F32),
                   jax.ShapeDtypeStruct((1, 128), F32)],
        compiler_params=_cparams(),
    )(x, g.reshape(1, d), target)
    return loss[0, 0], dx, dg.reshape(d)


def _ret_tables(s, dk):
    h = jnp.arange(RET_HEADS, dtype=F32)
    log_gamma = jnp.log1p(-jnp.exp2(-5.0 - h))
    pos = jnp.arange(CHUNK, dtype=F32)
    dist = jnp.abs(pos[:, None] - pos[None, :])
    dintra = jnp.exp(log_gamma[:, None, None] * dist)
    qdec = jnp.exp(log_gamma[:, None] * (pos + 1.0))[:, :, None]
    kdec = jnp.exp(log_gamma[:, None] * (CHUNK - 1.0 - pos))[:, :, None]
    cdec = jnp.exp(log_gamma * CHUNK)[:, None, None]
    inv = ROPE_BASE ** (-jnp.arange(0, dk, 2, dtype=F32) / dk)
    ang = jnp.arange(s, dtype=F32)[:, None] * inv[None, :]
    return jnp.cos(ang), jnp.sin(ang), dintra, qdec, kdec, cdec


def _ret_specs(s, dk, dv, tb, rev):
    nh = RET_HEADS
    nb = s // tb
    bi = (lambda b: nb - 1 - b) if rev else (lambda b: b)
    voff = 2 * nh * dk // dv
    cpb = tb // CHUNK
    return dict(
        q=pl.BlockSpec((tb, dk), lambda h, b: (bi(b), h)),
        k=pl.BlockSpec((tb, dk), lambda h, b: (bi(b), nh + h)),
        v=pl.BlockSpec((tb, dv), lambda h, b: (bi(b), voff + h)),
        g=pl.BlockSpec((tb, dv), lambda h, b: (bi(b), voff + nh + h)),
        gain=pl.BlockSpec((None, 1, dv), lambda h, b: (h, 0, 0)),
        cs=pl.BlockSpec((tb, dk // 2), lambda h, b: (bi(b), 0)),
        dintra=pl.BlockSpec((None, CHUNK, CHUNK), lambda h, b: (h, 0, 0)),
        dec=pl.BlockSpec((None, CHUNK, 1), lambda h, b: (h, 0, 0)),
        cdec=pl.BlockSpec((None, 1, 1), lambda h, b: (h, 0, 0)),
        hv=pl.BlockSpec((tb, dv), lambda h, b: (bi(b), h)),
        hk=pl.BlockSpec((tb, dk), lambda h, b: (bi(b), h)),
        st=pl.BlockSpec((None, cpb, dk, dv), lambda h, b: (h, bi(b), 0, 0)),
    )


def _ret_fwd(proj, gain, tables):
    s = proj.shape[0]
    d = proj.shape[1] // 6
    dk, dv = d // RET_HEADS, 2 * d // RET_HEADS
    tb = min(TOKEN_BLOCK, s)
    cpb = tb // CHUNK
    sp = _ret_specs(s, dk, dv, tb, False)
    cos, sin, dintra, qdec, kdec, cdec = tables

    def body(q_ref, k_ref, v_ref, g_ref, gain_ref, cos_ref, sin_ref, di_ref, qd_ref, kd_ref, cd_ref,
             y_ref, st_ref, state):
        @pl.when(pl.program_id(1) == 0)
        def _():
            state[...] = jnp.zeros_like(state)

        for c in range(cpb):
            sl = pl.ds(c * CHUNK, CHUNK)
            st_ref[c] = state[...]
            y, new_state = _ret_chunk(_mm_plain, q_ref[sl, :], k_ref[sl, :], v_ref[sl, :], g_ref[sl, :],
                                      gain_ref[...], state[...], cos_ref[sl, :], sin_ref[sl, :],
                                      di_ref[...], qd_ref[...], kd_ref[...], cd_ref[...])
            y_ref[sl, :] = y.astype(BF)
            state[...] = new_state

    return pl.pallas_call(
        body, name="ret_fwd", grid=(RET_HEADS, s // tb),
        in_specs=[sp['q'], sp['k'], sp['v'], sp['g'], sp['gain'], sp['cs'], sp['cs'], sp['dintra'], sp['dec'],
                  sp['dec'], sp['cdec']],
        out_specs=[sp['hv'], sp['st']],
        out_shape=[jax.ShapeDtypeStruct((s, RET_HEADS * dv), BF),
                   jax.ShapeDtypeStruct((RET_HEADS, s // CHUNK, dk, dv), F32)],
        scratch_shapes=[pltpu.VMEM((dk, dv), F32)], compiler_params=_cparams(),
    )(proj, proj, proj, proj, gain.reshape(RET_HEADS, 1, dv), cos, sin, dintra, qdec, kdec, cdec)


def _ret_bwd(proj, gain, tables, states, dy):
    s = proj.shape[0]
    d = proj.shape[1] // 6
    dk, dv = d // RET_HEADS, 2 * d // RET_HEADS
    tb = min(TOKEN_BLOCK, s)
    cpb = tb // CHUNK
    sp = _ret_specs(s, dk, dv, tb, True)
    cos, sin, dintra, qdec, kdec, cdec = tables

    def body(q_ref, k_ref, v_ref, g_ref, gain_ref, cos_ref, sin_ref, di_ref, qd_ref, kd_ref, cd_ref, st_ref,
             dy_ref, dq_ref, dk_ref, dv_ref, dg_ref, dgain_ref, dstate):
        @pl.when(pl.program_id(1) == 0)
        def _():
            dstate[...] = jnp.zeros_like(dstate)
            dgain_ref[...] = jnp.zeros_like(dgain_ref)

        for c in reversed(range(cpb)):
            sl = pl.ds(c * CHUNK, CHUNK)
            cos_c, sin_c = cos_ref[sl, :], sin_ref[sl, :]
            di, qd, kd, cd = di_ref[...], qd_ref[...], kd_ref[...], cd_ref[...]

            def fn(q, k, v, g, gn, st):
                return _ret_chunk(_mm_diff, q, k, v, g, gn, st, cos_c, sin_c, di, qd, kd, cd)

            _, vjp = jax.vjp(fn, q_ref[sl, :], k_ref[sl, :], v_ref[sl, :], g_ref[sl, :], gain_ref[...],
                             st_ref[c])
            dq, dkk, dvv, dg, dgn, dst = vjp((dy_ref[sl, :].astype(F32), dstate[...]))
            dq_ref[sl, :] = dq.astype(BF)
            dk_ref[sl, :] = dkk.astype(BF)
            dv_ref[sl, :] = dvv.astype(BF)
            dg_ref[sl, :] = dg.astype(BF)
            dgain_ref[...] += dgn
            dstate[...] = dst

    dq, dkk, dvv, dg, dgain = pl.pallas_call(
        body, name="ret_bwd", grid=(RET_HEADS, s // tb),
        in_specs=[sp['q'], sp['k'], sp['v'], sp['g'], sp['gain'], sp['cs'], sp['cs'], sp['dintra'], sp['dec'],
                  sp['dec'], sp['cdec'], sp['st'], sp['hv']],
        out_specs=[sp['hk'], sp['hk'], sp['hv'], sp['hv'], sp['gain']],
        out_shape=[jax.ShapeDtypeStruct((s, RET_HEADS * dk), BF), jax.ShapeDtypeStruct((s, RET_HEADS * dk), BF),
                   jax.ShapeDtypeStruct((s, RET_HEADS * dv), BF), jax.ShapeDtypeStruct((s, RET_HEADS * dv), BF),
                   jax.ShapeDtypeStruct((RET_HEADS, 1, dv), F32)],
        scratch_shapes=[pltpu.VMEM((dk, dv), F32)], compiler_params=_cparams(),
    )(proj, proj, proj, proj, gain.reshape(RET_HEADS, 1, dv), cos, sin, dintra, qdec, kdec, cdec, states, dy)
    return jnp.concatenate([dq, dkk, dvv, dg], axis=1), dgain.reshape(RET_HEADS, dv)


def _gla_specs(s, dk, dv, tb, rev):
    nh = GLA_HEADS
    nb = s // tb
    bi = (lambda b: nb - 1 - b) if rev else (lambda b: b)
    voff = 2 * nh * dk // dv
    cpb = tb // CHUNK
    return dict(
        q=pl.BlockSpec((tb, dk), lambda h, b: (bi(b), h)),
        k=pl.BlockSpec((tb, dk), lambda h, b: (bi(b), nh + h)),
        v=pl.BlockSpec((tb, dv), lambda h, b: (bi(b), voff + h)),
        r=pl.BlockSpec((tb, dv), lambda h, b: (bi(b), voff + nh + h)),
        glow=pl.BlockSpec((tb, 128), lambda h, b: (bi(b), 0)),
        wgu=pl.BlockSpec((128, dk), lambda h, b: (0, h)),
        bias=pl.BlockSpec((1, dk), lambda h, b: (0, h)),
        gain=pl.BlockSpec((None, 1, dv), lambda h, b: (h, 0, 0)),
        hv=pl.BlockSpec((tb, dv), lambda h, b: (bi(b), h)),
        hk=pl.BlockSpec((tb, dk), lambda h, b: (bi(b), h)),
        hg=pl.BlockSpec((tb, 128), lambda h, b: (bi(b), h)),
        st=pl.BlockSpec((None, cpb, dv, dk), lambda h, b: (h, bi(b), 0, 0)),
    )


def _gla_fwd(proj, glow, wgu, bias, gain):
    s = proj.shape[0]
    d = proj.shape[1] // 3
    dk, dv = d // 2 // GLA_HEADS, d // GLA_HEADS
    tb = min(TOKEN_BLOCK, s)
    cpb = tb // CHUNK
    sp = _gla_specs(s, dk, dv, tb, False)

    def body(q_ref, k_ref, v_ref, r_ref, gl_ref, wgu_ref, b_ref, gain_ref, y_ref, st_ref, state):
        @pl.when(pl.program_id(1) == 0)
        def _():
            state[...] = jnp.zeros_like(state)

        for c in range(cpb):
            sl = pl.ds(c * CHUNK, CHUNK)
            st_ref[c] = state[...]
            y, new_state = _gla_chunk(_mm_plain, q_ref[sl, :], k_ref[sl, :], v_ref[sl, :], r_ref[sl, :],
                                      gl_ref[sl, :], wgu_ref[...], b_ref[...], gain_ref[...], state[...])
            y_ref[sl, :] = y.astype(BF)
            state[...] = new_state

    return pl.pallas_call(
        body, name="gla_fwd", grid=(GLA_HEADS, s // tb),
        in_specs=[sp['q'], sp['k'], sp['v'], sp['r'], sp['glow'], sp['wgu'], sp['bias'], sp['gain']],
        out_specs=[sp['hv'], sp['st']],
        out_shape=[jax.ShapeDtypeStruct((s, GLA_HEADS * dv), BF),
                   jax.ShapeDtypeStruct((GLA_HEADS, s // CHUNK, dv, dk), F32)],
        scratch_shapes=[pltpu.VMEM((dv, dk), F32)], compiler_params=_cparams(),
    )(proj, proj, proj, proj, glow, wgu, bias, gain.reshape(GLA_HEADS, 1, dv))


def _gla_bwd(proj, glow, wgu, bias, gain, states, dy):
    s = proj.shape[0]
    d = proj.shape[1] // 3
    dk, dv = d // 2 // GLA_HEADS, d // GLA_HEADS
    tb = min(TOKEN_BLOCK, s)
    cpb = tb // CHUNK
    sp = _gla_specs(s, dk, dv, tb, True)

    def body(q_ref, k_ref, v_ref, r_ref, gl_ref, wgu_ref, b_ref, gain_ref, st_ref, dy_ref,
             dq_ref, dk_ref, dv_ref, dr_ref, dgl_ref, dwgu_ref, db_ref, dgain_ref, dstate):
        @pl.when(pl.program_id(1) == 0)
        def _():
            dstate[...] = jnp.zeros_like(dstate)
            dwgu_ref[...] = jnp.zeros_like(dwgu_ref)
            db_ref[...] = jnp.zeros_like(db_ref)
            dgain_ref[...] = jnp.zeros_like(dgain_ref)

        for c in reversed(range(cpb)):
            sl = pl.ds(c * CHUNK, CHUNK)

            def fn(q, k, v, r, gl, w, b, gn, st):
                return _gla_chunk(_mm_diff, q, k, v, r, gl, w, b, gn, st)

            _, vjp = jax.vjp(fn, q_ref[sl, :], k_ref[sl, :], v_ref[sl, :], r_ref[sl, :], gl_ref[sl, :],
                             wgu_ref[...], b_ref[...], gain_ref[...], st_ref[c])
            dq, dkk, dvv, dr, dgl, dw, db, dgn, dst = vjp((dy_ref[sl, :].astype(F32), dstate[...]))
            dq_ref[sl, :] = dq.astype(BF)
            dk_ref[sl, :] = dkk.astype(BF)
            dv_ref[sl, :] = dvv.astype(BF)
            dr_ref[sl, :] = dr.astype(BF)
            dgl_ref[sl, :] = dgl.astype(BF)
            dwgu_ref[...] += dw
            db_ref[...] += db
            dgain_ref[...] += dgn
            dstate[...] = dst

    nh = GLA_HEADS
    dq, dkk, dvv, dr, dgl, dwgu, db, dgain = pl.pallas_call(
        body, name="gla_bwd", grid=(nh, s // tb),
        in_specs=[sp['q'], sp['k'], sp['v'], sp['r'], sp['glow'], sp['wgu'], sp['bias'], sp['gain'], sp['st'],
                  sp['hv']],
        out_specs=[sp['hk'], sp['hk'], sp['hv'], sp['hv'], sp['hg'], sp['wgu'], sp['bias'], sp['gain']],
        out_shape=[jax.ShapeDtypeStruct((s, nh * dk), BF), jax.ShapeDtypeStruct((s, nh * dk), BF),
                   jax.ShapeDtypeStruct((s, nh * dv), BF), jax.ShapeDtypeStruct((s, nh * dv), BF),
                   jax.ShapeDtypeStruct((s, nh * 128), BF), jax.ShapeDtypeStruct((128, nh * dk), F32),
                   jax.ShapeDtypeStruct((1, nh * dk), F32), jax.ShapeDtypeStruct((nh, 1, dv), F32)],
        scratch_shapes=[pltpu.VMEM((dv, dk), F32)], compiler_params=_cparams(),
    )(proj, proj, proj, proj, glow, wgu, bias, gain.reshape(nh, 1, dv), states, dy)
    return jnp.concatenate([dq, dkk, dvv, dr], axis=1), dgl, dwgu, db, dgain.reshape(nh, dv)


def _gdn_specs(s, dk):
    nh = GDN_HEADS
    col = lambda off: pl.BlockSpec((s, dk), lambda h: (0, off + h))
    tap = lambda off: pl.BlockSpec((CONV_WIDTH, dk), lambda h: (0, off + h))
    vec = pl.BlockSpec((1, 128), lambda h: (0, 0))
    return dict(q=col(0), k=col(nh), v=col(2 * nh), z=col(3 * nh), ba=pl.BlockSpec((s, 128), lambda h: (0, 0)),
                cq=tap(0), ck=tap(nh), cv=tap(2 * nh), vec=vec, gain=pl.BlockSpec((1, dk), lambda h: (0, 0)),
                head=col(0))


def _conv_silu(shift, x, taps):
    return _silu(_conv_taps(shift, x, taps))


def _load_taps(ref):
    return [ref[t:t + 1, :] for t in range(CONV_WIDTH)]


def _gdn_fwd(proj, ba, conv_w, a_log, dt_bias, gain):
    s = proj.shape[0]
    dk = proj.shape[1] // (4 * GDN_HEADS)
    sp = _gdn_specs(s, dk)
    rows = GDN_GROUP * CHUNK

    def grp(t):
        return t.reshape(GDN_GROUP, CHUNK, t.shape[-1])

    ngroups = s // rows

    def body(q_ref, k_ref, v_ref, z_ref, ba_ref, cq_ref, ck_ref, cv_ref, al_ref, dt_ref, gain_ref, y_ref, states,
             qc, kc, vc, state):
        head = pl.program_id(0)
        qc[...] = _conv_silu(_shift_rows, q_ref[...], _load_taps(cq_ref))
        kc[...] = _conv_silu(_shift_rows, k_ref[...], _load_taps(ck_ref))
        vc[...] = _conv_silu(_shift_rows, v_ref[...], _load_taps(cv_ref))
        state[...] = jnp.zeros_like(state)

        def step(n, carry):
            base = pl.multiple_of(n * rows, rows)
            sl = pl.ds(base, rows)
            states[n] = state[...]
            ys, new_state = _gdn_group(_mm_plain, grp(qc[sl, :]), grp(kc[sl, :]), grp(vc[sl, :]), grp(z_ref[sl, :]),
                                       grp(ba_ref[sl, :]), al_ref[...], dt_ref[...], gain_ref[...], state[...], head)
            for i, y in enumerate(ys):
                y_ref[pl.ds(base + i * CHUNK, CHUNK), :] = y.astype(BF)
            state[...] = new_state
            return carry

        lax.fori_loop(0, ngroups, step, 0)

    return pl.pallas_call(
        body, name="gdn_fwd", grid=(GDN_HEADS,),
        in_specs=[sp['q'], sp['k'], sp['v'], sp['z'], sp['ba'], sp['cq'], sp['ck'], sp['cv'], sp['vec'], sp['vec'],
                  sp['gain']],
        out_specs=[sp['head'], pl.BlockSpec((None, ngroups, dk, dk), lambda h: (h, 0, 0, 0))],
        out_shape=[jax.ShapeDtypeStruct((s, GDN_HEADS * dk), BF),
                   jax.ShapeDtypeStruct((GDN_HEADS, ngroups, dk, dk), F32)],
        scratch_shapes=[pltpu.VMEM((s, dk), F32)] * 3 + [pltpu.VMEM((dk, dk), F32)],
        compiler_params=_cparams(),
    )(proj, proj, proj, proj, ba, conv_w, conv_w, conv_w, a_log, dt_bias, gain)


def _gdn_bwd(proj, ba, conv_w, a_log, dt_bias, gain, group_states, dy):
    s = proj.shape[0]
    dk = proj.shape[1] // (4 * GDN_HEADS)
    sp = _gdn_specs(s, dk)
    rows = GDN_GROUP * CHUNK
    ngroups = s // rows

    def grp(t):
        return t.reshape(GDN_GROUP, CHUNK, t.shape[-1])

    def body(q_ref, k_ref, v_ref, z_ref, ba_ref, cq_ref, ck_ref, cv_ref, al_ref, dt_ref, gain_ref, states, dy_ref,
             dq_ref, dk_ref, dv_ref, dz_ref, dba_ref, dcq_ref, dck_ref, dcv_ref, dal_ref, ddt_ref, dgain_ref,
             qc, kc, vc, dqc, dkc, dvc, dstate):
        head = pl.program_id(0)

        @pl.when(head == 0)
        def _():
            dba_ref[...] = jnp.zeros_like(dba_ref)
            dal_ref[...] = jnp.zeros_like(dal_ref)
            ddt_ref[...] = jnp.zeros_like(ddt_ref)
            dgain_ref[...] = jnp.zeros_like(dgain_ref)

        qc[...] = _conv_silu(_shift_rows, q_ref[...], _load_taps(cq_ref))
        kc[...] = _conv_silu(_shift_rows, k_ref[...], _load_taps(ck_ref))
        vc[...] = _conv_silu(_shift_rows, v_ref[...], _load_taps(cv_ref))
        dstate[...] = jnp.zeros_like(dstate)

        def bstep(i, carry):
            n = ngroups - 1 - i
            base = pl.multiple_of(n * rows, rows)
            sl = pl.ds(base, rows)

            def fn(q, k, v, z, b, al, dt, gn, st):
                return _gdn_group(_mm_diff, q, k, v, z, b, al, dt, gn, st, head)

            _, vjp = jax.vjp(fn, grp(qc[sl, :]), grp(kc[sl, :]), grp(vc[sl, :]), grp(z_ref[sl, :]),
                             grp(ba_ref[sl, :]), al_ref[...], dt_ref[...], gain_ref[...], states[n])
            dys = [dy_ref[pl.ds(base + j * CHUNK, CHUNK), :].astype(F32) for j in range(GDN_GROUP)]
            dq, dkk, dvv, dz, db, dal, ddt, dgn, dst = vjp((dys, dstate[...]))
            dqc[sl, :] = dq.reshape(rows, dk)
            dkc[sl, :] = dkk.reshape(rows, dk)
            dvc[sl, :] = dvv.reshape(rows, dk)
            dz_ref[sl, :] = dz.reshape(rows, dk).astype(BF)
            dba_ref[sl, :] += db.reshape(rows, db.shape[-1])
            dal_ref[...] += dal
            ddt_ref[...] += ddt
            dgain_ref[...] += dgn
            dstate[...] = dst
            return carry

        lax.fori_loop(0, ngroups, bstep, 0)

        for x_ref, c_ref, dpost, dx_ref, dc_ref in ((q_ref, cq_ref, dqc, dq_ref, dcq_ref),
                                                    (k_ref, ck_ref, dkc, dk_ref, dck_ref),
                                                    (v_ref, cv_ref, dvc, dv_ref, dcv_ref)):
            _, vjp = jax.vjp(lambda x, *taps: _conv_silu(_shift_vjp, x, taps), x_ref[...], *_load_taps(c_ref))
            grads = vjp(dpost[...])
            dx_ref[...] = grads[0].astype(BF)
            for t in range(CONV_WIDTH):
                dc_ref[t:t + 1, :] = grads[1 + t]

    nh = GDN_HEADS
    col_bf = jax.ShapeDtypeStruct((s, nh * dk), BF)
    tap_out = jax.ShapeDtypeStruct((CONV_WIDTH, nh * dk), F32)
    tap_spec = pl.BlockSpec((CONV_WIDTH, dk), lambda h: (0, h))
    dq, dkk, dvv, dz, dba, dcq, dck, dcv, dal, ddt, dgain = pl.pallas_call(
        body, name="gdn_bwd", grid=(nh,),
        in_specs=[sp['q'], sp['k'], sp['v'], sp['z'], sp['ba'], sp['cq'], sp['ck'], sp['cv'], sp['vec'], sp['vec'],
                  sp['gain'], pl.BlockSpec((None, ngroups, dk, dk), lambda h: (h, 0, 0, 0)), sp['head']],
        out_specs=[sp['head']] * 4 + [sp['ba'], tap_spec, tap_spec, tap_spec, sp['vec'], sp['vec'], sp['gain']],
        out_shape=[col_bf] * 4 + [jax.ShapeDtypeStruct((s, 128), F32), tap_out, tap_out, tap_out,
                                  jax.ShapeDtypeStruct((1, 128), F32), jax.ShapeDtypeStruct((1, 128), F32),
                                  jax.ShapeDtypeStruct((1, dk), F32)],
        scratch_shapes=[pltpu.VMEM((s, dk), F32)] * 6 + [pltpu.VMEM((dk, dk), F32)],
        compiler_params=_cparams(),
    )(proj, proj, proj, proj, ba, conv_w, conv_w, conv_w, a_log, dt_bias, gain, group_states, dy)
    return (jnp.concatenate([dq, dkk, dvv, dz], axis=1), dba, jnp.concatenate([dcq, dck, dcv], axis=1), dal, ddt,
            dgain)


def _lru_specs(s, bw):
    nb = LRU_BLOCKS
    return dict(
        xb=pl.BlockSpec((s, bw), lambda n: (0, n)), yb=pl.BlockSpec((s, bw), lambda n: (0, nb + n)),
        taps=pl.BlockSpec((CONV_WIDTH, bw), lambda n: (0, n)), vec=pl.BlockSpec((1, bw), lambda n: (0, n)),
        w=pl.BlockSpec((None, bw, bw), lambda n: (n, 0, 0)), b=pl.BlockSpec((None, 1, bw), lambda n: (n, 0, 0)),
        col=pl.BlockSpec((s, bw), lambda n: (0, n)))


def _lru_fwd(proj, conv_w, conv_b, lam, wr, br, wi, bi):
    s = proj.shape[0]
    bw = proj.shape[1] // (2 * LRU_BLOCKS)
    sp = _lru_specs(s, bw)

    def body(xb_ref, yb_ref, cw_ref, cb_ref, lam_ref, wr_ref, br_ref, wi_ref, bi_ref, y_ref):
        a, u, gy = _lru_pre(_mm_plain, _shift_rows, xb_ref[...], yb_ref[...], _load_taps(cw_ref), cb_ref[...],
                            lam_ref[...], wr_ref[...], br_ref[...], wi_ref[...], bi_ref[...])
        y_ref[...] = (_linscan(a, u, False) * gy).astype(BF)

    return pl.pallas_call(
        body, name="lru_fwd", grid=(LRU_BLOCKS,),
        in_specs=[sp['xb'], sp['yb'], sp['taps'], sp['vec'], sp['vec'], sp['w'], sp['b'], sp['w'], sp['b']],
        out_specs=sp['col'], out_shape=jax.ShapeDtypeStruct((s, LRU_BLOCKS * bw), BF),
        compiler_params=_cparams(),
    )(proj, proj, conv_w, conv_b, lam, wr, br, wi, bi)


def _lru_bwd(proj, conv_w, conv_b, lam, wr, br, wi, bi, dy):
    s = proj.shape[0]
    nb = LRU_BLOCKS
    bw = proj.shape[1] // (2 * nb)
    sp = _lru_specs(s, bw)

    def body(xb_ref, yb_ref, cw_ref, cb_ref, lam_ref, wr_ref, br_ref, wi_ref, bi_ref, dy_ref,
             dxb_ref, dyb_ref, dcw_ref, dcb_ref, dlam_ref, dwr_ref, dbr_ref, dwi_ref, dbi_ref):
        def pre(xb, yb, t0, t1, t2, t3, cb, lm, w_r, b_r, w_i, b_i):
            return _lru_pre(_mm_diff, _shift_vjp, xb, yb, (t0, t1, t2, t3), cb, lm, w_r, b_r, w_i, b_i)

        (a, u, gy), vjp = jax.vjp(pre, xb_ref[...], yb_ref[...], *_load_taps(cw_ref), cb_ref[...], lam_ref[...],
                                  wr_ref[...], br_ref[...], wi_ref[...], bi_ref[...])
        h = _linscan(a, u, False)
        dout = dy_ref[...].astype(F32)
        g = _linscan(_shift_rows(a, 1, True), dout * gy, True)
        grads = vjp((g * _shift_rows(h, 1, False), g, dout * h))
        dxb_ref[...] = grads[0].astype(BF)
        dyb_ref[...] = grads[1].astype(BF)
        for t in range(CONV_WIDTH):
            dcw_ref[t:t + 1, :] = grads[2 + t]
        dcb_ref[...] = grads[6]
        dlam_ref[...] = grads[7]
        dwr_ref[...] = grads[8]
        dbr_ref[...] = grads[9]
        dwi_ref[...] = grads[10]
        dbi_ref[...] = grads[11]

    outs = pl.pallas_call(
        body, name="lru_bwd", grid=(nb,),
        in_specs=[sp['xb'], sp['yb'], sp['taps'], sp['vec'], sp['vec'], sp['w'], sp['b'], sp['w'], sp['b'], sp['col']],
        out_specs=[sp['col'], sp['col'], sp['taps'], sp['vec'], sp['vec'], sp['w'], sp['b'], sp['w'], sp['b']],
        out_shape=[jax.ShapeDtypeStruct((s, nb * bw), BF), jax.ShapeDtypeStruct((s, nb * bw), BF),
                   jax.ShapeDtypeStruct((CONV_WIDTH, nb * bw), F32), jax.ShapeDtypeStruct((1, nb * bw), F32),
                   jax.ShapeDtypeStruct((1, nb * bw), F32), jax.ShapeDtypeStruct((nb, bw, bw), F32),
                   jax.ShapeDtypeStruct((nb, 1, bw), F32), jax.ShapeDtypeStruct((nb, bw, bw), F32),
                   jax.ShapeDtypeStruct((nb, 1, bw), F32)],
        compiler_params=_cparams(),
    )(proj, proj, conv_w, conv_b, lam, wr, br, wi, bi, dy)
    return (jnp.concatenate([outs[0], outs[1]], axis=1),) + tuple(outs[2:])


def _row_tile(rows, cols, bytes_per_row_set):
    t = 8
    while t * 2 <= rows and rows % (t * 2) == 0 and (t * 2) * cols * bytes_per_row_set <= 4 * 1024 * 1024:
        t *= 2
    return t


def _sum_slabs(a, out_dtype, name):
    n, rows, cols = a.shape
    tr = _row_tile(rows, cols, 4 * (n + 1))

    def body(*refs):
        acc = refs[0][...].astype(F32)
        for r in refs[1:n]:
            acc = acc + r[...].astype(F32)
        refs[n][...] = acc.astype(out_dtype)

    specs = [pl.BlockSpec((None, tr, cols), functools.partial(lambda i, k: (k, i, 0), k=k)) for k in range(n)]
    return pl.pallas_call(
        body, name=name, grid=(rows // tr,), in_specs=specs, out_specs=pl.BlockSpec((tr, cols), lambda i: (i, 0)),
        out_shape=jax.ShapeDtypeStruct((rows, cols), out_dtype), compiler_params=_cparams(),
    )(*([a] * n))


def _sum_keep_and_landed(g, landed, core, name):
    four, hr, cols = landed.shape
    tr = _row_tile(hr, cols, 2 * 3)
    tph = hr // tr

    def body(core_ref, g_ref, l_ref, o_ref):
        o_ref[...] = (g_ref[...].astype(F32) + l_ref[...].astype(F32)).astype(o_ref.dtype)

    grid_spec = pltpu.PrefetchScalarGridSpec(
        num_scalar_prefetch=1, grid=(four, tph),
        in_specs=[pl.BlockSpec((None, tr, cols), lambda j, r, core_ref: (j, core_ref[0] * tph + r, 0)),
                  pl.BlockSpec((None, tr, cols), lambda j, r, core_ref: (j, r, 0))],
        out_specs=pl.BlockSpec((None, tr, cols), lambda j, r, core_ref: (j, r, 0)))
    return pl.pallas_call(body, name=name, grid_spec=grid_spec, out_shape=jax.ShapeDtypeStruct(landed.shape, g.dtype),
                          compiler_params=_cparams())(core, g, landed)


def _sum_chip_parts(parts, landed, chip, name):
    _, rows, cols = parts.shape
    tr = _row_tile(rows, cols, 2 * 4 + 4)

    def body(chip_ref, p_ref, l0_ref, l1_ref, l2_ref, o_ref):
        acc = l0_ref[...].astype(F32) + l1_ref[...].astype(F32)
        o_ref[...] = (acc + l2_ref[...].astype(F32)) + p_ref[...].astype(F32)

    slab = lambda k: pl.BlockSpec((None, tr, cols), lambda i, chip_ref: (k, i, 0))
    grid_spec = pltpu.PrefetchScalarGridSpec(
        num_scalar_prefetch=1, grid=(rows // tr,),
        in_specs=[pl.BlockSpec((None, tr, cols), lambda i, chip_ref: (chip_ref[0], i, 0)), slab(0), slab(1), slab(2)],
        out_specs=pl.BlockSpec((tr, cols), lambda i, chip_ref: (i, 0)))
    return pl.pallas_call(body, name=name, grid_spec=grid_spec, out_shape=jax.ShapeDtypeStruct((rows, cols), F32),
                          compiler_params=_cparams())(chip, parts, landed, landed, landed)


_ADAM_C1 = 1.0 / (1.0 - ADAM_B1 ** ADAM_STEP)
_ADAM_C2 = 1.0 / (1.0 - ADAM_B2 ** ADAM_STEP)


def _adamw_math(w, g, m, v):
    nm = ADAM_B1 * m + (1.0 - ADAM_B1) * g
    nv = ADAM_B2 * v + (1.0 - ADAM_B2) * (g * g)
    den = jnp.sqrt(nv * _ADAM_C2) + ADAM_EPS
    inv = pl.reciprocal(den, approx=True)
    inv = inv * (2.0 - den * inv)
    delta = -ADAM_LR * ((nm * _ADAM_C1) * inv + ADAM_WD * w)
    return delta, nm, nv


def _adamw(w, g, m, v, name):
    rows, cols = w.shape
    tr = _row_tile(rows, cols, 4 * 7)

    def body(w_ref, g_ref, m_ref, v_ref, d_ref, nm_ref, nv_ref):
        d_ref[...], nm_ref[...], nv_ref[...] = _adamw_math(w_ref[...], g_ref[...], m_ref[...], v_ref[...])

    spec = pl.BlockSpec((tr, cols), lambda i: (i, 0))
    shape = jax.ShapeDtypeStruct((rows, cols), F32)
    return pl.pallas_call(
        body, name=name, grid=(rows // tr,), in_specs=[spec] * 4, out_specs=[spec] * 3, out_shape=[shape] * 3,
        compiler_params=_cparams(),
    )(w, g, m, v)


def _adamw_halves(w, m, v, layer, mine, theirs, core, name, carried=None):
    n_layers, rows, cols = w.shape
    hr = mine.shape[0]
    tr = _row_tile(hr, cols, 4 * 9)
    tph = hr // tr

    def body(core_ref, w_ref, q_ref, t_ref, m_ref, v_ref, *rest):
        g_ref, d_ref, nm_ref, nv_ref = rest[-4:]
        is_mine = (pl.program_id(0) // tph) == core_ref[0]
        g = jnp.where(is_mine, q_ref[...], t_ref[...])
        g_ref[...] = g
        d_ref[...], nm_ref[...], nv_ref[...] = _adamw_math(w_ref[...], g, m_ref[...], v_ref[...])

    slab = pl.BlockSpec((None, tr, cols), lambda i, core_ref: (layer, i, 0))
    half = pl.BlockSpec((tr, cols), lambda i, core_ref: (i % tph, 0))
    in_specs = [slab, half, half, slab, slab]
    operands = [core, w, mine, theirs, m, v]
    aliases = {}
    if carried is not None:
        in_specs += [_ANY] * 4
        aliases = {len(operands) + k: k for k in range(4)}
        operands += list(carried)
    grid_spec = pltpu.PrefetchScalarGridSpec(num_scalar_prefetch=1, grid=(rows // tr,), in_specs=in_specs,
                                             out_specs=[slab] * 4)
    return pl.pallas_call(
        body, name=name, grid_spec=grid_spec, out_shape=[jax.ShapeDtypeStruct(w.shape, F32)] * 4,
        input_output_aliases=aliases, compiler_params=_cparams(),
    )(*operands)


_ANY = pl.BlockSpec(memory_space=pl.ANY)


def _mesh_pos():
    return lax.axis_index("x"), lax.axis_index("y"), lax.axis_index("c")


def _remote(src, dst, send_sems, recv_sems, k, dev):
    return pltpu.make_async_remote_copy(src_ref=src, dst_ref=dst, send_sem=send_sems.at[k], recv_sem=recv_sems.at[k],
                                        device_id=dev, device_id_type=MESH)


STREAM_CHUNK_BYTES = 1024 * 1024
STREAM_SLOTS = 3


def _chunk_rows(rows, cols, itemsize):
    t = 16
    while t * 2 <= rows and rows % (t * 2) == 0 and (t * 2) * cols * itemsize <= STREAM_CHUNK_BYTES:
        t *= 2
    return t


def _stream_chunks(n_chunks, src_at, dst_at, buf, load_sems, send_sems, recv_sem, sibling):
    def load(k, slot):
        return pltpu.make_async_copy(src_at(k), buf.at[slot], load_sems.at[slot])

    def send(k, slot):
        return pltpu.make_async_remote_copy(src_ref=buf.at[slot], dst_ref=dst_at(k), send_sem=send_sems.at[slot],
                                            recv_sem=recv_sem, device_id=sibling, device_id_type=MESH)

    def step(k, carry):
        slot = k % STREAM_SLOTS

        @pl.when(k >= STREAM_SLOTS)
        def _():
            send(k - STREAM_SLOTS, slot).wait_send()

        load(k, slot).start()

        @pl.when(k >= 1)
        def _():
            prev = (k - 1) % STREAM_SLOTS
            load(k - 1, prev).wait()
            send(k - 1, prev).start()

        return carry

    lax.fori_loop(0, n_chunks, step, 0)
    last = (n_chunks - 1) % STREAM_SLOTS
    load(n_chunks - 1, last).wait()
    send(n_chunks - 1, last).start()
    for k in range(max(0, n_chunks - STREAM_SLOTS), n_chunks):
        send(k, k % STREAM_SLOTS).wait_send()


_HBM = pl.BlockSpec(memory_space=pltpu.HBM)
_SEM = pl.BlockSpec(memory_space=pltpu.SEMAPHORE)
_DATAFLOW = pltpu.SideEffectType.DATAFLOW_SIDE_EFFECTING


def _chip_copies(kind, ins, lands, split, send_sems, recv_sems):
    x, y, c = _mesh_pos()
    me = 2 * x + y
    chips = [(1 - x, y), (x, 1 - y), (1 - x, 1 - y)]
    pairs = []
    for i in range(len(ins)):
        for j, chip in enumerate(chips):
            pj = 2 * chip[0] + chip[1]
            if kind == 'scatter':
                src, dst, got = ins[i].at[pj], lands[i].at[j], lands[i].at[j]
            elif split[i]:
                hr = ins[i].shape[0] // 2
                rows = pl.ds(c * hr, hr)
                src, dst, got = ins[i].at[rows], lands[i].at[me, rows], lands[i].at[pj, rows]
            else:
                src, dst, got = ins[i], lands[i].at[me], lands[i].at[pj]
            k = 3 * i + j
            pairs.append((_remote(src, dst, send_sems, recv_sems, k, (*chip, c)),
                          _remote(got, got, send_sems, recv_sems, k, (*chip, c))))
    return pairs


def _chip_exchange_start(kind, srcs, land_shapes, split, name, after):
    n = len(srcs)

    def body(*refs):
        ins, lands = refs[:n], refs[n:2 * n]
        send_sems, recv_sems = refs[2 * n + 1], refs[2 * n + 2]
        token = refs[-1]
        for send, _ in _chip_copies(kind, ins, lands, split, send_sems, recv_sems):
            send.start()
        token[...] = jnp.zeros_like(token)

    hbm = lambda t: pltpu.with_memory_space_constraint(t, pltpu.HBM)
    operands = [hbm(s) for s in srcs] + [hbm(lax.empty(shp, s.dtype)) for shp, s in zip(land_shapes, srcs)] + [after]
    out = pl.pallas_call(
        body, name=name, in_specs=[_HBM] * (2 * n) + [_ANY],
        out_specs=[_SEM, _SEM] + [_HBM] * (2 * n) + [pl.BlockSpec(memory_space=pltpu.VMEM)],
        out_shape=[pltpu.SemaphoreType.DMA((3 * n,)), pltpu.SemaphoreType.DMA((3 * n,))]
        + [pltpu.HBM(s.shape, s.dtype) for s in srcs] + [pltpu.HBM(shp, s.dtype) for shp, s in zip(land_shapes, srcs)]
        + [jax.ShapeDtypeStruct((8, 128), F32)],
        input_output_aliases={i: 2 + i for i in range(2 * n)},
        compiler_params=pltpu.CompilerParams(has_side_effects=_DATAFLOW),
    )(*operands)
    return out[0], out[1], out[2:2 + n], out[2 + n:2 + 2 * n], out[-1]


def _chip_exchange_wait(kind, started, split, name, after):
    send_sems, recv_sems, srcs, lands, _ = started
    n = len(srcs)

    def body(*refs):
        ins, land_refs = refs[:n], refs[n:2 * n]
        for send, arrived in _chip_copies(kind, ins, land_refs, split, refs[2 * n], refs[2 * n + 1]):
            send.wait_send()
            arrived.wait_recv()

    out = pl.pallas_call(
        body, name=name, in_specs=[_HBM] * (2 * n) + [_SEM, _SEM, _ANY], out_specs=[_HBM] * (2 * n),
        out_shape=[pltpu.HBM(s.shape, s.dtype) for s in srcs] + [pltpu.HBM(l.shape, l.dtype) for l in lands],
        input_output_aliases={i: i for i in range(2 * n)},
        compiler_params=pltpu.CompilerParams(has_side_effects=_DATAFLOW),
    )(*srcs, *lands, send_sems, recv_sems, after)
    return out[:n], out[n:]


def _pass_to_sibling(gathered, name):
    n = len(gathered)
    tr = [_chunk_rows(g.shape[1] // 2, g.shape[2], g.dtype.itemsize) for g in gathered]

    def body(*refs):
        outs = refs[n:2 * n]
        recv_sems, load_sems, send_sems = refs[2 * n:2 * n + 3]
        bufs = refs[2 * n + 3:]
        x, y, c = _mesh_pos()
        sibling = (x, y, 1 - c)
        chips = [(1 - x, y), (x, 1 - y), (1 - x, 1 - y)]
        for i in range(n):
            hr = outs[i].shape[1] // 2
            for j, chip in enumerate(chips):
                def rows_at(k, i=i, pj=2 * chip[0] + chip[1], hr=hr):
                    return outs[i].at[pj, pl.ds(c * hr + k * tr[i], tr[i])]

                _stream_chunks(hr // tr[i], rows_at, rows_at, bufs[i], load_sems, send_sems, recv_sems.at[3 * i + j],
                               sibling)
        for i in range(n):
            hr = outs[i].shape[1] // 2
            for j, chip in enumerate(chips):
                blk = outs[i].at[2 * chip[0] + chip[1], pl.ds((1 - c) * hr, hr)]
                pltpu.make_async_remote_copy(src_ref=blk, dst_ref=blk, send_sem=send_sems.at[0],
                                             recv_sem=recv_sems.at[3 * i + j], device_id=sibling,
                                             device_id_type=MESH).wait_recv()

    return pl.pallas_call(
        body, name=name, in_specs=[_ANY] * n, out_specs=[_ANY] * n,
        out_shape=[jax.ShapeDtypeStruct(g.shape, g.dtype) for g in gathered],
        input_output_aliases={i: i for i in range(n)},
        scratch_shapes=[pltpu.SemaphoreType.DMA((3 * n,)), pltpu.SemaphoreType.DMA((STREAM_SLOTS,)),
                        pltpu.SemaphoreType.DMA((STREAM_SLOTS,))]
        + [pltpu.VMEM((STREAM_SLOTS, tr[i], g.shape[2]), g.dtype) for i, g in enumerate(gathered)],
        compiler_params=_cparams(),
    )(*gathered)


def _all_gather_devices(part, name):
    def body(in_ref, out_ref, send_sems, recv_sems, local_sem):
        x, y, c = _mesh_pos()
        me = 4 * x + 2 * y + c
        local = pltpu.make_async_copy(in_ref, out_ref.at[me], local_sem)
        local.start()
        peers = [(x ^ (k >> 2), y ^ ((k >> 1) & 1), c ^ (k & 1)) for k in range(1, N_DEV)]
        sends = [_remote(in_ref, out_ref.at[me], send_sems, recv_sems, k, p) for k, p in enumerate(peers)]
        for cp in sends:
            cp.start()
        for k, p in enumerate(peers):
            blk = out_ref.at[4 * p[0] + 2 * p[1] + p[2]]
            _remote(blk, blk, send_sems, recv_sems, k, p).wait_recv()
        for cp in sends:
            cp.wait_send()
        local.wait()

    return pl.pallas_call(
        body, name=name, in_specs=[_ANY], out_specs=_ANY,
        out_shape=jax.ShapeDtypeStruct((N_DEV,) + part.shape, part.dtype),
        scratch_shapes=[pltpu.SemaphoreType.DMA((N_DEV - 1,)), pltpu.SemaphoreType.DMA((N_DEV - 1,)),
                        pltpu.SemaphoreType.DMA],
    )(part)


def _stream_to_sibling(srcs, halved, name):
    n = len(srcs)
    geo = []
    for s in srcs:
        rows = s.shape[1] // 2 if halved else s.shape[1]
        geo.append((s.shape[0], rows, s.shape[2], _chunk_rows(rows, s.shape[2], s.dtype.itemsize)))

    def body(*refs):
        ins, outs = refs[:n], refs[n:2 * n]
        recv_sems, load_sems, send_sems = refs[2 * n:2 * n + 3]
        bufs = refs[2 * n + 3:]
        x, y, c = _mesh_pos()
        sibling = (x, y, 1 - c)
        for i in range(n):
            slabs, rows, _, tr = geo[i]
            per_slab = rows // tr
            off = (1 - c) * rows if halved else 0

            def src_at(k, i=i, per_slab=per_slab, tr=tr, off=off):
                return ins[i].at[k // per_slab, pl.ds(off + (k % per_slab) * tr, tr)]

            def dst_at(k, i=i, per_slab=per_slab, tr=tr):
                return outs[i].at[k // per_slab, pl.ds((k % per_slab) * tr, tr)]

            _stream_chunks(slabs * per_slab, src_at, dst_at, bufs[i], load_sems, send_sems, recv_sems.at[i], sibling)
        for i in range(n):
            pltpu.make_async_remote_copy(src_ref=outs[i], dst_ref=outs[i], send_sem=send_sems.at[0],
                                         recv_sem=recv_sems.at[i], device_id=sibling, device_id_type=MESH).wait_recv()

    return pl.pallas_call(
        body, name=name, in_specs=[_ANY] * n, out_specs=[_ANY] * n,
        out_shape=[jax.ShapeDtypeStruct((g[0], g[1], g[2]), s.dtype) for g, s in zip(geo, srcs)],
        scratch_shapes=[pltpu.SemaphoreType.DMA((n,)), pltpu.SemaphoreType.DMA((STREAM_SLOTS,)),
                        pltpu.SemaphoreType.DMA((STREAM_SLOTS,))]
        + [pltpu.VMEM((STREAM_SLOTS, g[3], g[2]), s.dtype) for g, s in zip(geo, srcs)],
        compiler_params=_cparams(),
    )(*srcs)


def _reduce_scatter_begin(grads, tag, core):
    landed = _stream_to_sibling(grads, True, f"rs_swap_{tag}")
    parts = [_sum_keep_and_landed(g, l, core, f"rs_add2_{tag}_{i}") for i, (g, l) in enumerate(zip(grads, landed))]
    return _chip_exchange_start('scatter', parts, [(3,) + p.shape[1:] for p in parts], None, f"rs_scatter_start_{tag}",
                                core)


def _reduce_scatter_finish(started, tag, chip, after):
    parts, landed = _chip_exchange_wait('scatter', started, None, f"rs_scatter_wait_{tag}", after)
    mine = [_sum_chip_parts(p, l, chip, f"rs_add4_{tag}_{i}") for i, (p, l) in enumerate(zip(parts, landed))]
    theirs = _stream_to_sibling([m[None] for m in mine], False, f"rs_join_{tag}")
    return [(m, t[0]) for m, t in zip(mine, theirs)]


def _pad_cols(a, width=128):
    return jnp.pad(a, ((0, 0), (0, width - a.shape[1])))


def _mixer_forward(kind, hn, w, tables):
    if kind == 0:
        proj = _matmul(hn, w['ret_w_in'], name="ret_proj", b_sharded=True)
        og, states = _ret_fwd(proj, w['ret_gn_gain'], tables)
        return og, (proj, states)
    if kind == 1:
        proj = _matmul(hn, w['gdn_w_main'], name="gdn_proj")
        ba = _matmul(hn, w['gdn_w_small'], name="gdn_proj_ba")
        og, states = _gdn_fwd(proj, ba, w['gdn_conv_w'], w['gdn_a_log'], w['gdn_dt_bias'], w['gdn_norm_gain'])
        return og, (proj, ba, states)
    if kind == 2:
        proj = _matmul(hn, w['gla_w_main'], name="gla_proj")
        glow = _matmul(hn, w['gla_w_small'], name="gla_proj_gate")
        og, states = _gla_fwd(proj, glow, w['gla_w_gate_up'], w['gla_gate_bias'], w['gla_norm_gain'])
        return og, (proj, glow, states)
    proj = _matmul(hn, w['lru_w_in'], name="lru_proj", b_sharded=True)
    og = _lru_fwd(proj, w['lru_conv_w'], w['lru_conv_b'], w['lru_lambda'], w['lru_w_rgate'], w['lru_b_rgate'],
                  w['lru_w_igate'], w['lru_b_igate'])
    return og, (proj,)


def _mixer_backward(kind, hn, w, tables, saved, d_og, grads):
    d = hn.shape[1]
    if kind == 0:
        proj, states = saved
        d_proj, grads['ret_gn_gain'] = _ret_bwd(proj, w['ret_gn_gain'], tables, states, d_og)
        grads['ret_w_in'] = _matmul(hn, d_proj, name="ret_dw_in", ta=True, out_dtype=BF, o_sharded=True)
        return _matmul(d_proj, w['ret_w_in'], name="ret_dhn", tb=True, b_sharded=True)
    if kind == 1:
        proj, ba, states = saved
        d_proj, d_ba, grads['gdn_conv_w'], grads['gdn_a_log'], grads['gdn_dt_bias'], grads['gdn_norm_gain'] = _gdn_bwd(
            proj, ba, w['gdn_conv_w'], w['gdn_a_log'], w['gdn_dt_bias'], w['gdn_norm_gain'], states, d_og)
        d_ba = d_ba.astype(BF)
        dw_main = _matmul(hn, d_proj, name="gdn_dw_main", ta=True, out_dtype=BF)
        dw_small = _matmul(hn, d_ba, name="gdn_dw_small", ta=True, out_dtype=BF)
        dw = jnp.concatenate([dw_main, dw_small[:, :2 * GDN_HEADS]], axis=1)
        grads['gdn_w_in'] = dw.reshape(d, N_CHIPS, dw.shape[1] // N_CHIPS).transpose(1, 0, 2)
        d_hn = _matmul(d_proj, w['gdn_w_main'], name="gdn_dhn_main", tb=True)
        return _matmul(d_ba, w['gdn_w_small'], name="gdn_dhn_small", tb=True, epilogue='add', extra=d_hn)
    if kind == 2:
        proj, glow, states = saved
        d_proj, d_glow4, d_wgu, grads['gla_gate_bias'], grads['gla_norm_gain'] = _gla_bwd(
            proj, glow, w['gla_w_gate_up'], w['gla_gate_bias'], w['gla_norm_gain'], states, d_og)
        grads['gla_w_gate_up'] = d_wgu[:GLA_RANK]
        dw_main = _matmul(hn, d_proj, name="gla_dw_main", ta=True, out_dtype=BF)
        dw_small4 = _matmul(hn, d_glow4, name="gla_dw_small", ta=True, out_dtype=F32)
        dw_small = dw_small4.reshape(d, GLA_HEADS, 128)[:, :, :GLA_RANK].sum(axis=1).astype(BF)
        dw = jnp.concatenate([dw_main, dw_small], axis=1)
        grads['gla_w_in'] = dw.reshape(d, N_CHIPS, dw.shape[1] // N_CHIPS).transpose(1, 0, 2)
        d_hn = _matmul(d_proj, w['gla_w_main'], name="gla_dhn_main", tb=True)
        w_small4 = jnp.tile(w['gla_w_small'], (1, GLA_HEADS))
        return _matmul(d_glow4, w_small4, name="gla_dhn_small", tb=True, epilogue='add', extra=d_hn)
    (proj,) = saved
    (d_proj, grads['lru_conv_w'], grads['lru_conv_b'], grads['lru_lambda'], grads['lru_w_rgate'], grads['lru_b_rgate'],
     grads['lru_w_igate'], grads['lru_b_igate']) = _lru_bwd(
        proj, w['lru_conv_w'], w['lru_conv_b'], w['lru_lambda'], w['lru_w_rgate'], w['lru_b_rgate'], w['lru_w_igate'],
        w['lru_b_igate'], d_og)
    grads['lru_w_in'] = _matmul(hn, d_proj, name="lru_dw_in", ta=True, out_dtype=BF, o_sharded=True)
    return _matmul(d_proj, w['lru_w_in'], name="lru_dhn", tb=True, b_sharded=True)


_W_OUT = ('ret_w_out', 'gdn_w_out', 'gla_w_out', 'lru_w_out')
_W_IN = ('ret_w_in', 'gdn_w_in', 'gla_w_in', 'lru_w_in')


def _layer_forward(layer, x, w, tables, token=None):
    hn = _rmsnorm_fwd(x, w['norm1'][layer], f"norm1_fwd_{layer}", token)
    og, mixer_saved = _mixer_forward(layer, hn, w, tables)
    x1 = _matmul(og, w[_W_OUT[layer]], name=f"mixer_out_{layer}", epilogue='add', extra=x)
    hn2 = _rmsnorm_fwd(x1, w['norm2'][layer], f"norm2_fwd_{layer}")
    act = _matmul(hn2, w['mlp_w_up'][layer], name="mlp_up", b_sharded=True, epilogue='relu2', out_dtype=BF)
    x2 = _matmul(act, w['mlp_w_down'][layer], name="mlp_down", epilogue='add', extra=x1)
    return x2, (x, hn, mixer_saved, og, x1, hn2, act)


def _layer_backward(layer, dx2, w, tables, saved, token=None):
    x, hn, mixer_saved, og, x1, hn2, act = saved
    d = x.shape[1]
    grads = {}
    d_up = _matmul(dx2, w['mlp_w_down'][layer], name="mlp_d_up", tb=True, epilogue='dact', extra=act, out_dtype=BF,
                   token=token)
    dw_down = _matmul(act, dx2, name="mlp_dw_down", ta=True, out_dtype=BF)
    grads['mlp_w_down'] = dw_down.reshape(N_CHIPS, dw_down.shape[0] // N_CHIPS, d)
    grads['mlp_w_up'] = _matmul(hn2, d_up, name="mlp_dw_up", ta=True, out_dtype=BF, o_sharded=True)
    d_hn2 = _matmul(d_up, w['mlp_w_up'][layer], name="mlp_d_hn", tb=True, b_sharded=True)
    dx1, grads['norm2'] = _rmsnorm_bwd(x1, w['norm2'][layer], d_hn2, dx2, f"norm2_bwd_{layer}")
    w_out = w[_W_OUT[layer]]
    d_og = _matmul(dx1, w_out, name=f"mixer_d_og_{layer}", tb=True, out_dtype=BF)
    dw_out = _matmul(og, dx1, name=f"mixer_dw_out_{layer}", ta=True, out_dtype=BF)
    grads[_W_OUT[layer]] = dw_out.reshape(N_CHIPS, dw_out.shape[0] // N_CHIPS, d)
    d_hn = _mixer_backward(layer, hn, w, tables, mixer_saved, d_og, grads)
    dx, grads['norm1'] = _rmsnorm_bwd(x, w['norm1'][layer], d_hn, dx1, f"norm1_bwd_{layer}")
    return dx, grads


def _pack(arrays):
    flat = []
    for a in arrays:
        v = a.astype(F32).reshape(-1)
        v = jnp.pad(v, (0, (-v.shape[0]) % 128))
        flat.append(v.reshape(-1, 128))
    buf = jnp.concatenate(flat, axis=0)
    return jnp.pad(buf, ((0, (-buf.shape[0]) % 8), (0, 0)))


def _unpack(buf, shapes):
    lead = buf.shape[:-2]
    out, off = [], 0
    for shp in shapes:
        n = math.prod(shp)
        rows = -(-n // 128)
        piece = buf[..., off:off + rows, :].reshape(lead + (rows * 128,))[..., :n]
        out.append(piece.reshape(lead + tuple(shp)))
        off += rows
    return out


_WEIGHTS = ('norm1', 'norm2', 'final_norm', 'ret_w_in', 'ret_gn_gain', 'ret_w_out', 'gdn_w_in', 'gdn_conv_w',
            'gdn_a_log', 'gdn_dt_bias', 'gdn_norm_gain', 'gdn_w_out', 'gla_w_in', 'gla_w_gate_up', 'gla_gate_bias',
            'gla_norm_gain', 'gla_w_out', 'lru_w_in', 'lru_conv_w', 'lru_conv_b', 'lru_w_rgate', 'lru_b_rgate',
            'lru_w_igate', 'lru_b_igate', 'lru_lambda', 'lru_w_out', 'mlp_w_up', 'mlp_w_down')
_FWD_PARAMS = ('x',) + _WEIGHTS
_BIG = ('ret_w_in', 'ret_w_out', 'gdn_w_in', 'gdn_w_out', 'gla_w_in', 'gla_w_out', 'lru_w_in', 'lru_w_out',
        'mlp_w_up', 'mlp_w_down')
_SMALL = tuple(n for n in _WEIGHTS if n not in _BIG)
_SMALL_SHARDED = ('ret_gn_gain', 'gdn_conv_w', 'gla_w_gate_up', 'gla_gate_bias', 'gla_norm_gain', 'lru_conv_w',
                  'lru_conv_b', 'lru_lambda')


def kernel(*args):
    names = _FWD_PARAMS + ('loss_target',) + tuple('m_' + n for n in _WEIGHTS) + tuple('v_' + n for n in _WEIGHTS)
    assert len(args) == len(names)
    a = dict(zip(names, args))
    x = a['x'][0]
    target = a['loss_target'][0]
    s, d = x.shape
    chip = 2 * lax.axis_index("x") + lax.axis_index("y")

    small_local = [a[n][0] if a[n].ndim == 3 else a[n] for n in _SMALL_SHARDED]
    small_pack = _pack(small_local)
    core_arr = lax.axis_index("c").astype(jnp.int32).reshape(1)
    chip_arr = chip.astype(jnp.int32).reshape(1)

    def whole_cols(g):
        return g.transpose(1, 0, 2).reshape(g.shape[1], N_CHIPS * g.shape[2])

    def whole_rows(g):
        return g.reshape(N_CHIPS * g.shape[1], g.shape[2])

    w = {'mlp_w_up': [None] * 4, 'mlp_w_down': [None] * 4}
    for n in _SMALL:
        if n not in _SMALL_SHARDED:
            w[n] = a[n][0] if n.startswith('lru_') else a[n]
    w['gdn_a_log'], w['gdn_dt_bias'] = _pad_cols(w['gdn_a_log']), _pad_cols(w['gdn_dt_bias'])
    w['lru_b_rgate'] = w['lru_b_rgate'].reshape(LRU_BLOCKS, 1, -1)
    w['lru_b_igate'] = w['lru_b_igate'].reshape(LRU_BLOCKS, 1, -1)

    def gather_start(layer, after):
        ops = [a[_W_IN[layer]][0].astype(BF), a[_W_OUT[layer]][0].astype(BF), a['mlp_w_up'][layer].astype(BF),
               a['mlp_w_down'][layer].astype(BF)] + ([small_pack] if layer == 0 else [])
        split = [True] * 4 + ([False] if layer == 0 else [])
        return _chip_exchange_start('gather', ops, [(N_CHIPS,) + o.shape for o in ops], split,
                                    f"gather_start_{layer}", after), split

    def gather_finish(layer, started, after):
        started, split = started
        own, lands = _chip_exchange_wait('gather', started, split, f"gather_wait_{layer}", after)
        lands = list(_pass_to_sibling(lands[:4], f"gather_pass_{layer}")) + list(lands[4:])
        g_in, g_out, g_up, g_down = [lax.dynamic_update_slice_in_dim(l, o[None], chip, axis=0)
                                     for l, o in zip(lands[:4], own[:4])]
        w['mlp_w_up'][layer], w['mlp_w_down'][layer], w[_W_OUT[layer]] = g_up, whole_rows(g_down), whole_rows(g_out)
        if layer == 0:
            small_all = lax.dynamic_update_slice_in_dim(lands[4], own[4][None], chip, axis=0)
            for n, piece in zip(_SMALL_SHARDED, _unpack(small_all, [p.shape for p in small_local])):
                w[n] = whole_cols(piece)
            w['gla_w_gate_up'] = jnp.pad(w['gla_w_gate_up'], ((0, 128 - GLA_RANK), (0, 0)))
        if layer in (0, 3):
            w[_W_IN[layer]] = g_in
        else:
            name, tail = ('gdn', 2 * GDN_HEADS) if layer == 1 else ('gla', GLA_RANK)
            full = whole_cols(g_in)
            w[name + '_w_main'] = full[:, :full.shape[1] - tail]
            w[name + '_w_small'] = _pad_cols(full[:, full.shape[1] - tail:])

    tables = _ret_tables(s, d // RET_HEADS)
    saved = []
    h = x
    gather_finish(0, gather_start(0, x), x)
    for layer in range(4):
        nxt = gather_start(layer + 1, w[_W_OUT[layer]]) if layer < 3 else None
        h, sv = _layer_forward(layer, h, w, tables, None if nxt is None else nxt[0][4])
        saved.append(sv)
        if nxt is not None:
            gather_finish(layer + 1, nxt, h)
    loss_part, dh, g_final = _loss_head(h, w['final_norm'], target)
    loss = lax.psum(loss_part, ("x", "y", "c"))

    grad, delta, new_m, new_v = {}, {}, {}, {}
    small_grads = {'final_norm': g_final}
    norm_grads = {'norm1': [None] * 4, 'norm2': [None] * 4}
    mlp_upd = {'mlp_w_up': None, 'mlp_w_down': None}

    def update_layer(layer, started, after):
        red = _reduce_scatter_finish(started, str(layer), chip_arr, after)
        for n, (mine, theirs) in ((_W_IN[layer], red[0]), (_W_OUT[layer], red[1])):
            grad[n], delta[n], new_m[n], new_v[n] = _adamw_halves(a[n], a['m_' + n], a['v_' + n], 0, mine, theirs,
                                                                  core_arr, f"adamw_{n}")
        for n, (mine, theirs) in (('mlp_w_up', red[2]), ('mlp_w_down', red[3])):
            mlp_upd[n] = _adamw_halves(a[n], a['m_' + n], a['v_' + n], layer, mine, theirs, core_arr,
                                       f"adamw_{n}_{layer}", carried=mlp_upd[n])

    pending = None
    for layer in reversed(range(4)):
        dh, g = _layer_backward(layer, dh, w, tables, saved[layer], None if pending is None else pending[1][4])
        saved[layer] = None
        started = _reduce_scatter_begin([g[_W_IN[layer]], g[_W_OUT[layer]], g['mlp_w_up'], g['mlp_w_down']],
                                        str(layer), core_arr)
        if pending is not None:
            update_layer(*pending, started[4])
        pending = (layer, started)
        norm_grads['norm1'][layer], norm_grads['norm2'][layer] = g['norm1'], g['norm2']
        for n in _SMALL:
            if n in g:
                small_grads[n] = g[n]
    small_grads['norm1'] = jnp.stack(norm_grads['norm1'])
    small_grads['norm2'] = jnp.stack(norm_grads['norm2'])

    full_shapes = [small_grads[n].shape for n in _SMALL]
    small_part = _pack([small_grads[n] for n in _SMALL]) + pending[1][4][0, 0]
    total = _sum_slabs(_all_gather_devices(small_part, "gather_small_grads"), F32, "sum_small_grads")
    local_g = {}
    for n, full in zip(_SMALL, _unpack(total, full_shapes)):
        shp = a[n].shape
        if n in _SMALL_SHARDED:
            full = full.reshape(full.shape[0], -1)
            cq = shp[-1]
            full = lax.dynamic_slice_in_dim(full, chip * cq, cq, axis=1)
        elif n in ('gdn_a_log', 'gdn_dt_bias'):
            full = full[:, :shp[-1]]
        local_g[n] = full.reshape(shp)
    shapes = [a[n].shape for n in _SMALL]
    packed = [_pack([src[n] for n in _SMALL]) for src in
              (a, local_g, {n: a['m_' + n] for n in _SMALL}, {n: a['v_' + n] for n in _SMALL})]
    upd = _adamw(*packed, "adamw_small")
    for n, gr, dl, nm, nv in zip(_SMALL, _unpack(packed[1], shapes), *[_unpack(u, shapes) for u in upd]):
        grad[n], delta[n], new_m[n], new_v[n] = gr, dl, nm, nv
    update_layer(*pending, upd[0])
    for n in ('mlp_w_up', 'mlp_w_down'):
        grad[n], delta[n], new_m[n], new_v[n] = mlp_upd[n]

    out = [loss, dh.reshape(a['x'].shape)]
    for group in (grad, delta, new_m, new_v):
        out += [group[n].reshape(a[n].shape) for n in _WEIGHTS]
    return tuple(out)
```

```python
import functools
import math

import jax
import jax.numpy as jnp
from jax import lax
from jax.experimental import pallas as pl
from jax.experimental.pallas import tpu as pltpu

F32 = jnp.float32
BF = jnp.bfloat16
MESH = pl.DeviceIdType.MESH

NORM_EPS = 1e-6
CHUNK = 64
RET_HEADS = 8
GDN_HEADS = 16
GLA_HEADS = 4
GLA_RANK = 16
GLA_TAU = 16.0
LRU_BLOCKS = 16
LRU_C = 8.0
CONV_WIDTH = 4
ROPE_BASE = 10000.0
N_CHIPS = 4
N_DEV = 8

ADAM_LR = 0.001
ADAM_B1 = 0.9
ADAM_B2 = 0.999
ADAM_EPS = 1e-08
ADAM_WD = 0.01
ADAM_STEP = 10

VMEM_LIMIT_BYTES = 56 * 1024 * 1024
TOKEN_BLOCK = 256
ROW_BLOCK = 256
GDN_GROUP = 8


def _cparams(**kw):
    return pltpu.CompilerParams(vmem_limit_bytes=VMEM_LIMIT_BYTES, **kw)


def _bf16_parts(x, n):
    parts = []
    for _ in range(n - 1):
        p = x.astype(BF)
        parts.append(p)
        x = x - p.astype(F32)
    return parts + [x.astype(BF)]


def _raw_mm(a, b, ta, tb, hi):
    nb = a.ndim - 2
    batch = tuple(range(nb))
    dims = (((nb + (0 if ta else 1),), (nb + (1 if tb else 0),)), (batch, batch))

    def dot(p, q):
        return lax.dot_general(p, q, dims, preferred_element_type=F32)

    if not hi:
        return dot(a.astype(BF), b.astype(BF))
    if hi == 'l':
        ae = a.astype(BF)
        b0, b1, b2 = _bf16_parts(b.astype(F32), 3)
        return dot(ae, b0) + (dot(ae, b1) + dot(ae, b2))
    if hi == 'r':
        be = b.astype(BF)
        a0, a1, a2 = _bf16_parts(a.astype(F32), 3)
        return dot(a0, be) + (dot(a1, be) + dot(a2, be))
    a0, a1 = _bf16_parts(a.astype(F32), 2)
    b0, b1 = _bf16_parts(b.astype(F32), 2)
    return dot(a0, b0) + (dot(a0, b1) + dot(a1, b0))


@functools.partial(jax.custom_vjp, nondiff_argnums=(2, 3, 4))
def _mm_vjp(a, b, ta, tb, hi):
    return _raw_mm(a, b, ta, tb, hi)


def _mm_vjp_fwd(a, b, ta, tb, hi):
    return _raw_mm(a, b, ta, tb, hi), (a, b)


def _mm_vjp_bwd(ta, tb, hi, res, g):
    a, b = res
    if ta:
        da = _raw_mm(b, g, tb, True, 'l' if hi == 'r' else bool(hi))
    else:
        da = _raw_mm(g, b, False, not tb, 'r' if hi == 'r' else bool(hi))
    if tb:
        db = _raw_mm(g, a, True, ta, 'r' if hi == 'l' else bool(hi))
    else:
        db = _raw_mm(a, g, not ta, False, 'l' if hi == 'l' else bool(hi))
    return da, db


_mm_vjp.defvjp(_mm_vjp_fwd, _mm_vjp_bwd)


def _mm_diff(a, b, ta=False, tb=False, hi=False):
    return _mm_vjp(a, b, ta, tb, hi)


def _mm_plain(a, b, ta=False, tb=False, hi=False):
    return _raw_mm(a, b, ta, tb, hi)


def _shift_rows(x, k, up):
    if k == 0:
        return x
    n = x.shape[0]
    rows = lax.broadcasted_iota(jnp.int32, x.shape, 0)
    if up:
        return jnp.where(rows < n - k, pltpu.roll(x, n - k, 0), 0.0)
    return jnp.where(rows >= k, pltpu.roll(x, k, 0), 0.0)


@functools.partial(jax.custom_vjp, nondiff_argnums=(1, 2))
def _shift_vjp(x, k, up):
    return _shift_rows(x, k, up)


def _shift_vjp_fwd(x, k, up):
    return _shift_rows(x, k, up), None


def _shift_vjp_bwd(k, up, _, g):
    return (_shift_rows(g, k, not up),)


_shift_vjp.defvjp(_shift_vjp_fwd, _shift_vjp_bwd)


def _sigmoid(x):
    return 1.0 / (1.0 + jnp.exp(-x))


def _silu(x):
    return x * _sigmoid(x)


def _softplus(x):
    return jnp.maximum(x, 0.0) + jnp.log(1.0 + jnp.exp(-jnp.abs(x)))


def _gelu_tanh(x):
    return 0.5 * x * (1.0 + jnp.tanh(math.sqrt(2.0 / math.pi) * (x + 0.044715 * (x * x * x))))


def _expm1(x):
    series = x * (1.0 + x * (0.5 + x * (1.0 / 6.0 + x * (1.0 / 24.0))))
    return jnp.where(jnp.abs(x) < 0.03, series, jnp.exp(x) - 1.0)


def _rmsnorm(x, g):
    return x * lax.rsqrt(jnp.mean(x * x, axis=-1, keepdims=True) + NORM_EPS) * g


def _head_norm(o, gain, center):
    if center:
        o = o - jnp.mean(o, axis=-1, keepdims=True)
    return o * lax.rsqrt(jnp.mean(o * o, axis=-1, keepdims=True) + NORM_EPS) * gain


def _l2norm(x):
    return x * lax.rsqrt(jnp.sum(x * x, axis=-1, keepdims=True) + NORM_EPS)


def _iota2(shape, dim):
    return lax.broadcasted_iota(jnp.int32, shape, dim)


def _tri_ones(n, upper=False):
    i, j = _iota2((n, n), 0), _iota2((n, n), 1)
    return jnp.where((i <= j) if upper else (j <= i), 1.0, 0.0).astype(F32)


def _linscan(a, u, rev):
    n = a.shape[0]
    rows = _iota2(a.shape, 0)
    d = 1
    while d < n:
        if rev:
            valid = rows < n - d
            a_s, u_s = pltpu.roll(a, n - d, 0), pltpu.roll(u, n - d, 0)
        else:
            valid = rows >= d
            a_s, u_s = pltpu.roll(a, d, 0), pltpu.roll(u, d, 0)
        u = a * jnp.where(valid, u_s, 0.0) + u
        a = a * jnp.where(valid, a_s, 1.0)
        d *= 2
    return u


def _ret_chunk(mm, q, k, v, g, gain, state, cos, sin, dintra, qdec, kdec, cdec):
    dk = q.shape[-1]
    half = dk // 2

    def rot(t):
        t1, t2 = t[:, :half], t[:, half:]
        return jnp.concatenate([t1 * cos - t2 * sin, t1 * sin + t2 * cos], axis=-1)

    qr = rot(q)
    kr = rot(k) * (dk ** -0.5)
    scores = mm(qr, kr, tb=True) * dintra
    o = mm(scores, v) + mm(qr * qdec, state)
    new_state = state * cdec + mm(kr * kdec, v, ta=True)
    y = _head_norm(o, gain, True) * _silu(g)
    return y, new_state


def _gla_chunk(mm, q, k, v, r, glow, wgu, bias, gain, state_t):
    dk = q.shape[-1]
    c = q.shape[0]
    logit = mm(glow, wgu) + bias
    la = -_softplus(-logit) * (1.0 / GLA_TAU)
    cum = mm(_tri_ones(c), la, hi='l')
    rows = _iota2(la.shape, 0)
    ref = jnp.sum(jnp.where(rows < c // 2, la, 0.0), axis=0, keepdims=True)
    tot = jnp.sum(la, axis=0, keepdims=True)
    fwd, bwd = jnp.exp(cum - ref), jnp.exp(ref - cum)
    qs = q * (dk ** -0.5)
    s_lo = mm(qs * fwd, k * bwd, tb=True)
    s_up = mm(qs * bwd, k * fwd, tb=True)
    i, j = _iota2((c, c), 0), _iota2((c, c), 1)
    scores = jnp.where(i >= j, s_lo, s_up)
    o = mm(scores, v) + mm(qs * jnp.exp(cum), state_t, tb=True)
    k_end = k * jnp.exp(tot - cum)
    new_state_t = state_t * jnp.exp(tot) + mm(v, k_end, ta=True)
    y = _head_norm(o, gain, False) * _silu(r)
    return y, new_state_t


def _gdn_group(mm, q, k, v, z, ba, a_log, dt_bias, gain, state, head):
    g, c, dk = q.shape
    lanes = lax.broadcasted_iota(jnp.int32, (1, 1, ba.shape[-1]), 2)
    oh_b = jnp.where(lanes == head, 1.0, 0.0).astype(F32)
    oh_a = jnp.where(lanes == head + GDN_HEADS, 1.0, 0.0).astype(F32)
    beta = _sigmoid(jnp.sum(ba * oh_b, axis=-1, keepdims=True))
    a_logit = jnp.sum(ba * oh_a, axis=-1, keepdims=True)
    a_h = jnp.sum(a_log * oh_b[0], axis=-1, keepdims=True)
    dt_h = jnp.sum(dt_bias * oh_b[0], axis=-1, keepdims=True)
    la = -jnp.exp(a_h) * _softplus(a_logit + dt_h)
    q = _l2norm(q) * (dk ** -0.5)
    k = _l2norm(k)
    tri = jnp.broadcast_to(_tri_ones(c), (g, c, c))
    tri_up = jnp.broadcast_to(_tri_ones(c, upper=True), (g, c, c))
    cum_k = mm(tri, la * jnp.ones((g, c, dk), F32), hi='l')
    la_sq = la * jnp.ones((g, c, c), F32)
    cum_i = mm(tri, la_sq, hi='l')
    cum_j = mm(la_sq, tri_up, ta=True, hi='r')
    i, j = _iota2((c, c), 0), _iota2((c, c), 1)
    strict = i > j
    rel = jnp.where(strict, jnp.exp(jnp.where(strict, cum_i - cum_j, 0.0)), 0.0)
    a_mat = beta * rel * mm(k, k, tb=True)
    inv = jnp.where(i == j, 1.0, 0.0).astype(F32) - a_mat
    power = a_mat
    for _ in range(int(math.log2(c)) - 1):
        power = mm(power, power, hi=True)
        inv = inv + mm(inv, power, hi=True)
    tot = jnp.sum(la, axis=1, keepdims=True)
    u = mm(inv, beta * v, hi=True)
    w = mm(inv, (beta * jnp.exp(cum_k)) * k, hi=True)
    k_end = k * jnp.exp(tot - cum_k)
    di, dj = _iota2((dk, dk), 0), _iota2((dk, dk), 1)
    trans = jnp.exp(tot) * jnp.where(di == dj, 1.0, 0.0).astype(F32) - mm(k_end, w, ta=True)
    inject = mm(k_end, u, ta=True)
    ys = []
    for n in range(g):
        state = mm(trans[n], state) + inject[n]
        ys.append(_head_norm(mm(q[n], state), gain, False) * _silu(z[n]))
    return ys, state


def _conv_taps(shift, x, taps):
    out = None
    for tap, w in enumerate(taps):
        term = shift(x, CONV_WIDTH - 1 - tap, False) * w
        out = term if out is None else out + term
    return out


def _lru_pre(mm, shift, xb, yb, taps, cb, lam, wr, br, wi, bi):
    xb = _conv_taps(shift, xb, taps) + cb
    r = _sigmoid(mm(xb, wr) + br)
    i = _sigmoid(mm(xb, wi) + bi)
    log_a = (-LRU_C) * _softplus(-lam) * r
    a = jnp.exp(log_a)
    u = jnp.sqrt(-_expm1(2.0 * log_a)) * (i * xb)
    return a, u, _gelu_tanh(yb)


_TOKEN_SPEC = pl.BlockSpec((8, 128), lambda *_: (0, 0))


MATMUL_VMEM_BUDGET = 40 * 1024 * 1024
_TILE_SIZES = (3072, 2048, 1536, 1024, 768, 512, 384, 256, 128)


def _tile_choices(extent):
    return [t for t in _TILE_SIZES if t <= extent and extent % t == 0] or [extent]


def _matmul_tiles(m, n, kdim, n_unit, k_unit, a_item, b_item, o_item, e_item):
    best = None
    for bm in _tile_choices(m):
        for bn in _tile_choices(n_unit):
            for bk in _tile_choices(k_unit):
                nk = kdim // bk
                vmem = 2 * (bm * bk * a_item + bk * bn * b_item + bm * bn * (o_item + e_item)) + bm * bn * 4 * (2 if nk > 1 else 1)
                if vmem > MATMUL_VMEM_BUDGET:
                    continue
                steps = (m // bm) * (n // bn) * nk
                traffic = m * kdim * a_item * (1 if nk == 1 else n // bn) + kdim * n * b_item * (m // bm)
                if best is None or (steps, traffic) < best[0]:
                    best = ((steps, traffic), (bm, bn, bk))
    return best[1]


def _matmul(a, b, *, name, ta=False, tb=False, out_dtype=F32, b_sharded=False, o_sharded=False,
            epilogue=None, extra=None, token=None):
    m, kdim = (a.shape[1], a.shape[0]) if ta else a.shape
    if b_sharded:
        nq = b.shape[2]
        n = b.shape[1] if tb else N_CHIPS * nq
        assert kdim == (N_CHIPS * nq if tb else b.shape[1])
    else:
        n = b.shape[0] if tb else b.shape[1]
        assert kdim == (b.shape[1] if tb else b.shape[0])
    n_unit = b.shape[2] if (b_sharded and not tb) else (n // N_CHIPS if o_sharded else n)
    k_unit = b.shape[2] if (b_sharded and tb) else kdim
    bm, bn, bk = _matmul_tiles(m, n, kdim, n_unit, k_unit, a.dtype.itemsize, b.dtype.itemsize,
                               jnp.dtype(out_dtype).itemsize, 0 if extra is None else extra.dtype.itemsize)
    nk = kdim // bk
    grid = (m // bm, n // bn, nk)

    a_spec = pl.BlockSpec((bk, bm), lambda i, j, k: (k, i)) if ta else pl.BlockSpec((bm, bk), lambda i, j, k: (i, k))
    if b_sharded and not tb:
        per = b.shape[2] // bn
        b_spec = pl.BlockSpec((None, bk, bn), lambda i, j, k: (j // per, k, j % per))
    elif b_sharded:
        per = b.shape[2] // bk
        b_spec = pl.BlockSpec((None, bn, bk), lambda i, j, k: (k // per, j, k % per))
    elif tb:
        b_spec = pl.BlockSpec((bn, bk), lambda i, j, k: (j, k))
    else:
        b_spec = pl.BlockSpec((bk, bn), lambda i, j, k: (k, j))
    if o_sharded:
        per_o = (n // N_CHIPS) // bn
        o_spec = pl.BlockSpec((None, bm, bn), lambda i, j, k: (j // per_o, i, j % per_o))
        out_shape = jax.ShapeDtypeStruct((N_CHIPS, m, n // N_CHIPS), out_dtype)
    else:
        o_spec = pl.BlockSpec((bm, bn), lambda i, j, k: (i, j))
        out_shape = jax.ShapeDtypeStruct((m, n), out_dtype)
    in_specs = [a_spec, b_spec]
    operands = [a, b]
    if extra is not None:
        in_specs.append(pl.BlockSpec((bm, bn), lambda i, j, k: (i, j)))
        operands.append(extra)
    if token is not None:
        in_specs.append(_TOKEN_SPEC)
        operands.append(token)

    def finish(acc, e_ref, o_ref):
        if epilogue == 'add':
            acc = acc + e_ref[...].astype(F32)
        elif epilogue == 'relu2':
            acc = jnp.square(jnp.maximum(acc, 0.0))
        elif epilogue == 'dact':
            acc = acc * (2.0 * jnp.sqrt(e_ref[...].astype(F32)))
        o_ref[...] = acc.astype(o_ref.dtype)

    def body_one_step(*refs):
        finish(_raw_mm(refs[0][...], refs[1][...], ta, tb, False), refs[2] if extra is not None else None, refs[-1])

    def body(*refs):
        a_ref, b_ref = refs[0], refs[1]
        e_ref = refs[2] if extra is not None else None
        o_ref, acc_ref = refs[-2], refs[-1]
        k = pl.program_id(2)

        @pl.when(k == 0)
        def _():
            acc_ref[...] = jnp.zeros_like(acc_ref)

        acc_ref[...] += _raw_mm(a_ref[...], b_ref[...], ta, tb, False)

        @pl.when(k == nk - 1)
        def _():
            finish(acc_ref[...], e_ref, o_ref)

    return pl.pallas_call(
        body_one_step if nk == 1 else body, name=name, grid=grid, in_specs=in_specs, out_specs=o_spec,
        out_shape=out_shape, scratch_shapes=[] if nk == 1 else [pltpu.VMEM((bm, bn), F32)],
        compiler_params=_cparams(),
    )(*operands)


def _rmsnorm_fwd(x, g, name, token=None):
    s, d = x.shape

    def body(x_ref, g_ref, *rest):
        rest[-1][...] = _rmsnorm(x_ref[...], g_ref[...]).astype(BF)

    return pl.pallas_call(
        body, name=name, grid=(s // ROW_BLOCK,),
        in_specs=[pl.BlockSpec((ROW_BLOCK, d), lambda i: (i, 0)), pl.BlockSpec((1, d), lambda i: (0, 0))]
        + ([] if token is None else [_TOKEN_SPEC]),
        out_specs=pl.BlockSpec((ROW_BLOCK, d), lambda i: (i, 0)),
        out_shape=jax.ShapeDtypeStruct((s, d), BF), compiler_params=_cparams(),
    )(x, g.reshape(1, d), *([] if token is None else [token]))


def _rmsnorm_bwd(x, g, dh, dres, name):
    s, d = x.shape

    def body(x_ref, g_ref, dh_ref, dres_ref, dx_ref, dg_ref):
        _, vjp = jax.vjp(_rmsnorm, x_ref[...], g_ref[...])
        dx, dg = vjp(dh_ref[...].astype(F32))
        dx_ref[...] = dres_ref[...] + dx

        @pl.when(pl.program_id(0) == 0)
        def _():
            dg_ref[...] = jnp.zeros_like(dg_ref)

        dg_ref[...] += dg

    row = pl.BlockSpec((ROW_BLOCK, d), lambda i: (i, 0))
    vec = pl.BlockSpec((1, d), lambda i: (0, 0))
    dx, dg = pl.pallas_call(
        body, name=name, grid=(s // ROW_BLOCK,), in_specs=[row, vec, row, row], out_specs=[row, vec],
        out_shape=[jax.ShapeDtypeStruct((s, d), F32), jax.ShapeDtypeStruct((1, d), F32)],
        compiler_params=_cparams(),
    )(x, g.reshape(1, d), dh, dres)
    return dx, dg.reshape(d)


def _loss_head(x, g, target):
    s, d = x.shape

    def loss_fn(xv, gv, tv):
        err = _rmsnorm(xv, gv) - tv
        return 0.5 * jnp.sum(jnp.mean(err * err, axis=-1, keepdims=True), axis=0, keepdims=True)

    def body(x_ref, g_ref, t_ref, dx_ref, dg_ref, loss_ref):
        tv = t_ref[...]
        loss, vjp = jax.vjp(lambda xv, gv: loss_fn(xv, gv, tv), x_ref[...], g_ref[...])
        dx, dg = vjp(jnp.ones((1, 1), F32))
        dx_ref[...] = dx

        @pl.when(pl.program_id(0) == 0)
        def _():
            dg_ref[...] = jnp.zeros_like(dg_ref)
            loss_ref[...] = jnp.zeros_like(loss_ref)

        dg_ref[...] += dg
        loss_ref[...] += loss * jnp.ones_like(loss_ref)

    row = pl.BlockSpec((ROW_BLOCK, d), lambda i: (i, 0))
    vec = pl.BlockSpec((1, d), lambda i: (0, 0))
    dx, dg, loss = pl.pallas_call(
        body, name="loss_head", grid=(s // ROW_BLOCK,), in_specs=[row, vec, row],
        out_specs=[row, vec, pl.BlockSpec((1, 128), lambda i: (0, 0))],
        out_shape=[jax.ShapeDtypeStruct((s, d), F32), jax.ShapeDtypeStruct((1, d), F32),
                   jax.ShapeDtypeStruct((1, 128), F32)],
        compiler_params=_cparams(),
    )(x, g.reshape(1, d), target)
    return loss[0, 0], dx, dg.reshape(d)


def _ret_tables(s, dk):
    h = jnp.arange(RET_HEADS, dtype=F32)
    log_gamma = jnp.log1p(-jnp.exp2(-5.0 - h))
    pos = jnp.arange(CHUNK, dtype=F32)
    dist = jnp.abs(pos[:, None] - pos[None, :])
    dintra = jnp.exp(log_gamma[:, None, None] * dist)
    qdec = jnp.exp(log_gamma[:, None] * (pos + 1.0))[:, :, None]
    kdec = jnp.exp(log_gamma[:, None] * (CHUNK - 1.0 - pos))[:, :, None]
    cdec = jnp.exp(log_gamma * CHUNK)[:, None, None]
    inv = ROPE_BASE ** (-jnp.arange(0, dk, 2, dtype=F32) / dk)
    ang = jnp.arange(s, dtype=F32)[:, None] * inv[None, :]
    return jnp.cos(ang), jnp.sin(ang), dintra, qdec, kdec, cdec


def _ret_specs(s, dk, dv, tb, rev):
    nh = RET_HEADS
    nb = s // tb
    bi = (lambda b: nb - 1 - b) if rev else (lambda b: b)
    voff = 2 * nh * dk // dv
    cpb = tb // CHUNK
    return dict(
        q=pl.BlockSpec((tb, dk), lambda h, b: (bi(b), h)),
        k=pl.BlockSpec((tb, dk), lambda h, b: (bi(b), nh + h)),
        v=pl.BlockSpec((tb, dv), lambda h, b: (bi(b), voff + h)),
        g=pl.BlockSpec((tb, dv), lambda h, b: (bi(b), voff + nh + h)),
        gain=pl.BlockSpec((None, 1, dv), lambda h, b: (h, 0, 0)),
        cs=pl.BlockSpec((tb, dk // 2), lambda h, b: (bi(b), 0)),
        dintra=pl.BlockSpec((None, CHUNK, CHUNK), lambda h, b: (h, 0, 0)),
        dec=pl.BlockSpec((None, CHUNK, 1), lambda h, b: (h, 0, 0)),
        cdec=pl.BlockSpec((None, 1, 1), lambda h, b: (h, 0, 0)),
        hv=pl.BlockSpec((tb, dv), lambda h, b: (bi(b), h)),
        hk=pl.BlockSpec((tb, dk), lambda h, b: (bi(b), h)),
        st=pl.BlockSpec((None, cpb, dk, dv), lambda h, b: (h, bi(b), 0, 0)),
    )


def _ret_fwd(proj, gain, tables):
    s = proj.shape[0]
    d = proj.shape[1] // 6
    dk, dv = d // RET_HEADS, 2 * d // RET_HEADS
    tb = min(TOKEN_BLOCK, s)
    cpb = tb // CHUNK
    sp = _ret_specs(s, dk, dv, tb, False)
    cos, sin, dintra, qdec, kdec, cdec = tables

    def body(q_ref, k_ref, v_ref, g_ref, gain_ref, cos_ref, sin_ref, di_ref, qd_ref, kd_ref, cd_ref,
             y_ref, st_ref, state):
        @pl.when(pl.program_id(1) == 0)
        def _():
            state[...] = jnp.zeros_like(state)

        for c in range(cpb):
            sl = pl.ds(c * CHUNK, CHUNK)
            st_ref[c] = state[...]
            y, new_state = _ret_chunk(_mm_plain, q_ref[sl, :], k_ref[sl, :], v_ref[sl, :], g_ref[sl, :],
                                      gain_ref[...], state[...], cos_ref[sl, :], sin_ref[sl, :],
                                      di_ref[...], qd_ref[...], kd_ref[...], cd_ref[...])
            y_ref[sl, :] = y.astype(BF)
            state[...] = new_state

    return pl.pallas_call(
        body, name="ret_fwd", grid=(RET_HEADS, s // tb),
        in_specs=[sp['q'], sp['k'], sp['v'], sp['g'], sp['gain'], sp['cs'], sp['cs'], sp['dintra'], sp['dec'],
                  sp['dec'], sp['cdec']],
        out_specs=[sp['hv'], sp['st']],
        out_shape=[jax.ShapeDtypeStruct((s, RET_HEADS * dv), BF),
                   jax.ShapeDtypeStruct((RET_HEADS, s // CHUNK, dk, dv), F32)],
        scratch_shapes=[pltpu.VMEM((dk, dv), F32)], compiler_params=_cparams(),
    )(proj, proj, proj, proj, gain.reshape(RET_HEADS, 1, dv), cos, sin, dintra, qdec, kdec, cdec)


def _ret_bwd(proj, gain, tables, states, dy):
    s = proj.shape[0]
    d = proj.shape[1] // 6
    dk, dv = d // RET_HEADS, 2 * d // RET_HEADS
    tb = min(TOKEN_BLOCK, s)
    cpb = tb // CHUNK
    sp = _ret_specs(s, dk, dv, tb, True)
    cos, sin, dintra, qdec, kdec, cdec = tables

    def body(q_ref, k_ref, v_ref, g_ref, gain_ref, cos_ref, sin_ref, di_ref, qd_ref, kd_ref, cd_ref, st_ref,
             dy_ref, dq_ref, dk_ref, dv_ref, dg_ref, dgain_ref, dstate):
        @pl.when(pl.program_id(1) == 0)
        def _():
            dstate[...] = jnp.zeros_like(dstate)
            dgain_ref[...] = jnp.zeros_like(dgain_ref)

        for c in reversed(range(cpb)):
            sl = pl.ds(c * CHUNK, CHUNK)
            cos_c, sin_c = cos_ref[sl, :], sin_ref[sl, :]
            di, qd, kd, cd = di_ref[...], qd_ref[...], kd_ref[...], cd_ref[...]

            def fn(q, k, v, g, gn, st):
                return _ret_chunk(_mm_diff, q, k, v, g, gn, st, cos_c, sin_c, di, qd, kd, cd)

            _, vjp = jax.vjp(fn, q_ref[sl, :], k_ref[sl, :], v_ref[sl, :], g_ref[sl, :], gain_ref[...],
                             st_ref[c])
            dq, dkk, dvv, dg, dgn, dst = vjp((dy_ref[sl, :].astype(F32), dstate[...]))
            dq_ref[sl, :] = dq.astype(BF)
            dk_ref[sl, :] = dkk.astype(BF)
            dv_ref[sl, :] = dvv.astype(BF)
            dg_ref[sl, :] = dg.astype(BF)
            dgain_ref[...] += dgn
            dstate[...] = dst

    dq, dkk, dvv, dg, dgain = pl.pallas_call(
        body, name="ret_bwd", grid=(RET_HEADS, s // tb),
        in_specs=[sp['q'], sp['k'], sp['v'], sp['g'], sp['gain'], sp['cs'], sp['cs'], sp['dintra'], sp['dec'],
                  sp['dec'], sp['cdec'], sp['st'], sp['hv']],
        out_specs=[sp['hk'], sp['hk'], sp['hv'], sp['hv'], sp['gain']],
        out_shape=[jax.ShapeDtypeStruct((s, RET_HEADS * dk), BF), jax.ShapeDtypeStruct((s, RET_HEADS * dk), BF),
                   jax.ShapeDtypeStruct((s, RET_HEADS * dv), BF), jax.ShapeDtypeStruct((s, RET_HEADS * dv), BF),
                   jax.ShapeDtypeStruct((RET_HEADS, 1, dv), F32)],
        scratch_shapes=[pltpu.VMEM((dk, dv), F32)], compiler_params=_cparams(),
    )(proj, proj, proj, proj, gain.reshape(RET_HEADS, 1, dv), cos, sin, dintra, qdec, kdec, cdec, states, dy)
    return jnp.concatenate([dq, dkk, dvv, dg], axis=1), dgain.reshape(RET_HEADS, dv)


def _gla_specs(s, dk, dv, tb, rev):
    nh = GLA_HEADS
    nb = s // tb
    bi = (lambda b: nb - 1 - b) if rev else (lambda b: b)
    voff = 2 * nh * dk // dv
    cpb = tb // CHUNK
    return dict(
        q=pl.BlockSpec((tb, dk), lambda h, b: (bi(b), h)),
        k=pl.BlockSpec((tb, dk), lambda h, b: (bi(b), nh + h)),
        v=pl.BlockSpec((tb, dv), lambda h, b: (bi(b), voff + h)),
        r=pl.BlockSpec((tb, dv), lambda h, b: (bi(b), voff + nh + h)),
        glow=pl.BlockSpec((tb, 128), lambda h, b: (bi(b), 0)),
        wgu=pl.BlockSpec((128, dk), lambda h, b: (0, h)),
        bias=pl.BlockSpec((1, dk), lambda h, b: (0, h)),
        gain=pl.BlockSpec((None, 1, dv), lambda h, b: (h, 0, 0)),
        hv=pl.BlockSpec((tb, dv), lambda h, b: (bi(b), h)),
        hk=pl.BlockSpec((tb, dk), lambda h, b: (bi(b), h)),
        hg=pl.BlockSpec((tb, 128), lambda h, b: (bi(b), h)),
        st=pl.BlockSpec((None, cpb, dv, dk), lambda h, b: (h, bi(b), 0, 0)),
    )


def _gla_fwd(proj, glow, wgu, bias, gain):
    s = proj.shape[0]
    d = proj.shape[1] // 3
    dk, dv = d // 2 // GLA_HEADS, d // GLA_HEADS
    tb = min(TOKEN_BLOCK, s)
    cpb = tb // CHUNK
    sp = _gla_specs(s, dk, dv, tb, False)

    def body(q_ref, k_ref, v_ref, r_ref, gl_ref, wgu_ref, b_ref, gain_ref, y_ref, st_ref, state):
        @pl.when(pl.program_id(1) == 0)
        def _():
            state[...] = jnp.zeros_like(state)

        for c in range(cpb):
            sl = pl.ds(c * CHUNK, CHUNK)
            st_ref[c] = state[...]
            y, new_state = _gla_chunk(_mm_plain, q_ref[sl, :], k_ref[sl, :], v_ref[sl, :], r_ref[sl, :],
                                      gl_ref[sl, :], wgu_ref[...], b_ref[...], gain_ref[...], state[...])
            y_ref[sl, :] = y.astype(BF)
            state[...] = new_state

    return pl.pallas_call(
        body, name="gla_fwd", grid=(GLA_HEADS, s // tb),
        in_specs=[sp['q'], sp['k'], sp['v'], sp['r'], sp['glow'], sp['wgu'], sp['bias'], sp['gain']],
        out_specs=[sp['hv'], sp['st']],
        out_shape=[jax.ShapeDtypeStruct((s, GLA_HEADS * dv), BF),
                   jax.ShapeDtypeStruct((GLA_HEADS, s // CHUNK, dv, dk), F32)],
        scratch_shapes=[pltpu.VMEM((dv, dk), F32)], compiler_params=_cparams(),
    )(proj, proj, proj, proj, glow, wgu, bias, gain.reshape(GLA_HEADS, 1, dv))


def _gla_bwd(proj, glow, wgu, bias, gain, states, dy):
    s = proj.shape[0]
    d = proj.shape[1] // 3
    dk, dv = d // 2 // GLA_HEADS, d // GLA_HEADS
    tb = min(TOKEN_BLOCK, s)
    cpb = tb // CHUNK
    sp = _gla_specs(s, dk, dv, tb, True)

    def body(q_ref, k_ref, v_ref, r_ref, gl_ref, wgu_ref, b_ref, gain_ref, st_ref, dy_ref,
             dq_ref, dk_ref, dv_ref, dr_ref, dgl_ref, dwgu_ref, db_ref, dgain_ref, dstate):
        @pl.when(pl.program_id(1) == 0)
        def _():
            dstate[...] = jnp.zeros_like(dstate)
            dwgu_ref[...] = jnp.zeros_like(dwgu_ref)
            db_ref[...] = jnp.zeros_like(db_ref)
            dgain_ref[...] = jnp.zeros_like(dgain_ref)

        for c in reversed(range(cpb)):
            sl = pl.ds(c * CHUNK, CHUNK)

            def fn(q, k, v, r, gl, w, b, gn, st):
                return _gla_chunk(_mm_diff, q, k, v, r, gl, w, b, gn, st)

            _, vjp = jax.vjp(fn, q_ref[sl, :], k_ref[sl, :], v_ref[sl, :], r_ref[sl, :], gl_ref[sl, :],
                             wgu_ref[...], b_ref[...], gain_ref[...], st_ref[c])
            dq, dkk, dvv, dr, dgl, dw, db, dgn, dst = vjp((dy_ref[sl, :].astype(F32), dstate[...]))
            dq_ref[sl, :] = dq.astype(BF)
            dk_ref[sl, :] = dkk.astype(BF)
            dv_ref[sl, :] = dvv.astype(BF)
            dr_ref[sl, :] = dr.astype(BF)
            dgl_ref[sl, :] = dgl.astype(BF)
            dwgu_ref[...] += dw
            db_ref[...] += db
            dgain_ref[...] += dgn
            dstate[...] = dst

    nh = GLA_HEADS
    dq, dkk, dvv, dr, dgl, dwgu, db, dgain = pl.pallas_call(
        body, name="gla_bwd", grid=(nh, s // tb),
        in_specs=[sp['q'], sp['k'], sp['v'], sp['r'], sp['glow'], sp['wgu'], sp['bias'], sp['gain'], sp['st'],
                  sp['hv']],
        out_specs=[sp['hk'], sp['hk'], sp['hv'], sp['hv'], sp['hg'], sp['wgu'], sp['bias'], sp['gain']],
        out_shape=[jax.ShapeDtypeStruct((s, nh * dk), BF), jax.ShapeDtypeStruct((s, nh * dk), BF),
                   jax.ShapeDtypeStruct((s, nh * dv), BF), jax.ShapeDtypeStruct((s, nh * dv), BF),
                   jax.ShapeDtypeStruct((s, nh * 128), BF), jax.ShapeDtypeStruct((128, nh * dk), F32),
                   jax.ShapeDtypeStruct((1, nh * dk), F32), jax.ShapeDtypeStruct((nh, 1, dv), F32)],
        scratch_shapes=[pltpu.VMEM((dv, dk), F32)], compiler_params=_cparams(),
    )(proj, proj, proj, proj, glow, wgu, bias, gain.reshape(nh, 1, dv), states, dy)
    return jnp.concatenate([dq, dkk, dvv, dr], axis=1), dgl, dwgu, db, dgain.reshape(nh, dv)


def _gdn_specs(s, dk):
    nh = GDN_HEADS
    col = lambda off: pl.BlockSpec((s, dk), lambda h: (0, off + h))
    tap = lambda off: pl.BlockSpec((CONV_WIDTH, dk), lambda h: (0, off + h))
    vec = pl.BlockSpec((1, 128), lambda h: (0, 0))
    return dict(q=col(0), k=col(nh), v=col(2 * nh), z=col(3 * nh), ba=pl.BlockSpec((s, 128), lambda h: (0, 0)),
                cq=tap(0), ck=tap(nh), cv=tap(2 * nh), vec=vec, gain=pl.BlockSpec((1, dk), lambda h: (0, 0)),
                head=col(0))


def _conv_silu(shift, x, taps):
    return _silu(_conv_taps(shift, x, taps))


def _load_taps(ref):
    return [ref[t:t + 1, :] for t in range(CONV_WIDTH)]


def _gdn_fwd(proj, ba, conv_w, a_log, dt_bias, gain):
    s = proj.shape[0]
    dk = proj.shape[1] // (4 * GDN_HEADS)
    sp = _gdn_specs(s, dk)
    rows = GDN_GROUP * CHUNK

    def grp(t):
        return t.reshape(GDN_GROUP, CHUNK, t.shape[-1])

    ngroups = s // rows

    def body(q_ref, k_ref, v_ref, z_ref, ba_ref, cq_ref, ck_ref, cv_ref, al_ref, dt_ref, gain_ref, y_ref, states,
             qc, kc, vc, state):
        head = pl.program_id(0)
        qc[...] = _conv_silu(_shift_rows, q_ref[...], _load_taps(cq_ref))
        kc[...] = _conv_silu(_shift_rows, k_ref[...], _load_taps(ck_ref))
        vc[...] = _conv_silu(_shift_rows, v_ref[...], _load_taps(cv_ref))
        state[...] = jnp.zeros_like(state)

        def step(n, carry):
            base = pl.multiple_of(n * rows, rows)
            sl = pl.ds(base, rows)
            states[n] = state[...]
            ys, new_state = _gdn_group(_mm_plain, grp(qc[sl, :]), grp(kc[sl, :]), grp(vc[sl, :]), grp(z_ref[sl, :]),
                                       grp(ba_ref[sl, :]), al_ref[...], dt_ref[...], gain_ref[...], state[...], head)
            for i, y in enumerate(ys):
                y_ref[pl.ds(base + i * CHUNK, CHUNK), :] = y.astype(BF)
            state[...] = new_state
            return carry

        lax.fori_loop(0, ngroups, step, 0)

    return pl.pallas_call(
        body, name="gdn_fwd", grid=(GDN_HEADS,),
        in_specs=[sp['q'], sp['k'], sp['v'], sp['z'], sp['ba'], sp['cq'], sp['ck'], sp['cv'], sp['vec'], sp['vec'],
                  sp['gain']],
        out_specs=[sp['head'], pl.BlockSpec((None, ngroups, dk, dk), lambda h: (h, 0, 0, 0))],
        out_shape=[jax.ShapeDtypeStruct((s, GDN_HEADS * dk), BF),
                   jax.ShapeDtypeStruct((GDN_HEADS, ngroups, dk, dk), F32)],
        scratch_shapes=[pltpu.VMEM((s, dk), F32)] * 3 + [pltpu.VMEM((dk, dk), F32)],
        compiler_params=_cparams(),
    )(proj, proj, proj, proj, ba, conv_w, conv_w, conv_w, a_log, dt_bias, gain)


def _gdn_bwd(proj, ba, conv_w, a_log, dt_bias, gain, group_states, dy):
    s = proj.shape[0]
    dk = proj.shape[1] // (4 * GDN_HEADS)
    sp = _gdn_specs(s, dk)
    rows = GDN_GROUP * CHUNK
    ngroups = s // rows

    def grp(t):
        return t.reshape(GDN_GROUP, CHUNK, t.shape[-1])

    def body(q_ref, k_ref, v_ref, z_ref, ba_ref, cq_ref, ck_ref, cv_ref, al_ref, dt_ref, gain_ref, states, dy_ref,
             dq_ref, dk_ref, dv_ref, dz_ref, dba_ref, dcq_ref, dck_ref, dcv_ref, dal_ref, ddt_ref, dgain_ref,
             qc, kc, vc, dqc, dkc, dvc, dstate):
        head = pl.program_id(0)

        @pl.when(head == 0)
        def _():
            dba_ref[...] = jnp.zeros_like(dba_ref)
            dal_ref[...] = jnp.zeros_like(dal_ref)
            ddt_ref[...] = jnp.zeros_like(ddt_ref)
            dgain_ref[...] = jnp.zeros_like(dgain_ref)

        qc[...] = _conv_silu(_shift_rows, q_ref[...], _load_taps(cq_ref))
        kc[...] = _conv_silu(_shift_rows, k_ref[...], _load_taps(ck_ref))
        vc[...] = _conv_silu(_shift_rows, v_ref[...], _load_taps(cv_ref))
        dstate[...] = jnp.zeros_like(dstate)

        def bstep(i, carry):
            n = ngroups - 1 - i
            base = pl.multiple_of(n * rows, rows)
            sl = pl.ds(base, rows)

            def fn(q, k, v, z, b, al, dt, gn, st):
                return _gdn_group(_mm_diff, q, k, v, z, b, al, dt, gn, st, head)

            _, vjp = jax.vjp(fn, grp(qc[sl, :]), grp(kc[sl, :]), grp(vc[sl, :]), grp(z_ref[sl, :]),
                             grp(ba_ref[sl, :]), al_ref[...], dt_ref[...], gain_ref[...], states[n])
            dys = [dy_ref[pl.ds(base + j * CHUNK, CHUNK), :].astype(F32) for j in range(GDN_GROUP)]
            dq, dkk, dvv, dz, db, dal, ddt, dgn, dst = vjp((dys, dstate[...]))
            dqc[sl, :] = dq.reshape(rows, dk)
            dkc[sl, :] = dkk.reshape(rows, dk)
            dvc[sl, :] = dvv.reshape(rows, dk)
            dz_ref[sl, :] = dz.reshape(rows, dk).astype(BF)
            dba_ref[sl, :] += db.reshape(rows, db.shape[-1])
            dal_ref[...] += dal
            ddt_ref[...] += ddt
            dgain_ref[...] += dgn
            dstate[...] = dst
            return carry

        lax.fori_loop(0, ngroups, bstep, 0)

        for x_ref, c_ref, dpost, dx_ref, dc_ref in ((q_ref, cq_ref, dqc, dq_ref, dcq_ref),
                                                    (k_ref, ck_ref, dkc, dk_ref, dck_ref),
                                                    (v_ref, cv_ref, dvc, dv_ref, dcv_ref)):
            _, vjp = jax.vjp(lambda x, *taps: _conv_silu(_shift_vjp, x, taps), x_ref[...], *_load_taps(c_ref))
            grads = vjp(dpost[...])
            dx_ref[...] = grads[0].astype(BF)
            for t in range(CONV_WIDTH):
                dc_ref[t:t + 1, :] = grads[1 + t]

    nh = GDN_HEADS
    col_bf = jax.ShapeDtypeStruct((s, nh * dk), BF)
    tap_out = jax.ShapeDtypeStruct((CONV_WIDTH, nh * dk), F32)
    tap_spec = pl.BlockSpec((CONV_WIDTH, dk), lambda h: (0, h))
    dq, dkk, dvv, dz, dba, dcq, dck, dcv, dal, ddt, dgain = pl.pallas_call(
        body, name="gdn_bwd", grid=(nh,),
        in_specs=[sp['q'], sp['k'], sp['v'], sp['z'], sp['ba'], sp['cq'], sp['ck'], sp['cv'], sp['vec'], sp['vec'],
                  sp['gain'], pl.BlockSpec((None, ngroups, dk, dk), lambda h: (h, 0, 0, 0)), sp['head']],
        out_specs=[sp['head']] * 4 + [sp['ba'], tap_spec, tap_spec, tap_spec, sp['vec'], sp['vec'], sp['gain']],
        out_shape=[col_bf] * 4 + [jax.ShapeDtypeStruct((s, 128), F32), tap_out, tap_out, tap_out,
                                  jax.ShapeDtypeStruct((1, 128), F32), jax.ShapeDtypeStruct((1, 128), F32),
                                  jax.ShapeDtypeStruct((1, dk), F32)],
        scratch_shapes=[pltpu.VMEM((s, dk), F32)] * 6 + [pltpu.VMEM((dk, dk), F32)],
        compiler_params=_cparams(),
    )(proj, proj, proj, proj, ba, conv_w, conv_w, conv_w, a_log, dt_bias, gain, group_states, dy)
    return (jnp.concatenate([dq, dkk, dvv, dz], axis=1), dba, jnp.concatenate([dcq, dck, dcv], axis=1), dal, ddt,
            dgain)


def _lru_specs(s, bw):
    nb = LRU_BLOCKS
    return dict(
        xb=pl.BlockSpec((s, bw), lambda n: (0, n)), yb=pl.BlockSpec((s, bw), lambda n: (0, nb + n)),
        taps=pl.BlockSpec((CONV_WIDTH, bw), lambda n: (0, n)), vec=pl.BlockSpec((1, bw), lambda n: (0, n)),
        w=pl.BlockSpec((None, bw, bw), lambda n: (n, 0, 0)), b=pl.BlockSpec((None, 1, bw), lambda n: (n, 0, 0)),
        col=pl.BlockSpec((s, bw), lambda n: (0, n)))


def _lru_fwd(proj, conv_w, conv_b, lam, wr, br, wi, bi):
    s = proj.shape[0]
    bw = proj.shape[1] // (2 * LRU_BLOCKS)
    sp = _lru_specs(s, bw)

    def body(xb_ref, yb_ref, cw_ref, cb_ref, lam_ref, wr_ref, br_ref, wi_ref, bi_ref, y_ref):
        a, u, gy = _lru_pre(_mm_plain, _shift_rows, xb_ref[...], yb_ref[...], _load_taps(cw_ref), cb_ref[...],
                            lam_ref[...], wr_ref[...], br_ref[...], wi_ref[...], bi_ref[...])
        y_ref[...] = (_linscan(a, u, False) * gy).astype(BF)

    return pl.pallas_call(
        body, name="lru_fwd", grid=(LRU_BLOCKS,),
        in_specs=[sp['xb'], sp['yb'], sp['taps'], sp['vec'], sp['vec'], sp['w'], sp['b'], sp['w'], sp['b']],
        out_specs=sp['col'], out_shape=jax.ShapeDtypeStruct((s, LRU_BLOCKS * bw), BF),
        compiler_params=_cparams(),
    )(proj, proj, conv_w, conv_b, lam, wr, br, wi, bi)


def _lru_bwd(proj, conv_w, conv_b, lam, wr, br, wi, bi, dy):
    s = proj.shape[0]
    nb = LRU_BLOCKS
    bw = proj.shape[1] // (2 * nb)
    sp = _lru_specs(s, bw)

    def body(xb_ref, yb_ref, cw_ref, cb_ref, lam_ref, wr_ref, br_ref, wi_ref, bi_ref, dy_ref,
             dxb_ref, dyb_ref, dcw_ref, dcb_ref, dlam_ref, dwr_ref, dbr_ref, dwi_ref, dbi_ref):
        def pre(xb, yb, t0, t1, t2, t3, cb, lm, w_r, b_r, w_i, b_i):
            return _lru_pre(_mm_diff, _shift_vjp, xb, yb, (t0, t1, t2, t3), cb, lm, w_r, b_r, w_i, b_i)

        (a, u, gy), vjp = jax.vjp(pre, xb_ref[...], yb_ref[...], *_load_taps(cw_ref), cb_ref[...], lam_ref[...],
                                  wr_ref[...], br_ref[...], wi_ref[...], bi_ref[...])
        h = _linscan(a, u, False)
        dout = dy_ref[...].astype(F32)
        g = _linscan(_shift_rows(a, 1, True), dout * gy, True)
        grads = vjp((g * _shift_rows(h, 1, False), g, dout * h))
        dxb_ref[...] = grads[0].astype(BF)
        dyb_ref[...] = grads[1].astype(BF)
        for t in range(CONV_WIDTH):
            dcw_ref[t:t + 1, :] = grads[2 + t]
        dcb_ref[...] = grads[6]
        dlam_ref[...] = grads[7]
        dwr_ref[...] = grads[8]
        dbr_ref[...] = grads[9]
        dwi_ref[...] = grads[10]
        dbi_ref[...] = grads[11]

    outs = pl.pallas_call(
        body, name="lru_bwd", grid=(nb,),
        in_specs=[sp['xb'], sp['yb'], sp['taps'], sp['vec'], sp['vec'], sp['w'], sp['b'], sp['w'], sp['b'], sp['col']],
        out_specs=[sp['col'], sp['col'], sp['taps'], sp['vec'], sp['vec'], sp['w'], sp['b'], sp['w'], sp['b']],
        out_shape=[jax.ShapeDtypeStruct((s, nb * bw), BF), jax.ShapeDtypeStruct((s, nb * bw), BF),
                   jax.ShapeDtypeStruct((CONV_WIDTH, nb * bw), F32), jax.ShapeDtypeStruct((1, nb * bw), F32),
                   jax.ShapeDtypeStruct((1, nb * bw), F32), jax.ShapeDtypeStruct((nb, bw, bw), F32),
                   jax.ShapeDtypeStruct((nb, 1, bw), F32), jax.ShapeDtypeStruct((nb, bw, bw), F32),
                   jax.ShapeDtypeStruct((nb, 1, bw), F32)],
        compiler_params=_cparams(),
    )(proj, proj, conv_w, conv_b, lam, wr, br, wi, bi, dy)
    return (jnp.concatenate([outs[0], outs[1]], axis=1),) + tuple(outs[2:])


def _row_tile(rows, cols, bytes_per_row_set):
    t = 8
    while t * 2 <= rows and rows % (t * 2) == 0 and (t * 2) * cols * bytes_per_row_set <= 4 * 1024 * 1024:
        t *= 2
    return t


def _sum_slabs(a, out_dtype, name):
    n, rows, cols = a.shape
    tr = _row_tile(rows, cols, 4 * (n + 1))

    def body(*refs):
        acc = refs[0][...].astype(F32)
        for r in refs[1:n]:
            acc = acc + r[...].astype(F32)
        refs[n][...] = acc.astype(out_dtype)

    specs = [pl.BlockSpec((None, tr, cols), functools.partial(lambda i, k: (k, i, 0), k=k)) for k in range(n)]
    return pl.pallas_call(
        body, name=name, grid=(rows // tr,), in_specs=specs, out_specs=pl.BlockSpec((tr, cols), lambda i: (i, 0)),
        out_shape=jax.ShapeDtypeStruct((rows, cols), out_dtype), compiler_params=_cparams(),
    )(*([a] * n))


def _sum_keep_and_landed(g, landed, core, name):
    four, hr, cols = landed.shape
    tr = _row_tile(hr, cols, 2 * 3)
    tph = hr // tr

    def body(core_ref, g_ref, l_ref, o_ref):
        o_ref[...] = (g_ref[...].astype(F32) + l_ref[...].astype(F32)).astype(o_ref.dtype)

    grid_spec = pltpu.PrefetchScalarGridSpec(
        num_scalar_prefetch=1, grid=(four, tph),
        in_specs=[pl.BlockSpec((None, tr, cols), lambda j, r, core_ref: (j, core_ref[0] * tph + r, 0)),
                  pl.BlockSpec((None, tr, cols), lambda j, r, core_ref: (j, r, 0))],
        out_specs=pl.BlockSpec((None, tr, cols), lambda j, r, core_ref: (j, r, 0)))
    return pl.pallas_call(body, name=name, grid_spec=grid_spec, out_shape=jax.ShapeDtypeStruct(landed.shape, g.dtype),
                          compiler_params=_cparams())(core, g, landed)


def _sum_chip_parts(parts, landed, chip, name):
    _, rows, cols = parts.shape
    tr = _row_tile(rows, cols, 2 * 4 + 4)

    def body(chip_ref, p_ref, l0_ref, l1_ref, l2_ref, o_ref):
        acc = l0_ref[...].astype(F32) + l1_ref[...].astype(F32)
        o_ref[...] = (acc + l2_ref[...].astype(F32)) + p_ref[...].astype(F32)

    slab = lambda k: pl.BlockSpec((None, tr, cols), lambda i, chip_ref: (k, i, 0))
    grid_spec = pltpu.PrefetchScalarGridSpec(
        num_scalar_prefetch=1, grid=(rows // tr,),
        in_specs=[pl.BlockSpec((None, tr, cols), lambda i, chip_ref: (chip_ref[0], i, 0)), slab(0), slab(1), slab(2)],
        out_specs=pl.BlockSpec((tr, cols), lambda i, chip_ref: (i, 0)))
    return pl.pallas_call(body, name=name, grid_spec=grid_spec, out_shape=jax.ShapeDtypeStruct((rows, cols), F32),
                          compiler_params=_cparams())(chip, parts, landed, landed, landed)


_ADAM_C1 = 1.0 / (1.0 - ADAM_B1 ** ADAM_STEP)
_ADAM_C2 = 1.0 / (1.0 - ADAM_B2 ** ADAM_STEP)


def _adamw_math(w, g, m, v):
    nm = ADAM_B1 * m + (1.0 - ADAM_B1) * g
    nv = ADAM_B2 * v + (1.0 - ADAM_B2) * (g * g)
    den = jnp.sqrt(nv * _ADAM_C2) + ADAM_EPS
    inv = pl.reciprocal(den, approx=True)
    inv = inv * (2.0 - den * inv)
    delta = -ADAM_LR * ((nm * _ADAM_C1) * inv + ADAM_WD * w)
    return delta, nm, nv


def _adamw(w, g, m, v, name):
    rows, cols = w.shape
    tr = _row_tile(rows, cols, 4 * 7)

    def body(w_ref, g_ref, m_ref, v_ref, d_ref, nm_ref, nv_ref):
        d_ref[...], nm_ref[...], nv_ref[...] = _adamw_math(w_ref[...], g_ref[...], m_ref[...], v_ref[...])

    spec = pl.BlockSpec((tr, cols), lambda i: (i, 0))
    shape = jax.ShapeDtypeStruct((rows, cols), F32)
    return pl.pallas_call(
        body, name=name, grid=(rows // tr,), in_specs=[spec] * 4, out_specs=[spec] * 3, out_shape=[shape] * 3,
        compiler_params=_cparams(),
    )(w, g, m, v)


def _adamw_halves(w, m, v, layer, mine, theirs, core, name, carried=None):
    n_layers, rows, cols = w.shape
    hr = mine.shape[0]
    tr = _row_tile(hr, cols, 4 * 9)
    tph = hr // tr

    def body(core_ref, w_ref, q_ref, t_ref, m_ref, v_ref, *rest):
        g_ref, d_ref, nm_ref, nv_ref = rest[-4:]
        is_mine = (pl.program_id(0) // tph) == core_ref[0]
        g = jnp.where(is_mine, q_ref[...], t_ref[...])
        g_ref[...] = g
        d_ref[...], nm_ref[...], nv_ref[...] = _adamw_math(w_ref[...], g, m_ref[...], v_ref[...])

    slab = pl.BlockSpec((None, tr, cols), lambda i, core_ref: (layer, i, 0))
    half = pl.BlockSpec((tr, cols), lambda i, core_ref: (i % tph, 0))
    in_specs = [slab, half, half, slab, slab]
    operands = [core, w, mine, theirs, m, v]
    aliases = {}
    if carried is not None:
        in_specs += [_ANY] * 4
        aliases = {len(operands) + k: k for k in range(4)}
        operands += list(carried)
    grid_spec = pltpu.PrefetchScalarGridSpec(num_scalar_prefetch=1, grid=(rows // tr,), in_specs=in_specs,
                                             out_specs=[slab] * 4)
    return pl.pallas_call(
        body, name=name, grid_spec=grid_spec, out_shape=[jax.ShapeDtypeStruct(w.shape, F32)] * 4,
        input_output_aliases=aliases, compiler_params=_cparams(),
    )(*operands)


_ANY = pl.BlockSpec(memory_space=pl.ANY)


def _mesh_pos():
    return lax.axis_index("x"), lax.axis_index("y"), lax.axis_index("c")


def _remote(src, dst, send_sems, recv_sems, k, dev):
    return pltpu.make_async_remote_copy(src_ref=src, dst_ref=dst, send_sem=send_sems.at[k], recv_sem=recv_sems.at[k],
                                        device_id=dev, device_id_type=MESH)


STREAM_CHUNK_BYTES = 1024 * 1024
STREAM_SLOTS = 3


def _chunk_rows(rows, cols, itemsize):
    assert rows % 16 == 0, rows
    t = 16
    while t * 2 <= rows and rows % (t * 2) == 0 and (t * 2) * cols * itemsize <= STREAM_CHUNK_BYTES:
        t *= 2
    return t


def _stream_chunks(n_chunks, src_at, dst_at, buf, load_sems, send_sems, recv_sem, sibling):
    def load(k, slot):
        return pltpu.make_async_copy(src_at(k), buf.at[slot], load_sems.at[slot])

    def send(k, slot):
        return pltpu.make_async_remote_copy(src_ref=buf.at[slot], dst_ref=dst_at(k), send_sem=send_sems.at[slot],
                                            recv_sem=recv_sem, device_id=sibling, device_id_type=MESH)

    def step(k, carry):
        slot = k % STREAM_SLOTS

        @pl.when(k >= STREAM_SLOTS)
        def _():
            send(k - STREAM_SLOTS, slot).wait_send()

        load(k, slot).start()

        @pl.when(k >= 1)
        def _():
            prev = (k - 1) % STREAM_SLOTS
            load(k - 1, prev).wait()
            send(k - 1, prev).start()

        return carry

    lax.fori_loop(0, n_chunks, step, 0)
    last = (n_chunks - 1) % STREAM_SLOTS
    load(n_chunks - 1, last).wait()
    send(n_chunks - 1, last).start()
    for k in range(max(0, n_chunks - STREAM_SLOTS), n_chunks):
        send(k, k % STREAM_SLOTS).wait_send()


_HBM = pl.BlockSpec(memory_space=pltpu.HBM)
_SEM = pl.BlockSpec(memory_space=pltpu.SEMAPHORE)
_DATAFLOW = pltpu.SideEffectType.DATAFLOW_SIDE_EFFECTING


def _chip_copies(kind, ins, lands, split, send_sems, recv_sems):
    x, y, c = _mesh_pos()
    me = 2 * x + y
    chips = [(1 - x, y), (x, 1 - y), (1 - x, 1 - y)]
    pairs = []
    for i in range(len(ins)):
        for j, chip in enumerate(chips):
            pj = 2 * chip[0] + chip[1]
            if kind == 'scatter':
                src, dst, got = ins[i].at[pj], lands[i].at[j], lands[i].at[j]
            elif split[i]:
                hr = ins[i].shape[0] // 2
                rows = pl.ds(c * hr, hr)
                src, dst, got = ins[i].at[rows], lands[i].at[me, rows], lands[i].at[pj, rows]
            else:
                src, dst, got = ins[i], lands[i].at[me], lands[i].at[pj]
            k = 3 * i + j
            pairs.append((_remote(src, dst, send_sems, recv_sems, k, (*chip, c)),
                          _remote(got, got, send_sems, recv_sems, k, (*chip, c))))
    return pairs


def _chip_exchange_start(kind, srcs, land_shapes, split, name, after):
    n = len(srcs)

    def body(*refs):
        ins, lands = refs[:n], refs[n:2 * n]
        send_sems, recv_sems = refs[2 * n + 1], refs[2 * n + 2]
        token = refs[-1]
        for send, _ in _chip_copies(kind, ins, lands, split, send_sems, recv_sems):
            send.start()
        token[...] = jnp.zeros_like(token)

    hbm = lambda t: pltpu.with_memory_space_constraint(t, pltpu.HBM)
    operands = [hbm(s) for s in srcs] + [hbm(lax.empty(shp, s.dtype)) for shp, s in zip(land_shapes, srcs)] + [after]
    out = pl.pallas_call(
        body, name=name, in_specs=[_HBM] * (2 * n) + [_ANY],
        out_specs=[_SEM, _SEM] + [_HBM] * (2 * n) + [pl.BlockSpec(memory_space=pltpu.VMEM)],
        out_shape=[pltpu.SemaphoreType.DMA((3 * n,)), pltpu.SemaphoreType.DMA((3 * n,))]
        + [pltpu.HBM(s.shape, s.dtype) for s in srcs] + [pltpu.HBM(shp, s.dtype) for shp, s in zip(land_shapes, srcs)]
        + [jax.ShapeDtypeStruct((8, 128), F32)],
        input_output_aliases={i: 2 + i for i in range(2 * n)},
        compiler_params=pltpu.CompilerParams(has_side_effects=_DATAFLOW),
    )(*operands)
    return out[0], out[1], out[2:2 + n], out[2 + n:2 + 2 * n], out[-1]


def _chip_exchange_wait(kind, started, split, name, after):
    send_sems, recv_sems, srcs, lands, _ = started
    n = len(srcs)

    def body(*refs):
        ins, land_refs = refs[:n], refs[n:2 * n]
        for send, arrived in _chip_copies(kind, ins, land_refs, split, refs[2 * n], refs[2 * n + 1]):
            send.wait_send()
            arrived.wait_recv()

    out = pl.pallas_call(
        body, name=name, in_specs=[_HBM] * (2 * n) + [_SEM, _SEM, _ANY], out_specs=[_HBM] * (2 * n),
        out_shape=[pltpu.HBM(s.shape, s.dtype) for s in srcs] + [pltpu.HBM(l.shape, l.dtype) for l in lands],
        input_output_aliases={i: i for i in range(2 * n)},
        compiler_params=pltpu.CompilerParams(has_side_effects=_DATAFLOW),
    )(*srcs, *lands, send_sems, recv_sems, after)
    return out[:n], out[n:]


def _pass_to_sibling(gathered, name):
    n = len(gathered)
    tr = [_chunk_rows(g.shape[1] // 2, g.shape[2], g.dtype.itemsize) for g in gathered]

    def body(*refs):
        outs = refs[n:2 * n]
        recv_sems, load_sems, send_sems = refs[2 * n:2 * n + 3]
        bufs = refs[2 * n + 3:]
        x, y, c = _mesh_pos()
        sibling = (x, y, 1 - c)
        chips = [(1 - x, y), (x, 1 - y), (1 - x, 1 - y)]
        for i in range(n):
            hr = outs[i].shape[1] // 2
            for j, chip in enumerate(chips):
                def rows_at(k, i=i, pj=2 * chip[0] + chip[1], hr=hr):
                    return outs[i].at[pj, pl.ds(c * hr + k * tr[i], tr[i])]

                _stream_chunks(hr // tr[i], rows_at, rows_at, bufs[i], load_sems, send_sems, recv_sems.at[3 * i + j],
                               sibling)
        for i in range(n):
            hr = outs[i].shape[1] // 2
            for j, chip in enumerate(chips):
                blk = outs[i].at[2 * chip[0] + chip[1], pl.ds((1 - c) * hr, hr)]
                pltpu.make_async_remote_copy(src_ref=blk, dst_ref=blk, send_sem=send_sems.at[0],
                                             recv_sem=recv_sems.at[3 * i + j], device_id=sibling,
                                             device_id_type=MESH).wait_recv()

    return pl.pallas_call(
        body, name=name, in_specs=[_ANY] * n, out_specs=[_ANY] * n,
        out_shape=[jax.ShapeDtypeStruct(g.shape, g.dtype) for g in gathered],
        input_output_aliases={i: i for i in range(n)},
        scratch_shapes=[pltpu.SemaphoreType.DMA((3 * n,)), pltpu.SemaphoreType.DMA((STREAM_SLOTS,)),
                        pltpu.SemaphoreType.DMA((STREAM_SLOTS,))]
        + [pltpu.VMEM((STREAM_SLOTS, tr[i], g.shape[2]), g.dtype) for i, g in enumerate(gathered)],
        compiler_params=_cparams(),
    )(*gathered)


def _stream_to_sibling(srcs, halved, name):
    n = len(srcs)
    geo = []
    for s in srcs:
        rows = s.shape[1] // 2 if halved else s.shape[1]
        geo.append((s.shape[0], rows, s.shape[2], _chunk_rows(rows, s.shape[2], s.dtype.itemsize)))

    def body(*refs):
        ins, outs = refs[:n], refs[n:2 * n]
        recv_sems, load_sems, send_sems = refs[2 * n:2 * n + 3]
        bufs = refs[2 * n + 3:]
        x, y, c = _mesh_pos()
        sibling = (x, y, 1 - c)
        for i in range(n):
            slabs, rows, _, tr = geo[i]
            per_slab = rows // tr
            off = (1 - c) * rows if halved else 0

            def src_at(k, i=i, per_slab=per_slab, tr=tr, off=off):
                return ins[i].at[k // per_slab, pl.ds(off + (k % per_slab) * tr, tr)]

            def dst_at(k, i=i, per_slab=per_slab, tr=tr):
                return outs[i].at[k // per_slab, pl.ds((k % per_slab) * tr, tr)]

            _stream_chunks(slabs * per_slab, src_at, dst_at, bufs[i], load_sems, send_sems, recv_sems.at[i], sibling)
        for i in range(n):
            pltpu.make_async_remote_copy(src_ref=outs[i], dst_ref=outs[i], send_sem=send_sems.at[0],
                                         recv_sem=recv_sems.at[i], device_id=sibling, device_id_type=MESH).wait_recv()

    return pl.pallas_call(
        body, name=name, in_specs=[_ANY] * n, out_specs=[_ANY] * n,
        out_shape=[jax.ShapeDtypeStruct((g[0], g[1], g[2]), s.dtype) for g, s in zip(geo, srcs)],
        scratch_shapes=[pltpu.SemaphoreType.DMA((n,)), pltpu.SemaphoreType.DMA((STREAM_SLOTS,)),
                        pltpu.SemaphoreType.DMA((STREAM_SLOTS,))]
        + [pltpu.VMEM((STREAM_SLOTS, g[3], g[2]), s.dtype) for g, s in zip(geo, srcs)],
        compiler_params=_cparams(),
    )(*srcs)


def _reduce_scatter_begin(grads, tag, core):
    landed = _stream_to_sibling(grads, True, f"rs_swap_{tag}")
    parts = [_sum_keep_and_landed(g, l, core, f"rs_add2_{tag}_{i}") for i, (g, l) in enumerate(zip(grads, landed))]
    return _chip_exchange_start('scatter', parts, [(3,) + p.shape[1:] for p in parts], None, f"rs_scatter_start_{tag}",
                                core)


def _reduce_scatter_finish(started, tag, chip, after):
    parts, landed = _chip_exchange_wait('scatter', started, None, f"rs_scatter_wait_{tag}", after)
    mine = [_sum_chip_parts(p, l, chip, f"rs_add4_{tag}_{i}") for i, (p, l) in enumerate(zip(parts, landed))]
    theirs = _stream_to_sibling([m[None] for m in mine], False, f"rs_join_{tag}")
    return [(m, t[0]) for m, t in zip(mine, theirs)]


def _pad_cols(a, width=128):
    return jnp.pad(a, ((0, 0), (0, width - a.shape[1])))


def _mixer_forward(kind, hn, w, tables):
    if kind == 0:
        proj = _matmul(hn, w['ret_w_in'], name="ret_proj", b_sharded=True)
        og, states = _ret_fwd(proj, w['ret_gn_gain'], tables)
        return og, (proj, states)
    if kind == 1:
        proj = _matmul(hn, w['gdn_w_main'], name="gdn_proj")
        ba = _matmul(hn, w['gdn_w_small'], name="gdn_proj_ba")
        og, states = _gdn_fwd(proj, ba, w['gdn_conv_w'], w['gdn_a_log'], w['gdn_dt_bias'], w['gdn_norm_gain'])
        return og, (proj, ba, states)
    if kind == 2:
        proj = _matmul(hn, w['gla_w_main'], name="gla_proj")
        glow = _matmul(hn, w['gla_w_small'], name="gla_proj_gate")
        og, states = _gla_fwd(proj, glow, w['gla_w_gate_up'], w['gla_gate_bias'], w['gla_norm_gain'])
        return og, (proj, glow, states)
    proj = _matmul(hn, w['lru_w_in'], name="lru_proj", b_sharded=True)
    og = _lru_fwd(proj, w['lru_conv_w'], w['lru_conv_b'], w['lru_lambda'], w['lru_w_rgate'], w['lru_b_rgate'],
                  w['lru_w_igate'], w['lru_b_igate'])
    return og, (proj,)


def _mixer_backward(kind, hn, w, tables, saved, d_og, grads):
    d = hn.shape[1]
    if kind == 0:
        proj, states = saved
        d_proj, grads['ret_gn_gain'] = _ret_bwd(proj, w['ret_gn_gain'], tables, states, d_og)
        grads['ret_w_in'] = _matmul(hn, d_proj, name="ret_dw_in", ta=True, out_dtype=BF, o_sharded=True)
        return _matmul(d_proj, w['ret_w_in'], name="ret_dhn", tb=True, b_sharded=True)
    if kind == 1:
        proj, ba, states = saved
        d_proj, d_ba, grads['gdn_conv_w'], grads['gdn_a_log'], grads['gdn_dt_bias'], grads['gdn_norm_gain'] = _gdn_bwd(
            proj, ba, w['gdn_conv_w'], w['gdn_a_log'], w['gdn_dt_bias'], w['gdn_norm_gain'], states, d_og)
        d_ba = d_ba.astype(BF)
        dw_main = _matmul(hn, d_proj, name="gdn_dw_main", ta=True, out_dtype=BF)
        dw_small = _matmul(hn, d_ba, name="gdn_dw_small", ta=True, out_dtype=BF)
        dw = jnp.concatenate([dw_main, dw_small[:, :2 * GDN_HEADS]], axis=1)
        grads['gdn_w_in'] = dw.reshape(d, N_CHIPS, dw.shape[1] // N_CHIPS).transpose(1, 0, 2)
        d_hn = _matmul(d_proj, w['gdn_w_main'], name="gdn_dhn_main", tb=True)
        return _matmul(d_ba, w['gdn_w_small'], name="gdn_dhn_small", tb=True, epilogue='add', extra=d_hn)
    if kind == 2:
        proj, glow, states = saved
        d_proj, d_glow4, d_wgu, grads['gla_gate_bias'], grads['gla_norm_gain'] = _gla_bwd(
            proj, glow, w['gla_w_gate_up'], w['gla_gate_bias'], w['gla_norm_gain'], states, d_og)
        grads['gla_w_gate_up'] = d_wgu[:GLA_RANK]
        dw_main = _matmul(hn, d_proj, name="gla_dw_main", ta=True, out_dtype=BF)
        dw_small4 = _matmul(hn, d_glow4, name="gla_dw_small", ta=True, out_dtype=F32)
        dw_small = dw_small4.reshape(d, GLA_HEADS, 128)[:, :, :GLA_RANK].sum(axis=1).astype(BF)
        dw = jnp.concatenate([dw_main, dw_small], axis=1)
        grads['gla_w_in'] = dw.reshape(d, N_CHIPS, dw.shape[1] // N_CHIPS).transpose(1, 0, 2)
        d_hn = _matmul(d_proj, w['gla_w_main'], name="gla_dhn_main", tb=True)
        w_small4 = jnp.tile(w['gla_w_small'], (1, GLA_HEADS))
        return _matmul(d_glow4, w_small4, name="gla_dhn_small", tb=True, epilogue='add', extra=d_hn)
    (proj,) = saved
    (d_proj, grads['lru_conv_w'], grads['lru_conv_b'], grads['lru_lambda'], grads['lru_w_rgate'], grads['lru_b_rgate'],
     grads['lru_w_igate'], grads['lru_b_igate']) = _lru_bwd(
        proj, w['lru_conv_w'], w['lru_conv_b'], w['lru_lambda'], w['lru_w_rgate'], w['lru_b_rgate'], w['lru_w_igate'],
        w['lru_b_igate'], d_og)
    grads['lru_w_in'] = _matmul(hn, d_proj, name="lru_dw_in", ta=True, out_dtype=BF, o_sharded=True)
    return _matmul(d_proj, w['lru_w_in'], name="lru_dhn", tb=True, b_sharded=True)


_W_OUT = ('ret_w_out', 'gdn_w_out', 'gla_w_out', 'lru_w_out')
_W_IN = ('ret_w_in', 'gdn_w_in', 'gla_w_in', 'lru_w_in')


def _layer_forward(layer, x, w, tables, token=None):
    hn = _rmsnorm_fwd(x, w['norm1'][layer], f"norm1_fwd_{layer}", token)
    og, mixer_saved = _mixer_forward(layer, hn, w, tables)
    x1 = _matmul(og, w[_W_OUT[layer]], name=f"mixer_out_{layer}", epilogue='add', extra=x)
    hn2 = _rmsnorm_fwd(x1, w['norm2'][layer], f"norm2_fwd_{layer}")
    act = _matmul(hn2, w['mlp_w_up'][layer], name="mlp_up", b_sharded=True, epilogue='relu2', out_dtype=BF)
    x2 = _matmul(act, w['mlp_w_down'][layer], name="mlp_down", epilogue='add', extra=x1)
    return x2, (x, hn, mixer_saved, og, x1, hn2, act)


def _layer_backward(layer, dx2, w, tables, saved, token=None):
    x, hn, mixer_saved, og, x1, hn2, act = saved
    d = x.shape[1]
    grads = {}
    d_up = _matmul(dx2, w['mlp_w_down'][layer], name="mlp_d_up", tb=True, epilogue='dact', extra=act, out_dtype=BF,
                   token=token)
    dw_down = _matmul(act, dx2, name="mlp_dw_down", ta=True, out_dtype=BF)
    grads['mlp_w_down'] = dw_down.reshape(N_CHIPS, dw_down.shape[0] // N_CHIPS, d)
    grads['mlp_w_up'] = _matmul(hn2, d_up, name="mlp_dw_up", ta=True, out_dtype=BF, o_sharded=True)
    d_hn2 = _matmul(d_up, w['mlp_w_up'][layer], name="mlp_d_hn", tb=True, b_sharded=True)
    dx1, grads['norm2'] = _rmsnorm_bwd(x1, w['norm2'][layer], d_hn2, dx2, f"norm2_bwd_{layer}")
    w_out = w[_W_OUT[layer]]
    d_og = _matmul(dx1, w_out, name=f"mixer_d_og_{layer}", tb=True, out_dtype=BF)
    dw_out = _matmul(og, dx1, name=f"mixer_dw_out_{layer}", ta=True, out_dtype=BF)
    grads[_W_OUT[layer]] = dw_out.reshape(N_CHIPS, dw_out.shape[0] // N_CHIPS, d)
    d_hn = _mixer_backward(layer, hn, w, tables, mixer_saved, d_og, grads)
    dx, grads['norm1'] = _rmsnorm_bwd(x, w['norm1'][layer], d_hn, dx1, f"norm1_bwd_{layer}")
    return dx, grads


PACK_ROWS = 256


def _pack(arrays):
    flat = []
    for a in arrays:
        v = a.astype(F32).reshape(-1)
        v = jnp.pad(v, (0, (-v.shape[0]) % 128))
        flat.append(v.reshape(-1, 128))
    buf = jnp.concatenate(flat, axis=0)
    return jnp.pad(buf, ((0, (-buf.shape[0]) % PACK_ROWS), (0, 0)))


def _unpack(buf, shapes):
    lead = buf.shape[:-2]
    out, off = [], 0
    for shp in shapes:
        n = math.prod(shp)
        rows = -(-n // 128)
        piece = buf[..., off:off + rows, :].reshape(lead + (rows * 128,))[..., :n]
        out.append(piece.reshape(lead + tuple(shp)))
        off += rows
    return out


_WEIGHTS = ('norm1', 'norm2', 'final_norm', 'ret_w_in', 'ret_gn_gain', 'ret_w_out', 'gdn_w_in', 'gdn_conv_w',
            'gdn_a_log', 'gdn_dt_bias', 'gdn_norm_gain', 'gdn_w_out', 'gla_w_in', 'gla_w_gate_up', 'gla_gate_bias',
            'gla_norm_gain', 'gla_w_out', 'lru_w_in', 'lru_conv_w', 'lru_conv_b', 'lru_w_rgate', 'lru_b_rgate',
            'lru_w_igate', 'lru_b_igate', 'lru_lambda', 'lru_w_out', 'mlp_w_up', 'mlp_w_down')
_FWD_PARAMS = ('x',) + _WEIGHTS
_BIG = ('ret_w_in', 'ret_w_out', 'gdn_w_in', 'gdn_w_out', 'gla_w_in', 'gla_w_out', 'lru_w_in', 'lru_w_out',
        'mlp_w_up', 'mlp_w_down')
_SMALL = tuple(n for n in _WEIGHTS if n not in _BIG)
_SMALL_SHARDED = ('ret_gn_gain', 'gdn_conv_w', 'gla_w_gate_up', 'gla_gate_bias', 'gla_norm_gain', 'lru_conv_w',
                  'lru_conv_b', 'lru_lambda')


def kernel(*args):
    names = _FWD_PARAMS + ('loss_target',) + tuple('m_' + n for n in _WEIGHTS) + tuple('v_' + n for n in _WEIGHTS)
    assert len(args) == len(names)
    a = dict(zip(names, args))
    x = a['x'][0]
    target = a['loss_target'][0]
    s, d = x.shape
    chip = 2 * lax.axis_index("x") + lax.axis_index("y")

    small_local = [a[n][0] if a[n].ndim == 3 else a[n] for n in _SMALL_SHARDED]
    small_pack = _pack(small_local)
    core_arr = lax.axis_index("c").astype(jnp.int32).reshape(1)
    chip_arr = chip.astype(jnp.int32).reshape(1)

    def whole_cols(g):
        return g.transpose(1, 0, 2).reshape(g.shape[1], N_CHIPS * g.shape[2])

    def whole_rows(g):
        return g.reshape(N_CHIPS * g.shape[1], g.shape[2])

    w = {'mlp_w_up': [None] * 4, 'mlp_w_down': [None] * 4}
    for n in _SMALL:
        if n not in _SMALL_SHARDED:
            w[n] = a[n][0] if n.startswith('lru_') else a[n]
    w['gdn_a_log'], w['gdn_dt_bias'] = _pad_cols(w['gdn_a_log']), _pad_cols(w['gdn_dt_bias'])
    w['lru_b_rgate'] = w['lru_b_rgate'].reshape(LRU_BLOCKS, 1, -1)
    w['lru_b_igate'] = w['lru_b_igate'].reshape(LRU_BLOCKS, 1, -1)

    def gather_start(layer, after):
        ops = [a[_W_IN[layer]][0].astype(BF), a[_W_OUT[layer]][0].astype(BF), a['mlp_w_up'][layer].astype(BF),
               a['mlp_w_down'][layer].astype(BF)] + ([small_pack] if layer == 0 else [])
        split = [True] * 4 + ([False] if layer == 0 else [])
        return _chip_exchange_start('gather', ops, [(N_CHIPS,) + o.shape for o in ops], split,
                                    f"gather_start_{layer}", after), split

    def gather_finish(layer, started, after):
        started, split = started
        own, lands = _chip_exchange_wait('gather', started, split, f"gather_wait_{layer}", after)
        lands = list(_pass_to_sibling(lands[:4], f"gather_pass_{layer}")) + list(lands[4:])
        g_in, g_out, g_up, g_down = [lax.dynamic_update_slice_in_dim(l, o[None], chip, axis=0)
                                     for l, o in zip(lands[:4], own[:4])]
        w['mlp_w_up'][layer], w['mlp_w_down'][layer], w[_W_OUT[layer]] = g_up, whole_rows(g_down), whole_rows(g_out)
        if layer == 0:
            small_all = lax.dynamic_update_slice_in_dim(lands[4], own[4][None], chip, axis=0)
            for n, piece in zip(_SMALL_SHARDED, _unpack(small_all, [p.shape for p in small_local])):
                w[n] = whole_cols(piece)
            w['gla_w_gate_up'] = jnp.pad(w['gla_w_gate_up'], ((0, 128 - GLA_RANK), (0, 0)))
        if layer in (0, 3):
            w[_W_IN[layer]] = g_in
        else:
            name, tail = ('gdn', 2 * GDN_HEADS) if layer == 1 else ('gla', GLA_RANK)
            full = whole_cols(g_in)
            w[name + '_w_main'] = full[:, :full.shape[1] - tail]
            w[name + '_w_small'] = _pad_cols(full[:, full.shape[1] - tail:])

    tables = _ret_tables(s, d // RET_HEADS)
    saved = []
    h = x
    gather_finish(0, gather_start(0, x), x)
    for layer in range(4):
        nxt = gather_start(layer + 1, w[_W_OUT[layer]]) if layer < 3 else None
        h, sv = _layer_forward(layer, h, w, tables, None if nxt is None else nxt[0][4])
        saved.append(sv)
        if nxt is not None:
            gather_finish(layer + 1, nxt, h)
    loss_part, dh, g_final = _loss_head(h, w['final_norm'], target)
    loss = lax.psum(loss_part, ("x", "y", "c"))

    grad, delta, new_m, new_v = {}, {}, {}, {}
    small_grads = {'final_norm': g_final}
    norm_grads = {'norm1': [None] * 4, 'norm2': [None] * 4}
    mlp_upd = {'mlp_w_up': None, 'mlp_w_down': None}

    def update_layer(layer, started, after):
        red = _reduce_scatter_finish(started, str(layer), chip_arr, after)
        for n, (mine, theirs) in ((_W_IN[layer], red[0]), (_W_OUT[layer], red[1])):
            grad[n], delta[n], new_m[n], new_v[n] = _adamw_halves(a[n], a['m_' + n], a['v_' + n], 0, mine, theirs,
                                                                  core_arr, f"adamw_{n}")
        for n, (mine, theirs) in (('mlp_w_up', red[2]), ('mlp_w_down', red[3])):
            mlp_upd[n] = _adamw_halves(a[n], a['m_' + n], a['v_' + n], layer, mine, theirs, core_arr,
                                       f"adamw_{n}_{layer}", carried=mlp_upd[n])

    scatters = {}
    token = None
    for layer in reversed(range(4)):
        dh, g = _layer_backward(layer, dh, w, tables, saved[layer], token)
        saved[layer] = None
        scatters[layer] = _reduce_scatter_begin([g[_W_IN[layer]], g[_W_OUT[layer]], g['mlp_w_up'], g['mlp_w_down']],
                                                str(layer), core_arr)
        token = scatters[layer][4]
        norm_grads['norm1'][layer], norm_grads['norm2'][layer] = g['norm1'], g['norm2']
        for n in _SMALL:
            if n in g:
                small_grads[n] = g[n]
    small_grads['norm1'] = jnp.stack(norm_grads['norm1'])
    small_grads['norm2'] = jnp.stack(norm_grads['norm2'])

    full_shapes = [small_grads[n].shape for n in _SMALL]
    small_part = _pack([small_grads[n] for n in _SMALL])
    sibling_part = _stream_to_sibling([small_part[None]], False, "small_swap")[0][0]
    chip_part = _sum_slabs(jnp.stack([small_part, sibling_part]), F32, "sum_small_cores")
    small_started = _chip_exchange_start('gather', [chip_part], [(N_CHIPS,) + chip_part.shape], [False],
                                         "small_gather_start", core_arr)
    after = token
    for layer in (3, 2, 1):
        update_layer(layer, scatters[layer], after)
        after = new_v[_W_OUT[layer]]
    own, lands = _chip_exchange_wait('gather', small_started, [False], "small_gather_wait", after)
    total = _sum_slabs(lax.dynamic_update_slice_in_dim(lands[0], own[0][None], chip, axis=0), F32, "sum_small_chips")
    local_g = {}
    for n, full in zip(_SMALL, _unpack(total, full_shapes)):
        shp = a[n].shape
        if n in _SMALL_SHARDED:
            full = full.reshape(full.shape[0], -1)
            cq = shp[-1]
            full = lax.dynamic_slice_in_dim(full, chip * cq, cq, axis=1)
        elif n in ('gdn_a_log', 'gdn_dt_bias'):
            full = full[:, :shp[-1]]
        local_g[n] = full.reshape(shp)
    shapes = [a[n].shape for n in _SMALL]
    packed = [_pack([src[n] for n in _SMALL]) for src in
              (a, local_g, {n: a['m_' + n] for n in _SMALL}, {n: a['v_' + n] for n in _SMALL})]
    upd = _adamw(*packed, "adamw_small")
    for n, gr, dl, nm, nv in zip(_SMALL, _unpack(packed[1], shapes), *[_unpack(u, shapes) for u in upd]):
        grad[n], delta[n], new_m[n], new_v[n] = gr, dl, nm, nv
    update_layer(0, scatters[0], upd[0])
    for n in ('mlp_w_up', 'mlp_w_down'):
        grad[n], delta[n], new_m[n], new_v[n] = mlp_upd[n]

    out = [loss, dh.reshape(a['x'].shape)]
    for group in (grad, delta, new_m, new_v):
        out += [group[n].reshape(a[n].shape) for n in _WEIGHTS]
    return tuple(out)
```

```python
import functools
import math

import jax
import jax.numpy as jnp
from jax import lax
from jax.experimental import pallas as pl
from jax.experimental.pallas import tpu as pltpu

F32 = jnp.float32
BF = jnp.bfloat16
MESH = pl.DeviceIdType.MESH

NORM_EPS = 1e-6
CHUNK = 64
RET_HEADS = 8
GDN_HEADS = 16
GLA_HEADS = 4
GLA_RANK = 16
GLA_TAU = 16.0
LRU_BLOCKS = 16
LRU_C = 8.0
CONV_WIDTH = 4
ROPE_BASE = 10000.0
N_CHIPS = 4
N_DEV = 8

ADAM_LR = 0.001
ADAM_B1 = 0.9
ADAM_B2 = 0.999
ADAM_EPS = 1e-08
ADAM_WD = 0.01
ADAM_STEP = 10

VMEM_LIMIT_BYTES = 56 * 1024 * 1024
TOKEN_BLOCK = 256
ROW_BLOCK = 256
GDN_GROUP = 8


def _cparams(**kw):
    return pltpu.CompilerParams(vmem_limit_bytes=VMEM_LIMIT_BYTES, **kw)


def _bf16_parts(x, n):
    parts = []
    for _ in range(n - 1):
        p = x.astype(BF)
        parts.append(p)
        x = x - p.astype(F32)
    return parts + [x.astype(BF)]


def _raw_mm(a, b, ta, tb, hi):
    nb = a.ndim - 2
    batch = tuple(range(nb))
    dims = (((nb + (0 if ta else 1),), (nb + (1 if tb else 0),)), (batch, batch))

    def dot(p, q):
        return lax.dot_general(p, q, dims, preferred_element_type=F32)

    if not hi:
        return dot(a.astype(BF), b.astype(BF))
    if hi == 'l':
        ae = a.astype(BF)
        b0, b1, b2 = _bf16_parts(b.astype(F32), 3)
        return dot(ae, b0) + (dot(ae, b1) + dot(ae, b2))
    if hi == 'r':
        be = b.astype(BF)
        a0, a1, a2 = _bf16_parts(a.astype(F32), 3)
        return dot(a0, be) + (dot(a1, be) + dot(a2, be))
    a0, a1 = _bf16_parts(a.astype(F32), 2)
    b0, b1 = _bf16_parts(b.astype(F32), 2)
    return dot(a0, b0) + (dot(a0, b1) + dot(a1, b0))


@functools.partial(jax.custom_vjp, nondiff_argnums=(2, 3, 4))
def _mm_vjp(a, b, ta, tb, hi):
    return _raw_mm(a, b, ta, tb, hi)


def _mm_vjp_fwd(a, b, ta, tb, hi):
    return _raw_mm(a, b, ta, tb, hi), (a, b)


def _mm_vjp_bwd(ta, tb, hi, res, g):
    a, b = res
    if ta:
        da = _raw_mm(b, g, tb, True, 'l' if hi == 'r' else bool(hi))
    else:
        da = _raw_mm(g, b, False, not tb, 'r' if hi == 'r' else bool(hi))
    if tb:
        db = _raw_mm(g, a, True, ta, 'r' if hi == 'l' else bool(hi))
    else:
        db = _raw_mm(a, g, not ta, False, 'l' if hi == 'l' else bool(hi))
    return da, db


_mm_vjp.defvjp(_mm_vjp_fwd, _mm_vjp_bwd)


def _mm_diff(a, b, ta=False, tb=False, hi=False):
    return _mm_vjp(a, b, ta, tb, hi)


def _mm_plain(a, b, ta=False, tb=False, hi=False):
    return _raw_mm(a, b, ta, tb, hi)


def _shift_rows(x, k, up):
    if k == 0:
        return x
    n = x.shape[0]
    rows = lax.broadcasted_iota(jnp.int32, x.shape, 0)
    if up:
        return jnp.where(rows < n - k, pltpu.roll(x, n - k, 0), 0.0)
    return jnp.where(rows >= k, pltpu.roll(x, k, 0), 0.0)


@functools.partial(jax.custom_vjp, nondiff_argnums=(1, 2))
def _shift_vjp(x, k, up):
    return _shift_rows(x, k, up)


def _shift_vjp_fwd(x, k, up):
    return _shift_rows(x, k, up), None


def _shift_vjp_bwd(k, up, _, g):
    return (_shift_rows(g, k, not up),)


_shift_vjp.defvjp(_shift_vjp_fwd, _shift_vjp_bwd)


def _sigmoid(x):
    return 1.0 / (1.0 + jnp.exp(-x))


def _silu(x):
    return x * _sigmoid(x)


def _softplus(x):
    return jnp.maximum(x, 0.0) + jnp.log(1.0 + jnp.exp(-jnp.abs(x)))


def _gelu_tanh(x):
    return 0.5 * x * (1.0 + jnp.tanh(math.sqrt(2.0 / math.pi) * (x + 0.044715 * (x * x * x))))


def _expm1(x):
    series = x * (1.0 + x * (0.5 + x * (1.0 / 6.0 + x * (1.0 / 24.0))))
    return jnp.where(jnp.abs(x) < 0.03, series, jnp.exp(x) - 1.0)


def _rmsnorm(x, g):
    return x * lax.rsqrt(jnp.mean(x * x, axis=-1, keepdims=True) + NORM_EPS) * g


def _head_norm(o, gain, center):
    if center:
        o = o - jnp.mean(o, axis=-1, keepdims=True)
    return o * lax.rsqrt(jnp.mean(o * o, axis=-1, keepdims=True) + NORM_EPS) * gain


def _l2norm(x):
    return x * lax.rsqrt(jnp.sum(x * x, axis=-1, keepdims=True) + NORM_EPS)


def _iota2(shape, dim):
    return lax.broadcasted_iota(jnp.int32, shape, dim)


def _tri_ones(n, upper=False):
    i, j = _iota2((n, n), 0), _iota2((n, n), 1)
    return jnp.where((i <= j) if upper else (j <= i), 1.0, 0.0).astype(F32)


def _linscan(a, u, rev):
    n = a.shape[0]
    rows = _iota2(a.shape, 0)
    d = 1
    while d < n:
        if rev:
            valid = rows < n - d
            a_s, u_s = pltpu.roll(a, n - d, 0), pltpu.roll(u, n - d, 0)
        else:
            valid = rows >= d
            a_s, u_s = pltpu.roll(a, d, 0), pltpu.roll(u, d, 0)
        u = a * jnp.where(valid, u_s, 0.0) + u
        a = a * jnp.where(valid, a_s, 1.0)
        d *= 2
    return u


def _ret_chunk(mm, q, k, v, g, gain, state, cos, sin, dintra, qdec, kdec, cdec):
    dk = q.shape[-1]
    half = dk // 2

    def rot(t):
        t1, t2 = t[:, :half], t[:, half:]
        return jnp.concatenate([t1 * cos - t2 * sin, t1 * sin + t2 * cos], axis=-1)

    qr = rot(q)
    kr = rot(k) * (dk ** -0.5)
    scores = mm(qr, kr, tb=True) * dintra
    o = mm(scores, v) + mm(qr * qdec, state)
    new_state = state * cdec + mm(kr * kdec, v, ta=True)
    y = _head_norm(o, gain, True) * _silu(g)
    return y, new_state


def _gla_chunk(mm, q, k, v, r, glow, wgu, bias, gain, state_t):
    dk = q.shape[-1]
    c = q.shape[0]
    logit = mm(glow, wgu) + bias
    la = -_softplus(-logit) * (1.0 / GLA_TAU)
    cum = mm(_tri_ones(c), la, hi='l')
    rows = _iota2(la.shape, 0)
    ref = jnp.sum(jnp.where(rows < c // 2, la, 0.0), axis=0, keepdims=True)
    tot = jnp.sum(la, axis=0, keepdims=True)
    fwd, bwd = jnp.exp(cum - ref), jnp.exp(ref - cum)
    qs = q * (dk ** -0.5)
    s_lo = mm(qs * fwd, k * bwd, tb=True)
    s_up = mm(qs * bwd, k * fwd, tb=True)
    i, j = _iota2((c, c), 0), _iota2((c, c), 1)
    scores = jnp.where(i >= j, s_lo, s_up)
    o = mm(scores, v) + mm(qs * jnp.exp(cum), state_t, tb=True)
    k_end = k * jnp.exp(tot - cum)
    new_state_t = state_t * jnp.exp(tot) + mm(v, k_end, ta=True)
    y = _head_norm(o, gain, False) * _silu(r)
    return y, new_state_t


def _gdn_group(mm, q, k, v, z, ba, a_log, dt_bias, gain, state, head):
    g, c, dk = q.shape
    lanes = lax.broadcasted_iota(jnp.int32, (1, 1, ba.shape[-1]), 2)
    oh_b = jnp.where(lanes == head, 1.0, 0.0).astype(F32)
    oh_a = jnp.where(lanes == head + GDN_HEADS, 1.0, 0.0).astype(F32)
    beta = _sigmoid(jnp.sum(ba * oh_b, axis=-1, keepdims=True))
    a_logit = jnp.sum(ba * oh_a, axis=-1, keepdims=True)
    a_h = jnp.sum(a_log * oh_b[0], axis=-1, keepdims=True)
    dt_h = jnp.sum(dt_bias * oh_b[0], axis=-1, keepdims=True)
    la = -jnp.exp(a_h) * _softplus(a_logit + dt_h)
    q = _l2norm(q) * (dk ** -0.5)
    k = _l2norm(k)
    tri = jnp.broadcast_to(_tri_ones(c), (g, c, c))
    tri_up = jnp.broadcast_to(_tri_ones(c, upper=True), (g, c, c))
    cum_k = mm(tri, la * jnp.ones((g, c, dk), F32), hi='l')
    la_sq = la * jnp.ones((g, c, c), F32)
    cum_i = mm(tri, la_sq, hi='l')
    cum_j = mm(la_sq, tri_up, ta=True, hi='r')
    i, j = _iota2((c, c), 0), _iota2((c, c), 1)
    strict = i > j
    rel = jnp.where(strict, jnp.exp(jnp.where(strict, cum_i - cum_j, 0.0)), 0.0)
    a_mat = beta * rel * mm(k, k, tb=True)
    inv = jnp.where(i == j, 1.0, 0.0).astype(F32) - a_mat
    power = a_mat
    for _ in range(int(math.log2(c)) - 1):
        power = mm(power, power, hi=True)
        inv = inv + mm(inv, power, hi=True)
    tot = jnp.sum(la, axis=1, keepdims=True)
    u = mm(inv, beta * v, hi=True)
    w = mm(inv, (beta * jnp.exp(cum_k)) * k, hi=True)
    k_end = k * jnp.exp(tot - cum_k)
    di, dj = _iota2((dk, dk), 0), _iota2((dk, dk), 1)
    trans = jnp.exp(tot) * jnp.where(di == dj, 1.0, 0.0).astype(F32) - mm(k_end, w, ta=True)
    inject = mm(k_end, u, ta=True)
    ys = []
    for n in range(g):
        state = mm(trans[n], state) + inject[n]
        ys.append(_head_norm(mm(q[n], state), gain, False) * _silu(z[n]))
    return ys, state


def _conv_taps(shift, x, taps):
    out = None
    for tap, w in enumerate(taps):
        term = shift(x, CONV_WIDTH - 1 - tap, False) * w
        out = term if out is None else out + term
    return out


def _lru_pre(mm, shift, xb, yb, taps, cb, lam, wr, br, wi, bi):
    xb = _conv_taps(shift, xb, taps) + cb
    r = _sigmoid(mm(xb, wr) + br)
    i = _sigmoid(mm(xb, wi) + bi)
    log_a = (-LRU_C) * _softplus(-lam) * r
    a = jnp.exp(log_a)
    u = jnp.sqrt(-_expm1(2.0 * log_a)) * (i * xb)
    return a, u, _gelu_tanh(yb)


_TOKEN_SPEC = pl.BlockSpec((8, 128), lambda *_: (0, 0))


MATMUL_VMEM_BUDGET = 40 * 1024 * 1024
_TILE_SIZES = (3072, 2048, 1536, 1024, 768, 512, 384, 256, 128)


def _tile_choices(extent):
    return [t for t in _TILE_SIZES if t <= extent and extent % t == 0] or [extent]


def _matmul_tiles(m, n, kdim, n_unit, k_unit, a_item, b_item, o_item, e_item):
    best = None
    for bm in _tile_choices(m):
        for bn in _tile_choices(n_unit):
            for bk in _tile_choices(k_unit):
                nk = kdim // bk
                vmem = 2 * (bm * bk * a_item + bk * bn * b_item + bm * bn * (o_item + e_item)) + bm * bn * 4 * (2 if nk > 1 else 1)
                if vmem > MATMUL_VMEM_BUDGET:
                    continue
                steps = (m // bm) * (n // bn) * nk
                traffic = m * kdim * a_item * (1 if nk == 1 else n // bn) + kdim * n * b_item * (m // bm)
                if best is None or (steps, traffic) < best[0]:
                    best = ((steps, traffic), (bm, bn, bk))
    return best[1]


def _matmul(a, b, *, name, ta=False, tb=False, out_dtype=F32, b_sharded=False, o_sharded=False,
            epilogue=None, extra=None, token=None):
    m, kdim = (a.shape[1], a.shape[0]) if ta else a.shape
    if b_sharded:
        nq = b.shape[2]
        n = b.shape[1] if tb else N_CHIPS * nq
        assert kdim == (N_CHIPS * nq if tb else b.shape[1])
    else:
        n = b.shape[0] if tb else b.shape[1]
        assert kdim == (b.shape[1] if tb else b.shape[0])
    n_unit = b.shape[2] if (b_sharded and not tb) else (n // N_CHIPS if o_sharded else n)
    k_unit = b.shape[2] if (b_sharded and tb) else kdim
    bm, bn, bk = _matmul_tiles(m, n, kdim, n_unit, k_unit, a.dtype.itemsize, b.dtype.itemsize,
                               jnp.dtype(out_dtype).itemsize, 0 if extra is None else extra.dtype.itemsize)
    nk = kdim // bk
    grid = (m // bm, n // bn, nk)

    a_spec = pl.BlockSpec((bk, bm), lambda i, j, k: (k, i)) if ta else pl.BlockSpec((bm, bk), lambda i, j, k: (i, k))
    if b_sharded and not tb:
        per = b.shape[2] // bn
        b_spec = pl.BlockSpec((None, bk, bn), lambda i, j, k: (j // per, k, j % per))
    elif b_sharded:
        per = b.shape[2] // bk
        b_spec = pl.BlockSpec((None, bn, bk), lambda i, j, k: (k // per, j, k % per))
    elif tb:
        b_spec = pl.BlockSpec((bn, bk), lambda i, j, k: (j, k))
    else:
        b_spec = pl.BlockSpec((bk, bn), lambda i, j, k: (k, j))
    if o_sharded:
        per_o = (n // N_CHIPS) // bn
        o_spec = pl.BlockSpec((None, bm, bn), lambda i, j, k: (j // per_o, i, j % per_o))
        out_shape = jax.ShapeDtypeStruct((N_CHIPS, m, n // N_CHIPS), out_dtype)
    else:
        o_spec = pl.BlockSpec((bm, bn), lambda i, j, k: (i, j))
        out_shape = jax.ShapeDtypeStruct((m, n), out_dtype)
    in_specs = [a_spec, b_spec]
    operands = [a, b]
    if extra is not None:
        in_specs.append(pl.BlockSpec((bm, bn), lambda i, j, k: (i, j)))
        operands.append(extra)
    if token is not None:
        in_specs.append(_TOKEN_SPEC)
        operands.append(token)

    def finish(acc, e_ref, o_ref):
        if epilogue == 'add':
            acc = acc + e_ref[...].astype(F32)
        elif epilogue == 'relu2':
            acc = jnp.square(jnp.maximum(acc, 0.0))
        elif epilogue == 'dact':
            acc = acc * (2.0 * jnp.sqrt(e_ref[...].astype(F32)))
        o_ref[...] = acc.astype(o_ref.dtype)

    def body_one_step(*refs):
        finish(_raw_mm(refs[0][...], refs[1][...], ta, tb, False), refs[2] if extra is not None else None, refs[-1])

    def body(*refs):
        a_ref, b_ref = refs[0], refs[1]
        e_ref = refs[2] if extra is not None else None
        o_ref, acc_ref = refs[-2], refs[-1]
        k = pl.program_id(2)

        @pl.when(k == 0)
        def _():
            acc_ref[...] = jnp.zeros_like(acc_ref)

        acc_ref[...] += _raw_mm(a_ref[...], b_ref[...], ta, tb, False)

        @pl.when(k == nk - 1)
        def _():
            finish(acc_ref[...], e_ref, o_ref)

    return pl.pallas_call(
        body_one_step if nk == 1 else body, name=name, grid=grid, in_specs=in_specs, out_specs=o_spec,
        out_shape=out_shape, scratch_shapes=[] if nk == 1 else [pltpu.VMEM((bm, bn), F32)],
        compiler_params=_cparams(),
    )(*operands)


def _rmsnorm_fwd(x, g, name, token=None):
    s, d = x.shape

    def body(x_ref, g_ref, *rest):
        rest[-1][...] = _rmsnorm(x_ref[...], g_ref[...]).astype(BF)

    return pl.pallas_call(
        body, name=name, grid=(s // ROW_BLOCK,),
        in_specs=[pl.BlockSpec((ROW_BLOCK, d), lambda i: (i, 0)), pl.BlockSpec((1, d), lambda i: (0, 0))]
        + ([] if token is None else [_TOKEN_SPEC]),
        out_specs=pl.BlockSpec((ROW_BLOCK, d), lambda i: (i, 0)),
        out_shape=jax.ShapeDtypeStruct((s, d), BF), compiler_params=_cparams(),
    )(x, g.reshape(1, d), *([] if token is None else [token]))


def _rmsnorm_bwd(x, g, dh, dres, name):
    s, d = x.shape

    def body(x_ref, g_ref, dh_ref, dres_ref, dx_ref, dg_ref):
        _, vjp = jax.vjp(_rmsnorm, x_ref[...], g_ref[...])
        dx, dg = vjp(dh_ref[...].astype(F32))
        dx_ref[...] = dres_ref[...] + dx

        @pl.when(pl.program_id(0) == 0)
        def _():
            dg_ref[...] = jnp.zeros_like(dg_ref)

        dg_ref[...] += dg

    row = pl.BlockSpec((ROW_BLOCK, d), lambda i: (i, 0))
    vec = pl.BlockSpec((1, d), lambda i: (0, 0))
    dx, dg = pl.pallas_call(
        body, name=name, grid=(s // ROW_BLOCK,), in_specs=[row, vec, row, row], out_specs=[row, vec],
        out_shape=[jax.ShapeDtypeStruct((s, d), F32), jax.ShapeDtypeStruct((1, d), F32)],
        compiler_params=_cparams(),
    )(x, g.reshape(1, d), dh, dres)
    return dx, dg.reshape(d)


def _loss_head(x, g, target):
    s, d = x.shape

    def loss_fn(xv, gv, tv):
        err = _rmsnorm(xv, gv) - tv
        return 0.5 * jnp.sum(jnp.mean(err * err, axis=-1, keepdims=True), axis=0, keepdims=True)

    def body(x_ref, g_ref, t_ref, dx_ref, dg_ref, loss_ref):
        tv = t_ref[...]
        loss, vjp = jax.vjp(lambda xv, gv: loss_fn(xv, gv, tv), x_ref[...], g_ref[...])
        dx, dg = vjp(jnp.ones((1, 1), F32))
        dx_ref[...] = dx

        @pl.when(pl.program_id(0) == 0)
        def _():
            dg_ref[...] = jnp.zeros_like(dg_ref)
            loss_ref[...] = jnp.zeros_like(loss_ref)

        dg_ref[...] += dg
        loss_ref[...] += loss * jnp.ones_like(loss_ref)

    row = pl.BlockSpec((ROW_BLOCK, d), lambda i: (i, 0))
    vec = pl.BlockSpec((1, d), lambda i: (0, 0))
    dx, dg, loss = pl.pallas_call(
        body, name="loss_head", grid=(s // ROW_BLOCK,), in_specs=[row, vec, row],
        out_specs=[row, vec, pl.BlockSpec((1, 128), lambda i: (0, 0))],
        out_shape=[jax.ShapeDtypeStruct((s, d), F32), jax.ShapeDtypeStruct((1, d), F32),
                   jax.ShapeDtypeStruct((1, 128), F32)],
        compiler_params=_cparams(),
    )(x, g.reshape(1, d), target)
    return loss[0, 0], dx, dg.reshape(d)


def _ret_tables(s, dk):
    h = jnp.arange(RET_HEADS, dtype=F32)
    log_gamma = jnp.log1p(-jnp.exp2(-5.0 - h))
    pos = jnp.arange(CHUNK, dtype=F32)
    dist = jnp.abs(pos[:, None] - pos[None, :])
    dintra = jnp.exp(log_gamma[:, None, None] * dist)
    qdec = jnp.exp(log_gamma[:, None] * (pos + 1.0))[:, :, None]
    kdec = jnp.exp(log_gamma[:, None] * (CHUNK - 1.0 - pos))[:, :, None]
    cdec = jnp.exp(log_gamma * CHUNK)[:, None, None]
    inv = ROPE_BASE ** (-jnp.arange(0, dk, 2, dtype=F32) / dk)
    ang = jnp.arange(s, dtype=F32)[:, None] * inv[None, :]
    return jnp.cos(ang), jnp.sin(ang), dintra, qdec, kdec, cdec


def _ret_specs(s, dk, dv, tb, rev):
    nh = RET_HEADS
    nb = s // tb
    bi = (lambda b: nb - 1 - b) if rev else (lambda b: b)
    voff = 2 * nh * dk // dv
    cpb = tb // CHUNK
    return dict(
        q=pl.BlockSpec((tb, dk), lambda h, b: (bi(b), h)),
        k=pl.BlockSpec((tb, dk), lambda h, b: (bi(b), nh + h)),
        v=pl.BlockSpec((tb, dv), lambda h, b: (bi(b), voff + h)),
        g=pl.BlockSpec((tb, dv), lambda h, b: (bi(b), voff + nh + h)),
        gain=pl.BlockSpec((None, 1, dv), lambda h, b: (h, 0, 0)),
        cs=pl.BlockSpec((tb, dk // 2), lambda h, b: (bi(b), 0)),
        dintra=pl.BlockSpec((None, CHUNK, CHUNK), lambda h, b: (h, 0, 0)),
        dec=pl.BlockSpec((None, CHUNK, 1), lambda h, b: (h, 0, 0)),
        cdec=pl.BlockSpec((None, 1, 1), lambda h, b: (h, 0, 0)),
        hv=pl.BlockSpec((tb, dv), lambda h, b: (bi(b), h)),
        hk=pl.BlockSpec((tb, dk), lambda h, b: (bi(b), h)),
        st=pl.BlockSpec((None, cpb, dk, dv), lambda h, b: (h, bi(b), 0, 0)),
    )


def _ret_fwd(proj, gain, tables):
    s = proj.shape[0]
    d = proj.shape[1] // 6
    dk, dv = d // RET_HEADS, 2 * d // RET_HEADS
    tb = min(TOKEN_BLOCK, s)
    cpb = tb // CHUNK
    sp = _ret_specs(s, dk, dv, tb, False)
    cos, sin, dintra, qdec, kdec, cdec = tables

    def body(q_ref, k_ref, v_ref, g_ref, gain_ref, cos_ref, sin_ref, di_ref, qd_ref, kd_ref, cd_ref,
             y_ref, st_ref, state):
        @pl.when(pl.program_id(1) == 0)
        def _():
            state[...] = jnp.zeros_like(state)

        for c in range(cpb):
            sl = pl.ds(c * CHUNK, CHUNK)
            st_ref[c] = state[...]
            y, new_state = _ret_chunk(_mm_plain, q_ref[sl, :], k_ref[sl, :], v_ref[sl, :], g_ref[sl, :],
                                      gain_ref[...], state[...], cos_ref[sl, :], sin_ref[sl, :],
                                      di_ref[...], qd_ref[...], kd_ref[...], cd_ref[...])
            y_ref[sl, :] = y.astype(BF)
            state[...] = new_state

    return pl.pallas_call(
        body, name="ret_fwd", grid=(RET_HEADS, s // tb),
        in_specs=[sp['q'], sp['k'], sp['v'], sp['g'], sp['gain'], sp['cs'], sp['cs'], sp['dintra'], sp['dec'],
                  sp['dec'], sp['cdec']],
        out_specs=[sp['hv'], sp['st']],
        out_shape=[jax.ShapeDtypeStruct((s, RET_HEADS * dv), BF),
                   jax.ShapeDtypeStruct((RET_HEADS, s // CHUNK, dk, dv), F32)],
        scratch_shapes=[pltpu.VMEM((dk, dv), F32)], compiler_params=_cparams(),
    )(proj, proj, proj, proj, gain.reshape(RET_HEADS, 1, dv), cos, sin, dintra, qdec, kdec, cdec)


def _ret_bwd(proj, gain, tables, states, dy):
    s = proj.shape[0]
    d = proj.shape[1] // 6
    dk, dv = d // RET_HEADS, 2 * d // RET_HEADS
    tb = min(TOKEN_BLOCK, s)
    cpb = tb // CHUNK
    sp = _ret_specs(s, dk, dv, tb, True)
    cos, sin, dintra, qdec, kdec, cdec = tables

    def body(q_ref, k_ref, v_ref, g_ref, gain_ref, cos_ref, sin_ref, di_ref, qd_ref, kd_ref, cd_ref, st_ref,
             dy_ref, dq_ref, dk_ref, dv_ref, dg_ref, dgain_ref, dstate):
        @pl.when(pl.program_id(1) == 0)
        def _():
            dstate[...] = jnp.zeros_like(dstate)
            dgain_ref[...] = jnp.zeros_like(dgain_ref)

        for c in reversed(range(cpb)):
            sl = pl.ds(c * CHUNK, CHUNK)
            cos_c, sin_c = cos_ref[sl, :], sin_ref[sl, :]
            di, qd, kd, cd = di_ref[...], qd_ref[...], kd_ref[...], cd_ref[...]

            def fn(q, k, v, g, gn, st):
                return _ret_chunk(_mm_diff, q, k, v, g, gn, st, cos_c, sin_c, di, qd, kd, cd)

            _, vjp = jax.vjp(fn, q_ref[sl, :], k_ref[sl, :], v_ref[sl, :], g_ref[sl, :], gain_ref[...],
                             st_ref[c])
            dq, dkk, dvv, dg, dgn, dst = vjp((dy_ref[sl, :].astype(F32), dstate[...]))
            dq_ref[sl, :] = dq.astype(BF)
            dk_ref[sl, :] = dkk.astype(BF)
            dv_ref[sl, :] = dvv.astype(BF)
            dg_ref[sl, :] = dg.astype(BF)
            dgain_ref[...] += dgn
            dstate[...] = dst

    dq, dkk, dvv, dg, dgain = pl.pallas_call(
        body, name="ret_bwd", grid=(RET_HEADS, s // tb),
        in_specs=[sp['q'], sp['k'], sp['v'], sp['g'], sp['gain'], sp['cs'], sp['cs'], sp['dintra'], sp['dec'],
                  sp['dec'], sp['cdec'], sp['st'], sp['hv']],
        out_specs=[sp['hk'], sp['hk'], sp['hv'], sp['hv'], sp['gain']],
        out_shape=[jax.ShapeDtypeStruct((s, RET_HEADS * dk), BF), jax.ShapeDtypeStruct((s, RET_HEADS * dk), BF),
                   jax.ShapeDtypeStruct((s, RET_HEADS * dv), BF), jax.ShapeDtypeStruct((s, RET_HEADS * dv), BF),
                   jax.ShapeDtypeStruct((RET_HEADS, 1, dv), F32)],
        scratch_shapes=[pltpu.VMEM((dk, dv), F32)], compiler_params=_cparams(),
    )(proj, proj, proj, proj, gain.reshape(RET_HEADS, 1, dv), cos, sin, dintra, qdec, kdec, cdec, states, dy)
    return jnp.concatenate([dq, dkk, dvv, dg], axis=1), dgain.reshape(RET_HEADS, dv)


def _gla_specs(s, dk, dv, tb, rev):
    nh = GLA_HEADS
    nb = s // tb
    bi = (lambda b: nb - 1 - b) if rev else (lambda b: b)
    voff = 2 * nh * dk // dv
    cpb = tb // CHUNK
    return dict(
        q=pl.BlockSpec((tb, dk), lambda h, b: (bi(b), h)),
        k=pl.BlockSpec((tb, dk), lambda h, b: (bi(b), nh + h)),
        v=pl.BlockSpec((tb, dv), lambda h, b: (bi(b), voff + h)),
        r=pl.BlockSpec((tb, dv), lambda h, b: (bi(b), voff + nh + h)),
        glow=pl.BlockSpec((tb, 128), lambda h, b: (bi(b), 0)),
        wgu=pl.BlockSpec((128, dk), lambda h, b: (0, h)),
        bias=pl.BlockSpec((1, dk), lambda h, b: (0, h)),
        gain=pl.BlockSpec((None, 1, dv), lambda h, b: (h, 0, 0)),
        hv=pl.BlockSpec((tb, dv), lambda h, b: (bi(b), h)),
        hk=pl.BlockSpec((tb, dk), lambda h, b: (bi(b), h)),
        hg=pl.BlockSpec((tb, 128), lambda h, b: (bi(b), h)),
        st=pl.BlockSpec((None, cpb, dv, dk), lambda h, b: (h, bi(b), 0, 0)),
    )


def _gla_fwd(proj, glow, wgu, bias, gain):
    s = proj.shape[0]
    d = proj.shape[1] // 3
    dk, dv = d // 2 // GLA_HEADS, d // GLA_HEADS
    tb = min(TOKEN_BLOCK, s)
    cpb = tb // CHUNK
    sp = _gla_specs(s, dk, dv, tb, False)

    def body(q_ref, k_ref, v_ref, r_ref, gl_ref, wgu_ref, b_ref, gain_ref, y_ref, st_ref, state):
        @pl.when(pl.program_id(1) == 0)
        def _():
            state[...] = jnp.zeros_like(state)

        for c in range(cpb):
            sl = pl.ds(c * CHUNK, CHUNK)
            st_ref[c] = state[...]
            y, new_state = _gla_chunk(_mm_plain, q_ref[sl, :], k_ref[sl, :], v_ref[sl, :], r_ref[sl, :],
                                      gl_ref[sl, :], wgu_ref[...], b_ref[...], gain_ref[...], state[...])
            y_ref[sl, :] = y.astype(BF)
            state[...] = new_state

    return pl.pallas_call(
        body, name="gla_fwd", grid=(GLA_HEADS, s // tb),
        in_specs=[sp['q'], sp['k'], sp['v'], sp['r'], sp['glow'], sp['wgu'], sp['bias'], sp['gain']],
        out_specs=[sp['hv'], sp['st']],
        out_shape=[jax.ShapeDtypeStruct((s, GLA_HEADS * dv), BF),
                   jax.ShapeDtypeStruct((GLA_HEADS, s // CHUNK, dv, dk), F32)],
        scratch_shapes=[pltpu.VMEM((dv, dk), F32)], compiler_params=_cparams(),
    )(proj, proj, proj, proj, glow, wgu, bias, gain.reshape(GLA_HEADS, 1, dv))


def _gla_bwd(proj, glow, wgu, bias, gain, states, dy):
    s = proj.shape[0]
    d = proj.shape[1] // 3
    dk, dv = d // 2 // GLA_HEADS, d // GLA_HEADS
    tb = min(TOKEN_BLOCK, s)
    cpb = tb // CHUNK
    sp = _gla_specs(s, dk, dv, tb, True)

    def body(q_ref, k_ref, v_ref, r_ref, gl_ref, wgu_ref, b_ref, gain_ref, st_ref, dy_ref,
             dq_ref, dk_ref, dv_ref, dr_ref, dgl_ref, dwgu_ref, db_ref, dgain_ref, dstate):
        @pl.when(pl.program_id(1) == 0)
        def _():
            dstate[...] = jnp.zeros_like(dstate)
            dwgu_ref[...] = jnp.zeros_like(dwgu_ref)
            db_ref[...] = jnp.zeros_like(db_ref)
            dgain_ref[...] = jnp.zeros_like(dgain_ref)

        for c in reversed(range(cpb)):
            sl = pl.ds(c * CHUNK, CHUNK)

            def fn(q, k, v, r, gl, w, b, gn, st):
                return _gla_chunk(_mm_diff, q, k, v, r, gl, w, b, gn, st)

            _, vjp = jax.vjp(fn, q_ref[sl, :], k_ref[sl, :], v_ref[sl, :], r_ref[sl, :], gl_ref[sl, :],
                             wgu_ref[...], b_ref[...], gain_ref[...], st_ref[c])
            dq, dkk, dvv, dr, dgl, dw, db, dgn, dst = vjp((dy_ref[sl, :].astype(F32), dstate[...]))
            dq_ref[sl, :] = dq.astype(BF)
            dk_ref[sl, :] = dkk.astype(BF)
            dv_ref[sl, :] = dvv.astype(BF)
            dr_ref[sl, :] = dr.astype(BF)
            dgl_ref[sl, :] = dgl.astype(BF)
            dwgu_ref[...] += dw
            db_ref[...] += db
            dgain_ref[...] += dgn
            dstate[...] = dst

    nh = GLA_HEADS
    dq, dkk, dvv, dr, dgl, dwgu, db, dgain = pl.pallas_call(
        body, name="gla_bwd", grid=(nh, s // tb),
        in_specs=[sp['q'], sp['k'], sp['v'], sp['r'], sp['glow'], sp['wgu'], sp['bias'], sp['gain'], sp['st'],
                  sp['hv']],
        out_specs=[sp['hk'], sp['hk'], sp['hv'], sp['hv'], sp['hg'], sp['wgu'], sp['bias'], sp['gain']],
        out_shape=[jax.ShapeDtypeStruct((s, nh * dk), BF), jax.ShapeDtypeStruct((s, nh * dk), BF),
                   jax.ShapeDtypeStruct((s, nh * dv), BF), jax.ShapeDtypeStruct((s, nh * dv), BF),
                   jax.ShapeDtypeStruct((s, nh * 128), BF), jax.ShapeDtypeStruct((128, nh * dk), F32),
                   jax.ShapeDtypeStruct((1, nh * dk), F32), jax.ShapeDtypeStruct((nh, 1, dv), F32)],
        scratch_shapes=[pltpu.VMEM((dv, dk), F32)], compiler_params=_cparams(),
    )(proj, proj, proj, proj, glow, wgu, bias, gain.reshape(nh, 1, dv), states, dy)
    return jnp.concatenate([dq, dkk, dvv, dr], axis=1), dgl, dwgu, db, dgain.reshape(nh, dv)


def _gdn_specs(s, dk):
    nh = GDN_HEADS
    col = lambda off: pl.BlockSpec((s, dk), lambda h: (0, off + h))
    tap = lambda off: pl.BlockSpec((CONV_WIDTH, dk), lambda h: (0, off + h))
    vec = pl.BlockSpec((1, 128), lambda h: (0, 0))
    return dict(q=col(0), k=col(nh), v=col(2 * nh), z=col(3 * nh), ba=pl.BlockSpec((s, 128), lambda h: (0, 0)),
                cq=tap(0), ck=tap(nh), cv=tap(2 * nh), vec=vec, gain=pl.BlockSpec((1, dk), lambda h: (0, 0)),
                head=col(0))


def _conv_silu(shift, x, taps):
    return _silu(_conv_taps(shift, x, taps))


def _load_taps(ref):
    return [ref[t:t + 1, :] for t in range(CONV_WIDTH)]


def _gdn_fwd(proj, ba, conv_w, a_log, dt_bias, gain):
    s = proj.shape[0]
    dk = proj.shape[1] // (4 * GDN_HEADS)
    sp = _gdn_specs(s, dk)
    rows = GDN_GROUP * CHUNK

    def grp(t):
        return t.reshape(GDN_GROUP, CHUNK, t.shape[-1])

    ngroups = s // rows

    def body(q_ref, k_ref, v_ref, z_ref, ba_ref, cq_ref, ck_ref, cv_ref, al_ref, dt_ref, gain_ref, y_ref, states,
             qc, kc, vc, state):
        head = pl.program_id(0)
        qc[...] = _conv_silu(_shift_rows, q_ref[...], _load_taps(cq_ref))
        kc[...] = _conv_silu(_shift_rows, k_ref[...], _load_taps(ck_ref))
        vc[...] = _conv_silu(_shift_rows, v_ref[...], _load_taps(cv_ref))
        state[...] = jnp.zeros_like(state)

        def step(n, carry):
            base = pl.multiple_of(n * rows, rows)
            sl = pl.ds(base, rows)
            states[n] = state[...]
            ys, new_state = _gdn_group(_mm_plain, grp(qc[sl, :]), grp(kc[sl, :]), grp(vc[sl, :]), grp(z_ref[sl, :]),
                                       grp(ba_ref[sl, :]), al_ref[...], dt_ref[...], gain_ref[...], state[...], head)
            for i, y in enumerate(ys):
                y_ref[pl.ds(base + i * CHUNK, CHUNK), :] = y.astype(BF)
            state[...] = new_state
            return carry

        lax.fori_loop(0, ngroups, step, 0)

    return pl.pallas_call(
        body, name="gdn_fwd", grid=(GDN_HEADS,),
        in_specs=[sp['q'], sp['k'], sp['v'], sp['z'], sp['ba'], sp['cq'], sp['ck'], sp['cv'], sp['vec'], sp['vec'],
                  sp['gain']],
        out_specs=[sp['head'], pl.BlockSpec((None, ngroups, dk, dk), lambda h: (h, 0, 0, 0))],
        out_shape=[jax.ShapeDtypeStruct((s, GDN_HEADS * dk), BF),
                   jax.ShapeDtypeStruct((GDN_HEADS, ngroups, dk, dk), F32)],
        scratch_shapes=[pltpu.VMEM((s, dk), F32)] * 3 + [pltpu.VMEM((dk, dk), F32)],
        compiler_params=_cparams(),
    )(proj, proj, proj, proj, ba, conv_w, conv_w, conv_w, a_log, dt_bias, gain)


def _gdn_bwd(proj, ba, conv_w, a_log, dt_bias, gain, group_states, dy):
    s = proj.shape[0]
    dk = proj.shape[1] // (4 * GDN_HEADS)
    sp = _gdn_specs(s, dk)
    rows = GDN_GROUP * CHUNK
    ngroups = s // rows

    def grp(t):
        return t.reshape(GDN_GROUP, CHUNK, t.shape[-1])

    def body(q_ref, k_ref, v_ref, z_ref, ba_ref, cq_ref, ck_ref, cv_ref, al_ref, dt_ref, gain_ref, states, dy_ref,
             dq_ref, dk_ref, dv_ref, dz_ref, dba_ref, dcq_ref, dck_ref, dcv_ref, dal_ref, ddt_ref, dgain_ref,
             qc, kc, vc, dqc, dkc, dvc, dstate):
        head = pl.program_id(0)

        @pl.when(head == 0)
        def _():
            dba_ref[...] = jnp.zeros_like(dba_ref)
            dal_ref[...] = jnp.zeros_like(dal_ref)
            ddt_ref[...] = jnp.zeros_like(ddt_ref)
            dgain_ref[...] = jnp.zeros_like(dgain_ref)

        qc[...] = _conv_silu(_shift_rows, q_ref[...], _load_taps(cq_ref))
        kc[...] = _conv_silu(_shift_rows, k_ref[...], _load_taps(ck_ref))
        vc[...] = _conv_silu(_shift_rows, v_ref[...], _load_taps(cv_ref))
        dstate[...] = jnp.zeros_like(dstate)

        def bstep(i, carry):
            n = ngroups - 1 - i
            base = pl.multiple_of(n * rows, rows)
            sl = pl.ds(base, rows)

            def fn(q, k, v, z, b, al, dt, gn, st):
                return _gdn_group(_mm_diff, q, k, v, z, b, al, dt, gn, st, head)

            _, vjp = jax.vjp(fn, grp(qc[sl, :]), grp(kc[sl, :]), grp(vc[sl, :]), grp(z_ref[sl, :]),
                             grp(ba_ref[sl, :]), al_ref[...], dt_ref[...], gain_ref[...], states[n])
            dys = [dy_ref[pl.ds(base + j * CHUNK, CHUNK), :].astype(F32) for j in range(GDN_GROUP)]
            dq, dkk, dvv, dz, db, dal, ddt, dgn, dst = vjp((dys, dstate[...]))
            dqc[sl, :] = dq.reshape(rows, dk)
            dkc[sl, :] = dkk.reshape(rows, dk)
            dvc[sl, :] = dvv.reshape(rows, dk)
            dz_ref[sl, :] = dz.reshape(rows, dk).astype(BF)
            dba_ref[sl, :] += db.reshape(rows, db.shape[-1])
            dal_ref[...] += dal
            ddt_ref[...] += ddt
            dgain_ref[...] += dgn
            dstate[...] = dst
            return carry

        lax.fori_loop(0, ngroups, bstep, 0)

        for x_ref, c_ref, dpost, dx_ref, dc_ref in ((q_ref, cq_ref, dqc, dq_ref, dcq_ref),
                                                    (k_ref, ck_ref, dkc, dk_ref, dck_ref),
                                                    (v_ref, cv_ref, dvc, dv_ref, dcv_ref)):
            _, vjp = jax.vjp(lambda x, *taps: _conv_silu(_shift_vjp, x, taps), x_ref[...], *_load_taps(c_ref))
            grads = vjp(dpost[...])
            dx_ref[...] = grads[0].astype(BF)
            for t in range(CONV_WIDTH):
                dc_ref[t:t + 1, :] = grads[1 + t]

    nh = GDN_HEADS
    col_bf = jax.ShapeDtypeStruct((s, nh * dk), BF)
    tap_out = jax.ShapeDtypeStruct((CONV_WIDTH, nh * dk), F32)
    tap_spec = pl.BlockSpec((CONV_WIDTH, dk), lambda h: (0, h))
    dq, dkk, dvv, dz, dba, dcq, dck, dcv, dal, ddt, dgain = pl.pallas_call(
        body, name="gdn_bwd", grid=(nh,),
        in_specs=[sp['q'], sp['k'], sp['v'], sp['z'], sp['ba'], sp['cq'], sp['ck'], sp['cv'], sp['vec'], sp['vec'],
                  sp['gain'], pl.BlockSpec((None, ngroups, dk, dk), lambda h: (h, 0, 0, 0)), sp['head']],
        out_specs=[sp['head']] * 4 + [sp['ba'], tap_spec, tap_spec, tap_spec, sp['vec'], sp['vec'], sp['gain']],
        out_shape=[col_bf] * 4 + [jax.ShapeDtypeStruct((s, 128), F32), tap_out, tap_out, tap_out,
                                  jax.ShapeDtypeStruct((1, 128), F32), jax.ShapeDtypeStruct((1, 128), F32),
                                  jax.ShapeDtypeStruct((1, dk), F32)],
        scratch_shapes=[pltpu.VMEM((s, dk), F32)] * 6 + [pltpu.VMEM((dk, dk), F32)],
        compiler_params=_cparams(),
    )(proj, proj, proj, proj, ba, conv_w, conv_w, conv_w, a_log, dt_bias, gain, group_states, dy)
    return (jnp.concatenate([dq, dkk, dvv, dz], axis=1), dba, jnp.concatenate([dcq, dck, dcv], axis=1), dal, ddt,
            dgain)


def _lru_specs(s, bw):
    nb = LRU_BLOCKS
    return dict(
        xb=pl.BlockSpec((s, bw), lambda n: (0, n)), yb=pl.BlockSpec((s, bw), lambda n: (0, nb + n)),
        taps=pl.BlockSpec((CONV_WIDTH, bw), lambda n: (0, n)), vec=pl.BlockSpec((1, bw), lambda n: (0, n)),
        w=pl.BlockSpec((None, bw, bw), lambda n: (n, 0, 0)), b=pl.BlockSpec((None, 1, bw), lambda n: (n, 0, 0)),
        col=pl.BlockSpec((s, bw), lambda n: (0, n)))


def _lru_fwd(proj, conv_w, conv_b, lam, wr, br, wi, bi):
    s = proj.shape[0]
    bw = proj.shape[1] // (2 * LRU_BLOCKS)
    sp = _lru_specs(s, bw)

    def body(xb_ref, yb_ref, cw_ref, cb_ref, lam_ref, wr_ref, br_ref, wi_ref, bi_ref, y_ref):
        a, u, gy = _lru_pre(_mm_plain, _shift_rows, xb_ref[...], yb_ref[...], _load_taps(cw_ref), cb_ref[...],
                            lam_ref[...], wr_ref[...], br_ref[...], wi_ref[...], bi_ref[...])
        y_ref[...] = (_linscan(a, u, False) * gy).astype(BF)

    return pl.pallas_call(
        body, name="lru_fwd", grid=(LRU_BLOCKS,),
        in_specs=[sp['xb'], sp['yb'], sp['taps'], sp['vec'], sp['vec'], sp['w'], sp['b'], sp['w'], sp['b']],
        out_specs=sp['col'], out_shape=jax.ShapeDtypeStruct((s, LRU_BLOCKS * bw), BF),
        compiler_params=_cparams(),
    )(proj, proj, conv_w, conv_b, lam, wr, br, wi, bi)


def _lru_bwd(proj, conv_w, conv_b, lam, wr, br, wi, bi, dy):
    s = proj.shape[0]
    nb = LRU_BLOCKS
    bw = proj.shape[1] // (2 * nb)
    sp = _lru_specs(s, bw)

    def body(xb_ref, yb_ref, cw_ref, cb_ref, lam_ref, wr_ref, br_ref, wi_ref, bi_ref, dy_ref,
             dxb_ref, dyb_ref, dcw_ref, dcb_ref, dlam_ref, dwr_ref, dbr_ref, dwi_ref, dbi_ref):
        def pre(xb, yb, t0, t1, t2, t3, cb, lm, w_r, b_r, w_i, b_i):
            return _lru_pre(_mm_diff, _shift_vjp, xb, yb, (t0, t1, t2, t3), cb, lm, w_r, b_r, w_i, b_i)

        (a, u, gy), vjp = jax.vjp(pre, xb_ref[...], yb_ref[...], *_load_taps(cw_ref), cb_ref[...], lam_ref[...],
                                  wr_ref[...], br_ref[...], wi_ref[...], bi_ref[...])
        h = _linscan(a, u, False)
        dout = dy_ref[...].astype(F32)
        g = _linscan(_shift_rows(a, 1, True), dout * gy, True)
        grads = vjp((g * _shift_rows(h, 1, False), g, dout * h))
        dxb_ref[...] = grads[0].astype(BF)
        dyb_ref[...] = grads[1].astype(BF)
        for t in range(CONV_WIDTH):
            dcw_ref[t:t + 1, :] = grads[2 + t]
        dcb_ref[...] = grads[6]
        dlam_ref[...] = grads[7]
        dwr_ref[...] = grads[8]
        dbr_ref[...] = grads[9]
        dwi_ref[...] = grads[10]
        dbi_ref[...] = grads[11]

    outs = pl.pallas_call(
        body, name="lru_bwd", grid=(nb,),
        in_specs=[sp['xb'], sp['yb'], sp['taps'], sp['vec'], sp['vec'], sp['w'], sp['b'], sp['w'], sp['b'], sp['col']],
        out_specs=[sp['col'], sp['col'], sp['taps'], sp['vec'], sp['vec'], sp['w'], sp['b'], sp['w'], sp['b']],
        out_shape=[jax.ShapeDtypeStruct((s, nb * bw), BF), jax.ShapeDtypeStruct((s, nb * bw), BF),
                   jax.ShapeDtypeStruct((CONV_WIDTH, nb * bw), F32), jax.ShapeDtypeStruct((1, nb * bw), F32),
                   jax.ShapeDtypeStruct((1, nb * bw), F32), jax.ShapeDtypeStruct((nb, bw, bw), F32),
                   jax.ShapeDtypeStruct((nb, 1, bw), F32), jax.ShapeDtypeStruct((nb, bw, bw), F32),
                   jax.ShapeDtypeStruct((nb, 1, bw), F32)],
        compiler_params=_cparams(),
    )(proj, proj, conv_w, conv_b, lam, wr, br, wi, bi, dy)
    return (jnp.concatenate([outs[0], outs[1]], axis=1),) + tuple(outs[2:])


ROW_STEP_BYTES = 12 * 1024 * 1024


def _row_tile(rows, cols, bytes_per_row_set):
    t = 8
    while t * 2 <= rows and rows % (t * 2) == 0 and (t * 2) * cols * bytes_per_row_set <= ROW_STEP_BYTES:
        t *= 2
    return t


def _sum_slabs(a, out_dtype, name):
    n, rows, cols = a.shape
    tr = _row_tile(rows, cols, 4 * (n + 1))

    def body(*refs):
        acc = refs[0][...].astype(F32)
        for r in refs[1:n]:
            acc = acc + r[...].astype(F32)
        refs[n][...] = acc.astype(out_dtype)

    specs = [pl.BlockSpec((None, tr, cols), functools.partial(lambda i, k: (k, i, 0), k=k)) for k in range(n)]
    return pl.pallas_call(
        body, name=name, grid=(rows // tr,), in_specs=specs, out_specs=pl.BlockSpec((tr, cols), lambda i: (i, 0)),
        out_shape=jax.ShapeDtypeStruct((rows, cols), out_dtype), compiler_params=_cparams(),
    )(*([a] * n))


def _sum_keep_and_landed(g, landed, core, name):
    four, hr, cols = landed.shape
    tr = _row_tile(hr, cols, 2 * 3)
    tph = hr // tr

    def body(core_ref, g_ref, l_ref, o_ref):
        o_ref[...] = (g_ref[...].astype(F32) + l_ref[...].astype(F32)).astype(o_ref.dtype)

    grid_spec = pltpu.PrefetchScalarGridSpec(
        num_scalar_prefetch=1, grid=(four, tph),
        in_specs=[pl.BlockSpec((None, tr, cols), lambda j, r, core_ref: (j, core_ref[0] * tph + r, 0)),
                  pl.BlockSpec((None, tr, cols), lambda j, r, core_ref: (j, r, 0))],
        out_specs=pl.BlockSpec((None, tr, cols), lambda j, r, core_ref: (j, r, 0)))
    return pl.pallas_call(body, name=name, grid_spec=grid_spec, out_shape=jax.ShapeDtypeStruct(landed.shape, g.dtype),
                          compiler_params=_cparams())(core, g, landed)


def _sum_chip_parts(parts, landed, chip, name):
    _, rows, cols = parts.shape
    tr = _row_tile(rows, cols, 2 * 4 + 4)

    def body(chip_ref, p_ref, l0_ref, l1_ref, l2_ref, o_ref):
        acc = l0_ref[...].astype(F32) + l1_ref[...].astype(F32)
        o_ref[...] = (acc + l2_ref[...].astype(F32)) + p_ref[...].astype(F32)

    slab = lambda k: pl.BlockSpec((None, tr, cols), lambda i, chip_ref: (k, i, 0))
    grid_spec = pltpu.PrefetchScalarGridSpec(
        num_scalar_prefetch=1, grid=(rows // tr,),
        in_specs=[pl.BlockSpec((None, tr, cols), lambda i, chip_ref: (chip_ref[0], i, 0)), slab(0), slab(1), slab(2)],
        out_specs=pl.BlockSpec((tr, cols), lambda i, chip_ref: (i, 0)))
    return pl.pallas_call(body, name=name, grid_spec=grid_spec, out_shape=jax.ShapeDtypeStruct((rows, cols), F32),
                          compiler_params=_cparams())(chip, parts, landed, landed, landed)


_ADAM_C1 = 1.0 / (1.0 - ADAM_B1 ** ADAM_STEP)
_ADAM_C2 = 1.0 / (1.0 - ADAM_B2 ** ADAM_STEP)


def _adamw_math(w, g, m, v):
    nm = ADAM_B1 * m + (1.0 - ADAM_B1) * g
    nv = ADAM_B2 * v + (1.0 - ADAM_B2) * (g * g)
    den = jnp.sqrt(nv * _ADAM_C2) + ADAM_EPS
    inv = pl.reciprocal(den, approx=True)
    inv = inv * (2.0 - den * inv)
    delta = -ADAM_LR * ((nm * _ADAM_C1) * inv + ADAM_WD * w)
    return delta, nm, nv


def _adamw(w, g, m, v, name):
    rows, cols = w.shape
    tr = _row_tile(rows, cols, 4 * 7)

    def body(w_ref, g_ref, m_ref, v_ref, d_ref, nm_ref, nv_ref):
        d_ref[...], nm_ref[...], nv_ref[...] = _adamw_math(w_ref[...], g_ref[...], m_ref[...], v_ref[...])

    spec = pl.BlockSpec((tr, cols), lambda i: (i, 0))
    shape = jax.ShapeDtypeStruct((rows, cols), F32)
    return pl.pallas_call(
        body, name=name, grid=(rows // tr,), in_specs=[spec] * 4, out_specs=[spec] * 3, out_shape=[shape] * 3,
        compiler_params=_cparams(),
    )(w, g, m, v)


def _adamw_halves(w, m, v, layer, mine, theirs, core, name, carried=None):
    n_layers, rows, cols = w.shape
    hr = mine.shape[0]
    tr = _row_tile(hr, cols, 4 * 9)
    tph = hr // tr

    def body(core_ref, w_ref, q_ref, t_ref, m_ref, v_ref, *rest):
        g_ref, d_ref, nm_ref, nv_ref = rest[-4:]
        is_mine = (pl.program_id(0) // tph) == core_ref[0]
        g = jnp.where(is_mine, q_ref[...], t_ref[...])
        g_ref[...] = g
        d_ref[...], nm_ref[...], nv_ref[...] = _adamw_math(w_ref[...], g, m_ref[...], v_ref[...])

    slab = pl.BlockSpec((None, tr, cols), lambda i, core_ref: (layer, i, 0))

    def mine_index(i, core_ref):
        return jnp.where(i // tph == core_ref[0], i % tph, jnp.where(core_ref[0] == 0, tph - 1, 0)), 0

    def theirs_index(i, core_ref):
        return jnp.where(i // tph == core_ref[0], jnp.where(core_ref[0] == 0, 0, tph - 1), i % tph), 0

    in_specs = [slab, pl.BlockSpec((tr, cols), mine_index), pl.BlockSpec((tr, cols), theirs_index), slab, slab]
    operands = [core, w, mine, theirs, m, v]
    aliases = {}
    if carried is not None:
        in_specs += [_ANY] * 4
        aliases = {len(operands) + k: k for k in range(4)}
        operands += list(carried)
    grid_spec = pltpu.PrefetchScalarGridSpec(num_scalar_prefetch=1, grid=(rows // tr,), in_specs=in_specs,
                                             out_specs=[slab] * 4)
    return pl.pallas_call(
        body, name=name, grid_spec=grid_spec, out_shape=[jax.ShapeDtypeStruct(w.shape, F32)] * 4,
        input_output_aliases=aliases, compiler_params=_cparams(),
    )(*operands)


_ANY = pl.BlockSpec(memory_space=pl.ANY)


def _mesh_pos():
    return lax.axis_index("x"), lax.axis_index("y"), lax.axis_index("c")


def _remote(src, dst, send_sems, recv_sems, k, dev):
    return pltpu.make_async_remote_copy(src_ref=src, dst_ref=dst, send_sem=send_sems.at[k], recv_sem=recv_sems.at[k],
                                        device_id=dev, device_id_type=MESH)


STREAM_CHUNK_BYTES = 2 * 1024 * 1024
STREAM_SLOTS = 4


def _chunk_rows(rows, cols, itemsize):
    assert rows % 16 == 0, rows
    t = 16
    while t * 2 <= rows and rows % (t * 2) == 0 and (t * 2) * cols * itemsize <= STREAM_CHUNK_BYTES:
        t *= 2
    return t


def _stream_chunks(n_chunks, src_at, dst_at, buf, load_sems, send_sems, recv_sem, sibling):
    def load(k, slot):
        return pltpu.make_async_copy(src_at(k), buf.at[slot], load_sems.at[slot])

    def send(k, slot):
        return pltpu.make_async_remote_copy(src_ref=buf.at[slot], dst_ref=dst_at(k), send_sem=send_sems.at[slot],
                                            recv_sem=recv_sem, device_id=sibling, device_id_type=MESH)

    def step(k, carry):
        slot = k % STREAM_SLOTS

        @pl.when(k >= STREAM_SLOTS)
        def _():
            send(k - STREAM_SLOTS, slot).wait_send()

        load(k, slot).start()

        @pl.when(k >= 1)
        def _():
            prev = (k - 1) % STREAM_SLOTS
            load(k - 1, prev).wait()
            send(k - 1, prev).start()

        return carry

    lax.fori_loop(0, n_chunks, step, 0)
    last = (n_chunks - 1) % STREAM_SLOTS
    load(n_chunks - 1, last).wait()
    send(n_chunks - 1, last).start()
    for k in range(max(0, n_chunks - STREAM_SLOTS), n_chunks):
        send(k, k % STREAM_SLOTS).wait_send()


_HBM = pl.BlockSpec(memory_space=pltpu.HBM)
_SEM = pl.BlockSpec(memory_space=pltpu.SEMAPHORE)
_DATAFLOW = pltpu.SideEffectType.DATAFLOW_SIDE_EFFECTING


def _chip_copies(kind, ins, lands, split, send_sems, recv_sems):
    x, y, c = _mesh_pos()
    me = 2 * x + y
    chips = [(1 - x, y), (x, 1 - y), (1 - x, 1 - y)]
    pairs = []
    for i in range(len(ins)):
        for j, chip in enumerate(chips):
            pj = 2 * chip[0] + chip[1]
            if kind == 'scatter':
                src, dst, got = ins[i].at[pj], lands[i].at[j], lands[i].at[j]
            elif split[i]:
                hr = ins[i].shape[0] // 2
                rows = pl.ds(c * hr, hr)
                src, dst, got = ins[i].at[rows], lands[i].at[me, rows], lands[i].at[pj, rows]
            else:
                src, dst, got = ins[i], lands[i].at[me], lands[i].at[pj]
            k = 3 * i + j
            pairs.append((_remote(src, dst, send_sems, recv_sems, k, (*chip, c)),
                          _remote(got, got, send_sems, recv_sems, k, (*chip, c))))
    return pairs


def _chip_exchange_start(kind, srcs, land_shapes, split, name, after):
    n = len(srcs)

    def body(*refs):
        ins, lands = refs[:n], refs[n:2 * n]
        send_sems, recv_sems = refs[2 * n + 1], refs[2 * n + 2]
        token = refs[-1]
        for send, _ in _chip_copies(kind, ins, lands, split, send_sems, recv_sems):
            send.start()
        token[...] = jnp.zeros_like(token)

    hbm = lambda t: pltpu.with_memory_space_constraint(t, pltpu.HBM)
    operands = [hbm(s) for s in srcs] + [hbm(lax.empty(shp, s.dtype)) for shp, s in zip(land_shapes, srcs)] + [after]
    out = pl.pallas_call(
        body, name=name, in_specs=[_HBM] * (2 * n) + [_ANY],
        out_specs=[_SEM, _SEM] + [_HBM] * (2 * n) + [pl.BlockSpec(memory_space=pltpu.VMEM)],
        out_shape=[pltpu.SemaphoreType.DMA((3 * n,)), pltpu.SemaphoreType.DMA((3 * n,))]
        + [pltpu.HBM(s.shape, s.dtype) for s in srcs] + [pltpu.HBM(shp, s.dtype) for shp, s in zip(land_shapes, srcs)]
        + [jax.ShapeDtypeStruct((8, 128), F32)],
        input_output_aliases={i: 2 + i for i in range(2 * n)},
        compiler_params=pltpu.CompilerParams(has_side_effects=_DATAFLOW),
    )(*operands)
    return out[0], out[1], out[2:2 + n], out[2 + n:2 + 2 * n], out[-1]


def _chip_exchange_wait(kind, started, split, name, after):
    send_sems, recv_sems, srcs, lands, _ = started
    n = len(srcs)

    def body(*refs):
        ins, land_refs = refs[:n], refs[n:2 * n]
        for send, arrived in _chip_copies(kind, ins, land_refs, split, refs[2 * n], refs[2 * n + 1]):
            send.wait_send()
            arrived.wait_recv()

    out = pl.pallas_call(
        body, name=name, in_specs=[_HBM] * (2 * n) + [_SEM, _SEM, _ANY], out_specs=[_HBM] * (2 * n),
        out_shape=[pltpu.HBM(s.shape, s.dtype) for s in srcs] + [pltpu.HBM(l.shape, l.dtype) for l in lands],
        input_output_aliases={i: i for i in range(2 * n)},
        compiler_params=pltpu.CompilerParams(has_side_effects=_DATAFLOW),
    )(*srcs, *lands, send_sems, recv_sems, after)
    return out[:n], out[n:]


def _pass_to_sibling(gathered, name):
    n = len(gathered)
    tr = [_chunk_rows(g.shape[1] // 2, g.shape[2], g.dtype.itemsize) for g in gathered]

    def body(*refs):
        outs = refs[n:2 * n]
        recv_sems, load_sems, send_sems = refs[2 * n:2 * n + 3]
        bufs = refs[2 * n + 3:]
        x, y, c = _mesh_pos()
        sibling = (x, y, 1 - c)
        chips = [(1 - x, y), (x, 1 - y), (1 - x, 1 - y)]
        for i in range(n):
            hr = outs[i].shape[1] // 2
            for j, chip in enumerate(chips):
                def rows_at(k, i=i, pj=2 * chip[0] + chip[1], hr=hr):
                    return outs[i].at[pj, pl.ds(c * hr + k * tr[i], tr[i])]

                _stream_chunks(hr // tr[i], rows_at, rows_at, bufs[i], load_sems, send_sems, recv_sems.at[3 * i + j],
                               sibling)
        for i in range(n):
            hr = outs[i].shape[1] // 2
            for j, chip in enumerate(chips):
                blk = outs[i].at[2 * chip[0] + chip[1], pl.ds((1 - c) * hr, hr)]
                pltpu.make_async_remote_copy(src_ref=blk, dst_ref=blk, send_sem=send_sems.at[0],
                                             recv_sem=recv_sems.at[3 * i + j], device_id=sibling,
                                             device_id_type=MESH).wait_recv()

    return pl.pallas_call(
        body, name=name, in_specs=[_ANY] * n, out_specs=[_ANY] * n,
        out_shape=[jax.ShapeDtypeStruct(g.shape, g.dtype) for g in gathered],
        input_output_aliases={i: i for i in range(n)},
        scratch_shapes=[pltpu.SemaphoreType.DMA((3 * n,)), pltpu.SemaphoreType.DMA((STREAM_SLOTS,)),
                        pltpu.SemaphoreType.DMA((STREAM_SLOTS,))]
        + [pltpu.VMEM((STREAM_SLOTS, tr[i], g.shape[2]), g.dtype) for i, g in enumerate(gathered)],
        compiler_params=_cparams(),
    )(*gathered)


def _stream_to_sibling(srcs, halved, name):
    n = len(srcs)
    geo = []
    for s in srcs:
        rows = s.shape[1] // 2 if halved else s.shape[1]
        geo.append((s.shape[0], rows, s.shape[2], _chunk_rows(rows, s.shape[2], s.dtype.itemsize)))

    def body(*refs):
        ins, outs = refs[:n], refs[n:2 * n]
        recv_sems, load_sems, send_sems = refs[2 * n:2 * n + 3]
        bufs = refs[2 * n + 3:]
        x, y, c = _mesh_pos()
        sibling = (x, y, 1 - c)
        for i in range(n):
            slabs, rows, _, tr = geo[i]
            per_slab = rows // tr
            off = (1 - c) * rows if halved else 0

            def src_at(k, i=i, per_slab=per_slab, tr=tr, off=off):
                return ins[i].at[k // per_slab, pl.ds(off + (k % per_slab) * tr, tr)]

            def dst_at(k, i=i, per_slab=per_slab, tr=tr):
                return outs[i].at[k // per_slab, pl.ds((k % per_slab) * tr, tr)]

            _stream_chunks(slabs * per_slab, src_at, dst_at, bufs[i], load_sems, send_sems, recv_sems.at[i], sibling)
        for i in range(n):
            pltpu.make_async_remote_copy(src_ref=outs[i], dst_ref=outs[i], send_sem=send_sems.at[0],
                                         recv_sem=recv_sems.at[i], device_id=sibling, device_id_type=MESH).wait_recv()

    return pl.pallas_call(
        body, name=name, in_specs=[_ANY] * n, out_specs=[_ANY] * n,
        out_shape=[jax.ShapeDtypeStruct((g[0], g[1], g[2]), s.dtype) for g, s in zip(geo, srcs)],
        scratch_shapes=[pltpu.SemaphoreType.DMA((n,)), pltpu.SemaphoreType.DMA((STREAM_SLOTS,)),
                        pltpu.SemaphoreType.DMA((STREAM_SLOTS,))]
        + [pltpu.VMEM((STREAM_SLOTS, g[3], g[2]), s.dtype) for g, s in zip(geo, srcs)],
        compiler_params=_cparams(),
    )(*srcs)


def _reduce_scatter_begin(grads, tag, core):
    landed = _stream_to_sibling(grads, True, f"rs_swap_{tag}")
    parts = [_sum_keep_and_landed(g, l, core, f"rs_add2_{tag}_{i}") for i, (g, l) in enumerate(zip(grads, landed))]
    return _chip_exchange_start('scatter', parts, [(3,) + p.shape[1:] for p in parts], None, f"rs_scatter_start_{tag}",
                                core)


def _reduce_scatter_finish(started, tag, chip, after):
    parts, landed = _chip_exchange_wait('scatter', started, None, f"rs_scatter_wait_{tag}", after)
    mine = [_sum_chip_parts(p, l, chip, f"rs_add4_{tag}_{i}") for i, (p, l) in enumerate(zip(parts, landed))]
    theirs = _stream_to_sibling([m[None] for m in mine], False, f"rs_join_{tag}")
    return [(m, t[0]) for m, t in zip(mine, theirs)]


def _pad_cols(a, width=128):
    return jnp.pad(a, ((0, 0), (0, width - a.shape[1])))


def _mixer_forward(kind, hn, w, tables):
    if kind == 0:
        proj = _matmul(hn, w['ret_w_in'], name="ret_proj", b_sharded=True)
        og, states = _ret_fwd(proj, w['ret_gn_gain'], tables)
        return og, (proj, states)
    if kind == 1:
        proj = _matmul(hn, w['gdn_w_main'], name="gdn_proj")
        ba = _matmul(hn, w['gdn_w_small'], name="gdn_proj_ba")
        og, states = _gdn_fwd(proj, ba, w['gdn_conv_w'], w['gdn_a_log'], w['gdn_dt_bias'], w['gdn_norm_gain'])
        return og, (proj, ba, states)
    if kind == 2:
        proj = _matmul(hn, w['gla_w_main'], name="gla_proj")
        glow = _matmul(hn, w['gla_w_small'], name="gla_proj_gate")
        og, states = _gla_fwd(proj, glow, w['gla_w_gate_up'], w['gla_gate_bias'], w['gla_norm_gain'])
        return og, (proj, glow, states)
    proj = _matmul(hn, w['lru_w_in'], name="lru_proj", b_sharded=True)
    og = _lru_fwd(proj, w['lru_conv_w'], w['lru_conv_b'], w['lru_lambda'], w['lru_w_rgate'], w['lru_b_rgate'],
                  w['lru_w_igate'], w['lru_b_igate'])
    return og, (proj,)


def _mixer_backward(kind, hn, w, tables, saved, d_og, grads):
    d = hn.shape[1]
    if kind == 0:
        proj, states = saved
        d_proj, grads['ret_gn_gain'] = _ret_bwd(proj, w['ret_gn_gain'], tables, states, d_og)
        grads['ret_w_in'] = _matmul(hn, d_proj, name="ret_dw_in", ta=True, out_dtype=BF, o_sharded=True)
        return _matmul(d_proj, w['ret_w_in'], name="ret_dhn", tb=True, b_sharded=True)
    if kind == 1:
        proj, ba, states = saved
        d_proj, d_ba, grads['gdn_conv_w'], grads['gdn_a_log'], grads['gdn_dt_bias'], grads['gdn_norm_gain'] = _gdn_bwd(
            proj, ba, w['gdn_conv_w'], w['gdn_a_log'], w['gdn_dt_bias'], w['gdn_norm_gain'], states, d_og)
        d_ba = d_ba.astype(BF)
        dw_main = _matmul(hn, d_proj, name="gdn_dw_main", ta=True, out_dtype=BF)
        dw_small = _matmul(hn, d_ba, name="gdn_dw_small", ta=True, out_dtype=BF)
        dw = jnp.concatenate([dw_main, dw_small[:, :2 * GDN_HEADS]], axis=1)
        grads['gdn_w_in'] = dw.reshape(d, N_CHIPS, dw.shape[1] // N_CHIPS).transpose(1, 0, 2)
        d_hn = _matmul(d_proj, w['gdn_w_main'], name="gdn_dhn_main", tb=True)
        return _matmul(d_ba, w['gdn_w_small'], name="gdn_dhn_small", tb=True, epilogue='add', extra=d_hn)
    if kind == 2:
        proj, glow, states = saved
        d_proj, d_glow4, d_wgu, grads['gla_gate_bias'], grads['gla_norm_gain'] = _gla_bwd(
            proj, glow, w['gla_w_gate_up'], w['gla_gate_bias'], w['gla_norm_gain'], states, d_og)
        grads['gla_w_gate_up'] = d_wgu[:GLA_RANK]
        dw_main = _matmul(hn, d_proj, name="gla_dw_main", ta=True, out_dtype=BF)
        dw_small4 = _matmul(hn, d_glow4, name="gla_dw_small", ta=True, out_dtype=F32)
        dw_small = dw_small4.reshape(d, GLA_HEADS, 128)[:, :, :GLA_RANK].sum(axis=1).astype(BF)
        dw = jnp.concatenate([dw_main, dw_small], axis=1)
        grads['gla_w_in'] = dw.reshape(d, N_CHIPS, dw.shape[1] // N_CHIPS).transpose(1, 0, 2)
        d_hn = _matmul(d_proj, w['gla_w_main'], name="gla_dhn_main", tb=True)
        w_small4 = jnp.tile(w['gla_w_small'], (1, GLA_HEADS))
        return _matmul(d_glow4, w_small4, name="gla_dhn_small", tb=True, epilogue='add', extra=d_hn)
    (proj,) = saved
    (d_proj, grads['lru_conv_w'], grads['lru_conv_b'], grads['lru_lambda'], grads['lru_w_rgate'], grads['lru_b_rgate'],
     grads['lru_w_igate'], grads['lru_b_igate']) = _lru_bwd(
        proj, w['lru_conv_w'], w['lru_conv_b'], w['lru_lambda'], w['lru_w_rgate'], w['lru_b_rgate'], w['lru_w_igate'],
        w['lru_b_igate'], d_og)
    grads['lru_w_in'] = _matmul(hn, d_proj, name="lru_dw_in", ta=True, out_dtype=BF, o_sharded=True)
    return _matmul(d_proj, w['lru_w_in'], name="lru_dhn", tb=True, b_sharded=True)


_W_OUT = ('ret_w_out', 'gdn_w_out', 'gla_w_out', 'lru_w_out')
_W_IN = ('ret_w_in', 'gdn_w_in', 'gla_w_in', 'lru_w_in')


def _layer_forward(layer, x, w, tables, token=None):
    hn = _rmsnorm_fwd(x, w['norm1'][layer], f"norm1_fwd_{layer}", token)
    og, mixer_saved = _mixer_forward(layer, hn, w, tables)
    x1 = _matmul(og, w[_W_OUT[layer]], name=f"mixer_out_{layer}", epilogue='add', extra=x)
    hn2 = _rmsnorm_fwd(x1, w['norm2'][layer], f"norm2_fwd_{layer}")
    act = _matmul(hn2, w['mlp_w_up'][layer], name="mlp_up", b_sharded=True, epilogue='relu2', out_dtype=BF)
    x2 = _matmul(act, w['mlp_w_down'][layer], name="mlp_down", epilogue='add', extra=x1)
    return x2, (x, hn, mixer_saved, og, x1, hn2, act)


def _layer_backward(layer, dx2, w, tables, saved, token=None):
    x, hn, mixer_saved, og, x1, hn2, act = saved
    d = x.shape[1]
    grads = {}
    d_up = _matmul(dx2, w['mlp_w_down'][layer], name="mlp_d_up", tb=True, epilogue='dact', extra=act, out_dtype=BF,
                   token=token)
    dw_down = _matmul(act, dx2, name="mlp_dw_down", ta=True, out_dtype=BF)
    grads['mlp_w_down'] = dw_down.reshape(N_CHIPS, dw_down.shape[0] // N_CHIPS, d)
    grads['mlp_w_up'] = _matmul(hn2, d_up, name="mlp_dw_up", ta=True, out_dtype=BF, o_sharded=True)
    d_hn2 = _matmul(d_up, w['mlp_w_up'][layer], name="mlp_d_hn", tb=True, b_sharded=True)
    dx1, grads['norm2'] = _rmsnorm_bwd(x1, w['norm2'][layer], d_hn2, dx2, f"norm2_bwd_{layer}")
    w_out = w[_W_OUT[layer]]
    d_og = _matmul(dx1, w_out, name=f"mixer_d_og_{layer}", tb=True, out_dtype=BF)
    dw_out = _matmul(og, dx1, name=f"mixer_dw_out_{layer}", ta=True, out_dtype=BF)
    grads[_W_OUT[layer]] = dw_out.reshape(N_CHIPS, dw_out.shape[0] // N_CHIPS, d)
    d_hn = _mixer_backward(layer, hn, w, tables, mixer_saved, d_og, grads)
    dx, grads['norm1'] = _rmsnorm_bwd(x, w['norm1'][layer], d_hn, dx1, f"norm1_bwd_{layer}")
    return dx, grads


PACK_ROWS = 256


def _pack(arrays):
    flat = []
    for a in arrays:
        v = a.astype(F32).reshape(-1)
        v = jnp.pad(v, (0, (-v.shape[0]) % 128))
        flat.append(v.reshape(-1, 128))
    buf = jnp.concatenate(flat, axis=0)
    return jnp.pad(buf, ((0, (-buf.shape[0]) % PACK_ROWS), (0, 0)))


def _unpack(buf, shapes):
    lead = buf.shape[:-2]
    out, off = [], 0
    for shp in shapes:
        n = math.prod(shp)
        rows = -(-n // 128)
        piece = buf[..., off:off + rows, :].reshape(lead + (rows * 128,))[..., :n]
        out.append(piece.reshape(lead + tuple(shp)))
        off += rows
    return out


_WEIGHTS = ('norm1', 'norm2', 'final_norm', 'ret_w_in', 'ret_gn_gain', 'ret_w_out', 'gdn_w_in', 'gdn_conv_w',
            'gdn_a_log', 'gdn_dt_bias', 'gdn_norm_gain', 'gdn_w_out', 'gla_w_in', 'gla_w_gate_up', 'gla_gate_bias',
            'gla_norm_gain', 'gla_w_out', 'lru_w_in', 'lru_conv_w', 'lru_conv_b', 'lru_w_rgate', 'lru_b_rgate',
            'lru_w_igate', 'lru_b_igate', 'lru_lambda', 'lru_w_out', 'mlp_w_up', 'mlp_w_down')
_FWD_PARAMS = ('x',) + _WEIGHTS
_BIG = ('ret_w_in', 'ret_w_out', 'gdn_w_in', 'gdn_w_out', 'gla_w_in', 'gla_w_out', 'lru_w_in', 'lru_w_out',
        'mlp_w_up', 'mlp_w_down')
_SMALL = tuple(n for n in _WEIGHTS if n not in _BIG)
_SMALL_SHARDED = ('ret_gn_gain', 'gdn_conv_w', 'gla_w_gate_up', 'gla_gate_bias', 'gla_norm_gain', 'lru_conv_w',
                  'lru_conv_b', 'lru_lambda')


def kernel(*args):
    names = _FWD_PARAMS + ('loss_target',) + tuple('m_' + n for n in _WEIGHTS) + tuple('v_' + n for n in _WEIGHTS)
    assert len(args) == len(names)
    a = dict(zip(names, args))
    x = a['x'][0]
    target = a['loss_target'][0]
    s, d = x.shape
    chip = 2 * lax.axis_index("x") + lax.axis_index("y")

    small_local = [a[n][0] if a[n].ndim == 3 else a[n] for n in _SMALL_SHARDED]
    small_pack = _pack(small_local)
    core_arr = lax.axis_index("c").astype(jnp.int32).reshape(1)
    chip_arr = chip.astype(jnp.int32).reshape(1)

    def whole_cols(g):
        return g.transpose(1, 0, 2).reshape(g.shape[1], N_CHIPS * g.shape[2])

    def whole_rows(g):
        return g.reshape(N_CHIPS * g.shape[1], g.shape[2])

    w = {'mlp_w_up': [None] * 4, 'mlp_w_down': [None] * 4}
    for n in _SMALL:
        if n not in _SMALL_SHARDED:
            w[n] = a[n][0] if n.startswith('lru_') else a[n]
    w['gdn_a_log'], w['gdn_dt_bias'] = _pad_cols(w['gdn_a_log']), _pad_cols(w['gdn_dt_bias'])
    w['lru_b_rgate'] = w['lru_b_rgate'].reshape(LRU_BLOCKS, 1, -1)
    w['lru_b_igate'] = w['lru_b_igate'].reshape(LRU_BLOCKS, 1, -1)

    def gather_start(layer, after):
        ops = [a[_W_IN[layer]][0].astype(BF), a[_W_OUT[layer]][0].astype(BF), a['mlp_w_up'][layer].astype(BF),
               a['mlp_w_down'][layer].astype(BF)] + ([small_pack] if layer == 0 else [])
        split = [True] * 4 + ([False] if layer == 0 else [])
        return _chip_exchange_start('gather', ops, [(N_CHIPS,) + o.shape for o in ops], split,
                                    f"gather_start_{layer}", after), split

    def gather_finish(layer, started, after):
        started, split = started
        own, lands = _chip_exchange_wait('gather', started, split, f"gather_wait_{layer}", after)
        lands = list(_pass_to_sibling(lands[:4], f"gather_pass_{layer}")) + list(lands[4:])
        g_in, g_out, g_up, g_down = [lax.dynamic_update_slice_in_dim(l, o[None], chip, axis=0)
                                     for l, o in zip(lands[:4], own[:4])]
        w['mlp_w_up'][layer], w['mlp_w_down'][layer], w[_W_OUT[layer]] = g_up, whole_rows(g_down), whole_rows(g_out)
        if layer == 0:
            small_all = lax.dynamic_update_slice_in_dim(lands[4], own[4][None], chip, axis=0)
            for n, piece in zip(_SMALL_SHARDED, _unpack(small_all, [p.shape for p in small_local])):
                w[n] = whole_cols(piece)
            w['gla_w_gate_up'] = jnp.pad(w['gla_w_gate_up'], ((0, 128 - GLA_RANK), (0, 0)))
        if layer in (0, 3):
            w[_W_IN[layer]] = g_in
        else:
            name, tail = ('gdn', 2 * GDN_HEADS) if layer == 1 else ('gla', GLA_RANK)
            full = whole_cols(g_in)
            w[name + '_w_main'] = full[:, :full.shape[1] - tail]
            w[name + '_w_small'] = _pad_cols(full[:, full.shape[1] - tail:])

    tables = _ret_tables(s, d // RET_HEADS)
    saved = []
    h = x
    gather_finish(0, gather_start(0, x), x)
    for layer in range(4):
        nxt = gather_start(layer + 1, w[_W_OUT[layer]]) if layer < 3 else None
        h, sv = _layer_forward(layer, h, w, tables, None if nxt is None else nxt[0][4])
        saved.append(sv)
        if nxt is not None:
            gather_finish(layer + 1, nxt, h)
    loss_part, dh, g_final = _loss_head(h, w['final_norm'], target)
    loss = lax.psum(loss_part, ("x", "y", "c"))

    grad, delta, new_m, new_v = {}, {}, {}, {}
    small_grads = {'final_norm': g_final}
    norm_grads = {'norm1': [None] * 4, 'norm2': [None] * 4}
    mlp_upd = {'mlp_w_up': None, 'mlp_w_down': None}

    def update_layer(layer, started, after):
        red = _reduce_scatter_finish(started, str(layer), chip_arr, after)
        for n, (mine, theirs) in ((_W_IN[layer], red[0]), (_W_OUT[layer], red[1])):
            grad[n], delta[n], new_m[n], new_v[n] = _adamw_halves(a[n], a['m_' + n], a['v_' + n], 0, mine, theirs,
                                                                  core_arr, f"adamw_{n}")
        for n, (mine, theirs) in (('mlp_w_up', red[2]), ('mlp_w_down', red[3])):
            mlp_upd[n] = _adamw_halves(a[n], a['m_' + n], a['v_' + n], layer, mine, theirs, core_arr,
                                       f"adamw_{n}_{layer}", carried=mlp_upd[n])

    scatters = {}
    token = None
    for layer in reversed(range(4)):
        dh, g = _layer_backward(layer, dh, w, tables, saved[layer], token)
        saved[layer] = None
        scatters[layer] = _reduce_scatter_begin([g[_W_IN[layer]], g[_W_OUT[layer]], g['mlp_w_up'], g['mlp_w_down']],
                                                str(layer), core_arr)
        token = scatters[layer][4]
        norm_grads['norm1'][layer], norm_grads['norm2'][layer] = g['norm1'], g['norm2']
        for n in _SMALL:
            if n in g:
                small_grads[n] = g[n]
    small_grads['norm1'] = jnp.stack(norm_grads['norm1'])
    small_grads['norm2'] = jnp.stack(norm_grads['norm2'])

    full_shapes = [small_grads[n].shape for n in _SMALL]
    small_part = _pack([small_grads[n] for n in _SMALL])
    sibling_part = _stream_to_sibling([small_part[None]], False, "small_swap")[0][0]
    chip_part = _sum_slabs(jnp.stack([small_part, sibling_part]), F32, "sum_small_cores")
    small_started = _chip_exchange_start('gather', [chip_part], [(N_CHIPS,) + chip_part.shape], [False],
                                         "small_gather_start", core_arr)
    after = token
    for layer in (3, 2, 1):
        update_layer(layer, scatters[layer], after)
        after = new_v[_W_OUT[layer]]
    own, lands = _chip_exchange_wait('gather', small_started, [False], "small_gather_wait", after)
    total = _sum_slabs(lax.dynamic_update_slice_in_dim(lands[0], own[0][None], chip, axis=0), F32, "sum_small_chips")
    local_g = {}
    for n, full in zip(_SMALL, _unpack(total, full_shapes)):
        shp = a[n].shape
        if n in _SMALL_SHARDED:
            full = full.reshape(full.shape[0], -1)
            cq = shp[-1]
            full = lax.dynamic_slice_in_dim(full, chip * cq, cq, axis=1)
        elif n in ('gdn_a_log', 'gdn_dt_bias'):
            full = full[:, :shp[-1]]
        local_g[n] = full.reshape(shp)
    shapes = [a[n].shape for n in _SMALL]
    packed = [_pack([src[n] for n in _SMALL]) for src in
              (a, local_g, {n: a['m_' + n] for n in _SMALL}, {n: a['v_' + n] for n in _SMALL})]
    upd = _adamw(*packed, "adamw_small")
    for n, gr, dl, nm, nv in zip(_SMALL, _unpack(packed[1], shapes), *[_unpack(u, shapes) for u in upd]):
        grad[n], delta[n], new_m[n], new_v[n] = gr, dl, nm, nv
    update_layer(0, scatters[0], upd[0])
    for n in ('mlp_w_up', 'mlp_w_down'):
        grad[n], delta[n], new_m[n], new_v[n] = mlp_upd[n]

    out = [loss, dh.reshape(a['x'].shape)]
    for group in (grad, delta, new_m, new_v):
        out += [group[n].reshape(a[n].shape) for n in _WEIGHTS]
    return tuple(out)
```

```python
import functools
import math

import jax
import jax.numpy as jnp
from jax import lax
from jax.experimental import pallas as pl
from jax.experimental.pallas import tpu as pltpu

F32 = jnp.float32
BF = jnp.bfloat16
MESH = pl.DeviceIdType.MESH

NORM_EPS = 1e-6
CHUNK = 64
RET_HEADS = 8
GDN_HEADS = 16
GLA_HEADS = 4
GLA_RANK = 16
GLA_TAU = 16.0
LRU_BLOCKS = 16
LRU_C = 8.0
CONV_WIDTH = 4
ROPE_BASE = 10000.0
N_CHIPS = 4
N_DEV = 8

ADAM_LR = 0.001
ADAM_B1 = 0.9
ADAM_B2 = 0.999
ADAM_EPS = 1e-08
ADAM_WD = 0.01
ADAM_STEP = 10

VMEM_LIMIT_BYTES = 56 * 1024 * 1024
TOKEN_BLOCK = 256
ROW_BLOCK = 256
GDN_GROUP = 16


def _cparams(**kw):
    return pltpu.CompilerParams(vmem_limit_bytes=VMEM_LIMIT_BYTES, **kw)


def _bf16_parts(x, n):
    parts = []
    for _ in range(n - 1):
        p = x.astype(BF)
        parts.append(p)
        x = x - p.astype(F32)
    return parts + [x.astype(BF)]


def _raw_mm(a, b, ta, tb, hi):
    nb = a.ndim - 2
    batch = tuple(range(nb))
    dims = (((nb + (0 if ta else 1),), (nb + (1 if tb else 0),)), (batch, batch))

    def dot(p, q):
        return lax.dot_general(p, q, dims, preferred_element_type=F32)

    if not hi:
        return dot(a.astype(BF), b.astype(BF))
    if hi == 'l':
        ae = a.astype(BF)
        b0, b1, b2 = _bf16_parts(b.astype(F32), 3)
        return dot(ae, b0) + (dot(ae, b1) + dot(ae, b2))
    if hi == 'r':
        be = b.astype(BF)
        a0, a1, a2 = _bf16_parts(a.astype(F32), 3)
        return dot(a0, be) + (dot(a1, be) + dot(a2, be))
    a0, a1 = _bf16_parts(a.astype(F32), 2)
    b0, b1 = _bf16_parts(b.astype(F32), 2)
    return dot(a0, b0) + (dot(a0, b1) + dot(a1, b0))


@functools.partial(jax.custom_vjp, nondiff_argnums=(2, 3, 4))
def _mm_vjp(a, b, ta, tb, hi):
    return _raw_mm(a, b, ta, tb, hi)


def _mm_vjp_fwd(a, b, ta, tb, hi):
    return _raw_mm(a, b, ta, tb, hi), (a, b)


def _mm_vjp_bwd(ta, tb, hi, res, g):
    a, b = res
    if ta:
        da = _raw_mm(b, g, tb, True, 'l' if hi == 'r' else bool(hi))
    else:
        da = _raw_mm(g, b, False, not tb, 'r' if hi == 'r' else bool(hi))
    if tb:
        db = _raw_mm(g, a, True, ta, 'r' if hi == 'l' else bool(hi))
    else:
        db = _raw_mm(a, g, not ta, False, 'l' if hi == 'l' else bool(hi))
    return da, db


_mm_vjp.defvjp(_mm_vjp_fwd, _mm_vjp_bwd)


def _mm_diff(a, b, ta=False, tb=False, hi=False):
    return _mm_vjp(a, b, ta, tb, hi)


def _mm_plain(a, b, ta=False, tb=False, hi=False):
    return _raw_mm(a, b, ta, tb, hi)


def _shift_rows(x, k, up):
    if k == 0:
        return x
    n = x.shape[0]
    rows = lax.broadcasted_iota(jnp.int32, x.shape, 0)
    if up:
        return jnp.where(rows < n - k, pltpu.roll(x, n - k, 0), 0.0)
    return jnp.where(rows >= k, pltpu.roll(x, k, 0), 0.0)


@functools.partial(jax.custom_vjp, nondiff_argnums=(1, 2))
def _shift_vjp(x, k, up):
    return _shift_rows(x, k, up)


def _shift_vjp_fwd(x, k, up):
    return _shift_rows(x, k, up), None


def _shift_vjp_bwd(k, up, _, g):
    return (_shift_rows(g, k, not up),)


_shift_vjp.defvjp(_shift_vjp_fwd, _shift_vjp_bwd)


def _sigmoid(x):
    return 1.0 / (1.0 + jnp.exp(-x))


def _silu(x):
    return x * _sigmoid(x)


def _softplus(x):
    return jnp.maximum(x, 0.0) + jnp.log(1.0 + jnp.exp(-jnp.abs(x)))


def _gelu_tanh(x):
    return 0.5 * x * (1.0 + jnp.tanh(math.sqrt(2.0 / math.pi) * (x + 0.044715 * (x * x * x))))


def _expm1(x):
    series = x * (1.0 + x * (0.5 + x * (1.0 / 6.0 + x * (1.0 / 24.0))))
    return jnp.where(jnp.abs(x) < 0.03, series, jnp.exp(x) - 1.0)


def _rmsnorm(x, g):
    return x * lax.rsqrt(jnp.mean(x * x, axis=-1, keepdims=True) + NORM_EPS) * g


def _head_norm(o, gain, center):
    if center:
        o = o - jnp.mean(o, axis=-1, keepdims=True)
    return o * lax.rsqrt(jnp.mean(o * o, axis=-1, keepdims=True) + NORM_EPS) * gain


def _l2norm(x):
    return x * lax.rsqrt(jnp.sum(x * x, axis=-1, keepdims=True) + NORM_EPS)


def _iota2(shape, dim):
    return lax.broadcasted_iota(jnp.int32, shape, dim)


def _tri_ones(n, upper=False):
    i, j = _iota2((n, n), 0), _iota2((n, n), 1)
    return jnp.where((i <= j) if upper else (j <= i), 1.0, 0.0).astype(F32)


def _linscan(a, u, rev):
    n = a.shape[0]
    rows = _iota2(a.shape, 0)
    d = 1
    while d < n:
        if rev:
            valid = rows < n - d
            a_s, u_s = pltpu.roll(a, n - d, 0), pltpu.roll(u, n - d, 0)
        else:
            valid = rows >= d
            a_s, u_s = pltpu.roll(a, d, 0), pltpu.roll(u, d, 0)
        u = a * jnp.where(valid, u_s, 0.0) + u
        a = a * jnp.where(valid, a_s, 1.0)
        d *= 2
    return u


def _ret_group(mm, q, k, v, g, gain, state, cos, sin, dintra, qdec, kdec, cdec):
    n, _, dk = q.shape
    half = dk // 2

    def rot(t):
        t1, t2 = t[..., :half], t[..., half:]
        return jnp.concatenate([t1 * cos - t2 * sin, t1 * sin + t2 * cos], axis=-1)

    qr = rot(q)
    kr = rot(k) * (dk ** -0.5)
    o_intra = mm(mm(qr, kr, tb=True) * dintra, v)
    kv = mm(kr * kdec, v, ta=True)
    q_in = qr * qdec
    ys = []
    for i in range(n):
        o = o_intra[i] + mm(q_in[i], state)
        state = state * cdec + kv[i]
        ys.append(_head_norm(o, gain, True) * _silu(g[i]))
    return ys, state


def _gla_group(mm, q, k, v, r, glow, wgu, bias, gain, state_t):
    n, c, dk = q.shape
    logit = mm(glow.reshape(n * c, glow.shape[-1]), wgu).reshape(n, c, dk) + bias
    la = -_softplus(-logit) * (1.0 / GLA_TAU)
    cum = mm(jnp.broadcast_to(_tri_ones(c), (n, c, c)), la, hi='l')
    rows = lax.broadcasted_iota(jnp.int32, la.shape, 1)
    ref = jnp.sum(jnp.where(rows < c // 2, la, 0.0), axis=1, keepdims=True)
    tot = jnp.sum(la, axis=1, keepdims=True)
    fwd, bwd = jnp.exp(cum - ref), jnp.exp(ref - cum)
    qs = q * (dk ** -0.5)
    s_lo = mm(qs * fwd, k * bwd, tb=True)
    s_up = mm(qs * bwd, k * fwd, tb=True)
    i, j = _iota2((c, c), 0), _iota2((c, c), 1)
    o_intra = mm(jnp.where(i >= j, s_lo, s_up), v)
    q_in = qs * jnp.exp(cum)
    kv_t = mm(v, k * jnp.exp(tot - cum), ta=True)
    dec = jnp.exp(tot)
    ys = []
    for m in range(n):
        o = o_intra[m] + mm(q_in[m], state_t, tb=True)
        state_t = state_t * dec[m] + kv_t[m]
        ys.append(_head_norm(o, gain, False) * _silu(r[m]))
    return ys, state_t


def _gdn_group(mm, q, k, v, z, ba, a_log, dt_bias, gain, state, head):
    g, c, dk = q.shape
    lanes = lax.broadcasted_iota(jnp.int32, (1, 1, ba.shape[-1]), 2)
    oh_b = jnp.where(lanes == head, 1.0, 0.0).astype(F32)
    oh_a = jnp.where(lanes == head + GDN_HEADS, 1.0, 0.0).astype(F32)
    beta = _sigmoid(jnp.sum(ba * oh_b, axis=-1, keepdims=True))
    a_logit = jnp.sum(ba * oh_a, axis=-1, keepdims=True)
    a_h = jnp.sum(a_log * oh_b[0], axis=-1, keepdims=True)
    dt_h = jnp.sum(dt_bias * oh_b[0], axis=-1, keepdims=True)
    la = -jnp.exp(a_h) * _softplus(a_logit + dt_h)
    q = _l2norm(q) * (dk ** -0.5)
    k = _l2norm(k)
    tri = jnp.broadcast_to(_tri_ones(c), (g, c, c))
    tri_up = jnp.broadcast_to(_tri_ones(c, upper=True), (g, c, c))
    cum_k = mm(tri, la * jnp.ones((g, c, dk), F32), hi='l')
    la_sq = la * jnp.ones((g, c, c), F32)
    cum_i = mm(tri, la_sq, hi='l')
    cum_j = mm(la_sq, tri_up, ta=True, hi='r')
    i, j = _iota2((c, c), 0), _iota2((c, c), 1)
    strict = i > j
    rel = jnp.where(strict, jnp.exp(jnp.where(strict, cum_i - cum_j, 0.0)), 0.0)
    a_mat = beta * rel * mm(k, k, tb=True)
    inv = jnp.where(i == j, 1.0, 0.0).astype(F32) - a_mat
    power = a_mat
    for _ in range(int(math.log2(c)) - 1):
        power = mm(power, power, hi=True)
        inv = inv + mm(inv, power, hi=True)
    tot = jnp.sum(la, axis=1, keepdims=True)
    u = mm(inv, beta * v, hi=True)
    w = mm(inv, (beta * jnp.exp(cum_k)) * k, hi=True)
    k_end = k * jnp.exp(tot - cum_k)
    di, dj = _iota2((dk, dk), 0), _iota2((dk, dk), 1)
    trans = jnp.exp(tot) * jnp.where(di == dj, 1.0, 0.0).astype(F32) - mm(k_end, w, ta=True)
    inject = mm(k_end, u, ta=True)
    ys = []
    for n in range(g):
        state = mm(trans[n], state) + inject[n]
        ys.append(_head_norm(mm(q[n], state), gain, False) * _silu(z[n]))
    return ys, state


def _conv_taps(shift, x, taps):
    out = None
    for tap, w in enumerate(taps):
        term = shift(x, CONV_WIDTH - 1 - tap, False) * w
        out = term if out is None else out + term
    return out


def _lru_pre(mm, shift, xb, yb, taps, cb, lam, wr, br, wi, bi):
    xb = _conv_taps(shift, xb, taps) + cb
    r = _sigmoid(mm(xb, wr) + br)
    i = _sigmoid(mm(xb, wi) + bi)
    log_a = (-LRU_C) * _softplus(-lam) * r
    a = jnp.exp(log_a)
    u = jnp.sqrt(-_expm1(2.0 * log_a)) * (i * xb)
    return a, u, _gelu_tanh(yb)


_TOKEN_SPEC = pl.BlockSpec((8, 128), lambda *_: (0, 0))


MATMUL_VMEM_BUDGET = 40 * 1024 * 1024
_TILE_SIZES = (3072, 2048, 1536, 1024, 768, 512, 384, 256, 128)


def _tile_choices(extent):
    return [t for t in _TILE_SIZES if t <= extent and extent % t == 0] or [extent]


def _matmul_tiles(m, n, kdim, n_unit, k_unit, a_item, b_item, o_item, e_item):
    best = None
    for bm in _tile_choices(m):
        for bn in _tile_choices(n_unit):
            for bk in _tile_choices(k_unit):
                nk = kdim // bk
                vmem = 2 * (bm * bk * a_item + bk * bn * b_item + bm * bn * (o_item + e_item)) + bm * bn * 4 * (2 if nk > 1 else 1)
                if vmem > MATMUL_VMEM_BUDGET:
                    continue
                steps = (m // bm) * (n // bn) * nk
                traffic = m * kdim * a_item * (1 if nk == 1 else n // bn) + kdim * n * b_item * (m // bm)
                if best is None or (steps, traffic) < best[0]:
                    best = ((steps, traffic), (bm, bn, bk))
    return best[1]


def _matmul(a, b, *, name, ta=False, tb=False, out_dtype=F32, b_sharded=False, o_sharded=False,
            epilogue=None, extra=None, token=None):
    m, kdim = (a.shape[1], a.shape[0]) if ta else a.shape
    if b_sharded:
        nq = b.shape[2]
        n = b.shape[1] if tb else N_CHIPS * nq
        assert kdim == (N_CHIPS * nq if tb else b.shape[1])
    else:
        n = b.shape[0] if tb else b.shape[1]
        assert kdim == (b.shape[1] if tb else b.shape[0])
    n_unit = b.shape[2] if (b_sharded and not tb) else (n // N_CHIPS if o_sharded else n)
    k_unit = b.shape[2] if (b_sharded and tb) else kdim
    bm, bn, bk = _matmul_tiles(m, n, kdim, n_unit, k_unit, a.dtype.itemsize, b.dtype.itemsize,
                               jnp.dtype(out_dtype).itemsize, 0 if extra is None else extra.dtype.itemsize)
    nk = kdim // bk
    grid = (m // bm, n // bn, nk)

    a_spec = pl.BlockSpec((bk, bm), lambda i, j, k: (k, i)) if ta else pl.BlockSpec((bm, bk), lambda i, j, k: (i, k))
    if b_sharded and not tb:
        per = b.shape[2] // bn
        b_spec = pl.BlockSpec((None, bk, bn), lambda i, j, k: (j // per, k, j % per))
    elif b_sharded:
        per = b.shape[2] // bk
        b_spec = pl.BlockSpec((None, bn, bk), lambda i, j, k: (k // per, j, k % per))
    elif tb:
        b_spec = pl.BlockSpec((bn, bk), lambda i, j, k: (j, k))
    else:
        b_spec = pl.BlockSpec((bk, bn), lambda i, j, k: (k, j))
    if o_sharded:
        per_o = (n // N_CHIPS) // bn
        o_spec = pl.BlockSpec((None, bm, bn), lambda i, j, k: (j // per_o, i, j % per_o))
        out_shape = jax.ShapeDtypeStruct((N_CHIPS, m, n // N_CHIPS), out_dtype)
    else:
        o_spec = pl.BlockSpec((bm, bn), lambda i, j, k: (i, j))
        out_shape = jax.ShapeDtypeStruct((m, n), out_dtype)
    in_specs = [a_spec, b_spec]
    operands = [a, b]
    if extra is not None:
        in_specs.append(pl.BlockSpec((bm, bn), lambda i, j, k: (i, j)))
        operands.append(extra)
    if token is not None:
        in_specs.append(_TOKEN_SPEC)
        operands.append(token)

    def finish(acc, e_ref, o_ref):
        if epilogue == 'add':
            acc = acc + e_ref[...].astype(F32)
        elif epilogue == 'relu2':
            acc = jnp.square(jnp.maximum(acc, 0.0))
        elif epilogue == 'dact':
            acc = acc * (2.0 * jnp.sqrt(e_ref[...].astype(F32)))
        o_ref[...] = acc.astype(o_ref.dtype)

    def body_one_step(*refs):
        finish(_raw_mm(refs[0][...], refs[1][...], ta, tb, False), refs[2] if extra is not None else None, refs[-1])

    def body(*refs):
        a_ref, b_ref = refs[0], refs[1]
        e_ref = refs[2] if extra is not None else None
        o_ref, acc_ref = refs[-2], refs[-1]
        k = pl.program_id(2)

        @pl.when(k == 0)
        def _():
            acc_ref[...] = jnp.zeros_like(acc_ref)

        acc_ref[...] += _raw_mm(a_ref[...], b_ref[...], ta, tb, False)

        @pl.when(k == nk - 1)
        def _():
            finish(acc_ref[...], e_ref, o_ref)

    return pl.pallas_call(
        body_one_step if nk == 1 else body, name=name, grid=grid, in_specs=in_specs, out_specs=o_spec,
        out_shape=out_shape, scratch_shapes=[] if nk == 1 else [pltpu.VMEM((bm, bn), F32)],
        compiler_params=_cparams(),
    )(*operands)


def _rmsnorm_fwd(x, g, name, token=None):
    s, d = x.shape

    def body(x_ref, g_ref, *rest):
        rest[-1][...] = _rmsnorm(x_ref[...], g_ref[...]).astype(BF)

    return pl.pallas_call(
        body, name=name, grid=(s // ROW_BLOCK,),
        in_specs=[pl.BlockSpec((ROW_BLOCK, d), lambda i: (i, 0)), pl.BlockSpec((1, d), lambda i: (0, 0))]
        + ([] if token is None else [_TOKEN_SPEC]),
        out_specs=pl.BlockSpec((ROW_BLOCK, d), lambda i: (i, 0)),
        out_shape=jax.ShapeDtypeStruct((s, d), BF), compiler_params=_cparams(),
    )(x, g.reshape(1, d), *([] if token is None else [token]))


def _rmsnorm_bwd(x, g, dh, dres, name):
    s, d = x.shape

    def body(x_ref, g_ref, dh_ref, dres_ref, dx_ref, dg_ref):
        _, vjp = jax.vjp(_rmsnorm, x_ref[...], g_ref[...])
        dx, dg = vjp(dh_ref[...].astype(F32))
        dx_ref[...] = dres_ref[...] + dx

        @pl.when(pl.program_id(0) == 0)
        def _():
            dg_ref[...] = jnp.zeros_like(dg_ref)

        dg_ref[...] += dg

    row = pl.BlockSpec((ROW_BLOCK, d), lambda i: (i, 0))
    vec = pl.BlockSpec((1, d), lambda i: (0, 0))
    dx, dg = pl.pallas_call(
        body, name=name, grid=(s // ROW_BLOCK,), in_specs=[row, vec, row, row], out_specs=[row, vec],
        out_shape=[jax.ShapeDtypeStruct((s, d), F32), jax.ShapeDtypeStruct((1, d), F32)],
        compiler_params=_cparams(),
    )(x, g.reshape(1, d), dh, dres)
    return dx, dg.reshape(d)


def _loss_head(x, g, target):
    s, d = x.shape

    def loss_fn(xv, gv, tv):
        err = _rmsnorm(xv, gv) - tv
        return 0.5 * jnp.sum(jnp.mean(err * err, axis=-1, keepdims=True), axis=0, keepdims=True)

    def body(x_ref, g_ref, t_ref, dx_ref, dg_ref, loss_ref):
        tv = t_ref[...]
        loss, vjp = jax.vjp(lambda xv, gv: loss_fn(xv, gv, tv), x_ref[...], g_ref[...])
        dx, dg = vjp(jnp.ones((1, 1), F32))
        dx_ref[...] = dx

        @pl.when(pl.program_id(0) == 0)
        def _():
            dg_ref[...] = jnp.zeros_like(dg_ref)
            loss_ref[...] = jnp.zeros_like(loss_ref)

        dg_ref[...] += dg
        loss_ref[...] += loss * jnp.ones_like(loss_ref)

    row = pl.BlockSpec((ROW_BLOCK, d), lambda i: (i, 0))
    vec = pl.BlockSpec((1, d), lambda i: (0, 0))
    dx, dg, loss = pl.pallas_call(
        body, name="loss_head", grid=(s // ROW_BLOCK,), in_specs=[row, vec, row],
        out_specs=[row, vec, pl.BlockSpec((1, 128), lambda i: (0, 0))],
        out_shape=[jax.ShapeDtypeStruct((s, d), F32), jax.ShapeDtypeStruct((1, d), F32),
                   jax.ShapeDtypeStruct((1, 128), F32)],
        compiler_params=_cparams(),
    )(x, g.reshape(1, d), target)
    return loss[0, 0], dx, dg.reshape(d)


def _ret_tables(s, dk):
    h = jnp.arange(RET_HEADS, dtype=F32)
    log_gamma = jnp.log1p(-jnp.exp2(-5.0 - h))
    pos = jnp.arange(CHUNK, dtype=F32)
    dist = jnp.abs(pos[:, None] - pos[None, :])
    dintra = jnp.exp(log_gamma[:, None, None] * dist)
    qdec = jnp.exp(log_gamma[:, None] * (pos + 1.0))[:, :, None]
    kdec = jnp.exp(log_gamma[:, None] * (CHUNK - 1.0 - pos))[:, :, None]
    cdec = jnp.exp(log_gamma * CHUNK)[:, None, None]
    inv = ROPE_BASE ** (-jnp.arange(0, dk, 2, dtype=F32) / dk)
    ang = jnp.arange(s, dtype=F32)[:, None] * inv[None, :]
    return jnp.cos(ang), jnp.sin(ang), dintra, qdec, kdec, cdec


def _ret_specs(s, dk, dv, tb, rev):
    nh = RET_HEADS
    nb = s // tb
    bi = (lambda b: nb - 1 - b) if rev else (lambda b: b)
    voff = 2 * nh * dk // dv
    cpb = tb // CHUNK
    return dict(
        q=pl.BlockSpec((tb, dk), lambda h, b: (bi(b), h)),
        k=pl.BlockSpec((tb, dk), lambda h, b: (bi(b), nh + h)),
        v=pl.BlockSpec((tb, dv), lambda h, b: (bi(b), voff + h)),
        g=pl.BlockSpec((tb, dv), lambda h, b: (bi(b), voff + nh + h)),
        gain=pl.BlockSpec((None, 1, dv), lambda h, b: (h, 0, 0)),
        cs=pl.BlockSpec((tb, dk // 2), lambda h, b: (bi(b), 0)),
        dintra=pl.BlockSpec((None, CHUNK, CHUNK), lambda h, b: (h, 0, 0)),
        dec=pl.BlockSpec((None, CHUNK, 1), lambda h, b: (h, 0, 0)),
        cdec=pl.BlockSpec((None, 1, 1), lambda h, b: (h, 0, 0)),
        hv=pl.BlockSpec((tb, dv), lambda h, b: (bi(b), h)),
        hk=pl.BlockSpec((tb, dk), lambda h, b: (bi(b), h)),
        st=pl.BlockSpec((None, None, dk, dv), lambda h, b: (h, bi(b), 0, 0)),
    )


def _group(t):
    return t.reshape(t.shape[0] // CHUNK, CHUNK, t.shape[-1])


def _ret_fwd(proj, gain, tables):
    s = proj.shape[0]
    d = proj.shape[1] // 6
    dk, dv = d // RET_HEADS, 2 * d // RET_HEADS
    tb = min(TOKEN_BLOCK, s)
    sp = _ret_specs(s, dk, dv, tb, False)
    cos, sin, dintra, qdec, kdec, cdec = tables

    def body(q_ref, k_ref, v_ref, g_ref, gain_ref, cos_ref, sin_ref, di_ref, qd_ref, kd_ref, cd_ref,
             y_ref, st_ref, state):
        @pl.when(pl.program_id(1) == 0)
        def _():
            state[...] = jnp.zeros_like(state)

        st_ref[...] = state[...]
        ys, new_state = _ret_group(_mm_plain, _group(q_ref[...]), _group(k_ref[...]), _group(v_ref[...]),
                                   _group(g_ref[...]), gain_ref[...], state[...], _group(cos_ref[...]),
                                   _group(sin_ref[...]), di_ref[...], qd_ref[...], kd_ref[...], cd_ref[...])
        for c, y in enumerate(ys):
            y_ref[pl.ds(c * CHUNK, CHUNK), :] = y.astype(BF)
        state[...] = new_state

    return pl.pallas_call(
        body, name="ret_fwd", grid=(RET_HEADS, s // tb),
        in_specs=[sp['q'], sp['k'], sp['v'], sp['g'], sp['gain'], sp['cs'], sp['cs'], sp['dintra'], sp['dec'],
                  sp['dec'], sp['cdec']],
        out_specs=[sp['hv'], sp['st']],
        out_shape=[jax.ShapeDtypeStruct((s, RET_HEADS * dv), BF),
                   jax.ShapeDtypeStruct((RET_HEADS, s // tb, dk, dv), F32)],
        scratch_shapes=[pltpu.VMEM((dk, dv), F32)], compiler_params=_cparams(),
    )(proj, proj, proj, proj, gain.reshape(RET_HEADS, 1, dv), cos, sin, dintra, qdec, kdec, cdec)


def _ret_bwd(proj, gain, tables, states, dy):
    s = proj.shape[0]
    d = proj.shape[1] // 6
    dk, dv = d // RET_HEADS, 2 * d // RET_HEADS
    tb = min(TOKEN_BLOCK, s)
    cpb = tb // CHUNK
    sp = _ret_specs(s, dk, dv, tb, True)
    cos, sin, dintra, qdec, kdec, cdec = tables

    def body(q_ref, k_ref, v_ref, g_ref, gain_ref, cos_ref, sin_ref, di_ref, qd_ref, kd_ref, cd_ref, st_ref,
             dy_ref, dq_ref, dk_ref, dv_ref, dg_ref, dgain_ref, dstate):
        @pl.when(pl.program_id(1) == 0)
        def _():
            dstate[...] = jnp.zeros_like(dstate)
            dgain_ref[...] = jnp.zeros_like(dgain_ref)

        cos_g, sin_g = _group(cos_ref[...]), _group(sin_ref[...])
        di, qd, kd, cd = di_ref[...], qd_ref[...], kd_ref[...], cd_ref[...]

        def fn(q, k, v, g, gn, st):
            return _ret_group(_mm_diff, q, k, v, g, gn, st, cos_g, sin_g, di, qd, kd, cd)

        _, vjp = jax.vjp(fn, _group(q_ref[...]), _group(k_ref[...]), _group(v_ref[...]), _group(g_ref[...]),
                         gain_ref[...], st_ref[...])
        dys = [dy_ref[pl.ds(c * CHUNK, CHUNK), :].astype(F32) for c in range(cpb)]
        dq, dkk, dvv, dg, dgn, dst = vjp((dys, dstate[...]))
        dq_ref[...] = dq.reshape(tb, dk).astype(BF)
        dk_ref[...] = dkk.reshape(tb, dk).astype(BF)
        dv_ref[...] = dvv.reshape(tb, dv).astype(BF)
        dg_ref[...] = dg.reshape(tb, dv).astype(BF)
        dgain_ref[...] += dgn
        dstate[...] = dst

    dq, dkk, dvv, dg, dgain = pl.pallas_call(
        body, name="ret_bwd", grid=(RET_HEADS, s // tb),
        in_specs=[sp['q'], sp['k'], sp['v'], sp['g'], sp['gain'], sp['cs'], sp['cs'], sp['dintra'], sp['dec'],
                  sp['dec'], sp['cdec'], sp['st'], sp['hv']],
        out_specs=[sp['hk'], sp['hk'], sp['hv'], sp['hv'], sp['gain']],
        out_shape=[jax.ShapeDtypeStruct((s, RET_HEADS * dk), BF), jax.ShapeDtypeStruct((s, RET_HEADS * dk), BF),
                   jax.ShapeDtypeStruct((s, RET_HEADS * dv), BF), jax.ShapeDtypeStruct((s, RET_HEADS * dv), BF),
                   jax.ShapeDtypeStruct((RET_HEADS, 1, dv), F32)],
        scratch_shapes=[pltpu.VMEM((dk, dv), F32)], compiler_params=_cparams(),
    )(proj, proj, proj, proj, gain.reshape(RET_HEADS, 1, dv), cos, sin, dintra, qdec, kdec, cdec, states, dy)
    return jnp.concatenate([dq, dkk, dvv, dg], axis=1), dgain.reshape(RET_HEADS, dv)


def _gla_specs(s, dk, dv, tb, rev):
    nh = GLA_HEADS
    nb = s // tb
    bi = (lambda b: nb - 1 - b) if rev else (lambda b: b)
    voff = 2 * nh * dk // dv
    cpb = tb // CHUNK
    return dict(
        q=pl.BlockSpec((tb, dk), lambda h, b: (bi(b), h)),
        k=pl.BlockSpec((tb, dk), lambda h, b: (bi(b), nh + h)),
        v=pl.BlockSpec((tb, dv), lambda h, b: (bi(b), voff + h)),
        r=pl.BlockSpec((tb, dv), lambda h, b: (bi(b), voff + nh + h)),
        glow=pl.BlockSpec((tb, 128), lambda h, b: (bi(b), 0)),
        wgu=pl.BlockSpec((128, dk), lambda h, b: (0, h)),
        bias=pl.BlockSpec((1, dk), lambda h, b: (0, h)),
        gain=pl.BlockSpec((None, 1, dv), lambda h, b: (h, 0, 0)),
        hv=pl.BlockSpec((tb, dv), lambda h, b: (bi(b), h)),
        hk=pl.BlockSpec((tb, dk), lambda h, b: (bi(b), h)),
        hg=pl.BlockSpec((tb, 128), lambda h, b: (bi(b), h)),
        st=pl.BlockSpec((None, None, dv, dk), lambda h, b: (h, bi(b), 0, 0)),
    )


def _gla_fwd(proj, glow, wgu, bias, gain):
    s = proj.shape[0]
    d = proj.shape[1] // 3
    dk, dv = d // 2 // GLA_HEADS, d // GLA_HEADS
    tb = min(TOKEN_BLOCK, s)
    sp = _gla_specs(s, dk, dv, tb, False)

    def body(q_ref, k_ref, v_ref, r_ref, gl_ref, wgu_ref, b_ref, gain_ref, y_ref, st_ref, state):
        @pl.when(pl.program_id(1) == 0)
        def _():
            state[...] = jnp.zeros_like(state)

        st_ref[...] = state[...]
        ys, new_state = _gla_group(_mm_plain, _group(q_ref[...]), _group(k_ref[...]), _group(v_ref[...]),
                                   _group(r_ref[...]), _group(gl_ref[...]), wgu_ref[...], b_ref[...], gain_ref[...],
                                   state[...])
        for c, y in enumerate(ys):
            y_ref[pl.ds(c * CHUNK, CHUNK), :] = y.astype(BF)
        state[...] = new_state

    return pl.pallas_call(
        body, name="gla_fwd", grid=(GLA_HEADS, s // tb),
        in_specs=[sp['q'], sp['k'], sp['v'], sp['r'], sp['glow'], sp['wgu'], sp['bias'], sp['gain']],
        out_specs=[sp['hv'], sp['st']],
        out_shape=[jax.ShapeDtypeStruct((s, GLA_HEADS * dv), BF),
                   jax.ShapeDtypeStruct((GLA_HEADS, s // tb, dv, dk), F32)],
        scratch_shapes=[pltpu.VMEM((dv, dk), F32)], compiler_params=_cparams(),
    )(proj, proj, proj, proj, glow, wgu, bias, gain.reshape(GLA_HEADS, 1, dv))


def _gla_bwd(proj, glow, wgu, bias, gain, states, dy):
    s = proj.shape[0]
    d = proj.shape[1] // 3
    dk, dv = d // 2 // GLA_HEADS, d // GLA_HEADS
    tb = min(TOKEN_BLOCK, s)
    cpb = tb // CHUNK
    sp = _gla_specs(s, dk, dv, tb, True)

    def body(q_ref, k_ref, v_ref, r_ref, gl_ref, wgu_ref, b_ref, gain_ref, st_ref, dy_ref,
             dq_ref, dk_ref, dv_ref, dr_ref, dgl_ref, dwgu_ref, db_ref, dgain_ref, dstate):
        @pl.when(pl.program_id(1) == 0)
        def _():
            dstate[...] = jnp.zeros_like(dstate)
            dwgu_ref[...] = jnp.zeros_like(dwgu_ref)
            db_ref[...] = jnp.zeros_like(db_ref)
            dgain_ref[...] = jnp.zeros_like(dgain_ref)

        def fn(q, k, v, r, gl, w, b, gn, st):
            return _gla_group(_mm_diff, q, k, v, r, gl, w, b, gn, st)

        _, vjp = jax.vjp(fn, _group(q_ref[...]), _group(k_ref[...]), _group(v_ref[...]), _group(r_ref[...]),
                         _group(gl_ref[...]), wgu_ref[...], b_ref[...], gain_ref[...], st_ref[...])
        dys = [dy_ref[pl.ds(c * CHUNK, CHUNK), :].astype(F32) for c in range(cpb)]
        dq, dkk, dvv, dr, dgl, dw, db, dgn, dst = vjp((dys, dstate[...]))
        dq_ref[...] = dq.reshape(tb, dk).astype(BF)
        dk_ref[...] = dkk.reshape(tb, dk).astype(BF)
        dv_ref[...] = dvv.reshape(tb, dv).astype(BF)
        dr_ref[...] = dr.reshape(tb, dv).astype(BF)
        dgl_ref[...] = dgl.reshape(tb, dgl.shape[-1]).astype(BF)
        dwgu_ref[...] += dw
        db_ref[...] += db
        dgain_ref[...] += dgn
        dstate[...] = dst

    nh = GLA_HEADS
    dq, dkk, dvv, dr, dgl, dwgu, db, dgain = pl.pallas_call(
        body, name="gla_bwd", grid=(nh, s // tb),
        in_specs=[sp['q'], sp['k'], sp['v'], sp['r'], sp['glow'], sp['wgu'], sp['bias'], sp['gain'], sp['st'],
                  sp['hv']],
        out_specs=[sp['hk'], sp['hk'], sp['hv'], sp['hv'], sp['hg'], sp['wgu'], sp['bias'], sp['gain']],
        out_shape=[jax.ShapeDtypeStruct((s, nh * dk), BF), jax.ShapeDtypeStruct((s, nh * dk), BF),
                   jax.ShapeDtypeStruct((s, nh * dv), BF), jax.ShapeDtypeStruct((s, nh * dv), BF),
                   jax.ShapeDtypeStruct((s, nh * 128), BF), jax.ShapeDtypeStruct((128, nh * dk), F32),
                   jax.ShapeDtypeStruct((1, nh * dk), F32), jax.ShapeDtypeStruct((nh, 1, dv), F32)],
        scratch_shapes=[pltpu.VMEM((dv, dk), F32)], compiler_params=_cparams(),
    )(proj, proj, proj, proj, glow, wgu, bias, gain.reshape(nh, 1, dv), states, dy)
    return jnp.concatenate([dq, dkk, dvv, dr], axis=1), dgl, dwgu, db, dgain.reshape(nh, dv)


def _gdn_specs(s, dk):
    nh = GDN_HEADS
    col = lambda off: pl.BlockSpec((s, dk), lambda h: (0, off + h))
    tap = lambda off: pl.BlockSpec((CONV_WIDTH, dk), lambda h: (0, off + h))
    vec = pl.BlockSpec((1, 128), lambda h: (0, 0))
    return dict(q=col(0), k=col(nh), v=col(2 * nh), z=col(3 * nh), ba=pl.BlockSpec((s, 128), lambda h: (0, 0)),
                cq=tap(0), ck=tap(nh), cv=tap(2 * nh), vec=vec, gain=pl.BlockSpec((1, dk), lambda h: (0, 0)),
                head=col(0))


def _conv_silu(shift, x, taps):
    return _silu(_conv_taps(shift, x, taps))


def _load_taps(ref):
    return [ref[t:t + 1, :] for t in range(CONV_WIDTH)]


def _gdn_fwd(proj, ba, conv_w, a_log, dt_bias, gain):
    s = proj.shape[0]
    dk = proj.shape[1] // (4 * GDN_HEADS)
    sp = _gdn_specs(s, dk)
    rows = GDN_GROUP * CHUNK

    def grp(t):
        return t.reshape(GDN_GROUP, CHUNK, t.shape[-1])

    ngroups = s // rows

    def body(q_ref, k_ref, v_ref, z_ref, ba_ref, cq_ref, ck_ref, cv_ref, al_ref, dt_ref, gain_ref, y_ref, states,
             qc, kc, vc, state):
        head = pl.program_id(0)
        qc[...] = _conv_silu(_shift_rows, q_ref[...], _load_taps(cq_ref))
        kc[...] = _conv_silu(_shift_rows, k_ref[...], _load_taps(ck_ref))
        vc[...] = _conv_silu(_shift_rows, v_ref[...], _load_taps(cv_ref))
        state[...] = jnp.zeros_like(state)

        def step(n, carry):
            base = pl.multiple_of(n * rows, rows)
            sl = pl.ds(base, rows)
            states[n] = state[...]
            ys, new_state = _gdn_group(_mm_plain, grp(qc[sl, :]), grp(kc[sl, :]), grp(vc[sl, :]), grp(z_ref[sl, :]),
                                       grp(ba_ref[sl, :]), al_ref[...], dt_ref[...], gain_ref[...], state[...], head)
            for i, y in enumerate(ys):
                y_ref[pl.ds(base + i * CHUNK, CHUNK), :] = y.astype(BF)
            state[...] = new_state
            return carry

        lax.fori_loop(0, ngroups, step, 0)

    return pl.pallas_call(
        body, name="gdn_fwd", grid=(GDN_HEADS,),
        in_specs=[sp['q'], sp['k'], sp['v'], sp['z'], sp['ba'], sp['cq'], sp['ck'], sp['cv'], sp['vec'], sp['vec'],
                  sp['gain']],
        out_specs=[sp['head'], pl.BlockSpec((None, ngroups, dk, dk), lambda h: (h, 0, 0, 0))],
        out_shape=[jax.ShapeDtypeStruct((s, GDN_HEADS * dk), BF),
                   jax.ShapeDtypeStruct((GDN_HEADS, ngroups, dk, dk), F32)],
        scratch_shapes=[pltpu.VMEM((s, dk), F32)] * 3 + [pltpu.VMEM((dk, dk), F32)],
        compiler_params=_cparams(),
    )(proj, proj, proj, proj, ba, conv_w, conv_w, conv_w, a_log, dt_bias, gain)


def _gdn_bwd(proj, ba, conv_w, a_log, dt_bias, gain, group_states, dy):
    s = proj.shape[0]
    dk = proj.shape[1] // (4 * GDN_HEADS)
    sp = _gdn_specs(s, dk)
    rows = GDN_GROUP * CHUNK
    ngroups = s // rows

    def grp(t):
        return t.reshape(GDN_GROUP, CHUNK, t.shape[-1])

    def body(q_ref, k_ref, v_ref, z_ref, ba_ref, cq_ref, ck_ref, cv_ref, al_ref, dt_ref, gain_ref, states, dy_ref,
             dq_ref, dk_ref, dv_ref, dz_ref, dba_ref, dcq_ref, dck_ref, dcv_ref, dal_ref, ddt_ref, dgain_ref,
             qc, kc, vc, dqc, dkc, dvc, dstate):
        head = pl.program_id(0)

        @pl.when(head == 0)
        def _():
            dba_ref[...] = jnp.zeros_like(dba_ref)
            dal_ref[...] = jnp.zeros_like(dal_ref)
            ddt_ref[...] = jnp.zeros_like(ddt_ref)
            dgain_ref[...] = jnp.zeros_like(dgain_ref)

        qc[...] = _conv_silu(_shift_rows, q_ref[...], _load_taps(cq_ref))
        kc[...] = _conv_silu(_shift_rows, k_ref[...], _load_taps(ck_ref))
        vc[...] = _conv_silu(_shift_rows, v_ref[...], _load_taps(cv_ref))
        dstate[...] = jnp.zeros_like(dstate)

        def bstep(i, carry):
            n = ngroups - 1 - i
            base = pl.multiple_of(n * rows, rows)
            sl = pl.ds(base, rows)

            def fn(q, k, v, z, b, al, dt, gn, st):
                return _gdn_group(_mm_diff, q, k, v, z, b, al, dt, gn, st, head)

            _, vjp = jax.vjp(fn, grp(qc[sl, :]), grp(kc[sl, :]), grp(vc[sl, :]), grp(z_ref[sl, :]),
                             grp(ba_ref[sl, :]), al_ref[...], dt_ref[...], gain_ref[...], states[n])
            dys = [dy_ref[pl.ds(base + j * CHUNK, CHUNK), :].astype(F32) for j in range(GDN_GROUP)]
            dq, dkk, dvv, dz, db, dal, ddt, dgn, dst = vjp((dys, dstate[...]))
            dqc[sl, :] = dq.reshape(rows, dk)
            dkc[sl, :] = dkk.reshape(rows, dk)
            dvc[sl, :] = dvv.reshape(rows, dk)
            dz_ref[sl, :] = dz.reshape(rows, dk).astype(BF)
            dba_ref[sl, :] += db.reshape(rows, db.shape[-1])
            dal_ref[...] += dal
            ddt_ref[...] += ddt
            dgain_ref[...] += dgn
            dstate[...] = dst
            return carry

        lax.fori_loop(0, ngroups, bstep, 0)

        for x_ref, c_ref, dpost, dx_ref, dc_ref in ((q_ref, cq_ref, dqc, dq_ref, dcq_ref),
                                                    (k_ref, ck_ref, dkc, dk_ref, dck_ref),
                                                    (v_ref, cv_ref, dvc, dv_ref, dcv_ref)):
            _, vjp = jax.vjp(lambda x, *taps: _conv_silu(_shift_vjp, x, taps), x_ref[...], *_load_taps(c_ref))
            grads = vjp(dpost[...])
            dx_ref[...] = grads[0].astype(BF)
            for t in range(CONV_WIDTH):
                dc_ref[t:t + 1, :] = grads[1 + t]

    nh = GDN_HEADS
    col_bf = jax.ShapeDtypeStruct((s, nh * dk), BF)
    tap_out = jax.ShapeDtypeStruct((CONV_WIDTH, nh * dk), F32)
    tap_spec = pl.BlockSpec((CONV_WIDTH, dk), lambda h: (0, h))
    dq, dkk, dvv, dz, dba, dcq, dck, dcv, dal, ddt, dgain = pl.pallas_call(
        body, name="gdn_bwd", grid=(nh,),
        in_specs=[sp['q'], sp['k'], sp['v'], sp['z'], sp['ba'], sp['cq'], sp['ck'], sp['cv'], sp['vec'], sp['vec'],
                  sp['gain'], pl.BlockSpec((None, ngroups, dk, dk), lambda h: (h, 0, 0, 0)), sp['head']],
        out_specs=[sp['head']] * 4 + [sp['ba'], tap_spec, tap_spec, tap_spec, sp['vec'], sp['vec'], sp['gain']],
        out_shape=[col_bf] * 4 + [jax.ShapeDtypeStruct((s, 128), F32), tap_out, tap_out, tap_out,
                                  jax.ShapeDtypeStruct((1, 128), F32), jax.ShapeDtypeStruct((1, 128), F32),
                                  jax.ShapeDtypeStruct((1, dk), F32)],
        scratch_shapes=[pltpu.VMEM((s, dk), F32)] * 6 + [pltpu.VMEM((dk, dk), F32)],
        compiler_params=_cparams(),
    )(proj, proj, proj, proj, ba, conv_w, conv_w, conv_w, a_log, dt_bias, gain, group_states, dy)
    return (jnp.concatenate([dq, dkk, dvv, dz], axis=1), dba, jnp.concatenate([dcq, dck, dcv], axis=1), dal, ddt,
            dgain)


def _lru_specs(s, bw):
    nb = LRU_BLOCKS
    return dict(
        xb=pl.BlockSpec((s, bw), lambda n: (0, n)), yb=pl.BlockSpec((s, bw), lambda n: (0, nb + n)),
        taps=pl.BlockSpec((CONV_WIDTH, bw), lambda n: (0, n)), vec=pl.BlockSpec((1, bw), lambda n: (0, n)),
        w=pl.BlockSpec((None, bw, bw), lambda n: (n, 0, 0)), b=pl.BlockSpec((None, 1, bw), lambda n: (n, 0, 0)),
        col=pl.BlockSpec((s, bw), lambda n: (0, n)))


def _lru_fwd(proj, conv_w, conv_b, lam, wr, br, wi, bi):
    s = proj.shape[0]
    bw = proj.shape[1] // (2 * LRU_BLOCKS)
    sp = _lru_specs(s, bw)

    def body(xb_ref, yb_ref, cw_ref, cb_ref, lam_ref, wr_ref, br_ref, wi_ref, bi_ref, y_ref):
        a, u, gy = _lru_pre(_mm_plain, _shift_rows, xb_ref[...], yb_ref[...], _load_taps(cw_ref), cb_ref[...],
                            lam_ref[...], wr_ref[...], br_ref[...], wi_ref[...], bi_ref[...])
        y_ref[...] = (_linscan(a, u, False) * gy).astype(BF)

    return pl.pallas_call(
        body, name="lru_fwd", grid=(LRU_BLOCKS,),
        in_specs=[sp['xb'], sp['yb'], sp['taps'], sp['vec'], sp['vec'], sp['w'], sp['b'], sp['w'], sp['b']],
        out_specs=sp['col'], out_shape=jax.ShapeDtypeStruct((s, LRU_BLOCKS * bw), BF),
        compiler_params=_cparams(),
    )(proj, proj, conv_w, conv_b, lam, wr, br, wi, bi)


def _lru_bwd(proj, conv_w, conv_b, lam, wr, br, wi, bi, dy):
    s = proj.shape[0]
    nb = LRU_BLOCKS
    bw = proj.shape[1] // (2 * nb)
    sp = _lru_specs(s, bw)

    def body(xb_ref, yb_ref, cw_ref, cb_ref, lam_ref, wr_ref, br_ref, wi_ref, bi_ref, dy_ref,
             dxb_ref, dyb_ref, dcw_ref, dcb_ref, dlam_ref, dwr_ref, dbr_ref, dwi_ref, dbi_ref):
        def pre(xb, yb, t0, t1, t2, t3, cb, lm, w_r, b_r, w_i, b_i):
            return _lru_pre(_mm_diff, _shift_vjp, xb, yb, (t0, t1, t2, t3), cb, lm, w_r, b_r, w_i, b_i)

        (a, u, gy), vjp = jax.vjp(pre, xb_ref[...], yb_ref[...], *_load_taps(cw_ref), cb_ref[...], lam_ref[...],
                                  wr_ref[...], br_ref[...], wi_ref[...], bi_ref[...])
        h = _linscan(a, u, False)
        dout = dy_ref[...].astype(F32)
        g = _linscan(_shift_rows(a, 1, True), dout * gy, True)
        grads = vjp((g * _shift_rows(h, 1, False), g, dout * h))
        dxb_ref[...] = grads[0].astype(BF)
        dyb_ref[...] = grads[1].astype(BF)
        for t in range(CONV_WIDTH):
            dcw_ref[t:t + 1, :] = grads[2 + t]
        dcb_ref[...] = grads[6]
        dlam_ref[...] = grads[7]
        dwr_ref[...] = grads[8]
        dbr_ref[...] = grads[9]
        dwi_ref[...] = grads[10]
        dbi_ref[...] = grads[11]

    outs = pl.pallas_call(
        body, name="lru_bwd", grid=(nb,),
        in_specs=[sp['xb'], sp['yb'], sp['taps'], sp['vec'], sp['vec'], sp['w'], sp['b'], sp['w'], sp['b'], sp['col']],
        out_specs=[sp['col'], sp['col'], sp['taps'], sp['vec'], sp['vec'], sp['w'], sp['b'], sp['w'], sp['b']],
        out_shape=[jax.ShapeDtypeStruct((s, nb * bw), BF), jax.ShapeDtypeStruct((s, nb * bw), BF),
                   jax.ShapeDtypeStruct((CONV_WIDTH, nb * bw), F32), jax.ShapeDtypeStruct((1, nb * bw), F32),
                   jax.ShapeDtypeStruct((1, nb * bw), F32), jax.ShapeDtypeStruct((nb, bw, bw), F32),
                   jax.ShapeDtypeStruct((nb, 1, bw), F32), jax.ShapeDtypeStruct((nb, bw, bw), F32),
                   jax.ShapeDtypeStruct((nb, 1, bw), F32)],
        compiler_params=_cparams(),
    )(proj, proj, conv_w, conv_b, lam, wr, br, wi, bi, dy)
    return (jnp.concatenate([outs[0], outs[1]], axis=1),) + tuple(outs[2:])


ROW_STEP_BYTES = 12 * 1024 * 1024


def _row_tile(rows, cols, bytes_per_row_set):
    t = 8
    while t * 2 <= rows and rows % (t * 2) == 0 and (t * 2) * cols * bytes_per_row_set <= ROW_STEP_BYTES:
        t *= 2
    return t


def _sum_slabs(a, out_dtype, name):
    n, rows, cols = a.shape
    tr = _row_tile(rows, cols, 4 * (n + 1))

    def body(*refs):
        acc = refs[0][...].astype(F32)
        for r in refs[1:n]:
            acc = acc + r[...].astype(F32)
        refs[n][...] = acc.astype(out_dtype)

    specs = [pl.BlockSpec((None, tr, cols), functools.partial(lambda i, k: (k, i, 0), k=k)) for k in range(n)]
    return pl.pallas_call(
        body, name=name, grid=(rows // tr,), in_specs=specs, out_specs=pl.BlockSpec((tr, cols), lambda i: (i, 0)),
        out_shape=jax.ShapeDtypeStruct((rows, cols), out_dtype), compiler_params=_cparams(),
    )(*([a] * n))


def _sum_keep_and_landed(g, landed, core, name):
    four, hr, cols = landed.shape
    tr = _row_tile(hr, cols, 2 * 3)
    tph = hr // tr

    def body(core_ref, g_ref, l_ref, o_ref):
        o_ref[...] = (g_ref[...].astype(F32) + l_ref[...].astype(F32)).astype(o_ref.dtype)

    grid_spec = pltpu.PrefetchScalarGridSpec(
        num_scalar_prefetch=1, grid=(four, tph),
        in_specs=[pl.BlockSpec((None, tr, cols), lambda j, r, core_ref: (j, core_ref[0] * tph + r, 0)),
                  pl.BlockSpec((None, tr, cols), lambda j, r, core_ref: (j, r, 0))],
        out_specs=pl.BlockSpec((None, tr, cols), lambda j, r, core_ref: (j, r, 0)))
    return pl.pallas_call(body, name=name, grid_spec=grid_spec, out_shape=jax.ShapeDtypeStruct(landed.shape, g.dtype),
                          compiler_params=_cparams())(core, g, landed)


def _sum_chip_parts(parts, landed, chip, name):
    _, rows, cols = parts.shape
    tr = _row_tile(rows, cols, 2 * 4 + 4)

    def body(chip_ref, p_ref, l0_ref, l1_ref, l2_ref, o_ref):
        acc = l0_ref[...].astype(F32) + l1_ref[...].astype(F32)
        o_ref[...] = (acc + l2_ref[...].astype(F32)) + p_ref[...].astype(F32)

    slab = lambda k: pl.BlockSpec((None, tr, cols), lambda i, chip_ref: (k, i, 0))
    grid_spec = pltpu.PrefetchScalarGridSpec(
        num_scalar_prefetch=1, grid=(rows // tr,),
        in_specs=[pl.BlockSpec((None, tr, cols), lambda i, chip_ref: (chip_ref[0], i, 0)), slab(0), slab(1), slab(2)],
        out_specs=pl.BlockSpec((tr, cols), lambda i, chip_ref: (i, 0)))
    return pl.pallas_call(body, name=name, grid_spec=grid_spec, out_shape=jax.ShapeDtypeStruct((rows, cols), F32),
                          compiler_params=_cparams())(chip, parts, landed, landed, landed)


_ADAM_C1 = 1.0 / (1.0 - ADAM_B1 ** ADAM_STEP)
_ADAM_C2 = 1.0 / (1.0 - ADAM_B2 ** ADAM_STEP)


def _adamw_math(w, g, m, v):
    nm = ADAM_B1 * m + (1.0 - ADAM_B1) * g
    nv = ADAM_B2 * v + (1.0 - ADAM_B2) * (g * g)
    den = jnp.sqrt(nv * _ADAM_C2) + ADAM_EPS
    inv = pl.reciprocal(den, approx=True)
    inv = inv * (2.0 - den * inv)
    delta = -ADAM_LR * ((nm * _ADAM_C1) * inv + ADAM_WD * w)
    return delta, nm, nv


def _adamw(w, g, m, v, name):
    rows, cols = w.shape
    tr = _row_tile(rows, cols, 4 * 7)

    def body(w_ref, g_ref, m_ref, v_ref, d_ref, nm_ref, nv_ref):
        d_ref[...], nm_ref[...], nv_ref[...] = _adamw_math(w_ref[...], g_ref[...], m_ref[...], v_ref[...])

    spec = pl.BlockSpec((tr, cols), lambda i: (i, 0))
    shape = jax.ShapeDtypeStruct((rows, cols), F32)
    return pl.pallas_call(
        body, name=name, grid=(rows // tr,), in_specs=[spec] * 4, out_specs=[spec] * 3, out_shape=[shape] * 3,
        compiler_params=_cparams(),
    )(w, g, m, v)


def _adamw_halves(w, m, v, layer, mine, theirs, core, name, carried=None):
    n_layers, rows, cols = w.shape
    hr = mine.shape[0]
    tr = _row_tile(hr, cols, 4 * 9)
    tph = hr // tr

    def body(core_ref, w_ref, q_ref, t_ref, m_ref, v_ref, *rest):
        g_ref, d_ref, nm_ref, nv_ref = rest[-4:]
        is_mine = (pl.program_id(0) // tph) == core_ref[0]
        g = jnp.where(is_mine, q_ref[...], t_ref[...])
        g_ref[...] = g
        d_ref[...], nm_ref[...], nv_ref[...] = _adamw_math(w_ref[...], g, m_ref[...], v_ref[...])

    slab = pl.BlockSpec((None, tr, cols), lambda i, core_ref: (layer, i, 0))

    def mine_index(i, core_ref):
        return jnp.where(i // tph == core_ref[0], i % tph, jnp.where(core_ref[0] == 0, tph - 1, 0)), 0

    def theirs_index(i, core_ref):
        return jnp.where(i // tph == core_ref[0], jnp.where(core_ref[0] == 0, 0, tph - 1), i % tph), 0

    in_specs = [slab, pl.BlockSpec((tr, cols), mine_index), pl.BlockSpec((tr, cols), theirs_index), slab, slab]
    operands = [core, w, mine, theirs, m, v]
    aliases = {}
    if carried is not None:
        in_specs += [_ANY] * 4
        aliases = {len(operands) + k: k for k in range(4)}
        operands += list(carried)
    grid_spec = pltpu.PrefetchScalarGridSpec(num_scalar_prefetch=1, grid=(rows // tr,), in_specs=in_specs,
                                             out_specs=[slab] * 4)
    return pl.pallas_call(
        body, name=name, grid_spec=grid_spec, out_shape=[jax.ShapeDtypeStruct(w.shape, F32)] * 4,
        input_output_aliases=aliases, compiler_params=_cparams(),
    )(*operands)


_ANY = pl.BlockSpec(memory_space=pl.ANY)


def _mesh_pos():
    return lax.axis_index("x"), lax.axis_index("y"), lax.axis_index("c")


def _remote(src, dst, send_sems, recv_sems, k, dev):
    return pltpu.make_async_remote_copy(src_ref=src, dst_ref=dst, send_sem=send_sems.at[k], recv_sem=recv_sems.at[k],
                                        device_id=dev, device_id_type=MESH)


STREAM_CHUNK_BYTES = 2 * 1024 * 1024
STREAM_SLOTS = 4


def _chunk_rows(rows, cols, itemsize):
    assert rows % 16 == 0, rows
    t = 16
    while t * 2 <= rows and rows % (t * 2) == 0 and (t * 2) * cols * itemsize <= STREAM_CHUNK_BYTES:
        t *= 2
    return t


def _stream_chunks(n_chunks, src_at, dst_at, buf, load_sems, send_sems, recv_sem, sibling):
    def load(k, slot):
        return pltpu.make_async_copy(src_at(k), buf.at[slot], load_sems.at[slot])

    def send(k, slot):
        return pltpu.make_async_remote_copy(src_ref=buf.at[slot], dst_ref=dst_at(k), send_sem=send_sems.at[slot],
                                            recv_sem=recv_sem, device_id=sibling, device_id_type=MESH)

    def step(k, carry):
        slot = k % STREAM_SLOTS

        @pl.when(k >= STREAM_SLOTS)
        def _():
            send(k - STREAM_SLOTS, slot).wait_send()

        load(k, slot).start()

        @pl.when(k >= 1)
        def _():
            prev = (k - 1) % STREAM_SLOTS
            load(k - 1, prev).wait()
            send(k - 1, prev).start()

        return carry

    lax.fori_loop(0, n_chunks, step, 0)
    last = (n_chunks - 1) % STREAM_SLOTS
    load(n_chunks - 1, last).wait()
    send(n_chunks - 1, last).start()
    for k in range(max(0, n_chunks - STREAM_SLOTS), n_chunks):
        send(k, k % STREAM_SLOTS).wait_send()


_HBM = pl.BlockSpec(memory_space=pltpu.HBM)
_SEM = pl.BlockSpec(memory_space=pltpu.SEMAPHORE)
_DATAFLOW = pltpu.SideEffectType.DATAFLOW_SIDE_EFFECTING


def _chip_copies(kind, ins, lands, split, send_sems, recv_sems):
    x, y, c = _mesh_pos()
    me = 2 * x + y
    chips = [(1 - x, y), (x, 1 - y), (1 - x, 1 - y)]
    pairs = []
    for i in range(len(ins)):
        for j, chip in enumerate(chips):
            pj = 2 * chip[0] + chip[1]
            if kind == 'scatter':
                src, dst, got = ins[i].at[pj], lands[i].at[j], lands[i].at[j]
            elif split[i]:
                hr = ins[i].shape[0] // 2
                rows = pl.ds(c * hr, hr)
                src, dst, got = ins[i].at[rows], lands[i].at[me, rows], lands[i].at[pj, rows]
            else:
                src, dst, got = ins[i], lands[i].at[me], lands[i].at[pj]
            k = 3 * i + j
            pairs.append((_remote(src, dst, send_sems, recv_sems, k, (*chip, c)),
                          _remote(got, got, send_sems, recv_sems, k, (*chip, c))))
    return pairs


def _chip_exchange_start(kind, srcs, land_shapes, split, name, after):
    n = len(srcs)

    def body(*refs):
        ins, lands = refs[:n], refs[n:2 * n]
        send_sems, recv_sems = refs[2 * n + 1], refs[2 * n + 2]
        token = refs[-1]
        for send, _ in _chip_copies(kind, ins, lands, split, send_sems, recv_sems):
            send.start()
        token[...] = jnp.zeros_like(token)

    hbm = lambda t: pltpu.with_memory_space_constraint(t, pltpu.HBM)
    operands = [hbm(s) for s in srcs] + [hbm(lax.empty(shp, s.dtype)) for shp, s in zip(land_shapes, srcs)] + [after]
    out = pl.pallas_call(
        body, name=name, in_specs=[_HBM] * (2 * n) + [_ANY],
        out_specs=[_SEM, _SEM] + [_HBM] * (2 * n) + [pl.BlockSpec(memory_space=pltpu.VMEM)],
        out_shape=[pltpu.SemaphoreType.DMA((3 * n,)), pltpu.SemaphoreType.DMA((3 * n,))]
        + [pltpu.HBM(s.shape, s.dtype) for s in srcs] + [pltpu.HBM(shp, s.dtype) for shp, s in zip(land_shapes, srcs)]
        + [jax.ShapeDtypeStruct((8, 128), F32)],
        input_output_aliases={i: 2 + i for i in range(2 * n)},
        compiler_params=pltpu.CompilerParams(has_side_effects=_DATAFLOW),
    )(*operands)
    return out[0], out[1], out[2:2 + n], out[2 + n:2 + 2 * n], out[-1]


def _chip_exchange_wait(kind, started, split, name, after):
    send_sems, recv_sems, srcs, lands, _ = started
    n = len(srcs)

    def body(*refs):
        ins, land_refs = refs[:n], refs[n:2 * n]
        for send, arrived in _chip_copies(kind, ins, land_refs, split, refs[2 * n], refs[2 * n + 1]):
            send.wait_send()
            arrived.wait_recv()

    out = pl.pallas_call(
        body, name=name, in_specs=[_HBM] * (2 * n) + [_SEM, _SEM, _ANY], out_specs=[_HBM] * (2 * n),
        out_shape=[pltpu.HBM(s.shape, s.dtype) for s in srcs] + [pltpu.HBM(l.shape, l.dtype) for l in lands],
        input_output_aliases={i: i for i in range(2 * n)},
        compiler_params=pltpu.CompilerParams(has_side_effects=_DATAFLOW),
    )(*srcs, *lands, send_sems, recv_sems, after)
    return out[:n], out[n:]


def _pass_to_sibling(gathered, name):
    n = len(gathered)
    tr = [_chunk_rows(g.shape[1] // 2, g.shape[2], g.dtype.itemsize) for g in gathered]

    def body(*refs):
        outs = refs[n:2 * n]
        recv_sems, load_sems, send_sems = refs[2 * n:2 * n + 3]
        bufs = refs[2 * n + 3:]
        x, y, c = _mesh_pos()
        sibling = (x, y, 1 - c)
        chips = [(1 - x, y), (x, 1 - y), (1 - x, 1 - y)]
        for i in range(n):
            hr = outs[i].shape[1] // 2
            for j, chip in enumerate(chips):
                def rows_at(k, i=i, pj=2 * chip[0] + chip[1], hr=hr):
                    return outs[i].at[pj, pl.ds(c * hr + k * tr[i], tr[i])]

                _stream_chunks(hr // tr[i], rows_at, rows_at, bufs[i], load_sems, send_sems, recv_sems.at[3 * i + j],
                               sibling)
        for i in range(n):
            hr = outs[i].shape[1] // 2
            for j, chip in enumerate(chips):
                blk = outs[i].at[2 * chip[0] + chip[1], pl.ds((1 - c) * hr, hr)]
                pltpu.make_async_remote_copy(src_ref=blk, dst_ref=blk, send_sem=send_sems.at[0],
                                             recv_sem=recv_sems.at[3 * i + j], device_id=sibling,
                                             device_id_type=MESH).wait_recv()

    return pl.pallas_call(
        body, name=name, in_specs=[_ANY] * n, out_specs=[_ANY] * n,
        out_shape=[jax.ShapeDtypeStruct(g.shape, g.dtype) for g in gathered],
        input_output_aliases={i: i for i in range(n)},
        scratch_shapes=[pltpu.SemaphoreType.DMA((3 * n,)), pltpu.SemaphoreType.DMA((STREAM_SLOTS,)),
                        pltpu.SemaphoreType.DMA((STREAM_SLOTS,))]
        + [pltpu.VMEM((STREAM_SLOTS, tr[i], g.shape[2]), g.dtype) for i, g in enumerate(gathered)],
        compiler_params=_cparams(),
    )(*gathered)


def _stream_to_sibling(srcs, halved, name):
    n = len(srcs)
    geo = []
    for s in srcs:
        rows = s.shape[1] // 2 if halved else s.shape[1]
        geo.append((s.shape[0], rows, s.shape[2], _chunk_rows(rows, s.shape[2], s.dtype.itemsize)))

    def body(*refs):
        ins, outs = refs[:n], refs[n:2 * n]
        recv_sems, load_sems, send_sems = refs[2 * n:2 * n + 3]
        bufs = refs[2 * n + 3:]
        x, y, c = _mesh_pos()
        sibling = (x, y, 1 - c)
        for i in range(n):
            slabs, rows, _, tr = geo[i]
            per_slab = rows // tr
            off = (1 - c) * rows if halved else 0

            def src_at(k, i=i, per_slab=per_slab, tr=tr, off=off):
                return ins[i].at[k // per_slab, pl.ds(off + (k % per_slab) * tr, tr)]

            def dst_at(k, i=i, per_slab=per_slab, tr=tr):
                return outs[i].at[k // per_slab, pl.ds((k % per_slab) * tr, tr)]

            _stream_chunks(slabs * per_slab, src_at, dst_at, bufs[i], load_sems, send_sems, recv_sems.at[i], sibling)
        for i in range(n):
            pltpu.make_async_remote_copy(src_ref=outs[i], dst_ref=outs[i], send_sem=send_sems.at[0],
                                         recv_sem=recv_sems.at[i], device_id=sibling, device_id_type=MESH).wait_recv()

    return pl.pallas_call(
        body, name=name, in_specs=[_ANY] * n, out_specs=[_ANY] * n,
        out_shape=[jax.ShapeDtypeStruct((g[0], g[1], g[2]), s.dtype) for g, s in zip(geo, srcs)],
        scratch_shapes=[pltpu.SemaphoreType.DMA((n,)), pltpu.SemaphoreType.DMA((STREAM_SLOTS,)),
                        pltpu.SemaphoreType.DMA((STREAM_SLOTS,))]
        + [pltpu.VMEM((STREAM_SLOTS, g[3], g[2]), s.dtype) for g, s in zip(geo, srcs)],
        compiler_params=_cparams(),
    )(*srcs)


def _reduce_scatter_begin(grads, tag, core):
    landed = _stream_to_sibling(grads, True, f"rs_swap_{tag}")
    parts = [_sum_keep_and_landed(g, l, core, f"rs_add2_{tag}_{i}") for i, (g, l) in enumerate(zip(grads, landed))]
    return _chip_exchange_start('scatter', parts, [(3,) + p.shape[1:] for p in parts], None, f"rs_scatter_start_{tag}",
                                core)


def _reduce_scatter_finish(started, tag, chip, after):
    parts, landed = _chip_exchange_wait('scatter', started, None, f"rs_scatter_wait_{tag}", after)
    mine = [_sum_chip_parts(p, l, chip, f"rs_add4_{tag}_{i}") for i, (p, l) in enumerate(zip(parts, landed))]
    theirs = _stream_to_sibling([m[None] for m in mine], False, f"rs_join_{tag}")
    return [(m, t[0]) for m, t in zip(mine, theirs)]


def _pad_cols(a, width=128):
    return jnp.pad(a, ((0, 0), (0, width - a.shape[1])))


def _mixer_forward(kind, hn, w, tables):
    if kind == 0:
        proj = _matmul(hn, w['ret_w_in'], name="ret_proj", b_sharded=True)
        og, states = _ret_fwd(proj, w['ret_gn_gain'], tables)
        return og, (proj, states)
    if kind == 1:
        proj = _matmul(hn, w['gdn_w_main'], name="gdn_proj")
        ba = _matmul(hn, w['gdn_w_small'], name="gdn_proj_ba")
        og, states = _gdn_fwd(proj, ba, w['gdn_conv_w'], w['gdn_a_log'], w['gdn_dt_bias'], w['gdn_norm_gain'])
        return og, (proj, ba, states)
    if kind == 2:
        proj = _matmul(hn, w['gla_w_main'], name="gla_proj")
        glow = _matmul(hn, w['gla_w_small'], name="gla_proj_gate")
        og, states = _gla_fwd(proj, glow, w['gla_w_gate_up'], w['gla_gate_bias'], w['gla_norm_gain'])
        return og, (proj, glow, states)
    proj = _matmul(hn, w['lru_w_in'], name="lru_proj", b_sharded=True)
    og = _lru_fwd(proj, w['lru_conv_w'], w['lru_conv_b'], w['lru_lambda'], w['lru_w_rgate'], w['lru_b_rgate'],
                  w['lru_w_igate'], w['lru_b_igate'])
    return og, (proj,)


def _mixer_backward(kind, hn, w, tables, saved, d_og, grads):
    d = hn.shape[1]
    if kind == 0:
        proj, states = saved
        d_proj, grads['ret_gn_gain'] = _ret_bwd(proj, w['ret_gn_gain'], tables, states, d_og)
        grads['ret_w_in'] = _matmul(hn, d_proj, name="ret_dw_in", ta=True, out_dtype=BF, o_sharded=True)
        return _matmul(d_proj, w['ret_w_in'], name="ret_dhn", tb=True, b_sharded=True)
    if kind == 1:
        proj, ba, states = saved
        d_proj, d_ba, grads['gdn_conv_w'], grads['gdn_a_log'], grads['gdn_dt_bias'], grads['gdn_norm_gain'] = _gdn_bwd(
            proj, ba, w['gdn_conv_w'], w['gdn_a_log'], w['gdn_dt_bias'], w['gdn_norm_gain'], states, d_og)
        d_ba = d_ba.astype(BF)
        dw_main = _matmul(hn, d_proj, name="gdn_dw_main", ta=True, out_dtype=BF)
        dw_small = _matmul(hn, d_ba, name="gdn_dw_small", ta=True, out_dtype=BF)
        dw = jnp.concatenate([dw_main, dw_small[:, :2 * GDN_HEADS]], axis=1)
        grads['gdn_w_in'] = dw.reshape(d, N_CHIPS, dw.shape[1] // N_CHIPS).transpose(1, 0, 2)
        d_hn = _matmul(d_proj, w['gdn_w_main'], name="gdn_dhn_main", tb=True)
        return _matmul(d_ba, w['gdn_w_small'], name="gdn_dhn_small", tb=True, epilogue='add', extra=d_hn)
    if kind == 2:
        proj, glow, states = saved
        d_proj, d_glow4, d_wgu, grads['gla_gate_bias'], grads['gla_norm_gain'] = _gla_bwd(
            proj, glow, w['gla_w_gate_up'], w['gla_gate_bias'], w['gla_norm_gain'], states, d_og)
        grads['gla_w_gate_up'] = d_wgu[:GLA_RANK]
        dw_main = _matmul(hn, d_proj, name="gla_dw_main", ta=True, out_dtype=BF)
        dw_small4 = _matmul(hn, d_glow4, name="gla_dw_small", ta=True, out_dtype=F32)
        dw_small = dw_small4.reshape(d, GLA_HEADS, 128)[:, :, :GLA_RANK].sum(axis=1).astype(BF)
        dw = jnp.concatenate([dw_main, dw_small], axis=1)
        grads['gla_w_in'] = dw.reshape(d, N_CHIPS, dw.shape[1] // N_CHIPS).transpose(1, 0, 2)
        d_hn = _matmul(d_proj, w['gla_w_main'], name="gla_dhn_main", tb=True)
        w_small4 = jnp.tile(w['gla_w_small'], (1, GLA_HEADS))
        return _matmul(d_glow4, w_small4, name="gla_dhn_small", tb=True, epilogue='add', extra=d_hn)
    (proj,) = saved
    (d_proj, grads['lru_conv_w'], grads['lru_conv_b'], grads['lru_lambda'], grads['lru_w_rgate'], grads['lru_b_rgate'],
     grads['lru_w_igate'], grads['lru_b_igate']) = _lru_bwd(
        proj, w['lru_conv_w'], w['lru_conv_b'], w['lru_lambda'], w['lru_w_rgate'], w['lru_b_rgate'], w['lru_w_igate'],
        w['lru_b_igate'], d_og)
    grads['lru_w_in'] = _matmul(hn, d_proj, name="lru_dw_in", ta=True, out_dtype=BF, o_sharded=True)
    return _matmul(d_proj, w['lru_w_in'], name="lru_dhn", tb=True, b_sharded=True)


_W_OUT = ('ret_w_out', 'gdn_w_out', 'gla_w_out', 'lru_w_out')
_W_IN = ('ret_w_in', 'gdn_w_in', 'gla_w_in', 'lru_w_in')


def _layer_forward(layer, x, w, tables, token=None):
    hn = _rmsnorm_fwd(x, w['norm1'][layer], f"norm1_fwd_{layer}", token)
    og, mixer_saved = _mixer_forward(layer, hn, w, tables)
    x1 = _matmul(og, w[_W_OUT[layer]], name=f"mixer_out_{layer}", epilogue='add', extra=x)
    hn2 = _rmsnorm_fwd(x1, w['norm2'][layer], f"norm2_fwd_{layer}")
    act = _matmul(hn2, w['mlp_w_up'][layer], name="mlp_up", b_sharded=True, epilogue='relu2', out_dtype=BF)
    x2 = _matmul(act, w['mlp_w_down'][layer], name="mlp_down", epilogue='add', extra=x1)
    return x2, (x, hn, mixer_saved, og, x1, hn2, act)


def _layer_backward(layer, dx2, w, tables, saved, token=None):
    x, hn, mixer_saved, og, x1, hn2, act = saved
    d = x.shape[1]
    grads = {}
    d_up = _matmul(dx2, w['mlp_w_down'][layer], name="mlp_d_up", tb=True, epilogue='dact', extra=act, out_dtype=BF,
                   token=token)
    dw_down = _matmul(act, dx2, name="mlp_dw_down", ta=True, out_dtype=BF)
    grads['mlp_w_down'] = dw_down.reshape(N_CHIPS, dw_down.shape[0] // N_CHIPS, d)
    grads['mlp_w_up'] = _matmul(hn2, d_up, name="mlp_dw_up", ta=True, out_dtype=BF, o_sharded=True)
    d_hn2 = _matmul(d_up, w['mlp_w_up'][layer], name="mlp_d_hn", tb=True, b_sharded=True)
    dx1, grads['norm2'] = _rmsnorm_bwd(x1, w['norm2'][layer], d_hn2, dx2, f"norm2_bwd_{layer}")
    w_out = w[_W_OUT[layer]]
    d_og = _matmul(dx1, w_out, name=f"mixer_d_og_{layer}", tb=True, out_dtype=BF)
    dw_out = _matmul(og, dx1, name=f"mixer_dw_out_{layer}", ta=True, out_dtype=BF)
    grads[_W_OUT[layer]] = dw_out.reshape(N_CHIPS, dw_out.shape[0] // N_CHIPS, d)
    d_hn = _mixer_backward(layer, hn, w, tables, mixer_saved, d_og, grads)
    dx, grads['norm1'] = _rmsnorm_bwd(x, w['norm1'][layer], d_hn, dx1, f"norm1_bwd_{layer}")
    return dx, grads


PACK_ROWS = 256


def _pack(arrays):
    flat = []
    for a in arrays:
        v = a.astype(F32).reshape(-1)
        v = jnp.pad(v, (0, (-v.shape[0]) % 128))
        flat.append(v.reshape(-1, 128))
    buf = jnp.concatenate(flat, axis=0)
    return jnp.pad(buf, ((0, (-buf.shape[0]) % PACK_ROWS), (0, 0)))


def _unpack(buf, shapes):
    lead = buf.shape[:-2]
    out, off = [], 0
    for shp in shapes:
        n = math.prod(shp)
        rows = -(-n // 128)
        piece = buf[..., off:off + rows, :].reshape(lead + (rows * 128,))[..., :n]
        out.append(piece.reshape(lead + tuple(shp)))
        off += rows
    return out


_WEIGHTS = ('norm1', 'norm2', 'final_norm', 'ret_w_in', 'ret_gn_gain', 'ret_w_out', 'gdn_w_in', 'gdn_conv_w',
            'gdn_a_log', 'gdn_dt_bias', 'gdn_norm_gain', 'gdn_w_out', 'gla_w_in', 'gla_w_gate_up', 'gla_gate_bias',
            'gla_norm_gain', 'gla_w_out', 'lru_w_in', 'lru_conv_w', 'lru_conv_b', 'lru_w_rgate', 'lru_b_rgate',
            'lru_w_igate', 'lru_b_igate', 'lru_lambda', 'lru_w_out', 'mlp_w_up', 'mlp_w_down')
_FWD_PARAMS = ('x',) + _WEIGHTS
_BIG = ('ret_w_in', 'ret_w_out', 'gdn_w_in', 'gdn_w_out', 'gla_w_in', 'gla_w_out', 'lru_w_in', 'lru_w_out',
        'mlp_w_up', 'mlp_w_down')
_SMALL = tuple(n for n in _WEIGHTS if n not in _BIG)
_SMALL_SHARDED = ('ret_gn_gain', 'gdn_conv_w', 'gla_w_gate_up', 'gla_gate_bias', 'gla_norm_gain', 'lru_conv_w',
                  'lru_conv_b', 'lru_lambda')


def kernel(*args):
    names = _FWD_PARAMS + ('loss_target',) + tuple('m_' + n for n in _WEIGHTS) + tuple('v_' + n for n in _WEIGHTS)
    assert len(args) == len(names)
    a = dict(zip(names, args))
    x = a['x'][0]
    target = a['loss_target'][0]
    s, d = x.shape
    chip = 2 * lax.axis_index("x") + lax.axis_index("y")

    small_local = [a[n][0] if a[n].ndim == 3 else a[n] for n in _SMALL_SHARDED]
    small_pack = _pack(small_local)
    core_arr = lax.axis_index("c").astype(jnp.int32).reshape(1)
    chip_arr = chip.astype(jnp.int32).reshape(1)

    def whole_cols(g):
        return g.transpose(1, 0, 2).reshape(g.shape[1], N_CHIPS * g.shape[2])

    def whole_rows(g):
        return g.reshape(N_CHIPS * g.shape[1], g.shape[2])

    w = {'mlp_w_up': [None] * 4, 'mlp_w_down': [None] * 4}
    for n in _SMALL:
        if n not in _SMALL_SHARDED:
            w[n] = a[n][0] if n.startswith('lru_') else a[n]
    w['gdn_a_log'], w['gdn_dt_bias'] = _pad_cols(w['gdn_a_log']), _pad_cols(w['gdn_dt_bias'])
    w['lru_b_rgate'] = w['lru_b_rgate'].reshape(LRU_BLOCKS, 1, -1)
    w['lru_b_igate'] = w['lru_b_igate'].reshape(LRU_BLOCKS, 1, -1)

    def gather_start(layer, after):
        ops = [a[_W_IN[layer]][0].astype(BF), a[_W_OUT[layer]][0].astype(BF), a['mlp_w_up'][layer].astype(BF),
               a['mlp_w_down'][layer].astype(BF)] + ([small_pack] if layer == 0 else [])
        split = [True] * 4 + ([False] if layer == 0 else [])
        return _chip_exchange_start('gather', ops, [(N_CHIPS,) + o.shape for o in ops], split,
                                    f"gather_start_{layer}", after), split

    def gather_finish(layer, started, after):
        started, split = started
        own, lands = _chip_exchange_wait('gather', started, split, f"gather_wait_{layer}", after)
        lands = list(_pass_to_sibling(lands[:4], f"gather_pass_{layer}")) + list(lands[4:])
        g_in, g_out, g_up, g_down = [lax.dynamic_update_slice_in_dim(l, o[None], chip, axis=0)
                                     for l, o in zip(lands[:4], own[:4])]
        w['mlp_w_up'][layer], w['mlp_w_down'][layer], w[_W_OUT[layer]] = g_up, whole_rows(g_down), whole_rows(g_out)
        if layer == 0:
            small_all = lax.dynamic_update_slice_in_dim(lands[4], own[4][None], chip, axis=0)
            for n, piece in zip(_SMALL_SHARDED, _unpack(small_all, [p.shape for p in small_local])):
                w[n] = whole_cols(piece)
            w['gla_w_gate_up'] = jnp.pad(w['gla_w_gate_up'], ((0, 128 - GLA_RANK), (0, 0)))
        if layer in (0, 3):
            w[_W_IN[layer]] = g_in
        else:
            name, tail = ('gdn', 2 * GDN_HEADS) if layer == 1 else ('gla', GLA_RANK)
            full = whole_cols(g_in)
            w[name + '_w_main'] = full[:, :full.shape[1] - tail]
            w[name + '_w_small'] = _pad_cols(full[:, full.shape[1] - tail:])

    tables = _ret_tables(s, d // RET_HEADS)
    saved = []
    h = x
    gather_finish(0, gather_start(0, x), x)
    for layer in range(4):
        nxt = gather_start(layer + 1, w[_W_OUT[layer]]) if layer < 3 else None
        h, sv = _layer_forward(layer, h, w, tables, None if nxt is None else nxt[0][4])
        saved.append(sv)
        if nxt is not None:
            gather_finish(layer + 1, nxt, h)
    loss_part, dh, g_final = _loss_head(h, w['final_norm'], target)
    loss = lax.psum(loss_part, ("x", "y", "c"))

    grad, delta, new_m, new_v = {}, {}, {}, {}
    small_grads = {'final_norm': g_final}
    norm_grads = {'norm1': [None] * 4, 'norm2': [None] * 4}
    mlp_upd = {'mlp_w_up': None, 'mlp_w_down': None}

    def update_layer(layer, started, after):
        red = _reduce_scatter_finish(started, str(layer), chip_arr, after)
        for n, (mine, theirs) in ((_W_IN[layer], red[0]), (_W_OUT[layer], red[1])):
            grad[n], delta[n], new_m[n], new_v[n] = _adamw_halves(a[n], a['m_' + n], a['v_' + n], 0, mine, theirs,
                                                                  core_arr, f"adamw_{n}")
        for n, (mine, theirs) in (('mlp_w_up', red[2]), ('mlp_w_down', red[3])):
            mlp_upd[n] = _adamw_halves(a[n], a['m_' + n], a['v_' + n], layer, mine, theirs, core_arr,
                                       f"adamw_{n}_{layer}", carried=mlp_upd[n])

    scatters = {}
    token = None
    for layer in reversed(range(4)):
        dh, g = _layer_backward(layer, dh, w, tables, saved[layer], token)
        saved[layer] = None
        scatters[layer] = _reduce_scatter_begin([g[_W_IN[layer]], g[_W_OUT[layer]], g['mlp_w_up'], g['mlp_w_down']],
                                                str(layer), core_arr)
        token = scatters[layer][4]
        norm_grads['norm1'][layer], norm_grads['norm2'][layer] = g['norm1'], g['norm2']
        for n in _SMALL:
            if n in g:
                small_grads[n] = g[n]
    small_grads['norm1'] = jnp.stack(norm_grads['norm1'])
    small_grads['norm2'] = jnp.stack(norm_grads['norm2'])

    full_shapes = [small_grads[n].shape for n in _SMALL]
    small_part = _pack([small_grads[n] for n in _SMALL])
    sibling_part = _stream_to_sibling([small_part[None]], False, "small_swap")[0][0]
    chip_part = _sum_slabs(jnp.stack([small_part, sibling_part]), F32, "sum_small_cores")
    small_started = _chip_exchange_start('gather', [chip_part], [(N_CHIPS,) + chip_part.shape], [False],
                                         "small_gather_start", core_arr)
    after = token
    for layer in (3, 2, 1):
        update_layer(layer, scatters[layer], after)
        after = new_v[_W_OUT[layer]]
    own, lands = _chip_exchange_wait('gather', small_started, [False], "small_gather_wait", after)
    total = _sum_slabs(lax.dynamic_update_slice_in_dim(lands[0], own[0][None], chip, axis=0), F32, "sum_small_chips")
    local_g = {}
    for n, full in zip(_SMALL, _unpack(total, full_shapes)):
        shp = a[n].shape
        if n in _SMALL_SHARDED:
            full = full.reshape(full.shape[0], -1)
            cq = shp[-1]
            full = lax.dynamic_slice_in_dim(full, chip * cq, cq, axis=1)
        elif n in ('gdn_a_log', 'gdn_dt_bias'):
            full = full[:, :shp[-1]]
        local_g[n] = full.reshape(shp)
    shapes = [a[n].shape for n in _SMALL]
    packed = [_pack([src[n] for n in _SMALL]) for src in
              (a, local_g, {n: a['m_' + n] for n in _SMALL}, {n: a['v_' + n] for n in _SMALL})]
    upd = _adamw(*packed, "adamw_small")
    for n, gr, dl, nm, nv in zip(_SMALL, _unpack(packed[1], shapes), *[_unpack(u, shapes) for u in upd]):
        grad[n], delta[n], new_m[n], new_v[n] = gr, dl, nm, nv
    update_layer(0, scatters[0], upd[0])
    for n in ('mlp_w_up', 'mlp_w_down'):
        grad[n], delta[n], new_m[n], new_v[n] = mlp_upd[n]

    out = [loss, dh.reshape(a['x'].shape)]
    for group in (grad, delta, new_m, new_v):
        out += [group[n].reshape(a[n].shape) for n in _WEIGHTS]
    return tuple(out)
```

```python
import functools
import math

import jax
import jax.numpy as jnp
from jax import lax
from jax.experimental import pallas as pl
from jax.experimental.pallas import tpu as pltpu

F32 = jnp.float32
BF = jnp.bfloat16
MESH = pl.DeviceIdType.MESH

NORM_EPS = 1e-6
CHUNK = 64
RET_HEADS = 8
GDN_HEADS = 16
GLA_HEADS = 4
GLA_RANK = 16
GLA_TAU = 16.0
LRU_BLOCKS = 16
LRU_C = 8.0
CONV_WIDTH = 4
ROPE_BASE = 10000.0
N_CHIPS = 4
N_DEV = 8

ADAM_LR = 0.001
ADAM_B1 = 0.9
ADAM_B2 = 0.999
ADAM_EPS = 1e-08
ADAM_WD = 0.01
ADAM_STEP = 10

VMEM_LIMIT_BYTES = 56 * 1024 * 1024
TOKEN_BLOCK = 256
ROW_BLOCK = 256
GDN_GROUP = 16


def _cparams(**kw):
    return pltpu.CompilerParams(vmem_limit_bytes=VMEM_LIMIT_BYTES, **kw)


def _bf16_parts(x, n):
    parts = []
    for _ in range(n - 1):
        p = x.astype(BF)
        parts.append(p)
        x = x - p.astype(F32)
    return parts + [x.astype(BF)]


def _raw_mm(a, b, ta, tb, hi):
    nb = a.ndim - 2
    batch = tuple(range(nb))
    dims = (((nb + (0 if ta else 1),), (nb + (1 if tb else 0),)), (batch, batch))

    def dot(p, q):
        return lax.dot_general(p, q, dims, preferred_element_type=F32)

    if not hi:
        return dot(a.astype(BF), b.astype(BF))
    if hi == 'l':
        ae = a.astype(BF)
        b0, b1, b2 = _bf16_parts(b.astype(F32), 3)
        return dot(ae, b0) + (dot(ae, b1) + dot(ae, b2))
    if hi == 'r':
        be = b.astype(BF)
        a0, a1, a2 = _bf16_parts(a.astype(F32), 3)
        return dot(a0, be) + (dot(a1, be) + dot(a2, be))
    a0, a1 = _bf16_parts(a.astype(F32), 2)
    b0, b1 = _bf16_parts(b.astype(F32), 2)
    return dot(a0, b0) + (dot(a0, b1) + dot(a1, b0))


@functools.partial(jax.custom_vjp, nondiff_argnums=(2, 3, 4))
def _mm_vjp(a, b, ta, tb, hi):
    return _raw_mm(a, b, ta, tb, hi)


def _mm_vjp_fwd(a, b, ta, tb, hi):
    return _raw_mm(a, b, ta, tb, hi), (a, b)


def _mm_vjp_bwd(ta, tb, hi, res, g):
    a, b = res
    if ta:
        da = _raw_mm(b, g, tb, True, 'l' if hi == 'r' else bool(hi))
    else:
        da = _raw_mm(g, b, False, not tb, 'r' if hi == 'r' else bool(hi))
    if tb:
        db = _raw_mm(g, a, True, ta, 'r' if hi == 'l' else bool(hi))
    else:
        db = _raw_mm(a, g, not ta, False, 'l' if hi == 'l' else bool(hi))
    return da, db


_mm_vjp.defvjp(_mm_vjp_fwd, _mm_vjp_bwd)


def _mm_diff(a, b, ta=False, tb=False, hi=False):
    return _mm_vjp(a, b, ta, tb, hi)


def _mm_plain(a, b, ta=False, tb=False, hi=False):
    return _raw_mm(a, b, ta, tb, hi)


def _shift_rows(x, k, up):
    if k == 0:
        return x
    n = x.shape[0]
    rows = lax.broadcasted_iota(jnp.int32, x.shape, 0)
    if up:
        return jnp.where(rows < n - k, pltpu.roll(x, n - k, 0), 0.0)
    return jnp.where(rows >= k, pltpu.roll(x, k, 0), 0.0)


@functools.partial(jax.custom_vjp, nondiff_argnums=(1, 2))
def _shift_vjp(x, k, up):
    return _shift_rows(x, k, up)


def _shift_vjp_fwd(x, k, up):
    return _shift_rows(x, k, up), None


def _shift_vjp_bwd(k, up, _, g):
    return (_shift_rows(g, k, not up),)


_shift_vjp.defvjp(_shift_vjp_fwd, _shift_vjp_bwd)


def _sigmoid(x):
    return 1.0 / (1.0 + jnp.exp(-x))


def _silu(x):
    return x * _sigmoid(x)


def _softplus(x):
    return jnp.maximum(x, 0.0) + jnp.log(1.0 + jnp.exp(-jnp.abs(x)))


def _gelu_tanh(x):
    return 0.5 * x * (1.0 + jnp.tanh(math.sqrt(2.0 / math.pi) * (x + 0.044715 * (x * x * x))))


def _expm1(x):
    series = x * (1.0 + x * (0.5 + x * (1.0 / 6.0 + x * (1.0 / 24.0))))
    return jnp.where(jnp.abs(x) < 0.03, series, jnp.exp(x) - 1.0)


def _rmsnorm(x, g):
    return x * lax.rsqrt(jnp.mean(x * x, axis=-1, keepdims=True) + NORM_EPS) * g


def _head_norm(o, gain, center):
    if center:
        o = o - jnp.mean(o, axis=-1, keepdims=True)
    return o * lax.rsqrt(jnp.mean(o * o, axis=-1, keepdims=True) + NORM_EPS) * gain


def _l2norm(x):
    return x * lax.rsqrt(jnp.sum(x * x, axis=-1, keepdims=True) + NORM_EPS)


def _iota2(shape, dim):
    return lax.broadcasted_iota(jnp.int32, shape, dim)


def _tri_ones(n, upper=False):
    i, j = _iota2((n, n), 0), _iota2((n, n), 1)
    return jnp.where((i <= j) if upper else (j <= i), 1.0, 0.0).astype(F32)


def _linscan(a, u, rev):
    n = a.shape[0]
    rows = _iota2(a.shape, 0)
    d = 1
    while d < n:
        if rev:
            valid = rows < n - d
            a_s, u_s = pltpu.roll(a, n - d, 0), pltpu.roll(u, n - d, 0)
        else:
            valid = rows >= d
            a_s, u_s = pltpu.roll(a, d, 0), pltpu.roll(u, d, 0)
        u = a * jnp.where(valid, u_s, 0.0) + u
        a = a * jnp.where(valid, a_s, 1.0)
        d *= 2
    return u


def _ret_group(mm, q, k, v, g, gain, state, cos, sin, dintra, qdec, kdec, cdec):
    n, _, dk = q.shape
    half = dk // 2

    def rot(t):
        t1, t2 = t[..., :half], t[..., half:]
        return jnp.concatenate([t1 * cos - t2 * sin, t1 * sin + t2 * cos], axis=-1)

    qr = rot(q)
    kr = rot(k) * (dk ** -0.5)
    o_intra = mm(mm(qr, kr, tb=True) * dintra, v)
    kv = mm(kr * kdec, v, ta=True)
    q_in = qr * qdec
    ys = []
    for i in range(n):
        o = o_intra[i] + mm(q_in[i], state)
        state = state * cdec + kv[i]
        ys.append(_head_norm(o, gain, True) * _silu(g[i]))
    return ys, state


def _gla_group(mm, q, k, v, r, glow, wgu, bias, gain, state_t):
    n, c, dk = q.shape
    logit = mm(glow.reshape(n * c, glow.shape[-1]), wgu).reshape(n, c, dk) + bias
    la = -_softplus(-logit) * (1.0 / GLA_TAU)
    cum = mm(jnp.broadcast_to(_tri_ones(c), (n, c, c)), la, hi='l')
    rows = lax.broadcasted_iota(jnp.int32, la.shape, 1)
    ref = jnp.sum(jnp.where(rows < c // 2, la, 0.0), axis=1, keepdims=True)
    tot = jnp.sum(la, axis=1, keepdims=True)
    fwd, bwd = jnp.exp(cum - ref), jnp.exp(ref - cum)
    qs = q * (dk ** -0.5)
    s_lo = mm(qs * fwd, k * bwd, tb=True)
    s_up = mm(qs * bwd, k * fwd, tb=True)
    i, j = _iota2((c, c), 0), _iota2((c, c), 1)
    o_intra = mm(jnp.where(i >= j, s_lo, s_up), v)
    q_in = qs * jnp.exp(cum)
    kv_t = mm(v, k * jnp.exp(tot - cum), ta=True)
    dec = jnp.exp(tot)
    ys = []
    for m in range(n):
        o = o_intra[m] + mm(q_in[m], state_t, tb=True)
        state_t = state_t * dec[m] + kv_t[m]
        ys.append(_head_norm(o, gain, False) * _silu(r[m]))
    return ys, state_t


def _gdn_group(mm, q, k, v, z, ba, a_log, dt_bias, gain, state, head):
    g, c, dk = q.shape
    lanes = lax.broadcasted_iota(jnp.int32, (1, 1, ba.shape[-1]), 2)
    oh_b = jnp.where(lanes == head, 1.0, 0.0).astype(F32)
    oh_a = jnp.where(lanes == head + GDN_HEADS, 1.0, 0.0).astype(F32)
    beta = _sigmoid(jnp.sum(ba * oh_b, axis=-1, keepdims=True))
    a_logit = jnp.sum(ba * oh_a, axis=-1, keepdims=True)
    a_h = jnp.sum(a_log * oh_b[0], axis=-1, keepdims=True)
    dt_h = jnp.sum(dt_bias * oh_b[0], axis=-1, keepdims=True)
    la = -jnp.exp(a_h) * _softplus(a_logit + dt_h)
    q = _l2norm(q) * (dk ** -0.5)
    k = _l2norm(k)
    tri = jnp.broadcast_to(_tri_ones(c), (g, c, c))
    tri_up = jnp.broadcast_to(_tri_ones(c, upper=True), (g, c, c))
    cum_k = mm(tri, la * jnp.ones((g, c, dk), F32), hi='l')
    la_sq = la * jnp.ones((g, c, c), F32)
    cum_i = mm(tri, la_sq, hi='l')
    cum_j = mm(la_sq, tri_up, ta=True, hi='r')
    i, j = _iota2((c, c), 0), _iota2((c, c), 1)
    strict = i > j
    rel = jnp.where(strict, jnp.exp(jnp.where(strict, cum_i - cum_j, 0.0)), 0.0)
    a_mat = beta * rel * mm(k, k, tb=True)
    inv = jnp.where(i == j, 1.0, 0.0).astype(F32) - a_mat
    power = a_mat
    for _ in range(int(math.log2(c)) - 1):
        power = mm(power, power, hi=True)
        inv = inv + mm(inv, power, hi=True)
    tot = jnp.sum(la, axis=1, keepdims=True)
    u = mm(inv, beta * v, hi=True)
    w = mm(inv, (beta * jnp.exp(cum_k)) * k, hi=True)
    k_end = k * jnp.exp(tot - cum_k)
    di, dj = _iota2((dk, dk), 0), _iota2((dk, dk), 1)
    trans = jnp.exp(tot) * jnp.where(di == dj, 1.0, 0.0).astype(F32) - mm(k_end, w, ta=True)
    inject = mm(k_end, u, ta=True)
    ys = []
    for n in range(g):
        state = mm(trans[n], state) + inject[n]
        ys.append(_head_norm(mm(q[n], state), gain, False) * _silu(z[n]))
    return ys, state


def _conv_taps(shift, x, taps):
    out = None
    for tap, w in enumerate(taps):
        term = shift(x, CONV_WIDTH - 1 - tap, False) * w
        out = term if out is None else out + term
    return out


def _lru_pre(mm, shift, xb, yb, taps, cb, lam, wr, br, wi, bi):
    xb = _conv_taps(shift, xb, taps) + cb
    r = _sigmoid(mm(xb, wr) + br)
    i = _sigmoid(mm(xb, wi) + bi)
    log_a = (-LRU_C) * _softplus(-lam) * r
    a = jnp.exp(log_a)
    u = jnp.sqrt(-_expm1(2.0 * log_a)) * (i * xb)
    return a, u, _gelu_tanh(yb)


_TOKEN_SPEC = pl.BlockSpec((8, 128), lambda *_: (0, 0))


MATMUL_VMEM_BUDGET = 40 * 1024 * 1024
_TILE_SIZES = (3072, 2048, 1536, 1024, 768, 512, 384, 256, 128)


def _tile_choices(extent):
    return [t for t in _TILE_SIZES if t <= extent and extent % t == 0] or [extent]


def _matmul_tiles(m, n, kdim, n_unit, k_unit, a_item, b_item, o_item, e_item):
    best = None
    for bm in _tile_choices(m):
        for bn in _tile_choices(n_unit):
            for bk in _tile_choices(k_unit):
                nk = kdim // bk
                vmem = 2 * (bm * bk * a_item + bk * bn * b_item + bm * bn * (o_item + e_item)) + bm * bn * 4 * (2 if nk > 1 else 1)
                if vmem > MATMUL_VMEM_BUDGET:
                    continue
                steps = (m // bm) * (n // bn) * nk
                traffic = m * kdim * a_item * (1 if nk == 1 else n // bn) + kdim * n * b_item * (m // bm)
                if best is None or (steps, traffic) < best[0]:
                    best = ((steps, traffic), (bm, bn, bk))
    return best[1]


def _matmul(a, b, *, name, ta=False, tb=False, out_dtype=F32, b_sharded=False, o_sharded=False,
            epilogue=None, extra=None, token=None):
    m, kdim = (a.shape[1], a.shape[0]) if ta else a.shape
    if b_sharded:
        nq = b.shape[2]
        n = b.shape[1] if tb else N_CHIPS * nq
        assert kdim == (N_CHIPS * nq if tb else b.shape[1])
    else:
        n = b.shape[0] if tb else b.shape[1]
        assert kdim == (b.shape[1] if tb else b.shape[0])
    n_unit = b.shape[2] if (b_sharded and not tb) else (n // N_CHIPS if o_sharded else n)
    k_unit = b.shape[2] if (b_sharded and tb) else kdim
    bm, bn, bk = _matmul_tiles(m, n, kdim, n_unit, k_unit, a.dtype.itemsize, b.dtype.itemsize,
                               jnp.dtype(out_dtype).itemsize, 0 if extra is None else extra.dtype.itemsize)
    nk = kdim // bk
    grid = (m // bm, n // bn, nk)

    a_spec = pl.BlockSpec((bk, bm), lambda i, j, k: (k, i)) if ta else pl.BlockSpec((bm, bk), lambda i, j, k: (i, k))
    if b_sharded and not tb:
        per = b.shape[2] // bn
        b_spec = pl.BlockSpec((None, bk, bn), lambda i, j, k: (j // per, k, j % per))
    elif b_sharded:
        per = b.shape[2] // bk
        b_spec = pl.BlockSpec((None, bn, bk), lambda i, j, k: (k // per, j, k % per))
    elif tb:
        b_spec = pl.BlockSpec((bn, bk), lambda i, j, k: (j, k))
    else:
        b_spec = pl.BlockSpec((bk, bn), lambda i, j, k: (k, j))
    if o_sharded:
        per_o = (n // N_CHIPS) // bn
        o_spec = pl.BlockSpec((None, bm, bn), lambda i, j, k: (j // per_o, i, j % per_o))
        out_shape = jax.ShapeDtypeStruct((N_CHIPS, m, n // N_CHIPS), out_dtype)
    else:
        o_spec = pl.BlockSpec((bm, bn), lambda i, j, k: (i, j))
        out_shape = jax.ShapeDtypeStruct((m, n), out_dtype)
    in_specs = [a_spec, b_spec]
    operands = [a, b]
    if extra is not None:
        in_specs.append(pl.BlockSpec((bm, bn), lambda i, j, k: (i, j)))
        operands.append(extra)
    if token is not None:
        in_specs.append(_TOKEN_SPEC)
        operands.append(token)

    def finish(acc, e_ref, o_ref):
        if epilogue == 'add':
            acc = acc + e_ref[...].astype(F32)
        elif epilogue == 'relu2':
            acc = jnp.square(jnp.maximum(acc, 0.0))
        elif epilogue == 'dact':
            acc = acc * (2.0 * jnp.sqrt(e_ref[...].astype(F32)))
        o_ref[...] = acc.astype(o_ref.dtype)

    def body_one_step(*refs):
        finish(_raw_mm(refs[0][...], refs[1][...], ta, tb, False), refs[2] if extra is not None else None, refs[-1])

    def body(*refs):
        a_ref, b_ref = refs[0], refs[1]
        e_ref = refs[2] if extra is not None else None
        o_ref, acc_ref = refs[-2], refs[-1]
        k = pl.program_id(2)

        @pl.when(k == 0)
        def _():
            acc_ref[...] = jnp.zeros_like(acc_ref)

        acc_ref[...] += _raw_mm(a_ref[...], b_ref[...], ta, tb, False)

        @pl.when(k == nk - 1)
        def _():
            finish(acc_ref[...], e_ref, o_ref)

    return pl.pallas_call(
        body_one_step if nk == 1 else body, name=name, grid=grid, in_specs=in_specs, out_specs=o_spec,
        out_shape=out_shape, scratch_shapes=[] if nk == 1 else [pltpu.VMEM((bm, bn), F32)],
        compiler_params=_cparams(),
    )(*operands)


def _rmsnorm_fwd(x, g, name, token=None):
    s, d = x.shape

    def body(x_ref, g_ref, *rest):
        rest[-1][...] = _rmsnorm(x_ref[...], g_ref[...]).astype(BF)

    return pl.pallas_call(
        body, name=name, grid=(s // ROW_BLOCK,),
        in_specs=[pl.BlockSpec((ROW_BLOCK, d), lambda i: (i, 0)), pl.BlockSpec((1, d), lambda i: (0, 0))]
        + ([] if token is None else [_TOKEN_SPEC]),
        out_specs=pl.BlockSpec((ROW_BLOCK, d), lambda i: (i, 0)),
        out_shape=jax.ShapeDtypeStruct((s, d), BF), compiler_params=_cparams(),
    )(x, g.reshape(1, d), *([] if token is None else [token]))


def _rmsnorm_bwd(x, g, dh, dres, name):
    s, d = x.shape

    def body(x_ref, g_ref, dh_ref, dres_ref, dx_ref, dg_ref):
        _, vjp = jax.vjp(_rmsnorm, x_ref[...], g_ref[...])
        dx, dg = vjp(dh_ref[...].astype(F32))
        dx_ref[...] = dres_ref[...] + dx

        @pl.when(pl.program_id(0) == 0)
        def _():
            dg_ref[...] = jnp.zeros_like(dg_ref)

        dg_ref[...] += dg

    row = pl.BlockSpec((ROW_BLOCK, d), lambda i: (i, 0))
    vec = pl.BlockSpec((1, d), lambda i: (0, 0))
    dx, dg = pl.pallas_call(
        body, name=name, grid=(s // ROW_BLOCK,), in_specs=[row, vec, row, row], out_specs=[row, vec],
        out_shape=[jax.ShapeDtypeStruct((s, d), F32), jax.ShapeDtypeStruct((1, d), F32)],
        compiler_params=_cparams(),
    )(x, g.reshape(1, d), dh, dres)
    return dx, dg.reshape(d)


def _loss_head(x, g, target):
    s, d = x.shape

    def loss_fn(xv, gv, tv):
        err = _rmsnorm(xv, gv) - tv
        return 0.5 * jnp.sum(jnp.mean(err * err, axis=-1, keepdims=True), axis=0, keepdims=True)

    def body(x_ref, g_ref, t_ref, dx_ref, dg_ref, loss_ref):
        tv = t_ref[...]
        loss, vjp = jax.vjp(lambda xv, gv: loss_fn(xv, gv, tv), x_ref[...], g_ref[...])
        dx, dg = vjp(jnp.ones((1, 1), F32))
        dx_ref[...] = dx

        @pl.when(pl.program_id(0) == 0)
        def _():
            dg_ref[...] = jnp.zeros_like(dg_ref)
            loss_ref[...] = jnp.zeros_like(loss_ref)

        dg_ref[...] += dg
        loss_ref[...] += loss * jnp.ones_like(loss_ref)

    row = pl.BlockSpec((ROW_BLOCK, d), lambda i: (i, 0))
    vec = pl.BlockSpec((1, d), lambda i: (0, 0))
    dx, dg, loss = pl.pallas_call(
        body, name="loss_head", grid=(s // ROW_BLOCK,), in_specs=[row, vec, row],
        out_specs=[row, vec, pl.BlockSpec((1, 128), lambda i: (0, 0))],
        out_shape=[jax.ShapeDtypeStruct((s, d), F32), jax.ShapeDtypeStruct((1, d), F32),
                   jax.ShapeDtypeStruct((1, 128), F32)],
        compiler_params=_cparams(),
    )(x, g.reshape(1, d), target)
    return loss[0, 0], dx, dg.reshape(d)


def _ret_tables(s, dk):
    h = jnp.arange(RET_HEADS, dtype=F32)
    log_gamma = jnp.log1p(-jnp.exp2(-5.0 - h))
    pos = jnp.arange(CHUNK, dtype=F32)
    dist = jnp.abs(pos[:, None] - pos[None, :])
    dintra = jnp.exp(log_gamma[:, None, None] * dist)
    qdec = jnp.exp(log_gamma[:, None] * (pos + 1.0))[:, :, None]
    kdec = jnp.exp(log_gamma[:, None] * (CHUNK - 1.0 - pos))[:, :, None]
    cdec = jnp.exp(log_gamma * CHUNK)[:, None, None]
    inv = ROPE_BASE ** (-jnp.arange(0, dk, 2, dtype=F32) / dk)
    ang = jnp.arange(s, dtype=F32)[:, None] * inv[None, :]
    return jnp.cos(ang), jnp.sin(ang), dintra, qdec, kdec, cdec


def _ret_specs(s, dk, dv, tb, rev):
    nh = RET_HEADS
    nb = s // tb
    bi = (lambda b: nb - 1 - b) if rev else (lambda b: b)
    voff = 2 * nh * dk // dv
    cpb = tb // CHUNK
    return dict(
        q=pl.BlockSpec((tb, dk), lambda h, b: (bi(b), h)),
        k=pl.BlockSpec((tb, dk), lambda h, b: (bi(b), nh + h)),
        v=pl.BlockSpec((tb, dv), lambda h, b: (bi(b), voff + h)),
        g=pl.BlockSpec((tb, dv), lambda h, b: (bi(b), voff + nh + h)),
        gain=pl.BlockSpec((None, 1, dv), lambda h, b: (h, 0, 0)),
        cs=pl.BlockSpec((tb, dk // 2), lambda h, b: (bi(b), 0)),
        dintra=pl.BlockSpec((None, CHUNK, CHUNK), lambda h, b: (h, 0, 0)),
        dec=pl.BlockSpec((None, CHUNK, 1), lambda h, b: (h, 0, 0)),
        cdec=pl.BlockSpec((None, 1, 1), lambda h, b: (h, 0, 0)),
        hv=pl.BlockSpec((tb, dv), lambda h, b: (bi(b), h)),
        hk=pl.BlockSpec((tb, dk), lambda h, b: (bi(b), h)),
        st=pl.BlockSpec((None, None, dk, dv), lambda h, b: (h, bi(b), 0, 0)),
    )


def _group(t):
    return t.reshape(t.shape[0] // CHUNK, CHUNK, t.shape[-1])


def _ret_fwd(proj, gain, tables):
    s = proj.shape[0]
    d = proj.shape[1] // 6
    dk, dv = d // RET_HEADS, 2 * d // RET_HEADS
    tb = min(TOKEN_BLOCK, s)
    sp = _ret_specs(s, dk, dv, tb, False)
    cos, sin, dintra, qdec, kdec, cdec = tables

    def body(q_ref, k_ref, v_ref, g_ref, gain_ref, cos_ref, sin_ref, di_ref, qd_ref, kd_ref, cd_ref,
             y_ref, st_ref, state):
        @pl.when(pl.program_id(1) == 0)
        def _():
            state[...] = jnp.zeros_like(state)

        st_ref[...] = state[...]
        ys, new_state = _ret_group(_mm_plain, _group(q_ref[...]), _group(k_ref[...]), _group(v_ref[...]),
                                   _group(g_ref[...]), gain_ref[...], state[...], _group(cos_ref[...]),
                                   _group(sin_ref[...]), di_ref[...], qd_ref[...], kd_ref[...], cd_ref[...])
        for c, y in enumerate(ys):
            y_ref[pl.ds(c * CHUNK, CHUNK), :] = y.astype(BF)
        state[...] = new_state

    return pl.pallas_call(
        body, name="ret_fwd", grid=(RET_HEADS, s // tb),
        in_specs=[sp['q'], sp['k'], sp['v'], sp['g'], sp['gain'], sp['cs'], sp['cs'], sp['dintra'], sp['dec'],
                  sp['dec'], sp['cdec']],
        out_specs=[sp['hv'], sp['st']],
        out_shape=[jax.ShapeDtypeStruct((s, RET_HEADS * dv), BF),
                   jax.ShapeDtypeStruct((RET_HEADS, s // tb, dk, dv), F32)],
        scratch_shapes=[pltpu.VMEM((dk, dv), F32)], compiler_params=_cparams(),
    )(proj, proj, proj, proj, gain.reshape(RET_HEADS, 1, dv), cos, sin, dintra, qdec, kdec, cdec)


def _ret_bwd(proj, gain, tables, states, dy):
    s = proj.shape[0]
    d = proj.shape[1] // 6
    dk, dv = d // RET_HEADS, 2 * d // RET_HEADS
    tb = min(TOKEN_BLOCK, s)
    cpb = tb // CHUNK
    sp = _ret_specs(s, dk, dv, tb, True)
    cos, sin, dintra, qdec, kdec, cdec = tables

    def body(q_ref, k_ref, v_ref, g_ref, gain_ref, cos_ref, sin_ref, di_ref, qd_ref, kd_ref, cd_ref, st_ref,
             dy_ref, dq_ref, dk_ref, dv_ref, dg_ref, dgain_ref, dstate):
        @pl.when(pl.program_id(1) == 0)
        def _():
            dstate[...] = jnp.zeros_like(dstate)
            dgain_ref[...] = jnp.zeros_like(dgain_ref)

        cos_g, sin_g = _group(cos_ref[...]), _group(sin_ref[...])
        di, qd, kd, cd = di_ref[...], qd_ref[...], kd_ref[...], cd_ref[...]

        def fn(q, k, v, g, gn, st):
            return _ret_group(_mm_diff, q, k, v, g, gn, st, cos_g, sin_g, di, qd, kd, cd)

        _, vjp = jax.vjp(fn, _group(q_ref[...]), _group(k_ref[...]), _group(v_ref[...]), _group(g_ref[...]),
                         gain_ref[...], st_ref[...])
        dys = [dy_ref[pl.ds(c * CHUNK, CHUNK), :].astype(F32) for c in range(cpb)]
        dq, dkk, dvv, dg, dgn, dst = vjp((dys, dstate[...]))
        dq_ref[...] = dq.reshape(tb, dk).astype(BF)
        dk_ref[...] = dkk.reshape(tb, dk).astype(BF)
        dv_ref[...] = dvv.reshape(tb, dv).astype(BF)
        dg_ref[...] = dg.reshape(tb, dv).astype(BF)
        dgain_ref[...] += dgn
        dstate[...] = dst

    dq, dkk, dvv, dg, dgain = pl.pallas_call(
        body, name="ret_bwd", grid=(RET_HEADS, s // tb),
        in_specs=[sp['q'], sp['k'], sp['v'], sp['g'], sp['gain'], sp['cs'], sp['cs'], sp['dintra'], sp['dec'],
                  sp['dec'], sp['cdec'], sp['st'], sp['hv']],
        out_specs=[sp['hk'], sp['hk'], sp['hv'], sp['hv'], sp['gain']],
        out_shape=[jax.ShapeDtypeStruct((s, RET_HEADS * dk), BF), jax.ShapeDtypeStruct((s, RET_HEADS * dk), BF),
                   jax.ShapeDtypeStruct((s, RET_HEADS * dv), BF), jax.ShapeDtypeStruct((s, RET_HEADS * dv), BF),
                   jax.ShapeDtypeStruct((RET_HEADS, 1, dv), F32)],
        scratch_shapes=[pltpu.VMEM((dk, dv), F32)], compiler_params=_cparams(),
    )(proj, proj, proj, proj, gain.reshape(RET_HEADS, 1, dv), cos, sin, dintra, qdec, kdec, cdec, states, dy)
    return jnp.concatenate([dq, dkk, dvv, dg], axis=1), dgain.reshape(RET_HEADS, dv)


def _gla_specs(s, dk, dv, tb, rev):
    nh = GLA_HEADS
    nb = s // tb
    bi = (lambda b: nb - 1 - b) if rev else (lambda b: b)
    voff = 2 * nh * dk // dv
    cpb = tb // CHUNK
    return dict(
        q=pl.BlockSpec((tb, dk), lambda h, b: (bi(b), h)),
        k=pl.BlockSpec((tb, dk), lambda h, b: (bi(b), nh + h)),
        v=pl.BlockSpec((tb, dv), lambda h, b: (bi(b), voff + h)),
        r=pl.BlockSpec((tb, dv), lambda h, b: (bi(b), voff + nh + h)),
        glow=pl.BlockSpec((tb, 128), lambda h, b: (bi(b), 0)),
        wgu=pl.BlockSpec((128, dk), lambda h, b: (0, h)),
        bias=pl.BlockSpec((1, dk), lambda h, b: (0, h)),
        gain=pl.BlockSpec((None, 1, dv), lambda h, b: (h, 0, 0)),
        hv=pl.BlockSpec((tb, dv), lambda h, b: (bi(b), h)),
        hk=pl.BlockSpec((tb, dk), lambda h, b: (bi(b), h)),
        hg=pl.BlockSpec((tb, 128), lambda h, b: (bi(b), h)),
        st=pl.BlockSpec((None, None, dv, dk), lambda h, b: (h, bi(b), 0, 0)),
    )


def _gla_fwd(proj, glow, wgu, bias, gain):
    s = proj.shape[0]
    d = proj.shape[1] // 3
    dk, dv = d // 2 // GLA_HEADS, d // GLA_HEADS
    tb = min(TOKEN_BLOCK, s)
    sp = _gla_specs(s, dk, dv, tb, False)

    def body(q_ref, k_ref, v_ref, r_ref, gl_ref, wgu_ref, b_ref, gain_ref, y_ref, st_ref, state):
        @pl.when(pl.program_id(1) == 0)
        def _():
            state[...] = jnp.zeros_like(state)

        st_ref[...] = state[...]
        ys, new_state = _gla_group(_mm_plain, _group(q_ref[...]), _group(k_ref[...]), _group(v_ref[...]),
                                   _group(r_ref[...]), _group(gl_ref[...]), wgu_ref[...], b_ref[...], gain_ref[...],
                                   state[...])
        for c, y in enumerate(ys):
            y_ref[pl.ds(c * CHUNK, CHUNK), :] = y.astype(BF)
        state[...] = new_state

    return pl.pallas_call(
        body, name="gla_fwd", grid=(GLA_HEADS, s // tb),
        in_specs=[sp['q'], sp['k'], sp['v'], sp['r'], sp['glow'], sp['wgu'], sp['bias'], sp['gain']],
        out_specs=[sp['hv'], sp['st']],
        out_shape=[jax.ShapeDtypeStruct((s, GLA_HEADS * dv), BF),
                   jax.ShapeDtypeStruct((GLA_HEADS, s // tb, dv, dk), F32)],
        scratch_shapes=[pltpu.VMEM((dv, dk), F32)], compiler_params=_cparams(),
    )(proj, proj, proj, proj, glow, wgu, bias, gain.reshape(GLA_HEADS, 1, dv))


def _gla_bwd(proj, glow, wgu, bias, gain, states, dy):
    s = proj.shape[0]
    d = proj.shape[1] // 3
    dk, dv = d // 2 // GLA_HEADS, d // GLA_HEADS
    tb = min(TOKEN_BLOCK, s)
    cpb = tb // CHUNK
    sp = _gla_specs(s, dk, dv, tb, True)

    def body(q_ref, k_ref, v_ref, r_ref, gl_ref, wgu_ref, b_ref, gain_ref, st_ref, dy_ref,
             dq_ref, dk_ref, dv_ref, dr_ref, dgl_ref, dwgu_ref, db_ref, dgain_ref, dstate):
        @pl.when(pl.program_id(1) == 0)
        def _():
            dstate[...] = jnp.zeros_like(dstate)
            dwgu_ref[...] = jnp.zeros_like(dwgu_ref)
            db_ref[...] = jnp.zeros_like(db_ref)
            dgain_ref[...] = jnp.zeros_like(dgain_ref)

        def fn(q, k, v, r, gl, w, b, gn, st):
            return _gla_group(_mm_diff, q, k, v, r, gl, w, b, gn, st)

        _, vjp = jax.vjp(fn, _group(q_ref[...]), _group(k_ref[...]), _group(v_ref[...]), _group(r_ref[...]),
                         _group(gl_ref[...]), wgu_ref[...], b_ref[...], gain_ref[...], st_ref[...])
        dys = [dy_ref[pl.ds(c * CHUNK, CHUNK), :].astype(F32) for c in range(cpb)]
        dq, dkk, dvv, dr, dgl, dw, db, dgn, dst = vjp((dys, dstate[...]))
        dq_ref[...] = dq.reshape(tb, dk).astype(BF)
        dk_ref[...] = dkk.reshape(tb, dk).astype(BF)
        dv_ref[...] = dvv.reshape(tb, dv).astype(BF)
        dr_ref[...] = dr.reshape(tb, dv).astype(BF)
        dgl_ref[...] = dgl.reshape(tb, dgl.shape[-1]).astype(BF)
        dwgu_ref[...] += dw
        db_ref[...] += db
        dgain_ref[...] += dgn
        dstate[...] = dst

    nh = GLA_HEADS
    dq, dkk, dvv, dr, dgl, dwgu, db, dgain = pl.pallas_call(
        body, name="gla_bwd", grid=(nh, s // tb),
        in_specs=[sp['q'], sp['k'], sp['v'], sp['r'], sp['glow'], sp['wgu'], sp['bias'], sp['gain'], sp['st'],
                  sp['hv']],
        out_specs=[sp['hk'], sp['hk'], sp['hv'], sp['hv'], sp['hg'], sp['wgu'], sp['bias'], sp['gain']],
        out_shape=[jax.ShapeDtypeStruct((s, nh * dk), BF), jax.ShapeDtypeStruct((s, nh * dk), BF),
                   jax.ShapeDtypeStruct((s, nh * dv), BF), jax.ShapeDtypeStruct((s, nh * dv), BF),
                   jax.ShapeDtypeStruct((s, nh * 128), BF), jax.ShapeDtypeStruct((128, nh * dk), F32),
                   jax.ShapeDtypeStruct((1, nh * dk), F32), jax.ShapeDtypeStruct((nh, 1, dv), F32)],
        scratch_shapes=[pltpu.VMEM((dv, dk), F32)], compiler_params=_cparams(),
    )(proj, proj, proj, proj, glow, wgu, bias, gain.reshape(nh, 1, dv), states, dy)
    return jnp.concatenate([dq, dkk, dvv, dr], axis=1), dgl, dwgu, db, dgain.reshape(nh, dv)


def _gdn_specs(s, dk):
    nh = GDN_HEADS
    col = lambda off: pl.BlockSpec((s, dk), lambda h: (0, off + h))
    tap = lambda off: pl.BlockSpec((CONV_WIDTH, dk), lambda h: (0, off + h))
    vec = pl.BlockSpec((1, 128), lambda h: (0, 0))
    return dict(q=col(0), k=col(nh), v=col(2 * nh), z=col(3 * nh), ba=pl.BlockSpec((s, 128), lambda h: (0, 0)),
                cq=tap(0), ck=tap(nh), cv=tap(2 * nh), vec=vec, gain=pl.BlockSpec((1, dk), lambda h: (0, 0)),
                head=col(0))


def _conv_silu(shift, x, taps):
    return _silu(_conv_taps(shift, x, taps))


def _load_taps(ref):
    return [ref[t:t + 1, :] for t in range(CONV_WIDTH)]


def _gdn_fwd(proj, ba, conv_w, a_log, dt_bias, gain):
    s = proj.shape[0]
    dk = proj.shape[1] // (4 * GDN_HEADS)
    sp = _gdn_specs(s, dk)
    rows = GDN_GROUP * CHUNK

    def grp(t):
        return t.reshape(GDN_GROUP, CHUNK, t.shape[-1])

    ngroups = s // rows

    def body(q_ref, k_ref, v_ref, z_ref, ba_ref, cq_ref, ck_ref, cv_ref, al_ref, dt_ref, gain_ref, y_ref, states,
             qc, kc, vc, state):
        head = pl.program_id(0)
        qc[...] = _conv_silu(_shift_rows, q_ref[...], _load_taps(cq_ref))
        kc[...] = _conv_silu(_shift_rows, k_ref[...], _load_taps(ck_ref))
        vc[...] = _conv_silu(_shift_rows, v_ref[...], _load_taps(cv_ref))
        state[...] = jnp.zeros_like(state)

        def step(n, carry):
            base = pl.multiple_of(n * rows, rows)
            sl = pl.ds(base, rows)
            states[n] = state[...]
            ys, new_state = _gdn_group(_mm_plain, grp(qc[sl, :]), grp(kc[sl, :]), grp(vc[sl, :]), grp(z_ref[sl, :]),
                                       grp(ba_ref[sl, :]), al_ref[...], dt_ref[...], gain_ref[...], state[...], head)
            for i, y in enumerate(ys):
                y_ref[pl.ds(base + i * CHUNK, CHUNK), :] = y.astype(BF)
            state[...] = new_state
            return carry

        lax.fori_loop(0, ngroups, step, 0)

    return pl.pallas_call(
        body, name="gdn_fwd", grid=(GDN_HEADS,),
        in_specs=[sp['q'], sp['k'], sp['v'], sp['z'], sp['ba'], sp['cq'], sp['ck'], sp['cv'], sp['vec'], sp['vec'],
                  sp['gain']],
        out_specs=[sp['head'], pl.BlockSpec((None, ngroups, dk, dk), lambda h: (h, 0, 0, 0))],
        out_shape=[jax.ShapeDtypeStruct((s, GDN_HEADS * dk), BF),
                   jax.ShapeDtypeStruct((GDN_HEADS, ngroups, dk, dk), F32)],
        scratch_shapes=[pltpu.VMEM((s, dk), F32)] * 3 + [pltpu.VMEM((dk, dk), F32)],
        compiler_params=_cparams(),
    )(proj, proj, proj, proj, ba, conv_w, conv_w, conv_w, a_log, dt_bias, gain)


def _gdn_bwd(proj, ba, conv_w, a_log, dt_bias, gain, group_states, dy):
    s = proj.shape[0]
    dk = proj.shape[1] // (4 * GDN_HEADS)
    sp = _gdn_specs(s, dk)
    rows = GDN_GROUP * CHUNK
    ngroups = s // rows

    def grp(t):
        return t.reshape(GDN_GROUP, CHUNK, t.shape[-1])

    def body(q_ref, k_ref, v_ref, z_ref, ba_ref, cq_ref, ck_ref, cv_ref, al_ref, dt_ref, gain_ref, states, dy_ref,
             dq_ref, dk_ref, dv_ref, dz_ref, dba_ref, dcq_ref, dck_ref, dcv_ref, dal_ref, ddt_ref, dgain_ref,
             qc, kc, vc, dqc, dkc, dvc, dstate):
        head = pl.program_id(0)

        @pl.when(head == 0)
        def _():
            dba_ref[...] = jnp.zeros_like(dba_ref)
            dal_ref[...] = jnp.zeros_like(dal_ref)
            ddt_ref[...] = jnp.zeros_like(ddt_ref)
            dgain_ref[...] = jnp.zeros_like(dgain_ref)

        qc[...] = _conv_silu(_shift_rows, q_ref[...], _load_taps(cq_ref))
        kc[...] = _conv_silu(_shift_rows, k_ref[...], _load_taps(ck_ref))
        vc[...] = _conv_silu(_shift_rows, v_ref[...], _load_taps(cv_ref))
        dstate[...] = jnp.zeros_like(dstate)

        def bstep(i, carry):
            n = ngroups - 1 - i
            base = pl.multiple_of(n * rows, rows)
            sl = pl.ds(base, rows)

            def fn(q, k, v, z, b, al, dt, gn, st):
                return _gdn_group(_mm_diff, q, k, v, z, b, al, dt, gn, st, head)

            _, vjp = jax.vjp(fn, grp(qc[sl, :]), grp(kc[sl, :]), grp(vc[sl, :]), grp(z_ref[sl, :]),
                             grp(ba_ref[sl, :]), al_ref[...], dt_ref[...], gain_ref[...], states[n])
            dys = [dy_ref[pl.ds(base + j * CHUNK, CHUNK), :].astype(F32) for j in range(GDN_GROUP)]
            dq, dkk, dvv, dz, db, dal, ddt, dgn, dst = vjp((dys, dstate[...]))
            dqc[sl, :] = dq.reshape(rows, dk)
            dkc[sl, :] = dkk.reshape(rows, dk)
            dvc[sl, :] = dvv.reshape(rows, dk)
            dz_ref[sl, :] = dz.reshape(rows, dk).astype(BF)
            dba_ref[sl, :] += db.reshape(rows, db.shape[-1])
            dal_ref[...] += dal
            ddt_ref[...] += ddt
            dgain_ref[...] += dgn
            dstate[...] = dst
            return carry

        lax.fori_loop(0, ngroups, bstep, 0)

        for x_ref, c_ref, dpost, dx_ref, dc_ref in ((q_ref, cq_ref, dqc, dq_ref, dcq_ref),
                                                    (k_ref, ck_ref, dkc, dk_ref, dck_ref),
                                                    (v_ref, cv_ref, dvc, dv_ref, dcv_ref)):
            _, vjp = jax.vjp(lambda x, *taps: _conv_silu(_shift_vjp, x, taps), x_ref[...], *_load_taps(c_ref))
            grads = vjp(dpost[...])
            dx_ref[...] = grads[0].astype(BF)
            for t in range(CONV_WIDTH):
                dc_ref[t:t + 1, :] = grads[1 + t]

    nh = GDN_HEADS
    col_bf = jax.ShapeDtypeStruct((s, nh * dk), BF)
    tap_out = jax.ShapeDtypeStruct((CONV_WIDTH, nh * dk), F32)
    tap_spec = pl.BlockSpec((CONV_WIDTH, dk), lambda h: (0, h))
    dq, dkk, dvv, dz, dba, dcq, dck, dcv, dal, ddt, dgain = pl.pallas_call(
        body, name="gdn_bwd", grid=(nh,),
        in_specs=[sp['q'], sp['k'], sp['v'], sp['z'], sp['ba'], sp['cq'], sp['ck'], sp['cv'], sp['vec'], sp['vec'],
                  sp['gain'], pl.BlockSpec((None, ngroups, dk, dk), lambda h: (h, 0, 0, 0)), sp['head']],
        out_specs=[sp['head']] * 4 + [sp['ba'], tap_spec, tap_spec, tap_spec, sp['vec'], sp['vec'], sp['gain']],
        out_shape=[col_bf] * 4 + [jax.ShapeDtypeStruct((s, 128), F32), tap_out, tap_out, tap_out,
                                  jax.ShapeDtypeStruct((1, 128), F32), jax.ShapeDtypeStruct((1, 128), F32),
                                  jax.ShapeDtypeStruct((1, dk), F32)],
        scratch_shapes=[pltpu.VMEM((s, dk), F32)] * 6 + [pltpu.VMEM((dk, dk), F32)],
        compiler_params=_cparams(),
    )(proj, proj, proj, proj, ba, conv_w, conv_w, conv_w, a_log, dt_bias, gain, group_states, dy)
    return (jnp.concatenate([dq, dkk, dvv, dz], axis=1), dba, jnp.concatenate([dcq, dck, dcv], axis=1), dal, ddt,
            dgain)


def _lru_specs(s, bw):
    nb = LRU_BLOCKS
    return dict(
        xb=pl.BlockSpec((s, bw), lambda n: (0, n)), yb=pl.BlockSpec((s, bw), lambda n: (0, nb + n)),
        taps=pl.BlockSpec((CONV_WIDTH, bw), lambda n: (0, n)), vec=pl.BlockSpec((1, bw), lambda n: (0, n)),
        w=pl.BlockSpec((None, bw, bw), lambda n: (n, 0, 0)), b=pl.BlockSpec((None, 1, bw), lambda n: (n, 0, 0)),
        col=pl.BlockSpec((s, bw), lambda n: (0, n)))


def _lru_fwd(proj, conv_w, conv_b, lam, wr, br, wi, bi):
    s = proj.shape[0]
    bw = proj.shape[1] // (2 * LRU_BLOCKS)
    sp = _lru_specs(s, bw)

    def body(xb_ref, yb_ref, cw_ref, cb_ref, lam_ref, wr_ref, br_ref, wi_ref, bi_ref, y_ref):
        a, u, gy = _lru_pre(_mm_plain, _shift_rows, xb_ref[...], yb_ref[...], _load_taps(cw_ref), cb_ref[...],
                            lam_ref[...], wr_ref[...], br_ref[...], wi_ref[...], bi_ref[...])
        y_ref[...] = (_linscan(a, u, False) * gy).astype(BF)

    return pl.pallas_call(
        body, name="lru_fwd", grid=(LRU_BLOCKS,),
        in_specs=[sp['xb'], sp['yb'], sp['taps'], sp['vec'], sp['vec'], sp['w'], sp['b'], sp['w'], sp['b']],
        out_specs=sp['col'], out_shape=jax.ShapeDtypeStruct((s, LRU_BLOCKS * bw), BF),
        compiler_params=_cparams(),
    )(proj, proj, conv_w, conv_b, lam, wr, br, wi, bi)


def _lru_bwd(proj, conv_w, conv_b, lam, wr, br, wi, bi, dy):
    s = proj.shape[0]
    nb = LRU_BLOCKS
    bw = proj.shape[1] // (2 * nb)
    sp = _lru_specs(s, bw)

    def body(xb_ref, yb_ref, cw_ref, cb_ref, lam_ref, wr_ref, br_ref, wi_ref, bi_ref, dy_ref,
             dxb_ref, dyb_ref, dcw_ref, dcb_ref, dlam_ref, dwr_ref, dbr_ref, dwi_ref, dbi_ref):
        def pre(xb, yb, t0, t1, t2, t3, cb, lm, w_r, b_r, w_i, b_i):
            return _lru_pre(_mm_diff, _shift_vjp, xb, yb, (t0, t1, t2, t3), cb, lm, w_r, b_r, w_i, b_i)

        (a, u, gy), vjp = jax.vjp(pre, xb_ref[...], yb_ref[...], *_load_taps(cw_ref), cb_ref[...], lam_ref[...],
                                  wr_ref[...], br_ref[...], wi_ref[...], bi_ref[...])
        h = _linscan(a, u, False)
        dout = dy_ref[...].astype(F32)
        g = _linscan(_shift_rows(a, 1, True), dout * gy, True)
        grads = vjp((g * _shift_rows(h, 1, False), g, dout * h))
        dxb_ref[...] = grads[0].astype(BF)
        dyb_ref[...] = grads[1].astype(BF)
        for t in range(CONV_WIDTH):
            dcw_ref[t:t + 1, :] = grads[2 + t]
        dcb_ref[...] = grads[6]
        dlam_ref[...] = grads[7]
        dwr_ref[...] = grads[8]
        dbr_ref[...] = grads[9]
        dwi_ref[...] = grads[10]
        dbi_ref[...] = grads[11]

    outs = pl.pallas_call(
        body, name="lru_bwd", grid=(nb,),
        in_specs=[sp['xb'], sp['yb'], sp['taps'], sp['vec'], sp['vec'], sp['w'], sp['b'], sp['w'], sp['b'], sp['col']],
        out_specs=[sp['col'], sp['col'], sp['taps'], sp['vec'], sp['vec'], sp['w'], sp['b'], sp['w'], sp['b']],
        out_shape=[jax.ShapeDtypeStruct((s, nb * bw), BF), jax.ShapeDtypeStruct((s, nb * bw), BF),
                   jax.ShapeDtypeStruct((CONV_WIDTH, nb * bw), F32), jax.ShapeDtypeStruct((1, nb * bw), F32),
                   jax.ShapeDtypeStruct((1, nb * bw), F32), jax.ShapeDtypeStruct((nb, bw, bw), F32),
                   jax.ShapeDtypeStruct((nb, 1, bw), F32), jax.ShapeDtypeStruct((nb, bw, bw), F32),
                   jax.ShapeDtypeStruct((nb, 1, bw), F32)],
        compiler_params=_cparams(),
    )(proj, proj, conv_w, conv_b, lam, wr, br, wi, bi, dy)
    return (jnp.concatenate([outs[0], outs[1]], axis=1),) + tuple(outs[2:])


ROW_STEP_BYTES = 12 * 1024 * 1024


def _row_tile(rows, cols, bytes_per_row_set):
    t = 8
    while t * 2 <= rows and rows % (t * 2) == 0 and (t * 2) * cols * bytes_per_row_set <= ROW_STEP_BYTES:
        t *= 2
    return t


def _sum_slabs(a, out_dtype, name):
    n, rows, cols = a.shape
    tr = _row_tile(rows, cols, 4 * (n + 1))

    def body(*refs):
        acc = refs[0][...].astype(F32)
        for r in refs[1:n]:
            acc = acc + r[...].astype(F32)
        refs[n][...] = acc.astype(out_dtype)

    specs = [pl.BlockSpec((None, tr, cols), functools.partial(lambda i, k: (k, i, 0), k=k)) for k in range(n)]
    return pl.pallas_call(
        body, name=name, grid=(rows // tr,), in_specs=specs, out_specs=pl.BlockSpec((tr, cols), lambda i: (i, 0)),
        out_shape=jax.ShapeDtypeStruct((rows, cols), out_dtype), compiler_params=_cparams(),
    )(*([a] * n))


def _sum_keep_and_landed(g, landed, core, name):
    four, hr, cols = landed.shape
    tr = _row_tile(hr, cols, 2 * 3)
    tph = hr // tr

    def body(core_ref, g_ref, l_ref, o_ref):
        o_ref[...] = (g_ref[...].astype(F32) + l_ref[...].astype(F32)).astype(o_ref.dtype)

    grid_spec = pltpu.PrefetchScalarGridSpec(
        num_scalar_prefetch=1, grid=(four, tph),
        in_specs=[pl.BlockSpec((None, tr, cols), lambda j, r, core_ref: (j, core_ref[0] * tph + r, 0)),
                  pl.BlockSpec((None, tr, cols), lambda j, r, core_ref: (j, r, 0))],
        out_specs=pl.BlockSpec((None, tr, cols), lambda j, r, core_ref: (j, r, 0)))
    return pl.pallas_call(body, name=name, grid_spec=grid_spec, out_shape=jax.ShapeDtypeStruct(landed.shape, g.dtype),
                          compiler_params=_cparams())(core, g, landed)


def _sum_chip_parts(parts, landed, chip, name):
    _, rows, cols = parts.shape
    tr = _row_tile(rows, cols, 2 * 4 + 4)

    def body(chip_ref, p_ref, l0_ref, l1_ref, l2_ref, o_ref):
        acc = l0_ref[...].astype(F32) + l1_ref[...].astype(F32)
        o_ref[...] = (acc + l2_ref[...].astype(F32)) + p_ref[...].astype(F32)

    slab = lambda k: pl.BlockSpec((None, tr, cols), lambda i, chip_ref: (k, i, 0))
    grid_spec = pltpu.PrefetchScalarGridSpec(
        num_scalar_prefetch=1, grid=(rows // tr,),
        in_specs=[pl.BlockSpec((None, tr, cols), lambda i, chip_ref: (chip_ref[0], i, 0)), slab(0), slab(1), slab(2)],
        out_specs=pl.BlockSpec((tr, cols), lambda i, chip_ref: (i, 0)))
    return pl.pallas_call(body, name=name, grid_spec=grid_spec, out_shape=jax.ShapeDtypeStruct((rows, cols), F32),
                          compiler_params=_cparams())(chip, parts, landed, landed, landed)


_ADAM_C1 = 1.0 / (1.0 - ADAM_B1 ** ADAM_STEP)
_ADAM_C2 = 1.0 / (1.0 - ADAM_B2 ** ADAM_STEP)


def _adamw_math(w, g, m, v):
    nm = ADAM_B1 * m + (1.0 - ADAM_B1) * g
    nv = ADAM_B2 * v + (1.0 - ADAM_B2) * (g * g)
    den = jnp.sqrt(nv * _ADAM_C2) + ADAM_EPS
    inv = pl.reciprocal(den, approx=True)
    inv = inv * (2.0 - den * inv)
    delta = -ADAM_LR * ((nm * _ADAM_C1) * inv + ADAM_WD * w)
    return delta, nm, nv


def _adamw(w, g, m, v, name):
    rows, cols = w.shape
    tr = _row_tile(rows, cols, 4 * 7)

    def body(w_ref, g_ref, m_ref, v_ref, d_ref, nm_ref, nv_ref):
        d_ref[...], nm_ref[...], nv_ref[...] = _adamw_math(w_ref[...], g_ref[...], m_ref[...], v_ref[...])

    spec = pl.BlockSpec((tr, cols), lambda i: (i, 0))
    shape = jax.ShapeDtypeStruct((rows, cols), F32)
    return pl.pallas_call(
        body, name=name, grid=(rows // tr,), in_specs=[spec] * 4, out_specs=[spec] * 3, out_shape=[shape] * 3,
        compiler_params=_cparams(),
    )(w, g, m, v)


def _adamw_halves(w, m, v, layer, mine, theirs, core, name, carried=None):
    n_layers, rows, cols = w.shape
    hr = mine.shape[0]
    tr = _row_tile(hr, cols, 4 * 9)
    tph = hr // tr

    def body(core_ref, w_ref, q_ref, t_ref, m_ref, v_ref, *rest):
        g_ref, d_ref, nm_ref, nv_ref = rest[-4:]
        is_mine = (pl.program_id(0) // tph) == core_ref[0]
        g = jnp.where(is_mine, q_ref[...], t_ref[...])
        g_ref[...] = g
        d_ref[...], nm_ref[...], nv_ref[...] = _adamw_math(w_ref[...], g, m_ref[...], v_ref[...])

    slab = pl.BlockSpec((None, tr, cols), lambda i, core_ref: (layer, i, 0))

    def mine_index(i, core_ref):
        return jnp.where(i // tph == core_ref[0], i % tph, jnp.where(core_ref[0] == 0, tph - 1, 0)), 0

    def theirs_index(i, core_ref):
        return jnp.where(i // tph == core_ref[0], jnp.where(core_ref[0] == 0, 0, tph - 1), i % tph), 0

    in_specs = [slab, pl.BlockSpec((tr, cols), mine_index), pl.BlockSpec((tr, cols), theirs_index), slab, slab]
    operands = [core, w, mine, theirs, m, v]
    aliases = {}
    if carried is not None:
        in_specs += [_ANY] * 4
        aliases = {len(operands) + k: k for k in range(4)}
        operands += list(carried)
    grid_spec = pltpu.PrefetchScalarGridSpec(num_scalar_prefetch=1, grid=(rows // tr,), in_specs=in_specs,
                                             out_specs=[slab] * 4)
    return pl.pallas_call(
        body, name=name, grid_spec=grid_spec, out_shape=[jax.ShapeDtypeStruct(w.shape, F32)] * 4,
        input_output_aliases=aliases, compiler_params=_cparams(),
    )(*operands)


_ANY = pl.BlockSpec(memory_space=pl.ANY)


def _mesh_pos():
    return lax.axis_index("x"), lax.axis_index("y"), lax.axis_index("c")


def _remote(src, dst, send_sems, recv_sems, k, dev):
    return pltpu.make_async_remote_copy(src_ref=src, dst_ref=dst, send_sem=send_sems.at[k], recv_sem=recv_sems.at[k],
                                        device_id=dev, device_id_type=MESH)


STREAM_CHUNK_BYTES = 2 * 1024 * 1024
STREAM_SLOTS = 4


def _chunk_rows(rows, cols, itemsize):
    assert rows % 16 == 0, rows
    t = 16
    while t * 2 <= rows and rows % (t * 2) == 0 and (t * 2) * cols * itemsize <= STREAM_CHUNK_BYTES:
        t *= 2
    return t


def _stream_chunks(n_chunks, src_at, dst_at, buf, load_sems, send_sems, recv_sem, sibling):
    def load(k, slot):
        return pltpu.make_async_copy(src_at(k), buf.at[slot], load_sems.at[slot])

    def send(k, slot):
        return pltpu.make_async_remote_copy(src_ref=buf.at[slot], dst_ref=dst_at(k), send_sem=send_sems.at[slot],
                                            recv_sem=recv_sem, device_id=sibling, device_id_type=MESH)

    def step(k, carry):
        slot = k % STREAM_SLOTS

        @pl.when(k >= STREAM_SLOTS)
        def _():
            send(k - STREAM_SLOTS, slot).wait_send()

        load(k, slot).start()

        @pl.when(k >= 1)
        def _():
            prev = (k - 1) % STREAM_SLOTS
            load(k - 1, prev).wait()
            send(k - 1, prev).start()

        return carry

    lax.fori_loop(0, n_chunks, step, 0)
    last = (n_chunks - 1) % STREAM_SLOTS
    load(n_chunks - 1, last).wait()
    send(n_chunks - 1, last).start()
    for k in range(max(0, n_chunks - STREAM_SLOTS), n_chunks):
        send(k, k % STREAM_SLOTS).wait_send()


_HBM = pl.BlockSpec(memory_space=pltpu.HBM)
_SEM = pl.BlockSpec(memory_space=pltpu.SEMAPHORE)
_DATAFLOW = pltpu.SideEffectType.DATAFLOW_SIDE_EFFECTING


def _chip_copies(kind, ins, lands, split, send_sems, recv_sems):
    x, y, c = _mesh_pos()
    me = 2 * x + y
    chips = [(1 - x, y), (x, 1 - y), (1 - x, 1 - y)]
    pairs = []
    for i in range(len(ins)):
        for j, chip in enumerate(chips):
            pj = 2 * chip[0] + chip[1]
            if kind == 'scatter':
                src, dst, got = ins[i].at[pj], lands[i].at[j], lands[i].at[j]
            elif split[i]:
                hr = ins[i].shape[0] // 2
                rows = pl.ds(c * hr, hr)
                src, dst, got = ins[i].at[rows], lands[i].at[me, rows], lands[i].at[pj, rows]
            else:
                src, dst, got = ins[i], lands[i].at[me], lands[i].at[pj]
            k = 3 * i + j
            pairs.append((_remote(src, dst, send_sems, recv_sems, k, (*chip, c)),
                          _remote(got, got, send_sems, recv_sems, k, (*chip, c))))
    return pairs


def _chip_exchange_start(kind, srcs, land_shapes, split, name, after):
    n = len(srcs)

    def body(*refs):
        ins, lands = refs[:n], refs[n:2 * n]
        send_sems, recv_sems = refs[2 * n + 1], refs[2 * n + 2]
        token = refs[-1]
        for send, _ in _chip_copies(kind, ins, lands, split, send_sems, recv_sems):
            send.start()
        token[...] = jnp.zeros_like(token)

    hbm = lambda t: pltpu.with_memory_space_constraint(t, pltpu.HBM)
    operands = [hbm(s) for s in srcs] + [hbm(lax.empty(shp, s.dtype)) for shp, s in zip(land_shapes, srcs)] + [after]
    out = pl.pallas_call(
        body, name=name, in_specs=[_HBM] * (2 * n) + [_ANY],
        out_specs=[_SEM, _SEM] + [_HBM] * (2 * n) + [pl.BlockSpec(memory_space=pltpu.VMEM)],
        out_shape=[pltpu.SemaphoreType.DMA((3 * n,)), pltpu.SemaphoreType.DMA((3 * n,))]
        + [pltpu.HBM(s.shape, s.dtype) for s in srcs] + [pltpu.HBM(shp, s.dtype) for shp, s in zip(land_shapes, srcs)]
        + [jax.ShapeDtypeStruct((8, 128), F32)],
        input_output_aliases={i: 2 + i for i in range(2 * n)},
        compiler_params=pltpu.CompilerParams(has_side_effects=_DATAFLOW),
    )(*operands)
    return out[0], out[1], out[2:2 + n], out[2 + n:2 + 2 * n], out[-1]


def _chip_exchange_wait(kind, started, split, name, after):
    send_sems, recv_sems, srcs, lands, _ = started
    n = len(srcs)
    after = list(after) if isinstance(after, (list, tuple)) else [after]

    def body(*refs):
        ins, land_refs = refs[:n], refs[n:2 * n]
        for send, arrived in _chip_copies(kind, ins, land_refs, split, refs[2 * n], refs[2 * n + 1]):
            send.wait_send()
            arrived.wait_recv()

    out = pl.pallas_call(
        body, name=name, in_specs=[_HBM] * (2 * n) + [_SEM, _SEM] + [_ANY] * len(after), out_specs=[_HBM] * (2 * n),
        out_shape=[pltpu.HBM(s.shape, s.dtype) for s in srcs] + [pltpu.HBM(l.shape, l.dtype) for l in lands],
        input_output_aliases={i: i for i in range(2 * n)},
        compiler_params=pltpu.CompilerParams(has_side_effects=_DATAFLOW),
    )(*srcs, *lands, send_sems, recv_sems, *after)
    return out[:n], out[n:]


def _pass_to_sibling(gathered, name):
    n = len(gathered)
    tr = [_chunk_rows(g.shape[1] // 2, g.shape[2], g.dtype.itemsize) for g in gathered]

    def body(*refs):
        outs = refs[n:2 * n]
        recv_sems, load_sems, send_sems = refs[2 * n:2 * n + 3]
        bufs = refs[2 * n + 3:]
        x, y, c = _mesh_pos()
        sibling = (x, y, 1 - c)
        chips = [(1 - x, y), (x, 1 - y), (1 - x, 1 - y)]
        for i in range(n):
            hr = outs[i].shape[1] // 2
            for j, chip in enumerate(chips):
                def rows_at(k, i=i, pj=2 * chip[0] + chip[1], hr=hr):
                    return outs[i].at[pj, pl.ds(c * hr + k * tr[i], tr[i])]

                _stream_chunks(hr // tr[i], rows_at, rows_at, bufs[i], load_sems, send_sems, recv_sems.at[3 * i + j],
                               sibling)
        for i in range(n):
            hr = outs[i].shape[1] // 2
            for j, chip in enumerate(chips):
                blk = outs[i].at[2 * chip[0] + chip[1], pl.ds((1 - c) * hr, hr)]
                pltpu.make_async_remote_copy(src_ref=blk, dst_ref=blk, send_sem=send_sems.at[0],
                                             recv_sem=recv_sems.at[3 * i + j], device_id=sibling,
                                             device_id_type=MESH).wait_recv()

    return pl.pallas_call(
        body, name=name, in_specs=[_ANY] * n, out_specs=[_ANY] * n,
        out_shape=[jax.ShapeDtypeStruct(g.shape, g.dtype) for g in gathered],
        input_output_aliases={i: i for i in range(n)},
        scratch_shapes=[pltpu.SemaphoreType.DMA((3 * n,)), pltpu.SemaphoreType.DMA((STREAM_SLOTS,)),
                        pltpu.SemaphoreType.DMA((STREAM_SLOTS,))]
        + [pltpu.VMEM((STREAM_SLOTS, tr[i], g.shape[2]), g.dtype) for i, g in enumerate(gathered)],
        compiler_params=_cparams(),
    )(*gathered)


def _stream_to_sibling(srcs, halved, name):
    n = len(srcs)
    geo = []
    for s in srcs:
        rows = s.shape[1] // 2 if halved else s.shape[1]
        geo.append((s.shape[0], rows, s.shape[2], _chunk_rows(rows, s.shape[2], s.dtype.itemsize)))

    def body(*refs):
        ins, outs = refs[:n], refs[n:2 * n]
        recv_sems, load_sems, send_sems = refs[2 * n:2 * n + 3]
        bufs = refs[2 * n + 3:]
        x, y, c = _mesh_pos()
        sibling = (x, y, 1 - c)
        for i in range(n):
            slabs, rows, _, tr = geo[i]
            per_slab = rows // tr
            off = (1 - c) * rows if halved else 0

            def src_at(k, i=i, per_slab=per_slab, tr=tr, off=off):
                return ins[i].at[k // per_slab, pl.ds(off + (k % per_slab) * tr, tr)]

            def dst_at(k, i=i, per_slab=per_slab, tr=tr):
                return outs[i].at[k // per_slab, pl.ds((k % per_slab) * tr, tr)]

            _stream_chunks(slabs * per_slab, src_at, dst_at, bufs[i], load_sems, send_sems, recv_sems.at[i], sibling)
        for i in range(n):
            pltpu.make_async_remote_copy(src_ref=outs[i], dst_ref=outs[i], send_sem=send_sems.at[0],
                                         recv_sem=recv_sems.at[i], device_id=sibling, device_id_type=MESH).wait_recv()

    return pl.pallas_call(
        body, name=name, in_specs=[_ANY] * n, out_specs=[_ANY] * n,
        out_shape=[jax.ShapeDtypeStruct((g[0], g[1], g[2]), s.dtype) for g, s in zip(geo, srcs)],
        scratch_shapes=[pltpu.SemaphoreType.DMA((n,)), pltpu.SemaphoreType.DMA((STREAM_SLOTS,)),
                        pltpu.SemaphoreType.DMA((STREAM_SLOTS,))]
        + [pltpu.VMEM((STREAM_SLOTS, g[3], g[2]), s.dtype) for g, s in zip(geo, srcs)],
        compiler_params=_cparams(),
    )(*srcs)


def _reduce_scatter_begin(grads, tag, core):
    landed = _stream_to_sibling(grads, True, f"rs_swap_{tag}")
    parts = [_sum_keep_and_landed(g, l, core, f"rs_add2_{tag}_{i}") for i, (g, l) in enumerate(zip(grads, landed))]
    return _chip_exchange_start('scatter', parts, [(3,) + p.shape[1:] for p in parts], None, f"rs_scatter_start_{tag}",
                                core)


def _reduce_scatter_finish(started, tag, chip, after):
    parts, landed = _chip_exchange_wait('scatter', started, None, f"rs_scatter_wait_{tag}", after)
    mine = [_sum_chip_parts(p, l, chip, f"rs_add4_{tag}_{i}") for i, (p, l) in enumerate(zip(parts, landed))]
    theirs = _stream_to_sibling([m[None] for m in mine], False, f"rs_join_{tag}")
    return [(m, t[0]) for m, t in zip(mine, theirs)]


def _pad_cols(a, width=128):
    return jnp.pad(a, ((0, 0), (0, width - a.shape[1])))


def _mixer_forward(kind, hn, w, tables):
    if kind == 0:
        proj = _matmul(hn, w['ret_w_in'], name="ret_proj", b_sharded=True)
        og, states = _ret_fwd(proj, w['ret_gn_gain'], tables)
        return og, (proj, states)
    if kind == 1:
        proj = _matmul(hn, w['gdn_w_main'], name="gdn_proj")
        ba = _matmul(hn, w['gdn_w_small'], name="gdn_proj_ba")
        og, states = _gdn_fwd(proj, ba, w['gdn_conv_w'], w['gdn_a_log'], w['gdn_dt_bias'], w['gdn_norm_gain'])
        return og, (proj, ba, states)
    if kind == 2:
        proj = _matmul(hn, w['gla_w_main'], name="gla_proj")
        glow = _matmul(hn, w['gla_w_small'], name="gla_proj_gate")
        og, states = _gla_fwd(proj, glow, w['gla_w_gate_up'], w['gla_gate_bias'], w['gla_norm_gain'])
        return og, (proj, glow, states)
    proj = _matmul(hn, w['lru_w_in'], name="lru_proj", b_sharded=True)
    og = _lru_fwd(proj, w['lru_conv_w'], w['lru_conv_b'], w['lru_lambda'], w['lru_w_rgate'], w['lru_b_rgate'],
                  w['lru_w_igate'], w['lru_b_igate'])
    return og, (proj,)


def _mixer_backward(kind, hn, w, tables, saved, d_og, grads):
    d = hn.shape[1]
    if kind == 0:
        proj, states = saved
        d_proj, grads['ret_gn_gain'] = _ret_bwd(proj, w['ret_gn_gain'], tables, states, d_og)
        grads['ret_w_in'] = _matmul(hn, d_proj, name="ret_dw_in", ta=True, out_dtype=BF, o_sharded=True)
        return _matmul(d_proj, w['ret_w_in'], name="ret_dhn", tb=True, b_sharded=True)
    if kind == 1:
        proj, ba, states = saved
        d_proj, d_ba, grads['gdn_conv_w'], grads['gdn_a_log'], grads['gdn_dt_bias'], grads['gdn_norm_gain'] = _gdn_bwd(
            proj, ba, w['gdn_conv_w'], w['gdn_a_log'], w['gdn_dt_bias'], w['gdn_norm_gain'], states, d_og)
        d_ba = d_ba.astype(BF)
        dw_main = _matmul(hn, d_proj, name="gdn_dw_main", ta=True, out_dtype=BF)
        dw_small = _matmul(hn, d_ba, name="gdn_dw_small", ta=True, out_dtype=BF)
        dw = jnp.concatenate([dw_main, dw_small[:, :2 * GDN_HEADS]], axis=1)
        grads['gdn_w_in'] = dw.reshape(d, N_CHIPS, dw.shape[1] // N_CHIPS).transpose(1, 0, 2)
        d_hn = _matmul(d_proj, w['gdn_w_main'], name="gdn_dhn_main", tb=True)
        return _matmul(d_ba, w['gdn_w_small'], name="gdn_dhn_small", tb=True, epilogue='add', extra=d_hn)
    if kind == 2:
        proj, glow, states = saved
        d_proj, d_glow4, d_wgu, grads['gla_gate_bias'], grads['gla_norm_gain'] = _gla_bwd(
            proj, glow, w['gla_w_gate_up'], w['gla_gate_bias'], w['gla_norm_gain'], states, d_og)
        grads['gla_w_gate_up'] = d_wgu[:GLA_RANK]
        dw_main = _matmul(hn, d_proj, name="gla_dw_main", ta=True, out_dtype=BF)
        dw_small4 = _matmul(hn, d_glow4, name="gla_dw_small", ta=True, out_dtype=F32)
        dw_small = dw_small4.reshape(d, GLA_HEADS, 128)[:, :, :GLA_RANK].sum(axis=1).astype(BF)
        dw = jnp.concatenate([dw_main, dw_small], axis=1)
        grads['gla_w_in'] = dw.reshape(d, N_CHIPS, dw.shape[1] // N_CHIPS).transpose(1, 0, 2)
        d_hn = _matmul(d_proj, w['gla_w_main'], name="gla_dhn_main", tb=True)
        w_small4 = jnp.tile(w['gla_w_small'], (1, GLA_HEADS))
        return _matmul(d_glow4, w_small4, name="gla_dhn_small", tb=True, epilogue='add', extra=d_hn)
    (proj,) = saved
    (d_proj, grads['lru_conv_w'], grads['lru_conv_b'], grads['lru_lambda'], grads['lru_w_rgate'], grads['lru_b_rgate'],
     grads['lru_w_igate'], grads['lru_b_igate']) = _lru_bwd(
        proj, w['lru_conv_w'], w['lru_conv_b'], w['lru_lambda'], w['lru_w_rgate'], w['lru_b_rgate'], w['lru_w_igate'],
        w['lru_b_igate'], d_og)
    grads['lru_w_in'] = _matmul(hn, d_proj, name="lru_dw_in", ta=True, out_dtype=BF, o_sharded=True)
    return _matmul(d_proj, w['lru_w_in'], name="lru_dhn", tb=True, b_sharded=True)


_W_OUT = ('ret_w_out', 'gdn_w_out', 'gla_w_out', 'lru_w_out')
_W_IN = ('ret_w_in', 'gdn_w_in', 'gla_w_in', 'lru_w_in')


def _layer_forward(layer, x, w, tables, token=None):
    hn = _rmsnorm_fwd(x, w['norm1'][layer], f"norm1_fwd_{layer}", token)
    og, mixer_saved = _mixer_forward(layer, hn, w, tables)
    x1 = _matmul(og, w[_W_OUT[layer]], name=f"mixer_out_{layer}", epilogue='add', extra=x)
    hn2 = _rmsnorm_fwd(x1, w['norm2'][layer], f"norm2_fwd_{layer}")
    act = _matmul(hn2, w['mlp_w_up'][layer], name="mlp_up", b_sharded=True, epilogue='relu2', out_dtype=BF)
    x2 = _matmul(act, w['mlp_w_down'][layer], name="mlp_down", epilogue='add', extra=x1)
    return x2, (x, hn, mixer_saved, og, x1, hn2, act)


def _layer_backward(layer, dx2, w, tables, saved, token=None):
    x, hn, mixer_saved, og, x1, hn2, act = saved
    d = x.shape[1]
    grads = {}
    d_up = _matmul(dx2, w['mlp_w_down'][layer], name="mlp_d_up", tb=True, epilogue='dact', extra=act, out_dtype=BF,
                   token=token)
    dw_down = _matmul(act, dx2, name="mlp_dw_down", ta=True, out_dtype=BF)
    grads['mlp_w_down'] = dw_down.reshape(N_CHIPS, dw_down.shape[0] // N_CHIPS, d)
    grads['mlp_w_up'] = _matmul(hn2, d_up, name="mlp_dw_up", ta=True, out_dtype=BF, o_sharded=True)
    d_hn2 = _matmul(d_up, w['mlp_w_up'][layer], name="mlp_d_hn", tb=True, b_sharded=True)
    dx1, grads['norm2'] = _rmsnorm_bwd(x1, w['norm2'][layer], d_hn2, dx2, f"norm2_bwd_{layer}")
    w_out = w[_W_OUT[layer]]
    d_og = _matmul(dx1, w_out, name=f"mixer_d_og_{layer}", tb=True, out_dtype=BF)
    dw_out = _matmul(og, dx1, name=f"mixer_dw_out_{layer}", ta=True, out_dtype=BF)
    grads[_W_OUT[layer]] = dw_out.reshape(N_CHIPS, dw_out.shape[0] // N_CHIPS, d)
    d_hn = _mixer_backward(layer, hn, w, tables, mixer_saved, d_og, grads)
    dx, grads['norm1'] = _rmsnorm_bwd(x, w['norm1'][layer], d_hn, dx1, f"norm1_bwd_{layer}")
    return dx, grads


PACK_ROWS = 256


def _pack(arrays):
    flat = []
    for a in arrays:
        v = a.astype(F32).reshape(-1)
        v = jnp.pad(v, (0, (-v.shape[0]) % 128))
        flat.append(v.reshape(-1, 128))
    buf = jnp.concatenate(flat, axis=0)
    return jnp.pad(buf, ((0, (-buf.shape[0]) % PACK_ROWS), (0, 0)))


def _unpack(buf, shapes):
    lead = buf.shape[:-2]
    out, off = [], 0
    for shp in shapes:
        n = math.prod(shp)
        rows = -(-n // 128)
        piece = buf[..., off:off + rows, :].reshape(lead + (rows * 128,))[..., :n]
        out.append(piece.reshape(lead + tuple(shp)))
        off += rows
    return out


_WEIGHTS = ('norm1', 'norm2', 'final_norm', 'ret_w_in', 'ret_gn_gain', 'ret_w_out', 'gdn_w_in', 'gdn_conv_w',
            'gdn_a_log', 'gdn_dt_bias', 'gdn_norm_gain', 'gdn_w_out', 'gla_w_in', 'gla_w_gate_up', 'gla_gate_bias',
            'gla_norm_gain', 'gla_w_out', 'lru_w_in', 'lru_conv_w', 'lru_conv_b', 'lru_w_rgate', 'lru_b_rgate',
            'lru_w_igate', 'lru_b_igate', 'lru_lambda', 'lru_w_out', 'mlp_w_up', 'mlp_w_down')
_FWD_PARAMS = ('x',) + _WEIGHTS
_BIG = ('ret_w_in', 'ret_w_out', 'gdn_w_in', 'gdn_w_out', 'gla_w_in', 'gla_w_out', 'lru_w_in', 'lru_w_out',
        'mlp_w_up', 'mlp_w_down')
_SMALL = tuple(n for n in _WEIGHTS if n not in _BIG)
_SMALL_SHARDED = ('ret_gn_gain', 'gdn_conv_w', 'gla_w_gate_up', 'gla_gate_bias', 'gla_norm_gain', 'lru_conv_w',
                  'lru_conv_b', 'lru_lambda')


def kernel(*args):
    names = _FWD_PARAMS + ('loss_target',) + tuple('m_' + n for n in _WEIGHTS) + tuple('v_' + n for n in _WEIGHTS)
    assert len(args) == len(names)
    a = dict(zip(names, args))
    x = a['x'][0]
    target = a['loss_target'][0]
    s, d = x.shape
    chip = 2 * lax.axis_index("x") + lax.axis_index("y")

    small_local = [a[n][0] if a[n].ndim == 3 else a[n] for n in _SMALL_SHARDED]
    small_pack = _pack(small_local)
    core_arr = lax.axis_index("c").astype(jnp.int32).reshape(1)
    chip_arr = chip.astype(jnp.int32).reshape(1)

    def whole_cols(g):
        return g.transpose(1, 0, 2).reshape(g.shape[1], N_CHIPS * g.shape[2])

    def whole_rows(g):
        return g.reshape(N_CHIPS * g.shape[1], g.shape[2])

    w = {'mlp_w_up': [None] * 4, 'mlp_w_down': [None] * 4}
    for n in _SMALL:
        if n not in _SMALL_SHARDED:
            w[n] = a[n][0] if n.startswith('lru_') else a[n]
    w['gdn_a_log'], w['gdn_dt_bias'] = _pad_cols(w['gdn_a_log']), _pad_cols(w['gdn_dt_bias'])
    w['lru_b_rgate'] = w['lru_b_rgate'].reshape(LRU_BLOCKS, 1, -1)
    w['lru_b_igate'] = w['lru_b_igate'].reshape(LRU_BLOCKS, 1, -1)

    def gather_start(layer, after):
        ops = [a[_W_IN[layer]][0].astype(BF), a[_W_OUT[layer]][0].astype(BF), a['mlp_w_up'][layer].astype(BF),
               a['mlp_w_down'][layer].astype(BF)] + ([small_pack] if layer == 0 else [])
        split = [True] * 4 + ([False] if layer == 0 else [])
        return _chip_exchange_start('gather', ops, [(N_CHIPS,) + o.shape for o in ops], split,
                                    f"gather_start_{layer}", after), split

    def gather_finish(layer, started, after):
        started, split = started
        own, lands = _chip_exchange_wait('gather', started, split, f"gather_wait_{layer}", after)
        lands = list(_pass_to_sibling(lands[:4], f"gather_pass_{layer}")) + list(lands[4:])
        g_in, g_out, g_up, g_down = [lax.dynamic_update_slice_in_dim(l, o[None], chip, axis=0)
                                     for l, o in zip(lands[:4], own[:4])]
        w['mlp_w_up'][layer], w['mlp_w_down'][layer], w[_W_OUT[layer]] = g_up, whole_rows(g_down), whole_rows(g_out)
        if layer == 0:
            small_all = lax.dynamic_update_slice_in_dim(lands[4], own[4][None], chip, axis=0)
            for n, piece in zip(_SMALL_SHARDED, _unpack(small_all, [p.shape for p in small_local])):
                w[n] = whole_cols(piece)
            w['gla_w_gate_up'] = jnp.pad(w['gla_w_gate_up'], ((0, 128 - GLA_RANK), (0, 0)))
        if layer in (0, 3):
            w[_W_IN[layer]] = g_in
        else:
            name, tail = ('gdn', 2 * GDN_HEADS) if layer == 1 else ('gla', GLA_RANK)
            full = whole_cols(g_in)
            w[name + '_w_main'] = full[:, :full.shape[1] - tail]
            w[name + '_w_small'] = _pad_cols(full[:, full.shape[1] - tail:])

    tables = _ret_tables(s, d // RET_HEADS)
    saved = []
    h = x
    gather_finish(0, gather_start(0, x), x)
    for layer in range(4):
        nxt = gather_start(layer + 1, w[_W_OUT[layer]]) if layer < 3 else None
        h, sv = _layer_forward(layer, h, w, tables, None if nxt is None else nxt[0][4])
        saved.append(sv)
        if nxt is not None:
            gather_finish(layer + 1, nxt, h)
    loss_part, dh, g_final = _loss_head(h, w['final_norm'], target)
    loss = lax.psum(loss_part, ("x", "y", "c"))

    grad, delta, new_m, new_v = {}, {}, {}, {}
    small_grads = {'final_norm': g_final}
    norm_grads = {'norm1': [None] * 4, 'norm2': [None] * 4}
    mlp_upd = {'mlp_w_up': None, 'mlp_w_down': None}

    def update_layer(layer, started, after):
        red = _reduce_scatter_finish(started, str(layer), chip_arr, after)
        for n, (mine, theirs) in ((_W_IN[layer], red[0]), (_W_OUT[layer], red[1])):
            grad[n], delta[n], new_m[n], new_v[n] = _adamw_halves(a[n], a['m_' + n], a['v_' + n], 0, mine, theirs,
                                                                  core_arr, f"adamw_{n}")
        for n, (mine, theirs) in (('mlp_w_up', red[2]), ('mlp_w_down', red[3])):
            mlp_upd[n] = _adamw_halves(a[n], a['m_' + n], a['v_' + n], layer, mine, theirs, core_arr,
                                       f"adamw_{n}_{layer}", carried=mlp_upd[n])

    scatters = {}
    token = None
    for layer in reversed(range(4)):
        dh, g = _layer_backward(layer, dh, w, tables, saved[layer], token)
        saved[layer] = None
        scatters[layer] = _reduce_scatter_begin([g[_W_IN[layer]], g[_W_OUT[layer]], g['mlp_w_up'], g['mlp_w_down']],
                                                str(layer), core_arr)
        token = scatters[layer][4]
        norm_grads['norm1'][layer], norm_grads['norm2'][layer] = g['norm1'], g['norm2']
        for n in _SMALL:
            if n in g:
                small_grads[n] = g[n]
    small_grads['norm1'] = jnp.stack(norm_grads['norm1'])
    small_grads['norm2'] = jnp.stack(norm_grads['norm2'])

    full_shapes = [small_grads[n].shape for n in _SMALL]
    small_part = _pack([small_grads[n] for n in _SMALL])
    sibling_part = _stream_to_sibling([small_part[None]], False, "small_swap")[0][0]
    chip_part = _sum_slabs(jnp.stack([small_part, sibling_part]), F32, "sum_small_cores")
    small_started = _chip_exchange_start('gather', [chip_part], [(N_CHIPS,) + chip_part.shape], [False],
                                         "small_gather_start", core_arr)
    after = [token]
    for layer in (3, 2, 1):
        update_layer(layer, scatters[layer], after)
        after = [new_v[_W_IN[layer]], new_v[_W_OUT[layer]], mlp_upd['mlp_w_up'][3], mlp_upd['mlp_w_down'][3]]
    own, lands = _chip_exchange_wait('gather', small_started, [False], "small_gather_wait", after)
    total = _sum_slabs(lax.dynamic_update_slice_in_dim(lands[0], own[0][None], chip, axis=0), F32, "sum_small_chips")
    local_g = {}
    for n, full in zip(_SMALL, _unpack(total, full_shapes)):
        shp = a[n].shape
        if n in _SMALL_SHARDED:
            full = full.reshape(full.shape[0], -1)
            cq = shp[-1]
            full = lax.dynamic_slice_in_dim(full, chip * cq, cq, axis=1)
        elif n in ('gdn_a_log', 'gdn_dt_bias'):
            full = full[:, :shp[-1]]
        local_g[n] = full.reshape(shp)
    shapes = [a[n].shape for n in _SMALL]
    packed = [_pack([src[n] for n in _SMALL]) for src in
              (a, local_g, {n: a['m_' + n] for n in _SMALL}, {n: a['v_' + n] for n in _SMALL})]
    upd = _adamw(*packed, "adamw_small")
    for n, gr, dl, nm, nv in zip(_SMALL, _unpack(packed[1], shapes), *[_unpack(u, shapes) for u in upd]):
        grad[n], delta[n], new_m[n], new_v[n] = gr, dl, nm, nv
    update_layer(0, scatters[0], upd[0])
    for n in ('mlp_w_up', 'mlp_w_down'):
        grad[n], delta[n], new_m[n], new_v[n] = mlp_upd[n]

    out = [loss, dh.reshape(a['x'].shape)]
    for group in (grad, delta, new_m, new_v):
        out += [group[n].reshape(a[n].shape) for n in _WEIGHTS]
    return tuple(out)
```

```python
import functools
import math

import jax
import jax.numpy as jnp
from jax import lax
from jax.experimental import pallas as pl
from jax.experimental.pallas import tpu as pltpu

F32 = jnp.float32
BF = jnp.bfloat16
MESH = pl.DeviceIdType.MESH

NORM_EPS = 1e-6
CHUNK = 64
RET_HEADS = 8
GDN_HEADS = 16
GLA_HEADS = 4
GLA_RANK = 16
GLA_TAU = 16.0
LRU_BLOCKS = 16
LRU_C = 8.0
CONV_WIDTH = 4
ROPE_BASE = 10000.0
N_CHIPS = 4
N_DEV = 8

ADAM_LR = 0.001
ADAM_B1 = 0.9
ADAM_B2 = 0.999
ADAM_EPS = 1e-08
ADAM_WD = 0.01
ADAM_STEP = 10

VMEM_LIMIT_BYTES = 56 * 1024 * 1024
TOKEN_BLOCK = 256
ROW_BLOCK = 256
GDN_GROUP = 16


def _cparams(**kw):
    return pltpu.CompilerParams(vmem_limit_bytes=VMEM_LIMIT_BYTES, **kw)


def _bf16_parts(x, n):
    parts = []
    for _ in range(n - 1):
        p = x.astype(BF)
        parts.append(p)
        x = x - p.astype(F32)
    return parts + [x.astype(BF)]


def _raw_mm(a, b, ta, tb, hi):
    nb = a.ndim - 2
    batch = tuple(range(nb))
    dims = (((nb + (0 if ta else 1),), (nb + (1 if tb else 0),)), (batch, batch))

    def dot(p, q):
        return lax.dot_general(p, q, dims, preferred_element_type=F32)

    if not hi:
        return dot(a.astype(BF), b.astype(BF))
    if hi == 'l':
        ae = a.astype(BF)
        b0, b1, b2 = _bf16_parts(b.astype(F32), 3)
        return dot(ae, b0) + (dot(ae, b1) + dot(ae, b2))
    if hi == 'r':
        be = b.astype(BF)
        a0, a1, a2 = _bf16_parts(a.astype(F32), 3)
        return dot(a0, be) + (dot(a1, be) + dot(a2, be))
    a0, a1 = _bf16_parts(a.astype(F32), 2)
    b0, b1 = _bf16_parts(b.astype(F32), 2)
    return dot(a0, b0) + (dot(a0, b1) + dot(a1, b0))


@functools.partial(jax.custom_vjp, nondiff_argnums=(2, 3, 4))
def _mm_vjp(a, b, ta, tb, hi):
    return _raw_mm(a, b, ta, tb, hi)


def _mm_vjp_fwd(a, b, ta, tb, hi):
    return _raw_mm(a, b, ta, tb, hi), (a, b)


def _mm_vjp_bwd(ta, tb, hi, res, g):
    a, b = res
    if ta:
        da = _raw_mm(b, g, tb, True, 'l' if hi == 'r' else bool(hi))
    else:
        da = _raw_mm(g, b, False, not tb, 'r' if hi == 'r' else bool(hi))
    if tb:
        db = _raw_mm(g, a, True, ta, 'r' if hi == 'l' else bool(hi))
    else:
        db = _raw_mm(a, g, not ta, False, 'l' if hi == 'l' else bool(hi))
    return da, db


_mm_vjp.defvjp(_mm_vjp_fwd, _mm_vjp_bwd)


def _mm_diff(a, b, ta=False, tb=False, hi=False):
    return _mm_vjp(a, b, ta, tb, hi)


def _mm_plain(a, b, ta=False, tb=False, hi=False):
    return _raw_mm(a, b, ta, tb, hi)


def _shift_rows(x, k, up):
    if k == 0:
        return x
    n = x.shape[0]
    rows = lax.broadcasted_iota(jnp.int32, x.shape, 0)
    if up:
        return jnp.where(rows < n - k, pltpu.roll(x, n - k, 0), 0.0)
    return jnp.where(rows >= k, pltpu.roll(x, k, 0), 0.0)


@functools.partial(jax.custom_vjp, nondiff_argnums=(1, 2))
def _shift_vjp(x, k, up):
    return _shift_rows(x, k, up)


def _shift_vjp_fwd(x, k, up):
    return _shift_rows(x, k, up), None


def _shift_vjp_bwd(k, up, _, g):
    return (_shift_rows(g, k, not up),)


_shift_vjp.defvjp(_shift_vjp_fwd, _shift_vjp_bwd)


def _sigmoid(x):
    return 1.0 / (1.0 + jnp.exp(-x))


def _silu(x):
    return x * _sigmoid(x)


def _softplus(x):
    return jnp.maximum(x, 0.0) + jnp.log(1.0 + jnp.exp(-jnp.abs(x)))


def _gelu_tanh(x):
    return 0.5 * x * (1.0 + jnp.tanh(math.sqrt(2.0 / math.pi) * (x + 0.044715 * (x * x * x))))


def _expm1(x):
    series = x * (1.0 + x * (0.5 + x * (1.0 / 6.0 + x * (1.0 / 24.0))))
    return jnp.where(jnp.abs(x) < 0.03, series, jnp.exp(x) - 1.0)


def _rmsnorm(x, g):
    return x * lax.rsqrt(jnp.mean(x * x, axis=-1, keepdims=True) + NORM_EPS) * g


def _head_norm(o, gain, center):
    if center:
        o = o - jnp.mean(o, axis=-1, keepdims=True)
    return o * lax.rsqrt(jnp.mean(o * o, axis=-1, keepdims=True) + NORM_EPS) * gain


def _l2norm(x):
    return x * lax.rsqrt(jnp.sum(x * x, axis=-1, keepdims=True) + NORM_EPS)


def _iota2(shape, dim):
    return lax.broadcasted_iota(jnp.int32, shape, dim)


def _tri_ones(n, upper=False):
    i, j = _iota2((n, n), 0), _iota2((n, n), 1)
    return jnp.where((i <= j) if upper else (j <= i), 1.0, 0.0).astype(F32)


def _linscan(a, u, rev):
    n = a.shape[0]
    rows = _iota2(a.shape, 0)
    d = 1
    while d < n:
        if rev:
            valid = rows < n - d
            a_s, u_s = pltpu.roll(a, n - d, 0), pltpu.roll(u, n - d, 0)
        else:
            valid = rows >= d
            a_s, u_s = pltpu.roll(a, d, 0), pltpu.roll(u, d, 0)
        u = a * jnp.where(valid, u_s, 0.0) + u
        a = a * jnp.where(valid, a_s, 1.0)
        d *= 2
    return u


def _ret_group(mm, q, k, v, g, gain, state, cos, sin, dintra, qdec, kdec, cdec):
    n, _, dk = q.shape
    half = dk // 2

    def rot(t):
        t1, t2 = t[..., :half], t[..., half:]
        return jnp.concatenate([t1 * cos - t2 * sin, t1 * sin + t2 * cos], axis=-1)

    qr = rot(q)
    kr = rot(k) * (dk ** -0.5)
    o_intra = mm(mm(qr, kr, tb=True) * dintra, v)
    kv = mm(kr * kdec, v, ta=True)
    q_in = qr * qdec
    ys = []
    for i in range(n):
        o = o_intra[i] + mm(q_in[i], state)
        state = state * cdec + kv[i]
        ys.append(_head_norm(o, gain, True) * _silu(g[i]))
    return ys, state


def _gla_group(mm, q, k, v, r, glow, wgu, bias, gain, state_t):
    n, c, dk = q.shape
    logit = mm(glow.reshape(n * c, glow.shape[-1]), wgu).reshape(n, c, dk) + bias
    la = -_softplus(-logit) * (1.0 / GLA_TAU)
    cum = mm(jnp.broadcast_to(_tri_ones(c), (n, c, c)), la, hi='l')
    rows = lax.broadcasted_iota(jnp.int32, la.shape, 1)
    ref = jnp.sum(jnp.where(rows < c // 2, la, 0.0), axis=1, keepdims=True)
    tot = jnp.sum(la, axis=1, keepdims=True)
    fwd, bwd = jnp.exp(cum - ref), jnp.exp(ref - cum)
    qs = q * (dk ** -0.5)
    s_lo = mm(qs * fwd, k * bwd, tb=True)
    s_up = mm(qs * bwd, k * fwd, tb=True)
    i, j = _iota2((c, c), 0), _iota2((c, c), 1)
    o_intra = mm(jnp.where(i >= j, s_lo, s_up), v)
    q_in = qs * jnp.exp(cum)
    kv_t = mm(v, k * jnp.exp(tot - cum), ta=True)
    dec = jnp.exp(tot)
    ys = []
    for m in range(n):
        o = o_intra[m] + mm(q_in[m], state_t, tb=True)
        state_t = state_t * dec[m] + kv_t[m]
        ys.append(_head_norm(o, gain, False) * _silu(r[m]))
    return ys, state_t


def _gdn_group(mm, q, k, v, z, ba, a_log, dt_bias, gain, state, head):
    g, c, dk = q.shape
    lanes = lax.broadcasted_iota(jnp.int32, (1, 1, ba.shape[-1]), 2)
    oh_b = jnp.where(lanes == head, 1.0, 0.0).astype(F32)
    oh_a = jnp.where(lanes == head + GDN_HEADS, 1.0, 0.0).astype(F32)
    beta = _sigmoid(jnp.sum(ba * oh_b, axis=-1, keepdims=True))
    a_logit = jnp.sum(ba * oh_a, axis=-1, keepdims=True)
    a_h = jnp.sum(a_log * oh_b[0], axis=-1, keepdims=True)
    dt_h = jnp.sum(dt_bias * oh_b[0], axis=-1, keepdims=True)
    la = -jnp.exp(a_h) * _softplus(a_logit + dt_h)
    q = _l2norm(q) * (dk ** -0.5)
    k = _l2norm(k)
    tri = jnp.broadcast_to(_tri_ones(c), (g, c, c))
    tri_up = jnp.broadcast_to(_tri_ones(c, upper=True), (g, c, c))
    cum_k = mm(tri, la * jnp.ones((g, c, dk), F32), hi='l')
    la_sq = la * jnp.ones((g, c, c), F32)
    cum_i = mm(tri, la_sq, hi='l')
    cum_j = mm(la_sq, tri_up, ta=True, hi='r')
    i, j = _iota2((c, c), 0), _iota2((c, c), 1)
    strict = i > j
    rel = jnp.where(strict, jnp.exp(jnp.where(strict, cum_i - cum_j, 0.0)), 0.0)
    a_mat = beta * rel * mm(k, k, tb=True)
    inv = jnp.where(i == j, 1.0, 0.0).astype(F32) - a_mat
    power = a_mat
    for _ in range(int(math.log2(c)) - 1):
        power = mm(power, power, hi=True)
        inv = inv + mm(inv, power, hi=True)
    tot = jnp.sum(la, axis=1, keepdims=True)
    u = mm(inv, beta * v, hi=True)
    w = mm(inv, (beta * jnp.exp(cum_k)) * k, hi=True)
    k_end = k * jnp.exp(tot - cum_k)
    di, dj = _iota2((dk, dk), 0), _iota2((dk, dk), 1)
    trans = jnp.exp(tot) * jnp.where(di == dj, 1.0, 0.0).astype(F32) - mm(k_end, w, ta=True)
    inject = mm(k_end, u, ta=True)
    ys = []
    for n in range(g):
        state = mm(trans[n], state) + inject[n]
        ys.append(_head_norm(mm(q[n], state), gain, False) * _silu(z[n]))
    return ys, state


def _conv_taps(shift, x, taps):
    out = None
    for tap, w in enumerate(taps):
        term = shift(x, CONV_WIDTH - 1 - tap, False) * w
        out = term if out is None else out + term
    return out


def _lru_pre(mm, shift, xb, yb, taps, cb, lam, wr, br, wi, bi):
    xb = _conv_taps(shift, xb, taps) + cb
    r = _sigmoid(mm(xb, wr) + br)
    i = _sigmoid(mm(xb, wi) + bi)
    log_a = (-LRU_C) * _softplus(-lam) * r
    a = jnp.exp(log_a)
    u = jnp.sqrt(-_expm1(2.0 * log_a)) * (i * xb)
    return a, u, _gelu_tanh(yb)


_TOKEN_SPEC = pl.BlockSpec((8, 128), lambda *_: (0, 0))


MATMUL_VMEM_BUDGET = 40 * 1024 * 1024
_TILE_SIZES = (3072, 2048, 1536, 1024, 768, 512, 384, 256, 128)


def _tile_choices(extent):
    return [t for t in _TILE_SIZES if t <= extent and extent % t == 0] or [extent]


def _matmul_tiles(m, n, kdim, n_unit, k_unit, a_item, b_item, o_item, e_item):
    best = None
    for bm in _tile_choices(m):
        for bn in _tile_choices(n_unit):
            for bk in _tile_choices(k_unit):
                nk = kdim // bk
                vmem = 2 * (bm * bk * a_item + bk * bn * b_item + bm * bn * (o_item + e_item)) + bm * bn * 4 * (2 if nk > 1 else 1)
                if vmem > MATMUL_VMEM_BUDGET:
                    continue
                steps = (m // bm) * (n // bn) * nk
                traffic = m * kdim * a_item * (1 if nk == 1 else n // bn) + kdim * n * b_item * (m // bm)
                if best is None or (steps, traffic) < best[0]:
                    best = ((steps, traffic), (bm, bn, bk))
    return best[1]


def _matmul(a, b, *, name, ta=False, tb=False, out_dtype=F32, b_sharded=False, o_sharded=False,
            epilogue=None, extra=None, token=None):
    m, kdim = (a.shape[1], a.shape[0]) if ta else a.shape
    if b_sharded:
        nq = b.shape[2]
        n = b.shape[1] if tb else N_CHIPS * nq
        assert kdim == (N_CHIPS * nq if tb else b.shape[1])
    else:
        n = b.shape[0] if tb else b.shape[1]
        assert kdim == (b.shape[1] if tb else b.shape[0])
    n_unit = b.shape[2] if (b_sharded and not tb) else (n // N_CHIPS if o_sharded else n)
    k_unit = b.shape[2] if (b_sharded and tb) else kdim
    bm, bn, bk = _matmul_tiles(m, n, kdim, n_unit, k_unit, a.dtype.itemsize, b.dtype.itemsize,
                               jnp.dtype(out_dtype).itemsize, 0 if extra is None else extra.dtype.itemsize)
    nk = kdim // bk
    grid = (m // bm, n // bn, nk)

    a_spec = pl.BlockSpec((bk, bm), lambda i, j, k: (k, i)) if ta else pl.BlockSpec((bm, bk), lambda i, j, k: (i, k))
    if b_sharded and not tb:
        per = b.shape[2] // bn
        b_spec = pl.BlockSpec((None, bk, bn), lambda i, j, k: (j // per, k, j % per))
    elif b_sharded:
        per = b.shape[2] // bk
        b_spec = pl.BlockSpec((None, bn, bk), lambda i, j, k: (k // per, j, k % per))
    elif tb:
        b_spec = pl.BlockSpec((bn, bk), lambda i, j, k: (j, k))
    else:
        b_spec = pl.BlockSpec((bk, bn), lambda i, j, k: (k, j))
    if o_sharded:
        per_o = (n // N_CHIPS) // bn
        o_spec = pl.BlockSpec((None, bm, bn), lambda i, j, k: (j // per_o, i, j % per_o))
        out_shape = jax.ShapeDtypeStruct((N_CHIPS, m, n // N_CHIPS), out_dtype)
    else:
        o_spec = pl.BlockSpec((bm, bn), lambda i, j, k: (i, j))
        out_shape = jax.ShapeDtypeStruct((m, n), out_dtype)
    in_specs = [a_spec, b_spec]
    operands = [a, b]
    if extra is not None:
        in_specs.append(pl.BlockSpec((bm, bn), lambda i, j, k: (i, j)))
        operands.append(extra)
    if token is not None:
        in_specs.append(_TOKEN_SPEC)
        operands.append(token)

    def finish(acc, e_ref, o_ref):
        if epilogue == 'add':
            acc = acc + e_ref[...].astype(F32)
        elif epilogue == 'relu2':
            acc = jnp.square(jnp.maximum(acc, 0.0))
        elif epilogue == 'dact':
            acc = acc * (2.0 * jnp.sqrt(e_ref[...].astype(F32)))
        o_ref[...] = acc.astype(o_ref.dtype)

    def body_one_step(*refs):
        finish(_raw_mm(refs[0][...], refs[1][...], ta, tb, False), refs[2] if extra is not None else None, refs[-1])

    def body(*refs):
        a_ref, b_ref = refs[0], refs[1]
        e_ref = refs[2] if extra is not None else None
        o_ref, acc_ref = refs[-2], refs[-1]
        k = pl.program_id(2)

        @pl.when(k == 0)
        def _():
            acc_ref[...] = jnp.zeros_like(acc_ref)

        acc_ref[...] += _raw_mm(a_ref[...], b_ref[...], ta, tb, False)

        @pl.when(k == nk - 1)
        def _():
            finish(acc_ref[...], e_ref, o_ref)

    return pl.pallas_call(
        body_one_step if nk == 1 else body, name=name, grid=grid, in_specs=in_specs, out_specs=o_spec,
        out_shape=out_shape, scratch_shapes=[] if nk == 1 else [pltpu.VMEM((bm, bn), F32)],
        compiler_params=_cparams(),
    )(*operands)


def _rmsnorm_fwd(x, g, name, token=None):
    s, d = x.shape

    def body(x_ref, g_ref, *rest):
        rest[-1][...] = _rmsnorm(x_ref[...], g_ref[...]).astype(BF)

    return pl.pallas_call(
        body, name=name, grid=(s // ROW_BLOCK,),
        in_specs=[pl.BlockSpec((ROW_BLOCK, d), lambda i: (i, 0)), pl.BlockSpec((1, d), lambda i: (0, 0))]
        + ([] if token is None else [_TOKEN_SPEC]),
        out_specs=pl.BlockSpec((ROW_BLOCK, d), lambda i: (i, 0)),
        out_shape=jax.ShapeDtypeStruct((s, d), BF), compiler_params=_cparams(),
    )(x, g.reshape(1, d), *([] if token is None else [token]))


def _rmsnorm_bwd(x, g, dh, dres, name):
    s, d = x.shape

    def body(x_ref, g_ref, dh_ref, dres_ref, dx_ref, dg_ref):
        _, vjp = jax.vjp(_rmsnorm, x_ref[...], g_ref[...])
        dx, dg = vjp(dh_ref[...].astype(F32))
        dx_ref[...] = dres_ref[...] + dx

        @pl.when(pl.program_id(0) == 0)
        def _():
            dg_ref[...] = jnp.zeros_like(dg_ref)

        dg_ref[...] += dg

    row = pl.BlockSpec((ROW_BLOCK, d), lambda i: (i, 0))
    vec = pl.BlockSpec((1, d), lambda i: (0, 0))
    dx, dg = pl.pallas_call(
        body, name=name, grid=(s // ROW_BLOCK,), in_specs=[row, vec, row, row], out_specs=[row, vec],
        out_shape=[jax.ShapeDtypeStruct((s, d), F32), jax.ShapeDtypeStruct((1, d), F32)],
        compiler_params=_cparams(),
    )(x, g.reshape(1, d), dh, dres)
    return dx, dg.reshape(d)


def _loss_head(x, g, target):
    s, d = x.shape

    def loss_fn(xv, gv, tv):
        err = _rmsnorm(xv, gv) - tv
        return 0.5 * jnp.sum(jnp.mean(err * err, axis=-1, keepdims=True), axis=0, keepdims=True)

    def body(x_ref, g_ref, t_ref, dx_ref, dg_ref, loss_ref):
        tv = t_ref[...]
        loss, vjp = jax.vjp(lambda xv, gv: loss_fn(xv, gv, tv), x_ref[...], g_ref[...])
        dx, dg = vjp(jnp.ones((1, 1), F32))
        dx_ref[...] = dx

        @pl.when(pl.program_id(0) == 0)
        def _():
            dg_ref[...] = jnp.zeros_like(dg_ref)
            loss_ref[...] = jnp.zeros_like(loss_ref)

        dg_ref[...] += dg
        loss_ref[...] += loss * jnp.ones_like(loss_ref)

    row = pl.BlockSpec((ROW_BLOCK, d), lambda i: (i, 0))
    vec = pl.BlockSpec((1, d), lambda i: (0, 0))
    dx, dg, loss = pl.pallas_call(
        body, name="loss_head", grid=(s // ROW_BLOCK,), in_specs=[row, vec, row],
        out_specs=[row, vec, pl.BlockSpec((1, 128), lambda i: (0, 0))],
        out_shape=[jax.ShapeDtypeStruct((s, d), F32), jax.ShapeDtypeStruct((1, d), F32),
                   jax.ShapeDtypeStruct((1, 128), F32)],
        compiler_params=_cparams(),
    )(x, g.reshape(1, d), target)
    return loss[0, 0], dx, dg.reshape(d)


def _ret_tables(s, dk):
    h = jnp.arange(RET_HEADS, dtype=F32)
    log_gamma = jnp.log1p(-jnp.exp2(-5.0 - h))
    pos = jnp.arange(CHUNK, dtype=F32)
    dist = jnp.abs(pos[:, None] - pos[None, :])
    dintra = jnp.exp(log_gamma[:, None, None] * dist)
    qdec = jnp.exp(log_gamma[:, None] * (pos + 1.0))[:, :, None]
    kdec = jnp.exp(log_gamma[:, None] * (CHUNK - 1.0 - pos))[:, :, None]
    cdec = jnp.exp(log_gamma * CHUNK)[:, None, None]
    inv = ROPE_BASE ** (-jnp.arange(0, dk, 2, dtype=F32) / dk)
    ang = jnp.arange(s, dtype=F32)[:, None] * inv[None, :]
    return jnp.cos(ang), jnp.sin(ang), dintra, qdec, kdec, cdec


def _ret_specs(s, dk, dv, tb, rev):
    nh = RET_HEADS
    nb = s // tb
    bi = (lambda b: nb - 1 - b) if rev else (lambda b: b)
    voff = 2 * nh * dk // dv
    cpb = tb // CHUNK
    return dict(
        q=pl.BlockSpec((tb, dk), lambda h, b: (bi(b), h)),
        k=pl.BlockSpec((tb, dk), lambda h, b: (bi(b), nh + h)),
        v=pl.BlockSpec((tb, dv), lambda h, b: (bi(b), voff + h)),
        g=pl.BlockSpec((tb, dv), lambda h, b: (bi(b), voff + nh + h)),
        gain=pl.BlockSpec((None, 1, dv), lambda h, b: (h, 0, 0)),
        cs=pl.BlockSpec((tb, dk // 2), lambda h, b: (bi(b), 0)),
        dintra=pl.BlockSpec((None, CHUNK, CHUNK), lambda h, b: (h, 0, 0)),
        dec=pl.BlockSpec((None, CHUNK, 1), lambda h, b: (h, 0, 0)),
        cdec=pl.BlockSpec((None, 1, 1), lambda h, b: (h, 0, 0)),
        hv=pl.BlockSpec((tb, dv), lambda h, b: (bi(b), h)),
        hk=pl.BlockSpec((tb, dk), lambda h, b: (bi(b), h)),
        st=pl.BlockSpec((None, None, dk, dv), lambda h, b: (h, bi(b), 0, 0)),
    )


def _group(t):
    return t.reshape(t.shape[0] // CHUNK, CHUNK, t.shape[-1])


def _ret_fwd(proj, gain, tables):
    s = proj.shape[0]
    d = proj.shape[1] // 6
    dk, dv = d // RET_HEADS, 2 * d // RET_HEADS
    tb = min(TOKEN_BLOCK, s)
    sp = _ret_specs(s, dk, dv, tb, False)
    cos, sin, dintra, qdec, kdec, cdec = tables

    def body(q_ref, k_ref, v_ref, g_ref, gain_ref, cos_ref, sin_ref, di_ref, qd_ref, kd_ref, cd_ref,
             y_ref, st_ref, state):
        @pl.when(pl.program_id(1) == 0)
        def _():
            state[...] = jnp.zeros_like(state)

        st_ref[...] = state[...]
        ys, new_state = _ret_group(_mm_plain, _group(q_ref[...]), _group(k_ref[...]), _group(v_ref[...]),
                                   _group(g_ref[...]), gain_ref[...], state[...], _group(cos_ref[...]),
                                   _group(sin_ref[...]), di_ref[...], qd_ref[...], kd_ref[...], cd_ref[...])
        for c, y in enumerate(ys):
            y_ref[pl.ds(c * CHUNK, CHUNK), :] = y.astype(BF)
        state[...] = new_state

    return pl.pallas_call(
        body, name="ret_fwd", grid=(RET_HEADS, s // tb),
        in_specs=[sp['q'], sp['k'], sp['v'], sp['g'], sp['gain'], sp['cs'], sp['cs'], sp['dintra'], sp['dec'],
                  sp['dec'], sp['cdec']],
        out_specs=[sp['hv'], sp['st']],
        out_shape=[jax.ShapeDtypeStruct((s, RET_HEADS * dv), BF),
                   jax.ShapeDtypeStruct((RET_HEADS, s // tb, dk, dv), F32)],
        scratch_shapes=[pltpu.VMEM((dk, dv), F32)], compiler_params=_cparams(),
    )(proj, proj, proj, proj, gain.reshape(RET_HEADS, 1, dv), cos, sin, dintra, qdec, kdec, cdec)


def _ret_bwd(proj, gain, tables, states, dy):
    s = proj.shape[0]
    d = proj.shape[1] // 6
    dk, dv = d // RET_HEADS, 2 * d // RET_HEADS
    tb = min(TOKEN_BLOCK, s)
    cpb = tb // CHUNK
    sp = _ret_specs(s, dk, dv, tb, True)
    cos, sin, dintra, qdec, kdec, cdec = tables

    def body(q_ref, k_ref, v_ref, g_ref, gain_ref, cos_ref, sin_ref, di_ref, qd_ref, kd_ref, cd_ref, st_ref,
             dy_ref, dq_ref, dk_ref, dv_ref, dg_ref, dgain_ref, dstate):
        @pl.when(pl.program_id(1) == 0)
        def _():
            dstate[...] = jnp.zeros_like(dstate)
            dgain_ref[...] = jnp.zeros_like(dgain_ref)

        cos_g, sin_g = _group(cos_ref[...]), _group(sin_ref[...])
        di, qd, kd, cd = di_ref[...], qd_ref[...], kd_ref[...], cd_ref[...]

        def fn(q, k, v, g, gn, st):
            return _ret_group(_mm_diff, q, k, v, g, gn, st, cos_g, sin_g, di, qd, kd, cd)

        _, vjp = jax.vjp(fn, _group(q_ref[...]), _group(k_ref[...]), _group(v_ref[...]), _group(g_ref[...]),
                         gain_ref[...], st_ref[...])
        dys = [dy_ref[pl.ds(c * CHUNK, CHUNK), :].astype(F32) for c in range(cpb)]
        dq, dkk, dvv, dg, dgn, dst = vjp((dys, dstate[...]))
        dq_ref[...] = dq.reshape(tb, dk).astype(BF)
        dk_ref[...] = dkk.reshape(tb, dk).astype(BF)
        dv_ref[...] = dvv.reshape(tb, dv).astype(BF)
        dg_ref[...] = dg.reshape(tb, dv).astype(BF)
        dgain_ref[...] += dgn
        dstate[...] = dst

    dq, dkk, dvv, dg, dgain = pl.pallas_call(
        body, name="ret_bwd", grid=(RET_HEADS, s // tb),
        in_specs=[sp['q'], sp['k'], sp['v'], sp['g'], sp['gain'], sp['cs'], sp['cs'], sp['dintra'], sp['dec'],
                  sp['dec'], sp['cdec'], sp['st'], sp['hv']],
        out_specs=[sp['hk'], sp['hk'], sp['hv'], sp['hv'], sp['gain']],
        out_shape=[jax.ShapeDtypeStruct((s, RET_HEADS * dk), BF), jax.ShapeDtypeStruct((s, RET_HEADS * dk), BF),
                   jax.ShapeDtypeStruct((s, RET_HEADS * dv), BF), jax.ShapeDtypeStruct((s, RET_HEADS * dv), BF),
                   jax.ShapeDtypeStruct((RET_HEADS, 1, dv), F32)],
        scratch_shapes=[pltpu.VMEM((dk, dv), F32)], compiler_params=_cparams(),
    )(proj, proj, proj, proj, gain.reshape(RET_HEADS, 1, dv), cos, sin, dintra, qdec, kdec, cdec, states, dy)
    return jnp.concatenate([dq, dkk, dvv, dg], axis=1), dgain.reshape(RET_HEADS, dv)


def _gla_specs(s, dk, dv, tb, rev):
    nh = GLA_HEADS
    nb = s // tb
    bi = (lambda b: nb - 1 - b) if rev else (lambda b: b)
    voff = 2 * nh * dk // dv
    cpb = tb // CHUNK
    return dict(
        q=pl.BlockSpec((tb, dk), lambda h, b: (bi(b), h)),
        k=pl.BlockSpec((tb, dk), lambda h, b: (bi(b), nh + h)),
        v=pl.BlockSpec((tb, dv), lambda h, b: (bi(b), voff + h)),
        r=pl.BlockSpec((tb, dv), lambda h, b: (bi(b), voff + nh + h)),
        glow=pl.BlockSpec((tb, 128), lambda h, b: (bi(b), 0)),
        wgu=pl.BlockSpec((128, dk), lambda h, b: (0, h)),
        bias=pl.BlockSpec((1, dk), lambda h, b: (0, h)),
        gain=pl.BlockSpec((None, 1, dv), lambda h, b: (h, 0, 0)),
        hv=pl.BlockSpec((tb, dv), lambda h, b: (bi(b), h)),
        hk=pl.BlockSpec((tb, dk), lambda h, b: (bi(b), h)),
        hg=pl.BlockSpec((tb, 128), lambda h, b: (bi(b), h)),
        st=pl.BlockSpec((None, None, dv, dk), lambda h, b: (h, bi(b), 0, 0)),
    )


def _gla_fwd(proj, glow, wgu, bias, gain):
    s = proj.shape[0]
    d = proj.shape[1] // 3
    dk, dv = d // 2 // GLA_HEADS, d // GLA_HEADS
    tb = min(TOKEN_BLOCK, s)
    sp = _gla_specs(s, dk, dv, tb, False)

    def body(q_ref, k_ref, v_ref, r_ref, gl_ref, wgu_ref, b_ref, gain_ref, y_ref, st_ref, state):
        @pl.when(pl.program_id(1) == 0)
        def _():
            state[...] = jnp.zeros_like(state)

        st_ref[...] = state[...]
        ys, new_state = _gla_group(_mm_plain, _group(q_ref[...]), _group(k_ref[...]), _group(v_ref[...]),
                                   _group(r_ref[...]), _group(gl_ref[...]), wgu_ref[...], b_ref[...], gain_ref[...],
                                   state[...])
        for c, y in enumerate(ys):
            y_ref[pl.ds(c * CHUNK, CHUNK), :] = y.astype(BF)
        state[...] = new_state

    return pl.pallas_call(
        body, name="gla_fwd", grid=(GLA_HEADS, s // tb),
        in_specs=[sp['q'], sp['k'], sp['v'], sp['r'], sp['glow'], sp['wgu'], sp['bias'], sp['gain']],
        out_specs=[sp['hv'], sp['st']],
        out_shape=[jax.ShapeDtypeStruct((s, GLA_HEADS * dv), BF),
                   jax.ShapeDtypeStruct((GLA_HEADS, s // tb, dv, dk), F32)],
        scratch_shapes=[pltpu.VMEM((dv, dk), F32)], compiler_params=_cparams(),
    )(proj, proj, proj, proj, glow, wgu, bias, gain.reshape(GLA_HEADS, 1, dv))


def _gla_bwd(proj, glow, wgu, bias, gain, states, dy):
    s = proj.shape[0]
    d = proj.shape[1] // 3
    dk, dv = d // 2 // GLA_HEADS, d // GLA_HEADS
    tb = min(TOKEN_BLOCK, s)
    cpb = tb // CHUNK
    sp = _gla_specs(s, dk, dv, tb, True)

    def body(q_ref, k_ref, v_ref, r_ref, gl_ref, wgu_ref, b_ref, gain_ref, st_ref, dy_ref,
             dq_ref, dk_ref, dv_ref, dr_ref, dgl_ref, dwgu_ref, db_ref, dgain_ref, dstate):
        @pl.when(pl.program_id(1) == 0)
        def _():
            dstate[...] = jnp.zeros_like(dstate)
            dwgu_ref[...] = jnp.zeros_like(dwgu_ref)
            db_ref[...] = jnp.zeros_like(db_ref)
            dgain_ref[...] = jnp.zeros_like(dgain_ref)

        def fn(q, k, v, r, gl, w, b, gn, st):
            return _gla_group(_mm_diff, q, k, v, r, gl, w, b, gn, st)

        _, vjp = jax.vjp(fn, _group(q_ref[...]), _group(k_ref[...]), _group(v_ref[...]), _group(r_ref[...]),
                         _group(gl_ref[...]), wgu_ref[...], b_ref[...], gain_ref[...], st_ref[...])
        dys = [dy_ref[pl.ds(c * CHUNK, CHUNK), :].astype(F32) for c in range(cpb)]
        dq, dkk, dvv, dr, dgl, dw, db, dgn, dst = vjp((dys, dstate[...]))
        dq_ref[...] = dq.reshape(tb, dk).astype(BF)
        dk_ref[...] = dkk.reshape(tb, dk).astype(BF)
        dv_ref[...] = dvv.reshape(tb, dv).astype(BF)
        dr_ref[...] = dr.reshape(tb, dv).astype(BF)
        dgl_ref[...] = dgl.reshape(tb, dgl.shape[-1]).astype(BF)
        dwgu_ref[...] += dw
        db_ref[...] += db
        dgain_ref[...] += dgn
        dstate[...] = dst

    nh = GLA_HEADS
    dq, dkk, dvv, dr, dgl, dwgu, db, dgain = pl.pallas_call(
        body, name="gla_bwd", grid=(nh, s // tb),
        in_specs=[sp['q'], sp['k'], sp['v'], sp['r'], sp['glow'], sp['wgu'], sp['bias'], sp['gain'], sp['st'],
                  sp['hv']],
        out_specs=[sp['hk'], sp['hk'], sp['hv'], sp['hv'], sp['hg'], sp['wgu'], sp['bias'], sp['gain']],
        out_shape=[jax.ShapeDtypeStruct((s, nh * dk), BF), jax.ShapeDtypeStruct((s, nh * dk), BF),
                   jax.ShapeDtypeStruct((s, nh * dv), BF), jax.ShapeDtypeStruct((s, nh * dv), BF),
                   jax.ShapeDtypeStruct((s, nh * 128), BF), jax.ShapeDtypeStruct((128, nh * dk), F32),
                   jax.ShapeDtypeStruct((1, nh * dk), F32), jax.ShapeDtypeStruct((nh, 1, dv), F32)],
        scratch_shapes=[pltpu.VMEM((dv, dk), F32)], compiler_params=_cparams(),
    )(proj, proj, proj, proj, glow, wgu, bias, gain.reshape(nh, 1, dv), states, dy)
    return jnp.concatenate([dq, dkk, dvv, dr], axis=1), dgl, dwgu, db, dgain.reshape(nh, dv)


def _gdn_specs(s, dk):
    nh = GDN_HEADS
    col = lambda off: pl.BlockSpec((s, dk), lambda h: (0, off + h))
    tap = lambda off: pl.BlockSpec((CONV_WIDTH, dk), lambda h: (0, off + h))
    vec = pl.BlockSpec((1, 128), lambda h: (0, 0))
    return dict(q=col(0), k=col(nh), v=col(2 * nh), z=col(3 * nh), ba=pl.BlockSpec((s, 128), lambda h: (0, 0)),
                cq=tap(0), ck=tap(nh), cv=tap(2 * nh), vec=vec, gain=pl.BlockSpec((1, dk), lambda h: (0, 0)),
                head=col(0))


def _conv_silu(shift, x, taps):
    return _silu(_conv_taps(shift, x, taps))


def _load_taps(ref):
    return [ref[t:t + 1, :] for t in range(CONV_WIDTH)]


def _gdn_fwd(proj, ba, conv_w, a_log, dt_bias, gain):
    s = proj.shape[0]
    dk = proj.shape[1] // (4 * GDN_HEADS)
    sp = _gdn_specs(s, dk)
    rows = GDN_GROUP * CHUNK

    def grp(t):
        return t.reshape(GDN_GROUP, CHUNK, t.shape[-1])

    ngroups = s // rows

    def body(q_ref, k_ref, v_ref, z_ref, ba_ref, cq_ref, ck_ref, cv_ref, al_ref, dt_ref, gain_ref, y_ref, states,
             qc, kc, vc, state):
        head = pl.program_id(0)
        qc[...] = _conv_silu(_shift_rows, q_ref[...], _load_taps(cq_ref))
        kc[...] = _conv_silu(_shift_rows, k_ref[...], _load_taps(ck_ref))
        vc[...] = _conv_silu(_shift_rows, v_ref[...], _load_taps(cv_ref))
        state[...] = jnp.zeros_like(state)

        def step(n, carry):
            base = pl.multiple_of(n * rows, rows)
            sl = pl.ds(base, rows)
            states[n] = state[...]
            ys, new_state = _gdn_group(_mm_plain, grp(qc[sl, :]), grp(kc[sl, :]), grp(vc[sl, :]), grp(z_ref[sl, :]),
                                       grp(ba_ref[sl, :]), al_ref[...], dt_ref[...], gain_ref[...], state[...], head)
            for i, y in enumerate(ys):
                y_ref[pl.ds(base + i * CHUNK, CHUNK), :] = y.astype(BF)
            state[...] = new_state
            return carry

        lax.fori_loop(0, ngroups, step, 0)

    return pl.pallas_call(
        body, name="gdn_fwd", grid=(GDN_HEADS,),
        in_specs=[sp['q'], sp['k'], sp['v'], sp['z'], sp['ba'], sp['cq'], sp['ck'], sp['cv'], sp['vec'], sp['vec'],
                  sp['gain']],
        out_specs=[sp['head'], pl.BlockSpec((None, ngroups, dk, dk), lambda h: (h, 0, 0, 0))],
        out_shape=[jax.ShapeDtypeStruct((s, GDN_HEADS * dk), BF),
                   jax.ShapeDtypeStruct((GDN_HEADS, ngroups, dk, dk), F32)],
        scratch_shapes=[pltpu.VMEM((s, dk), F32)] * 3 + [pltpu.VMEM((dk, dk), F32)],
        compiler_params=_cparams(),
    )(proj, proj, proj, proj, ba, conv_w, conv_w, conv_w, a_log, dt_bias, gain)


def _gdn_bwd(proj, ba, conv_w, a_log, dt_bias, gain, group_states, dy):
    s = proj.shape[0]
    dk = proj.shape[1] // (4 * GDN_HEADS)
    sp = _gdn_specs(s, dk)
    rows = GDN_GROUP * CHUNK
    ngroups = s // rows

    def grp(t):
        return t.reshape(GDN_GROUP, CHUNK, t.shape[-1])

    def body(q_ref, k_ref, v_ref, z_ref, ba_ref, cq_ref, ck_ref, cv_ref, al_ref, dt_ref, gain_ref, states, dy_ref,
             dq_ref, dk_ref, dv_ref, dz_ref, dba_ref, dcq_ref, dck_ref, dcv_ref, dal_ref, ddt_ref, dgain_ref,
             qc, kc, vc, dqc, dkc, dvc, dstate):
        head = pl.program_id(0)

        @pl.when(head == 0)
        def _():
            dba_ref[...] = jnp.zeros_like(dba_ref)
            dal_ref[...] = jnp.zeros_like(dal_ref)
            ddt_ref[...] = jnp.zeros_like(ddt_ref)
            dgain_ref[...] = jnp.zeros_like(dgain_ref)

        qc[...] = _conv_silu(_shift_rows, q_ref[...], _load_taps(cq_ref))
        kc[...] = _conv_silu(_shift_rows, k_ref[...], _load_taps(ck_ref))
        vc[...] = _conv_silu(_shift_rows, v_ref[...], _load_taps(cv_ref))
        dstate[...] = jnp.zeros_like(dstate)

        def bstep(i, carry):
            n = ngroups - 1 - i
            base = pl.multiple_of(n * rows, rows)
            sl = pl.ds(base, rows)

            def fn(q, k, v, z, b, al, dt, gn, st):
                return _gdn_group(_mm_diff, q, k, v, z, b, al, dt, gn, st, head)

            _, vjp = jax.vjp(fn, grp(qc[sl, :]), grp(kc[sl, :]), grp(vc[sl, :]), grp(z_ref[sl, :]),
                             grp(ba_ref[sl, :]), al_ref[...], dt_ref[...], gain_ref[...], states[n])
            dys = [dy_ref[pl.ds(base + j * CHUNK, CHUNK), :].astype(F32) for j in range(GDN_GROUP)]
            dq, dkk, dvv, dz, db, dal, ddt, dgn, dst = vjp((dys, dstate[...]))
            dqc[sl, :] = dq.reshape(rows, dk)
            dkc[sl, :] = dkk.reshape(rows, dk)
            dvc[sl, :] = dvv.reshape(rows, dk)
            dz_ref[sl, :] = dz.reshape(rows, dk).astype(BF)
            dba_ref[sl, :] += db.reshape(rows, db.shape[-1])
            dal_ref[...] += dal
            ddt_ref[...] += ddt
            dgain_ref[...] += dgn
            dstate[...] = dst
            return carry

        lax.fori_loop(0, ngroups, bstep, 0)

        for x_ref, c_ref, dpost, dx_ref, dc_ref in ((q_ref, cq_ref, dqc, dq_ref, dcq_ref),
                                                    (k_ref, ck_ref, dkc, dk_ref, dck_ref),
                                                    (v_ref, cv_ref, dvc, dv_ref, dcv_ref)):
            _, vjp = jax.vjp(lambda x, *taps: _conv_silu(_shift_vjp, x, taps), x_ref[...], *_load_taps(c_ref))
            grads = vjp(dpost[...])
            dx_ref[...] = grads[0].astype(BF)
            for t in range(CONV_WIDTH):
                dc_ref[t:t + 1, :] = grads[1 + t]

    nh = GDN_HEADS
    col_bf = jax.ShapeDtypeStruct((s, nh * dk), BF)
    tap_out = jax.ShapeDtypeStruct((CONV_WIDTH, nh * dk), F32)
    tap_spec = pl.BlockSpec((CONV_WIDTH, dk), lambda h: (0, h))
    dq, dkk, dvv, dz, dba, dcq, dck, dcv, dal, ddt, dgain = pl.pallas_call(
        body, name="gdn_bwd", grid=(nh,),
        in_specs=[sp['q'], sp['k'], sp['v'], sp['z'], sp['ba'], sp['cq'], sp['ck'], sp['cv'], sp['vec'], sp['vec'],
                  sp['gain'], pl.BlockSpec((None, ngroups, dk, dk), lambda h: (h, 0, 0, 0)), sp['head']],
        out_specs=[sp['head']] * 4 + [sp['ba'], tap_spec, tap_spec, tap_spec, sp['vec'], sp['vec'], sp['gain']],
        out_shape=[col_bf] * 4 + [jax.ShapeDtypeStruct((s, 128), F32), tap_out, tap_out, tap_out,
                                  jax.ShapeDtypeStruct((1, 128), F32), jax.ShapeDtypeStruct((1, 128), F32),
                                  jax.ShapeDtypeStruct((1, dk), F32)],
        scratch_shapes=[pltpu.VMEM((s, dk), F32)] * 6 + [pltpu.VMEM((dk, dk), F32)],
        compiler_params=_cparams(),
    )(proj, proj, proj, proj, ba, conv_w, conv_w, conv_w, a_log, dt_bias, gain, group_states, dy)
    return (jnp.concatenate([dq, dkk, dvv, dz], axis=1), dba, jnp.concatenate([dcq, dck, dcv], axis=1), dal, ddt,
            dgain)


def _lru_specs(s, bw):
    nb = LRU_BLOCKS
    return dict(
        xb=pl.BlockSpec((s, bw), lambda n: (0, n)), yb=pl.BlockSpec((s, bw), lambda n: (0, nb + n)),
        taps=pl.BlockSpec((CONV_WIDTH, bw), lambda n: (0, n)), vec=pl.BlockSpec((1, bw), lambda n: (0, n)),
        w=pl.BlockSpec((None, bw, bw), lambda n: (n, 0, 0)), b=pl.BlockSpec((None, 1, bw), lambda n: (n, 0, 0)),
        col=pl.BlockSpec((s, bw), lambda n: (0, n)))


def _lru_fwd(proj, conv_w, conv_b, lam, wr, br, wi, bi):
    s = proj.shape[0]
    bw = proj.shape[1] // (2 * LRU_BLOCKS)
    sp = _lru_specs(s, bw)

    def body(xb_ref, yb_ref, cw_ref, cb_ref, lam_ref, wr_ref, br_ref, wi_ref, bi_ref, y_ref):
        a, u, gy = _lru_pre(_mm_plain, _shift_rows, xb_ref[...], yb_ref[...], _load_taps(cw_ref), cb_ref[...],
                            lam_ref[...], wr_ref[...], br_ref[...], wi_ref[...], bi_ref[...])
        y_ref[...] = (_linscan(a, u, False) * gy).astype(BF)

    return pl.pallas_call(
        body, name="lru_fwd", grid=(LRU_BLOCKS,),
        in_specs=[sp['xb'], sp['yb'], sp['taps'], sp['vec'], sp['vec'], sp['w'], sp['b'], sp['w'], sp['b']],
        out_specs=sp['col'], out_shape=jax.ShapeDtypeStruct((s, LRU_BLOCKS * bw), BF),
        compiler_params=_cparams(),
    )(proj, proj, conv_w, conv_b, lam, wr, br, wi, bi)


def _lru_bwd(proj, conv_w, conv_b, lam, wr, br, wi, bi, dy):
    s = proj.shape[0]
    nb = LRU_BLOCKS
    bw = proj.shape[1] // (2 * nb)
    sp = _lru_specs(s, bw)

    def body(xb_ref, yb_ref, cw_ref, cb_ref, lam_ref, wr_ref, br_ref, wi_ref, bi_ref, dy_ref,
             dxb_ref, dyb_ref, dcw_ref, dcb_ref, dlam_ref, dwr_ref, dbr_ref, dwi_ref, dbi_ref):
        def pre(xb, yb, t0, t1, t2, t3, cb, lm, w_r, b_r, w_i, b_i):
            return _lru_pre(_mm_diff, _shift_vjp, xb, yb, (t0, t1, t2, t3), cb, lm, w_r, b_r, w_i, b_i)

        (a, u, gy), vjp = jax.vjp(pre, xb_ref[...], yb_ref[...], *_load_taps(cw_ref), cb_ref[...], lam_ref[...],
                                  wr_ref[...], br_ref[...], wi_ref[...], bi_ref[...])
        h = _linscan(a, u, False)
        dout = dy_ref[...].astype(F32)
        g = _linscan(_shift_rows(a, 1, True), dout * gy, True)
        grads = vjp((g * _shift_rows(h, 1, False), g, dout * h))
        dxb_ref[...] = grads[0].astype(BF)
        dyb_ref[...] = grads[1].astype(BF)
        for t in range(CONV_WIDTH):
            dcw_ref[t:t + 1, :] = grads[2 + t]
        dcb_ref[...] = grads[6]
        dlam_ref[...] = grads[7]
        dwr_ref[...] = grads[8]
        dbr_ref[...] = grads[9]
        dwi_ref[...] = grads[10]
        dbi_ref[...] = grads[11]

    outs = pl.pallas_call(
        body, name="lru_bwd", grid=(nb,),
        in_specs=[sp['xb'], sp['yb'], sp['taps'], sp['vec'], sp['vec'], sp['w'], sp['b'], sp['w'], sp['b'], sp['col']],
        out_specs=[sp['col'], sp['col'], sp['taps'], sp['vec'], sp['vec'], sp['w'], sp['b'], sp['w'], sp['b']],
        out_shape=[jax.ShapeDtypeStruct((s, nb * bw), BF), jax.ShapeDtypeStruct((s, nb * bw), BF),
                   jax.ShapeDtypeStruct((CONV_WIDTH, nb * bw), F32), jax.ShapeDtypeStruct((1, nb * bw), F32),
                   jax.ShapeDtypeStruct((1, nb * bw), F32), jax.ShapeDtypeStruct((nb, bw, bw), F32),
                   jax.ShapeDtypeStruct((nb, 1, bw), F32), jax.ShapeDtypeStruct((nb, bw, bw), F32),
                   jax.ShapeDtypeStruct((nb, 1, bw), F32)],
        compiler_params=_cparams(),
    )(proj, proj, conv_w, conv_b, lam, wr, br, wi, bi, dy)
    return (jnp.concatenate([outs[0], outs[1]], axis=1),) + tuple(outs[2:])


ROW_STEP_BYTES = 12 * 1024 * 1024


def _row_tile(rows, cols, bytes_per_row_set):
    t = 8
    while t * 2 <= rows and rows % (t * 2) == 0 and (t * 2) * cols * bytes_per_row_set <= ROW_STEP_BYTES:
        t *= 2
    return t


def _sum_slabs(a, out_dtype, name):
    n, rows, cols = a.shape
    tr = _row_tile(rows, cols, 4 * (n + 1))

    def body(*refs):
        acc = refs[0][...].astype(F32)
        for r in refs[1:n]:
            acc = acc + r[...].astype(F32)
        refs[n][...] = acc.astype(out_dtype)

    specs = [pl.BlockSpec((None, tr, cols), functools.partial(lambda i, k: (k, i, 0), k=k)) for k in range(n)]
    return pl.pallas_call(
        body, name=name, grid=(rows // tr,), in_specs=specs, out_specs=pl.BlockSpec((tr, cols), lambda i: (i, 0)),
        out_shape=jax.ShapeDtypeStruct((rows, cols), out_dtype), compiler_params=_cparams(),
    )(*([a] * n))


def _sum_keep_and_landed(g, landed, core, name):
    four, hr, cols = landed.shape
    tr = _row_tile(hr, cols, 2 * 3)
    tph = hr // tr

    def body(core_ref, g_ref, l_ref, o_ref):
        o_ref[...] = (g_ref[...].astype(F32) + l_ref[...].astype(F32)).astype(o_ref.dtype)

    grid_spec = pltpu.PrefetchScalarGridSpec(
        num_scalar_prefetch=1, grid=(four, tph),
        in_specs=[pl.BlockSpec((None, tr, cols), lambda j, r, core_ref: (j, core_ref[0] * tph + r, 0)),
                  pl.BlockSpec((None, tr, cols), lambda j, r, core_ref: (j, r, 0))],
        out_specs=pl.BlockSpec((None, tr, cols), lambda j, r, core_ref: (j, r, 0)))
    return pl.pallas_call(body, name=name, grid_spec=grid_spec, out_shape=jax.ShapeDtypeStruct(landed.shape, g.dtype),
                          compiler_params=_cparams())(core, g, landed)


def _sum_chip_parts(parts, landed, chip, name):
    _, rows, cols = parts.shape
    tr = _row_tile(rows, cols, 2 * 4 + 4)

    def body(chip_ref, p_ref, l0_ref, l1_ref, l2_ref, o_ref):
        acc = l0_ref[...].astype(F32) + l1_ref[...].astype(F32)
        o_ref[...] = (acc + l2_ref[...].astype(F32)) + p_ref[...].astype(F32)

    slab = lambda k: pl.BlockSpec((None, tr, cols), lambda i, chip_ref: (k, i, 0))
    grid_spec = pltpu.PrefetchScalarGridSpec(
        num_scalar_prefetch=1, grid=(rows // tr,),
        in_specs=[pl.BlockSpec((None, tr, cols), lambda i, chip_ref: (chip_ref[0], i, 0)), slab(0), slab(1), slab(2)],
        out_specs=pl.BlockSpec((tr, cols), lambda i, chip_ref: (i, 0)))
    return pl.pallas_call(body, name=name, grid_spec=grid_spec, out_shape=jax.ShapeDtypeStruct((rows, cols), F32),
                          compiler_params=_cparams())(chip, parts, landed, landed, landed)


_ADAM_C1 = 1.0 / (1.0 - ADAM_B1 ** ADAM_STEP)
_ADAM_C2 = 1.0 / (1.0 - ADAM_B2 ** ADAM_STEP)


def _adamw_math(w, g, m, v):
    nm = ADAM_B1 * m + (1.0 - ADAM_B1) * g
    nv = ADAM_B2 * v + (1.0 - ADAM_B2) * (g * g)
    den = jnp.sqrt(nv * _ADAM_C2) + ADAM_EPS
    inv = pl.reciprocal(den, approx=True)
    inv = inv * (2.0 - den * inv)
    delta = -ADAM_LR * ((nm * _ADAM_C1) * inv + ADAM_WD * w)
    return delta, nm, nv


def _adamw(w, g, m, v, name):
    rows, cols = w.shape
    tr = _row_tile(rows, cols, 4 * 7)

    def body(w_ref, g_ref, m_ref, v_ref, d_ref, nm_ref, nv_ref):
        d_ref[...], nm_ref[...], nv_ref[...] = _adamw_math(w_ref[...], g_ref[...], m_ref[...], v_ref[...])

    spec = pl.BlockSpec((tr, cols), lambda i: (i, 0))
    shape = jax.ShapeDtypeStruct((rows, cols), F32)
    return pl.pallas_call(
        body, name=name, grid=(rows // tr,), in_specs=[spec] * 4, out_specs=[spec] * 3, out_shape=[shape] * 3,
        compiler_params=_cparams(),
    )(w, g, m, v)


def _adamw_halves(w, m, v, layer, mine, theirs, core, name, carried=None):
    n_layers, rows, cols = w.shape
    hr = mine.shape[0]
    tr = _row_tile(hr, cols, 4 * 9)
    tph = hr // tr

    def body(core_ref, w_ref, q_ref, t_ref, m_ref, v_ref, *rest):
        g_ref, d_ref, nm_ref, nv_ref = rest[-4:]
        is_mine = (pl.program_id(0) // tph) == core_ref[0]
        g = jnp.where(is_mine, q_ref[...], t_ref[...])
        g_ref[...] = g
        d_ref[...], nm_ref[...], nv_ref[...] = _adamw_math(w_ref[...], g, m_ref[...], v_ref[...])

    slab = pl.BlockSpec((None, tr, cols), lambda i, core_ref: (layer, i, 0))

    def mine_index(i, core_ref):
        return jnp.where(i // tph == core_ref[0], i % tph, jnp.where(core_ref[0] == 0, tph - 1, 0)), 0

    def theirs_index(i, core_ref):
        return jnp.where(i // tph == core_ref[0], jnp.where(core_ref[0] == 0, 0, tph - 1), i % tph), 0

    in_specs = [slab, pl.BlockSpec((tr, cols), mine_index), pl.BlockSpec((tr, cols), theirs_index), slab, slab]
    operands = [core, w, mine, theirs, m, v]
    aliases = {}
    if carried is not None:
        in_specs += [_ANY] * 4
        aliases = {len(operands) + k: k for k in range(4)}
        operands += list(carried)
    grid_spec = pltpu.PrefetchScalarGridSpec(num_scalar_prefetch=1, grid=(rows // tr,), in_specs=in_specs,
                                             out_specs=[slab] * 4)
    return pl.pallas_call(
        body, name=name, grid_spec=grid_spec, out_shape=[jax.ShapeDtypeStruct(w.shape, F32)] * 4,
        input_output_aliases=aliases, compiler_params=_cparams(),
    )(*operands)


_ANY = pl.BlockSpec(memory_space=pl.ANY)


def _mesh_pos():
    return lax.axis_index("x"), lax.axis_index("y"), lax.axis_index("c")


def _remote(src, dst, send_sems, recv_sems, k, dev):
    return pltpu.make_async_remote_copy(src_ref=src, dst_ref=dst, send_sem=send_sems.at[k], recv_sem=recv_sems.at[k],
                                        device_id=dev, device_id_type=MESH)


STREAM_CHUNK_BYTES = 2 * 1024 * 1024
STREAM_SLOTS = 4


def _chunk_rows(rows, cols, itemsize):
    assert rows % 16 == 0, rows
    t = 16
    while t * 2 <= rows and rows % (t * 2) == 0 and (t * 2) * cols * itemsize <= STREAM_CHUNK_BYTES:
        t *= 2
    return t


def _stream_chunks(n_chunks, src_at, dst_at, buf, load_sems, send_sems, recv_sem, sibling):
    def load(k, slot):
        return pltpu.make_async_copy(src_at(k), buf.at[slot], load_sems.at[slot])

    def send(k, slot):
        return pltpu.make_async_remote_copy(src_ref=buf.at[slot], dst_ref=dst_at(k), send_sem=send_sems.at[slot],
                                            recv_sem=recv_sem, device_id=sibling, device_id_type=MESH)

    def step(k, carry):
        slot = k % STREAM_SLOTS

        @pl.when(k >= STREAM_SLOTS)
        def _():
            send(k - STREAM_SLOTS, slot).wait_send()

        load(k, slot).start()

        @pl.when(k >= 1)
        def _():
            prev = (k - 1) % STREAM_SLOTS
            load(k - 1, prev).wait()
            send(k - 1, prev).start()

        return carry

    lax.fori_loop(0, n_chunks, step, 0)
    last = (n_chunks - 1) % STREAM_SLOTS
    load(n_chunks - 1, last).wait()
    send(n_chunks - 1, last).start()
    for k in range(max(0, n_chunks - STREAM_SLOTS), n_chunks):
        send(k, k % STREAM_SLOTS).wait_send()


_HBM = pl.BlockSpec(memory_space=pltpu.HBM)
_SEM = pl.BlockSpec(memory_space=pltpu.SEMAPHORE)
_DATAFLOW = pltpu.SideEffectType.DATAFLOW_SIDE_EFFECTING


def _chip_copies(kind, ins, lands, split, send_sems, recv_sems):
    x, y, c = _mesh_pos()
    me = 2 * x + y
    chips = [(1 - x, y), (x, 1 - y), (1 - x, 1 - y)]
    pairs = []
    for i in range(len(ins)):
        for j, chip in enumerate(chips):
            pj = 2 * chip[0] + chip[1]
            if kind == 'scatter':
                src, dst, got = ins[i].at[pj], lands[i].at[j], lands[i].at[j]
            elif split[i]:
                hr = ins[i].shape[0] // 2
                rows = pl.ds(c * hr, hr)
                src, dst, got = ins[i].at[rows], lands[i].at[me, rows], lands[i].at[pj, rows]
            else:
                src, dst, got = ins[i], lands[i].at[me], lands[i].at[pj]
            k = 3 * i + j
            pairs.append((_remote(src, dst, send_sems, recv_sems, k, (*chip, c)),
                          _remote(got, got, send_sems, recv_sems, k, (*chip, c))))
    return pairs


def _chip_exchange_start(kind, srcs, land_shapes, split, name, after):
    n = len(srcs)

    def body(*refs):
        ins, lands = refs[:n], refs[n:2 * n]
        send_sems, recv_sems = refs[2 * n + 1], refs[2 * n + 2]
        token = refs[-1]
        for send, _ in _chip_copies(kind, ins, lands, split, send_sems, recv_sems):
            send.start()
        token[...] = jnp.zeros_like(token)

    hbm = lambda t: pltpu.with_memory_space_constraint(t, pltpu.HBM)
    operands = [hbm(s) for s in srcs] + [hbm(lax.empty(shp, s.dtype)) for shp, s in zip(land_shapes, srcs)] + [after]
    out = pl.pallas_call(
        body, name=name, in_specs=[_HBM] * (2 * n) + [_ANY],
        out_specs=[_SEM, _SEM] + [_HBM] * (2 * n) + [pl.BlockSpec(memory_space=pltpu.VMEM)],
        out_shape=[pltpu.SemaphoreType.DMA((3 * n,)), pltpu.SemaphoreType.DMA((3 * n,))]
        + [pltpu.HBM(s.shape, s.dtype) for s in srcs] + [pltpu.HBM(shp, s.dtype) for shp, s in zip(land_shapes, srcs)]
        + [jax.ShapeDtypeStruct((8, 128), F32)],
        input_output_aliases={i: 2 + i for i in range(2 * n)},
        compiler_params=pltpu.CompilerParams(has_side_effects=_DATAFLOW),
    )(*operands)
    return out[0], out[1], out[2:2 + n], out[2 + n:2 + 2 * n], out[-1]


def _chip_exchange_wait(kind, started, split, name, after):
    send_sems, recv_sems, srcs, lands, _ = started
    n = len(srcs)
    after = list(after) if isinstance(after, (list, tuple)) else [after]

    def body(*refs):
        ins, land_refs = refs[:n], refs[n:2 * n]
        for send, arrived in _chip_copies(kind, ins, land_refs, split, refs[2 * n], refs[2 * n + 1]):
            send.wait_send()
            arrived.wait_recv()

    out = pl.pallas_call(
        body, name=name, in_specs=[_HBM] * (2 * n) + [_SEM, _SEM] + [_ANY] * len(after), out_specs=[_HBM] * (2 * n),
        out_shape=[pltpu.HBM(s.shape, s.dtype) for s in srcs] + [pltpu.HBM(l.shape, l.dtype) for l in lands],
        input_output_aliases={i: i for i in range(2 * n)},
        compiler_params=pltpu.CompilerParams(has_side_effects=_DATAFLOW),
    )(*srcs, *lands, send_sems, recv_sems, *after)
    return out[:n], out[n:]


def _pass_to_sibling(gathered, name, token=None):
    n = len(gathered)
    tr = [_chunk_rows(g.shape[1] // 2, g.shape[2], g.dtype.itemsize) for g in gathered]
    extra = [] if token is None else [token]

    def body(*refs):
        refs = refs[:n] + refs[n + len(extra):]
        outs = refs[n:2 * n]
        recv_sems, load_sems, send_sems = refs[2 * n:2 * n + 3]
        bufs = refs[2 * n + 3:]
        x, y, c = _mesh_pos()
        sibling = (x, y, 1 - c)
        chips = [(1 - x, y), (x, 1 - y), (1 - x, 1 - y)]
        for i in range(n):
            hr = outs[i].shape[1] // 2
            for j, chip in enumerate(chips):
                def rows_at(k, i=i, pj=2 * chip[0] + chip[1], hr=hr):
                    return outs[i].at[pj, pl.ds(c * hr + k * tr[i], tr[i])]

                _stream_chunks(hr // tr[i], rows_at, rows_at, bufs[i], load_sems, send_sems, recv_sems.at[3 * i + j],
                               sibling)
        for i in range(n):
            hr = outs[i].shape[1] // 2
            for j, chip in enumerate(chips):
                blk = outs[i].at[2 * chip[0] + chip[1], pl.ds((1 - c) * hr, hr)]
                pltpu.make_async_remote_copy(src_ref=blk, dst_ref=blk, send_sem=send_sems.at[0],
                                             recv_sem=recv_sems.at[3 * i + j], device_id=sibling,
                                             device_id_type=MESH).wait_recv()

    return pl.pallas_call(
        body, name=name, in_specs=[_ANY] * (n + len(extra)), out_specs=[_ANY] * n,
        out_shape=[jax.ShapeDtypeStruct(g.shape, g.dtype) for g in gathered],
        input_output_aliases={i: i for i in range(n)},
        scratch_shapes=[pltpu.SemaphoreType.DMA((3 * n,)), pltpu.SemaphoreType.DMA((STREAM_SLOTS,)),
                        pltpu.SemaphoreType.DMA((STREAM_SLOTS,))]
        + [pltpu.VMEM((STREAM_SLOTS, tr[i], g.shape[2]), g.dtype) for i, g in enumerate(gathered)],
        compiler_params=_cparams(),
    )(*gathered, *extra)


def _stream_to_sibling(srcs, halved, name):
    n = len(srcs)
    geo = []
    for s in srcs:
        rows = s.shape[1] // 2 if halved else s.shape[1]
        geo.append((s.shape[0], rows, s.shape[2], _chunk_rows(rows, s.shape[2], s.dtype.itemsize)))

    def body(*refs):
        ins, outs = refs[:n], refs[n:2 * n]
        recv_sems, load_sems, send_sems = refs[2 * n:2 * n + 3]
        bufs = refs[2 * n + 3:]
        x, y, c = _mesh_pos()
        sibling = (x, y, 1 - c)
        for i in range(n):
            slabs, rows, _, tr = geo[i]
            per_slab = rows // tr
            off = (1 - c) * rows if halved else 0

            def src_at(k, i=i, per_slab=per_slab, tr=tr, off=off):
                return ins[i].at[k // per_slab, pl.ds(off + (k % per_slab) * tr, tr)]

            def dst_at(k, i=i, per_slab=per_slab, tr=tr):
                return outs[i].at[k // per_slab, pl.ds((k % per_slab) * tr, tr)]

            _stream_chunks(slabs * per_slab, src_at, dst_at, bufs[i], load_sems, send_sems, recv_sems.at[i], sibling)
        for i in range(n):
            pltpu.make_async_remote_copy(src_ref=outs[i], dst_ref=outs[i], send_sem=send_sems.at[0],
                                         recv_sem=recv_sems.at[i], device_id=sibling, device_id_type=MESH).wait_recv()

    return pl.pallas_call(
        body, name=name, in_specs=[_ANY] * n, out_specs=[_ANY] * n,
        out_shape=[jax.ShapeDtypeStruct((g[0], g[1], g[2]), s.dtype) for g, s in zip(geo, srcs)],
        scratch_shapes=[pltpu.SemaphoreType.DMA((n,)), pltpu.SemaphoreType.DMA((STREAM_SLOTS,)),
                        pltpu.SemaphoreType.DMA((STREAM_SLOTS,))]
        + [pltpu.VMEM((STREAM_SLOTS, g[3], g[2]), s.dtype) for g, s in zip(geo, srcs)],
        compiler_params=_cparams(),
    )(*srcs)


def _reduce_scatter_begin(grads, tag, core):
    landed = _stream_to_sibling(grads, True, f"rs_swap_{tag}")
    parts = [_sum_keep_and_landed(g, l, core, f"rs_add2_{tag}_{i}") for i, (g, l) in enumerate(zip(grads, landed))]
    return _chip_exchange_start('scatter', parts, [(3,) + p.shape[1:] for p in parts], None, f"rs_scatter_start_{tag}",
                                core)


def _reduce_scatter_finish(started, tag, chip, after):
    parts, landed = _chip_exchange_wait('scatter', started, None, f"rs_scatter_wait_{tag}", after)
    mine = [_sum_chip_parts(p, l, chip, f"rs_add4_{tag}_{i}") for i, (p, l) in enumerate(zip(parts, landed))]
    theirs = _stream_to_sibling([m[None] for m in mine], False, f"rs_join_{tag}")
    return [(m, t[0]) for m, t in zip(mine, theirs)]


def _pad_cols(a, width=128):
    return jnp.pad(a, ((0, 0), (0, width - a.shape[1])))


def _mixer_forward(kind, hn, w, tables):
    if kind == 0:
        proj = _matmul(hn, w['ret_w_in'], name="ret_proj", b_sharded=True)
        og, states = _ret_fwd(proj, w['ret_gn_gain'], tables)
        return og, (proj, states)
    if kind == 1:
        proj = _matmul(hn, w['gdn_w_main'], name="gdn_proj")
        ba = _matmul(hn, w['gdn_w_small'], name="gdn_proj_ba")
        og, states = _gdn_fwd(proj, ba, w['gdn_conv_w'], w['gdn_a_log'], w['gdn_dt_bias'], w['gdn_norm_gain'])
        return og, (proj, ba, states)
    if kind == 2:
        proj = _matmul(hn, w['gla_w_main'], name="gla_proj")
        glow = _matmul(hn, w['gla_w_small'], name="gla_proj_gate")
        og, states = _gla_fwd(proj, glow, w['gla_w_gate_up'], w['gla_gate_bias'], w['gla_norm_gain'])
        return og, (proj, glow, states)
    proj = _matmul(hn, w['lru_w_in'], name="lru_proj", b_sharded=True)
    og = _lru_fwd(proj, w['lru_conv_w'], w['lru_conv_b'], w['lru_lambda'], w['lru_w_rgate'], w['lru_b_rgate'],
                  w['lru_w_igate'], w['lru_b_igate'])
    return og, (proj,)


def _mixer_backward(kind, hn, w, tables, saved, d_og, grads):
    d = hn.shape[1]
    if kind == 0:
        proj, states = saved
        d_proj, grads['ret_gn_gain'] = _ret_bwd(proj, w['ret_gn_gain'], tables, states, d_og)
        grads['ret_w_in'] = _matmul(hn, d_proj, name="ret_dw_in", ta=True, out_dtype=BF, o_sharded=True)
        return _matmul(d_proj, w['ret_w_in'], name="ret_dhn", tb=True, b_sharded=True)
    if kind == 1:
        proj, ba, states = saved
        d_proj, d_ba, grads['gdn_conv_w'], grads['gdn_a_log'], grads['gdn_dt_bias'], grads['gdn_norm_gain'] = _gdn_bwd(
            proj, ba, w['gdn_conv_w'], w['gdn_a_log'], w['gdn_dt_bias'], w['gdn_norm_gain'], states, d_og)
        d_ba = d_ba.astype(BF)
        dw_main = _matmul(hn, d_proj, name="gdn_dw_main", ta=True, out_dtype=BF)
        dw_small = _matmul(hn, d_ba, name="gdn_dw_small", ta=True, out_dtype=BF)
        dw = jnp.concatenate([dw_main, dw_small[:, :2 * GDN_HEADS]], axis=1)
        grads['gdn_w_in'] = dw.reshape(d, N_CHIPS, dw.shape[1] // N_CHIPS).transpose(1, 0, 2)
        d_hn = _matmul(d_proj, w['gdn_w_main'], name="gdn_dhn_main", tb=True)
        return _matmul(d_ba, w['gdn_w_small'], name="gdn_dhn_small", tb=True, epilogue='add', extra=d_hn)
    if kind == 2:
        proj, glow, states = saved
        d_proj, d_glow4, d_wgu, grads['gla_gate_bias'], grads['gla_norm_gain'] = _gla_bwd(
            proj, glow, w['gla_w_gate_up'], w['gla_gate_bias'], w['gla_norm_gain'], states, d_og)
        grads['gla_w_gate_up'] = d_wgu[:GLA_RANK]
        dw_main = _matmul(hn, d_proj, name="gla_dw_main", ta=True, out_dtype=BF)
        dw_small4 = _matmul(hn, d_glow4, name="gla_dw_small", ta=True, out_dtype=F32)
        dw_small = dw_small4.reshape(d, GLA_HEADS, 128)[:, :, :GLA_RANK].sum(axis=1).astype(BF)
        dw = jnp.concatenate([dw_main, dw_small], axis=1)
        grads['gla_w_in'] = dw.reshape(d, N_CHIPS, dw.shape[1] // N_CHIPS).transpose(1, 0, 2)
        d_hn = _matmul(d_proj, w['gla_w_main'], name="gla_dhn_main", tb=True)
        w_small4 = jnp.tile(w['gla_w_small'], (1, GLA_HEADS))
        return _matmul(d_glow4, w_small4, name="gla_dhn_small", tb=True, epilogue='add', extra=d_hn)
    (proj,) = saved
    (d_proj, grads['lru_conv_w'], grads['lru_conv_b'], grads['lru_lambda'], grads['lru_w_rgate'], grads['lru_b_rgate'],
     grads['lru_w_igate'], grads['lru_b_igate']) = _lru_bwd(
        proj, w['lru_conv_w'], w['lru_conv_b'], w['lru_lambda'], w['lru_w_rgate'], w['lru_b_rgate'], w['lru_w_igate'],
        w['lru_b_igate'], d_og)
    grads['lru_w_in'] = _matmul(hn, d_proj, name="lru_dw_in", ta=True, out_dtype=BF, o_sharded=True)
    return _matmul(d_proj, w['lru_w_in'], name="lru_dhn", tb=True, b_sharded=True)


_W_OUT = ('ret_w_out', 'gdn_w_out', 'gla_w_out', 'lru_w_out')
_W_IN = ('ret_w_in', 'gdn_w_in', 'gla_w_in', 'lru_w_in')


def _layer_forward(layer, x, w, tables, token=None):
    hn = _rmsnorm_fwd(x, w['norm1'][layer], f"norm1_fwd_{layer}", token)
    og, mixer_saved = _mixer_forward(layer, hn, w, tables)
    x1 = _matmul(og, w[_W_OUT[layer]], name=f"mixer_out_{layer}", epilogue='add', extra=x)
    hn2 = _rmsnorm_fwd(x1, w['norm2'][layer], f"norm2_fwd_{layer}")
    act = _matmul(hn2, w['mlp_w_up'][layer], name="mlp_up", b_sharded=True, epilogue='relu2', out_dtype=BF)
    x2 = _matmul(act, w['mlp_w_down'][layer], name="mlp_down", epilogue='add', extra=x1)
    return x2, (x, hn, mixer_saved, og, x1, hn2, act)


def _layer_backward(layer, dx2, w, tables, saved, token=None):
    x, hn, mixer_saved, og, x1, hn2, act = saved
    d = x.shape[1]
    grads = {}
    d_up = _matmul(dx2, w['mlp_w_down'][layer], name="mlp_d_up", tb=True, epilogue='dact', extra=act, out_dtype=BF,
                   token=token)
    dw_down = _matmul(act, dx2, name="mlp_dw_down", ta=True, out_dtype=BF)
    grads['mlp_w_down'] = dw_down.reshape(N_CHIPS, dw_down.shape[0] // N_CHIPS, d)
    grads['mlp_w_up'] = _matmul(hn2, d_up, name="mlp_dw_up", ta=True, out_dtype=BF, o_sharded=True)
    d_hn2 = _matmul(d_up, w['mlp_w_up'][layer], name="mlp_d_hn", tb=True, b_sharded=True)
    dx1, grads['norm2'] = _rmsnorm_bwd(x1, w['norm2'][layer], d_hn2, dx2, f"norm2_bwd_{layer}")
    w_out = w[_W_OUT[layer]]
    d_og = _matmul(dx1, w_out, name=f"mixer_d_og_{layer}", tb=True, out_dtype=BF)
    dw_out = _matmul(og, dx1, name=f"mixer_dw_out_{layer}", ta=True, out_dtype=BF)
    grads[_W_OUT[layer]] = dw_out.reshape(N_CHIPS, dw_out.shape[0] // N_CHIPS, d)
    d_hn = _mixer_backward(layer, hn, w, tables, mixer_saved, d_og, grads)
    dx, grads['norm1'] = _rmsnorm_bwd(x, w['norm1'][layer], d_hn, dx1, f"norm1_bwd_{layer}")
    return dx, grads


PACK_ROWS = 256


def _pack(arrays):
    flat = []
    for a in arrays:
        v = a.astype(F32).reshape(-1)
        v = jnp.pad(v, (0, (-v.shape[0]) % 128))
        flat.append(v.reshape(-1, 128))
    buf = jnp.concatenate(flat, axis=0)
    return jnp.pad(buf, ((0, (-buf.shape[0]) % PACK_ROWS), (0, 0)))


def _unpack(buf, shapes):
    lead = buf.shape[:-2]
    out, off = [], 0
    for shp in shapes:
        n = math.prod(shp)
        rows = -(-n // 128)
        piece = buf[..., off:off + rows, :].reshape(lead + (rows * 128,))[..., :n]
        out.append(piece.reshape(lead + tuple(shp)))
        off += rows
    return out


_WEIGHTS = ('norm1', 'norm2', 'final_norm', 'ret_w_in', 'ret_gn_gain', 'ret_w_out', 'gdn_w_in', 'gdn_conv_w',
            'gdn_a_log', 'gdn_dt_bias', 'gdn_norm_gain', 'gdn_w_out', 'gla_w_in', 'gla_w_gate_up', 'gla_gate_bias',
            'gla_norm_gain', 'gla_w_out', 'lru_w_in', 'lru_conv_w', 'lru_conv_b', 'lru_w_rgate', 'lru_b_rgate',
            'lru_w_igate', 'lru_b_igate', 'lru_lambda', 'lru_w_out', 'mlp_w_up', 'mlp_w_down')
_FWD_PARAMS = ('x',) + _WEIGHTS
_BIG = ('ret_w_in', 'ret_w_out', 'gdn_w_in', 'gdn_w_out', 'gla_w_in', 'gla_w_out', 'lru_w_in', 'lru_w_out',
        'mlp_w_up', 'mlp_w_down')
_SMALL = tuple(n for n in _WEIGHTS if n not in _BIG)
_SMALL_SHARDED = ('ret_gn_gain', 'gdn_conv_w', 'gla_w_gate_up', 'gla_gate_bias', 'gla_norm_gain', 'lru_conv_w',
                  'lru_conv_b', 'lru_lambda')


def kernel(*args):
    names = _FWD_PARAMS + ('loss_target',) + tuple('m_' + n for n in _WEIGHTS) + tuple('v_' + n for n in _WEIGHTS)
    assert len(args) == len(names)
    a = dict(zip(names, args))
    x = a['x'][0]
    target = a['loss_target'][0]
    s, d = x.shape
    chip = 2 * lax.axis_index("x") + lax.axis_index("y")

    small_local = [a[n][0] if a[n].ndim == 3 else a[n] for n in _SMALL_SHARDED]
    small_pack = _pack(small_local)
    core_arr = lax.axis_index("c").astype(jnp.int32).reshape(1)
    chip_arr = chip.astype(jnp.int32).reshape(1)

    def whole_cols(g):
        return g.transpose(1, 0, 2).reshape(g.shape[1], N_CHIPS * g.shape[2])

    def whole_rows(g):
        return g.reshape(N_CHIPS * g.shape[1], g.shape[2])

    w = {'mlp_w_up': [None] * 4, 'mlp_w_down': [None] * 4}
    for n in _SMALL:
        if n not in _SMALL_SHARDED:
            w[n] = a[n][0] if n.startswith('lru_') else a[n]
    w['gdn_a_log'], w['gdn_dt_bias'] = _pad_cols(w['gdn_a_log']), _pad_cols(w['gdn_dt_bias'])
    w['lru_b_rgate'] = w['lru_b_rgate'].reshape(LRU_BLOCKS, 1, -1)
    w['lru_b_igate'] = w['lru_b_igate'].reshape(LRU_BLOCKS, 1, -1)

    def gather_start(layer, after):
        ops = [a[_W_IN[layer]][0].astype(BF), a[_W_OUT[layer]][0].astype(BF), a['mlp_w_up'][layer].astype(BF),
               a['mlp_w_down'][layer].astype(BF)] + ([small_pack] if layer == 0 else [])
        split = [True] * 4 + ([False] if layer == 0 else [])
        return _chip_exchange_start('gather', ops, [(N_CHIPS,) + o.shape for o in ops], split,
                                    f"gather_start_{layer}", after), split

    def gather_finish(layer, started, after):
        started, split = started
        own, lands = _chip_exchange_wait('gather', started, split, f"gather_wait_{layer}", after)
        nxt = gather_start(layer + 1, own[0]) if layer < 3 else None
        lands = list(_pass_to_sibling(lands[:4], f"gather_pass_{layer}", None if nxt is None else nxt[0][4])) \
            + list(lands[4:])
        g_in, g_out, g_up, g_down = [lax.dynamic_update_slice_in_dim(l, o[None], chip, axis=0)
                                     for l, o in zip(lands[:4], own[:4])]
        w['mlp_w_up'][layer], w['mlp_w_down'][layer], w[_W_OUT[layer]] = g_up, whole_rows(g_down), whole_rows(g_out)
        if layer == 0:
            small_all = lax.dynamic_update_slice_in_dim(lands[4], own[4][None], chip, axis=0)
            for n, piece in zip(_SMALL_SHARDED, _unpack(small_all, [p.shape for p in small_local])):
                w[n] = whole_cols(piece)
            w['gla_w_gate_up'] = jnp.pad(w['gla_w_gate_up'], ((0, 128 - GLA_RANK), (0, 0)))
        if layer in (0, 3):
            w[_W_IN[layer]] = g_in
        else:
            name, tail = ('gdn', 2 * GDN_HEADS) if layer == 1 else ('gla', GLA_RANK)
            full = whole_cols(g_in)
            w[name + '_w_main'] = full[:, :full.shape[1] - tail]
            w[name + '_w_small'] = _pad_cols(full[:, full.shape[1] - tail:])
        return nxt

    tables = _ret_tables(s, d // RET_HEADS)
    saved = []
    h = x
    nxt = gather_finish(0, gather_start(0, x), x)
    for layer in range(4):
        h, sv = _layer_forward(layer, h, w, tables, None if nxt is None else nxt[0][4])
        saved.append(sv)
        if nxt is not None:
            nxt = gather_finish(layer + 1, nxt, h)
    loss_part, dh, g_final = _loss_head(h, w['final_norm'], target)
    loss = lax.psum(loss_part, ("x", "y", "c"))

    grad, delta, new_m, new_v = {}, {}, {}, {}
    small_grads = {'final_norm': g_final}
    norm_grads = {'norm1': [None] * 4, 'norm2': [None] * 4}
    mlp_upd = {'mlp_w_up': None, 'mlp_w_down': None}

    def update_layer(layer, started, after):
        red = _reduce_scatter_finish(started, str(layer), chip_arr, after)
        for n, (mine, theirs) in ((_W_IN[layer], red[0]), (_W_OUT[layer], red[1])):
            grad[n], delta[n], new_m[n], new_v[n] = _adamw_halves(a[n], a['m_' + n], a['v_' + n], 0, mine, theirs,
                                                                  core_arr, f"adamw_{n}")
        for n, (mine, theirs) in (('mlp_w_up', red[2]), ('mlp_w_down', red[3])):
            mlp_upd[n] = _adamw_halves(a[n], a['m_' + n], a['v_' + n], layer, mine, theirs, core_arr,
                                       f"adamw_{n}_{layer}", carried=mlp_upd[n])

    scatters = {}
    token = None
    for layer in reversed(range(4)):
        dh, g = _layer_backward(layer, dh, w, tables, saved[layer], token)
        saved[layer] = None
        scatters[layer] = _reduce_scatter_begin([g[_W_IN[layer]], g[_W_OUT[layer]], g['mlp_w_up'], g['mlp_w_down']],
                                                str(layer), core_arr)
        token = scatters[layer][4]
        norm_grads['norm1'][layer], norm_grads['norm2'][layer] = g['norm1'], g['norm2']
        for n in _SMALL:
            if n in g:
                small_grads[n] = g[n]
    small_grads['norm1'] = jnp.stack(norm_grads['norm1'])
    small_grads['norm2'] = jnp.stack(norm_grads['norm2'])

    full_shapes = [small_grads[n].shape for n in _SMALL]
    small_part = _pack([small_grads[n] for n in _SMALL])
    sibling_part = _stream_to_sibling([small_part[None]], False, "small_swap")[0][0]
    chip_part = _sum_slabs(jnp.stack([small_part, sibling_part]), F32, "sum_small_cores")
    small_started = _chip_exchange_start('gather', [chip_part], [(N_CHIPS,) + chip_part.shape], [False],
                                         "small_gather_start", core_arr)
    after = [token]
    for layer in (3, 2, 1):
        update_layer(layer, scatters[layer], after)
        after = [new_v[_W_IN[layer]], new_v[_W_OUT[layer]], mlp_upd['mlp_w_up'][3], mlp_upd['mlp_w_down'][3]]
    own, lands = _chip_exchange_wait('gather', small_started, [False], "small_gather_wait", after)
    total = _sum_slabs(lax.dynamic_update_slice_in_dim(lands[0], own[0][None], chip, axis=0), F32, "sum_small_chips")
    local_g = {}
    for n, full in zip(_SMALL, _unpack(total, full_shapes)):
        shp = a[n].shape
        if n in _SMALL_SHARDED:
            full = full.reshape(full.shape[0], -1)
            cq = shp[-1]
            full = lax.dynamic_slice_in_dim(full, chip * cq, cq, axis=1)
        elif n in ('gdn_a_log', 'gdn_dt_bias'):
            full = full[:, :shp[-1]]
        local_g[n] = full.reshape(shp)
    shapes = [a[n].shape for n in _SMALL]
    packed = [_pack([src[n] for n in _SMALL]) for src in
              (a, local_g, {n: a['m_' + n] for n in _SMALL}, {n: a['v_' + n] for n in _SMALL})]
    upd = _adamw(*packed, "adamw_small")
    for n, gr, dl, nm, nv in zip(_SMALL, _unpack(packed[1], shapes), *[_unpack(u, shapes) for u in upd]):
        grad[n], delta[n], new_m[n], new_v[n] = gr, dl, nm, nv
    update_layer(0, scatters[0], upd[0])
    for n in ('mlp_w_up', 'mlp_w_down'):
        grad[n], delta[n], new_m[n], new_v[n] = mlp_upd[n]

    out = [loss, dh.reshape(a['x'].shape)]
    for group in (grad, delta, new_m, new_v):
        out += [group[n].reshape(a[n].shape) for n in _WEIGHTS]
    return tuple(out)
```

```python
import functools
import math

import jax
import jax.numpy as jnp
from jax import lax
from jax.experimental import pallas as pl
from jax.experimental.pallas import tpu as pltpu

F32 = jnp.float32
BF = jnp.bfloat16
MESH = pl.DeviceIdType.MESH

NORM_EPS = 1e-6
CHUNK = 64
RET_HEADS = 8
GDN_HEADS = 16
GLA_HEADS = 4
GLA_RANK = 16
GLA_TAU = 16.0
LRU_BLOCKS = 16
LRU_C = 8.0
CONV_WIDTH = 4
ROPE_BASE = 10000.0
N_CHIPS = 4
N_DEV = 8

ADAM_LR = 0.001
ADAM_B1 = 0.9
ADAM_B2 = 0.999
ADAM_EPS = 1e-08
ADAM_WD = 0.01
ADAM_STEP = 10

VMEM_LIMIT_BYTES = 56 * 1024 * 1024
TOKEN_BLOCK = 512
ROW_BLOCK = 256
GDN_GROUP = 16


def _cparams(**kw):
    return pltpu.CompilerParams(vmem_limit_bytes=VMEM_LIMIT_BYTES, **kw)


def _bf16_parts(x, n):
    parts = []
    for _ in range(n - 1):
        p = x.astype(BF)
        parts.append(p)
        x = x - p.astype(F32)
    return parts + [x.astype(BF)]


def _raw_mm(a, b, ta, tb, hi):
    nb = a.ndim - 2
    batch = tuple(range(nb))
    dims = (((nb + (0 if ta else 1),), (nb + (1 if tb else 0),)), (batch, batch))

    def dot(p, q):
        return lax.dot_general(p, q, dims, preferred_element_type=F32)

    if not hi:
        return dot(a.astype(BF), b.astype(BF))
    if hi == 'l':
        ae = a.astype(BF)
        b0, b1, b2 = _bf16_parts(b.astype(F32), 3)
        return dot(ae, b0) + (dot(ae, b1) + dot(ae, b2))
    if hi == 'r':
        be = b.astype(BF)
        a0, a1, a2 = _bf16_parts(a.astype(F32), 3)
        return dot(a0, be) + (dot(a1, be) + dot(a2, be))
    a0, a1 = _bf16_parts(a.astype(F32), 2)
    b0, b1 = _bf16_parts(b.astype(F32), 2)
    return dot(a0, b0) + (dot(a0, b1) + dot(a1, b0))


@functools.partial(jax.custom_vjp, nondiff_argnums=(2, 3, 4))
def _mm_vjp(a, b, ta, tb, hi):
    return _raw_mm(a, b, ta, tb, hi)


def _mm_vjp_fwd(a, b, ta, tb, hi):
    return _raw_mm(a, b, ta, tb, hi), (a, b)


def _mm_vjp_bwd(ta, tb, hi, res, g):
    a, b = res
    if ta:
        da = _raw_mm(b, g, tb, True, 'l' if hi == 'r' else bool(hi))
    else:
        da = _raw_mm(g, b, False, not tb, 'r' if hi == 'r' else bool(hi))
    if tb:
        db = _raw_mm(g, a, True, ta, 'r' if hi == 'l' else bool(hi))
    else:
        db = _raw_mm(a, g, not ta, False, 'l' if hi == 'l' else bool(hi))
    return da, db


_mm_vjp.defvjp(_mm_vjp_fwd, _mm_vjp_bwd)


def _mm_diff(a, b, ta=False, tb=False, hi=False):
    return _mm_vjp(a, b, ta, tb, hi)


def _mm_plain(a, b, ta=False, tb=False, hi=False):
    return _raw_mm(a, b, ta, tb, hi)


def _shift_rows(x, k, up):
    if k == 0:
        return x
    n = x.shape[0]
    rows = lax.broadcasted_iota(jnp.int32, x.shape, 0)
    if up:
        return jnp.where(rows < n - k, pltpu.roll(x, n - k, 0), 0.0)
    return jnp.where(rows >= k, pltpu.roll(x, k, 0), 0.0)


@functools.partial(jax.custom_vjp, nondiff_argnums=(1, 2))
def _shift_vjp(x, k, up):
    return _shift_rows(x, k, up)


def _shift_vjp_fwd(x, k, up):
    return _shift_rows(x, k, up), None


def _shift_vjp_bwd(k, up, _, g):
    return (_shift_rows(g, k, not up),)


_shift_vjp.defvjp(_shift_vjp_fwd, _shift_vjp_bwd)


def _sigmoid(x):
    return 1.0 / (1.0 + jnp.exp(-x))


def _silu(x):
    return x * _sigmoid(x)


def _softplus(x):
    return jnp.maximum(x, 0.0) + jnp.log(1.0 + jnp.exp(-jnp.abs(x)))


def _gelu_tanh(x):
    return 0.5 * x * (1.0 + jnp.tanh(math.sqrt(2.0 / math.pi) * (x + 0.044715 * (x * x * x))))


def _expm1(x):
    series = x * (1.0 + x * (0.5 + x * (1.0 / 6.0 + x * (1.0 / 24.0))))
    return jnp.where(jnp.abs(x) < 0.03, series, jnp.exp(x) - 1.0)


def _rmsnorm(x, g):
    return x * lax.rsqrt(jnp.mean(x * x, axis=-1, keepdims=True) + NORM_EPS) * g


def _head_norm(o, gain, center):
    if center:
        o = o - jnp.mean(o, axis=-1, keepdims=True)
    return o * lax.rsqrt(jnp.mean(o * o, axis=-1, keepdims=True) + NORM_EPS) * gain


def _l2norm(x):
    return x * lax.rsqrt(jnp.sum(x * x, axis=-1, keepdims=True) + NORM_EPS)


def _iota2(shape, dim):
    return lax.broadcasted_iota(jnp.int32, shape, dim)


def _tri_ones(n, upper=False):
    i, j = _iota2((n, n), 0), _iota2((n, n), 1)
    return jnp.where((i <= j) if upper else (j <= i), 1.0, 0.0).astype(F32)


def _linscan(a, u, rev):
    n = a.shape[0]
    rows = _iota2(a.shape, 0)
    d = 1
    while d < n:
        if rev:
            valid = rows < n - d
            a_s, u_s = pltpu.roll(a, n - d, 0), pltpu.roll(u, n - d, 0)
        else:
            valid = rows >= d
            a_s, u_s = pltpu.roll(a, d, 0), pltpu.roll(u, d, 0)
        u = a * jnp.where(valid, u_s, 0.0) + u
        a = a * jnp.where(valid, a_s, 1.0)
        d *= 2
    return u


def _ret_group(mm, q, k, v, g, gain, state, cos, sin, dintra, qdec, kdec, cdec):
    n, _, dk = q.shape
    half = dk // 2

    def rot(t):
        t1, t2 = t[..., :half], t[..., half:]
        return jnp.concatenate([t1 * cos - t2 * sin, t1 * sin + t2 * cos], axis=-1)

    qr = rot(q)
    kr = rot(k) * (dk ** -0.5)
    o_intra = mm(mm(qr, kr, tb=True) * dintra, v)
    kv = mm(kr * kdec, v, ta=True)
    q_in = qr * qdec
    ys = []
    for i in range(n):
        o = o_intra[i] + mm(q_in[i], state)
        state = state * cdec + kv[i]
        ys.append(_head_norm(o, gain, True) * _silu(g[i]))
    return ys, state


def _gla_group(mm, q, k, v, r, glow, wgu, bias, gain, state_t):
    n, c, dk = q.shape
    logit = mm(glow.reshape(n * c, glow.shape[-1]), wgu).reshape(n, c, dk) + bias
    la = -_softplus(-logit) * (1.0 / GLA_TAU)
    cum = mm(jnp.broadcast_to(_tri_ones(c), (n, c, c)), la, hi='l')
    rows = lax.broadcasted_iota(jnp.int32, la.shape, 1)
    ref = jnp.sum(jnp.where(rows < c // 2, la, 0.0), axis=1, keepdims=True)
    tot = jnp.sum(la, axis=1, keepdims=True)
    fwd, bwd = jnp.exp(cum - ref), jnp.exp(ref - cum)
    qs = q * (dk ** -0.5)
    s_lo = mm(qs * fwd, k * bwd, tb=True)
    s_up = mm(qs * bwd, k * fwd, tb=True)
    i, j = _iota2((c, c), 0), _iota2((c, c), 1)
    o_intra = mm(jnp.where(i >= j, s_lo, s_up), v)
    q_in = qs * jnp.exp(cum)
    kv_t = mm(v, k * jnp.exp(tot - cum), ta=True)
    dec = jnp.exp(tot)
    ys = []
    for m in range(n):
        o = o_intra[m] + mm(q_in[m], state_t, tb=True)
        state_t = state_t * dec[m] + kv_t[m]
        ys.append(_head_norm(o, gain, False) * _silu(r[m]))
    return ys, state_t


def _gdn_group(mm, q, k, v, z, ba, a_log, dt_bias, gain, state, head):
    g, c, dk = q.shape
    lanes = lax.broadcasted_iota(jnp.int32, (1, 1, ba.shape[-1]), 2)
    oh_b = jnp.where(lanes == head, 1.0, 0.0).astype(F32)
    oh_a = jnp.where(lanes == head + GDN_HEADS, 1.0, 0.0).astype(F32)
    beta = _sigmoid(jnp.sum(ba * oh_b, axis=-1, keepdims=True))
    a_logit = jnp.sum(ba * oh_a, axis=-1, keepdims=True)
    a_h = jnp.sum(a_log * oh_b[0], axis=-1, keepdims=True)
    dt_h = jnp.sum(dt_bias * oh_b[0], axis=-1, keepdims=True)
    la = -jnp.exp(a_h) * _softplus(a_logit + dt_h)
    q = _l2norm(q) * (dk ** -0.5)
    k = _l2norm(k)
    tri = jnp.broadcast_to(_tri_ones(c), (g, c, c))
    tri_up = jnp.broadcast_to(_tri_ones(c, upper=True), (g, c, c))
    cum_k = mm(tri, la * jnp.ones((g, c, dk), F32), hi='l')
    la_sq = la * jnp.ones((g, c, c), F32)
    cum_i = mm(tri, la_sq, hi='l')
    cum_j = mm(la_sq, tri_up, ta=True, hi='r')
    i, j = _iota2((c, c), 0), _iota2((c, c), 1)
    strict = i > j
    rel = jnp.where(strict, jnp.exp(jnp.where(strict, cum_i - cum_j, 0.0)), 0.0)
    a_mat = beta * rel * mm(k, k, tb=True)
    inv = jnp.where(i == j, 1.0, 0.0).astype(F32) - a_mat
    power = a_mat
    for _ in range(int(math.log2(c)) - 1):
        power = mm(power, power, hi=True)
        inv = inv + mm(inv, power, hi=True)
    tot = jnp.sum(la, axis=1, keepdims=True)
    u = mm(inv, beta * v, hi=True)
    w = mm(inv, (beta * jnp.exp(cum_k)) * k, hi=True)
    k_end = k * jnp.exp(tot - cum_k)
    di, dj = _iota2((dk, dk), 0), _iota2((dk, dk), 1)
    trans = jnp.exp(tot) * jnp.where(di == dj, 1.0, 0.0).astype(F32) - mm(k_end, w, ta=True)
    inject = mm(k_end, u, ta=True)
    ys = []
    for n in range(g):
        state = mm(trans[n], state) + inject[n]
        ys.append(_head_norm(mm(q[n], state), gain, False) * _silu(z[n]))
    return ys, state


def _conv_taps(shift, x, taps):
    out = None
    for tap, w in enumerate(taps):
        term = shift(x, CONV_WIDTH - 1 - tap, False) * w
        out = term if out is None else out + term
    return out


def _lru_pre(mm, shift, xb, yb, taps, cb, lam, wr, br, wi, bi):
    xb = _conv_taps(shift, xb, taps) + cb
    r = _sigmoid(mm(xb, wr) + br)
    i = _sigmoid(mm(xb, wi) + bi)
    log_a = (-LRU_C) * _softplus(-lam) * r
    a = jnp.exp(log_a)
    u = jnp.sqrt(-_expm1(2.0 * log_a)) * (i * xb)
    return a, u, _gelu_tanh(yb)


_TOKEN_SPEC = pl.BlockSpec((8, 128), lambda *_: (0, 0))


MATMUL_VMEM_BUDGET = 40 * 1024 * 1024
_TILE_SIZES = (3072, 2048, 1536, 1024, 768, 512, 384, 256, 128)


def _tile_choices(extent):
    return [t for t in _TILE_SIZES if t <= extent and extent % t == 0] or [extent]


def _matmul_tiles(m, n, kdim, n_unit, k_unit, a_item, b_item, o_item, e_item):
    best = None
    for bm in _tile_choices(m):
        for bn in _tile_choices(n_unit):
            for bk in _tile_choices(k_unit):
                nk = kdim // bk
                vmem = 2 * (bm * bk * a_item + bk * bn * b_item + bm * bn * (o_item + e_item)) + bm * bn * 4 * (2 if nk > 1 else 1)
                if vmem > MATMUL_VMEM_BUDGET:
                    continue
                steps = (m // bm) * (n // bn) * nk
                traffic = m * kdim * a_item * (1 if nk == 1 else n // bn) + kdim * n * b_item * (m // bm)
                if best is None or (steps, traffic) < best[0]:
                    best = ((steps, traffic), (bm, bn, bk))
    return best[1]


def _matmul(a, b, *, name, ta=False, tb=False, out_dtype=F32, b_sharded=False, o_sharded=False,
            epilogue=None, extra=None, token=None):
    m, kdim = (a.shape[1], a.shape[0]) if ta else a.shape
    if b_sharded:
        nq = b.shape[2]
        n = b.shape[1] if tb else N_CHIPS * nq
        assert kdim == (N_CHIPS * nq if tb else b.shape[1])
    else:
        n = b.shape[0] if tb else b.shape[1]
        assert kdim == (b.shape[1] if tb else b.shape[0])
    n_unit = b.shape[2] if (b_sharded and not tb) else (n // N_CHIPS if o_sharded else n)
    k_unit = b.shape[2] if (b_sharded and tb) else kdim
    bm, bn, bk = _matmul_tiles(m, n, kdim, n_unit, k_unit, a.dtype.itemsize, b.dtype.itemsize,
                               jnp.dtype(out_dtype).itemsize, 0 if extra is None else extra.dtype.itemsize)
    nk = kdim // bk
    grid = (m // bm, n // bn, nk)

    a_spec = pl.BlockSpec((bk, bm), lambda i, j, k: (k, i)) if ta else pl.BlockSpec((bm, bk), lambda i, j, k: (i, k))
    if b_sharded and not tb:
        per = b.shape[2] // bn
        b_spec = pl.BlockSpec((None, bk, bn), lambda i, j, k: (j // per, k, j % per))
    elif b_sharded:
        per = b.shape[2] // bk
        b_spec = pl.BlockSpec((None, bn, bk), lambda i, j, k: (k // per, j, k % per))
    elif tb:
        b_spec = pl.BlockSpec((bn, bk), lambda i, j, k: (j, k))
    else:
        b_spec = pl.BlockSpec((bk, bn), lambda i, j, k: (k, j))
    if o_sharded:
        per_o = (n // N_CHIPS) // bn
        o_spec = pl.BlockSpec((None, bm, bn), lambda i, j, k: (j // per_o, i, j % per_o))
        out_shape = jax.ShapeDtypeStruct((N_CHIPS, m, n // N_CHIPS), out_dtype)
    else:
        o_spec = pl.BlockSpec((bm, bn), lambda i, j, k: (i, j))
        out_shape = jax.ShapeDtypeStruct((m, n), out_dtype)
    in_specs = [a_spec, b_spec]
    operands = [a, b]
    if extra is not None:
        in_specs.append(pl.BlockSpec((bm, bn), lambda i, j, k: (i, j)))
        operands.append(extra)
    if token is not None:
        in_specs.append(_TOKEN_SPEC)
        operands.append(token)

    def finish(acc, e_ref, o_ref):
        if epilogue == 'add':
            acc = acc + e_ref[...].astype(F32)
        elif epilogue == 'relu2':
            acc = jnp.square(jnp.maximum(acc, 0.0))
        elif epilogue == 'dact':
            acc = acc * (2.0 * jnp.sqrt(e_ref[...].astype(F32)))
        o_ref[...] = acc.astype(o_ref.dtype)

    def body_one_step(*refs):
        finish(_raw_mm(refs[0][...], refs[1][...], ta, tb, False), refs[2] if extra is not None else None, refs[-1])

    def body(*refs):
        a_ref, b_ref = refs[0], refs[1]
        e_ref = refs[2] if extra is not None else None
        o_ref, acc_ref = refs[-2], refs[-1]
        k = pl.program_id(2)

        @pl.when(k == 0)
        def _():
            acc_ref[...] = jnp.zeros_like(acc_ref)

        acc_ref[...] += _raw_mm(a_ref[...], b_ref[...], ta, tb, False)

        @pl.when(k == nk - 1)
        def _():
            finish(acc_ref[...], e_ref, o_ref)

    return pl.pallas_call(
        body_one_step if nk == 1 else body, name=name, grid=grid, in_specs=in_specs, out_specs=o_spec,
        out_shape=out_shape, scratch_shapes=[] if nk == 1 else [pltpu.VMEM((bm, bn), F32)],
        compiler_params=_cparams(),
    )(*operands)


def _rmsnorm_fwd(x, g, name, token=None):
    s, d = x.shape

    def body(x_ref, g_ref, *rest):
        rest[-1][...] = _rmsnorm(x_ref[...], g_ref[...]).astype(BF)

    return pl.pallas_call(
        body, name=name, grid=(s // ROW_BLOCK,),
        in_specs=[pl.BlockSpec((ROW_BLOCK, d), lambda i: (i, 0)), pl.BlockSpec((1, d), lambda i: (0, 0))]
        + ([] if token is None else [_TOKEN_SPEC]),
        out_specs=pl.BlockSpec((ROW_BLOCK, d), lambda i: (i, 0)),
        out_shape=jax.ShapeDtypeStruct((s, d), BF), compiler_params=_cparams(),
    )(x, g.reshape(1, d), *([] if token is None else [token]))


def _rmsnorm_bwd(x, g, dh, dres, name):
    s, d = x.shape

    def body(x_ref, g_ref, dh_ref, dres_ref, dx_ref, dxb_ref, dg_ref):
        _, vjp = jax.vjp(_rmsnorm, x_ref[...], g_ref[...])
        dx, dg = vjp(dh_ref[...].astype(F32))
        dx = dres_ref[...] + dx
        dx_ref[...] = dx
        dxb_ref[...] = dx.astype(BF)

        @pl.when(pl.program_id(0) == 0)
        def _():
            dg_ref[...] = jnp.zeros_like(dg_ref)

        dg_ref[...] += dg

    row = pl.BlockSpec((ROW_BLOCK, d), lambda i: (i, 0))
    vec = pl.BlockSpec((1, d), lambda i: (0, 0))
    dx, dxb, dg = pl.pallas_call(
        body, name=name, grid=(s // ROW_BLOCK,), in_specs=[row, vec, row, row], out_specs=[row, row, vec],
        out_shape=[jax.ShapeDtypeStruct((s, d), F32), jax.ShapeDtypeStruct((s, d), BF),
                   jax.ShapeDtypeStruct((1, d), F32)],
        compiler_params=_cparams(),
    )(x, g.reshape(1, d), dh, dres)
    return dx, dxb, dg.reshape(d)


def _loss_head(x, g, target):
    s, d = x.shape

    def loss_fn(xv, gv, tv):
        err = _rmsnorm(xv, gv) - tv
        return 0.5 * jnp.sum(jnp.mean(err * err, axis=-1, keepdims=True), axis=0, keepdims=True)

    def body(x_ref, g_ref, t_ref, dx_ref, dxb_ref, dg_ref, loss_ref):
        tv = t_ref[...]
        loss, vjp = jax.vjp(lambda xv, gv: loss_fn(xv, gv, tv), x_ref[...], g_ref[...])
        dx, dg = vjp(jnp.ones((1, 1), F32))
        dx_ref[...] = dx
        dxb_ref[...] = dx.astype(BF)

        @pl.when(pl.program_id(0) == 0)
        def _():
            dg_ref[...] = jnp.zeros_like(dg_ref)
            loss_ref[...] = jnp.zeros_like(loss_ref)

        dg_ref[...] += dg
        loss_ref[...] += loss * jnp.ones_like(loss_ref)

    row = pl.BlockSpec((ROW_BLOCK, d), lambda i: (i, 0))
    vec = pl.BlockSpec((1, d), lambda i: (0, 0))
    dx, dxb, dg, loss = pl.pallas_call(
        body, name="loss_head", grid=(s // ROW_BLOCK,), in_specs=[row, vec, row],
        out_specs=[row, row, vec, pl.BlockSpec((1, 128), lambda i: (0, 0))],
        out_shape=[jax.ShapeDtypeStruct((s, d), F32), jax.ShapeDtypeStruct((s, d), BF),
                   jax.ShapeDtypeStruct((1, d), F32), jax.ShapeDtypeStruct((1, 128), F32)],
        compiler_params=_cparams(),
    )(x, g.reshape(1, d), target)
    return loss[0, 0], dx, dxb, dg.reshape(d)


def _ret_tables(s, dk):
    h = jnp.arange(RET_HEADS, dtype=F32)
    log_gamma = jnp.log1p(-jnp.exp2(-5.0 - h))
    pos = jnp.arange(CHUNK, dtype=F32)
    dist = jnp.abs(pos[:, None] - pos[None, :])
    dintra = jnp.exp(log_gamma[:, None, None] * dist)
    qdec = jnp.exp(log_gamma[:, None] * (pos + 1.0))[:, :, None]
    kdec = jnp.exp(log_gamma[:, None] * (CHUNK - 1.0 - pos))[:, :, None]
    cdec = jnp.exp(log_gamma * CHUNK)[:, None, None]
    inv = ROPE_BASE ** (-jnp.arange(0, dk, 2, dtype=F32) / dk)
    ang = jnp.arange(s, dtype=F32)[:, None] * inv[None, :]
    return jnp.cos(ang), jnp.sin(ang), dintra, qdec, kdec, cdec


def _ret_specs(s, dk, dv, tb, rev):
    nh = RET_HEADS
    nb = s // tb
    bi = (lambda b: nb - 1 - b) if rev else (lambda b: b)
    voff = 2 * nh * dk // dv
    cpb = tb // CHUNK
    return dict(
        q=pl.BlockSpec((tb, dk), lambda h, b: (bi(b), h)),
        k=pl.BlockSpec((tb, dk), lambda h, b: (bi(b), nh + h)),
        v=pl.BlockSpec((tb, dv), lambda h, b: (bi(b), voff + h)),
        g=pl.BlockSpec((tb, dv), lambda h, b: (bi(b), voff + nh + h)),
        gain=pl.BlockSpec((None, 1, dv), lambda h, b: (h, 0, 0)),
        cs=pl.BlockSpec((tb, dk // 2), lambda h, b: (bi(b), 0)),
        dintra=pl.BlockSpec((None, CHUNK, CHUNK), lambda h, b: (h, 0, 0)),
        dec=pl.BlockSpec((None, CHUNK, 1), lambda h, b: (h, 0, 0)),
        cdec=pl.BlockSpec((None, 1, 1), lambda h, b: (h, 0, 0)),
        hv=pl.BlockSpec((tb, dv), lambda h, b: (bi(b), h)),
        hk=pl.BlockSpec((tb, dk), lambda h, b: (bi(b), h)),
        st=pl.BlockSpec((None, None, dk, dv), lambda h, b: (h, bi(b), 0, 0)),
    )


def _group(t):
    return t.reshape(t.shape[0] // CHUNK, CHUNK, t.shape[-1])


def _ret_fwd(proj, gain, tables):
    s = proj.shape[0]
    d = proj.shape[1] // 6
    dk, dv = d // RET_HEADS, 2 * d // RET_HEADS
    tb = min(TOKEN_BLOCK, s)
    sp = _ret_specs(s, dk, dv, tb, False)
    cos, sin, dintra, qdec, kdec, cdec = tables

    def body(q_ref, k_ref, v_ref, g_ref, gain_ref, cos_ref, sin_ref, di_ref, qd_ref, kd_ref, cd_ref,
             y_ref, st_ref, state):
        @pl.when(pl.program_id(1) == 0)
        def _():
            state[...] = jnp.zeros_like(state)

        st_ref[...] = state[...]
        ys, new_state = _ret_group(_mm_plain, _group(q_ref[...]), _group(k_ref[...]), _group(v_ref[...]),
                                   _group(g_ref[...]), gain_ref[...], state[...], _group(cos_ref[...]),
                                   _group(sin_ref[...]), di_ref[...], qd_ref[...], kd_ref[...], cd_ref[...])
        for c, y in enumerate(ys):
            y_ref[pl.ds(c * CHUNK, CHUNK), :] = y.astype(BF)
        state[...] = new_state

    return pl.pallas_call(
        body, name="ret_fwd", grid=(RET_HEADS, s // tb),
        in_specs=[sp['q'], sp['k'], sp['v'], sp['g'], sp['gain'], sp['cs'], sp['cs'], sp['dintra'], sp['dec'],
                  sp['dec'], sp['cdec']],
        out_specs=[sp['hv'], sp['st']],
        out_shape=[jax.ShapeDtypeStruct((s, RET_HEADS * dv), BF),
                   jax.ShapeDtypeStruct((RET_HEADS, s // tb, dk, dv), F32)],
        scratch_shapes=[pltpu.VMEM((dk, dv), F32)], compiler_params=_cparams(),
    )(proj, proj, proj, proj, gain.reshape(RET_HEADS, 1, dv), cos, sin, dintra, qdec, kdec, cdec)


def _ret_bwd(proj, gain, tables, states, dy):
    s = proj.shape[0]
    d = proj.shape[1] // 6
    dk, dv = d // RET_HEADS, 2 * d // RET_HEADS
    tb = min(TOKEN_BLOCK, s)
    cpb = tb // CHUNK
    sp = _ret_specs(s, dk, dv, tb, True)
    cos, sin, dintra, qdec, kdec, cdec = tables

    def body(q_ref, k_ref, v_ref, g_ref, gain_ref, cos_ref, sin_ref, di_ref, qd_ref, kd_ref, cd_ref, st_ref,
             dy_ref, dq_ref, dk_ref, dv_ref, dg_ref, dgain_ref, dstate):
        @pl.when(pl.program_id(1) == 0)
        def _():
            dstate[...] = jnp.zeros_like(dstate)
            dgain_ref[...] = jnp.zeros_like(dgain_ref)

        cos_g, sin_g = _group(cos_ref[...]), _group(sin_ref[...])
        di, qd, kd, cd = di_ref[...], qd_ref[...], kd_ref[...], cd_ref[...]

        def fn(q, k, v, g, gn, st):
            return _ret_group(_mm_diff, q, k, v, g, gn, st, cos_g, sin_g, di, qd, kd, cd)

        _, vjp = jax.vjp(fn, _group(q_ref[...]), _group(k_ref[...]), _group(v_ref[...]), _group(g_ref[...]),
                         gain_ref[...], st_ref[...])
        dys = [dy_ref[pl.ds(c * CHUNK, CHUNK), :].astype(F32) for c in range(cpb)]
        dq, dkk, dvv, dg, dgn, dst = vjp((dys, dstate[...]))
        dq_ref[...] = dq.reshape(tb, dk).astype(BF)
        dk_ref[...] = dkk.reshape(tb, dk).astype(BF)
        dv_ref[...] = dvv.reshape(tb, dv).astype(BF)
        dg_ref[...] = dg.reshape(tb, dv).astype(BF)
        dgain_ref[...] += dgn
        dstate[...] = dst

    dq, dkk, dvv, dg, dgain = pl.pallas_call(
        body, name="ret_bwd", grid=(RET_HEADS, s // tb),
        in_specs=[sp['q'], sp['k'], sp['v'], sp['g'], sp['gain'], sp['cs'], sp['cs'], sp['dintra'], sp['dec'],
                  sp['dec'], sp['cdec'], sp['st'], sp['hv']],
        out_specs=[sp['hk'], sp['hk'], sp['hv'], sp['hv'], sp['gain']],
        out_shape=[jax.ShapeDtypeStruct((s, RET_HEADS * dk), BF), jax.ShapeDtypeStruct((s, RET_HEADS * dk), BF),
                   jax.ShapeDtypeStruct((s, RET_HEADS * dv), BF), jax.ShapeDtypeStruct((s, RET_HEADS * dv), BF),
                   jax.ShapeDtypeStruct((RET_HEADS, 1, dv), F32)],
        scratch_shapes=[pltpu.VMEM((dk, dv), F32)], compiler_params=_cparams(),
    )(proj, proj, proj, proj, gain.reshape(RET_HEADS, 1, dv), cos, sin, dintra, qdec, kdec, cdec, states, dy)
    return jnp.concatenate([dq, dkk, dvv, dg], axis=1), dgain.reshape(RET_HEADS, dv)


def _gla_specs(s, dk, dv, tb, rev):
    nh = GLA_HEADS
    nb = s // tb
    bi = (lambda b: nb - 1 - b) if rev else (lambda b: b)
    voff = 2 * nh * dk // dv
    cpb = tb // CHUNK
    return dict(
        q=pl.BlockSpec((tb, dk), lambda h, b: (bi(b), h)),
        k=pl.BlockSpec((tb, dk), lambda h, b: (bi(b), nh + h)),
        v=pl.BlockSpec((tb, dv), lambda h, b: (bi(b), voff + h)),
        r=pl.BlockSpec((tb, dv), lambda h, b: (bi(b), voff + nh + h)),
        glow=pl.BlockSpec((tb, 128), lambda h, b: (bi(b), 0)),
        wgu=pl.BlockSpec((128, dk), lambda h, b: (0, h)),
        bias=pl.BlockSpec((1, dk), lambda h, b: (0, h)),
        gain=pl.BlockSpec((None, 1, dv), lambda h, b: (h, 0, 0)),
        hv=pl.BlockSpec((tb, dv), lambda h, b: (bi(b), h)),
        hk=pl.BlockSpec((tb, dk), lambda h, b: (bi(b), h)),
        hg=pl.BlockSpec((tb, 128), lambda h, b: (bi(b), h)),
        st=pl.BlockSpec((None, None, dv, dk), lambda h, b: (h, bi(b), 0, 0)),
    )


def _gla_fwd(proj, glow, wgu, bias, gain):
    s = proj.shape[0]
    d = proj.shape[1] // 3
    dk, dv = d // 2 // GLA_HEADS, d // GLA_HEADS
    tb = min(TOKEN_BLOCK, s)
    sp = _gla_specs(s, dk, dv, tb, False)

    def body(q_ref, k_ref, v_ref, r_ref, gl_ref, wgu_ref, b_ref, gain_ref, y_ref, st_ref, state):
        @pl.when(pl.program_id(1) == 0)
        def _():
            state[...] = jnp.zeros_like(state)

        st_ref[...] = state[...]
        ys, new_state = _gla_group(_mm_plain, _group(q_ref[...]), _group(k_ref[...]), _group(v_ref[...]),
                                   _group(r_ref[...]), _group(gl_ref[...]), wgu_ref[...], b_ref[...], gain_ref[...],
                                   state[...])
        for c, y in enumerate(ys):
            y_ref[pl.ds(c * CHUNK, CHUNK), :] = y.astype(BF)
        state[...] = new_state

    return pl.pallas_call(
        body, name="gla_fwd", grid=(GLA_HEADS, s // tb),
        in_specs=[sp['q'], sp['k'], sp['v'], sp['r'], sp['glow'], sp['wgu'], sp['bias'], sp['gain']],
        out_specs=[sp['hv'], sp['st']],
        out_shape=[jax.ShapeDtypeStruct((s, GLA_HEADS * dv), BF),
                   jax.ShapeDtypeStruct((GLA_HEADS, s // tb, dv, dk), F32)],
        scratch_shapes=[pltpu.VMEM((dv, dk), F32)], compiler_params=_cparams(),
    )(proj, proj, proj, proj, glow, wgu, bias, gain.reshape(GLA_HEADS, 1, dv))


def _gla_bwd(proj, glow, wgu, bias, gain, states, dy):
    s = proj.shape[0]
    d = proj.shape[1] // 3
    dk, dv = d // 2 // GLA_HEADS, d // GLA_HEADS
    tb = min(TOKEN_BLOCK, s)
    cpb = tb // CHUNK
    sp = _gla_specs(s, dk, dv, tb, True)

    def body(q_ref, k_ref, v_ref, r_ref, gl_ref, wgu_ref, b_ref, gain_ref, st_ref, dy_ref,
             dq_ref, dk_ref, dv_ref, dr_ref, dgl_ref, dwgu_ref, db_ref, dgain_ref, dstate):
        @pl.when(pl.program_id(1) == 0)
        def _():
            dstate[...] = jnp.zeros_like(dstate)
            dwgu_ref[...] = jnp.zeros_like(dwgu_ref)
            db_ref[...] = jnp.zeros_like(db_ref)
            dgain_ref[...] = jnp.zeros_like(dgain_ref)

        def fn(q, k, v, r, gl, w, b, gn, st):
            return _gla_group(_mm_diff, q, k, v, r, gl, w, b, gn, st)

        _, vjp = jax.vjp(fn, _group(q_ref[...]), _group(k_ref[...]), _group(v_ref[...]), _group(r_ref[...]),
                         _group(gl_ref[...]), wgu_ref[...], b_ref[...], gain_ref[...], st_ref[...])
        dys = [dy_ref[pl.ds(c * CHUNK, CHUNK), :].astype(F32) for c in range(cpb)]
        dq, dkk, dvv, dr, dgl, dw, db, dgn, dst = vjp((dys, dstate[...]))
        dq_ref[...] = dq.reshape(tb, dk).astype(BF)
        dk_ref[...] = dkk.reshape(tb, dk).astype(BF)
        dv_ref[...] = dvv.reshape(tb, dv).astype(BF)
        dr_ref[...] = dr.reshape(tb, dv).astype(BF)
        dgl_ref[...] = dgl.reshape(tb, dgl.shape[-1]).astype(BF)
        dwgu_ref[...] += dw
        db_ref[...] += db
        dgain_ref[...] += dgn
        dstate[...] = dst

    nh = GLA_HEADS
    dq, dkk, dvv, dr, dgl, dwgu, db, dgain = pl.pallas_call(
        body, name="gla_bwd", grid=(nh, s // tb),
        in_specs=[sp['q'], sp['k'], sp['v'], sp['r'], sp['glow'], sp['wgu'], sp['bias'], sp['gain'], sp['st'],
                  sp['hv']],
        out_specs=[sp['hk'], sp['hk'], sp['hv'], sp['hv'], sp['hg'], sp['wgu'], sp['bias'], sp['gain']],
        out_shape=[jax.ShapeDtypeStruct((s, nh * dk), BF), jax.ShapeDtypeStruct((s, nh * dk), BF),
                   jax.ShapeDtypeStruct((s, nh * dv), BF), jax.ShapeDtypeStruct((s, nh * dv), BF),
                   jax.ShapeDtypeStruct((s, nh * 128), BF), jax.ShapeDtypeStruct((128, nh * dk), F32),
                   jax.ShapeDtypeStruct((1, nh * dk), F32), jax.ShapeDtypeStruct((nh, 1, dv), F32)],
        scratch_shapes=[pltpu.VMEM((dv, dk), F32)], compiler_params=_cparams(),
    )(proj, proj, proj, proj, glow, wgu, bias, gain.reshape(nh, 1, dv), states, dy)
    return jnp.concatenate([dq, dkk, dvv, dr], axis=1), dgl, dwgu, db, dgain.reshape(nh, dv)


def _gdn_specs(s, dk):
    nh = GDN_HEADS
    col = lambda off: pl.BlockSpec((s, dk), lambda h: (0, off + h))
    tap = lambda off: pl.BlockSpec((CONV_WIDTH, dk), lambda h: (0, off + h))
    vec = pl.BlockSpec((1, 128), lambda h: (0, 0))
    return dict(q=col(0), k=col(nh), v=col(2 * nh), z=col(3 * nh), ba=pl.BlockSpec((s, 128), lambda h: (0, 0)),
                cq=tap(0), ck=tap(nh), cv=tap(2 * nh), vec=vec, gain=pl.BlockSpec((1, dk), lambda h: (0, 0)),
                head=col(0))


def _conv_silu(shift, x, taps):
    return _silu(_conv_taps(shift, x, taps))


def _load_taps(ref):
    return [ref[t:t + 1, :] for t in range(CONV_WIDTH)]


def _gdn_fwd(proj, ba, conv_w, a_log, dt_bias, gain):
    s = proj.shape[0]
    dk = proj.shape[1] // (4 * GDN_HEADS)
    sp = _gdn_specs(s, dk)
    rows = GDN_GROUP * CHUNK

    def grp(t):
        return t.reshape(GDN_GROUP, CHUNK, t.shape[-1])

    ngroups = s // rows

    def body(q_ref, k_ref, v_ref, z_ref, ba_ref, cq_ref, ck_ref, cv_ref, al_ref, dt_ref, gain_ref, y_ref, states,
             qc, kc, vc, state):
        head = pl.program_id(0)
        qc[...] = _conv_silu(_shift_rows, q_ref[...], _load_taps(cq_ref))
        kc[...] = _conv_silu(_shift_rows, k_ref[...], _load_taps(ck_ref))
        vc[...] = _conv_silu(_shift_rows, v_ref[...], _load_taps(cv_ref))
        state[...] = jnp.zeros_like(state)

        def step(n, carry):
            base = pl.multiple_of(n * rows, rows)
            sl = pl.ds(base, rows)
            states[n] = state[...]
            ys, new_state = _gdn_group(_mm_plain, grp(qc[sl, :]), grp(kc[sl, :]), grp(vc[sl, :]), grp(z_ref[sl, :]),
                                       grp(ba_ref[sl, :]), al_ref[...], dt_ref[...], gain_ref[...], state[...], head)
            for i, y in enumerate(ys):
                y_ref[pl.ds(base + i * CHUNK, CHUNK), :] = y.astype(BF)
            state[...] = new_state
            return carry

        lax.fori_loop(0, ngroups, step, 0)

    return pl.pallas_call(
        body, name="gdn_fwd", grid=(GDN_HEADS,),
        in_specs=[sp['q'], sp['k'], sp['v'], sp['z'], sp['ba'], sp['cq'], sp['ck'], sp['cv'], sp['vec'], sp['vec'],
                  sp['gain']],
        out_specs=[sp['head'], pl.BlockSpec((None, ngroups, dk, dk), lambda h: (h, 0, 0, 0))],
        out_shape=[jax.ShapeDtypeStruct((s, GDN_HEADS * dk), BF),
                   jax.ShapeDtypeStruct((GDN_HEADS, ngroups, dk, dk), F32)],
        scratch_shapes=[pltpu.VMEM((s, dk), F32)] * 3 + [pltpu.VMEM((dk, dk), F32)],
        compiler_params=_cparams(),
    )(proj, proj, proj, proj, ba, conv_w, conv_w, conv_w, a_log, dt_bias, gain)


def _gdn_bwd(proj, ba, conv_w, a_log, dt_bias, gain, group_states, dy):
    s = proj.shape[0]
    dk = proj.shape[1] // (4 * GDN_HEADS)
    sp = _gdn_specs(s, dk)
    rows = GDN_GROUP * CHUNK
    ngroups = s // rows

    def grp(t):
        return t.reshape(GDN_GROUP, CHUNK, t.shape[-1])

    def body(q_ref, k_ref, v_ref, z_ref, ba_ref, cq_ref, ck_ref, cv_ref, al_ref, dt_ref, gain_ref, states, dy_ref,
             dq_ref, dk_ref, dv_ref, dz_ref, dba_ref, dcq_ref, dck_ref, dcv_ref, dal_ref, ddt_ref, dgain_ref,
             qc, kc, vc, dqc, dkc, dvc, dstate):
        head = pl.program_id(0)

        @pl.when(head == 0)
        def _():
            dba_ref[...] = jnp.zeros_like(dba_ref)
            dal_ref[...] = jnp.zeros_like(dal_ref)
            ddt_ref[...] = jnp.zeros_like(ddt_ref)
            dgain_ref[...] = jnp.zeros_like(dgain_ref)

        qc[...] = _conv_silu(_shift_rows, q_ref[...], _load_taps(cq_ref))
        kc[...] = _conv_silu(_shift_rows, k_ref[...], _load_taps(ck_ref))
        vc[...] = _conv_silu(_shift_rows, v_ref[...], _load_taps(cv_ref))
        dstate[...] = jnp.zeros_like(dstate)

        def bstep(i, carry):
            n = ngroups - 1 - i
            base = pl.multiple_of(n * rows, rows)
            sl = pl.ds(base, rows)

            def fn(q, k, v, z, b, al, dt, gn, st):
                return _gdn_group(_mm_diff, q, k, v, z, b, al, dt, gn, st, head)

            _, vjp = jax.vjp(fn, grp(qc[sl, :]), grp(kc[sl, :]), grp(vc[sl, :]), grp(z_ref[sl, :]),
                             grp(ba_ref[sl, :]), al_ref[...], dt_ref[...], gain_ref[...], states[n])
            dys = [dy_ref[pl.ds(base + j * CHUNK, CHUNK), :].astype(F32) for j in range(GDN_GROUP)]
            dq, dkk, dvv, dz, db, dal, ddt, dgn, dst = vjp((dys, dstate[...]))
            dqc[sl, :] = dq.reshape(rows, dk)
            dkc[sl, :] = dkk.reshape(rows, dk)
            dvc[sl, :] = dvv.reshape(rows, dk)
            dz_ref[sl, :] = dz.reshape(rows, dk).astype(BF)
            dba_ref[sl, :] += db.reshape(rows, db.shape[-1])
            dal_ref[...] += dal
            ddt_ref[...] += ddt
            dgain_ref[...] += dgn
            dstate[...] = dst
            return carry

        lax.fori_loop(0, ngroups, bstep, 0)

        for x_ref, c_ref, dpost, dx_ref, dc_ref in ((q_ref, cq_ref, dqc, dq_ref, dcq_ref),
                                                    (k_ref, ck_ref, dkc, dk_ref, dck_ref),
                                                    (v_ref, cv_ref, dvc, dv_ref, dcv_ref)):
            _, vjp = jax.vjp(lambda x, *taps: _conv_silu(_shift_vjp, x, taps), x_ref[...], *_load_taps(c_ref))
            grads = vjp(dpost[...])
            dx_ref[...] = grads[0].astype(BF)
            for t in range(CONV_WIDTH):
                dc_ref[t:t + 1, :] = grads[1 + t]

    nh = GDN_HEADS
    col_bf = jax.ShapeDtypeStruct((s, nh * dk), BF)
    tap_out = jax.ShapeDtypeStruct((CONV_WIDTH, nh * dk), F32)
    tap_spec = pl.BlockSpec((CONV_WIDTH, dk), lambda h: (0, h))
    dq, dkk, dvv, dz, dba, dcq, dck, dcv, dal, ddt, dgain = pl.pallas_call(
        body, name="gdn_bwd", grid=(nh,),
        in_specs=[sp['q'], sp['k'], sp['v'], sp['z'], sp['ba'], sp['cq'], sp['ck'], sp['cv'], sp['vec'], sp['vec'],
                  sp['gain'], pl.BlockSpec((None, ngroups, dk, dk), lambda h: (h, 0, 0, 0)), sp['head']],
        out_specs=[sp['head']] * 4 + [sp['ba'], tap_spec, tap_spec, tap_spec, sp['vec'], sp['vec'], sp['gain']],
        out_shape=[col_bf] * 4 + [jax.ShapeDtypeStruct((s, 128), F32), tap_out, tap_out, tap_out,
                                  jax.ShapeDtypeStruct((1, 128), F32), jax.ShapeDtypeStruct((1, 128), F32),
                                  jax.ShapeDtypeStruct((1, dk), F32)],
        scratch_shapes=[pltpu.VMEM((s, dk), F32)] * 6 + [pltpu.VMEM((dk, dk), F32)],
        compiler_params=_cparams(),
    )(proj, proj, proj, proj, ba, conv_w, conv_w, conv_w, a_log, dt_bias, gain, group_states, dy)
    return (jnp.concatenate([dq, dkk, dvv, dz], axis=1), dba, jnp.concatenate([dcq, dck, dcv], axis=1), dal, ddt,
            dgain)


def _lru_specs(s, bw):
    nb = LRU_BLOCKS
    return dict(
        xb=pl.BlockSpec((s, bw), lambda n: (0, n)), yb=pl.BlockSpec((s, bw), lambda n: (0, nb + n)),
        taps=pl.BlockSpec((CONV_WIDTH, bw), lambda n: (0, n)), vec=pl.BlockSpec((1, bw), lambda n: (0, n)),
        w=pl.BlockSpec((None, bw, bw), lambda n: (n, 0, 0)), b=pl.BlockSpec((None, 1, bw), lambda n: (n, 0, 0)),
        col=pl.BlockSpec((s, bw), lambda n: (0, n)))


def _lru_fwd(proj, conv_w, conv_b, lam, wr, br, wi, bi):
    s = proj.shape[0]
    bw = proj.shape[1] // (2 * LRU_BLOCKS)
    sp = _lru_specs(s, bw)

    def body(xb_ref, yb_ref, cw_ref, cb_ref, lam_ref, wr_ref, br_ref, wi_ref, bi_ref, y_ref):
        a, u, gy = _lru_pre(_mm_plain, _shift_rows, xb_ref[...], yb_ref[...], _load_taps(cw_ref), cb_ref[...],
                            lam_ref[...], wr_ref[...], br_ref[...], wi_ref[...], bi_ref[...])
        y_ref[...] = (_linscan(a, u, False) * gy).astype(BF)

    return pl.pallas_call(
        body, name="lru_fwd", grid=(LRU_BLOCKS,),
        in_specs=[sp['xb'], sp['yb'], sp['taps'], sp['vec'], sp['vec'], sp['w'], sp['b'], sp['w'], sp['b']],
        out_specs=sp['col'], out_shape=jax.ShapeDtypeStruct((s, LRU_BLOCKS * bw), BF),
        compiler_params=_cparams(),
    )(proj, proj, conv_w, conv_b, lam, wr, br, wi, bi)


def _lru_bwd(proj, conv_w, conv_b, lam, wr, br, wi, bi, dy):
    s = proj.shape[0]
    nb = LRU_BLOCKS
    bw = proj.shape[1] // (2 * nb)
    sp = _lru_specs(s, bw)

    def body(xb_ref, yb_ref, cw_ref, cb_ref, lam_ref, wr_ref, br_ref, wi_ref, bi_ref, dy_ref,
             dxb_ref, dyb_ref, dcw_ref, dcb_ref, dlam_ref, dwr_ref, dbr_ref, dwi_ref, dbi_ref):
        def pre(xb, yb, t0, t1, t2, t3, cb, lm, w_r, b_r, w_i, b_i):
            return _lru_pre(_mm_diff, _shift_vjp, xb, yb, (t0, t1, t2, t3), cb, lm, w_r, b_r, w_i, b_i)

        (a, u, gy), vjp = jax.vjp(pre, xb_ref[...], yb_ref[...], *_load_taps(cw_ref), cb_ref[...], lam_ref[...],
                                  wr_ref[...], br_ref[...], wi_ref[...], bi_ref[...])
        h = _linscan(a, u, False)
        dout = dy_ref[...].astype(F32)
        g = _linscan(_shift_rows(a, 1, True), dout * gy, True)
        grads = vjp((g * _shift_rows(h, 1, False), g, dout * h))
        dxb_ref[...] = grads[0].astype(BF)
        dyb_ref[...] = grads[1].astype(BF)
        for t in range(CONV_WIDTH):
            dcw_ref[t:t + 1, :] = grads[2 + t]
        dcb_ref[...] = grads[6]
        dlam_ref[...] = grads[7]
        dwr_ref[...] = grads[8]
        dbr_ref[...] = grads[9]
        dwi_ref[...] = grads[10]
        dbi_ref[...] = grads[11]

    outs = pl.pallas_call(
        body, name="lru_bwd", grid=(nb,),
        in_specs=[sp['xb'], sp['yb'], sp['taps'], sp['vec'], sp['vec'], sp['w'], sp['b'], sp['w'], sp['b'], sp['col']],
        out_specs=[sp['col'], sp['col'], sp['taps'], sp['vec'], sp['vec'], sp['w'], sp['b'], sp['w'], sp['b']],
        out_shape=[jax.ShapeDtypeStruct((s, nb * bw), BF), jax.ShapeDtypeStruct((s, nb * bw), BF),
                   jax.ShapeDtypeStruct((CONV_WIDTH, nb * bw), F32), jax.ShapeDtypeStruct((1, nb * bw), F32),
                   jax.ShapeDtypeStruct((1, nb * bw), F32), jax.ShapeDtypeStruct((nb, bw, bw), F32),
                   jax.ShapeDtypeStruct((nb, 1, bw), F32), jax.ShapeDtypeStruct((nb, bw, bw), F32),
                   jax.ShapeDtypeStruct((nb, 1, bw), F32)],
        compiler_params=_cparams(),
    )(proj, proj, conv_w, conv_b, lam, wr, br, wi, bi, dy)
    return (jnp.concatenate([outs[0], outs[1]], axis=1),) + tuple(outs[2:])


ROW_STEP_BYTES = 12 * 1024 * 1024


def _row_tile(rows, cols, bytes_per_row_set):
    t = 8
    while t * 2 <= rows and rows % (t * 2) == 0 and (t * 2) * cols * bytes_per_row_set <= ROW_STEP_BYTES:
        t *= 2
    return t


def _sum_slabs(a, out_dtype, name):
    n, rows, cols = a.shape
    tr = _row_tile(rows, cols, 4 * (n + 1))

    def body(*refs):
        acc = refs[0][...].astype(F32)
        for r in refs[1:n]:
            acc = acc + r[...].astype(F32)
        refs[n][...] = acc.astype(out_dtype)

    specs = [pl.BlockSpec((None, tr, cols), functools.partial(lambda i, k: (k, i, 0), k=k)) for k in range(n)]
    return pl.pallas_call(
        body, name=name, grid=(rows // tr,), in_specs=specs, out_specs=pl.BlockSpec((tr, cols), lambda i: (i, 0)),
        out_shape=jax.ShapeDtypeStruct((rows, cols), out_dtype), compiler_params=_cparams(),
    )(*([a] * n))


def _sum_keep_and_landed(g, landed, core, name):
    four, hr, cols = landed.shape
    tr = _row_tile(hr, cols, 2 * 3)
    tph = hr // tr

    def body(core_ref, g_ref, l_ref, o_ref):
        o_ref[...] = (g_ref[...].astype(F32) + l_ref[...].astype(F32)).astype(o_ref.dtype)

    grid_spec = pltpu.PrefetchScalarGridSpec(
        num_scalar_prefetch=1, grid=(four, tph),
        in_specs=[pl.BlockSpec((None, tr, cols), lambda j, r, core_ref: (j, core_ref[0] * tph + r, 0)),
                  pl.BlockSpec((None, tr, cols), lambda j, r, core_ref: (j, r, 0))],
        out_specs=pl.BlockSpec((None, tr, cols), lambda j, r, core_ref: (j, r, 0)))
    return pl.pallas_call(body, name=name, grid_spec=grid_spec, out_shape=jax.ShapeDtypeStruct(landed.shape, g.dtype),
                          compiler_params=_cparams())(core, g, landed)


def _sum_chip_parts(parts, landed, chip, name):
    _, rows, cols = parts.shape
    tr = _row_tile(rows, cols, 2 * 4 + 4)

    def body(chip_ref, p_ref, l0_ref, l1_ref, l2_ref, o_ref):
        acc = l0_ref[...].astype(F32) + l1_ref[...].astype(F32)
        o_ref[...] = (acc + l2_ref[...].astype(F32)) + p_ref[...].astype(F32)

    slab = lambda k: pl.BlockSpec((None, tr, cols), lambda i, chip_ref: (k, i, 0))
    grid_spec = pltpu.PrefetchScalarGridSpec(
        num_scalar_prefetch=1, grid=(rows // tr,),
        in_specs=[pl.BlockSpec((None, tr, cols), lambda i, chip_ref: (chip_ref[0], i, 0)), slab(0), slab(1), slab(2)],
        out_specs=pl.BlockSpec((tr, cols), lambda i, chip_ref: (i, 0)))
    return pl.pallas_call(body, name=name, grid_spec=grid_spec, out_shape=jax.ShapeDtypeStruct((rows, cols), F32),
                          compiler_params=_cparams())(chip, parts, landed, landed, landed)


_ADAM_C1 = 1.0 / (1.0 - ADAM_B1 ** ADAM_STEP)
_ADAM_C2 = 1.0 / (1.0 - ADAM_B2 ** ADAM_STEP)


def _adamw_math(w, g, m, v):
    nm = ADAM_B1 * m + (1.0 - ADAM_B1) * g
    nv = ADAM_B2 * v + (1.0 - ADAM_B2) * (g * g)
    den = jnp.sqrt(nv * _ADAM_C2) + ADAM_EPS
    inv = pl.reciprocal(den, approx=True)
    inv = inv * (2.0 - den * inv)
    delta = -ADAM_LR * ((nm * _ADAM_C1) * inv + ADAM_WD * w)
    return delta, nm, nv


def _adamw(w, g, m, v, name):
    rows, cols = w.shape
    tr = _row_tile(rows, cols, 4 * 7)

    def body(w_ref, g_ref, m_ref, v_ref, d_ref, nm_ref, nv_ref):
        d_ref[...], nm_ref[...], nv_ref[...] = _adamw_math(w_ref[...], g_ref[...], m_ref[...], v_ref[...])

    spec = pl.BlockSpec((tr, cols), lambda i: (i, 0))
    shape = jax.ShapeDtypeStruct((rows, cols), F32)
    return pl.pallas_call(
        body, name=name, grid=(rows // tr,), in_specs=[spec] * 4, out_specs=[spec] * 3, out_shape=[shape] * 3,
        compiler_params=_cparams(),
    )(w, g, m, v)


def _adamw_halves(w, m, v, layer, mine, theirs, core, name, carried=None):
    n_layers, rows, cols = w.shape
    hr = mine.shape[0]
    tr = _row_tile(hr, cols, 4 * 9)
    tph = hr // tr

    def body(core_ref, w_ref, q_ref, t_ref, m_ref, v_ref, *rest):
        g_ref, d_ref, nm_ref, nv_ref = rest[-4:]
        is_mine = (pl.program_id(0) // tph) == core_ref[0]
        g = jnp.where(is_mine, q_ref[...], t_ref[...])
        g_ref[...] = g
        d_ref[...], nm_ref[...], nv_ref[...] = _adamw_math(w_ref[...], g, m_ref[...], v_ref[...])

    slab = pl.BlockSpec((None, tr, cols), lambda i, core_ref: (layer, i, 0))

    def mine_index(i, core_ref):
        return jnp.where(i // tph == core_ref[0], i % tph, jnp.where(core_ref[0] == 0, tph - 1, 0)), 0

    def theirs_index(i, core_ref):
        return jnp.where(i // tph == core_ref[0], jnp.where(core_ref[0] == 0, 0, tph - 1), i % tph), 0

    in_specs = [slab, pl.BlockSpec((tr, cols), mine_index), pl.BlockSpec((tr, cols), theirs_index), slab, slab]
    operands = [core, w, mine, theirs, m, v]
    aliases = {}
    if carried is not None:
        in_specs += [_ANY] * 4
        aliases = {len(operands) + k: k for k in range(4)}
        operands += list(carried)
    grid_spec = pltpu.PrefetchScalarGridSpec(num_scalar_prefetch=1, grid=(rows // tr,), in_specs=in_specs,
                                             out_specs=[slab] * 4)
    return pl.pallas_call(
        body, name=name, grid_spec=grid_spec, out_shape=[jax.ShapeDtypeStruct(w.shape, F32)] * 4,
        input_output_aliases=aliases, compiler_params=_cparams(),
    )(*operands)


_ANY = pl.BlockSpec(memory_space=pl.ANY)


def _mesh_pos():
    return lax.axis_index("x"), lax.axis_index("y"), lax.axis_index("c")


def _remote(src, dst, send_sems, recv_sems, k, dev):
    return pltpu.make_async_remote_copy(src_ref=src, dst_ref=dst, send_sem=send_sems.at[k], recv_sem=recv_sems.at[k],
                                        device_id=dev, device_id_type=MESH)


STREAM_CHUNK_BYTES = 2 * 1024 * 1024
STREAM_SLOTS = 4


def _chunk_rows(rows, cols, itemsize):
    assert rows % 16 == 0, rows
    t = 16
    while t * 2 <= rows and rows % (t * 2) == 0 and (t * 2) * cols * itemsize <= STREAM_CHUNK_BYTES:
        t *= 2
    return t


def _stream_chunks(n_chunks, src_at, dst_at, buf, load_sems, send_sems, recv_sem, sibling):
    def load(k, slot):
        return pltpu.make_async_copy(src_at(k), buf.at[slot], load_sems.at[slot])

    def send(k, slot):
        return pltpu.make_async_remote_copy(src_ref=buf.at[slot], dst_ref=dst_at(k), send_sem=send_sems.at[slot],
                                            recv_sem=recv_sem, device_id=sibling, device_id_type=MESH)

    def step(k, carry):
        slot = k % STREAM_SLOTS

        @pl.when(k >= STREAM_SLOTS)
        def _():
            send(k - STREAM_SLOTS, slot).wait_send()

        load(k, slot).start()

        @pl.when(k >= 1)
        def _():
            prev = (k - 1) % STREAM_SLOTS
            load(k - 1, prev).wait()
            send(k - 1, prev).start()

        return carry

    lax.fori_loop(0, n_chunks, step, 0)
    last = (n_chunks - 1) % STREAM_SLOTS
    load(n_chunks - 1, last).wait()
    send(n_chunks - 1, last).start()
    for k in range(max(0, n_chunks - STREAM_SLOTS), n_chunks):
        send(k, k % STREAM_SLOTS).wait_send()


_HBM = pl.BlockSpec(memory_space=pltpu.HBM)
_SEM = pl.BlockSpec(memory_space=pltpu.SEMAPHORE)
_DATAFLOW = pltpu.SideEffectType.DATAFLOW_SIDE_EFFECTING


def _chip_copies(kind, ins, lands, split, send_sems, recv_sems):
    x, y, c = _mesh_pos()
    me = 2 * x + y
    chips = [(1 - x, y), (x, 1 - y), (1 - x, 1 - y)]
    pairs = []
    for i in range(len(ins)):
        for j, chip in enumerate(chips):
            pj = 2 * chip[0] + chip[1]
            if kind == 'scatter':
                src, dst, got = ins[i].at[pj], lands[i].at[j], lands[i].at[j]
            elif split[i]:
                hr = ins[i].shape[0] // 2
                rows = pl.ds(c * hr, hr)
                src, dst, got = ins[i].at[rows], lands[i].at[me, rows], lands[i].at[pj, rows]
            else:
                src, dst, got = ins[i], lands[i].at[me], lands[i].at[pj]
            k = 3 * i + j
            pairs.append((_remote(src, dst, send_sems, recv_sems, k, (*chip, c)),
                          _remote(got, got, send_sems, recv_sems, k, (*chip, c))))
    return pairs


def _chip_exchange_start(kind, srcs, land_shapes, split, name, after):
    n = len(srcs)

    def body(*refs):
        ins, lands = refs[:n], refs[n:2 * n]
        send_sems, recv_sems = refs[2 * n + 1], refs[2 * n + 2]
        token = refs[-1]
        for send, _ in _chip_copies(kind, ins, lands, split, send_sems, recv_sems):
            send.start()
        token[...] = jnp.zeros_like(token)

    hbm = lambda t: pltpu.with_memory_space_constraint(t, pltpu.HBM)
    operands = [hbm(s) for s in srcs] + [hbm(lax.empty(shp, s.dtype)) for shp, s in zip(land_shapes, srcs)] + [after]
    out = pl.pallas_call(
        body, name=name, in_specs=[_HBM] * (2 * n) + [_ANY],
        out_specs=[_SEM, _SEM] + [_HBM] * (2 * n) + [pl.BlockSpec(memory_space=pltpu.VMEM)],
        out_shape=[pltpu.SemaphoreType.DMA((3 * n,)), pltpu.SemaphoreType.DMA((3 * n,))]
        + [pltpu.HBM(s.shape, s.dtype) for s in srcs] + [pltpu.HBM(shp, s.dtype) for shp, s in zip(land_shapes, srcs)]
        + [jax.ShapeDtypeStruct((8, 128), F32)],
        input_output_aliases={i: 2 + i for i in range(2 * n)},
        compiler_params=pltpu.CompilerParams(has_side_effects=_DATAFLOW),
    )(*operands)
    return out[0], out[1], out[2:2 + n], out[2 + n:2 + 2 * n], out[-1]


def _chip_exchange_wait(kind, started, split, name, after):
    send_sems, recv_sems, srcs, lands, _ = started
    n = len(srcs)
    after = list(after) if isinstance(after, (list, tuple)) else [after]

    def body(*refs):
        ins, land_refs = refs[:n], refs[n:2 * n]
        for send, arrived in _chip_copies(kind, ins, land_refs, split, refs[2 * n], refs[2 * n + 1]):
            send.wait_send()
            arrived.wait_recv()

    out = pl.pallas_call(
        body, name=name, in_specs=[_HBM] * (2 * n) + [_SEM, _SEM] + [_ANY] * len(after), out_specs=[_HBM] * (2 * n),
        out_shape=[pltpu.HBM(s.shape, s.dtype) for s in srcs] + [pltpu.HBM(l.shape, l.dtype) for l in lands],
        input_output_aliases={i: i for i in range(2 * n)},
        compiler_params=pltpu.CompilerParams(has_side_effects=_DATAFLOW),
    )(*srcs, *lands, send_sems, recv_sems, *after)
    return out[:n], out[n:]


def _pass_to_sibling(gathered, name, token=None):
    n = len(gathered)
    tr = [_chunk_rows(g.shape[1] // 2, g.shape[2], g.dtype.itemsize) for g in gathered]
    extra = [] if token is None else [token]

    def body(*refs):
        refs = refs[:n] + refs[n + len(extra):]
        outs = refs[n:2 * n]
        recv_sems, load_sems, send_sems = refs[2 * n:2 * n + 3]
        bufs = refs[2 * n + 3:]
        x, y, c = _mesh_pos()
        sibling = (x, y, 1 - c)
        chips = [(1 - x, y), (x, 1 - y), (1 - x, 1 - y)]
        for i in range(n):
            hr = outs[i].shape[1] // 2
            for j, chip in enumerate(chips):
                def rows_at(k, i=i, pj=2 * chip[0] + chip[1], hr=hr):
                    return outs[i].at[pj, pl.ds(c * hr + k * tr[i], tr[i])]

                _stream_chunks(hr // tr[i], rows_at, rows_at, bufs[i], load_sems, send_sems, recv_sems.at[3 * i + j],
                               sibling)
        for i in range(n):
            hr = outs[i].shape[1] // 2
            for j, chip in enumerate(chips):
                blk = outs[i].at[2 * chip[0] + chip[1], pl.ds((1 - c) * hr, hr)]
                pltpu.make_async_remote_copy(src_ref=blk, dst_ref=blk, send_sem=send_sems.at[0],
                                             recv_sem=recv_sems.at[3 * i + j], device_id=sibling,
                                             device_id_type=MESH).wait_recv()

    return pl.pallas_call(
        body, name=name, in_specs=[_ANY] * (n + len(extra)), out_specs=[_ANY] * n,
        out_shape=[jax.ShapeDtypeStruct(g.shape, g.dtype) for g in gathered],
        input_output_aliases={i: i for i in range(n)},
        scratch_shapes=[pltpu.SemaphoreType.DMA((3 * n,)), pltpu.SemaphoreType.DMA((STREAM_SLOTS,)),
                        pltpu.SemaphoreType.DMA((STREAM_SLOTS,))]
        + [pltpu.VMEM((STREAM_SLOTS, tr[i], g.shape[2]), g.dtype) for i, g in enumerate(gathered)],
        compiler_params=_cparams(),
    )(*gathered, *extra)


def _stream_to_sibling(srcs, halved, name):
    n = len(srcs)
    geo = []
    for s in srcs:
        rows = s.shape[1] // 2 if halved else s.shape[1]
        geo.append((s.shape[0], rows, s.shape[2], _chunk_rows(rows, s.shape[2], s.dtype.itemsize)))

    def body(*refs):
        ins, outs = refs[:n], refs[n:2 * n]
        recv_sems, load_sems, send_sems = refs[2 * n:2 * n + 3]
        bufs = refs[2 * n + 3:]
        x, y, c = _mesh_pos()
        sibling = (x, y, 1 - c)
        for i in range(n):
            slabs, rows, _, tr = geo[i]
            per_slab = rows // tr
            off = (1 - c) * rows if halved else 0

            def src_at(k, i=i, per_slab=per_slab, tr=tr, off=off):
                return ins[i].at[k // per_slab, pl.ds(off + (k % per_slab) * tr, tr)]

            def dst_at(k, i=i, per_slab=per_slab, tr=tr):
                return outs[i].at[k // per_slab, pl.ds((k % per_slab) * tr, tr)]

            _stream_chunks(slabs * per_slab, src_at, dst_at, bufs[i], load_sems, send_sems, recv_sems.at[i], sibling)
        for i in range(n):
            pltpu.make_async_remote_copy(src_ref=outs[i], dst_ref=outs[i], send_sem=send_sems.at[0],
                                         recv_sem=recv_sems.at[i], device_id=sibling, device_id_type=MESH).wait_recv()

    return pl.pallas_call(
        body, name=name, in_specs=[_ANY] * n, out_specs=[_ANY] * n,
        out_shape=[jax.ShapeDtypeStruct((g[0], g[1], g[2]), s.dtype) for g, s in zip(geo, srcs)],
        scratch_shapes=[pltpu.SemaphoreType.DMA((n,)), pltpu.SemaphoreType.DMA((STREAM_SLOTS,)),
                        pltpu.SemaphoreType.DMA((STREAM_SLOTS,))]
        + [pltpu.VMEM((STREAM_SLOTS, g[3], g[2]), s.dtype) for g, s in zip(geo, srcs)],
        compiler_params=_cparams(),
    )(*srcs)


def _reduce_scatter_begin(grads, tag, core):
    landed = _stream_to_sibling(grads, True, f"rs_swap_{tag}")
    parts = [_sum_keep_and_landed(g, l, core, f"rs_add2_{tag}_{i}") for i, (g, l) in enumerate(zip(grads, landed))]
    return _chip_exchange_start('scatter', parts, [(3,) + p.shape[1:] for p in parts], None, f"rs_scatter_start_{tag}",
                                core)


def _reduce_scatter_finish(started, tag, chip, after):
    parts, landed = _chip_exchange_wait('scatter', started, None, f"rs_scatter_wait_{tag}", after)
    mine = [_sum_chip_parts(p, l, chip, f"rs_add4_{tag}_{i}") for i, (p, l) in enumerate(zip(parts, landed))]
    theirs = _stream_to_sibling([m[None] for m in mine], False, f"rs_join_{tag}")
    return [(m, t[0]) for m, t in zip(mine, theirs)]


def _pad_cols(a, width=128):
    return jnp.pad(a, ((0, 0), (0, width - a.shape[1])))


def _mixer_forward(kind, hn, w, tables):
    if kind == 0:
        proj = _matmul(hn, w['ret_w_in'], name="ret_proj", b_sharded=True)
        og, states = _ret_fwd(proj, w['ret_gn_gain'], tables)
        return og, (proj, states)
    if kind == 1:
        proj = _matmul(hn, w['gdn_w_main'], name="gdn_proj")
        ba = _matmul(hn, w['gdn_w_small'], name="gdn_proj_ba")
        og, states = _gdn_fwd(proj, ba, w['gdn_conv_w'], w['gdn_a_log'], w['gdn_dt_bias'], w['gdn_norm_gain'])
        return og, (proj, ba, states)
    if kind == 2:
        proj = _matmul(hn, w['gla_w_main'], name="gla_proj")
        glow = _matmul(hn, w['gla_w_small'], name="gla_proj_gate")
        og, states = _gla_fwd(proj, glow, w['gla_w_gate_up'], w['gla_gate_bias'], w['gla_norm_gain'])
        return og, (proj, glow, states)
    proj = _matmul(hn, w['lru_w_in'], name="lru_proj", b_sharded=True)
    og = _lru_fwd(proj, w['lru_conv_w'], w['lru_conv_b'], w['lru_lambda'], w['lru_w_rgate'], w['lru_b_rgate'],
                  w['lru_w_igate'], w['lru_b_igate'])
    return og, (proj,)


def _mixer_backward(kind, hn, w, tables, saved, d_og, grads):
    d = hn.shape[1]
    if kind == 0:
        proj, states = saved
        d_proj, grads['ret_gn_gain'] = _ret_bwd(proj, w['ret_gn_gain'], tables, states, d_og)
        grads['ret_w_in'] = _matmul(hn, d_proj, name="ret_dw_in", ta=True, out_dtype=BF, o_sharded=True)
        return _matmul(d_proj, w['ret_w_in'], name="ret_dhn", tb=True, b_sharded=True)
    if kind == 1:
        proj, ba, states = saved
        d_proj, d_ba, grads['gdn_conv_w'], grads['gdn_a_log'], grads['gdn_dt_bias'], grads['gdn_norm_gain'] = _gdn_bwd(
            proj, ba, w['gdn_conv_w'], w['gdn_a_log'], w['gdn_dt_bias'], w['gdn_norm_gain'], states, d_og)
        d_ba = d_ba.astype(BF)
        dw_main = _matmul(hn, d_proj, name="gdn_dw_main", ta=True, out_dtype=BF)
        dw_small = _matmul(hn, d_ba, name="gdn_dw_small", ta=True, out_dtype=BF)
        dw = jnp.concatenate([dw_main, dw_small[:, :2 * GDN_HEADS]], axis=1)
        grads['gdn_w_in'] = dw.reshape(d, N_CHIPS, dw.shape[1] // N_CHIPS).transpose(1, 0, 2)
        d_hn = _matmul(d_proj, w['gdn_w_main'], name="gdn_dhn_main", tb=True)
        return _matmul(d_ba, w['gdn_w_small'], name="gdn_dhn_small", tb=True, epilogue='add', extra=d_hn)
    if kind == 2:
        proj, glow, states = saved
        d_proj, d_glow4, d_wgu, grads['gla_gate_bias'], grads['gla_norm_gain'] = _gla_bwd(
            proj, glow, w['gla_w_gate_up'], w['gla_gate_bias'], w['gla_norm_gain'], states, d_og)
        grads['gla_w_gate_up'] = d_wgu[:GLA_RANK]
        dw_main = _matmul(hn, d_proj, name="gla_dw_main", ta=True, out_dtype=BF)
        dw_small4 = _matmul(hn, d_glow4, name="gla_dw_small", ta=True, out_dtype=F32)
        dw_small = dw_small4.reshape(d, GLA_HEADS, 128)[:, :, :GLA_RANK].sum(axis=1).astype(BF)
        dw = jnp.concatenate([dw_main, dw_small], axis=1)
        grads['gla_w_in'] = dw.reshape(d, N_CHIPS, dw.shape[1] // N_CHIPS).transpose(1, 0, 2)
        d_hn = _matmul(d_proj, w['gla_w_main'], name="gla_dhn_main", tb=True)
        w_small4 = jnp.tile(w['gla_w_small'], (1, GLA_HEADS))
        return _matmul(d_glow4, w_small4, name="gla_dhn_small", tb=True, epilogue='add', extra=d_hn)
    (proj,) = saved
    (d_proj, grads['lru_conv_w'], grads['lru_conv_b'], grads['lru_lambda'], grads['lru_w_rgate'], grads['lru_b_rgate'],
     grads['lru_w_igate'], grads['lru_b_igate']) = _lru_bwd(
        proj, w['lru_conv_w'], w['lru_conv_b'], w['lru_lambda'], w['lru_w_rgate'], w['lru_b_rgate'], w['lru_w_igate'],
        w['lru_b_igate'], d_og)
    grads['lru_w_in'] = _matmul(hn, d_proj, name="lru_dw_in", ta=True, out_dtype=BF, o_sharded=True)
    return _matmul(d_proj, w['lru_w_in'], name="lru_dhn", tb=True, b_sharded=True)


_W_OUT = ('ret_w_out', 'gdn_w_out', 'gla_w_out', 'lru_w_out')
_W_IN = ('ret_w_in', 'gdn_w_in', 'gla_w_in', 'lru_w_in')


def _layer_forward(layer, x, w, tables, token=None):
    hn = _rmsnorm_fwd(x, w['norm1'][layer], f"norm1_fwd_{layer}", token)
    og, mixer_saved = _mixer_forward(layer, hn, w, tables)
    x1 = _matmul(og, w[_W_OUT[layer]], name=f"mixer_out_{layer}", epilogue='add', extra=x)
    hn2 = _rmsnorm_fwd(x1, w['norm2'][layer], f"norm2_fwd_{layer}")
    act = _matmul(hn2, w['mlp_w_up'][layer], name="mlp_up", b_sharded=True, epilogue='relu2', out_dtype=BF)
    x2 = _matmul(act, w['mlp_w_down'][layer], name="mlp_down", epilogue='add', extra=x1)
    return x2, (x, hn, mixer_saved, og, x1, hn2, act)


def _layer_backward(layer, dx2, dx2_bf, w, tables, saved, token=None):
    x, hn, mixer_saved, og, x1, hn2, act = saved
    d = x.shape[1]
    grads = {}
    d_up = _matmul(dx2_bf, w['mlp_w_down'][layer], name="mlp_d_up", tb=True, epilogue='dact', extra=act,
                   out_dtype=BF, token=token)
    dw_down = _matmul(act, dx2_bf, name="mlp_dw_down", ta=True, out_dtype=BF)
    grads['mlp_w_down'] = dw_down.reshape(N_CHIPS, dw_down.shape[0] // N_CHIPS, d)
    grads['mlp_w_up'] = _matmul(hn2, d_up, name="mlp_dw_up", ta=True, out_dtype=BF, o_sharded=True)
    d_hn2 = _matmul(d_up, w['mlp_w_up'][layer], name="mlp_d_hn", tb=True, b_sharded=True)
    dx1, dx1_bf, grads['norm2'] = _rmsnorm_bwd(x1, w['norm2'][layer], d_hn2, dx2, f"norm2_bwd_{layer}")
    w_out = w[_W_OUT[layer]]
    d_og = _matmul(dx1_bf, w_out, name=f"mixer_d_og_{layer}", tb=True, out_dtype=BF)
    dw_out = _matmul(og, dx1_bf, name=f"mixer_dw_out_{layer}", ta=True, out_dtype=BF)
    grads[_W_OUT[layer]] = dw_out.reshape(N_CHIPS, dw_out.shape[0] // N_CHIPS, d)
    d_hn = _mixer_backward(layer, hn, w, tables, mixer_saved, d_og, grads)
    dx, dx_bf, grads['norm1'] = _rmsnorm_bwd(x, w['norm1'][layer], d_hn, dx1, f"norm1_bwd_{layer}")
    return dx, dx_bf, grads


PACK_ROWS = 256


def _pack(arrays):
    flat = []
    for a in arrays:
        v = a.astype(F32).reshape(-1)
        v = jnp.pad(v, (0, (-v.shape[0]) % 128))
        flat.append(v.reshape(-1, 128))
    buf = jnp.concatenate(flat, axis=0)
    return jnp.pad(buf, ((0, (-buf.shape[0]) % PACK_ROWS), (0, 0)))


def _unpack(buf, shapes):
    lead = buf.shape[:-2]
    out, off = [], 0
    for shp in shapes:
        n = math.prod(shp)
        rows = -(-n // 128)
        piece = buf[..., off:off + rows, :].reshape(lead + (rows * 128,))[..., :n]
        out.append(piece.reshape(lead + tuple(shp)))
        off += rows
    return out


_WEIGHTS = ('norm1', 'norm2', 'final_norm', 'ret_w_in', 'ret_gn_gain', 'ret_w_out', 'gdn_w_in', 'gdn_conv_w',
            'gdn_a_log', 'gdn_dt_bias', 'gdn_norm_gain', 'gdn_w_out', 'gla_w_in', 'gla_w_gate_up', 'gla_gate_bias',
            'gla_norm_gain', 'gla_w_out', 'lru_w_in', 'lru_conv_w', 'lru_conv_b', 'lru_w_rgate', 'lru_b_rgate',
            'lru_w_igate', 'lru_b_igate', 'lru_lambda', 'lru_w_out', 'mlp_w_up', 'mlp_w_down')
_FWD_PARAMS = ('x',) + _WEIGHTS
_BIG = ('ret_w_in', 'ret_w_out', 'gdn_w_in', 'gdn_w_out', 'gla_w_in', 'gla_w_out', 'lru_w_in', 'lru_w_out',
        'mlp_w_up', 'mlp_w_down')
_SMALL = tuple(n for n in _WEIGHTS if n not in _BIG)
_SMALL_SHARDED = ('ret_gn_gain', 'gdn_conv_w', 'gla_w_gate_up', 'gla_gate_bias', 'gla_norm_gain', 'lru_conv_w',
                  'lru_conv_b', 'lru_lambda')


def kernel(*args):
    names = _FWD_PARAMS + ('loss_target',) + tuple('m_' + n for n in _WEIGHTS) + tuple('v_' + n for n in _WEIGHTS)
    assert len(args) == len(names)
    a = dict(zip(names, args))
    x = a['x'][0]
    target = a['loss_target'][0]
    s, d = x.shape
    chip = 2 * lax.axis_index("x") + lax.axis_index("y")

    small_local = [a[n][0] if a[n].ndim == 3 else a[n] for n in _SMALL_SHARDED]
    small_pack = _pack(small_local)
    core_arr = lax.axis_index("c").astype(jnp.int32).reshape(1)
    chip_arr = chip.astype(jnp.int32).reshape(1)

    def whole_cols(g):
        return g.transpose(1, 0, 2).reshape(g.shape[1], N_CHIPS * g.shape[2])

    def whole_rows(g):
        return g.reshape(N_CHIPS * g.shape[1], g.shape[2])

    w = {'mlp_w_up': [None] * 4, 'mlp_w_down': [None] * 4}
    for n in _SMALL:
        if n not in _SMALL_SHARDED:
            w[n] = a[n][0] if n.startswith('lru_') else a[n]
    w['gdn_a_log'], w['gdn_dt_bias'] = _pad_cols(w['gdn_a_log']), _pad_cols(w['gdn_dt_bias'])
    w['lru_b_rgate'] = w['lru_b_rgate'].reshape(LRU_BLOCKS, 1, -1)
    w['lru_b_igate'] = w['lru_b_igate'].reshape(LRU_BLOCKS, 1, -1)

    def gather_start(layer, after):
        ops = [a[_W_IN[layer]][0].astype(BF), a[_W_OUT[layer]][0].astype(BF), a['mlp_w_up'][layer].astype(BF),
               a['mlp_w_down'][layer].astype(BF)] + ([small_pack] if layer == 0 else [])
        split = [True] * 4 + ([False] if layer == 0 else [])
        return _chip_exchange_start('gather', ops, [(N_CHIPS,) + o.shape for o in ops], split,
                                    f"gather_start_{layer}", after), split

    def gather_finish(layer, started, after):
        started, split = started
        own, lands = _chip_exchange_wait('gather', started, split, f"gather_wait_{layer}", after)
        nxt = gather_start(layer + 1, own[0]) if layer < 3 else None
        lands = list(_pass_to_sibling(lands[:4], f"gather_pass_{layer}", None if nxt is None else nxt[0][4])) \
            + list(lands[4:])
        g_in, g_out, g_up, g_down = [lax.dynamic_update_slice_in_dim(l, o[None], chip, axis=0)
                                     for l, o in zip(lands[:4], own[:4])]
        w['mlp_w_up'][layer], w['mlp_w_down'][layer], w[_W_OUT[layer]] = g_up, whole_rows(g_down), whole_rows(g_out)
        if layer == 0:
            small_all = lax.dynamic_update_slice_in_dim(lands[4], own[4][None], chip, axis=0)
            for n, piece in zip(_SMALL_SHARDED, _unpack(small_all, [p.shape for p in small_local])):
                w[n] = whole_cols(piece)
            w['gla_w_gate_up'] = jnp.pad(w['gla_w_gate_up'], ((0, 128 - GLA_RANK), (0, 0)))
        if layer in (0, 3):
            w[_W_IN[layer]] = g_in
        else:
            name, tail = ('gdn', 2 * GDN_HEADS) if layer == 1 else ('gla', GLA_RANK)
            full = whole_cols(g_in)
            w[name + '_w_main'] = full[:, :full.shape[1] - tail]
            w[name + '_w_small'] = _pad_cols(full[:, full.shape[1] - tail:])
        return nxt

    tables = _ret_tables(s, d // RET_HEADS)
    saved = []
    h = x
    nxt = gather_finish(0, gather_start(0, x), x)
    for layer in range(4):
        h, sv = _layer_forward(layer, h, w, tables, None if nxt is None else nxt[0][4])
        saved.append(sv)
        if nxt is not None:
            nxt = gather_finish(layer + 1, nxt, h)
    loss_part, dh, dh_bf, g_final = _loss_head(h, w['final_norm'], target)
    loss = lax.psum(loss_part, ("x", "y", "c"))

    grad, delta, new_m, new_v = {}, {}, {}, {}
    small_grads = {'final_norm': g_final}
    norm_grads = {'norm1': [None] * 4, 'norm2': [None] * 4}
    mlp_upd = {'mlp_w_up': None, 'mlp_w_down': None}

    def update_layer(layer, started, after):
        red = _reduce_scatter_finish(started, str(layer), chip_arr, after)
        for n, (mine, theirs) in ((_W_IN[layer], red[0]), (_W_OUT[layer], red[1])):
            grad[n], delta[n], new_m[n], new_v[n] = _adamw_halves(a[n], a['m_' + n], a['v_' + n], 0, mine, theirs,
                                                                  core_arr, f"adamw_{n}")
        for n, (mine, theirs) in (('mlp_w_up', red[2]), ('mlp_w_down', red[3])):
            mlp_upd[n] = _adamw_halves(a[n], a['m_' + n], a['v_' + n], layer, mine, theirs, core_arr,
                                       f"adamw_{n}_{layer}", carried=mlp_upd[n])

    scatters = {}
    token = None
    for layer in reversed(range(4)):
        dh, dh_bf, g = _layer_backward(layer, dh, dh_bf, w, tables, saved[layer], token)
        saved[layer] = None
        scatters[layer] = _reduce_scatter_begin([g[_W_IN[layer]], g[_W_OUT[layer]], g['mlp_w_up'], g['mlp_w_down']],
                                                str(layer), core_arr)
        token = scatters[layer][4]
        norm_grads['norm1'][layer], norm_grads['norm2'][layer] = g['norm1'], g['norm2']
        for n in _SMALL:
            if n in g:
                small_grads[n] = g[n]
    small_grads['norm1'] = jnp.stack(norm_grads['norm1'])
    small_grads['norm2'] = jnp.stack(norm_grads['norm2'])

    full_shapes = [small_grads[n].shape for n in _SMALL]
    small_part = _pack([small_grads[n] for n in _SMALL])
    sibling_part = _stream_to_sibling([small_part[None]], False, "small_swap")[0][0]
    chip_part = _sum_slabs(jnp.stack([small_part, sibling_part]), F32, "sum_small_cores")
    small_started = _chip_exchange_start('gather', [chip_part], [(N_CHIPS,) + chip_part.shape], [False],
                                         "small_gather_start", core_arr)
    after = [token]
    for layer in (3, 2, 1):
        update_layer(layer, scatters[layer], after)
        after = [new_v[_W_IN[layer]], new_v[_W_OUT[layer]], mlp_upd['mlp_w_up'][3], mlp_upd['mlp_w_down'][3]]
    own, lands = _chip_exchange_wait('gather', small_started, [False], "small_gather_wait", after)
    total = _sum_slabs(lax.dynamic_update_slice_in_dim(lands[0], own[0][None], chip, axis=0), F32, "sum_small_chips")
    local_g = {}
    for n, full in zip(_SMALL, _unpack(total, full_shapes)):
        shp = a[n].shape
        if n in _SMALL_SHARDED:
            full = full.reshape(full.shape[0], -1)
            cq = shp[-1]
            full = lax.dynamic_slice_in_dim(full, chip * cq, cq, axis=1)
        elif n in ('gdn_a_log', 'gdn_dt_bias'):
            full = full[:, :shp[-1]]
        local_g[n] = full.reshape(shp)
    shapes = [a[n].shape for n in _SMALL]
    packed = [_pack([src[n] for n in _SMALL]) for src in
              (a, local_g, {n: a['m_' + n] for n in _SMALL}, {n: a['v_' + n] for n in _SMALL})]
    upd = _adamw(*packed, "adamw_small")
    for n, gr, dl, nm, nv in zip(_SMALL, _unpack(packed[1], shapes), *[_unpack(u, shapes) for u in upd]):
        grad[n], delta[n], new_m[n], new_v[n] = gr, dl, nm, nv
    update_layer(0, scatters[0], upd[0])
    for n in ('mlp_w_up', 'mlp_w_down'):
        grad[n], delta[n], new_m[n], new_v[n] = mlp_upd[n]

    out = [loss, dh.reshape(a['x'].shape)]
    for group in (grad, delta, new_m, new_v):
        out += [group[n].reshape(a[n].shape) for n in _WEIGHTS]
    return tuple(out)
```

```python
import functools
import math

import jax
import jax.numpy as jnp
from jax import lax
from jax.experimental import pallas as pl
from jax.experimental.pallas import tpu as pltpu

F32 = jnp.float32
BF = jnp.bfloat16
MESH = pl.DeviceIdType.MESH

NORM_EPS = 1e-6
CHUNK = 64
RET_HEADS = 8
GDN_HEADS = 16
GLA_HEADS = 4
GLA_RANK = 16
GLA_TAU = 16.0
LRU_BLOCKS = 16
LRU_C = 8.0
CONV_WIDTH = 4
ROPE_BASE = 10000.0
N_CHIPS = 4
N_DEV = 8

ADAM_LR = 0.001
ADAM_B1 = 0.9
ADAM_B2 = 0.999
ADAM_EPS = 1e-08
ADAM_WD = 0.01
ADAM_STEP = 10

VMEM_LIMIT_BYTES = 56 * 1024 * 1024
TOKEN_BLOCK = 512
ROW_BLOCK = 256
GDN_GROUP = 16


def _cparams(**kw):
    return pltpu.CompilerParams(vmem_limit_bytes=VMEM_LIMIT_BYTES, **kw)


def _bf16_parts(x, n):
    parts = []
    for _ in range(n - 1):
        p = x.astype(BF)
        parts.append(p)
        x = x - p.astype(F32)
    return parts + [x.astype(BF)]


def _raw_mm(a, b, ta, tb, hi):
    nb = a.ndim - 2
    batch = tuple(range(nb))
    dims = (((nb + (0 if ta else 1),), (nb + (1 if tb else 0),)), (batch, batch))

    def dot(p, q):
        return lax.dot_general(p, q, dims, preferred_element_type=F32)

    if not hi:
        return dot(a.astype(BF), b.astype(BF))
    if hi == 'l':
        ae = a.astype(BF)
        b0, b1, b2 = _bf16_parts(b.astype(F32), 3)
        return dot(ae, b0) + (dot(ae, b1) + dot(ae, b2))
    if hi == 'r':
        be = b.astype(BF)
        a0, a1, a2 = _bf16_parts(a.astype(F32), 3)
        return dot(a0, be) + (dot(a1, be) + dot(a2, be))
    a0, a1 = _bf16_parts(a.astype(F32), 2)
    b0, b1 = _bf16_parts(b.astype(F32), 2)
    return dot(a0, b0) + (dot(a0, b1) + dot(a1, b0))


@functools.partial(jax.custom_vjp, nondiff_argnums=(2, 3, 4))
def _mm_vjp(a, b, ta, tb, hi):
    return _raw_mm(a, b, ta, tb, hi)


def _mm_vjp_fwd(a, b, ta, tb, hi):
    return _raw_mm(a, b, ta, tb, hi), (a, b)


def _mm_vjp_bwd(ta, tb, hi, res, g):
    a, b = res
    if ta:
        da = _raw_mm(b, g, tb, True, 'l' if hi == 'r' else bool(hi))
    else:
        da = _raw_mm(g, b, False, not tb, 'r' if hi == 'r' else bool(hi))
    if tb:
        db = _raw_mm(g, a, True, ta, 'r' if hi == 'l' else bool(hi))
    else:
        db = _raw_mm(a, g, not ta, False, 'l' if hi == 'l' else bool(hi))
    return da, db


_mm_vjp.defvjp(_mm_vjp_fwd, _mm_vjp_bwd)


def _mm_diff(a, b, ta=False, tb=False, hi=False):
    return _mm_vjp(a, b, ta, tb, hi)


def _mm_plain(a, b, ta=False, tb=False, hi=False):
    return _raw_mm(a, b, ta, tb, hi)


def _shift_rows(x, k, up):
    if k == 0:
        return x
    n = x.shape[0]
    rows = lax.broadcasted_iota(jnp.int32, x.shape, 0)
    if up:
        return jnp.where(rows < n - k, pltpu.roll(x, n - k, 0), 0.0)
    return jnp.where(rows >= k, pltpu.roll(x, k, 0), 0.0)


@functools.partial(jax.custom_vjp, nondiff_argnums=(1, 2))
def _shift_vjp(x, k, up):
    return _shift_rows(x, k, up)


def _shift_vjp_fwd(x, k, up):
    return _shift_rows(x, k, up), None


def _shift_vjp_bwd(k, up, _, g):
    return (_shift_rows(g, k, not up),)


_shift_vjp.defvjp(_shift_vjp_fwd, _shift_vjp_bwd)


def _sigmoid(x):
    return 1.0 / (1.0 + jnp.exp(-x))


def _silu(x):
    return x * _sigmoid(x)


def _softplus(x):
    return jnp.maximum(x, 0.0) + jnp.log(1.0 + jnp.exp(-jnp.abs(x)))


def _gelu_tanh(x):
    return 0.5 * x * (1.0 + jnp.tanh(math.sqrt(2.0 / math.pi) * (x + 0.044715 * (x * x * x))))


def _expm1(x):
    series = x * (1.0 + x * (0.5 + x * (1.0 / 6.0 + x * (1.0 / 24.0))))
    return jnp.where(jnp.abs(x) < 0.03, series, jnp.exp(x) - 1.0)


def _rmsnorm(x, g):
    return x * lax.rsqrt(jnp.mean(x * x, axis=-1, keepdims=True) + NORM_EPS) * g


def _head_norm(o, gain, center):
    if center:
        o = o - jnp.mean(o, axis=-1, keepdims=True)
    return o * lax.rsqrt(jnp.mean(o * o, axis=-1, keepdims=True) + NORM_EPS) * gain


def _l2norm(x):
    return x * lax.rsqrt(jnp.sum(x * x, axis=-1, keepdims=True) + NORM_EPS)


def _iota2(shape, dim):
    return lax.broadcasted_iota(jnp.int32, shape, dim)


def _tri_ones(n, upper=False):
    i, j = _iota2((n, n), 0), _iota2((n, n), 1)
    return jnp.where((i <= j) if upper else (j <= i), 1.0, 0.0).astype(F32)


def _linscan(a, u, rev):
    n = a.shape[0]
    rows = _iota2(a.shape, 0)
    d = 1
    while d < n:
        if rev:
            valid = rows < n - d
            a_s, u_s = pltpu.roll(a, n - d, 0), pltpu.roll(u, n - d, 0)
        else:
            valid = rows >= d
            a_s, u_s = pltpu.roll(a, d, 0), pltpu.roll(u, d, 0)
        u = a * jnp.where(valid, u_s, 0.0) + u
        a = a * jnp.where(valid, a_s, 1.0)
        d *= 2
    return u


def _ret_group(mm, q, k, v, g, gain, state, cos, sin, dintra, qdec, kdec, cdec):
    n, _, dk = q.shape
    half = dk // 2

    def rot(t):
        t1, t2 = t[..., :half], t[..., half:]
        return jnp.concatenate([t1 * cos - t2 * sin, t1 * sin + t2 * cos], axis=-1)

    qr = rot(q)
    kr = rot(k) * (dk ** -0.5)
    o_intra = mm(mm(qr, kr, tb=True) * dintra, v)
    kv = mm(kr * kdec, v, ta=True)
    q_in = qr * qdec
    ys = []
    for i in range(n):
        o = o_intra[i] + mm(q_in[i], state)
        state = state * cdec + kv[i]
        ys.append(_head_norm(o, gain, True) * _silu(g[i]))
    return ys, state


def _gla_group(mm, q, k, v, r, glow, wgu, bias, gain, state_t):
    n, c, dk = q.shape
    logit = mm(glow.reshape(n * c, glow.shape[-1]), wgu).reshape(n, c, dk) + bias
    la = -_softplus(-logit) * (1.0 / GLA_TAU)
    cum = mm(jnp.broadcast_to(_tri_ones(c), (n, c, c)), la, hi='l')
    rows = lax.broadcasted_iota(jnp.int32, la.shape, 1)
    ref = jnp.sum(jnp.where(rows < c // 2, la, 0.0), axis=1, keepdims=True)
    tot = jnp.sum(la, axis=1, keepdims=True)
    fwd, bwd = jnp.exp(cum - ref), jnp.exp(ref - cum)
    qs = q * (dk ** -0.5)
    s_lo = mm(qs * fwd, k * bwd, tb=True)
    s_up = mm(qs * bwd, k * fwd, tb=True)
    i, j = _iota2((c, c), 0), _iota2((c, c), 1)
    o_intra = mm(jnp.where(i >= j, s_lo, s_up), v)
    q_in = qs * jnp.exp(cum)
    kv_t = mm(v, k * jnp.exp(tot - cum), ta=True)
    dec = jnp.exp(tot)
    ys = []
    for m in range(n):
        o = o_intra[m] + mm(q_in[m], state_t, tb=True)
        state_t = state_t * dec[m] + kv_t[m]
        ys.append(_head_norm(o, gain, False) * _silu(r[m]))
    return ys, state_t


def _gdn_group(mm, q, k, v, z, ba, a_log, dt_bias, gain, state, head):
    g, c, dk = q.shape
    lanes = lax.broadcasted_iota(jnp.int32, (1, 1, ba.shape[-1]), 2)
    oh_b = jnp.where(lanes == head, 1.0, 0.0).astype(F32)
    oh_a = jnp.where(lanes == head + GDN_HEADS, 1.0, 0.0).astype(F32)
    beta = _sigmoid(jnp.sum(ba * oh_b, axis=-1, keepdims=True))
    a_logit = jnp.sum(ba * oh_a, axis=-1, keepdims=True)
    a_h = jnp.sum(a_log * oh_b[0], axis=-1, keepdims=True)
    dt_h = jnp.sum(dt_bias * oh_b[0], axis=-1, keepdims=True)
    la = -jnp.exp(a_h) * _softplus(a_logit + dt_h)
    q = _l2norm(q) * (dk ** -0.5)
    k = _l2norm(k)
    tri = jnp.broadcast_to(_tri_ones(c), (g, c, c))
    tri_up = jnp.broadcast_to(_tri_ones(c, upper=True), (g, c, c))
    cum_k = mm(tri, la * jnp.ones((g, c, dk), F32), hi='l')
    la_sq = la * jnp.ones((g, c, c), F32)
    cum_i = mm(tri, la_sq, hi='l')
    cum_j = mm(la_sq, tri_up, ta=True, hi='r')
    i, j = _iota2((c, c), 0), _iota2((c, c), 1)
    strict = i > j
    rel = jnp.where(strict, jnp.exp(jnp.where(strict, cum_i - cum_j, 0.0)), 0.0)
    a_mat = beta * rel * mm(k, k, tb=True)
    inv = jnp.where(i == j, 1.0, 0.0).astype(F32) - a_mat
    power = a_mat
    for _ in range(int(math.log2(c)) - 1):
        power = mm(power, power, hi=True)
        inv = inv + mm(inv, power, hi=True)
    tot = jnp.sum(la, axis=1, keepdims=True)
    u = mm(inv, beta * v, hi=True)
    w = mm(inv, (beta * jnp.exp(cum_k)) * k, hi=True)
    k_end = k * jnp.exp(tot - cum_k)
    di, dj = _iota2((dk, dk), 0), _iota2((dk, dk), 1)
    trans = jnp.exp(tot) * jnp.where(di == dj, 1.0, 0.0).astype(F32) - mm(k_end, w, ta=True)
    inject = mm(k_end, u, ta=True)
    ys = []
    for n in range(g):
        state = mm(trans[n], state) + inject[n]
        ys.append(_head_norm(mm(q[n], state), gain, False) * _silu(z[n]))
    return ys, state


def _conv_taps(shift, x, taps):
    out = None
    for tap, w in enumerate(taps):
        term = shift(x, CONV_WIDTH - 1 - tap, False) * w
        out = term if out is None else out + term
    return out


def _lru_pre(mm, shift, xb, yb, taps, cb, lam, wr, br, wi, bi):
    xb = _conv_taps(shift, xb, taps) + cb
    r = _sigmoid(mm(xb, wr) + br)
    i = _sigmoid(mm(xb, wi) + bi)
    log_a = (-LRU_C) * _softplus(-lam) * r
    a = jnp.exp(log_a)
    u = jnp.sqrt(-_expm1(2.0 * log_a)) * (i * xb)
    return a, u, _gelu_tanh(yb)


_TOKEN_SPEC = pl.BlockSpec((8, 128), lambda *_: (0, 0))


MATMUL_VMEM_BUDGET = 40 * 1024 * 1024
_TILE_SIZES = (3072, 2048, 1536, 1024, 768, 512, 384, 256, 128)


def _tile_choices(extent):
    return [t for t in _TILE_SIZES if t <= extent and extent % t == 0] or [extent]


def _matmul_tiles(m, n, kdim, n_unit, k_unit, a_item, b_item, o_item, e_item):
    best = None
    for bm in _tile_choices(m):
        for bn in _tile_choices(n_unit):
            for bk in _tile_choices(k_unit):
                nk = kdim // bk
                vmem = 2 * (bm * bk * a_item + bk * bn * b_item + bm * bn * (o_item + e_item)) + bm * bn * 4 * (2 if nk > 1 else 1)
                if vmem > MATMUL_VMEM_BUDGET:
                    continue
                steps = (m // bm) * (n // bn) * nk
                traffic = m * kdim * a_item * (1 if nk == 1 else n // bn) + kdim * n * b_item * (m // bm)
                if best is None or (steps, traffic) < best[0]:
                    best = ((steps, traffic), (bm, bn, bk))
    return best[1]


def _matmul(a, b, *, name, ta=False, tb=False, out_dtype=F32, b_sharded=False, o_sharded=False,
            epilogue=None, extra=None, token=None):
    m, kdim = (a.shape[1], a.shape[0]) if ta else a.shape
    if b_sharded:
        nq = b.shape[2]
        n = b.shape[1] if tb else N_CHIPS * nq
        assert kdim == (N_CHIPS * nq if tb else b.shape[1])
    else:
        n = b.shape[0] if tb else b.shape[1]
        assert kdim == (b.shape[1] if tb else b.shape[0])
    n_unit = b.shape[2] if (b_sharded and not tb) else (n // N_CHIPS if o_sharded else n)
    k_unit = b.shape[2] if (b_sharded and tb) else kdim
    bm, bn, bk = _matmul_tiles(m, n, kdim, n_unit, k_unit, a.dtype.itemsize, b.dtype.itemsize,
                               jnp.dtype(out_dtype).itemsize, 0 if extra is None else extra.dtype.itemsize)
    nk = kdim // bk
    grid = (m // bm, n // bn, nk)

    a_spec = pl.BlockSpec((bk, bm), lambda i, j, k: (k, i)) if ta else pl.BlockSpec((bm, bk), lambda i, j, k: (i, k))
    if b_sharded and not tb:
        per = b.shape[2] // bn
        b_spec = pl.BlockSpec((None, bk, bn), lambda i, j, k: (j // per, k, j % per))
    elif b_sharded:
        per = b.shape[2] // bk
        b_spec = pl.BlockSpec((None, bn, bk), lambda i, j, k: (k // per, j, k % per))
    elif tb:
        b_spec = pl.BlockSpec((bn, bk), lambda i, j, k: (j, k))
    else:
        b_spec = pl.BlockSpec((bk, bn), lambda i, j, k: (k, j))
    if o_sharded:
        per_o = (n // N_CHIPS) // bn
        o_spec = pl.BlockSpec((None, bm, bn), lambda i, j, k: (j // per_o, i, j % per_o))
        out_shape = jax.ShapeDtypeStruct((N_CHIPS, m, n // N_CHIPS), out_dtype)
    else:
        o_spec = pl.BlockSpec((bm, bn), lambda i, j, k: (i, j))
        out_shape = jax.ShapeDtypeStruct((m, n), out_dtype)
    in_specs = [a_spec, b_spec]
    operands = [a, b]
    if extra is not None:
        in_specs.append(pl.BlockSpec((bm, bn), lambda i, j, k: (i, j)))
        operands.append(extra)
    if token is not None:
        in_specs.append(_TOKEN_SPEC)
        operands.append(token)

    def finish(acc, e_ref, o_ref):
        if epilogue == 'add':
            acc = acc + e_ref[...].astype(F32)
        elif epilogue == 'relu2':
            acc = jnp.square(jnp.maximum(acc, 0.0))
        elif epilogue == 'dact':
            acc = acc * (2.0 * jnp.sqrt(e_ref[...].astype(F32)))
        o_ref[...] = acc.astype(o_ref.dtype)

    def body_one_step(*refs):
        finish(_raw_mm(refs[0][...], refs[1][...], ta, tb, False), refs[2] if extra is not None else None, refs[-1])

    def body(*refs):
        a_ref, b_ref = refs[0], refs[1]
        e_ref = refs[2] if extra is not None else None
        o_ref, acc_ref = refs[-2], refs[-1]
        k = pl.program_id(2)

        @pl.when(k == 0)
        def _():
            acc_ref[...] = jnp.zeros_like(acc_ref)

        acc_ref[...] += _raw_mm(a_ref[...], b_ref[...], ta, tb, False)

        @pl.when(k == nk - 1)
        def _():
            finish(acc_ref[...], e_ref, o_ref)

    return pl.pallas_call(
        body_one_step if nk == 1 else body, name=name, grid=grid, in_specs=in_specs, out_specs=o_spec,
        out_shape=out_shape, scratch_shapes=[] if nk == 1 else [pltpu.VMEM((bm, bn), F32)],
        compiler_params=_cparams(),
    )(*operands)


def _rmsnorm_fwd(x, g, name, token=None):
    s, d = x.shape

    def body(x_ref, g_ref, *rest):
        rest[-1][...] = _rmsnorm(x_ref[...], g_ref[...]).astype(BF)

    return pl.pallas_call(
        body, name=name, grid=(s // ROW_BLOCK,),
        in_specs=[pl.BlockSpec((ROW_BLOCK, d), lambda i: (i, 0)), pl.BlockSpec((1, d), lambda i: (0, 0))]
        + ([] if token is None else [_TOKEN_SPEC]),
        out_specs=pl.BlockSpec((ROW_BLOCK, d), lambda i: (i, 0)),
        out_shape=jax.ShapeDtypeStruct((s, d), BF), compiler_params=_cparams(),
    )(x, g.reshape(1, d), *([] if token is None else [token]))


def _rmsnorm_bwd(x, g, dh, dres, name):
    s, d = x.shape

    def body(x_ref, g_ref, dh_ref, dres_ref, dx_ref, dxb_ref, dg_ref):
        _, vjp = jax.vjp(_rmsnorm, x_ref[...], g_ref[...])
        dx, dg = vjp(dh_ref[...].astype(F32))
        dx = dres_ref[...] + dx
        dx_ref[...] = dx
        dxb_ref[...] = dx.astype(BF)

        @pl.when(pl.program_id(0) == 0)
        def _():
            dg_ref[...] = jnp.zeros_like(dg_ref)

        dg_ref[...] += dg

    row = pl.BlockSpec((ROW_BLOCK, d), lambda i: (i, 0))
    vec = pl.BlockSpec((1, d), lambda i: (0, 0))
    dx, dxb, dg = pl.pallas_call(
        body, name=name, grid=(s // ROW_BLOCK,), in_specs=[row, vec, row, row], out_specs=[row, row, vec],
        out_shape=[jax.ShapeDtypeStruct((s, d), F32), jax.ShapeDtypeStruct((s, d), BF),
                   jax.ShapeDtypeStruct((1, d), F32)],
        compiler_params=_cparams(),
    )(x, g.reshape(1, d), dh, dres)
    return dx, dxb, dg.reshape(d)


def _loss_head(x, g, target):
    s, d = x.shape

    def loss_fn(xv, gv, tv):
        err = _rmsnorm(xv, gv) - tv
        return 0.5 * jnp.sum(jnp.mean(err * err, axis=-1, keepdims=True), axis=0, keepdims=True)

    def body(x_ref, g_ref, t_ref, dx_ref, dxb_ref, dg_ref, loss_ref):
        tv = t_ref[...]
        loss, vjp = jax.vjp(lambda xv, gv: loss_fn(xv, gv, tv), x_ref[...], g_ref[...])
        dx, dg = vjp(jnp.ones((1, 1), F32))
        dx_ref[...] = dx
        dxb_ref[...] = dx.astype(BF)

        @pl.when(pl.program_id(0) == 0)
        def _():
            dg_ref[...] = jnp.zeros_like(dg_ref)
            loss_ref[...] = jnp.zeros_like(loss_ref)

        dg_ref[...] += dg
        loss_ref[...] += loss * jnp.ones_like(loss_ref)

    row = pl.BlockSpec((ROW_BLOCK, d), lambda i: (i, 0))
    vec = pl.BlockSpec((1, d), lambda i: (0, 0))
    dx, dxb, dg, loss = pl.pallas_call(
        body, name="loss_head", grid=(s // ROW_BLOCK,), in_specs=[row, vec, row],
        out_specs=[row, row, vec, pl.BlockSpec((1, 128), lambda i: (0, 0))],
        out_shape=[jax.ShapeDtypeStruct((s, d), F32), jax.ShapeDtypeStruct((s, d), BF),
                   jax.ShapeDtypeStruct((1, d), F32), jax.ShapeDtypeStruct((1, 128), F32)],
        compiler_params=_cparams(),
    )(x, g.reshape(1, d), target)
    return loss[0, 0], dx, dxb, dg.reshape(d)


def _ret_tables(s, dk):
    h = jnp.arange(RET_HEADS, dtype=F32)
    log_gamma = jnp.log1p(-jnp.exp2(-5.0 - h))
    pos = jnp.arange(CHUNK, dtype=F32)
    dist = jnp.abs(pos[:, None] - pos[None, :])
    dintra = jnp.exp(log_gamma[:, None, None] * dist)
    qdec = jnp.exp(log_gamma[:, None] * (pos + 1.0))[:, :, None]
    kdec = jnp.exp(log_gamma[:, None] * (CHUNK - 1.0 - pos))[:, :, None]
    cdec = jnp.exp(log_gamma * CHUNK)[:, None, None]
    inv = ROPE_BASE ** (-jnp.arange(0, dk, 2, dtype=F32) / dk)
    ang = jnp.arange(s, dtype=F32)[:, None] * inv[None, :]
    return jnp.cos(ang), jnp.sin(ang), dintra, qdec, kdec, cdec


def _ret_specs(s, dk, dv, tb, rev):
    nh = RET_HEADS
    nb = s // tb
    bi = (lambda b: nb - 1 - b) if rev else (lambda b: b)
    voff = 2 * nh * dk // dv
    cpb = tb // CHUNK
    return dict(
        q=pl.BlockSpec((tb, dk), lambda h, b: (bi(b), h)),
        k=pl.BlockSpec((tb, dk), lambda h, b: (bi(b), nh + h)),
        v=pl.BlockSpec((tb, dv), lambda h, b: (bi(b), voff + h)),
        g=pl.BlockSpec((tb, dv), lambda h, b: (bi(b), voff + nh + h)),
        gain=pl.BlockSpec((None, 1, dv), lambda h, b: (h, 0, 0)),
        cs=pl.BlockSpec((tb, dk // 2), lambda h, b: (bi(b), 0)),
        dintra=pl.BlockSpec((None, CHUNK, CHUNK), lambda h, b: (h, 0, 0)),
        dec=pl.BlockSpec((None, CHUNK, 1), lambda h, b: (h, 0, 0)),
        cdec=pl.BlockSpec((None, 1, 1), lambda h, b: (h, 0, 0)),
        hv=pl.BlockSpec((tb, dv), lambda h, b: (bi(b), h)),
        hk=pl.BlockSpec((tb, dk), lambda h, b: (bi(b), h)),
        st=pl.BlockSpec((None, None, dk, dv), lambda h, b: (h, bi(b), 0, 0)),
    )


def _group(t):
    return t.reshape(t.shape[0] // CHUNK, CHUNK, t.shape[-1])


def _ret_fwd(proj, gain, tables):
    s = proj.shape[0]
    d = proj.shape[1] // 6
    dk, dv = d // RET_HEADS, 2 * d // RET_HEADS
    tb = min(TOKEN_BLOCK, s)
    sp = _ret_specs(s, dk, dv, tb, False)
    cos, sin, dintra, qdec, kdec, cdec = tables

    def body(q_ref, k_ref, v_ref, g_ref, gain_ref, cos_ref, sin_ref, di_ref, qd_ref, kd_ref, cd_ref,
             y_ref, st_ref, state):
        @pl.when(pl.program_id(1) == 0)
        def _():
            state[...] = jnp.zeros_like(state)

        st_ref[...] = state[...]
        ys, new_state = _ret_group(_mm_plain, _group(q_ref[...]), _group(k_ref[...]), _group(v_ref[...]),
                                   _group(g_ref[...]), gain_ref[...], state[...], _group(cos_ref[...]),
                                   _group(sin_ref[...]), di_ref[...], qd_ref[...], kd_ref[...], cd_ref[...])
        for c, y in enumerate(ys):
            y_ref[pl.ds(c * CHUNK, CHUNK), :] = y.astype(BF)
        state[...] = new_state

    return pl.pallas_call(
        body, name="ret_fwd", grid=(RET_HEADS, s // tb),
        in_specs=[sp['q'], sp['k'], sp['v'], sp['g'], sp['gain'], sp['cs'], sp['cs'], sp['dintra'], sp['dec'],
                  sp['dec'], sp['cdec']],
        out_specs=[sp['hv'], sp['st']],
        out_shape=[jax.ShapeDtypeStruct((s, RET_HEADS * dv), BF),
                   jax.ShapeDtypeStruct((RET_HEADS, s // tb, dk, dv), F32)],
        scratch_shapes=[pltpu.VMEM((dk, dv), F32)], compiler_params=_cparams(),
    )(proj, proj, proj, proj, gain.reshape(RET_HEADS, 1, dv), cos, sin, dintra, qdec, kdec, cdec)


def _ret_bwd(proj, gain, tables, states, dy):
    s = proj.shape[0]
    d = proj.shape[1] // 6
    dk, dv = d // RET_HEADS, 2 * d // RET_HEADS
    tb = min(TOKEN_BLOCK, s)
    cpb = tb // CHUNK
    sp = _ret_specs(s, dk, dv, tb, True)
    cos, sin, dintra, qdec, kdec, cdec = tables

    def body(q_ref, k_ref, v_ref, g_ref, gain_ref, cos_ref, sin_ref, di_ref, qd_ref, kd_ref, cd_ref, st_ref,
             dy_ref, dq_ref, dk_ref, dv_ref, dg_ref, dgain_ref, dstate):
        @pl.when(pl.program_id(1) == 0)
        def _():
            dstate[...] = jnp.zeros_like(dstate)
            dgain_ref[...] = jnp.zeros_like(dgain_ref)

        cos_g, sin_g = _group(cos_ref[...]), _group(sin_ref[...])
        di, qd, kd, cd = di_ref[...], qd_ref[...], kd_ref[...], cd_ref[...]

        def fn(q, k, v, g, gn, st):
            return _ret_group(_mm_diff, q, k, v, g, gn, st, cos_g, sin_g, di, qd, kd, cd)

        _, vjp = jax.vjp(fn, _group(q_ref[...]), _group(k_ref[...]), _group(v_ref[...]), _group(g_ref[...]),
                         gain_ref[...], st_ref[...])
        dys = [dy_ref[pl.ds(c * CHUNK, CHUNK), :].astype(F32) for c in range(cpb)]
        dq, dkk, dvv, dg, dgn, dst = vjp((dys, dstate[...]))
        dq_ref[...] = dq.reshape(tb, dk).astype(BF)
        dk_ref[...] = dkk.reshape(tb, dk).astype(BF)
        dv_ref[...] = dvv.reshape(tb, dv).astype(BF)
        dg_ref[...] = dg.reshape(tb, dv).astype(BF)
        dgain_ref[...] += dgn
        dstate[...] = dst

    dq, dkk, dvv, dg, dgain = pl.pallas_call(
        body, name="ret_bwd", grid=(RET_HEADS, s // tb),
        in_specs=[sp['q'], sp['k'], sp['v'], sp['g'], sp['gain'], sp['cs'], sp['cs'], sp['dintra'], sp['dec'],
                  sp['dec'], sp['cdec'], sp['st'], sp['hv']],
        out_specs=[sp['hk'], sp['hk'], sp['hv'], sp['hv'], sp['gain']],
        out_shape=[jax.ShapeDtypeStruct((s, RET_HEADS * dk), BF), jax.ShapeDtypeStruct((s, RET_HEADS * dk), BF),
                   jax.ShapeDtypeStruct((s, RET_HEADS * dv), BF), jax.ShapeDtypeStruct((s, RET_HEADS * dv), BF),
                   jax.ShapeDtypeStruct((RET_HEADS, 1, dv), F32)],
        scratch_shapes=[pltpu.VMEM((dk, dv), F32)], compiler_params=_cparams(),
    )(proj, proj, proj, proj, gain.reshape(RET_HEADS, 1, dv), cos, sin, dintra, qdec, kdec, cdec, states, dy)
    return jnp.concatenate([dq, dkk, dvv, dg], axis=1), dgain.reshape(RET_HEADS, dv)


def _gla_specs(s, dk, dv, tb, rev):
    nh = GLA_HEADS
    nb = s // tb
    bi = (lambda b: nb - 1 - b) if rev else (lambda b: b)
    voff = 2 * nh * dk // dv
    cpb = tb // CHUNK
    return dict(
        q=pl.BlockSpec((tb, dk), lambda h, b: (bi(b), h)),
        k=pl.BlockSpec((tb, dk), lambda h, b: (bi(b), nh + h)),
        v=pl.BlockSpec((tb, dv), lambda h, b: (bi(b), voff + h)),
        r=pl.BlockSpec((tb, dv), lambda h, b: (bi(b), voff + nh + h)),
        glow=pl.BlockSpec((tb, 128), lambda h, b: (bi(b), 0)),
        wgu=pl.BlockSpec((128, dk), lambda h, b: (0, h)),
        bias=pl.BlockSpec((1, dk), lambda h, b: (0, h)),
        gain=pl.BlockSpec((None, 1, dv), lambda h, b: (h, 0, 0)),
        hv=pl.BlockSpec((tb, dv), lambda h, b: (bi(b), h)),
        hk=pl.BlockSpec((tb, dk), lambda h, b: (bi(b), h)),
        hg=pl.BlockSpec((tb, 128), lambda h, b: (bi(b), h)),
        st=pl.BlockSpec((None, None, dv, dk), lambda h, b: (h, bi(b), 0, 0)),
    )


def _gla_fwd(proj, glow, wgu, bias, gain):
    s = proj.shape[0]
    d = proj.shape[1] // 3
    dk, dv = d // 2 // GLA_HEADS, d // GLA_HEADS
    tb = min(TOKEN_BLOCK, s)
    sp = _gla_specs(s, dk, dv, tb, False)

    def body(q_ref, k_ref, v_ref, r_ref, gl_ref, wgu_ref, b_ref, gain_ref, y_ref, st_ref, state):
        @pl.when(pl.program_id(1) == 0)
        def _():
            state[...] = jnp.zeros_like(state)

        st_ref[...] = state[...]
        ys, new_state = _gla_group(_mm_plain, _group(q_ref[...]), _group(k_ref[...]), _group(v_ref[...]),
                                   _group(r_ref[...]), _group(gl_ref[...]), wgu_ref[...], b_ref[...], gain_ref[...],
                                   state[...])
        for c, y in enumerate(ys):
            y_ref[pl.ds(c * CHUNK, CHUNK), :] = y.astype(BF)
        state[...] = new_state

    return pl.pallas_call(
        body, name="gla_fwd", grid=(GLA_HEADS, s // tb),
        in_specs=[sp['q'], sp['k'], sp['v'], sp['r'], sp['glow'], sp['wgu'], sp['bias'], sp['gain']],
        out_specs=[sp['hv'], sp['st']],
        out_shape=[jax.ShapeDtypeStruct((s, GLA_HEADS * dv), BF),
                   jax.ShapeDtypeStruct((GLA_HEADS, s // tb, dv, dk), F32)],
        scratch_shapes=[pltpu.VMEM((dv, dk), F32)], compiler_params=_cparams(),
    )(proj, proj, proj, proj, glow, wgu, bias, gain.reshape(GLA_HEADS, 1, dv))


def _gla_bwd(proj, glow, wgu, bias, gain, states, dy):
    s = proj.shape[0]
    d = proj.shape[1] // 3
    dk, dv = d // 2 // GLA_HEADS, d // GLA_HEADS
    tb = min(TOKEN_BLOCK, s)
    cpb = tb // CHUNK
    sp = _gla_specs(s, dk, dv, tb, True)

    def body(q_ref, k_ref, v_ref, r_ref, gl_ref, wgu_ref, b_ref, gain_ref, st_ref, dy_ref,
             dq_ref, dk_ref, dv_ref, dr_ref, dgl_ref, dwgu_ref, db_ref, dgain_ref, dstate):
        @pl.when(pl.program_id(1) == 0)
        def _():
            dstate[...] = jnp.zeros_like(dstate)
            dwgu_ref[...] = jnp.zeros_like(dwgu_ref)
            db_ref[...] = jnp.zeros_like(db_ref)
            dgain_ref[...] = jnp.zeros_like(dgain_ref)

        def fn(q, k, v, r, gl, w, b, gn, st):
            return _gla_group(_mm_diff, q, k, v, r, gl, w, b, gn, st)

        _, vjp = jax.vjp(fn, _group(q_ref[...]), _group(k_ref[...]), _group(v_ref[...]), _group(r_ref[...]),
                         _group(gl_ref[...]), wgu_ref[...], b_ref[...], gain_ref[...], st_ref[...])
        dys = [dy_ref[pl.ds(c * CHUNK, CHUNK), :].astype(F32) for c in range(cpb)]
        dq, dkk, dvv, dr, dgl, dw, db, dgn, dst = vjp((dys, dstate[...]))
        dq_ref[...] = dq.reshape(tb, dk).astype(BF)
        dk_ref[...] = dkk.reshape(tb, dk).astype(BF)
        dv_ref[...] = dvv.reshape(tb, dv).astype(BF)
        dr_ref[...] = dr.reshape(tb, dv).astype(BF)
        dgl_ref[...] = dgl.reshape(tb, dgl.shape[-1]).astype(BF)
        dwgu_ref[...] += dw
        db_ref[...] += db
        dgain_ref[...] += dgn
        dstate[...] = dst

    nh = GLA_HEADS
    dq, dkk, dvv, dr, dgl, dwgu, db, dgain = pl.pallas_call(
        body, name="gla_bwd", grid=(nh, s // tb),
        in_specs=[sp['q'], sp['k'], sp['v'], sp['r'], sp['glow'], sp['wgu'], sp['bias'], sp['gain'], sp['st'],
                  sp['hv']],
        out_specs=[sp['hk'], sp['hk'], sp['hv'], sp['hv'], sp['hg'], sp['wgu'], sp['bias'], sp['gain']],
        out_shape=[jax.ShapeDtypeStruct((s, nh * dk), BF), jax.ShapeDtypeStruct((s, nh * dk), BF),
                   jax.ShapeDtypeStruct((s, nh * dv), BF), jax.ShapeDtypeStruct((s, nh * dv), BF),
                   jax.ShapeDtypeStruct((s, nh * 128), BF), jax.ShapeDtypeStruct((128, nh * dk), F32),
                   jax.ShapeDtypeStruct((1, nh * dk), F32), jax.ShapeDtypeStruct((nh, 1, dv), F32)],
        scratch_shapes=[pltpu.VMEM((dv, dk), F32)], compiler_params=_cparams(),
    )(proj, proj, proj, proj, glow, wgu, bias, gain.reshape(nh, 1, dv), states, dy)
    return jnp.concatenate([dq, dkk, dvv, dr], axis=1), dgl, dwgu, db, dgain.reshape(nh, dv)


def _gdn_specs(s, dk):
    nh = GDN_HEADS
    col = lambda off: pl.BlockSpec((s, dk), lambda h: (0, off + h))
    tap = lambda off: pl.BlockSpec((CONV_WIDTH, dk), lambda h: (0, off + h))
    vec = pl.BlockSpec((1, 128), lambda h: (0, 0))
    return dict(q=col(0), k=col(nh), v=col(2 * nh), z=col(3 * nh), ba=pl.BlockSpec((s, 128), lambda h: (0, 0)),
                cq=tap(0), ck=tap(nh), cv=tap(2 * nh), vec=vec, gain=pl.BlockSpec((1, dk), lambda h: (0, 0)),
                head=col(0))


def _conv_silu(shift, x, taps):
    return _silu(_conv_taps(shift, x, taps))


def _load_taps(ref):
    return [ref[t:t + 1, :] for t in range(CONV_WIDTH)]


def _gdn_fwd(proj, ba, conv_w, a_log, dt_bias, gain):
    s = proj.shape[0]
    dk = proj.shape[1] // (4 * GDN_HEADS)
    sp = _gdn_specs(s, dk)
    rows = GDN_GROUP * CHUNK

    def grp(t):
        return t.reshape(GDN_GROUP, CHUNK, t.shape[-1])

    ngroups = s // rows

    def body(q_ref, k_ref, v_ref, z_ref, ba_ref, cq_ref, ck_ref, cv_ref, al_ref, dt_ref, gain_ref, y_ref, states,
             qc, kc, vc, state):
        head = pl.program_id(0)
        qc[...] = _conv_silu(_shift_rows, q_ref[...], _load_taps(cq_ref))
        kc[...] = _conv_silu(_shift_rows, k_ref[...], _load_taps(ck_ref))
        vc[...] = _conv_silu(_shift_rows, v_ref[...], _load_taps(cv_ref))
        state[...] = jnp.zeros_like(state)

        def step(n, carry):
            base = pl.multiple_of(n * rows, rows)
            sl = pl.ds(base, rows)
            states[n] = state[...]
            ys, new_state = _gdn_group(_mm_plain, grp(qc[sl, :]), grp(kc[sl, :]), grp(vc[sl, :]), grp(z_ref[sl, :]),
                                       grp(ba_ref[sl, :]), al_ref[...], dt_ref[...], gain_ref[...], state[...], head)
            for i, y in enumerate(ys):
                y_ref[pl.ds(base + i * CHUNK, CHUNK), :] = y.astype(BF)
            state[...] = new_state
            return carry

        lax.fori_loop(0, ngroups, step, 0)

    return pl.pallas_call(
        body, name="gdn_fwd", grid=(GDN_HEADS,),
        in_specs=[sp['q'], sp['k'], sp['v'], sp['z'], sp['ba'], sp['cq'], sp['ck'], sp['cv'], sp['vec'], sp['vec'],
                  sp['gain']],
        out_specs=[sp['head'], pl.BlockSpec((None, ngroups, dk, dk), lambda h: (h, 0, 0, 0))],
        out_shape=[jax.ShapeDtypeStruct((s, GDN_HEADS * dk), BF),
                   jax.ShapeDtypeStruct((GDN_HEADS, ngroups, dk, dk), F32)],
        scratch_shapes=[pltpu.VMEM((s, dk), F32)] * 3 + [pltpu.VMEM((dk, dk), F32)],
        compiler_params=_cparams(),
    )(proj, proj, proj, proj, ba, conv_w, conv_w, conv_w, a_log, dt_bias, gain)


def _gdn_bwd(proj, ba, conv_w, a_log, dt_bias, gain, group_states, dy):
    s = proj.shape[0]
    dk = proj.shape[1] // (4 * GDN_HEADS)
    sp = _gdn_specs(s, dk)
    rows = GDN_GROUP * CHUNK
    ngroups = s // rows

    def grp(t):
        return t.reshape(GDN_GROUP, CHUNK, t.shape[-1])

    def body(q_ref, k_ref, v_ref, z_ref, ba_ref, cq_ref, ck_ref, cv_ref, al_ref, dt_ref, gain_ref, states, dy_ref,
             dq_ref, dk_ref, dv_ref, dz_ref, dba_ref, dcq_ref, dck_ref, dcv_ref, dal_ref, ddt_ref, dgain_ref,
             qc, kc, vc, dqc, dkc, dvc, dstate):
        head = pl.program_id(0)

        @pl.when(head == 0)
        def _():
            dba_ref[...] = jnp.zeros_like(dba_ref)
            dal_ref[...] = jnp.zeros_like(dal_ref)
            ddt_ref[...] = jnp.zeros_like(ddt_ref)
            dgain_ref[...] = jnp.zeros_like(dgain_ref)

        qc[...] = _conv_silu(_shift_rows, q_ref[...], _load_taps(cq_ref))
        kc[...] = _conv_silu(_shift_rows, k_ref[...], _load_taps(ck_ref))
        vc[...] = _conv_silu(_shift_rows, v_ref[...], _load_taps(cv_ref))
        dstate[...] = jnp.zeros_like(dstate)

        def bstep(i, carry):
            n = ngroups - 1 - i
            base = pl.multiple_of(n * rows, rows)
            sl = pl.ds(base, rows)

            def fn(q, k, v, z, b, al, dt, gn, st):
                return _gdn_group(_mm_diff, q, k, v, z, b, al, dt, gn, st, head)

            _, vjp = jax.vjp(fn, grp(qc[sl, :]), grp(kc[sl, :]), grp(vc[sl, :]), grp(z_ref[sl, :]),
                             grp(ba_ref[sl, :]), al_ref[...], dt_ref[...], gain_ref[...], states[n])
            dys = [dy_ref[pl.ds(base + j * CHUNK, CHUNK), :].astype(F32) for j in range(GDN_GROUP)]
            dq, dkk, dvv, dz, db, dal, ddt, dgn, dst = vjp((dys, dstate[...]))
            dqc[sl, :] = dq.reshape(rows, dk)
            dkc[sl, :] = dkk.reshape(rows, dk)
            dvc[sl, :] = dvv.reshape(rows, dk)
            dz_ref[sl, :] = dz.reshape(rows, dk).astype(BF)
            dba_ref[sl, :] += db.reshape(rows, db.shape[-1])
            dal_ref[...] += dal
            ddt_ref[...] += ddt
            dgain_ref[...] += dgn
            dstate[...] = dst
            return carry

        lax.fori_loop(0, ngroups, bstep, 0)

        for x_ref, c_ref, dpost, dx_ref, dc_ref in ((q_ref, cq_ref, dqc, dq_ref, dcq_ref),
                                                    (k_ref, ck_ref, dkc, dk_ref, dck_ref),
                                                    (v_ref, cv_ref, dvc, dv_ref, dcv_ref)):
            _, vjp = jax.vjp(lambda x, *taps: _conv_silu(_shift_vjp, x, taps), x_ref[...], *_load_taps(c_ref))
            grads = vjp(dpost[...])
            dx_ref[...] = grads[0].astype(BF)
            for t in range(CONV_WIDTH):
                dc_ref[t:t + 1, :] = grads[1 + t]

    nh = GDN_HEADS
    col_bf = jax.ShapeDtypeStruct((s, nh * dk), BF)
    tap_out = jax.ShapeDtypeStruct((CONV_WIDTH, nh * dk), F32)
    tap_spec = pl.BlockSpec((CONV_WIDTH, dk), lambda h: (0, h))
    dq, dkk, dvv, dz, dba, dcq, dck, dcv, dal, ddt, dgain = pl.pallas_call(
        body, name="gdn_bwd", grid=(nh,),
        in_specs=[sp['q'], sp['k'], sp['v'], sp['z'], sp['ba'], sp['cq'], sp['ck'], sp['cv'], sp['vec'], sp['vec'],
                  sp['gain'], pl.BlockSpec((None, ngroups, dk, dk), lambda h: (h, 0, 0, 0)), sp['head']],
        out_specs=[sp['head']] * 4 + [sp['ba'], tap_spec, tap_spec, tap_spec, sp['vec'], sp['vec'], sp['gain']],
        out_shape=[col_bf] * 4 + [jax.ShapeDtypeStruct((s, 128), F32), tap_out, tap_out, tap_out,
                                  jax.ShapeDtypeStruct((1, 128), F32), jax.ShapeDtypeStruct((1, 128), F32),
                                  jax.ShapeDtypeStruct((1, dk), F32)],
        scratch_shapes=[pltpu.VMEM((s, dk), F32)] * 6 + [pltpu.VMEM((dk, dk), F32)],
        compiler_params=_cparams(),
    )(proj, proj, proj, proj, ba, conv_w, conv_w, conv_w, a_log, dt_bias, gain, group_states, dy)
    return (jnp.concatenate([dq, dkk, dvv, dz], axis=1), dba, jnp.concatenate([dcq, dck, dcv], axis=1), dal, ddt,
            dgain)


def _lru_specs(s, bw):
    nb = LRU_BLOCKS
    return dict(
        xb=pl.BlockSpec((s, bw), lambda n: (0, n)), yb=pl.BlockSpec((s, bw), lambda n: (0, nb + n)),
        taps=pl.BlockSpec((CONV_WIDTH, bw), lambda n: (0, n)), vec=pl.BlockSpec((1, bw), lambda n: (0, n)),
        w=pl.BlockSpec((None, bw, bw), lambda n: (n, 0, 0)), b=pl.BlockSpec((None, 1, bw), lambda n: (n, 0, 0)),
        col=pl.BlockSpec((s, bw), lambda n: (0, n)))


def _lru_fwd(proj, conv_w, conv_b, lam, wr, br, wi, bi):
    s = proj.shape[0]
    bw = proj.shape[1] // (2 * LRU_BLOCKS)
    sp = _lru_specs(s, bw)

    def body(xb_ref, yb_ref, cw_ref, cb_ref, lam_ref, wr_ref, br_ref, wi_ref, bi_ref, y_ref):
        a, u, gy = _lru_pre(_mm_plain, _shift_rows, xb_ref[...], yb_ref[...], _load_taps(cw_ref), cb_ref[...],
                            lam_ref[...], wr_ref[...], br_ref[...], wi_ref[...], bi_ref[...])
        y_ref[...] = (_linscan(a, u, False) * gy).astype(BF)

    return pl.pallas_call(
        body, name="lru_fwd", grid=(LRU_BLOCKS,),
        in_specs=[sp['xb'], sp['yb'], sp['taps'], sp['vec'], sp['vec'], sp['w'], sp['b'], sp['w'], sp['b']],
        out_specs=sp['col'], out_shape=jax.ShapeDtypeStruct((s, LRU_BLOCKS * bw), BF),
        compiler_params=_cparams(),
    )(proj, proj, conv_w, conv_b, lam, wr, br, wi, bi)


def _lru_bwd(proj, conv_w, conv_b, lam, wr, br, wi, bi, dy):
    s = proj.shape[0]
    nb = LRU_BLOCKS
    bw = proj.shape[1] // (2 * nb)
    sp = _lru_specs(s, bw)

    def body(xb_ref, yb_ref, cw_ref, cb_ref, lam_ref, wr_ref, br_ref, wi_ref, bi_ref, dy_ref,
             dxb_ref, dyb_ref, dcw_ref, dcb_ref, dlam_ref, dwr_ref, dbr_ref, dwi_ref, dbi_ref):
        def pre(xb, yb, t0, t1, t2, t3, cb, lm, w_r, b_r, w_i, b_i):
            return _lru_pre(_mm_diff, _shift_vjp, xb, yb, (t0, t1, t2, t3), cb, lm, w_r, b_r, w_i, b_i)

        (a, u, gy), vjp = jax.vjp(pre, xb_ref[...], yb_ref[...], *_load_taps(cw_ref), cb_ref[...], lam_ref[...],
                                  wr_ref[...], br_ref[...], wi_ref[...], bi_ref[...])
        h = _linscan(a, u, False)
        dout = dy_ref[...].astype(F32)
        g = _linscan(_shift_rows(a, 1, True), dout * gy, True)
        grads = vjp((g * _shift_rows(h, 1, False), g, dout * h))
        dxb_ref[...] = grads[0].astype(BF)
        dyb_ref[...] = grads[1].astype(BF)
        for t in range(CONV_WIDTH):
            dcw_ref[t:t + 1, :] = grads[2 + t]
        dcb_ref[...] = grads[6]
        dlam_ref[...] = grads[7]
        dwr_ref[...] = grads[8]
        dbr_ref[...] = grads[9]
        dwi_ref[...] = grads[10]
        dbi_ref[...] = grads[11]

    outs = pl.pallas_call(
        body, name="lru_bwd", grid=(nb,),
        in_specs=[sp['xb'], sp['yb'], sp['taps'], sp['vec'], sp['vec'], sp['w'], sp['b'], sp['w'], sp['b'], sp['col']],
        out_specs=[sp['col'], sp['col'], sp['taps'], sp['vec'], sp['vec'], sp['w'], sp['b'], sp['w'], sp['b']],
        out_shape=[jax.ShapeDtypeStruct((s, nb * bw), BF), jax.ShapeDtypeStruct((s, nb * bw), BF),
                   jax.ShapeDtypeStruct((CONV_WIDTH, nb * bw), F32), jax.ShapeDtypeStruct((1, nb * bw), F32),
                   jax.ShapeDtypeStruct((1, nb * bw), F32), jax.ShapeDtypeStruct((nb, bw, bw), F32),
                   jax.ShapeDtypeStruct((nb, 1, bw), F32), jax.ShapeDtypeStruct((nb, bw, bw), F32),
                   jax.ShapeDtypeStruct((nb, 1, bw), F32)],
        compiler_params=_cparams(),
    )(proj, proj, conv_w, conv_b, lam, wr, br, wi, bi, dy)
    return (jnp.concatenate([outs[0], outs[1]], axis=1),) + tuple(outs[2:])


ROW_STEP_BYTES = 12 * 1024 * 1024


def _row_tile(rows, cols, bytes_per_row_set):
    t = 8
    while t * 2 <= rows and rows % (t * 2) == 0 and (t * 2) * cols * bytes_per_row_set <= ROW_STEP_BYTES:
        t *= 2
    return t


def _sum_slabs(a, out_dtype, name):
    n, rows, cols = a.shape
    tr = _row_tile(rows, cols, 4 * (n + 1))

    def body(*refs):
        acc = refs[0][...].astype(F32)
        for r in refs[1:n]:
            acc = acc + r[...].astype(F32)
        refs[n][...] = acc.astype(out_dtype)

    specs = [pl.BlockSpec((None, tr, cols), functools.partial(lambda i, k: (k, i, 0), k=k)) for k in range(n)]
    return pl.pallas_call(
        body, name=name, grid=(rows // tr,), in_specs=specs, out_specs=pl.BlockSpec((tr, cols), lambda i: (i, 0)),
        out_shape=jax.ShapeDtypeStruct((rows, cols), out_dtype), compiler_params=_cparams(),
    )(*([a] * n))


def _sum_keep_and_landed(g, landed, core, name):
    four, hr, cols = landed.shape
    tr = _row_tile(hr, cols, 2 * 3)
    tph = hr // tr

    def body(core_ref, g_ref, l_ref, o_ref):
        o_ref[...] = (g_ref[...].astype(F32) + l_ref[...].astype(F32)).astype(o_ref.dtype)

    grid_spec = pltpu.PrefetchScalarGridSpec(
        num_scalar_prefetch=1, grid=(four, tph),
        in_specs=[pl.BlockSpec((None, tr, cols), lambda j, r, core_ref: (j, core_ref[0] * tph + r, 0)),
                  pl.BlockSpec((None, tr, cols), lambda j, r, core_ref: (j, r, 0))],
        out_specs=pl.BlockSpec((None, tr, cols), lambda j, r, core_ref: (j, r, 0)))
    return pl.pallas_call(body, name=name, grid_spec=grid_spec, out_shape=jax.ShapeDtypeStruct(landed.shape, g.dtype),
                          compiler_params=_cparams())(core, g, landed)


def _sum_chip_parts(parts, landed, chip, name):
    _, rows, cols = parts.shape
    tr = _row_tile(rows, cols, 2 * 4 + 4)

    def body(chip_ref, p_ref, l0_ref, l1_ref, l2_ref, o_ref):
        acc = l0_ref[...].astype(F32) + l1_ref[...].astype(F32)
        o_ref[...] = (acc + l2_ref[...].astype(F32)) + p_ref[...].astype(F32)

    slab = lambda k: pl.BlockSpec((None, tr, cols), lambda i, chip_ref: (k, i, 0))
    grid_spec = pltpu.PrefetchScalarGridSpec(
        num_scalar_prefetch=1, grid=(rows // tr,),
        in_specs=[pl.BlockSpec((None, tr, cols), lambda i, chip_ref: (chip_ref[0], i, 0)), slab(0), slab(1), slab(2)],
        out_specs=pl.BlockSpec((tr, cols), lambda i, chip_ref: (i, 0)))
    return pl.pallas_call(body, name=name, grid_spec=grid_spec, out_shape=jax.ShapeDtypeStruct((rows, cols), F32),
                          compiler_params=_cparams())(chip, parts, landed, landed, landed)


_ADAM_C1 = 1.0 / (1.0 - ADAM_B1 ** ADAM_STEP)
_ADAM_C2 = 1.0 / (1.0 - ADAM_B2 ** ADAM_STEP)


def _adamw_math(w, g, m, v):
    nm = ADAM_B1 * m + (1.0 - ADAM_B1) * g
    nv = ADAM_B2 * v + (1.0 - ADAM_B2) * (g * g)
    den = jnp.sqrt(nv * _ADAM_C2) + ADAM_EPS
    inv = pl.reciprocal(den, approx=True)
    inv = inv * (2.0 - den * inv)
    delta = -ADAM_LR * ((nm * _ADAM_C1) * inv + ADAM_WD * w)
    return delta, nm, nv


def _adamw(w, g, m, v, name):
    rows, cols = w.shape
    tr = _row_tile(rows, cols, 4 * 7)

    def body(w_ref, g_ref, m_ref, v_ref, d_ref, nm_ref, nv_ref):
        d_ref[...], nm_ref[...], nv_ref[...] = _adamw_math(w_ref[...], g_ref[...], m_ref[...], v_ref[...])

    spec = pl.BlockSpec((tr, cols), lambda i: (i, 0))
    shape = jax.ShapeDtypeStruct((rows, cols), F32)
    return pl.pallas_call(
        body, name=name, grid=(rows // tr,), in_specs=[spec] * 4, out_specs=[spec] * 3, out_shape=[shape] * 3,
        compiler_params=_cparams(),
    )(w, g, m, v)


def _adamw_halves(w, m, v, layer, mine, theirs, core, name, carried=None):
    n_layers, rows, cols = w.shape
    hr = mine.shape[0]
    tr = _row_tile(hr, cols, 4 * 9)
    tph = hr // tr

    def body(core_ref, w_ref, q_ref, t_ref, m_ref, v_ref, *rest):
        g_ref, d_ref, nm_ref, nv_ref = rest[-4:]
        is_mine = (pl.program_id(0) // tph) == core_ref[0]
        g = jnp.where(is_mine, q_ref[...], t_ref[...])
        g_ref[...] = g
        d_ref[...], nm_ref[...], nv_ref[...] = _adamw_math(w_ref[...], g, m_ref[...], v_ref[...])

    slab = pl.BlockSpec((None, tr, cols), lambda i, core_ref: (layer, i, 0))

    def mine_index(i, core_ref):
        return jnp.where(i // tph == core_ref[0], i % tph, jnp.where(core_ref[0] == 0, tph - 1, 0)), 0

    def theirs_index(i, core_ref):
        return jnp.where(i // tph == core_ref[0], jnp.where(core_ref[0] == 0, 0, tph - 1), i % tph), 0

    in_specs = [slab, pl.BlockSpec((tr, cols), mine_index), pl.BlockSpec((tr, cols), theirs_index), slab, slab]
    operands = [core, w, mine, theirs, m, v]
    aliases = {}
    if carried is not None:
        in_specs += [_ANY] * 4
        aliases = {len(operands) + k: k for k in range(4)}
        operands += list(carried)
    grid_spec = pltpu.PrefetchScalarGridSpec(num_scalar_prefetch=1, grid=(rows // tr,), in_specs=in_specs,
                                             out_specs=[slab] * 4)
    return pl.pallas_call(
        body, name=name, grid_spec=grid_spec, out_shape=[jax.ShapeDtypeStruct(w.shape, F32)] * 4,
        input_output_aliases=aliases, compiler_params=_cparams(),
    )(*operands)


_ANY = pl.BlockSpec(memory_space=pl.ANY)


def _mesh_pos():
    return lax.axis_index("x"), lax.axis_index("y"), lax.axis_index("c")


def _remote(src, dst, send_sems, recv_sems, k, dev):
    return pltpu.make_async_remote_copy(src_ref=src, dst_ref=dst, send_sem=send_sems.at[k], recv_sem=recv_sems.at[k],
                                        device_id=dev, device_id_type=MESH)


STREAM_CHUNK_BYTES = 2 * 1024 * 1024
STREAM_SLOTS = 4


def _chunk_rows(rows, cols, itemsize):
    assert rows % 16 == 0, rows
    t = 16
    while t * 2 <= rows and rows % (t * 2) == 0 and (t * 2) * cols * itemsize <= STREAM_CHUNK_BYTES:
        t *= 2
    return t


def _stream_chunks(n_chunks, src_at, dst_at, buf, load_sems, send_sems, recv_sem, sibling):
    def load(k, slot):
        return pltpu.make_async_copy(src_at(k), buf.at[slot], load_sems.at[slot])

    def send(k, slot):
        if sibling is None:
            return pltpu.make_async_copy(buf.at[slot], dst_at(k), send_sems.at[slot])
        return pltpu.make_async_remote_copy(src_ref=buf.at[slot], dst_ref=dst_at(k), send_sem=send_sems.at[slot],
                                            recv_sem=recv_sem, device_id=sibling, device_id_type=MESH)

    def sent(k, slot):
        return send(k, slot).wait() if sibling is None else send(k, slot).wait_send()

    def step(k, carry):
        slot = k % STREAM_SLOTS

        @pl.when(k >= STREAM_SLOTS)
        def _():
            sent(k - STREAM_SLOTS, slot)

        load(k, slot).start()

        @pl.when(k >= 1)
        def _():
            prev = (k - 1) % STREAM_SLOTS
            load(k - 1, prev).wait()
            send(k - 1, prev).start()

        return carry

    lax.fori_loop(0, n_chunks, step, 0)
    last = (n_chunks - 1) % STREAM_SLOTS
    load(n_chunks - 1, last).wait()
    send(n_chunks - 1, last).start()
    for k in range(max(0, n_chunks - STREAM_SLOTS), n_chunks):
        sent(k, k % STREAM_SLOTS)


_HBM = pl.BlockSpec(memory_space=pltpu.HBM)
_SEM = pl.BlockSpec(memory_space=pltpu.SEMAPHORE)
_DATAFLOW = pltpu.SideEffectType.DATAFLOW_SIDE_EFFECTING


def _chip_copies(kind, ins, lands, split, send_sems, recv_sems):
    x, y, c = _mesh_pos()
    me = 2 * x + y
    chips = [(1 - x, y), (x, 1 - y), (1 - x, 1 - y)]
    pairs = []
    for i in range(len(ins)):
        for j, chip in enumerate(chips):
            pj = 2 * chip[0] + chip[1]
            if kind == 'scatter':
                src, dst, got = ins[i].at[pj], lands[i].at[j], lands[i].at[j]
            elif split[i]:
                hr = ins[i].shape[0] // 2
                rows = pl.ds(c * hr, hr)
                src, dst, got = ins[i].at[rows], lands[i].at[me, rows], lands[i].at[pj, rows]
            else:
                src, dst, got = ins[i], lands[i].at[me], lands[i].at[pj]
            k = 3 * i + j
            pairs.append((_remote(src, dst, send_sems, recv_sems, k, (*chip, c)),
                          _remote(got, got, send_sems, recv_sems, k, (*chip, c))))
    return pairs


def _chip_exchange_start(kind, srcs, land_shapes, split, name, after):
    n = len(srcs)

    def body(*refs):
        ins, lands = refs[:n], refs[n:2 * n]
        send_sems, recv_sems = refs[2 * n + 1], refs[2 * n + 2]
        token = refs[-1]
        for send, _ in _chip_copies(kind, ins, lands, split, send_sems, recv_sems):
            send.start()
        token[...] = jnp.zeros_like(token)

    hbm = lambda t: pltpu.with_memory_space_constraint(t, pltpu.HBM)
    operands = [hbm(s) for s in srcs] + [hbm(lax.empty(shp, s.dtype)) for shp, s in zip(land_shapes, srcs)] + [after]
    out = pl.pallas_call(
        body, name=name, in_specs=[_HBM] * (2 * n) + [_ANY],
        out_specs=[_SEM, _SEM] + [_HBM] * (2 * n) + [pl.BlockSpec(memory_space=pltpu.VMEM)],
        out_shape=[pltpu.SemaphoreType.DMA((3 * n,)), pltpu.SemaphoreType.DMA((3 * n,))]
        + [pltpu.HBM(s.shape, s.dtype) for s in srcs] + [pltpu.HBM(shp, s.dtype) for shp, s in zip(land_shapes, srcs)]
        + [jax.ShapeDtypeStruct((8, 128), F32)],
        input_output_aliases={i: 2 + i for i in range(2 * n)},
        compiler_params=pltpu.CompilerParams(has_side_effects=_DATAFLOW),
    )(*operands)
    return out[0], out[1], out[2:2 + n], out[2 + n:2 + 2 * n], out[-1]


def _chip_exchange_wait(kind, started, split, name, after):
    send_sems, recv_sems, srcs, lands, _ = started
    n = len(srcs)
    after = list(after) if isinstance(after, (list, tuple)) else [after]

    def body(*refs):
        ins, land_refs = refs[:n], refs[n:2 * n]
        for send, arrived in _chip_copies(kind, ins, land_refs, split, refs[2 * n], refs[2 * n + 1]):
            send.wait_send()
            arrived.wait_recv()

    out = pl.pallas_call(
        body, name=name, in_specs=[_HBM] * (2 * n) + [_SEM, _SEM] + [_ANY] * len(after), out_specs=[_HBM] * (2 * n),
        out_shape=[pltpu.HBM(s.shape, s.dtype) for s in srcs] + [pltpu.HBM(l.shape, l.dtype) for l in lands],
        input_output_aliases={i: i for i in range(2 * n)},
        compiler_params=pltpu.CompilerParams(has_side_effects=_DATAFLOW),
    )(*srcs, *lands, send_sems, recv_sems, *after)
    return out[:n], out[n:]


def _pass_to_sibling(gathered, own, name, token=None):
    n = len(gathered)
    tr = [_chunk_rows(g.shape[1] // 2, g.shape[2], g.dtype.itemsize) for g in gathered]
    extra = list(own) + ([] if token is None else [token])

    def body(*refs):
        own_refs = refs[n:2 * n]
        refs = refs[:n] + refs[n + len(extra):]
        outs = refs[n:2 * n]
        recv_sems, load_sems, send_sems = refs[2 * n:2 * n + 3]
        bufs = refs[2 * n + 3:]
        x, y, c = _mesh_pos()
        sibling = (x, y, 1 - c)
        chips = [(1 - x, y), (x, 1 - y), (1 - x, 1 - y)]
        for i in range(n):
            def own_rows(k, i=i):
                return own_refs[i].at[pl.ds(k * tr[i], tr[i])]

            def own_place(k, i=i):
                return outs[i].at[2 * x + y, pl.ds(k * tr[i], tr[i])]

            _stream_chunks(outs[i].shape[1] // tr[i], own_rows, own_place, bufs[i], load_sems, send_sems, None, None)
        for i in range(n):
            hr = outs[i].shape[1] // 2
            for j, chip in enumerate(chips):
                def rows_at(k, i=i, pj=2 * chip[0] + chip[1], hr=hr):
                    return outs[i].at[pj, pl.ds(c * hr + k * tr[i], tr[i])]

                _stream_chunks(hr // tr[i], rows_at, rows_at, bufs[i], load_sems, send_sems, recv_sems.at[3 * i + j],
                               sibling)
        for i in range(n):
            hr = outs[i].shape[1] // 2
            for j, chip in enumerate(chips):
                blk = outs[i].at[2 * chip[0] + chip[1], pl.ds((1 - c) * hr, hr)]
                pltpu.make_async_remote_copy(src_ref=blk, dst_ref=blk, send_sem=send_sems.at[0],
                                             recv_sem=recv_sems.at[3 * i + j], device_id=sibling,
                                             device_id_type=MESH).wait_recv()

    return pl.pallas_call(
        body, name=name, in_specs=[_ANY] * (n + len(extra)), out_specs=[_ANY] * n,
        out_shape=[jax.ShapeDtypeStruct(g.shape, g.dtype) for g in gathered],
        input_output_aliases={i: i for i in range(n)},
        scratch_shapes=[pltpu.SemaphoreType.DMA((3 * n,)), pltpu.SemaphoreType.DMA((STREAM_SLOTS,)),
                        pltpu.SemaphoreType.DMA((STREAM_SLOTS,))]
        + [pltpu.VMEM((STREAM_SLOTS, tr[i], g.shape[2]), g.dtype) for i, g in enumerate(gathered)],
        compiler_params=_cparams(),
    )(*gathered, *extra)


def _stream_to_sibling(srcs, halved, name):
    n = len(srcs)
    geo = []
    for s in srcs:
        rows = s.shape[1] // 2 if halved else s.shape[1]
        geo.append((s.shape[0], rows, s.shape[2], _chunk_rows(rows, s.shape[2], s.dtype.itemsize)))

    def body(*refs):
        ins, outs = refs[:n], refs[n:2 * n]
        recv_sems, load_sems, send_sems = refs[2 * n:2 * n + 3]
        bufs = refs[2 * n + 3:]
        x, y, c = _mesh_pos()
        sibling = (x, y, 1 - c)
        for i in range(n):
            slabs, rows, _, tr = geo[i]
            per_slab = rows // tr
            off = (1 - c) * rows if halved else 0

            def src_at(k, i=i, per_slab=per_slab, tr=tr, off=off):
                return ins[i].at[k // per_slab, pl.ds(off + (k % per_slab) * tr, tr)]

            def dst_at(k, i=i, per_slab=per_slab, tr=tr):
                return outs[i].at[k // per_slab, pl.ds((k % per_slab) * tr, tr)]

            _stream_chunks(slabs * per_slab, src_at, dst_at, bufs[i], load_sems, send_sems, recv_sems.at[i], sibling)
        for i in range(n):
            pltpu.make_async_remote_copy(src_ref=outs[i], dst_ref=outs[i], send_sem=send_sems.at[0],
                                         recv_sem=recv_sems.at[i], device_id=sibling, device_id_type=MESH).wait_recv()

    return pl.pallas_call(
        body, name=name, in_specs=[_ANY] * n, out_specs=[_ANY] * n,
        out_shape=[jax.ShapeDtypeStruct((g[0], g[1], g[2]), s.dtype) for g, s in zip(geo, srcs)],
        scratch_shapes=[pltpu.SemaphoreType.DMA((n,)), pltpu.SemaphoreType.DMA((STREAM_SLOTS,)),
                        pltpu.SemaphoreType.DMA((STREAM_SLOTS,))]
        + [pltpu.VMEM((STREAM_SLOTS, g[3], g[2]), s.dtype) for g, s in zip(geo, srcs)],
        compiler_params=_cparams(),
    )(*srcs)


def _reduce_scatter_begin(grads, tag, core):
    landed = _stream_to_sibling(grads, True, f"rs_swap_{tag}")
    parts = [_sum_keep_and_landed(g, l, core, f"rs_add2_{tag}_{i}") for i, (g, l) in enumerate(zip(grads, landed))]
    return _chip_exchange_start('scatter', parts, [(3,) + p.shape[1:] for p in parts], None, f"rs_scatter_start_{tag}",
                                core)


def _reduce_scatter_finish(started, tag, chip, after):
    parts, landed = _chip_exchange_wait('scatter', started, None, f"rs_scatter_wait_{tag}", after)
    mine = [_sum_chip_parts(p, l, chip, f"rs_add4_{tag}_{i}") for i, (p, l) in enumerate(zip(parts, landed))]
    theirs = _stream_to_sibling([m[None] for m in mine], False, f"rs_join_{tag}")
    return [(m, t[0]) for m, t in zip(mine, theirs)]


def _pad_cols(a, width=128):
    return jnp.pad(a, ((0, 0), (0, width - a.shape[1])))


def _mixer_forward(kind, hn, w, tables):
    if kind == 0:
        proj = _matmul(hn, w['ret_w_in'], name="ret_proj", b_sharded=True)
        og, states = _ret_fwd(proj, w['ret_gn_gain'], tables)
        return og, (proj, states)
    if kind == 1:
        proj = _matmul(hn, w['gdn_w_main'], name="gdn_proj")
        ba = _matmul(hn, w['gdn_w_small'], name="gdn_proj_ba")
        og, states = _gdn_fwd(proj, ba, w['gdn_conv_w'], w['gdn_a_log'], w['gdn_dt_bias'], w['gdn_norm_gain'])
        return og, (proj, ba, states)
    if kind == 2:
        proj = _matmul(hn, w['gla_w_main'], name="gla_proj")
        glow = _matmul(hn, w['gla_w_small'], name="gla_proj_gate")
        og, states = _gla_fwd(proj, glow, w['gla_w_gate_up'], w['gla_gate_bias'], w['gla_norm_gain'])
        return og, (proj, glow, states)
    proj = _matmul(hn, w['lru_w_in'], name="lru_proj", b_sharded=True)
    og = _lru_fwd(proj, w['lru_conv_w'], w['lru_conv_b'], w['lru_lambda'], w['lru_w_rgate'], w['lru_b_rgate'],
                  w['lru_w_igate'], w['lru_b_igate'])
    return og, (proj,)


def _mixer_backward(kind, hn, w, tables, saved, d_og, grads):
    d = hn.shape[1]
    if kind == 0:
        proj, states = saved
        d_proj, grads['ret_gn_gain'] = _ret_bwd(proj, w['ret_gn_gain'], tables, states, d_og)
        grads['ret_w_in'] = _matmul(hn, d_proj, name="ret_dw_in", ta=True, out_dtype=BF, o_sharded=True)
        return _matmul(d_proj, w['ret_w_in'], name="ret_dhn", tb=True, b_sharded=True)
    if kind == 1:
        proj, ba, states = saved
        d_proj, d_ba, grads['gdn_conv_w'], grads['gdn_a_log'], grads['gdn_dt_bias'], grads['gdn_norm_gain'] = _gdn_bwd(
            proj, ba, w['gdn_conv_w'], w['gdn_a_log'], w['gdn_dt_bias'], w['gdn_norm_gain'], states, d_og)
        d_ba = d_ba.astype(BF)
        dw_main = _matmul(hn, d_proj, name="gdn_dw_main", ta=True, out_dtype=BF)
        dw_small = _matmul(hn, d_ba, name="gdn_dw_small", ta=True, out_dtype=BF)
        dw = jnp.concatenate([dw_main, dw_small[:, :2 * GDN_HEADS]], axis=1)
        grads['gdn_w_in'] = dw.reshape(d, N_CHIPS, dw.shape[1] // N_CHIPS).transpose(1, 0, 2)
        d_hn = _matmul(d_proj, w['gdn_w_main'], name="gdn_dhn_main", tb=True)
        return _matmul(d_ba, w['gdn_w_small'], name="gdn_dhn_small", tb=True, epilogue='add', extra=d_hn)
    if kind == 2:
        proj, glow, states = saved
        d_proj, d_glow4, d_wgu, grads['gla_gate_bias'], grads['gla_norm_gain'] = _gla_bwd(
            proj, glow, w['gla_w_gate_up'], w['gla_gate_bias'], w['gla_norm_gain'], states, d_og)
        grads['gla_w_gate_up'] = d_wgu[:GLA_RANK]
        dw_main = _matmul(hn, d_proj, name="gla_dw_main", ta=True, out_dtype=BF)
        dw_small4 = _matmul(hn, d_glow4, name="gla_dw_small", ta=True, out_dtype=F32)
        dw_small = dw_small4.reshape(d, GLA_HEADS, 128)[:, :, :GLA_RANK].sum(axis=1).astype(BF)
        dw = jnp.concatenate([dw_main, dw_small], axis=1)
        grads['gla_w_in'] = dw.reshape(d, N_CHIPS, dw.shape[1] // N_CHIPS).transpose(1, 0, 2)
        d_hn = _matmul(d_proj, w['gla_w_main'], name="gla_dhn_main", tb=True)
        w_small4 = jnp.tile(w['gla_w_small'], (1, GLA_HEADS))
        return _matmul(d_glow4, w_small4, name="gla_dhn_small", tb=True, epilogue='add', extra=d_hn)
    (proj,) = saved
    (d_proj, grads['lru_conv_w'], grads['lru_conv_b'], grads['lru_lambda'], grads['lru_w_rgate'], grads['lru_b_rgate'],
     grads['lru_w_igate'], grads['lru_b_igate']) = _lru_bwd(
        proj, w['lru_conv_w'], w['lru_conv_b'], w['lru_lambda'], w['lru_w_rgate'], w['lru_b_rgate'], w['lru_w_igate'],
        w['lru_b_igate'], d_og)
    grads['lru_w_in'] = _matmul(hn, d_proj, name="lru_dw_in", ta=True, out_dtype=BF, o_sharded=True)
    return _matmul(d_proj, w['lru_w_in'], name="lru_dhn", tb=True, b_sharded=True)


_W_OUT = ('ret_w_out', 'gdn_w_out', 'gla_w_out', 'lru_w_out')
_W_IN = ('ret_w_in', 'gdn_w_in', 'gla_w_in', 'lru_w_in')


def _layer_forward(layer, x, w, tables, token=None):
    hn = _rmsnorm_fwd(x, w['norm1'][layer], f"norm1_fwd_{layer}", token)
    og, mixer_saved = _mixer_forward(layer, hn, w, tables)
    x1 = _matmul(og, w[_W_OUT[layer]], name=f"mixer_out_{layer}", epilogue='add', extra=x)
    hn2 = _rmsnorm_fwd(x1, w['norm2'][layer], f"norm2_fwd_{layer}")
    act = _matmul(hn2, w['mlp_w_up'][layer], name="mlp_up", b_sharded=True, epilogue='relu2', out_dtype=BF)
    x2 = _matmul(act, w['mlp_w_down'][layer], name="mlp_down", epilogue='add', extra=x1)
    return x2, (x, hn, mixer_saved, og, x1, hn2, act)


def _layer_backward(layer, dx2, dx2_bf, w, tables, saved, token=None):
    x, hn, mixer_saved, og, x1, hn2, act = saved
    d = x.shape[1]
    grads = {}
    d_up = _matmul(dx2_bf, w['mlp_w_down'][layer], name="mlp_d_up", tb=True, epilogue='dact', extra=act,
                   out_dtype=BF, token=token)
    dw_down = _matmul(act, dx2_bf, name="mlp_dw_down", ta=True, out_dtype=BF)
    grads['mlp_w_down'] = dw_down.reshape(N_CHIPS, dw_down.shape[0] // N_CHIPS, d)
    grads['mlp_w_up'] = _matmul(hn2, d_up, name="mlp_dw_up", ta=True, out_dtype=BF, o_sharded=True)
    d_hn2 = _matmul(d_up, w['mlp_w_up'][layer], name="mlp_d_hn", tb=True, b_sharded=True)
    dx1, dx1_bf, grads['norm2'] = _rmsnorm_bwd(x1, w['norm2'][layer], d_hn2, dx2, f"norm2_bwd_{layer}")
    w_out = w[_W_OUT[layer]]
    d_og = _matmul(dx1_bf, w_out, name=f"mixer_d_og_{layer}", tb=True, out_dtype=BF)
    dw_out = _matmul(og, dx1_bf, name=f"mixer_dw_out_{layer}", ta=True, out_dtype=BF)
    grads[_W_OUT[layer]] = dw_out.reshape(N_CHIPS, dw_out.shape[0] // N_CHIPS, d)
    d_hn = _mixer_backward(layer, hn, w, tables, mixer_saved, d_og, grads)
    dx, dx_bf, grads['norm1'] = _rmsnorm_bwd(x, w['norm1'][layer], d_hn, dx1, f"norm1_bwd_{layer}")
    return dx, dx_bf, grads


PACK_ROWS = 256


def _pack(arrays):
    flat = []
    for a in arrays:
        v = a.astype(F32).reshape(-1)
        v = jnp.pad(v, (0, (-v.shape[0]) % 128))
        flat.append(v.reshape(-1, 128))
    buf = jnp.concatenate(flat, axis=0)
    return jnp.pad(buf, ((0, (-buf.shape[0]) % PACK_ROWS), (0, 0)))


def _unpack(buf, shapes):
    lead = buf.shape[:-2]
    out, off = [], 0
    for shp in shapes:
        n = math.prod(shp)
        rows = -(-n // 128)
        piece = buf[..., off:off + rows, :].reshape(lead + (rows * 128,))[..., :n]
        out.append(piece.reshape(lead + tuple(shp)))
        off += rows
    return out


_WEIGHTS = ('norm1', 'norm2', 'final_norm', 'ret_w_in', 'ret_gn_gain', 'ret_w_out', 'gdn_w_in', 'gdn_conv_w',
            'gdn_a_log', 'gdn_dt_bias', 'gdn_norm_gain', 'gdn_w_out', 'gla_w_in', 'gla_w_gate_up', 'gla_gate_bias',
            'gla_norm_gain', 'gla_w_out', 'lru_w_in', 'lru_conv_w', 'lru_conv_b', 'lru_w_rgate', 'lru_b_rgate',
            'lru_w_igate', 'lru_b_igate', 'lru_lambda', 'lru_w_out', 'mlp_w_up', 'mlp_w_down')
_FWD_PARAMS = ('x',) + _WEIGHTS
_BIG = ('ret_w_in', 'ret_w_out', 'gdn_w_in', 'gdn_w_out', 'gla_w_in', 'gla_w_out', 'lru_w_in', 'lru_w_out',
        'mlp_w_up', 'mlp_w_down')
_SMALL = tuple(n for n in _WEIGHTS if n not in _BIG)
_SMALL_SHARDED = ('ret_gn_gain', 'gdn_conv_w', 'gla_w_gate_up', 'gla_gate_bias', 'gla_norm_gain', 'lru_conv_w',
                  'lru_conv_b', 'lru_lambda')


def kernel(*args):
    names = _FWD_PARAMS + ('loss_target',) + tuple('m_' + n for n in _WEIGHTS) + tuple('v_' + n for n in _WEIGHTS)
    assert len(args) == len(names)
    a = dict(zip(names, args))
    x = a['x'][0]
    target = a['loss_target'][0]
    s, d = x.shape
    chip = 2 * lax.axis_index("x") + lax.axis_index("y")

    small_local = [a[n][0] if a[n].ndim == 3 else a[n] for n in _SMALL_SHARDED]
    small_pack = _pack(small_local)
    core_arr = lax.axis_index("c").astype(jnp.int32).reshape(1)
    chip_arr = chip.astype(jnp.int32).reshape(1)

    def whole_cols(g):
        return g.transpose(1, 0, 2).reshape(g.shape[1], N_CHIPS * g.shape[2])

    def whole_rows(g):
        return g.reshape(N_CHIPS * g.shape[1], g.shape[2])

    w = {'mlp_w_up': [None] * 4, 'mlp_w_down': [None] * 4}
    for n in _SMALL:
        if n not in _SMALL_SHARDED:
            w[n] = a[n][0] if n.startswith('lru_') else a[n]
    w['gdn_a_log'], w['gdn_dt_bias'] = _pad_cols(w['gdn_a_log']), _pad_cols(w['gdn_dt_bias'])
    w['lru_b_rgate'] = w['lru_b_rgate'].reshape(LRU_BLOCKS, 1, -1)
    w['lru_b_igate'] = w['lru_b_igate'].reshape(LRU_BLOCKS, 1, -1)

    def gather_start(layer, after):
        ops = [a[_W_IN[layer]][0].astype(BF), a[_W_OUT[layer]][0].astype(BF), a['mlp_w_up'][layer].astype(BF),
               a['mlp_w_down'][layer].astype(BF)] + ([small_pack] if layer == 0 else [])
        split = [True] * 4 + ([False] if layer == 0 else [])
        return _chip_exchange_start('gather', ops, [(N_CHIPS,) + o.shape for o in ops], split,
                                    f"gather_start_{layer}", after), split

    def gather_finish(layer, started, after):
        started, split = started
        own, lands = _chip_exchange_wait('gather', started, split, f"gather_wait_{layer}", after)
        nxt = gather_start(layer + 1, own[0]) if layer < 3 else None
        g_in, g_out, g_up, g_down = _pass_to_sibling(lands[:4], own[:4], f"gather_pass_{layer}",
                                                     None if nxt is None else nxt[0][4])
        w['mlp_w_up'][layer], w['mlp_w_down'][layer], w[_W_OUT[layer]] = g_up, whole_rows(g_down), whole_rows(g_out)
        if layer == 0:
            small_all = lax.dynamic_update_slice_in_dim(lands[4], own[4][None], chip, axis=0)
            for n, piece in zip(_SMALL_SHARDED, _unpack(small_all, [p.shape for p in small_local])):
                w[n] = whole_cols(piece)
            w['gla_w_gate_up'] = jnp.pad(w['gla_w_gate_up'], ((0, 128 - GLA_RANK), (0, 0)))
        if layer in (0, 3):
            w[_W_IN[layer]] = g_in
        else:
            name, tail = ('gdn', 2 * GDN_HEADS) if layer == 1 else ('gla', GLA_RANK)
            full = whole_cols(g_in)
            w[name + '_w_main'] = full[:, :full.shape[1] - tail]
            w[name + '_w_small'] = _pad_cols(full[:, full.shape[1] - tail:])
        return nxt

    tables = _ret_tables(s, d // RET_HEADS)
    saved = []
    h = x
    nxt = gather_finish(0, gather_start(0, x), x)
    for layer in range(4):
        h, sv = _layer_forward(layer, h, w, tables, None if nxt is None else nxt[0][4])
        saved.append(sv)
        if nxt is not None:
            nxt = gather_finish(layer + 1, nxt, h)
    loss_part, dh, dh_bf, g_final = _loss_head(h, w['final_norm'], target)
    loss = lax.psum(loss_part, ("x", "y", "c"))

    grad, delta, new_m, new_v = {}, {}, {}, {}
    small_grads = {'final_norm': g_final}
    norm_grads = {'norm1': [None] * 4, 'norm2': [None] * 4}
    mlp_upd = {'mlp_w_up': None, 'mlp_w_down': None}

    def update_layer(layer, started, after):
        red = _reduce_scatter_finish(started, str(layer), chip_arr, after)
        for n, (mine, theirs) in ((_W_IN[layer], red[0]), (_W_OUT[layer], red[1])):
            grad[n], delta[n], new_m[n], new_v[n] = _adamw_halves(a[n], a['m_' + n], a['v_' + n], 0, mine, theirs,
                                                                  core_arr, f"adamw_{n}")
        for n, (mine, theirs) in (('mlp_w_up', red[2]), ('mlp_w_down', red[3])):
            mlp_upd[n] = _adamw_halves(a[n], a['m_' + n], a['v_' + n], layer, mine, theirs, core_arr,
                                       f"adamw_{n}_{layer}", carried=mlp_upd[n])

    scatters = {}
    token = None
    for layer in reversed(range(4)):
        dh, dh_bf, g = _layer_backward(layer, dh, dh_bf, w, tables, saved[layer], token)
        saved[layer] = None
        scatters[layer] = _reduce_scatter_begin([g[_W_IN[layer]], g[_W_OUT[layer]], g['mlp_w_up'], g['mlp_w_down']],
                                                str(layer), core_arr)
        token = scatters[layer][4]
        norm_grads['norm1'][layer], norm_grads['norm2'][layer] = g['norm1'], g['norm2']
        for n in _SMALL:
            if n in g:
                small_grads[n] = g[n]
    small_grads['norm1'] = jnp.stack(norm_grads['norm1'])
    small_grads['norm2'] = jnp.stack(norm_grads['norm2'])

    full_shapes = [small_grads[n].shape for n in _SMALL]
    small_part = _pack([small_grads[n] for n in _SMALL])
    sibling_part = _stream_to_sibling([small_part[None]], False, "small_swap")[0][0]
    chip_part = _sum_slabs(jnp.stack([small_part, sibling_part]), F32, "sum_small_cores")
    small_started = _chip_exchange_start('gather', [chip_part], [(N_CHIPS,) + chip_part.shape], [False],
                                         "small_gather_start", core_arr)
    after = [token]
    for layer in (3, 2, 1):
        update_layer(layer, scatters[layer], after)
        after = [new_v[_W_IN[layer]], new_v[_W_OUT[layer]], mlp_upd['mlp_w_up'][3], mlp_upd['mlp_w_down'][3]]
    own, lands = _chip_exchange_wait('gather', small_started, [False], "small_gather_wait", after)
    total = _sum_slabs(lax.dynamic_update_slice_in_dim(lands[0], own[0][None], chip, axis=0), F32, "sum_small_chips")
    local_g = {}
    for n, full in zip(_SMALL, _unpack(total, full_shapes)):
        shp = a[n].shape
        if n in _SMALL_SHARDED:
            full = full.reshape(full.shape[0], -1)
            cq = shp[-1]
            full = lax.dynamic_slice_in_dim(full, chip * cq, cq, axis=1)
        elif n in ('gdn_a_log', 'gdn_dt_bias'):
            full = full[:, :shp[-1]]
        local_g[n] = full.reshape(shp)
    shapes = [a[n].shape for n in _SMALL]
    packed = [_pack([src[n] for n in _SMALL]) for src in
              (a, local_g, {n: a['m_' + n] for n in _SMALL}, {n: a['v_' + n] for n in _SMALL})]
    upd = _adamw(*packed, "adamw_small")
    for n, gr, dl, nm, nv in zip(_SMALL, _unpack(packed[1], shapes), *[_unpack(u, shapes) for u in upd]):
        grad[n], delta[n], new_m[n], new_v[n] = gr, dl, nm, nv
    update_layer(0, scatters[0], upd[0])
    for n in ('mlp_w_up', 'mlp_w_down'):
        grad[n], delta[n], new_m[n], new_v[n] = mlp_upd[n]

    out = [loss, dh.reshape(a['x'].shape)]
    for group in (grad, delta, new_m, new_v):
        out += [group[n].reshape(a[n].shape) for n in _WEIGHTS]
    return tuple(out)
```

```python
import functools
import math

import jax
import jax.numpy as jnp
from jax import lax
from jax.experimental import pallas as pl
from jax.experimental.pallas import tpu as pltpu

F32 = jnp.float32
BF = jnp.bfloat16
MESH = pl.DeviceIdType.MESH

NORM_EPS = 1e-6
CHUNK = 64
RET_HEADS = 8
GDN_HEADS = 16
GLA_HEADS = 4
GLA_RANK = 16
GLA_TAU = 16.0
LRU_BLOCKS = 16
LRU_C = 8.0
CONV_WIDTH = 4
ROPE_BASE = 10000.0
N_CHIPS = 4
N_DEV = 8

ADAM_LR = 0.001
ADAM_B1 = 0.9
ADAM_B2 = 0.999
ADAM_EPS = 1e-08
ADAM_WD = 0.01
ADAM_STEP = 10

VMEM_LIMIT_BYTES = 56 * 1024 * 1024
TOKEN_BLOCK = 512
ROW_BLOCK = 256
GDN_GROUP = 16


def _cparams(**kw):
    return pltpu.CompilerParams(vmem_limit_bytes=VMEM_LIMIT_BYTES, **kw)


def _bf16_parts(x, n):
    parts = []
    for _ in range(n - 1):
        p = x.astype(BF)
        parts.append(p)
        x = x - p.astype(F32)
    return parts + [x.astype(BF)]


def _raw_mm(a, b, ta, tb, hi):
    nb = a.ndim - 2
    batch = tuple(range(nb))
    dims = (((nb + (0 if ta else 1),), (nb + (1 if tb else 0),)), (batch, batch))

    def dot(p, q):
        return lax.dot_general(p, q, dims, preferred_element_type=F32)

    if not hi:
        return dot(a.astype(BF), b.astype(BF))
    if hi == 'l':
        ae = a.astype(BF)
        b0, b1, b2 = _bf16_parts(b.astype(F32), 3)
        return dot(ae, b0) + (dot(ae, b1) + dot(ae, b2))
    if hi == 'r':
        be = b.astype(BF)
        a0, a1, a2 = _bf16_parts(a.astype(F32), 3)
        return dot(a0, be) + (dot(a1, be) + dot(a2, be))
    a0, a1 = _bf16_parts(a.astype(F32), 2)
    b0, b1 = _bf16_parts(b.astype(F32), 2)
    return dot(a0, b0) + (dot(a0, b1) + dot(a1, b0))


@functools.partial(jax.custom_vjp, nondiff_argnums=(2, 3, 4))
def _mm_vjp(a, b, ta, tb, hi):
    return _raw_mm(a, b, ta, tb, hi)


def _mm_vjp_fwd(a, b, ta, tb, hi):
    return _raw_mm(a, b, ta, tb, hi), (a, b)


def _mm_vjp_bwd(ta, tb, hi, res, g):
    a, b = res
    if ta:
        da = _raw_mm(b, g, tb, True, 'l' if hi == 'r' else bool(hi))
    else:
        da = _raw_mm(g, b, False, not tb, 'r' if hi == 'r' else bool(hi))
    if tb:
        db = _raw_mm(g, a, True, ta, 'r' if hi == 'l' else bool(hi))
    else:
        db = _raw_mm(a, g, not ta, False, 'l' if hi == 'l' else bool(hi))
    return da, db


_mm_vjp.defvjp(_mm_vjp_fwd, _mm_vjp_bwd)


def _mm_diff(a, b, ta=False, tb=False, hi=False):
    return _mm_vjp(a, b, ta, tb, hi)


def _mm_plain(a, b, ta=False, tb=False, hi=False):
    return _raw_mm(a, b, ta, tb, hi)


def _shift_rows(x, k, up):
    if k == 0:
        return x
    n = x.shape[0]
    rows = lax.broadcasted_iota(jnp.int32, x.shape, 0)
    if up:
        return jnp.where(rows < n - k, pltpu.roll(x, n - k, 0), 0.0)
    return jnp.where(rows >= k, pltpu.roll(x, k, 0), 0.0)


@functools.partial(jax.custom_vjp, nondiff_argnums=(1, 2))
def _shift_vjp(x, k, up):
    return _shift_rows(x, k, up)


def _shift_vjp_fwd(x, k, up):
    return _shift_rows(x, k, up), None


def _shift_vjp_bwd(k, up, _, g):
    return (_shift_rows(g, k, not up),)


_shift_vjp.defvjp(_shift_vjp_fwd, _shift_vjp_bwd)


def _sigmoid(x):
    return 1.0 / (1.0 + jnp.exp(-x))


def _silu(x):
    return x * _sigmoid(x)


def _softplus(x):
    return jnp.maximum(x, 0.0) + jnp.log(1.0 + jnp.exp(-jnp.abs(x)))


def _gelu_tanh(x):
    return 0.5 * x * (1.0 + jnp.tanh(math.sqrt(2.0 / math.pi) * (x + 0.044715 * (x * x * x))))


def _expm1(x):
    series = x * (1.0 + x * (0.5 + x * (1.0 / 6.0 + x * (1.0 / 24.0))))
    return jnp.where(jnp.abs(x) < 0.03, series, jnp.exp(x) - 1.0)


def _rmsnorm(x, g):
    return x * lax.rsqrt(jnp.mean(x * x, axis=-1, keepdims=True) + NORM_EPS) * g


def _head_norm(o, gain, center):
    if center:
        o = o - jnp.mean(o, axis=-1, keepdims=True)
    return o * lax.rsqrt(jnp.mean(o * o, axis=-1, keepdims=True) + NORM_EPS) * gain


def _l2norm(x):
    return x * lax.rsqrt(jnp.sum(x * x, axis=-1, keepdims=True) + NORM_EPS)


def _iota2(shape, dim):
    return lax.broadcasted_iota(jnp.int32, shape, dim)


def _tri_ones(n, upper=False):
    i, j = _iota2((n, n), 0), _iota2((n, n), 1)
    return jnp.where((i <= j) if upper else (j <= i), 1.0, 0.0).astype(F32)


def _linscan(a, u, rev):
    n = a.shape[0]
    rows = _iota2(a.shape, 0)
    d = 1
    while d < n:
        if rev:
            valid = rows < n - d
            a_s, u_s = pltpu.roll(a, n - d, 0), pltpu.roll(u, n - d, 0)
        else:
            valid = rows >= d
            a_s, u_s = pltpu.roll(a, d, 0), pltpu.roll(u, d, 0)
        u = a * jnp.where(valid, u_s, 0.0) + u
        a = a * jnp.where(valid, a_s, 1.0)
        d *= 2
    return u


def _ret_group(mm, q, k, v, g, gain, state, cos, sin, dintra, qdec, kdec, cdec):
    n, _, dk = q.shape
    half = dk // 2

    def rot(t):
        t1, t2 = t[..., :half], t[..., half:]
        return jnp.concatenate([t1 * cos - t2 * sin, t1 * sin + t2 * cos], axis=-1)

    qr = rot(q)
    kr = rot(k) * (dk ** -0.5)
    o_intra = mm(mm(qr, kr, tb=True) * dintra, v)
    kv = mm(kr * kdec, v, ta=True)
    q_in = qr * qdec
    ys = []
    for i in range(n):
        o = o_intra[i] + mm(q_in[i], state)
        state = state * cdec + kv[i]
        ys.append(_head_norm(o, gain, True) * _silu(g[i]))
    return ys, state


def _gla_group(mm, q, k, v, r, glow, wgu, bias, gain, state_t):
    n, c, dk = q.shape
    logit = mm(glow.reshape(n * c, glow.shape[-1]), wgu).reshape(n, c, dk) + bias
    la = -_softplus(-logit) * (1.0 / GLA_TAU)
    cum = mm(jnp.broadcast_to(_tri_ones(c), (n, c, c)), la, hi='l')
    rows = lax.broadcasted_iota(jnp.int32, la.shape, 1)
    ref = jnp.sum(jnp.where(rows < c // 2, la, 0.0), axis=1, keepdims=True)
    tot = jnp.sum(la, axis=1, keepdims=True)
    fwd, bwd = jnp.exp(cum - ref), jnp.exp(ref - cum)
    qs = q * (dk ** -0.5)
    s_lo = mm(qs * fwd, k * bwd, tb=True)
    s_up = mm(qs * bwd, k * fwd, tb=True)
    i, j = _iota2((c, c), 0), _iota2((c, c), 1)
    o_intra = mm(jnp.where(i >= j, s_lo, s_up), v)
    q_in = qs * jnp.exp(cum)
    kv_t = mm(v, k * jnp.exp(tot - cum), ta=True)
    dec = jnp.exp(tot)
    ys = []
    for m in range(n):
        o = o_intra[m] + mm(q_in[m], state_t, tb=True)
        state_t = state_t * dec[m] + kv_t[m]
        ys.append(_head_norm(o, gain, False) * _silu(r[m]))
    return ys, state_t


def _gdn_group(mm, q, k, v, z, ba, a_log, dt_bias, gain, state, head):
    g, c, dk = q.shape
    lanes = lax.broadcasted_iota(jnp.int32, (1, 1, ba.shape[-1]), 2)
    oh_b = jnp.where(lanes == head, 1.0, 0.0).astype(F32)
    oh_a = jnp.where(lanes == head + GDN_HEADS, 1.0, 0.0).astype(F32)
    beta = _sigmoid(jnp.sum(ba * oh_b, axis=-1, keepdims=True))
    a_logit = jnp.sum(ba * oh_a, axis=-1, keepdims=True)
    a_h = jnp.sum(a_log * oh_b[0], axis=-1, keepdims=True)
    dt_h = jnp.sum(dt_bias * oh_b[0], axis=-1, keepdims=True)
    la = -jnp.exp(a_h) * _softplus(a_logit + dt_h)
    q = _l2norm(q) * (dk ** -0.5)
    k = _l2norm(k)
    tri = jnp.broadcast_to(_tri_ones(c), (g, c, c))
    tri_up = jnp.broadcast_to(_tri_ones(c, upper=True), (g, c, c))
    cum_k = mm(tri, la * jnp.ones((g, c, dk), F32), hi='l')
    la_sq = la * jnp.ones((g, c, c), F32)
    cum_i = mm(tri, la_sq, hi='l')
    cum_j = mm(la_sq, tri_up, ta=True, hi='r')
    i, j = _iota2((c, c), 0), _iota2((c, c), 1)
    strict = i > j
    rel = jnp.where(strict, jnp.exp(jnp.where(strict, cum_i - cum_j, 0.0)), 0.0)
    a_mat = beta * rel * mm(k, k, tb=True)
    inv = jnp.where(i == j, 1.0, 0.0).astype(F32) - a_mat
    power = a_mat
    for _ in range(int(math.log2(c)) - 1):
        power = mm(power, power, hi=True)
        inv = inv + mm(inv, power, hi=True)
    tot = jnp.sum(la, axis=1, keepdims=True)
    u = mm(inv, beta * v, hi=True)
    w = mm(inv, (beta * jnp.exp(cum_k)) * k, hi=True)
    k_end = k * jnp.exp(tot - cum_k)
    di, dj = _iota2((dk, dk), 0), _iota2((dk, dk), 1)
    trans = jnp.exp(tot) * jnp.where(di == dj, 1.0, 0.0).astype(F32) - mm(k_end, w, ta=True)
    inject = mm(k_end, u, ta=True)
    ys = []
    for n in range(g):
        state = mm(trans[n], state) + inject[n]
        ys.append(_head_norm(mm(q[n], state), gain, False) * _silu(z[n]))
    return ys, state


def _conv_taps(shift, x, taps):
    out = None
    for tap, w in enumerate(taps):
        term = shift(x, CONV_WIDTH - 1 - tap, False) * w
        out = term if out is None else out + term
    return out


def _lru_pre(mm, shift, xb, yb, taps, cb, lam, wr, br, wi, bi):
    xb = _conv_taps(shift, xb, taps) + cb
    r = _sigmoid(mm(xb, wr) + br)
    i = _sigmoid(mm(xb, wi) + bi)
    log_a = (-LRU_C) * _softplus(-lam) * r
    a = jnp.exp(log_a)
    u = jnp.sqrt(-_expm1(2.0 * log_a)) * (i * xb)
    return a, u, _gelu_tanh(yb)


_TOKEN_SPEC = pl.BlockSpec((8, 128), lambda *_: (0, 0))


MATMUL_VMEM_BUDGET = 40 * 1024 * 1024
_TILE_SIZES = (3072, 2048, 1536, 1024, 768, 512, 384, 256, 128)


def _tile_choices(extent):
    return [t for t in _TILE_SIZES if t <= extent and extent % t == 0] or [extent]


def _matmul_tiles(m, n, kdim, n_unit, k_unit, a_item, b_item, o_item, e_item):
    best = None
    for bm in _tile_choices(m):
        for bn in _tile_choices(n_unit):
            for bk in _tile_choices(k_unit):
                nk = kdim // bk
                vmem = 2 * (bm * bk * a_item + bk * bn * b_item + bm * bn * (o_item + e_item)) + bm * bn * 4 * (2 if nk > 1 else 1)
                if vmem > MATMUL_VMEM_BUDGET:
                    continue
                steps = (m // bm) * (n // bn) * nk
                traffic = m * kdim * a_item * (1 if nk == 1 else n // bn) + kdim * n * b_item * (m // bm)
                if best is None or (steps, traffic) < best[0]:
                    best = ((steps, traffic), (bm, bn, bk))
    return best[1]


def _matmul(a, b, *, name, ta=False, tb=False, out_dtype=F32, b_sharded=False, o_sharded=False,
            epilogue=None, extra=None, token=None):
    m, kdim = (a.shape[1], a.shape[0]) if ta else a.shape
    if b_sharded:
        nq = b.shape[2]
        n = b.shape[1] if tb else N_CHIPS * nq
        assert kdim == (N_CHIPS * nq if tb else b.shape[1])
    else:
        n = b.shape[0] if tb else b.shape[1]
        assert kdim == (b.shape[1] if tb else b.shape[0])
    n_unit = b.shape[2] if (b_sharded and not tb) else (n // N_CHIPS if o_sharded else n)
    k_unit = b.shape[2] if (b_sharded and tb) else kdim
    bm, bn, bk = _matmul_tiles(m, n, kdim, n_unit, k_unit, a.dtype.itemsize, b.dtype.itemsize,
                               jnp.dtype(out_dtype).itemsize, 0 if extra is None else extra.dtype.itemsize)
    nk = kdim // bk
    grid = (m // bm, n // bn, nk)

    a_spec = pl.BlockSpec((bk, bm), lambda i, j, k: (k, i)) if ta else pl.BlockSpec((bm, bk), lambda i, j, k: (i, k))
    if b_sharded and not tb:
        per = b.shape[2] // bn
        b_spec = pl.BlockSpec((None, bk, bn), lambda i, j, k: (j // per, k, j % per))
    elif b_sharded:
        per = b.shape[2] // bk
        b_spec = pl.BlockSpec((None, bn, bk), lambda i, j, k: (k // per, j, k % per))
    elif tb:
        b_spec = pl.BlockSpec((bn, bk), lambda i, j, k: (j, k))
    else:
        b_spec = pl.BlockSpec((bk, bn), lambda i, j, k: (k, j))
    if o_sharded:
        per_o = (n // N_CHIPS) // bn
        o_spec = pl.BlockSpec((None, bm, bn), lambda i, j, k: (j // per_o, i, j % per_o))
        out_shape = jax.ShapeDtypeStruct((N_CHIPS, m, n // N_CHIPS), out_dtype)
    else:
        o_spec = pl.BlockSpec((bm, bn), lambda i, j, k: (i, j))
        out_shape = jax.ShapeDtypeStruct((m, n), out_dtype)
    in_specs = [a_spec, b_spec]
    operands = [a, b]
    if extra is not None:
        in_specs.append(pl.BlockSpec((bm, bn), lambda i, j, k: (i, j)))
        operands.append(extra)
    if token is not None:
        in_specs.append(_TOKEN_SPEC)
        operands.append(token)

    def finish(acc, e_ref, o_ref):
        if epilogue == 'add':
            acc = acc + e_ref[...].astype(F32)
        elif epilogue == 'relu2':
            acc = jnp.square(jnp.maximum(acc, 0.0))
        elif epilogue == 'dact':
            acc = acc * (2.0 * jnp.sqrt(e_ref[...].astype(F32)))
        o_ref[...] = acc.astype(o_ref.dtype)

    def body_one_step(*refs):
        finish(_raw_mm(refs[0][...], refs[1][...], ta, tb, False), refs[2] if extra is not None else None, refs[-1])

    def body(*refs):
        a_ref, b_ref = refs[0], refs[1]
        e_ref = refs[2] if extra is not None else None
        o_ref, acc_ref = refs[-2], refs[-1]
        k = pl.program_id(2)

        @pl.when(k == 0)
        def _():
            acc_ref[...] = jnp.zeros_like(acc_ref)

        acc_ref[...] += _raw_mm(a_ref[...], b_ref[...], ta, tb, False)

        @pl.when(k == nk - 1)
        def _():
            finish(acc_ref[...], e_ref, o_ref)

    return pl.pallas_call(
        body_one_step if nk == 1 else body, name=name, grid=grid, in_specs=in_specs, out_specs=o_spec,
        out_shape=out_shape, scratch_shapes=[] if nk == 1 else [pltpu.VMEM((bm, bn), F32)],
        compiler_params=_cparams(),
    )(*operands)


def _rmsnorm_fwd(x, g, name, token=None):
    s, d = x.shape

    def body(x_ref, g_ref, *rest):
        rest[-1][...] = _rmsnorm(x_ref[...], g_ref[...]).astype(BF)

    return pl.pallas_call(
        body, name=name, grid=(s // ROW_BLOCK,),
        in_specs=[pl.BlockSpec((ROW_BLOCK, d), lambda i: (i, 0)), pl.BlockSpec((1, d), lambda i: (0, 0))]
        + ([] if token is None else [_TOKEN_SPEC]),
        out_specs=pl.BlockSpec((ROW_BLOCK, d), lambda i: (i, 0)),
        out_shape=jax.ShapeDtypeStruct((s, d), BF), compiler_params=_cparams(),
    )(x, g.reshape(1, d), *([] if token is None else [token]))


def _rmsnorm_bwd(x, g, dh, dres, name):
    s, d = x.shape

    def body(x_ref, g_ref, dh_ref, dres_ref, dx_ref, dxb_ref, dg_ref):
        _, vjp = jax.vjp(_rmsnorm, x_ref[...], g_ref[...])
        dx, dg = vjp(dh_ref[...].astype(F32))
        dx = dres_ref[...] + dx
        dx_ref[...] = dx
        dxb_ref[...] = dx.astype(BF)

        @pl.when(pl.program_id(0) == 0)
        def _():
            dg_ref[...] = jnp.zeros_like(dg_ref)

        dg_ref[...] += dg

    row = pl.BlockSpec((ROW_BLOCK, d), lambda i: (i, 0))
    vec = pl.BlockSpec((1, d), lambda i: (0, 0))
    dx, dxb, dg = pl.pallas_call(
        body, name=name, grid=(s // ROW_BLOCK,), in_specs=[row, vec, row, row], out_specs=[row, row, vec],
        out_shape=[jax.ShapeDtypeStruct((s, d), F32), jax.ShapeDtypeStruct((s, d), BF),
                   jax.ShapeDtypeStruct((1, d), F32)],
        compiler_params=_cparams(),
    )(x, g.reshape(1, d), dh, dres)
    return dx, dxb, dg.reshape(d)


def _loss_head(x, g, target):
    s, d = x.shape

    def loss_fn(xv, gv, tv):
        err = _rmsnorm(xv, gv) - tv
        return 0.5 * jnp.sum(jnp.mean(err * err, axis=-1, keepdims=True), axis=0, keepdims=True)

    def body(x_ref, g_ref, t_ref, dx_ref, dxb_ref, dg_ref, loss_ref):
        tv = t_ref[...]
        loss, vjp = jax.vjp(lambda xv, gv: loss_fn(xv, gv, tv), x_ref[...], g_ref[...])
        dx, dg = vjp(jnp.ones((1, 1), F32))
        dx_ref[...] = dx
        dxb_ref[...] = dx.astype(BF)

        @pl.when(pl.program_id(0) == 0)
        def _():
            dg_ref[...] = jnp.zeros_like(dg_ref)
            loss_ref[...] = jnp.zeros_like(loss_ref)

        dg_ref[...] += dg
        loss_ref[...] += loss * jnp.ones_like(loss_ref)

    row = pl.BlockSpec((ROW_BLOCK, d), lambda i: (i, 0))
    vec = pl.BlockSpec((1, d), lambda i: (0, 0))
    dx, dxb, dg, loss = pl.pallas_call(
        body, name="loss_head", grid=(s // ROW_BLOCK,), in_specs=[row, vec, row],
        out_specs=[row, row, vec, pl.BlockSpec((1, 128), lambda i: (0, 0))],
        out_shape=[jax.ShapeDtypeStruct((s, d), F32), jax.ShapeDtypeStruct((s, d), BF),
                   jax.ShapeDtypeStruct((1, d), F32), jax.ShapeDtypeStruct((1, 128), F32)],
        compiler_params=_cparams(),
    )(x, g.reshape(1, d), target)
    return loss[0, 0], dx, dxb, dg.reshape(d)


def _ret_tables(s, dk):
    h = jnp.arange(RET_HEADS, dtype=F32)
    log_gamma = jnp.log1p(-jnp.exp2(-5.0 - h))
    pos = jnp.arange(CHUNK, dtype=F32)
    dist = jnp.abs(pos[:, None] - pos[None, :])
    dintra = jnp.exp(log_gamma[:, None, None] * dist)
    qdec = jnp.exp(log_gamma[:, None] * (pos + 1.0))[:, :, None]
    kdec = jnp.exp(log_gamma[:, None] * (CHUNK - 1.0 - pos))[:, :, None]
    cdec = jnp.exp(log_gamma * CHUNK)[:, None, None]
    inv = ROPE_BASE ** (-jnp.arange(0, dk, 2, dtype=F32) / dk)
    ang = jnp.arange(s, dtype=F32)[:, None] * inv[None, :]
    return jnp.cos(ang), jnp.sin(ang), dintra, qdec, kdec, cdec


def _ret_specs(s, dk, dv, tb, rev):
    nh = RET_HEADS
    nb = s // tb
    bi = (lambda b: nb - 1 - b) if rev else (lambda b: b)
    voff = 2 * nh * dk // dv
    cpb = tb // CHUNK
    return dict(
        q=pl.BlockSpec((tb, dk), lambda h, b: (bi(b), h)),
        k=pl.BlockSpec((tb, dk), lambda h, b: (bi(b), nh + h)),
        v=pl.BlockSpec((tb, dv), lambda h, b: (bi(b), voff + h)),
        g=pl.BlockSpec((tb, dv), lambda h, b: (bi(b), voff + nh + h)),
        gain=pl.BlockSpec((None, 1, dv), lambda h, b: (h, 0, 0)),
        cs=pl.BlockSpec((tb, dk // 2), lambda h, b: (bi(b), 0)),
        dintra=pl.BlockSpec((None, CHUNK, CHUNK), lambda h, b: (h, 0, 0)),
        dec=pl.BlockSpec((None, CHUNK, 1), lambda h, b: (h, 0, 0)),
        cdec=pl.BlockSpec((None, 1, 1), lambda h, b: (h, 0, 0)),
        hv=pl.BlockSpec((tb, dv), lambda h, b: (bi(b), h)),
        hk=pl.BlockSpec((tb, dk), lambda h, b: (bi(b), h)),
        st=pl.BlockSpec((None, None, dk, dv), lambda h, b: (h, bi(b), 0, 0)),
    )


def _group(t):
    return t.reshape(t.shape[0] // CHUNK, CHUNK, t.shape[-1])


def _ret_fwd(proj, gain, tables):
    s = proj.shape[0]
    d = proj.shape[1] // 6
    dk, dv = d // RET_HEADS, 2 * d // RET_HEADS
    tb = min(TOKEN_BLOCK, s)
    sp = _ret_specs(s, dk, dv, tb, False)
    cos, sin, dintra, qdec, kdec, cdec = tables

    def body(q_ref, k_ref, v_ref, g_ref, gain_ref, cos_ref, sin_ref, di_ref, qd_ref, kd_ref, cd_ref,
             y_ref, st_ref, state):
        @pl.when(pl.program_id(1) == 0)
        def _():
            state[...] = jnp.zeros_like(state)

        st_ref[...] = state[...]
        ys, new_state = _ret_group(_mm_plain, _group(q_ref[...]), _group(k_ref[...]), _group(v_ref[...]),
                                   _group(g_ref[...]), gain_ref[...], state[...], _group(cos_ref[...]),
                                   _group(sin_ref[...]), di_ref[...], qd_ref[...], kd_ref[...], cd_ref[...])
        for c, y in enumerate(ys):
            y_ref[pl.ds(c * CHUNK, CHUNK), :] = y.astype(BF)
        state[...] = new_state

    return pl.pallas_call(
        body, name="ret_fwd", grid=(RET_HEADS, s // tb),
        in_specs=[sp['q'], sp['k'], sp['v'], sp['g'], sp['gain'], sp['cs'], sp['cs'], sp['dintra'], sp['dec'],
                  sp['dec'], sp['cdec']],
        out_specs=[sp['hv'], sp['st']],
        out_shape=[jax.ShapeDtypeStruct((s, RET_HEADS * dv), BF),
                   jax.ShapeDtypeStruct((RET_HEADS, s // tb, dk, dv), F32)],
        scratch_shapes=[pltpu.VMEM((dk, dv), F32)], compiler_params=_cparams(),
    )(proj, proj, proj, proj, gain.reshape(RET_HEADS, 1, dv), cos, sin, dintra, qdec, kdec, cdec)


def _ret_bwd(proj, gain, tables, states, dy):
    s = proj.shape[0]
    d = proj.shape[1] // 6
    dk, dv = d // RET_HEADS, 2 * d // RET_HEADS
    tb = min(TOKEN_BLOCK, s)
    cpb = tb // CHUNK
    sp = _ret_specs(s, dk, dv, tb, True)
    cos, sin, dintra, qdec, kdec, cdec = tables

    def body(q_ref, k_ref, v_ref, g_ref, gain_ref, cos_ref, sin_ref, di_ref, qd_ref, kd_ref, cd_ref, st_ref,
             dy_ref, dq_ref, dk_ref, dv_ref, dg_ref, dgain_ref, dstate):
        @pl.when(pl.program_id(1) == 0)
        def _():
            dstate[...] = jnp.zeros_like(dstate)
            dgain_ref[...] = jnp.zeros_like(dgain_ref)

        cos_g, sin_g = _group(cos_ref[...]), _group(sin_ref[...])
        di, qd, kd, cd = di_ref[...], qd_ref[...], kd_ref[...], cd_ref[...]

        def fn(q, k, v, g, gn, st):
            return _ret_group(_mm_diff, q, k, v, g, gn, st, cos_g, sin_g, di, qd, kd, cd)

        _, vjp = jax.vjp(fn, _group(q_ref[...]), _group(k_ref[...]), _group(v_ref[...]), _group(g_ref[...]),
                         gain_ref[...], st_ref[...])
        dys = [dy_ref[pl.ds(c * CHUNK, CHUNK), :].astype(F32) for c in range(cpb)]
        dq, dkk, dvv, dg, dgn, dst = vjp((dys, dstate[...]))
        dq_ref[...] = dq.reshape(tb, dk).astype(BF)
        dk_ref[...] = dkk.reshape(tb, dk).astype(BF)
        dv_ref[...] = dvv.reshape(tb, dv).astype(BF)
        dg_ref[...] = dg.reshape(tb, dv).astype(BF)
        dgain_ref[...] += dgn
        dstate[...] = dst

    dq, dkk, dvv, dg, dgain = pl.pallas_call(
        body, name="ret_bwd", grid=(RET_HEADS, s // tb),
        in_specs=[sp['q'], sp['k'], sp['v'], sp['g'], sp['gain'], sp['cs'], sp['cs'], sp['dintra'], sp['dec'],
                  sp['dec'], sp['cdec'], sp['st'], sp['hv']],
        out_specs=[sp['hk'], sp['hk'], sp['hv'], sp['hv'], sp['gain']],
        out_shape=[jax.ShapeDtypeStruct((s, RET_HEADS * dk), BF), jax.ShapeDtypeStruct((s, RET_HEADS * dk), BF),
                   jax.ShapeDtypeStruct((s, RET_HEADS * dv), BF), jax.ShapeDtypeStruct((s, RET_HEADS * dv), BF),
                   jax.ShapeDtypeStruct((RET_HEADS, 1, dv), F32)],
        scratch_shapes=[pltpu.VMEM((dk, dv), F32)], compiler_params=_cparams(),
    )(proj, proj, proj, proj, gain.reshape(RET_HEADS, 1, dv), cos, sin, dintra, qdec, kdec, cdec, states, dy)
    return jnp.concatenate([dq, dkk, dvv, dg], axis=1), dgain.reshape(RET_HEADS, dv)


def _gla_specs(s, dk, dv, tb, rev):
    nh = GLA_HEADS
    nb = s // tb
    bi = (lambda b: nb - 1 - b) if rev else (lambda b: b)
    voff = 2 * nh * dk // dv
    cpb = tb // CHUNK
    return dict(
        q=pl.BlockSpec((tb, dk), lambda h, b: (bi(b), h)),
        k=pl.BlockSpec((tb, dk), lambda h, b: (bi(b), nh + h)),
        v=pl.BlockSpec((tb, dv), lambda h, b: (bi(b), voff + h)),
        r=pl.BlockSpec((tb, dv), lambda h, b: (bi(b), voff + nh + h)),
        glow=pl.BlockSpec((tb, 128), lambda h, b: (bi(b), 0)),
        wgu=pl.BlockSpec((128, dk), lambda h, b: (0, h)),
        bias=pl.BlockSpec((1, dk), lambda h, b: (0, h)),
        gain=pl.BlockSpec((None, 1, dv), lambda h, b: (h, 0, 0)),
        hv=pl.BlockSpec((tb, dv), lambda h, b: (bi(b), h)),
        hk=pl.BlockSpec((tb, dk), lambda h, b: (bi(b), h)),
        hg=pl.BlockSpec((tb, 128), lambda h, b: (bi(b), h)),
        st=pl.BlockSpec((None, None, dv, dk), lambda h, b: (h, bi(b), 0, 0)),
    )


def _gla_fwd(proj, glow, wgu, bias, gain):
    s = proj.shape[0]
    d = proj.shape[1] // 3
    dk, dv = d // 2 // GLA_HEADS, d // GLA_HEADS
    tb = min(TOKEN_BLOCK, s)
    sp = _gla_specs(s, dk, dv, tb, False)

    def body(q_ref, k_ref, v_ref, r_ref, gl_ref, wgu_ref, b_ref, gain_ref, y_ref, st_ref, state):
        @pl.when(pl.program_id(1) == 0)
        def _():
            state[...] = jnp.zeros_like(state)

        st_ref[...] = state[...]
        ys, new_state = _gla_group(_mm_plain, _group(q_ref[...]), _group(k_ref[...]), _group(v_ref[...]),
                                   _group(r_ref[...]), _group(gl_ref[...]), wgu_ref[...], b_ref[...], gain_ref[...],
                                   state[...])
        for c, y in enumerate(ys):
            y_ref[pl.ds(c * CHUNK, CHUNK), :] = y.astype(BF)
        state[...] = new_state

    return pl.pallas_call(
        body, name="gla_fwd", grid=(GLA_HEADS, s // tb),
        in_specs=[sp['q'], sp['k'], sp['v'], sp['r'], sp['glow'], sp['wgu'], sp['bias'], sp['gain']],
        out_specs=[sp['hv'], sp['st']],
        out_shape=[jax.ShapeDtypeStruct((s, GLA_HEADS * dv), BF),
                   jax.ShapeDtypeStruct((GLA_HEADS, s // tb, dv, dk), F32)],
        scratch_shapes=[pltpu.VMEM((dv, dk), F32)], compiler_params=_cparams(),
    )(proj, proj, proj, proj, glow, wgu, bias, gain.reshape(GLA_HEADS, 1, dv))


def _gla_bwd(proj, glow, wgu, bias, gain, states, dy):
    s = proj.shape[0]
    d = proj.shape[1] // 3
    dk, dv = d // 2 // GLA_HEADS, d // GLA_HEADS
    tb = min(TOKEN_BLOCK, s)
    cpb = tb // CHUNK
    sp = _gla_specs(s, dk, dv, tb, True)

    def body(q_ref, k_ref, v_ref, r_ref, gl_ref, wgu_ref, b_ref, gain_ref, st_ref, dy_ref,
             dq_ref, dk_ref, dv_ref, dr_ref, dgl_ref, dwgu_ref, db_ref, dgain_ref, dstate):
        @pl.when(pl.program_id(1) == 0)
        def _():
            dstate[...] = jnp.zeros_like(dstate)
            dwgu_ref[...] = jnp.zeros_like(dwgu_ref)
            db_ref[...] = jnp.zeros_like(db_ref)
            dgain_ref[...] = jnp.zeros_like(dgain_ref)

        def fn(q, k, v, r, gl, w, b, gn, st):
            return _gla_group(_mm_diff, q, k, v, r, gl, w, b, gn, st)

        _, vjp = jax.vjp(fn, _group(q_ref[...]), _group(k_ref[...]), _group(v_ref[...]), _group(r_ref[...]),
                         _group(gl_ref[...]), wgu_ref[...], b_ref[...], gain_ref[...], st_ref[...])
        dys = [dy_ref[pl.ds(c * CHUNK, CHUNK), :].astype(F32) for c in range(cpb)]
        dq, dkk, dvv, dr, dgl, dw, db, dgn, dst = vjp((dys, dstate[...]))
        dq_ref[...] = dq.reshape(tb, dk).astype(BF)
        dk_ref[...] = dkk.reshape(tb, dk).astype(BF)
        dv_ref[...] = dvv.reshape(tb, dv).astype(BF)
        dr_ref[...] = dr.reshape(tb, dv).astype(BF)
        dgl_ref[...] = dgl.reshape(tb, dgl.shape[-1]).astype(BF)
        dwgu_ref[...] += dw
        db_ref[...] += db
        dgain_ref[...] += dgn
        dstate[...] = dst

    nh = GLA_HEADS
    dq, dkk, dvv, dr, dgl, dwgu, db, dgain = pl.pallas_call(
        body, name="gla_bwd", grid=(nh, s // tb),
        in_specs=[sp['q'], sp['k'], sp['v'], sp['r'], sp['glow'], sp['wgu'], sp['bias'], sp['gain'], sp['st'],
                  sp['hv']],
        out_specs=[sp['hk'], sp['hk'], sp['hv'], sp['hv'], sp['hg'], sp['wgu'], sp['bias'], sp['gain']],
        out_shape=[jax.ShapeDtypeStruct((s, nh * dk), BF), jax.ShapeDtypeStruct((s, nh * dk), BF),
                   jax.ShapeDtypeStruct((s, nh * dv), BF), jax.ShapeDtypeStruct((s, nh * dv), BF),
                   jax.ShapeDtypeStruct((s, nh * 128), BF), jax.ShapeDtypeStruct((128, nh * dk), F32),
                   jax.ShapeDtypeStruct((1, nh * dk), F32), jax.ShapeDtypeStruct((nh, 1, dv), F32)],
        scratch_shapes=[pltpu.VMEM((dv, dk), F32)], compiler_params=_cparams(),
    )(proj, proj, proj, proj, glow, wgu, bias, gain.reshape(nh, 1, dv), states, dy)
    return jnp.concatenate([dq, dkk, dvv, dr], axis=1), dgl, dwgu, db, dgain.reshape(nh, dv)


def _gdn_specs(s, dk):
    nh = GDN_HEADS
    col = lambda off: pl.BlockSpec((s, dk), lambda h: (0, off + h))
    tap = lambda off: pl.BlockSpec((CONV_WIDTH, dk), lambda h: (0, off + h))
    vec = pl.BlockSpec((1, 128), lambda h: (0, 0))
    return dict(q=col(0), k=col(nh), v=col(2 * nh), z=col(3 * nh), ba=pl.BlockSpec((s, 128), lambda h: (0, 0)),
                cq=tap(0), ck=tap(nh), cv=tap(2 * nh), vec=vec, gain=pl.BlockSpec((1, dk), lambda h: (0, 0)),
                head=col(0))


def _conv_silu(shift, x, taps):
    return _silu(_conv_taps(shift, x, taps))


def _load_taps(ref):
    return [ref[t:t + 1, :] for t in range(CONV_WIDTH)]


def _gdn_fwd(proj, ba, conv_w, a_log, dt_bias, gain):
    s = proj.shape[0]
    dk = proj.shape[1] // (4 * GDN_HEADS)
    sp = _gdn_specs(s, dk)
    rows = GDN_GROUP * CHUNK

    def grp(t):
        return t.reshape(GDN_GROUP, CHUNK, t.shape[-1])

    ngroups = s // rows

    def body(q_ref, k_ref, v_ref, z_ref, ba_ref, cq_ref, ck_ref, cv_ref, al_ref, dt_ref, gain_ref, y_ref, states,
             qc, kc, vc, state):
        head = pl.program_id(0)
        qc[...] = _conv_silu(_shift_rows, q_ref[...], _load_taps(cq_ref))
        kc[...] = _conv_silu(_shift_rows, k_ref[...], _load_taps(ck_ref))
        vc[...] = _conv_silu(_shift_rows, v_ref[...], _load_taps(cv_ref))
        state[...] = jnp.zeros_like(state)

        def step(n, carry):
            base = pl.multiple_of(n * rows, rows)
            sl = pl.ds(base, rows)
            states[n] = state[...]
            ys, new_state = _gdn_group(_mm_plain, grp(qc[sl, :]), grp(kc[sl, :]), grp(vc[sl, :]), grp(z_ref[sl, :]),
                                       grp(ba_ref[sl, :]), al_ref[...], dt_ref[...], gain_ref[...], state[...], head)
            for i, y in enumerate(ys):
                y_ref[pl.ds(base + i * CHUNK, CHUNK), :] = y.astype(BF)
            state[...] = new_state
            return carry

        lax.fori_loop(0, ngroups, step, 0)

    return pl.pallas_call(
        body, name="gdn_fwd", grid=(GDN_HEADS,),
        in_specs=[sp['q'], sp['k'], sp['v'], sp['z'], sp['ba'], sp['cq'], sp['ck'], sp['cv'], sp['vec'], sp['vec'],
                  sp['gain']],
        out_specs=[sp['head'], pl.BlockSpec((None, ngroups, dk, dk), lambda h: (h, 0, 0, 0))],
        out_shape=[jax.ShapeDtypeStruct((s, GDN_HEADS * dk), BF),
                   jax.ShapeDtypeStruct((GDN_HEADS, ngroups, dk, dk), F32)],
        scratch_shapes=[pltpu.VMEM((s, dk), F32)] * 3 + [pltpu.VMEM((dk, dk), F32)],
        compiler_params=_cparams(),
    )(proj, proj, proj, proj, ba, conv_w, conv_w, conv_w, a_log, dt_bias, gain)


def _gdn_bwd(proj, ba, conv_w, a_log, dt_bias, gain, group_states, dy):
    s = proj.shape[0]
    dk = proj.shape[1] // (4 * GDN_HEADS)
    sp = _gdn_specs(s, dk)
    rows = GDN_GROUP * CHUNK
    ngroups = s // rows

    def grp(t):
        return t.reshape(GDN_GROUP, CHUNK, t.shape[-1])

    def body(q_ref, k_ref, v_ref, z_ref, ba_ref, cq_ref, ck_ref, cv_ref, al_ref, dt_ref, gain_ref, states, dy_ref,
             dq_ref, dk_ref, dv_ref, dz_ref, dba_ref, dcq_ref, dck_ref, dcv_ref, dal_ref, ddt_ref, dgain_ref,
             qc, kc, vc, dqc, dkc, dvc, dstate):
        head = pl.program_id(0)

        @pl.when(head == 0)
        def _():
            dba_ref[...] = jnp.zeros_like(dba_ref)
            dal_ref[...] = jnp.zeros_like(dal_ref)
            ddt_ref[...] = jnp.zeros_like(ddt_ref)
            dgain_ref[...] = jnp.zeros_like(dgain_ref)

        qc[...] = _conv_silu(_shift_rows, q_ref[...], _load_taps(cq_ref))
        kc[...] = _conv_silu(_shift_rows, k_ref[...], _load_taps(ck_ref))
        vc[...] = _conv_silu(_shift_rows, v_ref[...], _load_taps(cv_ref))
        dstate[...] = jnp.zeros_like(dstate)

        def bstep(i, carry):
            n = ngroups - 1 - i
            base = pl.multiple_of(n * rows, rows)
            sl = pl.ds(base, rows)

            def fn(q, k, v, z, b, al, dt, gn, st):
                return _gdn_group(_mm_diff, q, k, v, z, b, al, dt, gn, st, head)

            _, vjp = jax.vjp(fn, grp(qc[sl, :]), grp(kc[sl, :]), grp(vc[sl, :]), grp(z_ref[sl, :]),
                             grp(ba_ref[sl, :]), al_ref[...], dt_ref[...], gain_ref[...], states[n])
            dys = [dy_ref[pl.ds(base + j * CHUNK, CHUNK), :].astype(F32) for j in range(GDN_GROUP)]
            dq, dkk, dvv, dz, db, dal, ddt, dgn, dst = vjp((dys, dstate[...]))
            dqc[sl, :] = dq.reshape(rows, dk)
            dkc[sl, :] = dkk.reshape(rows, dk)
            dvc[sl, :] = dvv.reshape(rows, dk)
            dz_ref[sl, :] = dz.reshape(rows, dk).astype(BF)
            dba_ref[sl, :] += db.reshape(rows, db.shape[-1])
            dal_ref[...] += dal
            ddt_ref[...] += ddt
            dgain_ref[...] += dgn
            dstate[...] = dst
            return carry

        lax.fori_loop(0, ngroups, bstep, 0)

        for x_ref, c_ref, dpost, dx_ref, dc_ref in ((q_ref, cq_ref, dqc, dq_ref, dcq_ref),
                                                    (k_ref, ck_ref, dkc, dk_ref, dck_ref),
                                                    (v_ref, cv_ref, dvc, dv_ref, dcv_ref)):
            _, vjp = jax.vjp(lambda x, *taps: _conv_silu(_shift_vjp, x, taps), x_ref[...], *_load_taps(c_ref))
            grads = vjp(dpost[...])
            dx_ref[...] = grads[0].astype(BF)
            for t in range(CONV_WIDTH):
                dc_ref[t:t + 1, :] = grads[1 + t]

    nh = GDN_HEADS
    col_bf = jax.ShapeDtypeStruct((s, nh * dk), BF)
    tap_out = jax.ShapeDtypeStruct((CONV_WIDTH, nh * dk), F32)
    tap_spec = pl.BlockSpec((CONV_WIDTH, dk), lambda h: (0, h))
    dq, dkk, dvv, dz, dba, dcq, dck, dcv, dal, ddt, dgain = pl.pallas_call(
        body, name="gdn_bwd", grid=(nh,),
        in_specs=[sp['q'], sp['k'], sp['v'], sp['z'], sp['ba'], sp['cq'], sp['ck'], sp['cv'], sp['vec'], sp['vec'],
                  sp['gain'], pl.BlockSpec((None, ngroups, dk, dk), lambda h: (h, 0, 0, 0)), sp['head']],
        out_specs=[sp['head']] * 4 + [sp['ba'], tap_spec, tap_spec, tap_spec, sp['vec'], sp['vec'], sp['gain']],
        out_shape=[col_bf] * 4 + [jax.ShapeDtypeStruct((s, 128), F32), tap_out, tap_out, tap_out,
                                  jax.ShapeDtypeStruct((1, 128), F32), jax.ShapeDtypeStruct((1, 128), F32),
                                  jax.ShapeDtypeStruct((1, dk), F32)],
        scratch_shapes=[pltpu.VMEM((s, dk), F32)] * 6 + [pltpu.VMEM((dk, dk), F32)],
        compiler_params=_cparams(),
    )(proj, proj, proj, proj, ba, conv_w, conv_w, conv_w, a_log, dt_bias, gain, group_states, dy)
    return (jnp.concatenate([dq, dkk, dvv, dz], axis=1), dba, jnp.concatenate([dcq, dck, dcv], axis=1), dal, ddt,
            dgain)


def _lru_specs(s, bw):
    nb = LRU_BLOCKS
    return dict(
        xb=pl.BlockSpec((s, bw), lambda n: (0, n)), yb=pl.BlockSpec((s, bw), lambda n: (0, nb + n)),
        taps=pl.BlockSpec((CONV_WIDTH, bw), lambda n: (0, n)), vec=pl.BlockSpec((1, bw), lambda n: (0, n)),
        w=pl.BlockSpec((None, bw, bw), lambda n: (n, 0, 0)), b=pl.BlockSpec((None, 1, bw), lambda n: (n, 0, 0)),
        col=pl.BlockSpec((s, bw), lambda n: (0, n)))


def _lru_fwd(proj, conv_w, conv_b, lam, wr, br, wi, bi):
    s = proj.shape[0]
    bw = proj.shape[1] // (2 * LRU_BLOCKS)
    sp = _lru_specs(s, bw)

    def body(xb_ref, yb_ref, cw_ref, cb_ref, lam_ref, wr_ref, br_ref, wi_ref, bi_ref, y_ref):
        a, u, gy = _lru_pre(_mm_plain, _shift_rows, xb_ref[...], yb_ref[...], _load_taps(cw_ref), cb_ref[...],
                            lam_ref[...], wr_ref[...], br_ref[...], wi_ref[...], bi_ref[...])
        y_ref[...] = (_linscan(a, u, False) * gy).astype(BF)

    return pl.pallas_call(
        body, name="lru_fwd", grid=(LRU_BLOCKS,),
        in_specs=[sp['xb'], sp['yb'], sp['taps'], sp['vec'], sp['vec'], sp['w'], sp['b'], sp['w'], sp['b']],
        out_specs=sp['col'], out_shape=jax.ShapeDtypeStruct((s, LRU_BLOCKS * bw), BF),
        compiler_params=_cparams(),
    )(proj, proj, conv_w, conv_b, lam, wr, br, wi, bi)


def _lru_bwd(proj, conv_w, conv_b, lam, wr, br, wi, bi, dy):
    s = proj.shape[0]
    nb = LRU_BLOCKS
    bw = proj.shape[1] // (2 * nb)
    sp = _lru_specs(s, bw)

    def body(xb_ref, yb_ref, cw_ref, cb_ref, lam_ref, wr_ref, br_ref, wi_ref, bi_ref, dy_ref,
             dxb_ref, dyb_ref, dcw_ref, dcb_ref, dlam_ref, dwr_ref, dbr_ref, dwi_ref, dbi_ref):
        def pre(xb, yb, t0, t1, t2, t3, cb, lm, w_r, b_r, w_i, b_i):
            return _lru_pre(_mm_diff, _shift_vjp, xb, yb, (t0, t1, t2, t3), cb, lm, w_r, b_r, w_i, b_i)

        (a, u, gy), vjp = jax.vjp(pre, xb_ref[...], yb_ref[...], *_load_taps(cw_ref), cb_ref[...], lam_ref[...],
                                  wr_ref[...], br_ref[...], wi_ref[...], bi_ref[...])
        h = _linscan(a, u, False)
        dout = dy_ref[...].astype(F32)
        g = _linscan(_shift_rows(a, 1, True), dout * gy, True)
        grads = vjp((g * _shift_rows(h, 1, False), g, dout * h))
        dxb_ref[...] = grads[0].astype(BF)
        dyb_ref[...] = grads[1].astype(BF)
        for t in range(CONV_WIDTH):
            dcw_ref[t:t + 1, :] = grads[2 + t]
        dcb_ref[...] = grads[6]
        dlam_ref[...] = grads[7]
        dwr_ref[...] = grads[8]
        dbr_ref[...] = grads[9]
        dwi_ref[...] = grads[10]
        dbi_ref[...] = grads[11]

    outs = pl.pallas_call(
        body, name="lru_bwd", grid=(nb,),
        in_specs=[sp['xb'], sp['yb'], sp['taps'], sp['vec'], sp['vec'], sp['w'], sp['b'], sp['w'], sp['b'], sp['col']],
        out_specs=[sp['col'], sp['col'], sp['taps'], sp['vec'], sp['vec'], sp['w'], sp['b'], sp['w'], sp['b']],
        out_shape=[jax.ShapeDtypeStruct((s, nb * bw), BF), jax.ShapeDtypeStruct((s, nb * bw), BF),
                   jax.ShapeDtypeStruct((CONV_WIDTH, nb * bw), F32), jax.ShapeDtypeStruct((1, nb * bw), F32),
                   jax.ShapeDtypeStruct((1, nb * bw), F32), jax.ShapeDtypeStruct((nb, bw, bw), F32),
                   jax.ShapeDtypeStruct((nb, 1, bw), F32), jax.ShapeDtypeStruct((nb, bw, bw), F32),
                   jax.ShapeDtypeStruct((nb, 1, bw), F32)],
        compiler_params=_cparams(),
    )(proj, proj, conv_w, conv_b, lam, wr, br, wi, bi, dy)
    return (jnp.concatenate([outs[0], outs[1]], axis=1),) + tuple(outs[2:])


ROW_STEP_BYTES = 12 * 1024 * 1024


def _row_tile(rows, cols, bytes_per_row_set):
    t = 8
    while t * 2 <= rows and rows % (t * 2) == 0 and (t * 2) * cols * bytes_per_row_set <= ROW_STEP_BYTES:
        t *= 2
    return t


def _sum_slabs(a, out_dtype, name):
    n, rows, cols = a.shape
    tr = _row_tile(rows, cols, 4 * (n + 1))

    def body(*refs):
        acc = refs[0][...].astype(F32)
        for r in refs[1:n]:
            acc = acc + r[...].astype(F32)
        refs[n][...] = acc.astype(out_dtype)

    specs = [pl.BlockSpec((None, tr, cols), functools.partial(lambda i, k: (k, i, 0), k=k)) for k in range(n)]
    return pl.pallas_call(
        body, name=name, grid=(rows // tr,), in_specs=specs, out_specs=pl.BlockSpec((tr, cols), lambda i: (i, 0)),
        out_shape=jax.ShapeDtypeStruct((rows, cols), out_dtype), compiler_params=_cparams(),
    )(*([a] * n))


def _sum_keep_and_landed(g, landed, core, name):
    four, hr, cols = landed.shape
    tr = _row_tile(hr, cols, 2 * 3)
    tph = hr // tr

    def body(core_ref, g_ref, l_ref, o_ref):
        o_ref[...] = (g_ref[...].astype(F32) + l_ref[...].astype(F32)).astype(o_ref.dtype)

    grid_spec = pltpu.PrefetchScalarGridSpec(
        num_scalar_prefetch=1, grid=(four, tph),
        in_specs=[pl.BlockSpec((None, tr, cols), lambda j, r, core_ref: (j, core_ref[0] * tph + r, 0)),
                  pl.BlockSpec((None, tr, cols), lambda j, r, core_ref: (j, r, 0))],
        out_specs=pl.BlockSpec((None, tr, cols), lambda j, r, core_ref: (j, r, 0)))
    return pl.pallas_call(body, name=name, grid_spec=grid_spec, out_shape=jax.ShapeDtypeStruct(landed.shape, g.dtype),
                          compiler_params=_cparams())(core, g, landed)


def _sum_chip_parts(parts, landed, chip, name):
    _, rows, cols = parts.shape
    tr = _row_tile(rows, cols, 2 * 4 + 4)

    def body(chip_ref, p_ref, l0_ref, l1_ref, l2_ref, o_ref):
        acc = l0_ref[...].astype(F32) + l1_ref[...].astype(F32)
        o_ref[...] = (acc + l2_ref[...].astype(F32)) + p_ref[...].astype(F32)

    slab = lambda k: pl.BlockSpec((None, tr, cols), lambda i, chip_ref: (k, i, 0))
    grid_spec = pltpu.PrefetchScalarGridSpec(
        num_scalar_prefetch=1, grid=(rows // tr,),
        in_specs=[pl.BlockSpec((None, tr, cols), lambda i, chip_ref: (chip_ref[0], i, 0)), slab(0), slab(1), slab(2)],
        out_specs=pl.BlockSpec((tr, cols), lambda i, chip_ref: (i, 0)))
    return pl.pallas_call(body, name=name, grid_spec=grid_spec, out_shape=jax.ShapeDtypeStruct((rows, cols), F32),
                          compiler_params=_cparams())(chip, parts, landed, landed, landed)


_ADAM_C1 = 1.0 / (1.0 - ADAM_B1 ** ADAM_STEP)
_ADAM_C2 = 1.0 / (1.0 - ADAM_B2 ** ADAM_STEP)


def _adamw_math(w, g, m, v):
    nm = ADAM_B1 * m + (1.0 - ADAM_B1) * g
    nv = ADAM_B2 * v + (1.0 - ADAM_B2) * (g * g)
    den = jnp.sqrt(nv * _ADAM_C2) + ADAM_EPS
    inv = pl.reciprocal(den, approx=True)
    inv = inv * (2.0 - den * inv)
    delta = -ADAM_LR * ((nm * _ADAM_C1) * inv + ADAM_WD * w)
    return delta, nm, nv


def _adamw(w, g, m, v, name):
    rows, cols = w.shape
    tr = _row_tile(rows, cols, 4 * 7)

    def body(w_ref, g_ref, m_ref, v_ref, d_ref, nm_ref, nv_ref):
        d_ref[...], nm_ref[...], nv_ref[...] = _adamw_math(w_ref[...], g_ref[...], m_ref[...], v_ref[...])

    spec = pl.BlockSpec((tr, cols), lambda i: (i, 0))
    shape = jax.ShapeDtypeStruct((rows, cols), F32)
    return pl.pallas_call(
        body, name=name, grid=(rows // tr,), in_specs=[spec] * 4, out_specs=[spec] * 3, out_shape=[shape] * 3,
        compiler_params=_cparams(),
    )(w, g, m, v)


def _adamw_halves(w, m, v, layer, mine, theirs, core, name, carried=None):
    n_layers, rows, cols = w.shape
    hr = mine.shape[0]
    tr = _row_tile(hr, cols, 4 * 9)
    tph = hr // tr

    def body(core_ref, w_ref, q_ref, t_ref, m_ref, v_ref, *rest):
        g_ref, d_ref, nm_ref, nv_ref = rest[-4:]
        is_mine = (pl.program_id(0) // tph) == core_ref[0]
        g = jnp.where(is_mine, q_ref[...], t_ref[...])
        g_ref[...] = g
        d_ref[...], nm_ref[...], nv_ref[...] = _adamw_math(w_ref[...], g, m_ref[...], v_ref[...])

    slab = pl.BlockSpec((None, tr, cols), lambda i, core_ref: (layer, i, 0))

    def mine_index(i, core_ref):
        return jnp.where(i // tph == core_ref[0], i % tph, jnp.where(core_ref[0] == 0, tph - 1, 0)), 0

    def theirs_index(i, core_ref):
        return jnp.where(i // tph == core_ref[0], jnp.where(core_ref[0] == 0, 0, tph - 1), i % tph), 0

    in_specs = [slab, pl.BlockSpec((tr, cols), mine_index), pl.BlockSpec((tr, cols), theirs_index), slab, slab]
    operands = [core, w, mine, theirs, m, v]
    aliases = {}
    if carried is not None:
        in_specs += [_ANY] * 4
        aliases = {len(operands) + k: k for k in range(4)}
        operands += list(carried)
    grid_spec = pltpu.PrefetchScalarGridSpec(num_scalar_prefetch=1, grid=(rows // tr,), in_specs=in_specs,
                                             out_specs=[slab] * 4)
    return pl.pallas_call(
        body, name=name, grid_spec=grid_spec, out_shape=[jax.ShapeDtypeStruct(w.shape, F32)] * 4,
        input_output_aliases=aliases, compiler_params=_cparams(),
    )(*operands)


_ANY = pl.BlockSpec(memory_space=pl.ANY)


def _mesh_pos():
    return lax.axis_index("x"), lax.axis_index("y"), lax.axis_index("c")


def _remote(src, dst, send_sems, recv_sems, k, dev):
    return pltpu.make_async_remote_copy(src_ref=src, dst_ref=dst, send_sem=send_sems.at[k], recv_sem=recv_sems.at[k],
                                        device_id=dev, device_id_type=MESH)


STREAM_CHUNK_BYTES = 2 * 1024 * 1024
STREAM_SLOTS = 4


def _chunk_rows(rows, cols, itemsize):
    assert rows % 16 == 0, rows
    t = 16
    while t * 2 <= rows and rows % (t * 2) == 0 and (t * 2) * cols * itemsize <= STREAM_CHUNK_BYTES:
        t *= 2
    return t


def _stream_chunks(n_chunks, src_at, dst_at, buf, load_sems, send_sems, recv_sem, sibling):
    def load(k, slot):
        return pltpu.make_async_copy(src_at(k), buf.at[slot], load_sems.at[slot])

    def send(k, slot):
        return pltpu.make_async_remote_copy(src_ref=buf.at[slot], dst_ref=dst_at(k), send_sem=send_sems.at[slot],
                                            recv_sem=recv_sem, device_id=sibling, device_id_type=MESH)

    def step(k, carry):
        slot = k % STREAM_SLOTS

        @pl.when(k >= STREAM_SLOTS)
        def _():
            send(k - STREAM_SLOTS, slot).wait_send()

        load(k, slot).start()

        @pl.when(k >= 1)
        def _():
            prev = (k - 1) % STREAM_SLOTS
            load(k - 1, prev).wait()
            send(k - 1, prev).start()

        return carry

    lax.fori_loop(0, n_chunks, step, 0)
    last = (n_chunks - 1) % STREAM_SLOTS
    load(n_chunks - 1, last).wait()
    send(n_chunks - 1, last).start()
    for k in range(max(0, n_chunks - STREAM_SLOTS), n_chunks):
        send(k, k % STREAM_SLOTS).wait_send()


_HBM = pl.BlockSpec(memory_space=pltpu.HBM)
_SEM = pl.BlockSpec(memory_space=pltpu.SEMAPHORE)
_DATAFLOW = pltpu.SideEffectType.DATAFLOW_SIDE_EFFECTING


def _chip_copies(kind, ins, lands, split, send_sems, recv_sems):
    x, y, c = _mesh_pos()
    me = 2 * x + y
    chips = [(1 - x, y), (x, 1 - y), (1 - x, 1 - y)]
    pairs = []
    for i in range(len(ins)):
        for j, chip in enumerate(chips):
            pj = 2 * chip[0] + chip[1]
            if kind == 'scatter':
                src, dst, got = ins[i].at[pj], lands[i].at[j], lands[i].at[j]
            elif split[i]:
                hr = ins[i].shape[0] // 2
                rows = pl.ds(c * hr, hr)
                src, dst, got = ins[i].at[rows], lands[i].at[me, rows], lands[i].at[pj, rows]
            else:
                src, dst, got = ins[i], lands[i].at[me], lands[i].at[pj]
            k = 3 * i + j
            pairs.append((_remote(src, dst, send_sems, recv_sems, k, (*chip, c)),
                          _remote(got, got, send_sems, recv_sems, k, (*chip, c))))
    return pairs


def _chip_exchange_start(kind, srcs, land_shapes, split, name, after):
    n = len(srcs)

    def body(*refs):
        ins, lands = refs[:n], refs[n:2 * n]
        send_sems, recv_sems = refs[2 * n + 1], refs[2 * n + 2]
        token = refs[-1]
        for send, _ in _chip_copies(kind, ins, lands, split, send_sems, recv_sems):
            send.start()
        token[...] = jnp.zeros_like(token)

    hbm = lambda t: pltpu.with_memory_space_constraint(t, pltpu.HBM)
    operands = [hbm(s) for s in srcs] + [hbm(lax.empty(shp, s.dtype)) for shp, s in zip(land_shapes, srcs)] + [after]
    out = pl.pallas_call(
        body, name=name, in_specs=[_HBM] * (2 * n) + [_ANY],
        out_specs=[_SEM, _SEM] + [_HBM] * (2 * n) + [pl.BlockSpec(memory_space=pltpu.VMEM)],
        out_shape=[pltpu.SemaphoreType.DMA((3 * n,)), pltpu.SemaphoreType.DMA((3 * n,))]
        + [pltpu.HBM(s.shape, s.dtype) for s in srcs] + [pltpu.HBM(shp, s.dtype) for shp, s in zip(land_shapes, srcs)]
        + [jax.ShapeDtypeStruct((8, 128), F32)],
        input_output_aliases={i: 2 + i for i in range(2 * n)},
        compiler_params=pltpu.CompilerParams(has_side_effects=_DATAFLOW),
    )(*operands)
    return out[0], out[1], out[2:2 + n], out[2 + n:2 + 2 * n], out[-1]


def _chip_exchange_wait(kind, started, split, name, after):
    send_sems, recv_sems, srcs, lands, _ = started
    n = len(srcs)
    after = list(after) if isinstance(after, (list, tuple)) else [after]

    def body(*refs):
        ins, land_refs = refs[:n], refs[n:2 * n]
        for send, arrived in _chip_copies(kind, ins, land_refs, split, refs[2 * n], refs[2 * n + 1]):
            send.wait_send()
            arrived.wait_recv()

    out = pl.pallas_call(
        body, name=name, in_specs=[_HBM] * (2 * n) + [_SEM, _SEM] + [_ANY] * len(after), out_specs=[_HBM] * (2 * n),
        out_shape=[pltpu.HBM(s.shape, s.dtype) for s in srcs] + [pltpu.HBM(l.shape, l.dtype) for l in lands],
        input_output_aliases={i: i for i in range(2 * n)},
        compiler_params=pltpu.CompilerParams(has_side_effects=_DATAFLOW),
    )(*srcs, *lands, send_sems, recv_sems, *after)
    return out[:n], out[n:]


def _pass_to_sibling(gathered, name, token=None):
    n = len(gathered)
    tr = [_chunk_rows(g.shape[1] // 2, g.shape[2], g.dtype.itemsize) for g in gathered]
    extra = [] if token is None else [token]

    def body(*refs):
        refs = refs[:n] + refs[n + len(extra):]
        outs = refs[n:2 * n]
        recv_sems, load_sems, send_sems = refs[2 * n:2 * n + 3]
        bufs = refs[2 * n + 3:]
        x, y, c = _mesh_pos()
        sibling = (x, y, 1 - c)
        chips = [(1 - x, y), (x, 1 - y), (1 - x, 1 - y)]
        for i in range(n):
            hr = outs[i].shape[1] // 2
            for j, chip in enumerate(chips):
                def rows_at(k, i=i, pj=2 * chip[0] + chip[1], hr=hr):
                    return outs[i].at[pj, pl.ds(c * hr + k * tr[i], tr[i])]

                _stream_chunks(hr // tr[i], rows_at, rows_at, bufs[i], load_sems, send_sems, recv_sems.at[3 * i + j],
                               sibling)
        for i in range(n):
            hr = outs[i].shape[1] // 2
            for j, chip in enumerate(chips):
                blk = outs[i].at[2 * chip[0] + chip[1], pl.ds((1 - c) * hr, hr)]
                pltpu.make_async_remote_copy(src_ref=blk, dst_ref=blk, send_sem=send_sems.at[0],
                                             recv_sem=recv_sems.at[3 * i + j], device_id=sibling,
                                             device_id_type=MESH).wait_recv()

    return pl.pallas_call(
        body, name=name, in_specs=[_ANY] * (n + len(extra)), out_specs=[_ANY] * n,
        out_shape=[jax.ShapeDtypeStruct(g.shape, g.dtype) for g in gathered],
        input_output_aliases={i: i for i in range(n)},
        scratch_shapes=[pltpu.SemaphoreType.DMA((3 * n,)), pltpu.SemaphoreType.DMA((STREAM_SLOTS,)),
                        pltpu.SemaphoreType.DMA((STREAM_SLOTS,))]
        + [pltpu.VMEM((STREAM_SLOTS, tr[i], g.shape[2]), g.dtype) for i, g in enumerate(gathered)],
        compiler_params=_cparams(),
    )(*gathered, *extra)


def _stream_to_sibling(srcs, halved, name):
    n = len(srcs)
    geo = []
    for s in srcs:
        rows = s.shape[1] // 2 if halved else s.shape[1]
        geo.append((s.shape[0], rows, s.shape[2], _chunk_rows(rows, s.shape[2], s.dtype.itemsize)))

    def body(*refs):
        ins, outs = refs[:n], refs[n:2 * n]
        recv_sems, load_sems, send_sems = refs[2 * n:2 * n + 3]
        bufs = refs[2 * n + 3:]
        x, y, c = _mesh_pos()
        sibling = (x, y, 1 - c)
        for i in range(n):
            slabs, rows, _, tr = geo[i]
            per_slab = rows // tr
            off = (1 - c) * rows if halved else 0

            def src_at(k, i=i, per_slab=per_slab, tr=tr, off=off):
                return ins[i].at[k // per_slab, pl.ds(off + (k % per_slab) * tr, tr)]

            def dst_at(k, i=i, per_slab=per_slab, tr=tr):
                return outs[i].at[k // per_slab, pl.ds((k % per_slab) * tr, tr)]

            _stream_chunks(slabs * per_slab, src_at, dst_at, bufs[i], load_sems, send_sems, recv_sems.at[i], sibling)
        for i in range(n):
            pltpu.make_async_remote_copy(src_ref=outs[i], dst_ref=outs[i], send_sem=send_sems.at[0],
                                         recv_sem=recv_sems.at[i], device_id=sibling, device_id_type=MESH).wait_recv()

    return pl.pallas_call(
        body, name=name, in_specs=[_ANY] * n, out_specs=[_ANY] * n,
        out_shape=[jax.ShapeDtypeStruct((g[0], g[1], g[2]), s.dtype) for g, s in zip(geo, srcs)],
        scratch_shapes=[pltpu.SemaphoreType.DMA((n,)), pltpu.SemaphoreType.DMA((STREAM_SLOTS,)),
                        pltpu.SemaphoreType.DMA((STREAM_SLOTS,))]
        + [pltpu.VMEM((STREAM_SLOTS, g[3], g[2]), s.dtype) for g, s in zip(geo, srcs)],
        compiler_params=_cparams(),
    )(*srcs)


def _reduce_scatter_begin(grads, tag, core):
    landed = _stream_to_sibling(grads, True, f"rs_swap_{tag}")
    parts = [_sum_keep_and_landed(g, l, core, f"rs_add2_{tag}_{i}") for i, (g, l) in enumerate(zip(grads, landed))]
    return _chip_exchange_start('scatter', parts, [(3,) + p.shape[1:] for p in parts], None, f"rs_scatter_start_{tag}",
                                core)


def _reduce_scatter_finish(started, tag, chip, after):
    parts, landed = _chip_exchange_wait('scatter', started, None, f"rs_scatter_wait_{tag}", after)
    mine = [_sum_chip_parts(p, l, chip, f"rs_add4_{tag}_{i}") for i, (p, l) in enumerate(zip(parts, landed))]
    theirs = _stream_to_sibling([m[None] for m in mine], False, f"rs_join_{tag}")
    return [(m, t[0]) for m, t in zip(mine, theirs)]


def _pad_cols(a, width=128):
    return jnp.pad(a, ((0, 0), (0, width - a.shape[1])))


def _mixer_forward(kind, hn, w, tables):
    if kind == 0:
        proj = _matmul(hn, w['ret_w_in'], name="ret_proj", b_sharded=True)
        og, states = _ret_fwd(proj, w['ret_gn_gain'], tables)
        return og, (proj, states)
    if kind == 1:
        proj = _matmul(hn, w['gdn_w_main'], name="gdn_proj")
        ba = _matmul(hn, w['gdn_w_small'], name="gdn_proj_ba")
        og, states = _gdn_fwd(proj, ba, w['gdn_conv_w'], w['gdn_a_log'], w['gdn_dt_bias'], w['gdn_norm_gain'])
        return og, (proj, ba, states)
    if kind == 2:
        proj = _matmul(hn, w['gla_w_main'], name="gla_proj")
        glow = _matmul(hn, w['gla_w_small'], name="gla_proj_gate")
        og, states = _gla_fwd(proj, glow, w['gla_w_gate_up'], w['gla_gate_bias'], w['gla_norm_gain'])
        return og, (proj, glow, states)
    proj = _matmul(hn, w['lru_w_in'], name="lru_proj", b_sharded=True)
    og = _lru_fwd(proj, w['lru_conv_w'], w['lru_conv_b'], w['lru_lambda'], w['lru_w_rgate'], w['lru_b_rgate'],
                  w['lru_w_igate'], w['lru_b_igate'])
    return og, (proj,)


def _mixer_backward(kind, hn, w, tables, saved, d_og, grads):
    d = hn.shape[1]
    if kind == 0:
        proj, states = saved
        d_proj, grads['ret_gn_gain'] = _ret_bwd(proj, w['ret_gn_gain'], tables, states, d_og)
        grads['ret_w_in'] = _matmul(hn, d_proj, name="ret_dw_in", ta=True, out_dtype=BF, o_sharded=True)
        return _matmul(d_proj, w['ret_w_in'], name="ret_dhn", tb=True, b_sharded=True)
    if kind == 1:
        proj, ba, states = saved
        d_proj, d_ba, grads['gdn_conv_w'], grads['gdn_a_log'], grads['gdn_dt_bias'], grads['gdn_norm_gain'] = _gdn_bwd(
            proj, ba, w['gdn_conv_w'], w['gdn_a_log'], w['gdn_dt_bias'], w['gdn_norm_gain'], states, d_og)
        d_ba = d_ba.astype(BF)
        dw_main = _matmul(hn, d_proj, name="gdn_dw_main", ta=True, out_dtype=BF)
        dw_small = _matmul(hn, d_ba, name="gdn_dw_small", ta=True, out_dtype=BF)
        dw = jnp.concatenate([dw_main, dw_small[:, :2 * GDN_HEADS]], axis=1)
        grads['gdn_w_in'] = dw.reshape(d, N_CHIPS, dw.shape[1] // N_CHIPS).transpose(1, 0, 2)
        d_hn = _matmul(d_proj, w['gdn_w_main'], name="gdn_dhn_main", tb=True)
        return _matmul(d_ba, w['gdn_w_small'], name="gdn_dhn_small", tb=True, epilogue='add', extra=d_hn)
    if kind == 2:
        proj, glow, states = saved
        d_proj, d_glow4, d_wgu, grads['gla_gate_bias'], grads['gla_norm_gain'] = _gla_bwd(
            proj, glow, w['gla_w_gate_up'], w['gla_gate_bias'], w['gla_norm_gain'], states, d_og)
        grads['gla_w_gate_up'] = d_wgu[:GLA_RANK]
        dw_main = _matmul(hn, d_proj, name="gla_dw_main", ta=True, out_dtype=BF)
        dw_small4 = _matmul(hn, d_glow4, name="gla_dw_small", ta=True, out_dtype=F32)
        dw_small = dw_small4.reshape(d, GLA_HEADS, 128)[:, :, :GLA_RANK].sum(axis=1).astype(BF)
        dw = jnp.concatenate([dw_main, dw_small], axis=1)
        grads['gla_w_in'] = dw.reshape(d, N_CHIPS, dw.shape[1] // N_CHIPS).transpose(1, 0, 2)
        d_hn = _matmul(d_proj, w['gla_w_main'], name="gla_dhn_main", tb=True)
        w_small4 = jnp.tile(w['gla_w_small'], (1, GLA_HEADS))
        return _matmul(d_glow4, w_small4, name="gla_dhn_small", tb=True, epilogue='add', extra=d_hn)
    (proj,) = saved
    (d_proj, grads['lru_conv_w'], grads['lru_conv_b'], grads['lru_lambda'], grads['lru_w_rgate'], grads['lru_b_rgate'],
     grads['lru_w_igate'], grads['lru_b_igate']) = _lru_bwd(
        proj, w['lru_conv_w'], w['lru_conv_b'], w['lru_lambda'], w['lru_w_rgate'], w['lru_b_rgate'], w['lru_w_igate'],
        w['lru_b_igate'], d_og)
    grads['lru_w_in'] = _matmul(hn, d_proj, name="lru_dw_in", ta=True, out_dtype=BF, o_sharded=True)
    return _matmul(d_proj, w['lru_w_in'], name="lru_dhn", tb=True, b_sharded=True)


_W_OUT = ('ret_w_out', 'gdn_w_out', 'gla_w_out', 'lru_w_out')
_W_IN = ('ret_w_in', 'gdn_w_in', 'gla_w_in', 'lru_w_in')


def _layer_forward(layer, x, w, tables, token=None):
    hn = _rmsnorm_fwd(x, w['norm1'][layer], f"norm1_fwd_{layer}", token)
    og, mixer_saved = _mixer_forward(layer, hn, w, tables)
    x1 = _matmul(og, w[_W_OUT[layer]], name=f"mixer_out_{layer}", epilogue='add', extra=x)
    hn2 = _rmsnorm_fwd(x1, w['norm2'][layer], f"norm2_fwd_{layer}")
    act = _matmul(hn2, w['mlp_w_up'][layer], name="mlp_up", b_sharded=True, epilogue='relu2', out_dtype=BF)
    x2 = _matmul(act, w['mlp_w_down'][layer], name="mlp_down", epilogue='add', extra=x1)
    return x2, (x, hn, mixer_saved, og, x1, hn2, act)


def _layer_backward(layer, dx2, dx2_bf, w, tables, saved, token=None):
    x, hn, mixer_saved, og, x1, hn2, act = saved
    d = x.shape[1]
    grads = {}
    d_up = _matmul(dx2_bf, w['mlp_w_down'][layer], name="mlp_d_up", tb=True, epilogue='dact', extra=act,
                   out_dtype=BF, token=token)
    dw_down = _matmul(act, dx2_bf, name="mlp_dw_down", ta=True, out_dtype=BF)
    grads['mlp_w_down'] = dw_down.reshape(N_CHIPS, dw_down.shape[0] // N_CHIPS, d)
    grads['mlp_w_up'] = _matmul(hn2, d_up, name="mlp_dw_up", ta=True, out_dtype=BF, o_sharded=True)
    d_hn2 = _matmul(d_up, w['mlp_w_up'][layer], name="mlp_d_hn", tb=True, b_sharded=True)
    dx1, dx1_bf, grads['norm2'] = _rmsnorm_bwd(x1, w['norm2'][layer], d_hn2, dx2, f"norm2_bwd_{layer}")
    w_out = w[_W_OUT[layer]]
    d_og = _matmul(dx1_bf, w_out, name=f"mixer_d_og_{layer}", tb=True, out_dtype=BF)
    dw_out = _matmul(og, dx1_bf, name=f"mixer_dw_out_{layer}", ta=True, out_dtype=BF)
    grads[_W_OUT[layer]] = dw_out.reshape(N_CHIPS, dw_out.shape[0] // N_CHIPS, d)
    d_hn = _mixer_backward(layer, hn, w, tables, mixer_saved, d_og, grads)
    dx, dx_bf, grads['norm1'] = _rmsnorm_bwd(x, w['norm1'][layer], d_hn, dx1, f"norm1_bwd_{layer}")
    return dx, dx_bf, grads


PACK_ROWS = 256


def _pack(arrays):
    flat = []
    for a in arrays:
        v = a.astype(F32).reshape(-1)
        v = jnp.pad(v, (0, (-v.shape[0]) % 128))
        flat.append(v.reshape(-1, 128))
    buf = jnp.concatenate(flat, axis=0)
    return jnp.pad(buf, ((0, (-buf.shape[0]) % PACK_ROWS), (0, 0)))


def _unpack(buf, shapes):
    lead = buf.shape[:-2]
    out, off = [], 0
    for shp in shapes:
        n = math.prod(shp)
        rows = -(-n // 128)
        piece = buf[..., off:off + rows, :].reshape(lead + (rows * 128,))[..., :n]
        out.append(piece.reshape(lead + tuple(shp)))
        off += rows
    return out


_WEIGHTS = ('norm1', 'norm2', 'final_norm', 'ret_w_in', 'ret_gn_gain', 'ret_w_out', 'gdn_w_in', 'gdn_conv_w',
            'gdn_a_log', 'gdn_dt_bias', 'gdn_norm_gain', 'gdn_w_out', 'gla_w_in', 'gla_w_gate_up', 'gla_gate_bias',
            'gla_norm_gain', 'gla_w_out', 'lru_w_in', 'lru_conv_w', 'lru_conv_b', 'lru_w_rgate', 'lru_b_rgate',
            'lru_w_igate', 'lru_b_igate', 'lru_lambda', 'lru_w_out', 'mlp_w_up', 'mlp_w_down')
_FWD_PARAMS = ('x',) + _WEIGHTS
_BIG = ('ret_w_in', 'ret_w_out', 'gdn_w_in', 'gdn_w_out', 'gla_w_in', 'gla_w_out', 'lru_w_in', 'lru_w_out',
        'mlp_w_up', 'mlp_w_down')
_SMALL = tuple(n for n in _WEIGHTS if n not in _BIG)
_SMALL_SHARDED = ('ret_gn_gain', 'gdn_conv_w', 'gla_w_gate_up', 'gla_gate_bias', 'gla_norm_gain', 'lru_conv_w',
                  'lru_conv_b', 'lru_lambda')


def kernel(*args):
    names = _FWD_PARAMS + ('loss_target',) + tuple('m_' + n for n in _WEIGHTS) + tuple('v_' + n for n in _WEIGHTS)
    assert len(args) == len(names)
    a = dict(zip(names, args))
    x = a['x'][0]
    target = a['loss_target'][0]
    s, d = x.shape
    chip = 2 * lax.axis_index("x") + lax.axis_index("y")

    small_local = [a[n][0] if a[n].ndim == 3 else a[n] for n in _SMALL_SHARDED]
    small_pack = _pack(small_local)
    core_arr = lax.axis_index("c").astype(jnp.int32).reshape(1)
    chip_arr = chip.astype(jnp.int32).reshape(1)

    def whole_cols(g):
        return g.transpose(1, 0, 2).reshape(g.shape[1], N_CHIPS * g.shape[2])

    def whole_rows(g):
        return g.reshape(N_CHIPS * g.shape[1], g.shape[2])

    w = {'mlp_w_up': [None] * 4, 'mlp_w_down': [None] * 4}
    for n in _SMALL:
        if n not in _SMALL_SHARDED:
            w[n] = a[n][0] if n.startswith('lru_') else a[n]
    w['gdn_a_log'], w['gdn_dt_bias'] = _pad_cols(w['gdn_a_log']), _pad_cols(w['gdn_dt_bias'])
    w['lru_b_rgate'] = w['lru_b_rgate'].reshape(LRU_BLOCKS, 1, -1)
    w['lru_b_igate'] = w['lru_b_igate'].reshape(LRU_BLOCKS, 1, -1)

    early_ops = {}

    def weight_ops(layer, zero=0.0):
        return [(t + zero).astype(BF) for t in (a[_W_IN[layer]][0], a[_W_OUT[layer]][0], a['mlp_w_up'][layer],
                                                 a['mlp_w_down'][layer])]

    def gather_start(layer, after):
        ops = early_ops.pop(layer, None) or weight_ops(layer)
        ops = ops + ([small_pack] if layer == 0 else [])
        split = [True] * 4 + ([False] if layer == 0 else [])
        return _chip_exchange_start('gather', ops, [(N_CHIPS,) + o.shape for o in ops], split,
                                    f"gather_start_{layer}", after), split

    def gather_finish(layer, started, after):
        started, split = started
        own, lands = _chip_exchange_wait('gather', started, split, f"gather_wait_{layer}", after)
        nxt = gather_start(layer + 1, own[0]) if layer < 3 else None
        lands = list(_pass_to_sibling(lands[:4], f"gather_pass_{layer}", None if nxt is None else nxt[0][4])) \
            + list(lands[4:])
        g_in, g_out, g_up, g_down = [lax.dynamic_update_slice_in_dim(l, o[None], chip, axis=0)
                                     for l, o in zip(lands[:4], own[:4])]
        w['mlp_w_up'][layer], w['mlp_w_down'][layer], w[_W_OUT[layer]] = g_up, whole_rows(g_down), whole_rows(g_out)
        if layer == 0:
            small_all = lax.dynamic_update_slice_in_dim(lands[4], own[4][None], chip, axis=0)
            for n, piece in zip(_SMALL_SHARDED, _unpack(small_all, [p.shape for p in small_local])):
                w[n] = whole_cols(piece)
            w['gla_w_gate_up'] = jnp.pad(w['gla_w_gate_up'], ((0, 128 - GLA_RANK), (0, 0)))
        if layer in (0, 3):
            w[_W_IN[layer]] = g_in
        else:
            name, tail = ('gdn', 2 * GDN_HEADS) if layer == 1 else ('gla', GLA_RANK)
            full = whole_cols(g_in)
            w[name + '_w_main'] = full[:, :full.shape[1] - tail]
            w[name + '_w_small'] = _pad_cols(full[:, full.shape[1] - tail:])
        return nxt

    tables = _ret_tables(s, d // RET_HEADS)
    saved = []
    h = x
    first = gather_start(0, x)
    early_ops[1] = weight_ops(1, first[0][4][0, 0])
    nxt = gather_finish(0, first, [x] + early_ops[1])
    for layer in range(4):
        h, sv = _layer_forward(layer, h, w, tables, None if nxt is None else nxt[0][4])
        saved.append(sv)
        if nxt is not None:
            nxt = gather_finish(layer + 1, nxt, h)
    loss_part, dh, dh_bf, g_final = _loss_head(h, w['final_norm'], target)
    loss = lax.psum(loss_part, ("x", "y", "c"))

    grad, delta, new_m, new_v = {}, {}, {}, {}
    small_grads = {'final_norm': g_final}
    norm_grads = {'norm1': [None] * 4, 'norm2': [None] * 4}
    mlp_upd = {'mlp_w_up': None, 'mlp_w_down': None}

    def update_layer(layer, started, after):
        red = _reduce_scatter_finish(started, str(layer), chip_arr, after)
        for n, (mine, theirs) in ((_W_IN[layer], red[0]), (_W_OUT[layer], red[1])):
            grad[n], delta[n], new_m[n], new_v[n] = _adamw_halves(a[n], a['m_' + n], a['v_' + n], 0, mine, theirs,
                                                                  core_arr, f"adamw_{n}")
        for n, (mine, theirs) in (('mlp_w_up', red[2]), ('mlp_w_down', red[3])):
            mlp_upd[n] = _adamw_halves(a[n], a['m_' + n], a['v_' + n], layer, mine, theirs, core_arr,
                                       f"adamw_{n}_{layer}", carried=mlp_upd[n])

    scatters = {}
    token = None
    for layer in reversed(range(4)):
        dh, dh_bf, g = _layer_backward(layer, dh, dh_bf, w, tables, saved[layer], token)
        saved[layer] = None
        scatters[layer] = _reduce_scatter_begin([g[_W_IN[layer]], g[_W_OUT[layer]], g['mlp_w_up'], g['mlp_w_down']],
                                                str(layer), core_arr)
        token = scatters[layer][4]
        norm_grads['norm1'][layer], norm_grads['norm2'][layer] = g['norm1'], g['norm2']
        for n in _SMALL:
            if n in g:
                small_grads[n] = g[n]
    small_grads['norm1'] = jnp.stack(norm_grads['norm1'])
    small_grads['norm2'] = jnp.stack(norm_grads['norm2'])

    full_shapes = [small_grads[n].shape for n in _SMALL]
    small_part = _pack([small_grads[n] for n in _SMALL])
    sibling_part = _stream_to_sibling([small_part[None]], False, "small_swap")[0][0]
    chip_part = _sum_slabs(jnp.stack([small_part, sibling_part]), F32, "sum_small_cores")
    small_started = _chip_exchange_start('gather', [chip_part], [(N_CHIPS,) + chip_part.shape], [False],
                                         "small_gather_start", core_arr)
    after = [token]
    for layer in (3, 2, 1):
        update_layer(layer, scatters[layer], after)
        after = [new_v[_W_IN[layer]], new_v[_W_OUT[layer]], mlp_upd['mlp_w_up'][3], mlp_upd['mlp_w_down'][3]]
    own, lands = _chip_exchange_wait('gather', small_started, [False], "small_gather_wait", after)
    total = _sum_slabs(lax.dynamic_update_slice_in_dim(lands[0], own[0][None], chip, axis=0), F32, "sum_small_chips")
    local_g = {}
    for n, full in zip(_SMALL, _unpack(total, full_shapes)):
        shp = a[n].shape
        if n in _SMALL_SHARDED:
            full = full.reshape(full.shape[0], -1)
            cq = shp[-1]
            full = lax.dynamic_slice_in_dim(full, chip * cq, cq, axis=1)
        elif n in ('gdn_a_log', 'gdn_dt_bias'):
            full = full[:, :shp[-1]]
        local_g[n] = full.reshape(shp)
    shapes = [a[n].shape for n in _SMALL]
    packed = [_pack([src[n] for n in _SMALL]) for src in
              (a, local_g, {n: a['m_' + n] for n in _SMALL}, {n: a['v_' + n] for n in _SMALL})]
    upd = _adamw(*packed, "adamw_small")
    for n, gr, dl, nm, nv in zip(_SMALL, _unpack(packed[1], shapes), *[_unpack(u, shapes) for u in upd]):
        grad[n], delta[n], new_m[n], new_v[n] = gr, dl, nm, nv
    update_layer(0, scatters[0], upd[0])
    for n in ('mlp_w_up', 'mlp_w_down'):
        grad[n], delta[n], new_m[n], new_v[n] = mlp_upd[n]

    out = [loss, dh.reshape(a['x'].shape)]
    for group in (grad, delta, new_m, new_v):
        out += [group[n].reshape(a[n].shape) for n in _WEIGHTS]
    return tuple(out)
```
